```python
import jax, jax.numpy as jnp
from jax import lax
import numpy as np

D_MODEL = 2048
BATCH = 1
SEQ = 8192
DEPTH = 1

CTX_LEN = 256
GRID_W = 64
RET_HEADS = 8
RET_DK = 64
RET_DV = 128
RET_CHUNK = 128
ATT_HEADS = 16
ATT_KV_HEADS = 4
ATT_DH = 64
ATT_GROUP = ATT_HEADS // ATT_KV_HEADS
WINDOW = 128
ATT_BLOCK = 128
D_FF = -(-(8 * D_MODEL) // (3 * 256)) * 256
ROPE_BASE = 10000.0
NORM_EPS = 1e-6
PROJ_SIZES = (RET_HEADS * RET_DK, RET_HEADS * RET_DK, RET_HEADS * RET_DV, RET_HEADS * RET_DV,
              ATT_HEADS * ATT_DH, ATT_KV_HEADS * ATT_DH, ATT_KV_HEADS * ATT_DH)
D_PROJ = sum(PROJ_SIZES)
D_MIX_OUT = RET_HEADS * RET_DV + ATT_HEADS * ATT_DH

kernel_name = "hybrid_retention_window_gqa_dit_layer"


def _rmsnorm(x, g):
    xf = x.astype(jnp.float32)
    y = xf * lax.rsqrt(jnp.mean(xf * xf, axis=-1, keepdims=True) + NORM_EPS)
    return (y * g.astype(jnp.float32)).astype(x.dtype)


def _modulate(x, g, shift, scale):
    return _rmsnorm(x, g) * (1.0 + scale) + shift


def _split_proj(p):
    idx = [int(v) for v in np.cumsum(PROJ_SIZES)[:-1]]
    return jnp.split(p, idx, axis=-1)


def _heads(p, h, d):
    return p.reshape(p.shape[0], p.shape[1], h, d)


def _rope(x, pos):
    half = x.shape[-1] // 2
    inv = ROPE_BASE ** (-jnp.arange(half, dtype=jnp.float32) / half)
    ang = pos.astype(jnp.float32)[:, None] * inv[None, :]
    cos = jnp.cos(ang)[:, None, :]
    sin = jnp.sin(ang)[:, None, :]
    x1 = x[..., :half].astype(jnp.float32)
    x2 = x[..., half:].astype(jnp.float32)
    return jnp.concatenate([x1 * cos - x2 * sin, x1 * sin + x2 * cos], axis=-1).astype(x.dtype)


def _axial_rope(x, rows, cols):
    half = x.shape[-1] // 2
    return jnp.concatenate([_rope(x[..., :half], rows), _rope(x[..., half:], cols)], axis=-1)


def _ret_state(k, v, log_g):
    L = k.shape[1]
    w = jnp.exp(log_g[None, :] * (L - 1 - jnp.arange(L, dtype=jnp.float32))[:, None])
    return jnp.einsum('blhd,blhv->bhdv', k * w[None, :, :, None], v).astype(jnp.float32)


def _retention_chunkwise(q, k, v, log_g, s0, include_diag):
    b, L, h, dk = q.shape
    dv = v.shape[-1]
    C = RET_CHUNK
    n = L // C
    qc = q.reshape(b, n, C, h, dk)
    kc = k.reshape(b, n, C, h, dk)
    vc = v.reshape(b, n, C, h, dv)
    pos = jnp.arange(C, dtype=jnp.float32)
    rel = pos[:, None] - pos[None, :]
    mask = (rel >= 0) if include_diag else (rel > 0)
    decay_in = jnp.where(mask[None], jnp.exp(log_g[:, None, None] * jnp.maximum(rel, 0.0)[None]), 0.0)
    scores = jnp.einsum('bnihd,bnjhd->bnhij', qc, kc) * decay_in[None, None]
    o_in = jnp.einsum('bnhij,bnjhv->bnihv', scores, vc)
    k_w = jnp.exp(log_g[None, :] * (C - 1 - pos)[:, None])
    kv_chunk = jnp.einsum('bnjhd,bnjhv->nbhdv', kc * k_w[:, :, None], vc).astype(jnp.float32)
    g_chunk = jnp.exp(log_g * C)[None, :, None, None]

    def step(s, kv):
        return g_chunk * s + kv, s

    _, s_prev = lax.scan(step, s0.astype(jnp.float32), kv_chunk)
    q_w = jnp.exp(log_g[None, :] * (pos + 1.0)[:, None])
    o_x = jnp.einsum('bnihd,nbhdv->bnihv', qc * q_w[:, :, None], s_prev)
    return (o_in + o_x).reshape(b, L, h, dv)


def _bidir_retention(q, k, v, gate, lg_f, lg_b, s_f, s_b):
    o_f = _retention_chunkwise(q, k, v, lg_f, s_f, True)
    o_b = _retention_chunkwise(q[:, ::-1], k[:, ::-1], v[:, ::-1], lg_b, s_b, False)[:, ::-1]
    o = (o_f + o_b).astype(jnp.float32)
    o = o * lax.rsqrt(jnp.mean(o * o, axis=-1, keepdims=True) + NORM_EPS)
    o = o.reshape(o.shape[0], o.shape[1], RET_HEADS * RET_DV)
    return (o * jax.nn.silu(gate.astype(jnp.float32))).astype(gate.dtype)


def _window_attention(q, k, v, k_ctx, v_ctx, sink):
    b, L = q.shape[0], q.shape[1]
    Lc = k_ctx.shape[1]
    Bk = ATT_BLOCK
    n = L // Bk
    qb = q.reshape(b, n, Bk, ATT_KV_HEADS, ATT_GROUP, ATT_DH)
    pad = ((0, 0), (Bk, Bk), (0, 0), (0, 0))
    kp = jnp.pad(k, pad)
    vp = jnp.pad(v, pad)
    kb = jnp.concatenate([kp[:, j * Bk:j * Bk + L].reshape(b, n, Bk, ATT_KV_HEADS, ATT_DH) for j in range(3)], axis=2)
    vb = jnp.concatenate([vp[:, j * Bk:j * Bk + L].reshape(b, n, Bk, ATT_KV_HEADS, ATT_DH) for j in range(3)], axis=2)
    scale = ATT_DH ** -0.5
    s_loc = jnp.einsum('bnqkgd,bnskd->bnkgqs', qb, kb).astype(jnp.float32) * scale
    s_ctx = jnp.einsum('bnqkgd,bckd->bnkgqc', qb, k_ctx).astype(jnp.float32) * scale
    blk = jnp.arange(n)[:, None]
    qpos = (blk * Bk + jnp.arange(Bk)[None, :])[:, :, None]
    kpos = ((blk - 1) * Bk + jnp.arange(3 * Bk)[None, :])[:, None, :]
    valid = (jnp.abs(qpos - kpos) <= WINDOW) & (kpos >= 0) & (kpos < L)
    s_loc = jnp.where(valid[None, :, None, None], s_loc, -jnp.inf)
    sink_b = jnp.broadcast_to(sink.astype(jnp.float32).reshape(ATT_KV_HEADS, ATT_GROUP)[None, None, :, :, None, None],
                              s_loc.shape[:-1] + (1,))
    p = jax.nn.softmax(jnp.concatenate([s_loc, s_ctx, sink_b], axis=-1), axis=-1)
    p_loc = p[..., :3 * Bk].astype(v.dtype)
    p_ctx = p[..., 3 * Bk:3 * Bk + Lc].astype(v.dtype)
    o = jnp.einsum('bnkgqs,bnskd->bnqkgd', p_loc, vb) + jnp.einsum('bnkgqc,bckd->bnqkgd', p_ctx, v_ctx)
    return o.reshape(b, L, ATT_HEADS * ATT_DH)


def _context_attention(q, k, v, sink):
    b, Lc = q.shape[0], q.shape[1]
    qg = q.reshape(b, Lc, ATT_KV_HEADS, ATT_GROUP, ATT_DH)
    s = jnp.einsum('bqkgd,bckd->bkgqc', qg, k).astype(jnp.float32) * (ATT_DH ** -0.5)
    sink_b = jnp.broadcast_to(sink.astype(jnp.float32).reshape(ATT_KV_HEADS, ATT_GROUP)[None, :, :, None, None],
                              s.shape[:-1] + (1,))
    p = jax.nn.softmax(jnp.concatenate([s, sink_b], axis=-1), axis=-1)[..., :Lc].astype(v.dtype)
    o = jnp.einsum('bkgqc,bckd->bqkgd', p, v)
    return o.reshape(b, Lc, ATT_HEADS * ATT_DH)


def _swiglu(h, w_gate, w_up, w_down):
    return (jax.nn.silu(h @ w_gate) * (h @ w_up)) @ w_down


def setup_inputs(seed: int = 0) -> dict:
    key = jax.random.key(seed)
    ks = jax.random.split(key, 18)

    def nrm(k, shape, scale):
        return jax.random.normal(k, shape, jnp.float32) * scale

    base_decay = np.log(-np.log1p(-2.0 ** (-5.0 - np.arange(RET_HEADS)))).astype(np.float32)
    return {
        "x": nrm(ks[0], (BATCH, SEQ, D_MODEL), 1.0),
        "c": nrm(ks[1], (BATCH, D_MODEL), 1.0),
        "ctx": nrm(ks[2], (BATCH, CTX_LEN, D_MODEL), 1.0),
        "c_ctx": nrm(ks[3], (D_MODEL,), 1.0),
        "w_mod": nrm(ks[4], (DEPTH, D_MODEL, 6 * D_MODEL), 0.5 * D_MODEL ** -0.5),
        "b_mod": nrm(ks[5], (DEPTH, 6 * D_MODEL), 0.02),
        "norm_mix": 1.0 + nrm(ks[6], (DEPTH, D_MODEL), 0.02),
        "norm_ffn": 1.0 + nrm(ks[7], (DEPTH, D_MODEL), 0.02),
        "w_in": nrm(ks[8], (DEPTH, D_MODEL, D_PROJ), D_MODEL ** -0.5),
        "ret_decay": jnp.asarray(base_decay)[None, None, :] + nrm(ks[9], (DEPTH, 2, RET_HEADS), 0.05),
        "attn_sink": nrm(ks[10], (DEPTH, ATT_HEADS), 0.5),
        "w_out": nrm(ks[11], (DEPTH, D_MIX_OUT, D_MODEL), D_MIX_OUT ** -0.5),
        "w_gate": nrm(ks[12], (DEPTH, D_MODEL, D_FF), D_MODEL ** -0.5),
        "w_up": nrm(ks[13], (DEPTH, D_MODEL, D_FF), D_MODEL ** -0.5),
        "w_down": nrm(ks[14], (DEPTH, D_FF, D_MODEL), D_FF ** -0.5),
        "norm_final": 1.0 + nrm(ks[15], (D_MODEL,), 0.02),
    }


def reference(x, c, ctx, c_ctx, w_mod, b_mod, norm_mix, norm_ffn, w_in, ret_decay, attn_sink,
              w_out, w_gate, w_up, w_down, norm_final):
    L = x.shape[1]
    ROWS = L // GRID_W
    t = jnp.arange(L)
    rows = jnp.repeat(jnp.arange(ROWS), GRID_W)
    cols = jnp.tile(jnp.arange(GRID_W), ROWS)
    k_scale = RET_DK ** -0.5
    xc = ctx
    for l in range(DEPTH):
        last = l == DEPTH - 1
        mod = (jax.nn.silu(c) @ w_mod[l] + b_mod[l])[:, None, :]
        mod_c = (jax.nn.silu(c_ctx) @ w_mod[l] + b_mod[l])[None, None, :]
        sh_m, sc_m, gt_m, sh_f, sc_f, gt_f = jnp.split(mod, 6, axis=-1)
        sh_mc, sc_mc, gt_mc, sh_fc, sc_fc, gt_fc = jnp.split(mod_c, 6, axis=-1)
        lg_f = -jnp.exp(ret_decay[l, 0].astype(jnp.float32))
        lg_b = -jnp.exp(ret_decay[l, 1].astype(jnp.float32))

        hx = _modulate(x, norm_mix[l], sh_m, sc_m)
        hc = _modulate(xc, norm_mix[l], sh_mc, sc_mc)
        rq, rk, rv, rg, aq, ak, av = _split_proj(hx @ w_in[l])
        crq, crk, crv, crg, caq, cak, cav = _split_proj(hc @ w_in[l])

        crk = _heads(crk, RET_HEADS, RET_DK) * k_scale
        crv = _heads(crv, RET_HEADS, RET_DV)
        s_f = _ret_state(crk, crv, lg_f)
        s_b = _ret_state(crk[:, ::-1], crv[:, ::-1], lg_b)
        q_r = _rope(_heads(rq, RET_HEADS, RET_DK), t)
        k_r = _rope(_heads(rk, RET_HEADS, RET_DK), t) * k_scale
        v_r = _heads(rv, RET_HEADS, RET_DV)
        y_ret = _bidir_retention(q_r, k_r, v_r, rg, lg_f, lg_b, s_f, s_b)

        cak = _heads(cak, ATT_KV_HEADS, ATT_DH)
        cav = _heads(cav, ATT_KV_HEADS, ATT_DH)
        q_a = _axial_rope(_heads(aq, ATT_HEADS, ATT_DH), rows, cols)
        k_a = _axial_rope(_heads(ak, ATT_KV_HEADS, ATT_DH), rows, cols)
        v_a = _heads(av, ATT_KV_HEADS, ATT_DH)
        y_att = _window_attention(q_a, k_a, v_a, cak, cav, attn_sink[l])

        x = x + gt_m * (jnp.concatenate([y_ret, y_att], axis=-1) @ w_out[l])

        if not last:
            zero = jnp.zeros((xc.shape[0], RET_HEADS, RET_DK, RET_DV), jnp.float32)
            y_ret_c = _bidir_retention(_heads(crq, RET_HEADS, RET_DK), crk, crv, crg, lg_f, lg_b, zero, zero)
            y_att_c = _context_attention(_heads(caq, ATT_HEADS, ATT_DH), cak, cav, attn_sink[l])
            xc = xc + gt_mc * (jnp.concatenate([y_ret_c, y_att_c], axis=-1) @ w_out[l])
            xc = xc + gt_fc * _swiglu(_modulate(xc, norm_ffn[l], sh_fc, sc_fc), w_gate[l], w_up[l], w_down[l])

        x = x + gt_f * _swiglu(_modulate(x, norm_ffn[l], sh_f, sc_f), w_gate[l], w_up[l], w_down[l])
    return _rmsnorm(x, norm_final)
```

```python
import functools

import jax
import jax.numpy as jnp
from jax import lax
from jax.experimental import pallas as pl
from jax.experimental.pallas import tpu as pltpu

GRID_W = 64
RET_HEADS = 8
RET_DK = 64
RET_DV = 128
RET_CHUNK = 128
ATT_HEADS = 16
ATT_KV_HEADS = 4
ATT_DH = 64
ATT_GROUP = ATT_HEADS // ATT_KV_HEADS
WINDOW = 128
ATT_BLOCK = 128
ROPE_BASE = 10000.0
NORM_EPS = 1e-6
K_SCALE = RET_DK ** -0.5
ATT_SCALE = ATT_DH ** -0.5

LANES = 128
RET_PAIRS = RET_HEADS // 2
MASK_NEG = -1e30
VMEM_LIMIT = 56 * 1024 * 1024

BF16 = jnp.bfloat16
F32 = jnp.float32


def _params(*sem):
    return pltpu.CompilerParams(dimension_semantics=sem, vmem_limit_bytes=VMEM_LIMIT)


def _mod_kernel(cv_ref, w_ref, b_ref, o_ref):
    cv = cv_ref[...]
    s = cv / (1.0 + jnp.exp(-cv))
    o_ref[...] = jnp.dot(s.astype(BF16), w_ref[...].astype(BF16),
                         preferred_element_type=F32) + b_ref[...]


def _mod(cv, w, b):
    d, n = w.shape
    tn = 1024
    return pl.pallas_call(
        _mod_kernel,
        grid=(n // tn,),
        in_specs=[pl.BlockSpec((8, d), lambda j: (0, 0)),
                  pl.BlockSpec((d, tn), lambda j: (0, j)),
                  pl.BlockSpec((1, tn), lambda j: (0, j))],
        out_specs=pl.BlockSpec((8, tn), lambda j: (0, j)),
        out_shape=jax.ShapeDtypeStruct((8, n), F32),
        compiler_params=_params("parallel"),
        name="mod",
    )(cv, w, b)


def _rot_pairs(a, cos, sin_signed, half):
    lane = lax.broadcasted_iota(jnp.int32, a.shape, 1)
    first = (lane % (2 * half)) < half
    rot = jnp.where(first, pltpu.roll(a, LANES - half, 1), pltpu.roll(a, half, 1))
    return a * cos + rot * sin_signed


def _dup_halves(a):
    lane = lax.broadcasted_iota(jnp.int32, a.shape, 1)
    r = pltpu.roll(a, 64, 1)
    lo = lane < 64
    return jnp.where(lo, a, r), jnp.where(lo, r, a)


def _in_proj_kernel(x_ref, g_ref, sh_ref, sc_ref, w_ref, c1_ref, s1_ref, ca_ref, sa_ref,
                    o_ref, kd_ref, vd_ref, h_ref, acc_ref, *, rope):
    j = pl.program_id(1)

    @pl.when(j == 0)
    def _():
        xf = x_ref[...]
        y = xf * lax.rsqrt(jnp.mean(xf * xf, axis=-1, keepdims=True) + NORM_EPS)
        y = y * g_ref[...]
        h_ref[...] = (y * (1.0 + sc_ref[...]) + sh_ref[...]).astype(BF16)

    acc_ref[...] = jnp.dot(h_ref[...], w_ref[...], preferred_element_type=F32)

    def chunk(c):
        return acc_ref[:, c * LANES:(c + 1) * LANES]

    def put(c, a):
        o_ref[:, c * LANES:(c + 1) * LANES] = a.astype(BF16)

    def rope1(a):
        return _rot_pairs(a, c1_ref[...], s1_ref[...], 32) if rope else a

    def ropea(a):
        return _rot_pairs(a, ca_ref[...], sa_ref[...], 16) if rope else a

    @pl.when(j == 0)
    def _():
        for c in range(4):
            put(c, rope1(chunk(c)))

    @pl.when(j == 1)
    def _():
        for c in range(4):
            put(c, rope1(chunk(c)) * K_SCALE)

    @pl.when((j >= 2) & (j <= 5))
    def _():
        for c in range(4):
            put(c, chunk(c))

    @pl.when((j == 6) | (j == 7))
    def _():
        for c in range(4):
            put(c, ropea(chunk(c)))

    @pl.when(j == 8)
    def _():
        for c in range(2):
            k = ropea(chunk(c))
            put(c, k)
            k0, k1 = _dup_halves(k)
            kd_ref[:, (2 * c) * LANES:(2 * c + 1) * LANES] = k0.astype(BF16)
            kd_ref[:, (2 * c + 1) * LANES:(2 * c + 2) * LANES] = k1.astype(BF16)
            v = chunk(2 + c)
            put(2 + c, v)
            v0, v1 = _dup_halves(v)
            vd_ref[:, (2 * c) * LANES:(2 * c + 1) * LANES] = v0.astype(BF16)
            vd_ref[:, (2 * c + 1) * LANES:(2 * c + 2) * LANES] = v1.astype(BF16)


def _in_proj(x, g, sh, sc, w, tabs, *, rope, tm):
    m, d = x.shape
    n = w.shape[1]
    tn = 512
    assert n == 9 * tn and m % tm == 0
    c1, s1, ca, sa = tabs
    row = lambda i, j: (i, 0)
    vec = pl.BlockSpec((1, d), lambda i, j: (0, 0))
    tab = pl.BlockSpec((tm, LANES), row)
    return pl.pallas_call(
        functools.partial(_in_proj_kernel, rope=rope),
        grid=(m // tm, n // tn),
        in_specs=[pl.BlockSpec((tm, d), row), vec, vec, vec,
                  pl.BlockSpec((d, tn), lambda i, j: (0, j)),
                  tab, tab, tab, tab],
        out_specs=[pl.BlockSpec((tm, tn), lambda i, j: (i, j)),
                   pl.BlockSpec((tm, 512), row),
                   pl.BlockSpec((tm, 512), row)],
        out_shape=[jax.ShapeDtypeStruct((m, n), BF16),
                   jax.ShapeDtypeStruct((m, 512), BF16),
                   jax.ShapeDtypeStruct((m, 512), BF16)],
        scratch_shapes=[pltpu.VMEM((tm, d), BF16), pltpu.VMEM((tm, tn), F32)],
        compiler_params=_params("parallel", "arbitrary"),
        name="in_proj" if rope else "in_proj_ctx",
    )(x, g, sh, sc, w, c1, s1, ca, sa)


def _pair_lg(dec_ref, d, p, shape):
    lane = lax.broadcasted_iota(jnp.int32, shape, 1)
    first = (lane % LANES) < 64
    raw = jnp.where(first, jnp.full(shape, dec_ref[d, 2 * p], F32), jnp.full(shape, dec_ref[d, 2 * p + 1], F32))
    return -jnp.exp(raw)


def _head_block_mask(shape):
    r = lax.broadcasted_iota(jnp.int32, shape, 0)
    c = lax.broadcasted_iota(jnp.int32, shape, 1)
    return (r // 64) == (c // LANES)


def _kv_pair(k_pair, v_pair, w):
    kw = (k_pair.astype(F32) * w).astype(BF16)
    kv = lax.dot_general(kw, v_pair, (((0,), (0,)), ((), ())), preferred_element_type=F32)
    return jnp.where(_head_block_mask(kv.shape), kv, 0.0)


def _ret_state_kernel(dec_ref, kf_ref, vf_ref, kb_ref, vb_ref, ck_ref, cv_ref,
                      sf_ref, sb_ref, sfs, sbs):
    i = pl.program_id(0)
    C = RET_CHUNK
    lc = ck_ref.shape[0]

    @pl.when(i == 0)
    def _():
        pos = lax.broadcasted_iota(jnp.int32, (lc, LANES), 0).astype(F32)
        for p in range(RET_PAIRS):
            ks = slice(p * LANES, (p + 1) * LANES)
            vs = slice(p * 2 * RET_DV, (p + 1) * 2 * RET_DV)
            wf = jnp.exp(_pair_lg(dec_ref, 0, p, (lc, LANES)) * (lc - 1.0 - pos))
            wb = jnp.exp(_pair_lg(dec_ref, 1, p, (lc, LANES)) * pos)
            sfs[p] = _kv_pair(ck_ref[:, ks], cv_ref[:, vs], wf)
            sbs[p] = _kv_pair(ck_ref[:, ks], cv_ref[:, vs], wb)

    pos = lax.broadcasted_iota(jnp.int32, (C, LANES), 0).astype(F32)
    for p in range(RET_PAIRS):
        ks = slice(p * LANES, (p + 1) * LANES)
        vs = slice(p * 2 * RET_DV, (p + 1) * 2 * RET_DV)
        sf = sfs[p]
        sf_ref[0, p] = sf.astype(BF16)
        wf = jnp.exp(_pair_lg(dec_ref, 0, p, (C, LANES)) * (C - 1.0 - pos))
        rowh = lax.broadcasted_iota(jnp.int32, sf.shape, 0) < 64
        gf = jnp.exp(-jnp.exp(jnp.where(rowh, jnp.full(sf.shape, dec_ref[0, 2 * p], F32),
                                         jnp.full(sf.shape, dec_ref[0, 2 * p + 1], F32))) * float(C))
        sfs[p] = gf * sf + _kv_pair(kf_ref[:, ks], vf_ref[:, vs], wf)
        sb = sbs[p]
        sb_ref[0, p] = sb.astype(BF16)
        wb = jnp.exp(_pair_lg(dec_ref, 1, p, (C, LANES)) * pos)
        gb = jnp.exp(-jnp.exp(jnp.where(rowh, jnp.full(sb.shape, dec_ref[1, 2 * p], F32),
                                         jnp.full(sb.shape, dec_ref[1, 2 * p + 1], F32))) * float(C))
        sbs[p] = gb * sb + _kv_pair(kb_ref[:, ks], vb_ref[:, vs], wb)


def _ret_states(dec, proj, cproj):
    L = proj.shape[0]
    lc = cproj.shape[0]
    n = L // RET_CHUNK
    C = RET_CHUNK
    st = pl.BlockSpec((1, RET_PAIRS, LANES, 2 * RET_DV), lambda i: (i, 0, 0, 0))
    st_rev = pl.BlockSpec((1, RET_PAIRS, LANES, 2 * RET_DV), lambda i: (n - 1 - i, 0, 0, 0))
    shp = jax.ShapeDtypeStruct((n, RET_PAIRS, LANES, 2 * RET_DV), BF16)
    return pl.pallas_call(
        _ret_state_kernel,
        grid=(n,),
        in_specs=[pl.BlockSpec(memory_space=pltpu.SMEM),
                  pl.BlockSpec((C, 512), lambda i: (i, 1)),
                  pl.BlockSpec((C, 1024), lambda i: (i, 1)),
                  pl.BlockSpec((C, 512), lambda i: (n - 1 - i, 1)),
                  pl.BlockSpec((C, 1024), lambda i: (n - 1 - i, 1)),
                  pl.BlockSpec((lc, 512), lambda i: (0, 1)),
                  pl.BlockSpec((lc, 1024), lambda i: (0, 1))],
        out_specs=[st, st_rev],
        out_shape=[shp, shp],
        scratch_shapes=[pltpu.VMEM((RET_PAIRS, LANES, 2 * RET_DV), F32),
                        pltpu.VMEM((RET_PAIRS, LANES, 2 * RET_DV), F32)],
        compiler_params=_params("arbitrary"),
        name="ret_state",
    )(dec, proj, proj, proj, proj, cproj, cproj)


def _ret_out_kernel(dec_ref, q_ref, k_ref, v_ref, g_ref, sf_ref, sb_ref, o_ref):
    C = RET_CHUNK
    pos = lax.broadcasted_iota(jnp.int32, (C, LANES), 0).astype(F32)
    n_i = lax.broadcasted_iota(jnp.int32, (C, 2 * C), 0)
    m_i = lax.broadcasted_iota(jnp.int32, (C, 2 * C), 1) % C
    rel = (n_i - m_i).astype(F32)
    lane = lax.broadcasted_iota(jnp.int32, (C, LANES), 1)
    lo = lane < 64
    for p in range(RET_PAIRS):
        ks = slice(p * LANES, (p + 1) * LANES)
        vs = slice(p * 2 * RET_DV, (p + 1) * 2 * RET_DV)
        q = q_ref[:, ks]
        k = k_ref[:, ks]
        v = v_ref[:, vs]
        zk = jnp.zeros_like(k)
        kst = jnp.concatenate([jnp.where(lo, k, zk), jnp.where(lo, zk, k)], axis=0)
        s = lax.dot_general(q, kst, (((1,), (1,)), ((), ())), preferred_element_type=F32)
        col_a = lax.broadcasted_iota(jnp.int32, (C, 2 * C), 1) < C
        raw_f = jnp.where(col_a, jnp.full((C, 2 * C), dec_ref[0, 2 * p], F32), jnp.full((C, 2 * C), dec_ref[0, 2 * p + 1], F32))
        raw_b = jnp.where(col_a, jnp.full((C, 2 * C), dec_ref[1, 2 * p], F32), jnp.full((C, 2 * C), dec_ref[1, 2 * p + 1], F32))
        dmat = jnp.where(rel >= 0, jnp.exp(-jnp.exp(raw_f) * jnp.maximum(rel, 0.0)),
                         jnp.exp(-jnp.exp(raw_b) * jnp.maximum(-rel, 0.0)))
        sd = (s * dmat).astype(BF16)
        qf32 = q.astype(F32)
        qwf = (qf32 * jnp.exp(_pair_lg(dec_ref, 0, p, (C, LANES)) * (pos + 1.0))).astype(BF16)
        qwb = (qf32 * jnp.exp(_pair_lg(dec_ref, 1, p, (C, LANES)) * (float(C) - pos))).astype(BF16)
        zv = jnp.zeros((C, RET_DV), BF16)
        vbd = jnp.concatenate([jnp.concatenate([v[:, :RET_DV], zv], axis=1),
                               jnp.concatenate([zv, v[:, RET_DV:]], axis=1)], axis=0)
        lhs = jnp.concatenate([sd, qwf, qwb], axis=1)
        rhs = jnp.concatenate([vbd, sf_ref[0, p], sb_ref[0, p]], axis=0)
        o = jnp.dot(lhs, rhs, preferred_element_type=F32)
        for t in range(2):
            oh = o[:, t * RET_DV:(t + 1) * RET_DV]
            oh = oh * lax.rsqrt(jnp.mean(oh * oh, axis=-1, keepdims=True) + NORM_EPS)
            cs = slice(p * 2 * RET_DV + t * RET_DV, p * 2 * RET_DV + (t + 1) * RET_DV)
            gt = g_ref[:, cs].astype(F32)
            o_ref[:, cs] = (oh * (gt / (1.0 + jnp.exp(-gt)))).astype(BF16)


def _ret_out(dec, proj, sf, sb):
    L = proj.shape[0]
    C = RET_CHUNK
    n = L // C
    st = pl.BlockSpec((1, RET_PAIRS, LANES, 2 * RET_DV), lambda i: (i, 0, 0, 0))
    return pl.pallas_call(
        _ret_out_kernel,
        grid=(n,),
        in_specs=[pl.BlockSpec(memory_space=pltpu.SMEM),
                  pl.BlockSpec((C, 512), lambda i: (i, 0)),
                  pl.BlockSpec((C, 512), lambda i: (i, 1)),
                  pl.BlockSpec((C, 1024), lambda i: (i, 1)),
                  pl.BlockSpec((C, 1024), lambda i: (i, 2)),
                  st, st],
        out_specs=pl.BlockSpec((C, RET_HEADS * RET_DV), lambda i: (i, 0)),
        out_shape=jax.ShapeDtypeStruct((L, RET_HEADS * RET_DV), BF16),
        compiler_params=_params("parallel"),
        name="ret_out",
    )(dec, proj, proj, proj, proj, sf, sb)


def _attn_kernel(sink_ref, q_ref, kp_ref, kc_ref, kn_ref, vp_ref, vc_ref, vn_ref, ck_ref, cv_ref, o_ref):
    n = pl.program_id(0)
    nblk = pl.num_programs(0)
    B = ATT_BLOCK
    lc = ck_ref.shape[0]
    qi = lax.broadcasted_iota(jnp.int32, (B, 3 * B), 0)
    kj = lax.broadcasted_iota(jnp.int32, (B, 3 * B), 1)
    ok_prev = jnp.where(n > 0, 0.0, MASK_NEG).astype(F32)
    ok_next = jnp.where(n < nblk - 1, 0.0, MASK_NEG).astype(F32)
    bias = jnp.where(kj < B, jnp.where(kj >= qi, ok_prev, MASK_NEG),
                     jnp.where(kj < 2 * B, 0.0, jnp.where(kj - 2 * B <= qi, ok_next, MASK_NEG))).astype(F32)
    lane = lax.broadcasted_iota(jnp.int32, (B, LANES), 1)
    lo = lane < 64
    hi = lane >= 64
    for g in range(ATT_KV_HEADS):
        gs = slice(g * LANES, (g + 1) * LANES)
        kcat = jnp.concatenate([kp_ref[:, gs], kc_ref[:, gs], kn_ref[:, gs], ck_ref[:, gs]], axis=0)
        vcat = jnp.concatenate([vp_ref[:, gs], vc_ref[:, gs], vn_ref[:, gs], cv_ref[:, gs]], axis=0)
        res = []
        for r in range(ATT_GROUP):
            h = ATT_GROUP * g + r
            qt = q_ref[:, (h // 2) * LANES:(h // 2 + 1) * LANES]
            keep = lo if h % 2 == 0 else hi
            qm = jnp.where(keep, qt * jnp.asarray(ATT_SCALE, BF16), jnp.zeros_like(qt))
            s = lax.dot_general(qm, kcat, (((1,), (1,)), ((), ())), preferred_element_type=F32)
            s_loc = s[:, :3 * B] + bias
            s_ctx = s[:, 3 * B:]
            sk = sink_ref[h]
            m = jnp.maximum(jnp.maximum(jnp.max(s_loc, axis=-1, keepdims=True),
                                        jnp.max(s_ctx, axis=-1, keepdims=True)), sk)
            e_loc = jnp.exp(s_loc - m)
            e_ctx = jnp.exp(s_ctx - m)
            den = (jnp.sum(e_loc, axis=-1, keepdims=True) + jnp.sum(e_ctx, axis=-1, keepdims=True)
                   + jnp.exp(sk - m))
            pb = jnp.concatenate([e_loc, e_ctx], axis=1).astype(BF16)
            r_h = jnp.dot(pb, vcat, preferred_element_type=F32)
            res.append(r_h / den)
        for t in range(2):
            tile = jnp.where(lo, res[2 * t], res[2 * t + 1])
            c0 = (2 * g + t) * LANES
            o_ref[:, c0:c0 + LANES] = tile.astype(BF16)


def _attn(sink, proj, kd, vd, ckd, cvd):
    L = proj.shape[0]
    B = ATT_BLOCK
    n = L // B
    lc = ckd.shape[0]
    prev = lambda i: (jnp.maximum(i - 1, 0), 0)
    cur = lambda i: (i, 0)
    nxt = lambda i: (jnp.minimum(i + 1, n - 1), 0)
    kv = lambda f: pl.BlockSpec((B, 512), f)
    full = pl.BlockSpec((lc, 512), lambda i: (0, 0))
    return pl.pallas_call(
        _attn_kernel,
        grid=(n,),
        in_specs=[pl.BlockSpec(memory_space=pltpu.SMEM),
                  pl.BlockSpec((B, 1024), lambda i: (i, 3)),
                  kv(prev), kv(cur), kv(nxt), kv(prev), kv(cur), kv(nxt), full, full],
        out_specs=pl.BlockSpec((B, ATT_HEADS * ATT_DH), cur),
        out_shape=jax.ShapeDtypeStruct((L, ATT_HEADS * ATT_DH), BF16),
        compiler_params=_params("parallel"),
        name="attn",
    )(sink, proj, kd, kd, kd, vd, vd, vd, ckd, cvd)


def _out_proj_kernel(yr_ref, ya_ref, w_ref, x_ref, gt_ref, o_ref):
    kr = yr_ref.shape[1]
    acc = jnp.dot(yr_ref[...], w_ref[:kr, :], preferred_element_type=F32)
    acc = acc + jnp.dot(ya_ref[...], w_ref[kr:, :], preferred_element_type=F32)
    o_ref[...] = x_ref[...] + gt_ref[...] * acc


def _out_proj(yr, ya, w, x, gt, *, tm):
    m, d = x.shape
    kr, ka = yr.shape[1], ya.shape[1]
    row = lambda i: (i, 0)
    return pl.pallas_call(
        _out_proj_kernel,
        grid=(m // tm,),
        in_specs=[pl.BlockSpec((tm, kr), row), pl.BlockSpec((tm, ka), row),
                  pl.BlockSpec((kr + ka, d), lambda i: (0, 0)),
                  pl.BlockSpec((tm, d), row),
                  pl.BlockSpec((1, d), lambda i: (0, 0))],
        out_specs=pl.BlockSpec((tm, d), row),
        out_shape=jax.ShapeDtypeStruct((m, d), F32),
        compiler_params=_params("parallel"),
        name="out_proj",
    )(yr, ya, w, x, gt)


def _ffn_kernel(x_ref, g_ref, sh_ref, sc_ref, gt_ref, gfin_ref, wg_ref, wu_ref, wd_ref, o_ref, h_ref):
    f = pl.program_id(1)

    @pl.when(f == 0)
    def _():
        xf = x_ref[...]
        y = xf * lax.rsqrt(jnp.mean(xf * xf, axis=-1, keepdims=True) + NORM_EPS)
        y = y * g_ref[...]
        h_ref[...] = (y * (1.0 + sc_ref[...]) + sh_ref[...]).astype(BF16)

    h = h_ref[...]
    a = jnp.dot(h, wg_ref[...], preferred_element_type=F32)
    u = jnp.dot(h, wu_ref[...], preferred_element_type=F32)
    act = ((a / (1.0 + jnp.exp(-a))) * u).astype(BF16)
    part = jnp.dot(act, wd_ref[...], preferred_element_type=F32)

    @pl.when(f == 0)
    def _():
        o_ref[...] = part

    @pl.when(f > 0)
    def _():
        o_ref[...] += part

    @pl.when(f == pl.num_programs(1) - 1)
    def _():
        y = x_ref[...] + gt_ref[...] * o_ref[...]
        y = y * lax.rsqrt(jnp.mean(y * y, axis=-1, keepdims=True) + NORM_EPS)
        o_ref[...] = y * gfin_ref[...]


def _ffn(x, g, sh, sc, gt, gfin, wg, wu, wd, *, tm, tf):
    m, d = x.shape
    ff = wg.shape[1]
    assert ff % tf == 0 and m % tm == 0
    row = lambda i, f: (i, 0)
    vec = pl.BlockSpec((1, d), lambda i, f: (0, 0))
    return pl.pallas_call(
        _ffn_kernel,
        grid=(m // tm, ff // tf),
        in_specs=[pl.BlockSpec((tm, d), row), vec, vec, vec, vec, vec,
                  pl.BlockSpec((d, tf), lambda i, f: (0, f)),
                  pl.BlockSpec((d, tf), lambda i, f: (0, f)),
                  pl.BlockSpec((tf, d), lambda i, f: (f, 0))],
        out_specs=pl.BlockSpec((tm, d), row),
        out_shape=jax.ShapeDtypeStruct((m, d), F32),
        scratch_shapes=[pltpu.VMEM((tm, d), BF16)],
        compiler_params=_params("parallel", "arbitrary"),
        name="ffn",
    )(x, g, sh, sc, gt, gfin, wg, wu, wd)


def _rope_tables(L):
    t = jnp.arange(L)
    lane = jnp.arange(LANES)
    inv1 = ROPE_BASE ** (-jnp.arange(32, dtype=F32) / 32)
    ang1 = t.astype(F32)[:, None] * inv1[lane % 32][None, :]
    sgn1 = jnp.where((lane % 64) < 32, -1.0, 1.0).astype(F32)
    inv2 = ROPE_BASE ** (-jnp.arange(16, dtype=F32) / 16)
    rows = (t // GRID_W).astype(F32)
    cols = (t % GRID_W).astype(F32)
    posa = jnp.where(((lane % 64) < 32)[None, :], rows[:, None], cols[:, None])
    anga = posa * inv2[lane % 16][None, :]
    sgna = jnp.where((lane % 32) < 16, -1.0, 1.0).astype(F32)
    return (jnp.cos(ang1), jnp.sin(ang1) * sgn1[None, :], jnp.cos(anga), jnp.sin(anga) * sgna[None, :])


def kernel(x, c, ctx, c_ctx, w_mod, b_mod, norm_mix, norm_ffn, w_in, ret_decay, attn_sink,
           w_out, w_gate, w_up, w_down, norm_final):
    B, L, D = x.shape
    assert B == 1 and w_mod.shape[0] == 1, "single batch element, depth-1 layer"
    lc = ctx.shape[1]
    x2 = x[0]
    xc2 = ctx[0]

    cv = jnp.zeros((8, D), F32).at[0].set(c[0]).at[1].set(c_ctx)
    mod = _mod(cv, w_mod[0], b_mod[0][None, :])
    sh_m, sc_m, gt_m, sh_f, sc_f, gt_f = [mod[0:1, k * D:(k + 1) * D] for k in range(6)]
    sh_mc, sc_mc = mod[1:2, 0:D], mod[1:2, D:2 * D]

    w_in_b = w_in[0].astype(BF16)
    g_mix = norm_mix[0][None, :]
    tabs = _rope_tables(L)
    ctabs = tuple(tb[:lc] for tb in tabs)
    proj, kd, vd = _in_proj(x2, g_mix, sh_m, sc_m, w_in_b, tabs, rope=True, tm=1024)
    cproj, ckd, cvd = _in_proj(xc2, g_mix, sh_mc, sc_mc, w_in_b, ctabs, rope=False, tm=lc)

    dec = ret_decay[0].astype(F32)
    sf, sb = _ret_states(dec, proj, cproj)
    y_ret = _ret_out(dec, proj, sf, sb)
    y_att = _attn(attn_sink[0].astype(F32), proj, kd, vd, ckd, cvd)

    x1 = _out_proj(y_ret, y_att, w_out[0].astype(BF16), x2, gt_m, tm=256)
    out = _ffn(x1, norm_ffn[0][None, :], sh_f, sc_f, gt_f, norm_final[None, :],
               w_gate[0].astype(BF16), w_up[0].astype(BF16), w_down[0].astype(BF16), tm=512, tf=512)
    return out[None]
```

```python
import functools

import jax
import jax.numpy as jnp
from jax import lax
from jax.experimental import pallas as pl
from jax.experimental.pallas import tpu as pltpu

GRID_W = 64
RET_HEADS = 8
RET_DK = 64
RET_DV = 128
RET_CHUNK = 128
ATT_HEADS = 16
ATT_KV_HEADS = 4
ATT_DH = 64
ATT_GROUP = ATT_HEADS // ATT_KV_HEADS
WINDOW = 128
ATT_BLOCK = 128
ROPE_BASE = 10000.0
NORM_EPS = 1e-6
K_SCALE = RET_DK ** -0.5
ATT_SCALE = ATT_DH ** -0.5

LANES = 128
RET_PAIRS = RET_HEADS // 2
MASK_NEG = -1e30
VMEM_LIMIT = 56 * 1024 * 1024
OUT_ROW_CHUNK = 256

BF16 = jnp.bfloat16
F32 = jnp.float32


def _params(*sem):
    return pltpu.CompilerParams(dimension_semantics=sem, vmem_limit_bytes=VMEM_LIMIT)


def _mod_kernel(cv_ref, w_ref, b_ref, o_ref):
    cv = cv_ref[...]
    s = cv / (1.0 + jnp.exp(-cv))
    o_ref[...] = jnp.dot(s.astype(BF16), w_ref[...].astype(BF16),
                         preferred_element_type=F32) + b_ref[...]


def _mod(cv, w, b):
    d, n = w.shape
    tn = 1024
    return pl.pallas_call(
        _mod_kernel,
        grid=(n // tn,),
        in_specs=[pl.BlockSpec((8, d), lambda j: (0, 0)),
                  pl.BlockSpec((d, tn), lambda j: (0, j)),
                  pl.BlockSpec((1, tn), lambda j: (0, j))],
        out_specs=pl.BlockSpec((8, tn), lambda j: (0, j)),
        out_shape=jax.ShapeDtypeStruct((8, n), F32),
        compiler_params=_params("parallel"),
        name="mod",
    )(cv, w, b)


def _rot_pairs(a, cos, sin_signed, half):
    lane = lax.broadcasted_iota(jnp.int32, a.shape, 1)
    first = (lane % (2 * half)) < half
    rot = jnp.where(first, pltpu.roll(a, LANES - half, 1), pltpu.roll(a, half, 1))
    return a * cos + rot * sin_signed


def _dup_halves(a):
    lane = lax.broadcasted_iota(jnp.int32, a.shape, 1)
    r = pltpu.roll(a, 64, 1)
    lo = lane < 64
    return jnp.where(lo, a, r), jnp.where(lo, r, a)


def _in_proj_kernel(x_ref, g_ref, sh_ref, sc_ref, w_ref, c1_ref, s1_ref, ca_ref, sa_ref,
                    o_ref, kd_ref, vd_ref, h_ref, acc_ref, *, rope):
    j = pl.program_id(1)

    @pl.when(j == 0)
    def _():
        xf = x_ref[...]
        y = xf * lax.rsqrt(jnp.mean(xf * xf, axis=-1, keepdims=True) + NORM_EPS)
        y = y * g_ref[...]
        h_ref[...] = (y * (1.0 + sc_ref[...]) + sh_ref[...]).astype(BF16)

    acc_ref[...] = jnp.dot(h_ref[...], w_ref[...], preferred_element_type=F32)

    def chunk(c):
        return acc_ref[:, c * LANES:(c + 1) * LANES]

    def put(c, a):
        o_ref[:, c * LANES:(c + 1) * LANES] = a.astype(BF16)

    def rope1(a):
        return _rot_pairs(a, c1_ref[...], s1_ref[...], 32) if rope else a

    def ropea(a):
        return _rot_pairs(a, ca_ref[...], sa_ref[...], 16) if rope else a

    @pl.when(j == 0)
    def _():
        for c in range(4):
            put(c, rope1(chunk(c)))

    @pl.when(j == 1)
    def _():
        for c in range(4):
            put(c, rope1(chunk(c)) * K_SCALE)

    @pl.when((j >= 2) & (j <= 5))
    def _():
        for c in range(4):
            put(c, chunk(c))

    @pl.when((j == 6) | (j == 7))
    def _():
        for c in range(4):
            put(c, ropea(chunk(c)))

    @pl.when(j == 8)
    def _():
        for c in range(2):
            k = ropea(chunk(c))
            put(c, k)
            k0, k1 = _dup_halves(k)
            kd_ref[:, (2 * c) * LANES:(2 * c + 1) * LANES] = k0.astype(BF16)
            kd_ref[:, (2 * c + 1) * LANES:(2 * c + 2) * LANES] = k1.astype(BF16)
            v = chunk(2 + c)
            put(2 + c, v)
            v0, v1 = _dup_halves(v)
            vd_ref[:, (2 * c) * LANES:(2 * c + 1) * LANES] = v0.astype(BF16)
            vd_ref[:, (2 * c + 1) * LANES:(2 * c + 2) * LANES] = v1.astype(BF16)


def _in_proj(x, g, sh, sc, w, tabs, *, rope, tm):
    m, d = x.shape
    n = w.shape[1]
    tn = 512
    assert n == 9 * tn and m % tm == 0
    c1, s1, ca, sa = tabs
    row = lambda i, j: (i, 0)
    vec = pl.BlockSpec((1, d), lambda i, j: (0, 0))
    tab = pl.BlockSpec((tm, LANES), row)
    return pl.pallas_call(
        functools.partial(_in_proj_kernel, rope=rope),
        grid=(m // tm, n // tn),
        in_specs=[pl.BlockSpec((tm, d), row), vec, vec, vec,
                  pl.BlockSpec((d, tn), lambda i, j: (0, j)),
                  tab, tab, tab, tab],
        out_specs=[pl.BlockSpec((tm, tn), lambda i, j: (i, j)),
                   pl.BlockSpec((tm, 512), row),
                   pl.BlockSpec((tm, 512), row)],
        out_shape=[jax.ShapeDtypeStruct((m, n), BF16),
                   jax.ShapeDtypeStruct((m, 512), BF16),
                   jax.ShapeDtypeStruct((m, 512), BF16)],
        scratch_shapes=[pltpu.VMEM((tm, d), BF16), pltpu.VMEM((tm, tn), F32)],
        compiler_params=_params("parallel", "arbitrary"),
        name="in_proj" if rope else "in_proj_ctx",
    )(x, g, sh, sc, w, c1, s1, ca, sa)


def _pair_lg(dec_ref, d, p, shape):
    lane = lax.broadcasted_iota(jnp.int32, shape, 1)
    first = (lane % LANES) < 64
    raw = jnp.where(first, jnp.full(shape, dec_ref[d, 2 * p], F32), jnp.full(shape, dec_ref[d, 2 * p + 1], F32))
    return -jnp.exp(raw)


def _head_block_mask(shape):
    r = lax.broadcasted_iota(jnp.int32, shape, 0)
    c = lax.broadcasted_iota(jnp.int32, shape, 1)
    return (r // 64) == (c // LANES)


def _kv_pair(k_pair, v_pair, w):
    kw = (k_pair.astype(F32) * w).astype(BF16)
    kv = lax.dot_general(kw, v_pair, (((0,), (0,)), ((), ())), preferred_element_type=F32)
    return jnp.where(_head_block_mask(kv.shape), kv, 0.0)


def _ret_state_kernel(dec_ref, kf_ref, vf_ref, kb_ref, vb_ref, ck_ref, cv_ref,
                      sf_ref, sb_ref, sfs, sbs):
    i = pl.program_id(0)
    C = RET_CHUNK
    lc = ck_ref.shape[0]

    @pl.when(i == 0)
    def _():
        pos = lax.broadcasted_iota(jnp.int32, (lc, LANES), 0).astype(F32)
        for p in range(RET_PAIRS):
            ks = slice(p * LANES, (p + 1) * LANES)
            vs = slice(p * 2 * RET_DV, (p + 1) * 2 * RET_DV)
            wf = jnp.exp(_pair_lg(dec_ref, 0, p, (lc, LANES)) * (lc - 1.0 - pos))
            wb = jnp.exp(_pair_lg(dec_ref, 1, p, (lc, LANES)) * pos)
            sfs[p] = _kv_pair(ck_ref[:, ks], cv_ref[:, vs], wf)
            sbs[p] = _kv_pair(ck_ref[:, ks], cv_ref[:, vs], wb)

    pos = lax.broadcasted_iota(jnp.int32, (C, LANES), 0).astype(F32)
    for p in range(RET_PAIRS):
        ks = slice(p * LANES, (p + 1) * LANES)
        vs = slice(p * 2 * RET_DV, (p + 1) * 2 * RET_DV)
        sf = sfs[p]
        sf_ref[0, p] = sf.astype(BF16)
        wf = jnp.exp(_pair_lg(dec_ref, 0, p, (C, LANES)) * (C - 1.0 - pos))
        rowh = lax.broadcasted_iota(jnp.int32, sf.shape, 0) < 64
        gf = jnp.exp(-jnp.exp(jnp.where(rowh, jnp.full(sf.shape, dec_ref[0, 2 * p], F32),
                                         jnp.full(sf.shape, dec_ref[0, 2 * p + 1], F32))) * float(C))
        sfs[p] = gf * sf + _kv_pair(kf_ref[:, ks], vf_ref[:, vs], wf)
        sb = sbs[p]
        sb_ref[0, p] = sb.astype(BF16)
        wb = jnp.exp(_pair_lg(dec_ref, 1, p, (C, LANES)) * pos)
        gb = jnp.exp(-jnp.exp(jnp.where(rowh, jnp.full(sb.shape, dec_ref[1, 2 * p], F32),
                                         jnp.full(sb.shape, dec_ref[1, 2 * p + 1], F32))) * float(C))
        sbs[p] = gb * sb + _kv_pair(kb_ref[:, ks], vb_ref[:, vs], wb)


def _ret_states(dec, proj, cproj):
    L = proj.shape[0]
    lc = cproj.shape[0]
    n = L // RET_CHUNK
    C = RET_CHUNK
    st = pl.BlockSpec((1, RET_PAIRS, LANES, 2 * RET_DV), lambda i: (i, 0, 0, 0))
    st_rev = pl.BlockSpec((1, RET_PAIRS, LANES, 2 * RET_DV), lambda i: (n - 1 - i, 0, 0, 0))
    shp = jax.ShapeDtypeStruct((n, RET_PAIRS, LANES, 2 * RET_DV), BF16)
    return pl.pallas_call(
        _ret_state_kernel,
        grid=(n,),
        in_specs=[pl.BlockSpec(memory_space=pltpu.SMEM),
                  pl.BlockSpec((C, 512), lambda i: (i, 1)),
                  pl.BlockSpec((C, 1024), lambda i: (i, 1)),
                  pl.BlockSpec((C, 512), lambda i: (n - 1 - i, 1)),
                  pl.BlockSpec((C, 1024), lambda i: (n - 1 - i, 1)),
                  pl.BlockSpec((lc, 512), lambda i: (0, 1)),
                  pl.BlockSpec((lc, 1024), lambda i: (0, 1))],
        out_specs=[st, st_rev],
        out_shape=[shp, shp],
        scratch_shapes=[pltpu.VMEM((RET_PAIRS, LANES, 2 * RET_DV), F32),
                        pltpu.VMEM((RET_PAIRS, LANES, 2 * RET_DV), F32)],
        compiler_params=_params("arbitrary"),
        name="ret_state",
    )(dec, proj, proj, proj, proj, cproj, cproj)


def _ret_out_kernel(dec_ref, q_ref, k_ref, v_ref, g_ref, sf_ref, sb_ref, o_ref):
    C = RET_CHUNK
    pos = lax.broadcasted_iota(jnp.int32, (C, LANES), 0).astype(F32)
    n_i = lax.broadcasted_iota(jnp.int32, (C, 2 * C), 0)
    m_i = lax.broadcasted_iota(jnp.int32, (C, 2 * C), 1) % C
    rel = (n_i - m_i).astype(F32)
    lane = lax.broadcasted_iota(jnp.int32, (C, LANES), 1)
    lo = lane < 64
    for p in range(RET_PAIRS):
        ks = slice(p * LANES, (p + 1) * LANES)
        vs = slice(p * 2 * RET_DV, (p + 1) * 2 * RET_DV)
        q = q_ref[:, ks]
        k = k_ref[:, ks]
        v = v_ref[:, vs]
        zk = jnp.zeros_like(k)
        kst = jnp.concatenate([jnp.where(lo, k, zk), jnp.where(lo, zk, k)], axis=0)
        s = lax.dot_general(q, kst, (((1,), (1,)), ((), ())), preferred_element_type=F32)
        col_a = lax.broadcasted_iota(jnp.int32, (C, 2 * C), 1) < C
        raw_f = jnp.where(col_a, jnp.full((C, 2 * C), dec_ref[0, 2 * p], F32), jnp.full((C, 2 * C), dec_ref[0, 2 * p + 1], F32))
        raw_b = jnp.where(col_a, jnp.full((C, 2 * C), dec_ref[1, 2 * p], F32), jnp.full((C, 2 * C), dec_ref[1, 2 * p + 1], F32))
        dmat = jnp.where(rel >= 0, jnp.exp(-jnp.exp(raw_f) * jnp.maximum(rel, 0.0)),
                         jnp.exp(-jnp.exp(raw_b) * jnp.maximum(-rel, 0.0)))
        sd = (s * dmat).astype(BF16)
        qf32 = q.astype(F32)
        qwf = (qf32 * jnp.exp(_pair_lg(dec_ref, 0, p, (C, LANES)) * (pos + 1.0))).astype(BF16)
        qwb = (qf32 * jnp.exp(_pair_lg(dec_ref, 1, p, (C, LANES)) * (float(C) - pos))).astype(BF16)
        zv = jnp.zeros((C, RET_DV), BF16)
        vbd = jnp.concatenate([jnp.concatenate([v[:, :RET_DV], zv], axis=1),
                               jnp.concatenate([zv, v[:, RET_DV:]], axis=1)], axis=0)
        lhs = jnp.concatenate([sd, qwf, qwb], axis=1)
        rhs = jnp.concatenate([vbd, sf_ref[0, p], sb_ref[0, p]], axis=0)
        o = jnp.dot(lhs, rhs, preferred_element_type=F32)
        for t in range(2):
            oh = o[:, t * RET_DV:(t + 1) * RET_DV]
            oh = oh * lax.rsqrt(jnp.mean(oh * oh, axis=-1, keepdims=True) + NORM_EPS)
            cs = slice(p * 2 * RET_DV + t * RET_DV, p * 2 * RET_DV + (t + 1) * RET_DV)
            gt = g_ref[:, cs].astype(F32)
            o_ref[:, cs] = (oh * (gt / (1.0 + jnp.exp(-gt)))).astype(BF16)


def _ret_out(dec, proj, sf, sb):
    L = proj.shape[0]
    C = RET_CHUNK
    n = L // C
    st = pl.BlockSpec((1, RET_PAIRS, LANES, 2 * RET_DV), lambda i: (i, 0, 0, 0))
    return pl.pallas_call(
        _ret_out_kernel,
        grid=(n,),
        in_specs=[pl.BlockSpec(memory_space=pltpu.SMEM),
                  pl.BlockSpec((C, 512), lambda i: (i, 0)),
                  pl.BlockSpec((C, 512), lambda i: (i, 1)),
                  pl.BlockSpec((C, 1024), lambda i: (i, 1)),
                  pl.BlockSpec((C, 1024), lambda i: (i, 2)),
                  st, st],
        out_specs=pl.BlockSpec((C, RET_HEADS * RET_DV), lambda i: (i, 0)),
        out_shape=jax.ShapeDtypeStruct((L, RET_HEADS * RET_DV), BF16),
        compiler_params=_params("parallel"),
        name="ret_out",
    )(dec, proj, proj, proj, proj, sf, sb)


def _attn_kernel(sink_ref, q_ref, kp_ref, kc_ref, kn_ref, vp_ref, vc_ref, vn_ref, ck_ref, cv_ref, o_ref):
    n = pl.program_id(0)
    nblk = pl.num_programs(0)
    B = ATT_BLOCK
    lc = ck_ref.shape[0]
    qi = lax.broadcasted_iota(jnp.int32, (B, 3 * B), 0)
    kj = lax.broadcasted_iota(jnp.int32, (B, 3 * B), 1)
    ok_prev = jnp.where(n > 0, 0.0, MASK_NEG).astype(F32)
    ok_next = jnp.where(n < nblk - 1, 0.0, MASK_NEG).astype(F32)
    bias = jnp.where(kj < B, jnp.where(kj >= qi, ok_prev, MASK_NEG),
                     jnp.where(kj < 2 * B, 0.0, jnp.where(kj - 2 * B <= qi, ok_next, MASK_NEG))).astype(F32)
    bias = jnp.concatenate([bias, jnp.zeros((B, lc), F32)], axis=1)
    bias4 = jnp.concatenate([bias] * ATT_GROUP, axis=0)
    lane = lax.broadcasted_iota(jnp.int32, (B, LANES), 1)
    lo = lane < 64
    hi = lane >= 64
    for g in range(ATT_KV_HEADS):
        gs = slice(g * LANES, (g + 1) * LANES)
        kcat = jnp.concatenate([kp_ref[:, gs], kc_ref[:, gs], kn_ref[:, gs], ck_ref[:, gs]], axis=0)
        vcat = jnp.concatenate([vp_ref[:, gs], vc_ref[:, gs], vn_ref[:, gs], cv_ref[:, gs]], axis=0)
        qs, sinks = [], []
        for r in range(ATT_GROUP):
            h = ATT_GROUP * g + r
            qt = q_ref[:, (h // 2) * LANES:(h // 2 + 1) * LANES]
            keep = lo if h % 2 == 0 else hi
            qs.append(jnp.where(keep, qt * jnp.asarray(ATT_SCALE, BF16), jnp.zeros_like(qt)))
            sinks.append(jnp.full((B, 1), sink_ref[h], F32))
        q4 = jnp.concatenate(qs, axis=0)
        sk = jnp.concatenate(sinks, axis=0)
        s = lax.dot_general(q4, kcat, (((1,), (1,)), ((), ())), preferred_element_type=F32)
        s = s + bias4
        m = jnp.maximum(jnp.max(s, axis=-1, keepdims=True), sk)
        e = jnp.exp(s - m)
        den = jnp.sum(e, axis=-1, keepdims=True) + jnp.exp(sk - m)
        res = jnp.dot(e.astype(BF16), vcat, preferred_element_type=F32) / den
        for t in range(2):
            tile = jnp.where(lo, res[(2 * t) * B:(2 * t + 1) * B], res[(2 * t + 1) * B:(2 * t + 2) * B])
            c0 = (2 * g + t) * LANES
            o_ref[:, c0:c0 + LANES] = tile.astype(BF16)


def _attn(sink, proj, kd, vd, ckd, cvd):
    L = proj.shape[0]
    B = ATT_BLOCK
    n = L // B
    lc = ckd.shape[0]
    prev = lambda i: (jnp.maximum(i - 1, 0), 0)
    cur = lambda i: (i, 0)
    nxt = lambda i: (jnp.minimum(i + 1, n - 1), 0)
    kv = lambda f: pl.BlockSpec((B, 512), f)
    full = pl.BlockSpec((lc, 512), lambda i: (0, 0))
    return pl.pallas_call(
        _attn_kernel,
        grid=(n,),
        in_specs=[pl.BlockSpec(memory_space=pltpu.SMEM),
                  pl.BlockSpec((B, 1024), lambda i: (i, 3)),
                  kv(prev), kv(cur), kv(nxt), kv(prev), kv(cur), kv(nxt), full, full],
        out_specs=pl.BlockSpec((B, ATT_HEADS * ATT_DH), cur),
        out_shape=jax.ShapeDtypeStruct((L, ATT_HEADS * ATT_DH), BF16),
        compiler_params=_params("parallel"),
        name="attn",
    )(sink, proj, kd, kd, kd, vd, vd, vd, ckd, cvd)


def _out_proj_kernel(yr_ref, ya_ref, w_ref, x_ref, gt_ref, g_ref, sh_ref, sc_ref, o_ref, h_ref):
    kr = yr_ref.shape[1]
    for r in range(yr_ref.shape[0] // OUT_ROW_CHUNK):
        rs = slice(r * OUT_ROW_CHUNK, (r + 1) * OUT_ROW_CHUNK)
        acc = jnp.dot(yr_ref[rs, :], w_ref[:kr, :], preferred_element_type=F32)
        acc = acc + jnp.dot(ya_ref[rs, :], w_ref[kr:, :], preferred_element_type=F32)
        x1 = x_ref[rs, :] + gt_ref[...] * acc
        o_ref[rs, :] = x1
        y = x1 * lax.rsqrt(jnp.mean(x1 * x1, axis=-1, keepdims=True) + NORM_EPS)
        y = y * g_ref[...]
        h_ref[rs, :] = (y * (1.0 + sc_ref[...]) + sh_ref[...]).astype(BF16)


def _out_proj(yr, ya, w, x, gt, g, sh, sc, *, tm):
    m, d = x.shape
    kr, ka = yr.shape[1], ya.shape[1]
    row = lambda i: (i, 0)
    vec = pl.BlockSpec((1, d), lambda i: (0, 0))
    return pl.pallas_call(
        _out_proj_kernel,
        grid=(m // tm,),
        in_specs=[pl.BlockSpec((tm, kr), row), pl.BlockSpec((tm, ka), row),
                  pl.BlockSpec((kr + ka, d), lambda i: (0, 0)),
                  pl.BlockSpec((tm, d), row), vec, vec, vec, vec],
        out_specs=[pl.BlockSpec((tm, d), row), pl.BlockSpec((tm, d), row)],
        out_shape=[jax.ShapeDtypeStruct((m, d), F32), jax.ShapeDtypeStruct((m, d), BF16)],
        compiler_params=_params("parallel"),
        name="out_proj",
    )(yr, ya, w, x, gt, g, sh, sc)


def _ffn_kernel(h_ref, x_ref, gt_ref, gfin_ref, wg_ref, wu_ref, wd_ref, o_ref):
    f = pl.program_id(1)

    @pl.when(f == 0)
    def _():
        o_ref[...] = jnp.zeros_like(o_ref)

    h = h_ref[...]
    a = jnp.dot(h, wg_ref[...], preferred_element_type=F32)
    u = jnp.dot(h, wu_ref[...], preferred_element_type=F32)
    act = ((a / (1.0 + jnp.exp(-a))) * u).astype(BF16)
    o_ref[...] += jnp.dot(act, wd_ref[...], preferred_element_type=F32)

    @pl.when(f == pl.num_programs(1) - 1)
    def _():
        y = x_ref[...] + gt_ref[...] * o_ref[...]
        y = y * lax.rsqrt(jnp.mean(y * y, axis=-1, keepdims=True) + NORM_EPS)
        o_ref[...] = y * gfin_ref[...]


def _ffn(h, x, gt, gfin, wg, wu, wd, *, tm, tf):
    m, d = x.shape
    ff = wg.shape[1]
    assert ff % tf == 0 and m % tm == 0
    row = lambda i, f: (i, 0)
    vec = pl.BlockSpec((1, d), lambda i, f: (0, 0))
    return pl.pallas_call(
        _ffn_kernel,
        grid=(m // tm, ff // tf),
        in_specs=[pl.BlockSpec((tm, d), row), pl.BlockSpec((tm, d), row), vec, vec,
                  pl.BlockSpec((d, tf), lambda i, f: (0, f)),
                  pl.BlockSpec((d, tf), lambda i, f: (0, f)),
                  pl.BlockSpec((tf, d), lambda i, f: (f, 0))],
        out_specs=pl.BlockSpec((tm, d), row),
        out_shape=jax.ShapeDtypeStruct((m, d), F32),
        compiler_params=_params("parallel", "arbitrary"),
        name="ffn",
    )(h, x, gt, gfin, wg, wu, wd)


def _rope_tables(L):
    lane = jnp.arange(LANES)
    inv1 = ROPE_BASE ** (-jnp.arange(32, dtype=F32) / 32)
    ang1 = jnp.arange(L, dtype=F32)[:, None] * inv1[None, :]
    sgn1 = jnp.where((lane % 64) < 32, -1.0, 1.0).astype(F32)
    cos1 = jnp.tile(jnp.cos(ang1), (1, LANES // 32))
    sin1 = jnp.tile(jnp.sin(ang1), (1, LANES // 32)) * sgn1[None, :]
    inv2 = ROPE_BASE ** (-jnp.arange(16, dtype=F32) / 16)
    nrow = L // GRID_W
    ang_r = jnp.arange(nrow, dtype=F32)[:, None] * inv2[None, :]
    ang_c = jnp.arange(GRID_W, dtype=F32)[:, None] * inv2[None, :]
    sgna = jnp.where((lane % 32) < 16, -1.0, 1.0).astype(F32)

    def expand(fr, fc):
        by_row = jnp.broadcast_to(jnp.tile(fr, (1, 2))[:, None, :], (nrow, GRID_W, 32))
        by_col = jnp.broadcast_to(jnp.tile(fc, (1, 2))[None, :, :], (nrow, GRID_W, 32))
        head = jnp.concatenate([by_row, by_col], axis=-1).reshape(L, 64)
        return jnp.tile(head, (1, LANES // 64))

    cosa = expand(jnp.cos(ang_r), jnp.cos(ang_c))
    sina = expand(jnp.sin(ang_r), jnp.sin(ang_c)) * sgna[None, :]
    return cos1, sin1, cosa, sina


def kernel(x, c, ctx, c_ctx, w_mod, b_mod, norm_mix, norm_ffn, w_in, ret_decay, attn_sink,
           w_out, w_gate, w_up, w_down, norm_final):
    B, L, D = x.shape
    assert B == 1 and w_mod.shape[0] == 1, "single batch element, depth-1 layer"
    lc = ctx.shape[1]
    x2 = x[0]
    xc2 = ctx[0]

    cv = jnp.zeros((8, D), F32).at[0].set(c[0]).at[1].set(c_ctx)
    mod = _mod(cv, w_mod[0], b_mod[0][None, :])
    sh_m, sc_m, gt_m, sh_f, sc_f, gt_f = [mod[0:1, k * D:(k + 1) * D] for k in range(6)]
    sh_mc, sc_mc = mod[1:2, 0:D], mod[1:2, D:2 * D]

    w_in_b = w_in[0].astype(BF16)
    g_mix = norm_mix[0][None, :]
    tabs = _rope_tables(L)
    ctabs = tuple(tb[:lc] for tb in tabs)
    proj, kd, vd = _in_proj(x2, g_mix, sh_m, sc_m, w_in_b, tabs, rope=True, tm=1024)
    cproj, ckd, cvd = _in_proj(xc2, g_mix, sh_mc, sc_mc, w_in_b, ctabs, rope=False, tm=lc)

    dec = ret_decay[0].astype(F32)
    sf, sb = _ret_states(dec, proj, cproj)
    y_ret = _ret_out(dec, proj, sf, sb)
    y_att = _attn(attn_sink[0].astype(F32), proj, kd, vd, ckd, cvd)

    x1, hff = _out_proj(y_ret, y_att, w_out[0].astype(BF16), x2, gt_m,
                        norm_ffn[0][None, :], sh_f, sc_f, tm=512)
    out = _ffn(hff, x1, gt_f, norm_final[None, :],
               w_gate[0].astype(BF16), w_up[0].astype(BF16), w_down[0].astype(BF16), tm=512, tf=512)
    return out[None]
```

```python
import functools

import jax
import jax.numpy as jnp
from jax import lax
from jax.experimental import pallas as pl
from jax.experimental.pallas import tpu as pltpu

GRID_W = 64
RET_HEADS = 8
RET_DK = 64
RET_DV = 128
RET_CHUNK = 128
ATT_HEADS = 16
ATT_KV_HEADS = 4
ATT_DH = 64
ATT_GROUP = ATT_HEADS // ATT_KV_HEADS
WINDOW = 128
ATT_BLOCK = 128
ROPE_BASE = 10000.0
NORM_EPS = 1e-6
K_SCALE = RET_DK ** -0.5
ATT_SCALE = ATT_DH ** -0.5

LANES = 128
RET_PAIRS = RET_HEADS // 2
MASK_NEG = -1e30
VMEM_LIMIT = 56 * 1024 * 1024
RET_STEP_CHUNKS = 4
OUT_ROW_CHUNK = 256

BF16 = jnp.bfloat16
F32 = jnp.float32


def _params(*sem):
    return pltpu.CompilerParams(dimension_semantics=sem, vmem_limit_bytes=VMEM_LIMIT)


def _mod_kernel(cv_ref, w_ref, b_ref, o_ref):
    cv = cv_ref[...]
    s = cv / (1.0 + jnp.exp(-cv))
    o_ref[...] = jnp.dot(s.astype(BF16), w_ref[...].astype(BF16),
                         preferred_element_type=F32) + b_ref[...]


def _mod(cv, w, b):
    d, n = w.shape
    tn = 1024
    return pl.pallas_call(
        _mod_kernel,
        grid=(n // tn,),
        in_specs=[pl.BlockSpec((8, d), lambda j: (0, 0)),
                  pl.BlockSpec((d, tn), lambda j: (0, j)),
                  pl.BlockSpec((1, tn), lambda j: (0, j))],
        out_specs=pl.BlockSpec((8, tn), lambda j: (0, j)),
        out_shape=jax.ShapeDtypeStruct((8, n), F32),
        compiler_params=_params("parallel"),
        name="mod",
    )(cv, w, b)


def _rot_pairs(a, cos, sin_signed, half):
    lane = lax.broadcasted_iota(jnp.int32, a.shape, 1)
    first = (lane % (2 * half)) < half
    rot = jnp.where(first, pltpu.roll(a, LANES - half, 1), pltpu.roll(a, half, 1))
    return a * cos + rot * sin_signed


def _dup_halves(a):
    lane = lax.broadcasted_iota(jnp.int32, a.shape, 1)
    r = pltpu.roll(a, 64, 1)
    lo = lane < 64
    return jnp.where(lo, a, r), jnp.where(lo, r, a)


def _in_proj_kernel(x_ref, g_ref, sh_ref, sc_ref, w_ref, c1_ref, s1_ref, ca_ref, sa_ref,
                    o_ref, kd_ref, vd_ref, h_ref, acc_ref, *, rope):
    j = pl.program_id(1)

    @pl.when(j == 0)
    def _():
        xf = x_ref[...]
        y = xf * lax.rsqrt(jnp.mean(xf * xf, axis=-1, keepdims=True) + NORM_EPS)
        y = y * g_ref[...]
        h_ref[...] = (y * (1.0 + sc_ref[...]) + sh_ref[...]).astype(BF16)

    acc_ref[...] = jnp.dot(h_ref[...], w_ref[...], preferred_element_type=F32)

    def chunk(c):
        return acc_ref[:, c * LANES:(c + 1) * LANES]

    def put(c, a):
        o_ref[:, c * LANES:(c + 1) * LANES] = a.astype(BF16)

    def rope1(a):
        return _rot_pairs(a, c1_ref[...], s1_ref[...], 32) if rope else a

    def ropea(a):
        return _rot_pairs(a, ca_ref[...], sa_ref[...], 16) if rope else a

    @pl.when(j == 0)
    def _():
        for c in range(4):
            put(c, rope1(chunk(c)))

    @pl.when(j == 1)
    def _():
        for c in range(4):
            put(c, rope1(chunk(c)) * K_SCALE)

    @pl.when((j >= 2) & (j <= 5))
    def _():
        for c in range(4):
            put(c, chunk(c))

    @pl.when((j == 6) | (j == 7))
    def _():
        for c in range(4):
            put(c, ropea(chunk(c)))

    @pl.when(j == 8)
    def _():
        for c in range(2):
            k = ropea(chunk(c))
            put(c, k)
            k0, k1 = _dup_halves(k)
            kd_ref[:, (2 * c) * LANES:(2 * c + 1) * LANES] = k0.astype(BF16)
            kd_ref[:, (2 * c + 1) * LANES:(2 * c + 2) * LANES] = k1.astype(BF16)
            v = chunk(2 + c)
            put(2 + c, v)
            v0, v1 = _dup_halves(v)
            vd_ref[:, (2 * c) * LANES:(2 * c + 1) * LANES] = v0.astype(BF16)
            vd_ref[:, (2 * c + 1) * LANES:(2 * c + 2) * LANES] = v1.astype(BF16)


def _in_proj(x, g, sh, sc, w, tabs, *, rope, tm):
    m, d = x.shape
    n = w.shape[1]
    tn = 512
    assert n == 9 * tn and m % tm == 0
    c1, s1, ca, sa = tabs
    row = lambda i, j: (i, 0)
    vec = pl.BlockSpec((1, d), lambda i, j: (0, 0))
    tab = pl.BlockSpec((tm, LANES), row)
    return pl.pallas_call(
        functools.partial(_in_proj_kernel, rope=rope),
        grid=(m // tm, n // tn),
        in_specs=[pl.BlockSpec((tm, d), row), vec, vec, vec,
                  pl.BlockSpec((d, tn), lambda i, j: (0, j)),
                  tab, tab, tab, tab],
        out_specs=[pl.BlockSpec((tm, tn), lambda i, j: (i, j)),
                   pl.BlockSpec((tm, 512), row),
                   pl.BlockSpec((tm, 512), row)],
        out_shape=[jax.ShapeDtypeStruct((m, n), BF16),
                   jax.ShapeDtypeStruct((m, 512), BF16),
                   jax.ShapeDtypeStruct((m, 512), BF16)],
        scratch_shapes=[pltpu.VMEM((tm, d), BF16), pltpu.VMEM((tm, tn), F32)],
        compiler_params=_params("parallel", "arbitrary"),
        name="in_proj" if rope else "in_proj_ctx",
    )(x, g, sh, sc, w, c1, s1, ca, sa)


def _pair_lg(dec_ref, d, p, shape):
    lane = lax.broadcasted_iota(jnp.int32, shape, 1)
    first = (lane % LANES) < 64
    raw = jnp.where(first, jnp.full(shape, dec_ref[d, 2 * p], F32), jnp.full(shape, dec_ref[d, 2 * p + 1], F32))
    return -jnp.exp(raw)


def _head_block_mask(shape):
    r = lax.broadcasted_iota(jnp.int32, shape, 0)
    c = lax.broadcasted_iota(jnp.int32, shape, 1)
    return (r // 64) == (c // LANES)


def _kv_pair(k_pair, v_pair, w):
    kw = (k_pair.astype(F32) * w).astype(BF16)
    kv = lax.dot_general(kw, v_pair, (((0,), (0,)), ((), ())), preferred_element_type=F32)
    return jnp.where(_head_block_mask(kv.shape), kv, 0.0)


def _ret_state_kernel(dec_ref, kf_ref, vf_ref, kb_ref, vb_ref, ck_ref, cv_ref,
                      sf_ref, sb_ref, sfs, sbs):
    i = pl.program_id(0)
    C = RET_CHUNK
    lc = ck_ref.shape[0]

    @pl.when(i == 0)
    def _():
        pos = lax.broadcasted_iota(jnp.int32, (lc, LANES), 0).astype(F32)
        for p in range(RET_PAIRS):
            ks = slice(p * LANES, (p + 1) * LANES)
            vs = slice(p * 2 * RET_DV, (p + 1) * 2 * RET_DV)
            wf = jnp.exp(_pair_lg(dec_ref, 0, p, (lc, LANES)) * (lc - 1.0 - pos))
            wb = jnp.exp(_pair_lg(dec_ref, 1, p, (lc, LANES)) * pos)
            sfs[p] = _kv_pair(ck_ref[:, ks], cv_ref[:, vs], wf)
            sbs[p] = _kv_pair(ck_ref[:, ks], cv_ref[:, vs], wb)

    pos = lax.broadcasted_iota(jnp.int32, (C, LANES), 0).astype(F32)
    for p in range(RET_PAIRS):
        ks = slice(p * LANES, (p + 1) * LANES)
        vs = slice(p * 2 * RET_DV, (p + 1) * 2 * RET_DV)
        wf = jnp.exp(_pair_lg(dec_ref, 0, p, (C, LANES)) * (C - 1.0 - pos))
        wb = jnp.exp(_pair_lg(dec_ref, 1, p, (C, LANES)) * pos)
        rowh = lax.broadcasted_iota(jnp.int32, (LANES, 2 * RET_DV), 0) < 64
        gf = jnp.exp(-jnp.exp(jnp.where(rowh, jnp.full(rowh.shape, dec_ref[0, 2 * p], F32),
                                         jnp.full(rowh.shape, dec_ref[0, 2 * p + 1], F32))) * float(C))
        gb = jnp.exp(-jnp.exp(jnp.where(rowh, jnp.full(rowh.shape, dec_ref[1, 2 * p], F32),
                                         jnp.full(rowh.shape, dec_ref[1, 2 * p + 1], F32))) * float(C))
        sf = sfs[p]
        for cc in range(RET_STEP_CHUNKS):
            rs = slice(cc * C, (cc + 1) * C)
            sf_ref[cc, p] = sf.astype(BF16)
            sf = gf * sf + _kv_pair(kf_ref[rs, ks], vf_ref[rs, vs], wf)
        sfs[p] = sf
        sb = sbs[p]
        for cc in reversed(range(RET_STEP_CHUNKS)):
            rs = slice(cc * C, (cc + 1) * C)
            sb_ref[cc, p] = sb.astype(BF16)
            sb = gb * sb + _kv_pair(kb_ref[rs, ks], vb_ref[rs, vs], wb)
        sbs[p] = sb


def _ret_states(dec, proj, cproj):
    L = proj.shape[0]
    lc = cproj.shape[0]
    S = RET_STEP_CHUNKS
    R = S * RET_CHUNK
    n = L // R
    st = pl.BlockSpec((S, RET_PAIRS, LANES, 2 * RET_DV), lambda i: (i, 0, 0, 0))
    st_rev = pl.BlockSpec((S, RET_PAIRS, LANES, 2 * RET_DV), lambda i: (n - 1 - i, 0, 0, 0))
    shp = jax.ShapeDtypeStruct((n * S, RET_PAIRS, LANES, 2 * RET_DV), BF16)
    return pl.pallas_call(
        _ret_state_kernel,
        grid=(n,),
        in_specs=[pl.BlockSpec(memory_space=pltpu.SMEM),
                  pl.BlockSpec((R, 512), lambda i: (i, 1)),
                  pl.BlockSpec((R, 1024), lambda i: (i, 1)),
                  pl.BlockSpec((R, 512), lambda i: (n - 1 - i, 1)),
                  pl.BlockSpec((R, 1024), lambda i: (n - 1 - i, 1)),
                  pl.BlockSpec((lc, 512), lambda i: (0, 1)),
                  pl.BlockSpec((lc, 1024), lambda i: (0, 1))],
        out_specs=[st, st_rev],
        out_shape=[shp, shp],
        scratch_shapes=[pltpu.VMEM((RET_PAIRS, LANES, 2 * RET_DV), F32),
                        pltpu.VMEM((RET_PAIRS, LANES, 2 * RET_DV), F32)],
        compiler_params=_params("arbitrary"),
        name="ret_state",
    )(dec, proj, proj, proj, proj, cproj, cproj)


def _ret_out_kernel(dec_ref, q_ref, k_ref, v_ref, g_ref, sf_ref, sb_ref, o_ref):
    C = RET_CHUNK
    pos = lax.broadcasted_iota(jnp.int32, (C, LANES), 0).astype(F32)
    n_i = lax.broadcasted_iota(jnp.int32, (C, 2 * C), 0)
    m_i = lax.broadcasted_iota(jnp.int32, (C, 2 * C), 1) % C
    rel = (n_i - m_i).astype(F32)
    lane = lax.broadcasted_iota(jnp.int32, (C, LANES), 1)
    lo = lane < 64
    for p in range(RET_PAIRS):
        ks = slice(p * LANES, (p + 1) * LANES)
        vs = slice(p * 2 * RET_DV, (p + 1) * 2 * RET_DV)
        col_a = lax.broadcasted_iota(jnp.int32, (C, 2 * C), 1) < C
        raw_f = jnp.where(col_a, jnp.full((C, 2 * C), dec_ref[0, 2 * p], F32), jnp.full((C, 2 * C), dec_ref[0, 2 * p + 1], F32))
        raw_b = jnp.where(col_a, jnp.full((C, 2 * C), dec_ref[1, 2 * p], F32), jnp.full((C, 2 * C), dec_ref[1, 2 * p + 1], F32))
        dmat = jnp.where(rel >= 0, jnp.exp(-jnp.exp(raw_f) * jnp.maximum(rel, 0.0)),
                         jnp.exp(-jnp.exp(raw_b) * jnp.maximum(-rel, 0.0)))
        wqf = jnp.exp(_pair_lg(dec_ref, 0, p, (C, LANES)) * (pos + 1.0))
        wqb = jnp.exp(_pair_lg(dec_ref, 1, p, (C, LANES)) * (float(C) - pos))
        for cc in range(RET_STEP_CHUNKS):
            rs = slice(cc * C, (cc + 1) * C)
            q = q_ref[rs, ks]
            k = k_ref[rs, ks]
            v = v_ref[rs, vs]
            zk = jnp.zeros_like(k)
            kst = jnp.concatenate([jnp.where(lo, k, zk), jnp.where(lo, zk, k)], axis=0)
            s = lax.dot_general(q, kst, (((1,), (1,)), ((), ())), preferred_element_type=F32)
            sd = (s * dmat).astype(BF16)
            qf32 = q.astype(F32)
            qwf = (qf32 * wqf).astype(BF16)
            qwb = (qf32 * wqb).astype(BF16)
            zv = jnp.zeros((C, RET_DV), BF16)
            vbd = jnp.concatenate([jnp.concatenate([v[:, :RET_DV], zv], axis=1),
                                   jnp.concatenate([zv, v[:, RET_DV:]], axis=1)], axis=0)
            lhs = jnp.concatenate([sd, qwf, qwb], axis=1)
            rhs = jnp.concatenate([vbd, sf_ref[cc, p], sb_ref[cc, p]], axis=0)
            o = jnp.dot(lhs, rhs, preferred_element_type=F32)
            for t in range(2):
                oh = o[:, t * RET_DV:(t + 1) * RET_DV]
                oh = oh * lax.rsqrt(jnp.mean(oh * oh, axis=-1, keepdims=True) + NORM_EPS)
                cs = slice(p * 2 * RET_DV + t * RET_DV, p * 2 * RET_DV + (t + 1) * RET_DV)
                gt = g_ref[rs, cs].astype(F32)
                o_ref[rs, cs] = (oh * (gt / (1.0 + jnp.exp(-gt)))).astype(BF16)


def _ret_out(dec, proj, sf, sb):
    L = proj.shape[0]
    S = RET_STEP_CHUNKS
    R = S * RET_CHUNK
    n = L // R
    st = pl.BlockSpec((S, RET_PAIRS, LANES, 2 * RET_DV), lambda i: (i, 0, 0, 0))
    return pl.pallas_call(
        _ret_out_kernel,
        grid=(n,),
        in_specs=[pl.BlockSpec(memory_space=pltpu.SMEM),
                  pl.BlockSpec((R, 512), lambda i: (i, 0)),
                  pl.BlockSpec((R, 512), lambda i: (i, 1)),
                  pl.BlockSpec((R, 1024), lambda i: (i, 1)),
                  pl.BlockSpec((R, 1024), lambda i: (i, 2)),
                  st, st],
        out_specs=pl.BlockSpec((R, RET_HEADS * RET_DV), lambda i: (i, 0)),
        out_shape=jax.ShapeDtypeStruct((L, RET_HEADS * RET_DV), BF16),
        compiler_params=_params("parallel"),
        name="ret_out",
    )(dec, proj, proj, proj, proj, sf, sb)


def _attn_kernel(sink_ref, q_ref, kp_ref, kc_ref, kn_ref, vp_ref, vc_ref, vn_ref, ck_ref, cv_ref, o_ref):
    n = pl.program_id(0)
    nblk = pl.num_programs(0)
    B = ATT_BLOCK
    lc = ck_ref.shape[0]
    kj = lax.broadcasted_iota(jnp.int32, (3 * B, B), 0)
    qi = lax.broadcasted_iota(jnp.int32, (3 * B, B), 1)
    ok_prev = jnp.where(n > 0, 0.0, MASK_NEG).astype(F32)
    ok_next = jnp.where(n < nblk - 1, 0.0, MASK_NEG).astype(F32)
    bias = jnp.where(kj < B, jnp.where(kj >= qi, ok_prev, MASK_NEG),
                     jnp.where(kj < 2 * B, 0.0, jnp.where(kj - 2 * B <= qi, ok_next, MASK_NEG))).astype(F32)
    bias = jnp.concatenate([bias, jnp.zeros((lc, B), F32)], axis=0)
    bias4 = jnp.concatenate([bias] * ATT_GROUP, axis=1)
    lane = lax.broadcasted_iota(jnp.int32, (B, LANES), 1)
    lo = lane < 64
    hi = lane >= 64
    for g in range(ATT_KV_HEADS):
        gs = slice(g * LANES, (g + 1) * LANES)
        kcat = jnp.concatenate([kp_ref[:, gs], kc_ref[:, gs], kn_ref[:, gs], ck_ref[:, gs]], axis=0)
        vcat = jnp.concatenate([vp_ref[:, gs], vc_ref[:, gs], vn_ref[:, gs], cv_ref[:, gs]], axis=0)
        qs, sinks = [], []
        for r in range(ATT_GROUP):
            h = ATT_GROUP * g + r
            qt = q_ref[:, (h // 2) * LANES:(h // 2 + 1) * LANES]
            keep = lo if h % 2 == 0 else hi
            qs.append(jnp.where(keep, qt * jnp.asarray(ATT_SCALE, BF16), jnp.zeros_like(qt)))
            sinks.append(jnp.full((1, B), sink_ref[h], F32))
        q4 = jnp.concatenate(qs, axis=0)
        sk = jnp.concatenate(sinks, axis=1)
        s = lax.dot_general(kcat, q4, (((1,), (1,)), ((), ())), preferred_element_type=F32)
        s = s + bias4
        m = jnp.maximum(jnp.max(s, axis=0, keepdims=True), sk)
        e = jnp.exp(s - m)
        den = jnp.sum(e, axis=0, keepdims=True) + jnp.exp(sk - m)
        res = lax.dot_general(vcat, e.astype(BF16), (((0,), (0,)), ((), ())), preferred_element_type=F32) / den
        for t in range(2):
            even = res[:, (2 * t) * B:(2 * t + 1) * B].T
            odd = res[:, (2 * t + 1) * B:(2 * t + 2) * B].T
            c0 = (2 * g + t) * LANES
            o_ref[:, c0:c0 + LANES] = jnp.where(lo, even, odd).astype(BF16)


def _attn(sink, proj, kd, vd, ckd, cvd):
    L = proj.shape[0]
    B = ATT_BLOCK
    n = L // B
    lc = ckd.shape[0]
    prev = lambda i: (jnp.maximum(i - 1, 0), 0)
    cur = lambda i: (i, 0)
    nxt = lambda i: (jnp.minimum(i + 1, n - 1), 0)
    kv = lambda f: pl.BlockSpec((B, 512), f)
    full = pl.BlockSpec((lc, 512), lambda i: (0, 0))
    return pl.pallas_call(
        _attn_kernel,
        grid=(n,),
        in_specs=[pl.BlockSpec(memory_space=pltpu.SMEM),
                  pl.BlockSpec((B, 1024), lambda i: (i, 3)),
                  kv(prev), kv(cur), kv(nxt), kv(prev), kv(cur), kv(nxt), full, full],
        out_specs=pl.BlockSpec((B, ATT_HEADS * ATT_DH), cur),
        out_shape=jax.ShapeDtypeStruct((L, ATT_HEADS * ATT_DH), BF16),
        compiler_params=_params("parallel"),
        name="attn",
    )(sink, proj, kd, kd, kd, vd, vd, vd, ckd, cvd)


def _out_proj_kernel(yr_ref, ya_ref, w_ref, x_ref, gt_ref, g_ref, sh_ref, sc_ref, o_ref, h_ref):
    kr = yr_ref.shape[1]
    for r in range(yr_ref.shape[0] // OUT_ROW_CHUNK):
        rs = slice(r * OUT_ROW_CHUNK, (r + 1) * OUT_ROW_CHUNK)
        acc = jnp.dot(yr_ref[rs, :], w_ref[:kr, :], preferred_element_type=F32)
        acc = acc + jnp.dot(ya_ref[rs, :], w_ref[kr:, :], preferred_element_type=F32)
        x1 = x_ref[rs, :] + gt_ref[...] * acc
        o_ref[rs, :] = x1
        y = x1 * lax.rsqrt(jnp.mean(x1 * x1, axis=-1, keepdims=True) + NORM_EPS)
        y = y * g_ref[...]
        h_ref[rs, :] = (y * (1.0 + sc_ref[...]) + sh_ref[...]).astype(BF16)


def _out_proj(yr, ya, w, x, gt, g, sh, sc, *, tm):
    m, d = x.shape
    kr, ka = yr.shape[1], ya.shape[1]
    row = lambda i: (i, 0)
    vec = pl.BlockSpec((1, d), lambda i: (0, 0))
    return pl.pallas_call(
        _out_proj_kernel,
        grid=(m // tm,),
        in_specs=[pl.BlockSpec((tm, kr), row), pl.BlockSpec((tm, ka), row),
                  pl.BlockSpec((kr + ka, d), lambda i: (0, 0)),
                  pl.BlockSpec((tm, d), row), vec, vec, vec, vec],
        out_specs=[pl.BlockSpec((tm, d), row), pl.BlockSpec((tm, d), row)],
        out_shape=[jax.ShapeDtypeStruct((m, d), F32), jax.ShapeDtypeStruct((m, d), BF16)],
        compiler_params=_params("parallel"),
        name="out_proj",
    )(yr, ya, w, x, gt, g, sh, sc)


def _ffn_kernel(h_ref, x_ref, gt_ref, gfin_ref, wg_ref, wu_ref, wd_ref, o_ref):
    f = pl.program_id(1)

    @pl.when(f == 0)
    def _():
        o_ref[...] = jnp.zeros_like(o_ref)

    h = h_ref[...]
    a = jnp.dot(h, wg_ref[...], preferred_element_type=F32)
    u = jnp.dot(h, wu_ref[...], preferred_element_type=F32)
    act = ((a / (1.0 + jnp.exp(-a))) * u).astype(BF16)
    o_ref[...] += jnp.dot(act, wd_ref[...], preferred_element_type=F32)

    @pl.when(f == pl.num_programs(1) - 1)
    def _():
        y = x_ref[...] + gt_ref[...] * o_ref[...]
        y = y * lax.rsqrt(jnp.mean(y * y, axis=-1, keepdims=True) + NORM_EPS)
        o_ref[...] = y * gfin_ref[...]


def _ffn(h, x, gt, gfin, wg, wu, wd, *, tm, tf):
    m, d = x.shape
    ff = wg.shape[1]
    assert ff % tf == 0 and m % tm == 0
    row = lambda i, f: (i, 0)
    vec = pl.BlockSpec((1, d), lambda i, f: (0, 0))
    return pl.pallas_call(
        _ffn_kernel,
        grid=(m // tm, ff // tf),
        in_specs=[pl.BlockSpec((tm, d), row), pl.BlockSpec((tm, d), row), vec, vec,
                  pl.BlockSpec((d, tf), lambda i, f: (0, f)),
                  pl.BlockSpec((d, tf), lambda i, f: (0, f)),
                  pl.BlockSpec((tf, d), lambda i, f: (f, 0))],
        out_specs=pl.BlockSpec((tm, d), row),
        out_shape=jax.ShapeDtypeStruct((m, d), F32),
        compiler_params=_params("parallel", "arbitrary"),
        name="ffn",
    )(h, x, gt, gfin, wg, wu, wd)


def _rope_tables(L):
    lane = jnp.arange(LANES)
    inv1 = ROPE_BASE ** (-jnp.arange(32, dtype=F32) / 32)
    ang1 = jnp.arange(L, dtype=F32)[:, None] * inv1[None, :]
    sgn1 = jnp.where((lane % 64) < 32, -1.0, 1.0).astype(F32)
    cos1 = jnp.tile(jnp.cos(ang1), (1, LANES // 32))
    sin1 = jnp.tile(jnp.sin(ang1), (1, LANES // 32)) * sgn1[None, :]
    inv2 = ROPE_BASE ** (-jnp.arange(16, dtype=F32) / 16)
    nrow = L // GRID_W
    ang_r = jnp.arange(nrow, dtype=F32)[:, None] * inv2[None, :]
    ang_c = jnp.arange(GRID_W, dtype=F32)[:, None] * inv2[None, :]
    sgna = jnp.where((lane % 32) < 16, -1.0, 1.0).astype(F32)

    def expand(fr, fc):
        by_row = jnp.broadcast_to(jnp.tile(fr, (1, 2))[:, None, :], (nrow, GRID_W, 32))
        by_col = jnp.broadcast_to(jnp.tile(fc, (1, 2))[None, :, :], (nrow, GRID_W, 32))
        head = jnp.concatenate([by_row, by_col], axis=-1).reshape(L, 64)
        return jnp.tile(head, (1, LANES // 64))

    cosa = expand(jnp.cos(ang_r), jnp.cos(ang_c))
    sina = expand(jnp.sin(ang_r), jnp.sin(ang_c)) * sgna[None, :]
    return cos1, sin1, cosa, sina


def kernel(x, c, ctx, c_ctx, w_mod, b_mod, norm_mix, norm_ffn, w_in, ret_decay, attn_sink,
           w_out, w_gate, w_up, w_down, norm_final):
    B, L, D = x.shape
    assert B == 1 and w_mod.shape[0] == 1, "single batch element, depth-1 layer"
    lc = ctx.shape[1]
    x2 = x[0]
    xc2 = ctx[0]

    cv = jnp.zeros((8, D), F32).at[0].set(c[0]).at[1].set(c_ctx)
    mod = _mod(cv, w_mod[0], b_mod[0][None, :])
    sh_m, sc_m, gt_m, sh_f, sc_f, gt_f = [mod[0:1, k * D:(k + 1) * D] for k in range(6)]
    sh_mc, sc_mc = mod[1:2, 0:D], mod[1:2, D:2 * D]

    w_in_b = w_in[0].astype(BF16)
    g_mix = norm_mix[0][None, :]
    tabs = _rope_tables(L)
    ctabs = tuple(tb[:lc] for tb in tabs)
    proj, kd, vd = _in_proj(x2, g_mix, sh_m, sc_m, w_in_b, tabs, rope=True, tm=1024)
    cproj, ckd, cvd = _in_proj(xc2, g_mix, sh_mc, sc_mc, w_in_b, ctabs, rope=False, tm=lc)

    dec = ret_decay[0].astype(F32)
    sf, sb = _ret_states(dec, proj, cproj)
    y_ret = _ret_out(dec, proj, sf, sb)
    y_att = _attn(attn_sink[0].astype(F32), proj, kd, vd, ckd, cvd)

    x1, hff = _out_proj(y_ret, y_att, w_out[0].astype(BF16), x2, gt_m,
                        norm_ffn[0][None, :], sh_f, sc_f, tm=512)
    out = _ffn(hff, x1, gt_f, norm_final[None, :],
               w_gate[0].astype(BF16), w_up[0].astype(BF16), w_down[0].astype(BF16), tm=512, tf=512)
    return out[None]
```

```python
import functools

import jax
import jax.numpy as jnp
from jax import lax
from jax.experimental import pallas as pl
from jax.experimental.pallas import tpu as pltpu

GRID_W = 64
RET_HEADS = 8
RET_DK = 64
RET_DV = 128
RET_CHUNK = 128
ATT_HEADS = 16
ATT_KV_HEADS = 4
ATT_DH = 64
ATT_GROUP = ATT_HEADS // ATT_KV_HEADS
WINDOW = 128
ATT_BLOCK = 128
ROPE_BASE = 10000.0
NORM_EPS = 1e-6
K_SCALE = RET_DK ** -0.5
ATT_SCALE = ATT_DH ** -0.5

LANES = 128
RET_PAIRS = RET_HEADS // 2
MASK_NEG = -1e30
VMEM_LIMIT = 56 * 1024 * 1024
RET_STEP_CHUNKS = 4
OUT_ROW_CHUNK = 256
IN_ROW_CHUNK = 256

BF16 = jnp.bfloat16
F32 = jnp.float32


def _params(*sem):
    return pltpu.CompilerParams(dimension_semantics=sem, vmem_limit_bytes=VMEM_LIMIT)


def _mod_kernel(cv_ref, w_ref, b_ref, o_ref):
    cv = cv_ref[...]
    s = cv / (1.0 + jnp.exp(-cv))
    o_ref[...] = jnp.dot(s.astype(BF16), w_ref[...].astype(BF16),
                         preferred_element_type=F32) + b_ref[...]


def _mod(cv, w, b):
    d, n = w.shape
    tn = 1024
    return pl.pallas_call(
        _mod_kernel,
        grid=(n // tn,),
        in_specs=[pl.BlockSpec((8, d), lambda j: (0, 0)),
                  pl.BlockSpec((d, tn), lambda j: (0, j)),
                  pl.BlockSpec((1, tn), lambda j: (0, j))],
        out_specs=pl.BlockSpec((8, tn), lambda j: (0, j)),
        out_shape=jax.ShapeDtypeStruct((8, n), F32),
        compiler_params=_params("parallel"),
        name="mod",
    )(cv, w, b)


def _rot_pairs(a, cos, sin_signed, half):
    lane = lax.broadcasted_iota(jnp.int32, a.shape, 1)
    first = (lane % (2 * half)) < half
    rot = jnp.where(first, pltpu.roll(a, LANES - half, 1), pltpu.roll(a, half, 1))
    return a * cos + rot * sin_signed


def _dup_halves(a):
    lane = lax.broadcasted_iota(jnp.int32, a.shape, 1)
    r = pltpu.roll(a, 64, 1)
    lo = lane < 64
    return jnp.where(lo, a, r), jnp.where(lo, r, a)


_PROJ_TILE = 512
_PROJ_TILE_KINDS = ("ret_q", "ret_k", "plain", "plain", "plain", "plain", "att_q", "att_q", "att_kv")


def _in_proj_kernel(x_ref, g_ref, sh_ref, sc_ref, w_ref, c1_ref, s1_ref, ca_ref, sa_ref,
                    o_ref, kd_ref, vd_ref, *, rope):
    tn = _PROJ_TILE
    for r in range(x_ref.shape[0] // IN_ROW_CHUNK):
        rs = slice(r * IN_ROW_CHUNK, (r + 1) * IN_ROW_CHUNK)
        xf = x_ref[rs, :]
        y = xf * lax.rsqrt(jnp.mean(xf * xf, axis=-1, keepdims=True) + NORM_EPS)
        y = y * g_ref[...]
        h = (y * (1.0 + sc_ref[...]) + sh_ref[...]).astype(BF16)

        def rope1(a):
            return _rot_pairs(a, c1_ref[rs, :], s1_ref[rs, :], 32) if rope else a

        def ropea(a):
            return _rot_pairs(a, ca_ref[rs, :], sa_ref[rs, :], 16) if rope else a

        for j, kind in enumerate(_PROJ_TILE_KINDS):
            acc = jnp.dot(h, w_ref[:, j * tn:(j + 1) * tn], preferred_element_type=F32)
            for c in range(tn // LANES):
                a = acc[:, c * LANES:(c + 1) * LANES]
                if kind == "ret_q":
                    a = rope1(a)
                elif kind == "ret_k":
                    a = rope1(a) * K_SCALE
                elif kind == "att_q" or (kind == "att_kv" and c < 2):
                    a = ropea(a)
                o_ref[rs, j * tn + c * LANES:j * tn + (c + 1) * LANES] = a.astype(BF16)
                if kind == "att_kv":
                    dup_ref = kd_ref if c < 2 else vd_ref
                    d0, d1 = _dup_halves(a)
                    t = 2 * (c % 2)
                    dup_ref[rs, t * LANES:(t + 1) * LANES] = d0.astype(BF16)
                    dup_ref[rs, (t + 1) * LANES:(t + 2) * LANES] = d1.astype(BF16)


def _in_proj(x, g, sh, sc, w, tabs, *, rope, tm):
    m, d = x.shape
    n = w.shape[1]
    assert n == _PROJ_TILE * len(_PROJ_TILE_KINDS) and m % tm == 0 and tm % IN_ROW_CHUNK == 0
    c1, s1, ca, sa = tabs
    row = lambda i: (i, 0)
    vec = pl.BlockSpec((1, d), lambda i: (0, 0))
    tab = pl.BlockSpec((tm, LANES), row)
    return pl.pallas_call(
        functools.partial(_in_proj_kernel, rope=rope),
        grid=(m // tm,),
        in_specs=[pl.BlockSpec((tm, d), row), vec, vec, vec,
                  pl.BlockSpec((d, n), lambda i: (0, 0), pipeline_mode=pl.Buffered(1)),
                  tab, tab, tab, tab],
        out_specs=[pl.BlockSpec((tm, n), row),
                   pl.BlockSpec((tm, 512), row),
                   pl.BlockSpec((tm, 512), row)],
        out_shape=[jax.ShapeDtypeStruct((m, n), BF16),
                   jax.ShapeDtypeStruct((m, 512), BF16),
                   jax.ShapeDtypeStruct((m, 512), BF16)],
        compiler_params=_params("parallel"),
        name="in_proj" if rope else "in_proj_ctx",
    )(x, g, sh, sc, w, c1, s1, ca, sa)


def _pair_lg(dec_ref, d, p, shape):
    lane = lax.broadcasted_iota(jnp.int32, shape, 1)
    first = (lane % LANES) < 64
    raw = jnp.where(first, jnp.full(shape, dec_ref[d, 2 * p], F32), jnp.full(shape, dec_ref[d, 2 * p + 1], F32))
    return -jnp.exp(raw)


def _head_block_mask(shape):
    r = lax.broadcasted_iota(jnp.int32, shape, 0)
    c = lax.broadcasted_iota(jnp.int32, shape, 1)
    return (r // 64) == (c // LANES)


def _kv_pair(k_pair, v_pair, w):
    kw = (k_pair.astype(F32) * w).astype(BF16)
    kv = lax.dot_general(kw, v_pair, (((0,), (0,)), ((), ())), preferred_element_type=F32)
    return jnp.where(_head_block_mask(kv.shape), kv, 0.0)


def _ret_state_kernel(dec_ref, kf_ref, vf_ref, kb_ref, vb_ref, ck_ref, cv_ref,
                      sf_ref, sb_ref, sfs, sbs):
    i = pl.program_id(0)
    C = RET_CHUNK
    lc = ck_ref.shape[0]

    @pl.when(i == 0)
    def _():
        pos = lax.broadcasted_iota(jnp.int32, (lc, LANES), 0).astype(F32)
        for p in range(RET_PAIRS):
            ks = slice(p * LANES, (p + 1) * LANES)
            vs = slice(p * 2 * RET_DV, (p + 1) * 2 * RET_DV)
            wf = jnp.exp(_pair_lg(dec_ref, 0, p, (lc, LANES)) * (lc - 1.0 - pos))
            wb = jnp.exp(_pair_lg(dec_ref, 1, p, (lc, LANES)) * pos)
            sfs[p] = _kv_pair(ck_ref[:, ks], cv_ref[:, vs], wf)
            sbs[p] = _kv_pair(ck_ref[:, ks], cv_ref[:, vs], wb)

    pos = lax.broadcasted_iota(jnp.int32, (C, LANES), 0).astype(F32)
    for p in range(RET_PAIRS):
        ks = slice(p * LANES, (p + 1) * LANES)
        vs = slice(p * 2 * RET_DV, (p + 1) * 2 * RET_DV)
        wf = jnp.exp(_pair_lg(dec_ref, 0, p, (C, LANES)) * (C - 1.0 - pos))
        wb = jnp.exp(_pair_lg(dec_ref, 1, p, (C, LANES)) * pos)
        rowh = lax.broadcasted_iota(jnp.int32, (LANES, 2 * RET_DV), 0) < 64
        gf = jnp.exp(-jnp.exp(jnp.where(rowh, jnp.full(rowh.shape, dec_ref[0, 2 * p], F32),
                                         jnp.full(rowh.shape, dec_ref[0, 2 * p + 1], F32))) * float(C))
        gb = jnp.exp(-jnp.exp(jnp.where(rowh, jnp.full(rowh.shape, dec_ref[1, 2 * p], F32),
                                         jnp.full(rowh.shape, dec_ref[1, 2 * p + 1], F32))) * float(C))
        sf = sfs[p]
        for cc in range(RET_STEP_CHUNKS):
            rs = slice(cc * C, (cc + 1) * C)
            sf_ref[cc, p] = sf.astype(BF16)
            sf = gf * sf + _kv_pair(kf_ref[rs, ks], vf_ref[rs, vs], wf)
        sfs[p] = sf
        sb = sbs[p]
        for cc in reversed(range(RET_STEP_CHUNKS)):
            rs = slice(cc * C, (cc + 1) * C)
            sb_ref[cc, p] = sb.astype(BF16)
            sb = gb * sb + _kv_pair(kb_ref[rs, ks], vb_ref[rs, vs], wb)
        sbs[p] = sb


def _ret_states(dec, proj, cproj):
    L = proj.shape[0]
    lc = cproj.shape[0]
    S = RET_STEP_CHUNKS
    R = S * RET_CHUNK
    n = L // R
    st = pl.BlockSpec((S, RET_PAIRS, LANES, 2 * RET_DV), lambda i: (i, 0, 0, 0))
    st_rev = pl.BlockSpec((S, RET_PAIRS, LANES, 2 * RET_DV), lambda i: (n - 1 - i, 0, 0, 0))
    shp = jax.ShapeDtypeStruct((n * S, RET_PAIRS, LANES, 2 * RET_DV), BF16)
    return pl.pallas_call(
        _ret_state_kernel,
        grid=(n,),
        in_specs=[pl.BlockSpec(memory_space=pltpu.SMEM),
                  pl.BlockSpec((R, 512), lambda i: (i, 1)),
                  pl.BlockSpec((R, 1024), lambda i: (i, 1)),
                  pl.BlockSpec((R, 512), lambda i: (n - 1 - i, 1)),
                  pl.BlockSpec((R, 1024), lambda i: (n - 1 - i, 1)),
                  pl.BlockSpec((lc, 512), lambda i: (0, 1)),
                  pl.BlockSpec((lc, 1024), lambda i: (0, 1))],
        out_specs=[st, st_rev],
        out_shape=[shp, shp],
        scratch_shapes=[pltpu.VMEM((RET_PAIRS, LANES, 2 * RET_DV), F32),
                        pltpu.VMEM((RET_PAIRS, LANES, 2 * RET_DV), F32)],
        compiler_params=_params("arbitrary"),
        name="ret_state",
    )(dec, proj, proj, proj, proj, cproj, cproj)


def _ret_out_kernel(dec_ref, q_ref, k_ref, v_ref, g_ref, sf_ref, sb_ref, o_ref):
    C = RET_CHUNK
    pos = lax.broadcasted_iota(jnp.int32, (C, LANES), 0).astype(F32)
    n_i = lax.broadcasted_iota(jnp.int32, (C, 2 * C), 0)
    m_i = lax.broadcasted_iota(jnp.int32, (C, 2 * C), 1) % C
    rel = (n_i - m_i).astype(F32)
    lane = lax.broadcasted_iota(jnp.int32, (C, LANES), 1)
    lo = lane < 64
    for p in range(RET_PAIRS):
        ks = slice(p * LANES, (p + 1) * LANES)
        vs = slice(p * 2 * RET_DV, (p + 1) * 2 * RET_DV)
        col_a = lax.broadcasted_iota(jnp.int32, (C, 2 * C), 1) < C
        raw_f = jnp.where(col_a, jnp.full((C, 2 * C), dec_ref[0, 2 * p], F32), jnp.full((C, 2 * C), dec_ref[0, 2 * p + 1], F32))
        raw_b = jnp.where(col_a, jnp.full((C, 2 * C), dec_ref[1, 2 * p], F32), jnp.full((C, 2 * C), dec_ref[1, 2 * p + 1], F32))
        dmat = jnp.where(rel >= 0, jnp.exp(-jnp.exp(raw_f) * jnp.maximum(rel, 0.0)),
                         jnp.exp(-jnp.exp(raw_b) * jnp.maximum(-rel, 0.0)))
        wqf = jnp.exp(_pair_lg(dec_ref, 0, p, (C, LANES)) * (pos + 1.0))
        wqb = jnp.exp(_pair_lg(dec_ref, 1, p, (C, LANES)) * (float(C) - pos))
        for cc in range(RET_STEP_CHUNKS):
            rs = slice(cc * C, (cc + 1) * C)
            q = q_ref[rs, ks]
            k = k_ref[rs, ks]
            v = v_ref[rs, vs]
            zk = jnp.zeros_like(k)
            kst = jnp.concatenate([jnp.where(lo, k, zk), jnp.where(lo, zk, k)], axis=0)
            s = lax.dot_general(q, kst, (((1,), (1,)), ((), ())), preferred_element_type=F32)
            sd = (s * dmat).astype(BF16)
            qf32 = q.astype(F32)
            qwf = (qf32 * wqf).astype(BF16)
            qwb = (qf32 * wqb).astype(BF16)
            zv = jnp.zeros((C, RET_DV), BF16)
            vbd = jnp.concatenate([jnp.concatenate([v[:, :RET_DV], zv], axis=1),
                                   jnp.concatenate([zv, v[:, RET_DV:]], axis=1)], axis=0)
            lhs = jnp.concatenate([sd, qwf, qwb], axis=1)
            rhs = jnp.concatenate([vbd, sf_ref[cc, p], sb_ref[cc, p]], axis=0)
            o = jnp.dot(lhs, rhs, preferred_element_type=F32)
            for t in range(2):
                oh = o[:, t * RET_DV:(t + 1) * RET_DV]
                oh = oh * lax.rsqrt(jnp.mean(oh * oh, axis=-1, keepdims=True) + NORM_EPS)
                cs = slice(p * 2 * RET_DV + t * RET_DV, p * 2 * RET_DV + (t + 1) * RET_DV)
                gt = g_ref[rs, cs].astype(F32)
                o_ref[rs, cs] = (oh * (gt / (1.0 + jnp.exp(-gt)))).astype(BF16)


def _ret_out(dec, proj, sf, sb):
    L = proj.shape[0]
    S = RET_STEP_CHUNKS
    R = S * RET_CHUNK
    n = L // R
    st = pl.BlockSpec((S, RET_PAIRS, LANES, 2 * RET_DV), lambda i: (i, 0, 0, 0))
    return pl.pallas_call(
        _ret_out_kernel,
        grid=(n,),
        in_specs=[pl.BlockSpec(memory_space=pltpu.SMEM),
                  pl.BlockSpec((R, 512), lambda i: (i, 0)),
                  pl.BlockSpec((R, 512), lambda i: (i, 1)),
                  pl.BlockSpec((R, 1024), lambda i: (i, 1)),
                  pl.BlockSpec((R, 1024), lambda i: (i, 2)),
                  st, st],
        out_specs=pl.BlockSpec((R, RET_HEADS * RET_DV), lambda i: (i, 0)),
        out_shape=jax.ShapeDtypeStruct((L, RET_HEADS * RET_DV), BF16),
        compiler_params=_params("parallel"),
        name="ret_out",
    )(dec, proj, proj, proj, proj, sf, sb)


def _attn_kernel(sink_ref, q_ref, kp_ref, kc_ref, kn_ref, vp_ref, vc_ref, vn_ref, ck_ref, cv_ref, o_ref):
    n = pl.program_id(0)
    nblk = pl.num_programs(0)
    B = ATT_BLOCK
    lc = ck_ref.shape[0]
    kj = lax.broadcasted_iota(jnp.int32, (3 * B, B), 0)
    qi = lax.broadcasted_iota(jnp.int32, (3 * B, B), 1)
    ok_prev = jnp.where(n > 0, 0.0, MASK_NEG).astype(F32)
    ok_next = jnp.where(n < nblk - 1, 0.0, MASK_NEG).astype(F32)
    bias = jnp.where(kj < B, jnp.where(kj >= qi, ok_prev, MASK_NEG),
                     jnp.where(kj < 2 * B, 0.0, jnp.where(kj - 2 * B <= qi, ok_next, MASK_NEG))).astype(F32)
    bias = jnp.concatenate([bias, jnp.zeros((lc, B), F32)], axis=0)
    bias4 = jnp.concatenate([bias] * ATT_GROUP, axis=1)
    lane = lax.broadcasted_iota(jnp.int32, (B, LANES), 1)
    lo = lane < 64
    hi = lane >= 64
    for g in range(ATT_KV_HEADS):
        gs = slice(g * LANES, (g + 1) * LANES)
        kcat = jnp.concatenate([kp_ref[:, gs], kc_ref[:, gs], kn_ref[:, gs], ck_ref[:, gs]], axis=0)
        vcat = jnp.concatenate([vp_ref[:, gs], vc_ref[:, gs], vn_ref[:, gs], cv_ref[:, gs]], axis=0)
        qs, sinks = [], []
        for r in range(ATT_GROUP):
            h = ATT_GROUP * g + r
            qt = q_ref[:, (h // 2) * LANES:(h // 2 + 1) * LANES]
            keep = lo if h % 2 == 0 else hi
            qs.append(jnp.where(keep, qt * jnp.asarray(ATT_SCALE, BF16), jnp.zeros_like(qt)))
            sinks.append(jnp.full((1, B), sink_ref[h], F32))
        q4 = jnp.concatenate(qs, axis=0)
        sk = jnp.concatenate(sinks, axis=1)
        s = lax.dot_general(kcat, q4, (((1,), (1,)), ((), ())), preferred_element_type=F32)
        s = s + bias4
        m = jnp.maximum(jnp.max(s, axis=0, keepdims=True), sk)
        e = jnp.exp(s - m)
        den = jnp.sum(e, axis=0, keepdims=True) + jnp.exp(sk - m)
        res = lax.dot_general(vcat, e.astype(BF16), (((0,), (0,)), ((), ())), preferred_element_type=F32) / den
        for t in range(2):
            even = res[:, (2 * t) * B:(2 * t + 1) * B].T
            odd = res[:, (2 * t + 1) * B:(2 * t + 2) * B].T
            c0 = (2 * g + t) * LANES
            o_ref[:, c0:c0 + LANES] = jnp.where(lo, even, odd).astype(BF16)


def _attn(sink, proj, kd, vd, ckd, cvd):
    L = proj.shape[0]
    B = ATT_BLOCK
    n = L // B
    lc = ckd.shape[0]
    prev = lambda i: (jnp.maximum(i - 1, 0), 0)
    cur = lambda i: (i, 0)
    nxt = lambda i: (jnp.minimum(i + 1, n - 1), 0)
    kv = lambda f: pl.BlockSpec((B, 512), f)
    full = pl.BlockSpec((lc, 512), lambda i: (0, 0))
    return pl.pallas_call(
        _attn_kernel,
        grid=(n,),
        in_specs=[pl.BlockSpec(memory_space=pltpu.SMEM),
                  pl.BlockSpec((B, 1024), lambda i: (i, 3)),
                  kv(prev), kv(cur), kv(nxt), kv(prev), kv(cur), kv(nxt), full, full],
        out_specs=pl.BlockSpec((B, ATT_HEADS * ATT_DH), cur),
        out_shape=jax.ShapeDtypeStruct((L, ATT_HEADS * ATT_DH), BF16),
        compiler_params=_params("parallel"),
        name="attn",
    )(sink, proj, kd, kd, kd, vd, vd, vd, ckd, cvd)


def _out_proj_kernel(yr_ref, ya_ref, w_ref, x_ref, gt_ref, g_ref, sh_ref, sc_ref, o_ref, h_ref):
    kr = yr_ref.shape[1]
    for r in range(yr_ref.shape[0] // OUT_ROW_CHUNK):
        rs = slice(r * OUT_ROW_CHUNK, (r + 1) * OUT_ROW_CHUNK)
        acc = jnp.dot(yr_ref[rs, :], w_ref[:kr, :], preferred_element_type=F32)
        acc = acc + jnp.dot(ya_ref[rs, :], w_ref[kr:, :], preferred_element_type=F32)
        x1 = x_ref[rs, :] + gt_ref[...] * acc
        o_ref[rs, :] = x1
        y = x1 * lax.rsqrt(jnp.mean(x1 * x1, axis=-1, keepdims=True) + NORM_EPS)
        y = y * g_ref[...]
        h_ref[rs, :] = (y * (1.0 + sc_ref[...]) + sh_ref[...]).astype(BF16)


def _out_proj(yr, ya, w, x, gt, g, sh, sc, *, tm):
    m, d = x.shape
    kr, ka = yr.shape[1], ya.shape[1]
    row = lambda i: (i, 0)
    vec = pl.BlockSpec((1, d), lambda i: (0, 0))
    return pl.pallas_call(
        _out_proj_kernel,
        grid=(m // tm,),
        in_specs=[pl.BlockSpec((tm, kr), row), pl.BlockSpec((tm, ka), row),
                  pl.BlockSpec((kr + ka, d), lambda i: (0, 0)),
                  pl.BlockSpec((tm, d), row), vec, vec, vec, vec],
        out_specs=[pl.BlockSpec((tm, d), row), pl.BlockSpec((tm, d), row)],
        out_shape=[jax.ShapeDtypeStruct((m, d), F32), jax.ShapeDtypeStruct((m, d), BF16)],
        compiler_params=_params("parallel"),
        name="out_proj",
    )(yr, ya, w, x, gt, g, sh, sc)


def _ffn_kernel(h_ref, x_ref, gt_ref, gfin_ref, wg_ref, wu_ref, wd_ref, o_ref):
    f = pl.program_id(1)

    @pl.when(f == 0)
    def _():
        o_ref[...] = jnp.zeros_like(o_ref)

    h = h_ref[...]
    a = jnp.dot(h, wg_ref[...], preferred_element_type=F32)
    u = jnp.dot(h, wu_ref[...], preferred_element_type=F32)
    act = ((a / (1.0 + jnp.exp(-a))) * u).astype(BF16)
    o_ref[...] += jnp.dot(act, wd_ref[...], preferred_element_type=F32)

    @pl.when(f == pl.num_programs(1) - 1)
    def _():
        y = x_ref[...] + gt_ref[...] * o_ref[...]
        y = y * lax.rsqrt(jnp.mean(y * y, axis=-1, keepdims=True) + NORM_EPS)
        o_ref[...] = y * gfin_ref[...]


def _ffn(h, x, gt, gfin, wg, wu, wd, *, tm, tf):
    m, d = x.shape
    ff = wg.shape[1]
    assert ff % tf == 0 and m % tm == 0
    row = lambda i, f: (i, 0)
    vec = pl.BlockSpec((1, d), lambda i, f: (0, 0))
    return pl.pallas_call(
        _ffn_kernel,
        grid=(m // tm, ff // tf),
        in_specs=[pl.BlockSpec((tm, d), row), pl.BlockSpec((tm, d), row), vec, vec,
                  pl.BlockSpec((d, tf), lambda i, f: (0, f)),
                  pl.BlockSpec((d, tf), lambda i, f: (0, f)),
                  pl.BlockSpec((tf, d), lambda i, f: (f, 0))],
        out_specs=pl.BlockSpec((tm, d), row),
        out_shape=jax.ShapeDtypeStruct((m, d), F32),
        compiler_params=_params("parallel", "arbitrary"),
        name="ffn",
    )(h, x, gt, gfin, wg, wu, wd)


def _rope_tables(L):
    lane = jnp.arange(LANES)
    inv1 = ROPE_BASE ** (-jnp.arange(32, dtype=F32) / 32)
    ang1 = jnp.arange(L, dtype=F32)[:, None] * inv1[None, :]
    sgn1 = jnp.where((lane % 64) < 32, -1.0, 1.0).astype(F32)
    cos1 = jnp.tile(jnp.cos(ang1), (1, LANES // 32))
    sin1 = jnp.tile(jnp.sin(ang1), (1, LANES // 32)) * sgn1[None, :]
    inv2 = ROPE_BASE ** (-jnp.arange(16, dtype=F32) / 16)
    nrow = L // GRID_W
    ang_r = jnp.arange(nrow, dtype=F32)[:, None] * inv2[None, :]
    ang_c = jnp.arange(GRID_W, dtype=F32)[:, None] * inv2[None, :]
    sgna = jnp.where((lane % 32) < 16, -1.0, 1.0).astype(F32)

    def expand(fr, fc):
        by_row = jnp.broadcast_to(jnp.tile(fr, (1, 2))[:, None, :], (nrow, GRID_W, 32))
        by_col = jnp.broadcast_to(jnp.tile(fc, (1, 2))[None, :, :], (nrow, GRID_W, 32))
        head = jnp.concatenate([by_row, by_col], axis=-1).reshape(L, 64)
        return jnp.tile(head, (1, LANES // 64))

    cosa = expand(jnp.cos(ang_r), jnp.cos(ang_c))
    sina = expand(jnp.sin(ang_r), jnp.sin(ang_c)) * sgna[None, :]
    return cos1, sin1, cosa, sina


def kernel(x, c, ctx, c_ctx, w_mod, b_mod, norm_mix, norm_ffn, w_in, ret_decay, attn_sink,
           w_out, w_gate, w_up, w_down, norm_final):
    B, L, D = x.shape
    assert B == 1 and w_mod.shape[0] == 1, "single batch element, depth-1 layer"
    lc = ctx.shape[1]
    x2 = x[0]
    xc2 = ctx[0]

    cv = jnp.zeros((8, D), F32).at[0].set(c[0]).at[1].set(c_ctx)
    mod = _mod(cv, w_mod[0], b_mod[0][None, :])
    sh_m, sc_m, gt_m, sh_f, sc_f, gt_f = [mod[0:1, k * D:(k + 1) * D] for k in range(6)]
    sh_mc, sc_mc = mod[1:2, 0:D], mod[1:2, D:2 * D]

    w_in_b = w_in[0].astype(BF16)
    g_mix = norm_mix[0][None, :]
    tabs = _rope_tables(L)
    ctabs = tuple(tb[:lc] for tb in tabs)
    proj, kd, vd = _in_proj(x2, g_mix, sh_m, sc_m, w_in_b, tabs, rope=True, tm=512)
    cproj, ckd, cvd = _in_proj(xc2, g_mix, sh_mc, sc_mc, w_in_b, ctabs, rope=False, tm=lc)

    dec = ret_decay[0].astype(F32)
    sf, sb = _ret_states(dec, proj, cproj)
    y_ret = _ret_out(dec, proj, sf, sb)
    y_att = _attn(attn_sink[0].astype(F32), proj, kd, vd, ckd, cvd)

    x1, hff = _out_proj(y_ret, y_att, w_out[0].astype(BF16), x2, gt_m,
                        norm_ffn[0][None, :], sh_f, sc_f, tm=512)
    out = _ffn(hff, x1, gt_f, norm_final[None, :],
               w_gate[0].astype(BF16), w_up[0].astype(BF16), w_down[0].astype(BF16), tm=512, tf=512)
    return out[None]
```

```python
import functools

import jax
import jax.numpy as jnp
import numpy as np
from jax import lax
from jax.experimental import pallas as pl
from jax.experimental.pallas import tpu as pltpu

GRID_W = 64
RET_HEADS = 8
RET_DK = 64
RET_DV = 128
RET_CHUNK = 128
ATT_HEADS = 16
ATT_KV_HEADS = 4
ATT_DH = 64
ATT_GROUP = ATT_HEADS // ATT_KV_HEADS
WINDOW = 128
ATT_BLOCK = 128
ROPE_BASE = 10000.0
NORM_EPS = 1e-6
K_SCALE = RET_DK ** -0.5
ATT_SCALE = ATT_DH ** -0.5

LANES = 128
RET_PAIRS = RET_HEADS // 2
MASK_NEG = -1e30
VMEM_LIMIT = 56 * 1024 * 1024
RET_STEP_CHUNKS = 4
OUT_ROW_CHUNK = 256
IN_ROW_CHUNK = 256

BF16 = jnp.bfloat16
F32 = jnp.float32


def _params(*sem):
    return pltpu.CompilerParams(dimension_semantics=sem, vmem_limit_bytes=VMEM_LIMIT)


def _with_cast_riders(body, n_in, n_out, n_rid):
    def wrapped(*refs):
        ins = refs[:n_in]
        rid_in = refs[n_in:n_in + n_rid]
        outs = refs[n_in + n_rid:n_in + n_rid + n_out]
        rid_out = refs[n_in + n_rid + n_out:n_in + 2 * n_rid + n_out]
        scratch = refs[n_in + 2 * n_rid + n_out:]
        for src, dst in zip(rid_in, rid_out):
            dst[...] = src[...].astype(BF16)
        body(*ins, *outs, *scratch)
    return wrapped


def _rider_specs(riders, steps):
    specs, shapes = [], []
    for w in riders:
        rows, cols = w.shape
        assert rows % steps == 0 and (rows // steps) % 16 == 0, "row slab must be bf16-tile aligned"
        specs.append(pl.BlockSpec((rows // steps, cols), lambda i: (i, 0)))
        shapes.append(jax.ShapeDtypeStruct(w.shape, BF16))
    return specs, shapes


def _mod_kernel(cv_ref, w_ref, b_ref, o_ref):
    cv = cv_ref[...]
    s = cv / (1.0 + jnp.exp(-cv))
    o_ref[...] = jnp.dot(s.astype(BF16), w_ref[...].astype(BF16),
                         preferred_element_type=F32) + b_ref[...]


def _mod(cv, w, b):
    d, n = w.shape
    tn = 1024
    return pl.pallas_call(
        _mod_kernel,
        grid=(n // tn,),
        in_specs=[pl.BlockSpec((8, d), lambda j: (0, 0)),
                  pl.BlockSpec((d, tn), lambda j: (0, j)),
                  pl.BlockSpec((1, tn), lambda j: (0, j))],
        out_specs=pl.BlockSpec((8, tn), lambda j: (0, j)),
        out_shape=jax.ShapeDtypeStruct((8, n), F32),
        compiler_params=_params("parallel"),
        name="mod",
    )(cv, w, b)


def _rot_pairs(a, cos, sin_signed, half):
    lane = lax.broadcasted_iota(jnp.int32, a.shape, 1)
    first = (lane % (2 * half)) < half
    rot = jnp.where(first, pltpu.roll(a, LANES - half, 1), pltpu.roll(a, half, 1))
    return a * cos + rot * sin_signed


def _dup_halves(a):
    lane = lax.broadcasted_iota(jnp.int32, a.shape, 1)
    r = pltpu.roll(a, 64, 1)
    lo = lane < 64
    return jnp.where(lo, a, r), jnp.where(lo, r, a)


_PROJ_TILE = 512
_PROJ_TILE_KINDS = ("ret_q", "ret_k", "plain", "plain", "plain", "plain", "att_q", "att_q", "att_kv")


def _in_proj_kernel(x_ref, g_ref, sh_ref, sc_ref, w_ref, c1_ref, s1_ref, ca_ref, sa_ref,
                    o_ref, kd_ref, vd_ref, *, rope):
    tn = _PROJ_TILE
    for r in range(x_ref.shape[0] // IN_ROW_CHUNK):
        rs = slice(r * IN_ROW_CHUNK, (r + 1) * IN_ROW_CHUNK)
        xf = x_ref[rs, :]
        y = xf * lax.rsqrt(jnp.mean(xf * xf, axis=-1, keepdims=True) + NORM_EPS)
        y = y * g_ref[...]
        h = (y * (1.0 + sc_ref[...]) + sh_ref[...]).astype(BF16)

        def rope1(a):
            return _rot_pairs(a, c1_ref[rs, :], s1_ref[rs, :], 32) if rope else a

        def ropea(a):
            return _rot_pairs(a, ca_ref[rs, :], sa_ref[rs, :], 16) if rope else a

        for j, kind in enumerate(_PROJ_TILE_KINDS):
            acc = jnp.dot(h, w_ref[:, j * tn:(j + 1) * tn], preferred_element_type=F32)
            for c in range(tn // LANES):
                a = acc[:, c * LANES:(c + 1) * LANES]
                if kind == "ret_q":
                    a = rope1(a)
                elif kind == "ret_k":
                    a = rope1(a) * K_SCALE
                elif kind == "att_q" or (kind == "att_kv" and c < 2):
                    a = ropea(a)
                o_ref[rs, j * tn + c * LANES:j * tn + (c + 1) * LANES] = a.astype(BF16)
                if kind == "att_kv":
                    dup_ref = kd_ref if c < 2 else vd_ref
                    d0, d1 = _dup_halves(a)
                    t = 2 * (c % 2)
                    dup_ref[rs, t * LANES:(t + 1) * LANES] = d0.astype(BF16)
                    dup_ref[rs, (t + 1) * LANES:(t + 2) * LANES] = d1.astype(BF16)


def _in_proj(x, g, sh, sc, w, tabs, *, rope, tm):
    m, d = x.shape
    n = w.shape[1]
    assert n == _PROJ_TILE * len(_PROJ_TILE_KINDS) and m % tm == 0 and tm % IN_ROW_CHUNK == 0
    c1, s1, ca, sa = tabs
    row = lambda i: (i, 0)
    vec = pl.BlockSpec((1, d), lambda i: (0, 0))
    tab = pl.BlockSpec((tm, LANES), row)
    return pl.pallas_call(
        functools.partial(_in_proj_kernel, rope=rope),
        grid=(m // tm,),
        in_specs=[pl.BlockSpec((tm, d), row), vec, vec, vec,
                  pl.BlockSpec((d, n), lambda i: (0, 0), pipeline_mode=pl.Buffered(1)),
                  tab, tab, tab, tab],
        out_specs=[pl.BlockSpec((tm, n), row),
                   pl.BlockSpec((tm, 512), row),
                   pl.BlockSpec((tm, 512), row)],
        out_shape=[jax.ShapeDtypeStruct((m, n), BF16),
                   jax.ShapeDtypeStruct((m, 512), BF16),
                   jax.ShapeDtypeStruct((m, 512), BF16)],
        compiler_params=_params("parallel"),
        name="in_proj" if rope else "in_proj_ctx",
    )(x, g, sh, sc, w, c1, s1, ca, sa)


def _pair_lg(dec_ref, d, p, shape):
    lane = lax.broadcasted_iota(jnp.int32, shape, 1)
    first = (lane % LANES) < 64
    raw = jnp.where(first, jnp.full(shape, dec_ref[d, 2 * p], F32), jnp.full(shape, dec_ref[d, 2 * p + 1], F32))
    return -jnp.exp(raw)


def _head_block_mask(shape):
    r = lax.broadcasted_iota(jnp.int32, shape, 0)
    c = lax.broadcasted_iota(jnp.int32, shape, 1)
    return (r // 64) == (c // LANES)


def _kv_pair(k_pair, v_pair, w):
    kw = (k_pair.astype(F32) * w).astype(BF16)
    kv = lax.dot_general(kw, v_pair, (((0,), (0,)), ((), ())), preferred_element_type=F32)
    return jnp.where(_head_block_mask(kv.shape), kv, 0.0)


def _ret_state_kernel(dec_ref, kf_ref, vf_ref, kb_ref, vb_ref, ck_ref, cv_ref,
                      sf_ref, sb_ref, sfs, sbs):
    i = pl.program_id(0)
    C = RET_CHUNK
    lc = ck_ref.shape[0]

    @pl.when(i == 0)
    def _():
        pos = lax.broadcasted_iota(jnp.int32, (lc, LANES), 0).astype(F32)
        for p in range(RET_PAIRS):
            ks = slice(p * LANES, (p + 1) * LANES)
            vs = slice(p * 2 * RET_DV, (p + 1) * 2 * RET_DV)
            wf = jnp.exp(_pair_lg(dec_ref, 0, p, (lc, LANES)) * (lc - 1.0 - pos))
            wb = jnp.exp(_pair_lg(dec_ref, 1, p, (lc, LANES)) * pos)
            sfs[p] = _kv_pair(ck_ref[:, ks], cv_ref[:, vs], wf)
            sbs[p] = _kv_pair(ck_ref[:, ks], cv_ref[:, vs], wb)

    pos = lax.broadcasted_iota(jnp.int32, (C, LANES), 0).astype(F32)
    for p in range(RET_PAIRS):
        ks = slice(p * LANES, (p + 1) * LANES)
        vs = slice(p * 2 * RET_DV, (p + 1) * 2 * RET_DV)
        wf = jnp.exp(_pair_lg(dec_ref, 0, p, (C, LANES)) * (C - 1.0 - pos))
        wb = jnp.exp(_pair_lg(dec_ref, 1, p, (C, LANES)) * pos)
        rowh = lax.broadcasted_iota(jnp.int32, (LANES, 2 * RET_DV), 0) < 64
        gf = jnp.exp(-jnp.exp(jnp.where(rowh, jnp.full(rowh.shape, dec_ref[0, 2 * p], F32),
                                         jnp.full(rowh.shape, dec_ref[0, 2 * p + 1], F32))) * float(C))
        gb = jnp.exp(-jnp.exp(jnp.where(rowh, jnp.full(rowh.shape, dec_ref[1, 2 * p], F32),
                                         jnp.full(rowh.shape, dec_ref[1, 2 * p + 1], F32))) * float(C))
        sf = sfs[p]
        for cc in range(RET_STEP_CHUNKS):
            rs = slice(cc * C, (cc + 1) * C)
            sf_ref[cc, p] = sf.astype(BF16)
            sf = gf * sf + _kv_pair(kf_ref[rs, ks], vf_ref[rs, vs], wf)
        sfs[p] = sf
        sb = sbs[p]
        for cc in reversed(range(RET_STEP_CHUNKS)):
            rs = slice(cc * C, (cc + 1) * C)
            sb_ref[cc, p] = sb.astype(BF16)
            sb = gb * sb + _kv_pair(kb_ref[rs, ks], vb_ref[rs, vs], wb)
        sbs[p] = sb


def _ret_states(dec, proj, cproj, riders):
    L = proj.shape[0]
    lc = cproj.shape[0]
    S = RET_STEP_CHUNKS
    R = S * RET_CHUNK
    n = L // R
    st = pl.BlockSpec((S, RET_PAIRS, LANES, 2 * RET_DV), lambda i: (i, 0, 0, 0))
    st_rev = pl.BlockSpec((S, RET_PAIRS, LANES, 2 * RET_DV), lambda i: (n - 1 - i, 0, 0, 0))
    shp = jax.ShapeDtypeStruct((n * S, RET_PAIRS, LANES, 2 * RET_DV), BF16)
    rid_specs, rid_shapes = _rider_specs(riders, n)
    return pl.pallas_call(
        _with_cast_riders(_ret_state_kernel, 7, 2, len(riders)),
        grid=(n,),
        in_specs=[pl.BlockSpec(memory_space=pltpu.SMEM),
                  pl.BlockSpec((R, 512), lambda i: (i, 1)),
                  pl.BlockSpec((R, 1024), lambda i: (i, 1)),
                  pl.BlockSpec((R, 512), lambda i: (n - 1 - i, 1)),
                  pl.BlockSpec((R, 1024), lambda i: (n - 1 - i, 1)),
                  pl.BlockSpec((lc, 512), lambda i: (0, 1)),
                  pl.BlockSpec((lc, 1024), lambda i: (0, 1))] + rid_specs,
        out_specs=[st, st_rev] + rid_specs,
        out_shape=[shp, shp] + rid_shapes,
        scratch_shapes=[pltpu.VMEM((RET_PAIRS, LANES, 2 * RET_DV), F32),
                        pltpu.VMEM((RET_PAIRS, LANES, 2 * RET_DV), F32)],
        compiler_params=_params("arbitrary"),
        name="ret_state",
    )(dec, proj, proj, proj, proj, cproj, cproj, *riders)


def _ret_out_kernel(dec_ref, q_ref, k_ref, v_ref, g_ref, sf_ref, sb_ref, o_ref):
    C = RET_CHUNK
    pos = lax.broadcasted_iota(jnp.int32, (C, LANES), 0).astype(F32)
    n_i = lax.broadcasted_iota(jnp.int32, (C, 2 * C), 0)
    m_i = lax.broadcasted_iota(jnp.int32, (C, 2 * C), 1) % C
    rel = (n_i - m_i).astype(F32)
    lane = lax.broadcasted_iota(jnp.int32, (C, LANES), 1)
    lo = lane < 64
    for p in range(RET_PAIRS):
        ks = slice(p * LANES, (p + 1) * LANES)
        vs = slice(p * 2 * RET_DV, (p + 1) * 2 * RET_DV)
        col_a = lax.broadcasted_iota(jnp.int32, (C, 2 * C), 1) < C
        raw_f = jnp.where(col_a, jnp.full((C, 2 * C), dec_ref[0, 2 * p], F32), jnp.full((C, 2 * C), dec_ref[0, 2 * p + 1], F32))
        raw_b = jnp.where(col_a, jnp.full((C, 2 * C), dec_ref[1, 2 * p], F32), jnp.full((C, 2 * C), dec_ref[1, 2 * p + 1], F32))
        dmat = jnp.where(rel >= 0, jnp.exp(-jnp.exp(raw_f) * jnp.maximum(rel, 0.0)),
                         jnp.exp(-jnp.exp(raw_b) * jnp.maximum(-rel, 0.0)))
        wqf = jnp.exp(_pair_lg(dec_ref, 0, p, (C, LANES)) * (pos + 1.0))
        wqb = jnp.exp(_pair_lg(dec_ref, 1, p, (C, LANES)) * (float(C) - pos))
        for cc in range(RET_STEP_CHUNKS):
            rs = slice(cc * C, (cc + 1) * C)
            q = q_ref[rs, ks]
            k = k_ref[rs, ks]
            v = v_ref[rs, vs]
            zk = jnp.zeros_like(k)
            kst = jnp.concatenate([jnp.where(lo, k, zk), jnp.where(lo, zk, k)], axis=0)
            s = lax.dot_general(q, kst, (((1,), (1,)), ((), ())), preferred_element_type=F32)
            sd = (s * dmat).astype(BF16)
            qf32 = q.astype(F32)
            qwf = (qf32 * wqf).astype(BF16)
            qwb = (qf32 * wqb).astype(BF16)
            zv = jnp.zeros((C, RET_DV), BF16)
            vbd = jnp.concatenate([jnp.concatenate([v[:, :RET_DV], zv], axis=1),
                                   jnp.concatenate([zv, v[:, RET_DV:]], axis=1)], axis=0)
            lhs = jnp.concatenate([sd, qwf, qwb], axis=1)
            rhs = jnp.concatenate([vbd, sf_ref[cc, p], sb_ref[cc, p]], axis=0)
            o = jnp.dot(lhs, rhs, preferred_element_type=F32)
            for t in range(2):
                oh = o[:, t * RET_DV:(t + 1) * RET_DV]
                oh = oh * lax.rsqrt(jnp.mean(oh * oh, axis=-1, keepdims=True) + NORM_EPS)
                cs = slice(p * 2 * RET_DV + t * RET_DV, p * 2 * RET_DV + (t + 1) * RET_DV)
                gt = g_ref[rs, cs].astype(F32)
                o_ref[rs, cs] = (oh * (gt / (1.0 + jnp.exp(-gt)))).astype(BF16)


def _ret_out(dec, proj, sf, sb, riders):
    L = proj.shape[0]
    S = RET_STEP_CHUNKS
    R = S * RET_CHUNK
    n = L // R
    st = pl.BlockSpec((S, RET_PAIRS, LANES, 2 * RET_DV), lambda i: (i, 0, 0, 0))
    rid_specs, rid_shapes = _rider_specs(riders, n)
    return pl.pallas_call(
        _with_cast_riders(_ret_out_kernel, 7, 1, len(riders)),
        grid=(n,),
        in_specs=[pl.BlockSpec(memory_space=pltpu.SMEM),
                  pl.BlockSpec((R, 512), lambda i: (i, 0)),
                  pl.BlockSpec((R, 512), lambda i: (i, 1)),
                  pl.BlockSpec((R, 1024), lambda i: (i, 1)),
                  pl.BlockSpec((R, 1024), lambda i: (i, 2)),
                  st, st] + rid_specs,
        out_specs=[pl.BlockSpec((R, RET_HEADS * RET_DV), lambda i: (i, 0))] + rid_specs,
        out_shape=[jax.ShapeDtypeStruct((L, RET_HEADS * RET_DV), BF16)] + rid_shapes,
        compiler_params=_params("parallel"),
        name="ret_out",
    )(dec, proj, proj, proj, proj, sf, sb, *riders)


def _attn_kernel(sink_ref, q_ref, kp_ref, kc_ref, kn_ref, vp_ref, vc_ref, vn_ref, ck_ref, cv_ref, o_ref):
    n = pl.program_id(0)
    nblk = pl.num_programs(0)
    B = ATT_BLOCK
    lc = ck_ref.shape[0]
    kj = lax.broadcasted_iota(jnp.int32, (B, B), 0)
    qi = lax.broadcasted_iota(jnp.int32, (B, B), 1)
    ok_prev = jnp.where(n > 0, 0.0, MASK_NEG).astype(F32)
    ok_next = jnp.where(n < nblk - 1, 0.0, MASK_NEG).astype(F32)
    bias_prev = jnp.concatenate([jnp.where(kj >= qi, ok_prev, MASK_NEG).astype(F32)] * ATT_GROUP, axis=1)
    bias_next = jnp.concatenate([jnp.where(kj <= qi, ok_next, MASK_NEG).astype(F32)] * ATT_GROUP, axis=1)
    lane = lax.broadcasted_iota(jnp.int32, (B, LANES), 1)
    lo = lane < 64
    hi = lane >= 64
    for g in range(ATT_KV_HEADS):
        gs = slice(g * LANES, (g + 1) * LANES)
        kcat = jnp.concatenate([kp_ref[:, gs], kc_ref[:, gs], kn_ref[:, gs], ck_ref[:, gs]], axis=0)
        vcat = jnp.concatenate([vp_ref[:, gs], vc_ref[:, gs], vn_ref[:, gs], cv_ref[:, gs]], axis=0)
        qs, sinks = [], []
        for r in range(ATT_GROUP):
            h = ATT_GROUP * g + r
            qt = q_ref[:, (h // 2) * LANES:(h // 2 + 1) * LANES]
            keep = lo if h % 2 == 0 else hi
            qs.append(jnp.where(keep, qt * jnp.asarray(ATT_SCALE, BF16), jnp.zeros_like(qt)))
            sinks.append(jnp.full((1, B), sink_ref[h], F32))
        q4 = jnp.concatenate(qs, axis=0)
        sk = jnp.concatenate(sinks, axis=1)
        s = lax.dot_general(kcat, q4, (((1,), (1,)), ((), ())), preferred_element_type=F32)
        s = jnp.concatenate([s[:B] + bias_prev, s[B:2 * B], s[2 * B:3 * B] + bias_next, s[3 * B:]], axis=0)
        m = jnp.maximum(jnp.max(s, axis=0, keepdims=True), sk)
        e = jnp.exp(s - m)
        den = jnp.sum(e, axis=0, keepdims=True) + jnp.exp(sk - m)
        res = lax.dot_general(vcat, e.astype(BF16), (((0,), (0,)), ((), ())), preferred_element_type=F32) / den
        for t in range(2):
            even = res[:, (2 * t) * B:(2 * t + 1) * B].T
            odd = res[:, (2 * t + 1) * B:(2 * t + 2) * B].T
            c0 = (2 * g + t) * LANES
            o_ref[:, c0:c0 + LANES] = jnp.where(lo, even, odd).astype(BF16)


def _attn(sink, proj, kd, vd, ckd, cvd, riders):
    L = proj.shape[0]
    B = ATT_BLOCK
    n = L // B
    lc = ckd.shape[0]
    prev = lambda i: (jnp.maximum(i - 1, 0), 0)
    cur = lambda i: (i, 0)
    nxt = lambda i: (jnp.minimum(i + 1, n - 1), 0)
    kv = lambda f: pl.BlockSpec((B, 512), f)
    full = pl.BlockSpec((lc, 512), lambda i: (0, 0))
    rid_specs, rid_shapes = _rider_specs(riders, n)
    return pl.pallas_call(
        _with_cast_riders(_attn_kernel, 10, 1, len(riders)),
        grid=(n,),
        in_specs=[pl.BlockSpec(memory_space=pltpu.SMEM),
                  pl.BlockSpec((B, 1024), lambda i: (i, 3)),
                  kv(prev), kv(cur), kv(nxt), kv(prev), kv(cur), kv(nxt), full, full] + rid_specs,
        out_specs=[pl.BlockSpec((B, ATT_HEADS * ATT_DH), cur)] + rid_specs,
        out_shape=[jax.ShapeDtypeStruct((L, ATT_HEADS * ATT_DH), BF16)] + rid_shapes,
        compiler_params=_params("parallel"),
        name="attn",
    )(sink, proj, kd, kd, kd, vd, vd, vd, ckd, cvd, *riders)


def _out_proj_kernel(yr_ref, ya_ref, w_ref, x_ref, gt_ref, g_ref, sh_ref, sc_ref, o_ref, h_ref):
    kr = yr_ref.shape[1]
    for r in range(yr_ref.shape[0] // OUT_ROW_CHUNK):
        rs = slice(r * OUT_ROW_CHUNK, (r + 1) * OUT_ROW_CHUNK)
        acc = jnp.dot(yr_ref[rs, :], w_ref[:kr, :], preferred_element_type=F32)
        acc = acc + jnp.dot(ya_ref[rs, :], w_ref[kr:, :], preferred_element_type=F32)
        x1 = x_ref[rs, :] + gt_ref[...] * acc
        o_ref[rs, :] = x1
        y = x1 * lax.rsqrt(jnp.mean(x1 * x1, axis=-1, keepdims=True) + NORM_EPS)
        y = y * g_ref[...]
        h_ref[rs, :] = (y * (1.0 + sc_ref[...]) + sh_ref[...]).astype(BF16)


def _out_proj(yr, ya, w, x, gt, g, sh, sc, *, tm):
    m, d = x.shape
    kr, ka = yr.shape[1], ya.shape[1]
    row = lambda i: (i, 0)
    vec = pl.BlockSpec((1, d), lambda i: (0, 0))
    return pl.pallas_call(
        _out_proj_kernel,
        grid=(m // tm,),
        in_specs=[pl.BlockSpec((tm, kr), row), pl.BlockSpec((tm, ka), row),
                  pl.BlockSpec((kr + ka, d), lambda i: (0, 0)),
                  pl.BlockSpec((tm, d), row), vec, vec, vec, vec],
        out_specs=[pl.BlockSpec((tm, d), row), pl.BlockSpec((tm, d), row)],
        out_shape=[jax.ShapeDtypeStruct((m, d), F32), jax.ShapeDtypeStruct((m, d), BF16)],
        compiler_params=_params("parallel"),
        name="out_proj",
    )(yr, ya, w, x, gt, g, sh, sc)


def _ffn_kernel(h_ref, x_ref, gt_ref, gfin_ref, wg_ref, wu_ref, wd_ref, o_ref):
    f = pl.program_id(1)

    @pl.when(f == 0)
    def _():
        o_ref[...] = jnp.zeros_like(o_ref)

    h = h_ref[...]
    a = jnp.dot(h, wg_ref[...], preferred_element_type=F32)
    u = jnp.dot(h, wu_ref[...], preferred_element_type=F32)
    act = ((a / (1.0 + jnp.exp(-a))) * u).astype(BF16)
    o_ref[...] += jnp.dot(act, wd_ref[...], preferred_element_type=F32)

    @pl.when(f == pl.num_programs(1) - 1)
    def _():
        y = x_ref[...] + gt_ref[...] * o_ref[...]
        y = y * lax.rsqrt(jnp.mean(y * y, axis=-1, keepdims=True) + NORM_EPS)
        o_ref[...] = y * gfin_ref[...]


def _ffn(h, x, gt, gfin, wg, wu, wd, *, tm, tf):
    m, d = x.shape
    ff = wg.shape[1]
    assert ff % tf == 0 and m % tm == 0
    row = lambda i, f: (i, 0)
    vec = pl.BlockSpec((1, d), lambda i, f: (0, 0))
    return pl.pallas_call(
        _ffn_kernel,
        grid=(m // tm, ff // tf),
        in_specs=[pl.BlockSpec((tm, d), row), pl.BlockSpec((tm, d), row), vec, vec,
                  pl.BlockSpec((d, tf), lambda i, f: (0, f)),
                  pl.BlockSpec((d, tf), lambda i, f: (0, f)),
                  pl.BlockSpec((tf, d), lambda i, f: (f, 0))],
        out_specs=pl.BlockSpec((tm, d), row),
        out_shape=jax.ShapeDtypeStruct((m, d), F32),
        compiler_params=_params("parallel", "arbitrary"),
        name="ffn",
    )(h, x, gt, gfin, wg, wu, wd)


def _rope_tables(L):
    f32 = np.float32
    lane = np.arange(LANES)
    inv1 = f32(ROPE_BASE) ** (-np.arange(32, dtype=f32) / f32(32))
    ang1 = np.arange(L, dtype=f32)[:, None] * inv1[None, :]
    sgn1 = np.where((lane % 64) < 32, -1.0, 1.0).astype(f32)
    cos1 = np.tile(np.cos(ang1), (1, LANES // 32))
    sin1 = np.tile(np.sin(ang1), (1, LANES // 32)) * sgn1[None, :]
    inv2 = f32(ROPE_BASE) ** (-np.arange(16, dtype=f32) / f32(16))
    nrow = L // GRID_W
    ang_r = np.arange(nrow, dtype=f32)[:, None] * inv2[None, :]
    ang_c = np.arange(GRID_W, dtype=f32)[:, None] * inv2[None, :]
    sgna = np.where((lane % 32) < 16, -1.0, 1.0).astype(f32)

    def expand(fr, fc):
        by_row = np.broadcast_to(np.tile(fr, (1, 2))[:, None, :], (nrow, GRID_W, 32))
        by_col = np.broadcast_to(np.tile(fc, (1, 2))[None, :, :], (nrow, GRID_W, 32))
        head = np.concatenate([by_row, by_col], axis=-1).reshape(L, 64)
        return np.tile(head, (1, LANES // 64))

    cosa = expand(np.cos(ang_r), np.cos(ang_c))
    sina = expand(np.sin(ang_r), np.sin(ang_c)) * sgna[None, :]
    return tuple(np.ascontiguousarray(t, dtype=f32) for t in (cos1, sin1, cosa, sina))


def kernel(x, c, ctx, c_ctx, w_mod, b_mod, norm_mix, norm_ffn, w_in, ret_decay, attn_sink,
           w_out, w_gate, w_up, w_down, norm_final):
    B, L, D = x.shape
    assert B == 1 and w_mod.shape[0] == 1, "single batch element, depth-1 layer"
    lc = ctx.shape[1]
    x2 = x[0]
    xc2 = ctx[0]

    cv = jnp.zeros((8, D), F32).at[0].set(c[0]).at[1].set(c_ctx)
    mod = _mod(cv, w_mod[0], b_mod[0][None, :])
    sh_m, sc_m, gt_m, sh_f, sc_f, gt_f = [mod[0:1, k * D:(k + 1) * D] for k in range(6)]
    sh_mc, sc_mc = mod[1:2, 0:D], mod[1:2, D:2 * D]

    w_in_b = w_in[0].astype(BF16)
    g_mix = norm_mix[0][None, :]
    tabs = _rope_tables(L)
    ctabs = tuple(tb[:lc] for tb in tabs)
    proj, kd, vd = _in_proj(x2, g_mix, sh_m, sc_m, w_in_b, tabs, rope=True, tm=512)
    cproj, ckd, cvd = _in_proj(xc2, g_mix, sh_mc, sc_mc, w_in_b, ctabs, rope=False, tm=lc)

    dec = ret_decay[0].astype(F32)
    sf, sb, w_down_b = _ret_states(dec, proj, cproj, [w_down[0]])
    y_ret, w_out_b = _ret_out(dec, proj, sf, sb, [w_out[0]])
    y_att, w_gate_b, w_up_b = _attn(attn_sink[0].astype(F32), proj, kd, vd, ckd, cvd, [w_gate[0], w_up[0]])

    x1, hff = _out_proj(y_ret, y_att, w_out_b, x2, gt_m, norm_ffn[0][None, :], sh_f, sc_f, tm=512)
    out = _ffn(hff, x1, gt_f, norm_final[None, :], w_gate_b, w_up_b, w_down_b, tm=512, tf=512)
    return out[None]
```

```python
import jax
import jax.numpy as jnp
import numpy as np
from jax import lax
from jax.experimental import pallas as pl
from jax.experimental.pallas import tpu as pltpu

GRID_W = 64
RET_HEADS = 8
RET_DK = 64
RET_DV = 128
RET_CHUNK = 128
ATT_HEADS = 16
ATT_KV_HEADS = 4
ATT_DH = 64
ATT_GROUP = ATT_HEADS // ATT_KV_HEADS
WINDOW = 128
ATT_BLOCK = 128
ROPE_BASE = 10000.0
NORM_EPS = 1e-6
K_SCALE = RET_DK ** -0.5
ATT_SCALE = ATT_DH ** -0.5

LANES = 128
RET_PAIRS = RET_HEADS // 2
MASK_NEG = -1e30
VMEM_LIMIT = 56 * 1024 * 1024
RET_STEP_CHUNKS = 4
OUT_ROW_CHUNK = 256
IN_ROW_CHUNK = 256

BF16 = jnp.bfloat16
F32 = jnp.float32


def _params(*sem):
    return pltpu.CompilerParams(dimension_semantics=sem, vmem_limit_bytes=VMEM_LIMIT)


def _with_cast_riders(body, n_in, n_out, n_rid):
    def wrapped(*refs):
        ins = refs[:n_in]
        rid_in = refs[n_in:n_in + n_rid]
        outs = refs[n_in + n_rid:n_in + n_rid + n_out]
        rid_out = refs[n_in + n_rid + n_out:n_in + 2 * n_rid + n_out]
        scratch = refs[n_in + 2 * n_rid + n_out:]
        for src, dst in zip(rid_in, rid_out):
            dst[...] = src[...].astype(BF16)
        body(*ins, *outs, *scratch)
    return wrapped


def _rider_specs(riders, steps):
    specs, shapes = [], []
    for w, ncb in riders:
        rows, cols = w.shape
        nrb = steps // ncb
        assert nrb * ncb == steps and rows % nrb == 0 and cols % ncb == 0
        br, bc = rows // nrb, cols // ncb
        assert br % 16 == 0 and bc % LANES == 0, "slab must be bf16-tile aligned"
        specs.append(pl.BlockSpec((br, bc), lambda i, ncb=ncb: (i // ncb, i % ncb)))
        shapes.append(jax.ShapeDtypeStruct(w.shape, BF16))
    return specs, shapes


def _mod_kernel(cv_ref, w_ref, b_ref, o_ref):
    cv = cv_ref[...]
    s = cv / (1.0 + jnp.exp(-cv))
    o_ref[...] = jnp.dot(s.astype(BF16), w_ref[...].astype(BF16),
                         preferred_element_type=F32) + b_ref[...]


def _mod(cv, w, b):
    d, n = w.shape
    tn = 1024
    return pl.pallas_call(
        _mod_kernel,
        grid=(n // tn,),
        in_specs=[pl.BlockSpec((8, d), lambda j: (0, 0)),
                  pl.BlockSpec((d, tn), lambda j: (0, j)),
                  pl.BlockSpec((1, tn), lambda j: (0, j))],
        out_specs=pl.BlockSpec((8, tn), lambda j: (0, j)),
        out_shape=jax.ShapeDtypeStruct((8, n), F32),
        compiler_params=_params("parallel"),
        name="mod",
    )(cv, w, b)


def _rot_pairs(a, cos, sin_signed, half):
    lane = lax.broadcasted_iota(jnp.int32, a.shape, 1)
    first = (lane % (2 * half)) < half
    rot = jnp.where(first, pltpu.roll(a, LANES - half, 1), pltpu.roll(a, half, 1))
    return a * cos + rot * sin_signed


def _dup_halves(a):
    lane = lax.broadcasted_iota(jnp.int32, a.shape, 1)
    r = pltpu.roll(a, 64, 1)
    lo = lane < 64
    return jnp.where(lo, a, r), jnp.where(lo, r, a)


_PROJ_TILE = 512
_PROJ_TILE_KINDS = ("ret_q", "ret_k", "plain", "plain", "plain", "plain", "att_q", "att_q", "att_kv")


def _in_proj_kernel(x_ref, g_ref, sh_ref, sc_ref, w_ref, c1_ref, s1_ref, ca_ref, sa_ref,
                    o_ref, kd_ref, vd_ref):
    tn = _PROJ_TILE
    for r in range(x_ref.shape[0] // IN_ROW_CHUNK):
        rs = slice(r * IN_ROW_CHUNK, (r + 1) * IN_ROW_CHUNK)
        xf = x_ref[rs, :]
        y = xf * lax.rsqrt(jnp.mean(xf * xf, axis=-1, keepdims=True) + NORM_EPS)
        y = y * g_ref[...]
        h = (y * (1.0 + sc_ref[...]) + sh_ref[...]).astype(BF16)

        def rope1(a):
            return _rot_pairs(a, c1_ref[rs, :], s1_ref[rs, :], 32)

        def ropea(a):
            return _rot_pairs(a, ca_ref[rs, :], sa_ref[rs, :], 16)

        for j, kind in enumerate(_PROJ_TILE_KINDS):
            acc = jnp.dot(h, w_ref[:, j * tn:(j + 1) * tn], preferred_element_type=F32)
            for c in range(tn // LANES):
                a = acc[:, c * LANES:(c + 1) * LANES]
                if kind == "ret_q":
                    a = rope1(a)
                elif kind == "ret_k":
                    a = rope1(a) * K_SCALE
                elif kind == "att_q" or (kind == "att_kv" and c < 2):
                    a = ropea(a)
                o_ref[rs, j * tn + c * LANES:j * tn + (c + 1) * LANES] = a.astype(BF16)
                if kind == "att_kv":
                    dup_ref = kd_ref if c < 2 else vd_ref
                    d0, d1 = _dup_halves(a)
                    t = 2 * (c % 2)
                    dup_ref[rs, t * LANES:(t + 1) * LANES] = d0.astype(BF16)
                    dup_ref[rs, (t + 1) * LANES:(t + 2) * LANES] = d1.astype(BF16)


def _in_proj(x, g, sh, sc, w, tabs, *, tm):
    m, d = x.shape
    n = w.shape[1]
    assert n == _PROJ_TILE * len(_PROJ_TILE_KINDS) and m % tm == 0 and tm % IN_ROW_CHUNK == 0
    c1, s1, ca, sa = tabs
    row = lambda i: (i, 0)
    vec = pl.BlockSpec((1, d), lambda i: (0, 0))
    tab = pl.BlockSpec((tm, LANES), row)
    return pl.pallas_call(
        _in_proj_kernel,
        grid=(m // tm,),
        in_specs=[pl.BlockSpec((tm, d), row), vec, vec, vec,
                  pl.BlockSpec((d, n), lambda i: (0, 0), pipeline_mode=pl.Buffered(1)),
                  tab, tab, tab, tab],
        out_specs=[pl.BlockSpec((tm, n), row),
                   pl.BlockSpec((tm, 512), row),
                   pl.BlockSpec((tm, 512), row)],
        out_shape=[jax.ShapeDtypeStruct((m, n), BF16),
                   jax.ShapeDtypeStruct((m, 512), BF16),
                   jax.ShapeDtypeStruct((m, 512), BF16)],
        compiler_params=_params("parallel"),
        name="in_proj",
    )(x, g, sh, sc, w, c1, s1, ca, sa)


def _ctx_proj_kernel(x_ref, g_ref, sh_ref, sc_ref, w_ref, o_ref, kd_ref, vd_ref, wb_ref, h_ref):
    j = pl.program_id(0)

    @pl.when(j == 0)
    def _():
        xf = x_ref[...]
        y = xf * lax.rsqrt(jnp.mean(xf * xf, axis=-1, keepdims=True) + NORM_EPS)
        y = y * g_ref[...]
        h_ref[...] = (y * (1.0 + sc_ref[...]) + sh_ref[...]).astype(BF16)

    wb = w_ref[...].astype(BF16)
    wb_ref[...] = wb
    acc = jnp.dot(h_ref[...], wb, preferred_element_type=F32)
    is_ret_k = _PROJ_TILE_KINDS.index("ret_k")
    o_ref[...] = (acc * jnp.where(j == is_ret_k, K_SCALE, 1.0)).astype(BF16)

    @pl.when(j == _PROJ_TILE_KINDS.index("att_kv"))
    def _():
        for c in range(_PROJ_TILE // LANES):
            dup_ref = kd_ref if c < 2 else vd_ref
            d0, d1 = _dup_halves(acc[:, c * LANES:(c + 1) * LANES])
            t = 2 * (c % 2)
            dup_ref[:, t * LANES:(t + 1) * LANES] = d0.astype(BF16)
            dup_ref[:, (t + 1) * LANES:(t + 2) * LANES] = d1.astype(BF16)


def _ctx_proj(x, g, sh, sc, w):
    m, d = x.shape
    n = w.shape[1]
    tn = _PROJ_TILE
    assert n == tn * len(_PROJ_TILE_KINDS)
    fixed = lambda j: (0, 0)
    vec = pl.BlockSpec((1, d), fixed)
    return pl.pallas_call(
        _ctx_proj_kernel,
        grid=(n // tn,),
        in_specs=[pl.BlockSpec((m, d), fixed), vec, vec, vec,
                  pl.BlockSpec((d, tn), lambda j: (0, j))],
        out_specs=[pl.BlockSpec((m, tn), lambda j: (0, j)),
                   pl.BlockSpec((m, 512), fixed),
                   pl.BlockSpec((m, 512), fixed),
                   pl.BlockSpec((d, tn), lambda j: (0, j))],
        out_shape=[jax.ShapeDtypeStruct((m, n), BF16),
                   jax.ShapeDtypeStruct((m, 512), BF16),
                   jax.ShapeDtypeStruct((m, 512), BF16),
                   jax.ShapeDtypeStruct((d, n), BF16)],
        scratch_shapes=[pltpu.VMEM((m, d), BF16)],
        compiler_params=_params("arbitrary"),
        name="ctx_proj",
    )(x, g, sh, sc, w)


def _pair_lg(dec_ref, d, p, shape):
    lane = lax.broadcasted_iota(jnp.int32, shape, 1)
    first = (lane % LANES) < 64
    raw = jnp.where(first, jnp.full(shape, dec_ref[d, 2 * p], F32), jnp.full(shape, dec_ref[d, 2 * p + 1], F32))
    return -jnp.exp(raw)


def _head_block_mask(shape):
    r = lax.broadcasted_iota(jnp.int32, shape, 0)
    c = lax.broadcasted_iota(jnp.int32, shape, 1)
    return (r // 64) == (c // LANES)


def _kv_pair(k_pair, v_pair, w):
    kw = (k_pair.astype(F32) * w).astype(BF16)
    kv = lax.dot_general(kw, v_pair, (((0,), (0,)), ((), ())), preferred_element_type=F32)
    return jnp.where(_head_block_mask(kv.shape), kv, 0.0)


def _ret_state_kernel(dec_ref, kf_ref, vf_ref, kb_ref, vb_ref, ck_ref, cv_ref,
                      sf_ref, sb_ref, sfs, sbs):
    i = pl.program_id(0)
    C = RET_CHUNK
    lc = ck_ref.shape[0]

    @pl.when(i == 0)
    def _():
        pos = lax.broadcasted_iota(jnp.int32, (lc, LANES), 0).astype(F32)
        for p in range(RET_PAIRS):
            ks = slice(p * LANES, (p + 1) * LANES)
            vs = slice(p * 2 * RET_DV, (p + 1) * 2 * RET_DV)
            wf = jnp.exp(_pair_lg(dec_ref, 0, p, (lc, LANES)) * (lc - 1.0 - pos))
            wb = jnp.exp(_pair_lg(dec_ref, 1, p, (lc, LANES)) * pos)
            sfs[p] = _kv_pair(ck_ref[:, ks], cv_ref[:, vs], wf)
            sbs[p] = _kv_pair(ck_ref[:, ks], cv_ref[:, vs], wb)

    pos = lax.broadcasted_iota(jnp.int32, (C, LANES), 0).astype(F32)
    for p in range(RET_PAIRS):
        ks = slice(p * LANES, (p + 1) * LANES)
        vs = slice(p * 2 * RET_DV, (p + 1) * 2 * RET_DV)
        wf = jnp.exp(_pair_lg(dec_ref, 0, p, (C, LANES)) * (C - 1.0 - pos))
        wb = jnp.exp(_pair_lg(dec_ref, 1, p, (C, LANES)) * pos)
        rowh = lax.broadcasted_iota(jnp.int32, (LANES, 2 * RET_DV), 0) < 64
        gf = jnp.exp(-jnp.exp(jnp.where(rowh, jnp.full(rowh.shape, dec_ref[0, 2 * p], F32),
                                         jnp.full(rowh.shape, dec_ref[0, 2 * p + 1], F32))) * float(C))
        gb = jnp.exp(-jnp.exp(jnp.where(rowh, jnp.full(rowh.shape, dec_ref[1, 2 * p], F32),
                                         jnp.full(rowh.shape, dec_ref[1, 2 * p + 1], F32))) * float(C))
        sf = sfs[p]
        for cc in range(RET_STEP_CHUNKS):
            rs = slice(cc * C, (cc + 1) * C)
            sf_ref[cc, p] = sf.astype(BF16)
            sf = gf * sf + _kv_pair(kf_ref[rs, ks], vf_ref[rs, vs], wf)
        sfs[p] = sf
        sb = sbs[p]
        for cc in reversed(range(RET_STEP_CHUNKS)):
            rs = slice(cc * C, (cc + 1) * C)
            sb_ref[cc, p] = sb.astype(BF16)
            sb = gb * sb + _kv_pair(kb_ref[rs, ks], vb_ref[rs, vs], wb)
        sbs[p] = sb


def _ret_states(dec, proj, cproj):
    L = proj.shape[0]
    lc = cproj.shape[0]
    S = RET_STEP_CHUNKS
    R = S * RET_CHUNK
    n = L // R
    st = pl.BlockSpec((S, RET_PAIRS, LANES, 2 * RET_DV), lambda i: (i, 0, 0, 0))
    st_rev = pl.BlockSpec((S, RET_PAIRS, LANES, 2 * RET_DV), lambda i: (n - 1 - i, 0, 0, 0))
    shp = jax.ShapeDtypeStruct((n * S, RET_PAIRS, LANES, 2 * RET_DV), BF16)
    return pl.pallas_call(
        _ret_state_kernel,
        grid=(n,),
        in_specs=[pl.BlockSpec(memory_space=pltpu.SMEM),
                  pl.BlockSpec((R, 512), lambda i: (i, 1)),
                  pl.BlockSpec((R, 1024), lambda i: (i, 1)),
                  pl.BlockSpec((R, 512), lambda i: (n - 1 - i, 1)),
                  pl.BlockSpec((R, 1024), lambda i: (n - 1 - i, 1)),
                  pl.BlockSpec((lc, 512), lambda i: (0, 1)),
                  pl.BlockSpec((lc, 1024), lambda i: (0, 1))],
        out_specs=[st, st_rev],
        out_shape=[shp, shp],
        scratch_shapes=[pltpu.VMEM((RET_PAIRS, LANES, 2 * RET_DV), F32),
                        pltpu.VMEM((RET_PAIRS, LANES, 2 * RET_DV), F32)],
        compiler_params=_params("arbitrary"),
        name="ret_state",
    )(dec, proj, proj, proj, proj, cproj, cproj)


def _ret_out_kernel(dec_ref, q_ref, k_ref, v_ref, g_ref, sf_ref, sb_ref, o_ref):
    C = RET_CHUNK
    pos = lax.broadcasted_iota(jnp.int32, (C, LANES), 0).astype(F32)
    n_i = lax.broadcasted_iota(jnp.int32, (C, 2 * C), 0)
    m_i = lax.broadcasted_iota(jnp.int32, (C, 2 * C), 1) % C
    rel = (n_i - m_i).astype(F32)
    lane = lax.broadcasted_iota(jnp.int32, (C, LANES), 1)
    lo = lane < 64
    for p in range(RET_PAIRS):
        ks = slice(p * LANES, (p + 1) * LANES)
        vs = slice(p * 2 * RET_DV, (p + 1) * 2 * RET_DV)
        col_a = lax.broadcasted_iota(jnp.int32, (C, 2 * C), 1) < C
        raw_f = jnp.where(col_a, jnp.full((C, 2 * C), dec_ref[0, 2 * p], F32), jnp.full((C, 2 * C), dec_ref[0, 2 * p + 1], F32))
        raw_b = jnp.where(col_a, jnp.full((C, 2 * C), dec_ref[1, 2 * p], F32), jnp.full((C, 2 * C), dec_ref[1, 2 * p + 1], F32))
        dmat = jnp.where(rel >= 0, jnp.exp(-jnp.exp(raw_f) * jnp.maximum(rel, 0.0)),
                         jnp.exp(-jnp.exp(raw_b) * jnp.maximum(-rel, 0.0)))
        wqf = jnp.exp(_pair_lg(dec_ref, 0, p, (C, LANES)) * (pos + 1.0))
        wqb = jnp.exp(_pair_lg(dec_ref, 1, p, (C, LANES)) * (float(C) - pos))
        for cc in range(RET_STEP_CHUNKS):
            rs = slice(cc * C, (cc + 1) * C)
            q = q_ref[rs, ks]
            k = k_ref[rs, ks]
            v = v_ref[rs, vs]
            zk = jnp.zeros_like(k)
            kst = jnp.concatenate([jnp.where(lo, k, zk), jnp.where(lo, zk, k)], axis=0)
            s = lax.dot_general(q, kst, (((1,), (1,)), ((), ())), preferred_element_type=F32)
            sd = (s * dmat).astype(BF16)
            qf32 = q.astype(F32)
            qwf = (qf32 * wqf).astype(BF16)
            qwb = (qf32 * wqb).astype(BF16)
            zv = jnp.zeros((C, RET_DV), BF16)
            vbd = jnp.concatenate([jnp.concatenate([v[:, :RET_DV], zv], axis=1),
                                   jnp.concatenate([zv, v[:, RET_DV:]], axis=1)], axis=0)
            lhs = jnp.concatenate([sd, qwf, qwb], axis=1)
            rhs = jnp.concatenate([vbd, sf_ref[cc, p], sb_ref[cc, p]], axis=0)
            o = jnp.dot(lhs, rhs, preferred_element_type=F32)
            for t in range(2):
                oh = o[:, t * RET_DV:(t + 1) * RET_DV]
                oh = oh * lax.rsqrt(jnp.mean(oh * oh, axis=-1, keepdims=True) + NORM_EPS)
                cs = slice(p * 2 * RET_DV + t * RET_DV, p * 2 * RET_DV + (t + 1) * RET_DV)
                gt = g_ref[rs, cs].astype(F32)
                o_ref[rs, cs] = (oh * (gt / (1.0 + jnp.exp(-gt)))).astype(BF16)


def _ret_out(dec, proj, sf, sb):
    L = proj.shape[0]
    S = RET_STEP_CHUNKS
    R = S * RET_CHUNK
    n = L // R
    st = pl.BlockSpec((S, RET_PAIRS, LANES, 2 * RET_DV), lambda i: (i, 0, 0, 0))
    return pl.pallas_call(
        _ret_out_kernel,
        grid=(n,),
        in_specs=[pl.BlockSpec(memory_space=pltpu.SMEM),
                  pl.BlockSpec((R, 512), lambda i: (i, 0)),
                  pl.BlockSpec((R, 512), lambda i: (i, 1)),
                  pl.BlockSpec((R, 1024), lambda i: (i, 1)),
                  pl.BlockSpec((R, 1024), lambda i: (i, 2)),
                  st, st],
        out_specs=pl.BlockSpec((R, RET_HEADS * RET_DV), lambda i: (i, 0)),
        out_shape=jax.ShapeDtypeStruct((L, RET_HEADS * RET_DV), BF16),
        compiler_params=_params("parallel"),
        name="ret_out",
    )(dec, proj, proj, proj, proj, sf, sb)


def _attn_kernel(sink_ref, q_ref, kp_ref, kc_ref, kn_ref, vp_ref, vc_ref, vn_ref, ck_ref, cv_ref, o_ref):
    n = pl.program_id(0)
    nblk = pl.num_programs(0)
    B = ATT_BLOCK
    lc = ck_ref.shape[0]
    kj = lax.broadcasted_iota(jnp.int32, (B, B), 0)
    qi = lax.broadcasted_iota(jnp.int32, (B, B), 1)
    ok_prev = jnp.where(n > 0, 0.0, MASK_NEG).astype(F32)
    ok_next = jnp.where(n < nblk - 1, 0.0, MASK_NEG).astype(F32)
    bias_prev = jnp.concatenate([jnp.where(kj >= qi, ok_prev, MASK_NEG).astype(F32)] * ATT_GROUP, axis=1)
    bias_next = jnp.concatenate([jnp.where(kj <= qi, ok_next, MASK_NEG).astype(F32)] * ATT_GROUP, axis=1)
    lane = lax.broadcasted_iota(jnp.int32, (B, LANES), 1)
    lo = lane < 64
    hi = lane >= 64
    def scores(g):
        gs = slice(g * LANES, (g + 1) * LANES)
        kcat = jnp.concatenate([kp_ref[:, gs], kc_ref[:, gs], kn_ref[:, gs], ck_ref[:, gs]], axis=0)
        qs = []
        for r in range(ATT_GROUP):
            h = ATT_GROUP * g + r
            qt = q_ref[:, (h // 2) * LANES:(h // 2 + 1) * LANES]
            keep = lo if h % 2 == 0 else hi
            qs.append(jnp.where(keep, qt * jnp.asarray(ATT_SCALE, BF16), jnp.zeros_like(qt)))
        q4 = jnp.concatenate(qs, axis=0)
        return lax.dot_general(kcat, q4, (((1,), (1,)), ((), ())), preferred_element_type=F32)

    def softmax(g, s):
        sk = jnp.concatenate([jnp.full((1, B), sink_ref[ATT_GROUP * g + r], F32)
                              for r in range(ATT_GROUP)], axis=1)
        s = jnp.concatenate([s[:B] + bias_prev, s[B:2 * B], s[2 * B:3 * B] + bias_next, s[3 * B:]], axis=0)
        m = jnp.maximum(jnp.max(s, axis=0, keepdims=True), sk)
        e = jnp.exp(s - m)
        den = jnp.sum(e, axis=0, keepdims=True) + jnp.exp(sk - m)
        return e.astype(BF16), den

    def values(g, e, den):
        gs = slice(g * LANES, (g + 1) * LANES)
        vcat = jnp.concatenate([vp_ref[:, gs], vc_ref[:, gs], vn_ref[:, gs], cv_ref[:, gs]], axis=0)
        res = lax.dot_general(vcat, e, (((0,), (0,)), ((), ())), preferred_element_type=F32) / den
        for t in range(2):
            even = res[:, (2 * t) * B:(2 * t + 1) * B].T
            odd = res[:, (2 * t + 1) * B:(2 * t + 2) * B].T
            c0 = (2 * g + t) * LANES
            o_ref[:, c0:c0 + LANES] = jnp.where(lo, even, odd).astype(BF16)

    s_next = scores(0)
    pending = None
    for g in range(ATT_KV_HEADS):
        s_cur = s_next
        if g + 1 < ATT_KV_HEADS:
            s_next = scores(g + 1)
        e_den = softmax(g, s_cur)
        if pending is not None:
            values(g - 1, *pending)
        pending = e_den
    values(ATT_KV_HEADS - 1, *pending)


def _attn(sink, proj, kd, vd, ckd, cvd, riders):
    L = proj.shape[0]
    B = ATT_BLOCK
    n = L // B
    lc = ckd.shape[0]
    prev = lambda i: (jnp.maximum(i - 1, 0), 0)
    cur = lambda i: (i, 0)
    nxt = lambda i: (jnp.minimum(i + 1, n - 1), 0)
    kv = lambda f: pl.BlockSpec((B, 512), f)
    full = pl.BlockSpec((lc, 512), lambda i: (0, 0))
    rid_specs, rid_shapes = _rider_specs(riders, n)
    return pl.pallas_call(
        _with_cast_riders(_attn_kernel, 10, 1, len(riders)),
        grid=(n,),
        in_specs=[pl.BlockSpec(memory_space=pltpu.SMEM),
                  pl.BlockSpec((B, 1024), lambda i: (i, 3)),
                  kv(prev), kv(cur), kv(nxt), kv(prev), kv(cur), kv(nxt), full, full] + rid_specs,
        out_specs=[pl.BlockSpec((B, ATT_HEADS * ATT_DH), cur)] + rid_specs,
        out_shape=[jax.ShapeDtypeStruct((L, ATT_HEADS * ATT_DH), BF16)] + rid_shapes,
        compiler_params=_params("parallel"),
        name="attn",
    )(sink, proj, kd, kd, kd, vd, vd, vd, ckd, cvd, *[w for w, _ in riders])


def _out_proj_kernel(yr_ref, ya_ref, w_ref, x_ref, gt_ref, g_ref, sh_ref, sc_ref, o_ref, h_ref):
    kr = yr_ref.shape[1]
    for r in range(yr_ref.shape[0] // OUT_ROW_CHUNK):
        rs = slice(r * OUT_ROW_CHUNK, (r + 1) * OUT_ROW_CHUNK)
        acc = jnp.dot(yr_ref[rs, :], w_ref[:kr, :], preferred_element_type=F32)
        acc = acc + jnp.dot(ya_ref[rs, :], w_ref[kr:, :], preferred_element_type=F32)
        x1 = x_ref[rs, :] + gt_ref[...] * acc
        o_ref[rs, :] = x1
        y = x1 * lax.rsqrt(jnp.mean(x1 * x1, axis=-1, keepdims=True) + NORM_EPS)
        y = y * g_ref[...]
        h_ref[rs, :] = (y * (1.0 + sc_ref[...]) + sh_ref[...]).astype(BF16)


def _out_proj(yr, ya, w, x, gt, g, sh, sc, *, tm):
    m, d = x.shape
    kr, ka = yr.shape[1], ya.shape[1]
    row = lambda i: (i, 0)
    vec = pl.BlockSpec((1, d), lambda i: (0, 0))
    return pl.pallas_call(
        _out_proj_kernel,
        grid=(m // tm,),
        in_specs=[pl.BlockSpec((tm, kr), row), pl.BlockSpec((tm, ka), row),
                  pl.BlockSpec((kr + ka, d), lambda i: (0, 0)),
                  pl.BlockSpec((tm, d), row), vec, vec, vec, vec],
        out_specs=[pl.BlockSpec((tm, d), row), pl.BlockSpec((tm, d), row)],
        out_shape=[jax.ShapeDtypeStruct((m, d), F32), jax.ShapeDtypeStruct((m, d), BF16)],
        compiler_params=_params("parallel"),
        name="out_proj",
    )(yr, ya, w, x, gt, g, sh, sc)


def _ffn_kernel(h_ref, x_ref, gt_ref, gfin_ref, wg_ref, wu_ref, wd_ref, o_ref):
    f = pl.program_id(1)

    @pl.when(f == 0)
    def _():
        o_ref[...] = jnp.zeros_like(o_ref)

    h = h_ref[...]
    a = jnp.dot(h, wg_ref[...], preferred_element_type=F32)
    u = jnp.dot(h, wu_ref[...], preferred_element_type=F32)
    act = ((a / (1.0 + jnp.exp(-a))) * u).astype(BF16)
    o_ref[...] += jnp.dot(act, wd_ref[...], preferred_element_type=F32)

    @pl.when(f == pl.num_programs(1) - 1)
    def _():
        y = x_ref[...] + gt_ref[...] * o_ref[...]
        y = y * lax.rsqrt(jnp.mean(y * y, axis=-1, keepdims=True) + NORM_EPS)
        o_ref[...] = y * gfin_ref[...]


def _ffn(h, x, gt, gfin, wg, wu, wd, *, tm, tf):
    m, d = x.shape
    ff = wg.shape[1]
    assert ff % tf == 0 and m % tm == 0
    row = lambda i, f: (i, 0)
    vec = pl.BlockSpec((1, d), lambda i, f: (0, 0))
    return pl.pallas_call(
        _ffn_kernel,
        grid=(m // tm, ff // tf),
        in_specs=[pl.BlockSpec((tm, d), row), pl.BlockSpec((tm, d), row), vec, vec,
                  pl.BlockSpec((d, tf), lambda i, f: (0, f)),
                  pl.BlockSpec((d, tf), lambda i, f: (0, f)),
                  pl.BlockSpec((tf, d), lambda i, f: (f, 0))],
        out_specs=pl.BlockSpec((tm, d), row),
        out_shape=jax.ShapeDtypeStruct((m, d), F32),
        compiler_params=_params("parallel", "arbitrary"),
        name="ffn",
    )(h, x, gt, gfin, wg, wu, wd)


def _rope_tables(L):
    f32 = np.float32
    lane = np.arange(LANES)
    inv1 = f32(ROPE_BASE) ** (-np.arange(32, dtype=f32) / f32(32))
    ang1 = np.arange(L, dtype=f32)[:, None] * inv1[None, :]
    sgn1 = np.where((lane % 64) < 32, -1.0, 1.0).astype(f32)
    cos1 = np.tile(np.cos(ang1), (1, LANES // 32))
    sin1 = np.tile(np.sin(ang1), (1, LANES // 32)) * sgn1[None, :]
    inv2 = f32(ROPE_BASE) ** (-np.arange(16, dtype=f32) / f32(16))
    nrow = L // GRID_W
    ang_r = np.arange(nrow, dtype=f32)[:, None] * inv2[None, :]
    ang_c = np.arange(GRID_W, dtype=f32)[:, None] * inv2[None, :]
    sgna = np.where((lane % 32) < 16, -1.0, 1.0).astype(f32)

    def expand(fr, fc):
        by_row = np.broadcast_to(np.tile(fr, (1, 2))[:, None, :], (nrow, GRID_W, 32))
        by_col = np.broadcast_to(np.tile(fc, (1, 2))[None, :, :], (nrow, GRID_W, 32))
        head = np.concatenate([by_row, by_col], axis=-1).reshape(L, 64)
        return np.tile(head, (1, LANES // 64))

    cosa = expand(np.cos(ang_r), np.cos(ang_c))
    sina = expand(np.sin(ang_r), np.sin(ang_c)) * sgna[None, :]
    return tuple(np.ascontiguousarray(t, dtype=f32) for t in (cos1, sin1, cosa, sina))


def kernel(x, c, ctx, c_ctx, w_mod, b_mod, norm_mix, norm_ffn, w_in, ret_decay, attn_sink,
           w_out, w_gate, w_up, w_down, norm_final):
    B, L, D = x.shape
    assert B == 1 and w_mod.shape[0] == 1, "single batch element, depth-1 layer"
    x2 = x[0]
    xc2 = ctx[0]

    cv = jnp.zeros((8, D), F32).at[0].set(c[0]).at[1].set(c_ctx)
    mod = _mod(cv, w_mod[0], b_mod[0][None, :])
    sh_m, sc_m, gt_m, sh_f, sc_f, gt_f = [mod[0:1, k * D:(k + 1) * D] for k in range(6)]
    sh_mc, sc_mc = mod[1:2, 0:D], mod[1:2, D:2 * D]

    g_mix = norm_mix[0][None, :]
    cproj, ckd, cvd, w_in_b = _ctx_proj(xc2, g_mix, sh_mc, sc_mc, w_in[0])
    proj, kd, vd = _in_proj(x2, g_mix, sh_m, sc_m, w_in_b, _rope_tables(L), tm=512)

    dec = ret_decay[0].astype(F32)
    sf, sb = _ret_states(dec, proj, cproj)
    y_ret = _ret_out(dec, proj, sf, sb)
    y_att, w_gate_b, w_up_b, w_down_b, w_out_b = _attn(
        attn_sink[0].astype(F32), proj, kd, vd, ckd, cvd,
        [(w_gate[0], 1), (w_up[0], 1), (w_down[0], 2), (w_out[0], 1)])

    x1, hff = _out_proj(y_ret, y_att, w_out_b, x2, gt_m, norm_ffn[0][None, :], sh_f, sc_f, tm=512)
    out = _ffn(hff, x1, gt_f, norm_final[None, :], w_gate_b, w_up_b, w_down_b, tm=512, tf=512)
    return out[None]
```

```python
import jax
import jax.numpy as jnp
import numpy as np
from jax import lax
from jax.experimental import pallas as pl
from jax.experimental.pallas import tpu as pltpu

GRID_W = 64
RET_HEADS = 8
RET_DK = 64
RET_DV = 128
RET_CHUNK = 128
ATT_HEADS = 16
ATT_KV_HEADS = 4
ATT_DH = 64
ATT_GROUP = ATT_HEADS // ATT_KV_HEADS
WINDOW = 128
ATT_BLOCK = 128
ROPE_BASE = 10000.0
NORM_EPS = 1e-6
K_SCALE = RET_DK ** -0.5
ATT_SCALE = ATT_DH ** -0.5

LANES = 128
RET_PAIRS = RET_HEADS // 2
MASK_NEG = -1e30
VMEM_LIMIT = 56 * 1024 * 1024
RET_STEP_CHUNKS = 4
OUT_ROW_CHUNK = 256
IN_ROW_CHUNK = 256
FFN_TILE = 512
FFN_SLOTS = 3
FFN_ROW_CHUNK = 256

BF16 = jnp.bfloat16
F32 = jnp.float32


def _params(*sem):
    return pltpu.CompilerParams(dimension_semantics=sem, vmem_limit_bytes=VMEM_LIMIT)


def _with_cast_riders(body, n_in, n_out, n_rid):
    def wrapped(*refs):
        ins = refs[:n_in]
        rid_in = refs[n_in:n_in + n_rid]
        outs = refs[n_in + n_rid:n_in + n_rid + n_out]
        rid_out = refs[n_in + n_rid + n_out:n_in + 2 * n_rid + n_out]
        scratch = refs[n_in + 2 * n_rid + n_out:]
        for src, dst in zip(rid_in, rid_out):
            dst[...] = src[...].astype(BF16)
        body(*ins, *outs, *scratch)
    return wrapped


def _rider_specs(riders, steps):
    specs, shapes = [], []
    for w, ncb in riders:
        rows, cols = w.shape
        nrb = steps // ncb
        assert nrb * ncb == steps and rows % nrb == 0 and cols % ncb == 0
        br, bc = rows // nrb, cols // ncb
        assert br % 16 == 0 and bc % LANES == 0, "slab must be bf16-tile aligned"
        specs.append(pl.BlockSpec((br, bc), lambda i, ncb=ncb: (i // ncb, i % ncb)))
        shapes.append(jax.ShapeDtypeStruct(w.shape, BF16))
    return specs, shapes


def _mod_kernel(cv_ref, w_ref, b_ref, o_ref):
    cv = cv_ref[...]
    s = cv / (1.0 + jnp.exp(-cv))
    o_ref[...] = jnp.dot(s.astype(BF16), w_ref[...].astype(BF16),
                         preferred_element_type=F32) + b_ref[...]


def _mod(cv, w, b):
    d, n = w.shape
    tn = 1024
    return pl.pallas_call(
        _mod_kernel,
        grid=(n // tn,),
        in_specs=[pl.BlockSpec((8, d), lambda j: (0, 0)),
                  pl.BlockSpec((d, tn), lambda j: (0, j)),
                  pl.BlockSpec((1, tn), lambda j: (0, j))],
        out_specs=pl.BlockSpec((8, tn), lambda j: (0, j)),
        out_shape=jax.ShapeDtypeStruct((8, n), F32),
        compiler_params=_params("parallel"),
        name="mod",
    )(cv, w, b)


def _rot_pairs(a, cos, sin_signed, half):
    lane = lax.broadcasted_iota(jnp.int32, a.shape, 1)
    first = (lane % (2 * half)) < half
    rot = jnp.where(first, pltpu.roll(a, LANES - half, 1), pltpu.roll(a, half, 1))
    return a * cos + rot * sin_signed


def _dup_halves(a):
    lane = lax.broadcasted_iota(jnp.int32, a.shape, 1)
    r = pltpu.roll(a, 64, 1)
    lo = lane < 64
    return jnp.where(lo, a, r), jnp.where(lo, r, a)


_PROJ_TILE = 512
_PROJ_TILE_KINDS = ("ret_q", "ret_k", "plain", "plain", "plain", "plain", "att_q", "att_q", "att_kv")


def _in_proj_kernel(x_ref, g_ref, sh_ref, sc_ref, w_ref, c1_ref, s1_ref, ca_ref, sa_ref,
                    o_ref, kd_ref, vd_ref):
    tn = _PROJ_TILE
    for r in range(x_ref.shape[0] // IN_ROW_CHUNK):
        rs = slice(r * IN_ROW_CHUNK, (r + 1) * IN_ROW_CHUNK)
        xf = x_ref[rs, :]
        y = xf * lax.rsqrt(jnp.mean(xf * xf, axis=-1, keepdims=True) + NORM_EPS)
        y = y * g_ref[...]
        h = (y * (1.0 + sc_ref[...]) + sh_ref[...]).astype(BF16)

        def rope1(a):
            return _rot_pairs(a, c1_ref[rs, :], s1_ref[rs, :], 32)

        def ropea(a):
            return _rot_pairs(a, ca_ref[rs, :], sa_ref[rs, :], 16)

        for j, kind in enumerate(_PROJ_TILE_KINDS):
            acc = jnp.dot(h, w_ref[:, j * tn:(j + 1) * tn], preferred_element_type=F32)
            for c in range(tn // LANES):
                a = acc[:, c * LANES:(c + 1) * LANES]
                if kind == "ret_q":
                    a = rope1(a)
                elif kind == "ret_k":
                    a = rope1(a) * K_SCALE
                elif kind == "att_q" or (kind == "att_kv" and c < 2):
                    a = ropea(a)
                o_ref[rs, j * tn + c * LANES:j * tn + (c + 1) * LANES] = a.astype(BF16)
                if kind == "att_kv":
                    dup_ref = kd_ref if c < 2 else vd_ref
                    d0, d1 = _dup_halves(a)
                    t = 2 * (c % 2)
                    dup_ref[rs, t * LANES:(t + 1) * LANES] = d0.astype(BF16)
                    dup_ref[rs, (t + 1) * LANES:(t + 2) * LANES] = d1.astype(BF16)


def _in_proj(x, g, sh, sc, w, tabs, *, tm):
    m, d = x.shape
    n = w.shape[1]
    assert n == _PROJ_TILE * len(_PROJ_TILE_KINDS) and m % tm == 0 and tm % IN_ROW_CHUNK == 0
    c1, s1, ca, sa = tabs
    row = lambda i: (i, 0)
    vec = pl.BlockSpec((1, d), lambda i: (0, 0))
    tab = pl.BlockSpec((tm, LANES), row)
    return pl.pallas_call(
        _in_proj_kernel,
        grid=(m // tm,),
        in_specs=[pl.BlockSpec((tm, d), row), vec, vec, vec,
                  pl.BlockSpec((d, n), lambda i: (0, 0), pipeline_mode=pl.Buffered(1)),
                  tab, tab, tab, tab],
        out_specs=[pl.BlockSpec((tm, n), row),
                   pl.BlockSpec((tm, 512), row),
                   pl.BlockSpec((tm, 512), row)],
        out_shape=[jax.ShapeDtypeStruct((m, n), BF16),
                   jax.ShapeDtypeStruct((m, 512), BF16),
                   jax.ShapeDtypeStruct((m, 512), BF16)],
        compiler_params=_params("parallel"),
        name="in_proj",
    )(x, g, sh, sc, w, c1, s1, ca, sa)


def _ctx_proj_kernel(x_ref, g_ref, sh_ref, sc_ref, w_ref, o_ref, kd_ref, vd_ref, wb_ref, h_ref):
    j = pl.program_id(0)

    @pl.when(j == 0)
    def _():
        xf = x_ref[...]
        y = xf * lax.rsqrt(jnp.mean(xf * xf, axis=-1, keepdims=True) + NORM_EPS)
        y = y * g_ref[...]
        h_ref[...] = (y * (1.0 + sc_ref[...]) + sh_ref[...]).astype(BF16)

    wb = w_ref[...].astype(BF16)
    wb_ref[...] = wb
    acc = jnp.dot(h_ref[...], wb, preferred_element_type=F32)
    is_ret_k = _PROJ_TILE_KINDS.index("ret_k")
    o_ref[...] = (acc * jnp.where(j == is_ret_k, K_SCALE, 1.0)).astype(BF16)

    @pl.when(j == _PROJ_TILE_KINDS.index("att_kv"))
    def _():
        for c in range(_PROJ_TILE // LANES):
            dup_ref = kd_ref if c < 2 else vd_ref
            d0, d1 = _dup_halves(acc[:, c * LANES:(c + 1) * LANES])
            t = 2 * (c % 2)
            dup_ref[:, t * LANES:(t + 1) * LANES] = d0.astype(BF16)
            dup_ref[:, (t + 1) * LANES:(t + 2) * LANES] = d1.astype(BF16)


def _ctx_proj(x, g, sh, sc, w):
    m, d = x.shape
    n = w.shape[1]
    tn = _PROJ_TILE
    assert n == tn * len(_PROJ_TILE_KINDS)
    fixed = lambda j: (0, 0)
    vec = pl.BlockSpec((1, d), fixed)
    return pl.pallas_call(
        _ctx_proj_kernel,
        grid=(n // tn,),
        in_specs=[pl.BlockSpec((m, d), fixed), vec, vec, vec,
                  pl.BlockSpec((d, tn), lambda j: (0, j))],
        out_specs=[pl.BlockSpec((m, tn), lambda j: (0, j)),
                   pl.BlockSpec((m, 512), fixed),
                   pl.BlockSpec((m, 512), fixed),
                   pl.BlockSpec((d, tn), lambda j: (0, j))],
        out_shape=[jax.ShapeDtypeStruct((m, n), BF16),
                   jax.ShapeDtypeStruct((m, 512), BF16),
                   jax.ShapeDtypeStruct((m, 512), BF16),
                   jax.ShapeDtypeStruct((d, n), BF16)],
        scratch_shapes=[pltpu.VMEM((m, d), BF16)],
        compiler_params=_params("arbitrary"),
        name="ctx_proj",
    )(x, g, sh, sc, w)


def _pair_lg(dec_ref, d, p, shape):
    lane = lax.broadcasted_iota(jnp.int32, shape, 1)
    first = (lane % LANES) < 64
    raw = jnp.where(first, jnp.full(shape, dec_ref[d, 2 * p], F32), jnp.full(shape, dec_ref[d, 2 * p + 1], F32))
    return -jnp.exp(raw)


def _head_block_mask(shape):
    r = lax.broadcasted_iota(jnp.int32, shape, 0)
    c = lax.broadcasted_iota(jnp.int32, shape, 1)
    return (r // 64) == (c // LANES)


def _kv_pair(k_pair, v_pair, w):
    kw = (k_pair.astype(F32) * w).astype(BF16)
    kv = lax.dot_general(kw, v_pair, (((0,), (0,)), ((), ())), preferred_element_type=F32)
    return jnp.where(_head_block_mask(kv.shape), kv, 0.0)


def _ret_state_kernel(dec_ref, kf_ref, vf_ref, kb_ref, vb_ref, ck_ref, cv_ref,
                      sf_ref, sb_ref, sfs, sbs):
    i = pl.program_id(0)
    C = RET_CHUNK
    lc = ck_ref.shape[0]

    @pl.when(i == 0)
    def _():
        pos = lax.broadcasted_iota(jnp.int32, (lc, LANES), 0).astype(F32)
        for p in range(RET_PAIRS):
            ks = slice(p * LANES, (p + 1) * LANES)
            vs = slice(p * 2 * RET_DV, (p + 1) * 2 * RET_DV)
            wf = jnp.exp(_pair_lg(dec_ref, 0, p, (lc, LANES)) * (lc - 1.0 - pos))
            wb = jnp.exp(_pair_lg(dec_ref, 1, p, (lc, LANES)) * pos)
            sfs[p] = _kv_pair(ck_ref[:, ks], cv_ref[:, vs], wf)
            sbs[p] = _kv_pair(ck_ref[:, ks], cv_ref[:, vs], wb)

    pos = lax.broadcasted_iota(jnp.int32, (C, LANES), 0).astype(F32)
    for p in range(RET_PAIRS):
        ks = slice(p * LANES, (p + 1) * LANES)
        vs = slice(p * 2 * RET_DV, (p + 1) * 2 * RET_DV)
        wf = jnp.exp(_pair_lg(dec_ref, 0, p, (C, LANES)) * (C - 1.0 - pos))
        wb = jnp.exp(_pair_lg(dec_ref, 1, p, (C, LANES)) * pos)
        rowh = lax.broadcasted_iota(jnp.int32, (LANES, 2 * RET_DV), 0) < 64
        gf = jnp.exp(-jnp.exp(jnp.where(rowh, jnp.full(rowh.shape, dec_ref[0, 2 * p], F32),
                                         jnp.full(rowh.shape, dec_ref[0, 2 * p + 1], F32))) * float(C))
        gb = jnp.exp(-jnp.exp(jnp.where(rowh, jnp.full(rowh.shape, dec_ref[1, 2 * p], F32),
                                         jnp.full(rowh.shape, dec_ref[1, 2 * p + 1], F32))) * float(C))
        sf = sfs[p]
        for cc in range(RET_STEP_CHUNKS):
            rs = slice(cc * C, (cc + 1) * C)
            sf_ref[cc, p] = sf.astype(BF16)
            sf = gf * sf + _kv_pair(kf_ref[rs, ks], vf_ref[rs, vs], wf)
        sfs[p] = sf
        sb = sbs[p]
        for cc in reversed(range(RET_STEP_CHUNKS)):
            rs = slice(cc * C, (cc + 1) * C)
            sb_ref[cc, p] = sb.astype(BF16)
            sb = gb * sb + _kv_pair(kb_ref[rs, ks], vb_ref[rs, vs], wb)
        sbs[p] = sb


def _ret_states(dec, proj, cproj):
    L = proj.shape[0]
    lc = cproj.shape[0]
    S = RET_STEP_CHUNKS
    R = S * RET_CHUNK
    n = L // R
    st = pl.BlockSpec((S, RET_PAIRS, LANES, 2 * RET_DV), lambda i: (i, 0, 0, 0))
    st_rev = pl.BlockSpec((S, RET_PAIRS, LANES, 2 * RET_DV), lambda i: (n - 1 - i, 0, 0, 0))
    shp = jax.ShapeDtypeStruct((n * S, RET_PAIRS, LANES, 2 * RET_DV), BF16)
    return pl.pallas_call(
        _ret_state_kernel,
        grid=(n,),
        in_specs=[pl.BlockSpec(memory_space=pltpu.SMEM),
                  pl.BlockSpec((R, 512), lambda i: (i, 1)),
                  pl.BlockSpec((R, 1024), lambda i: (i, 1)),
                  pl.BlockSpec((R, 512), lambda i: (n - 1 - i, 1)),
                  pl.BlockSpec((R, 1024), lambda i: (n - 1 - i, 1)),
                  pl.BlockSpec((lc, 512), lambda i: (0, 1)),
                  pl.BlockSpec((lc, 1024), lambda i: (0, 1))],
        out_specs=[st, st_rev],
        out_shape=[shp, shp],
        scratch_shapes=[pltpu.VMEM((RET_PAIRS, LANES, 2 * RET_DV), F32),
                        pltpu.VMEM((RET_PAIRS, LANES, 2 * RET_DV), F32)],
        compiler_params=_params("arbitrary"),
        name="ret_state",
    )(dec, proj, proj, proj, proj, cproj, cproj)


def _ret_out_kernel(dec_ref, q_ref, k_ref, v_ref, g_ref, sf_ref, sb_ref, o_ref):
    C = RET_CHUNK
    pos = lax.broadcasted_iota(jnp.int32, (C, LANES), 0).astype(F32)
    n_i = lax.broadcasted_iota(jnp.int32, (C, 2 * C), 0)
    m_i = lax.broadcasted_iota(jnp.int32, (C, 2 * C), 1) % C
    rel = (n_i - m_i).astype(F32)
    lane = lax.broadcasted_iota(jnp.int32, (C, LANES), 1)
    lo = lane < 64
    for p in range(RET_PAIRS):
        ks = slice(p * LANES, (p + 1) * LANES)
        vs = slice(p * 2 * RET_DV, (p + 1) * 2 * RET_DV)
        col_a = lax.broadcasted_iota(jnp.int32, (C, 2 * C), 1) < C
        raw_f = jnp.where(col_a, jnp.full((C, 2 * C), dec_ref[0, 2 * p], F32), jnp.full((C, 2 * C), dec_ref[0, 2 * p + 1], F32))
        raw_b = jnp.where(col_a, jnp.full((C, 2 * C), dec_ref[1, 2 * p], F32), jnp.full((C, 2 * C), dec_ref[1, 2 * p + 1], F32))
        dmat = jnp.where(rel >= 0, jnp.exp(-jnp.exp(raw_f) * jnp.maximum(rel, 0.0)),
                         jnp.exp(-jnp.exp(raw_b) * jnp.maximum(-rel, 0.0)))
        wqf = jnp.exp(_pair_lg(dec_ref, 0, p, (C, LANES)) * (pos + 1.0))
        wqb = jnp.exp(_pair_lg(dec_ref, 1, p, (C, LANES)) * (float(C) - pos))
        for cc in range(RET_STEP_CHUNKS):
            rs = slice(cc * C, (cc + 1) * C)
            q = q_ref[rs, ks]
            k = k_ref[rs, ks]
            v = v_ref[rs, vs]
            zk = jnp.zeros_like(k)
            kst = jnp.concatenate([jnp.where(lo, k, zk), jnp.where(lo, zk, k)], axis=0)
            s = lax.dot_general(q, kst, (((1,), (1,)), ((), ())), preferred_element_type=F32)
            sd = (s * dmat).astype(BF16)
            qf32 = q.astype(F32)
            qwf = (qf32 * wqf).astype(BF16)
            qwb = (qf32 * wqb).astype(BF16)
            zv = jnp.zeros((C, RET_DV), BF16)
            vbd = jnp.concatenate([jnp.concatenate([v[:, :RET_DV], zv], axis=1),
                                   jnp.concatenate([zv, v[:, RET_DV:]], axis=1)], axis=0)
            lhs = jnp.concatenate([sd, qwf, qwb], axis=1)
            rhs = jnp.concatenate([vbd, sf_ref[cc, p], sb_ref[cc, p]], axis=0)
            o = jnp.dot(lhs, rhs, preferred_element_type=F32)
            for t in range(2):
                oh = o[:, t * RET_DV:(t + 1) * RET_DV]
                oh = oh * lax.rsqrt(jnp.mean(oh * oh, axis=-1, keepdims=True) + NORM_EPS)
                cs = slice(p * 2 * RET_DV + t * RET_DV, p * 2 * RET_DV + (t + 1) * RET_DV)
                gt = g_ref[rs, cs].astype(F32)
                o_ref[rs, cs] = (oh * (gt / (1.0 + jnp.exp(-gt)))).astype(BF16)


def _ret_out(dec, proj, sf, sb):
    L = proj.shape[0]
    S = RET_STEP_CHUNKS
    R = S * RET_CHUNK
    n = L // R
    st = pl.BlockSpec((S, RET_PAIRS, LANES, 2 * RET_DV), lambda i: (i, 0, 0, 0))
    return pl.pallas_call(
        _ret_out_kernel,
        grid=(n,),
        in_specs=[pl.BlockSpec(memory_space=pltpu.SMEM),
                  pl.BlockSpec((R, 512), lambda i: (i, 0)),
                  pl.BlockSpec((R, 512), lambda i: (i, 1)),
                  pl.BlockSpec((R, 1024), lambda i: (i, 1)),
                  pl.BlockSpec((R, 1024), lambda i: (i, 2)),
                  st, st],
        out_specs=pl.BlockSpec((R, RET_HEADS * RET_DV), lambda i: (i, 0)),
        out_shape=jax.ShapeDtypeStruct((L, RET_HEADS * RET_DV), BF16),
        compiler_params=_params("parallel"),
        name="ret_out",
    )(dec, proj, proj, proj, proj, sf, sb)


def _attn_kernel(sink_ref, q_ref, kp_ref, kc_ref, kn_ref, vp_ref, vc_ref, vn_ref, ck_ref, cv_ref, o_ref):
    n = pl.program_id(0)
    nblk = pl.num_programs(0)
    B = ATT_BLOCK
    lc = ck_ref.shape[0]
    kj = lax.broadcasted_iota(jnp.int32, (B, B), 0)
    qi = lax.broadcasted_iota(jnp.int32, (B, B), 1)
    ok_prev = jnp.where(n > 0, 0.0, MASK_NEG).astype(F32)
    ok_next = jnp.where(n < nblk - 1, 0.0, MASK_NEG).astype(F32)
    bias_prev = jnp.concatenate([jnp.where(kj >= qi, ok_prev, MASK_NEG).astype(F32)] * ATT_GROUP, axis=1)
    bias_next = jnp.concatenate([jnp.where(kj <= qi, ok_next, MASK_NEG).astype(F32)] * ATT_GROUP, axis=1)
    lane = lax.broadcasted_iota(jnp.int32, (B, LANES), 1)
    lo = lane < 64
    hi = lane >= 64
    def scores(g):
        gs = slice(g * LANES, (g + 1) * LANES)
        kcat = jnp.concatenate([kp_ref[:, gs], kc_ref[:, gs], kn_ref[:, gs], ck_ref[:, gs]], axis=0)
        qs = []
        for r in range(ATT_GROUP):
            h = ATT_GROUP * g + r
            qt = q_ref[:, (h // 2) * LANES:(h // 2 + 1) * LANES]
            keep = lo if h % 2 == 0 else hi
            qs.append(jnp.where(keep, qt * jnp.asarray(ATT_SCALE, BF16), jnp.zeros_like(qt)))
        q4 = jnp.concatenate(qs, axis=0)
        return lax.dot_general(kcat, q4, (((1,), (1,)), ((), ())), preferred_element_type=F32)

    def softmax(g, s):
        sk = jnp.concatenate([jnp.full((1, B), sink_ref[ATT_GROUP * g + r], F32)
                              for r in range(ATT_GROUP)], axis=1)
        s = jnp.concatenate([s[:B] + bias_prev, s[B:2 * B], s[2 * B:3 * B] + bias_next, s[3 * B:]], axis=0)
        m = jnp.maximum(jnp.max(s, axis=0, keepdims=True), sk)
        e = jnp.exp(s - m)
        den = jnp.sum(e, axis=0, keepdims=True) + jnp.exp(sk - m)
        return e.astype(BF16), den

    def values(g, e, den):
        gs = slice(g * LANES, (g + 1) * LANES)
        vcat = jnp.concatenate([vp_ref[:, gs], vc_ref[:, gs], vn_ref[:, gs], cv_ref[:, gs]], axis=0)
        res = lax.dot_general(vcat, e, (((0,), (0,)), ((), ())), preferred_element_type=F32) / den
        for t in range(2):
            even = res[:, (2 * t) * B:(2 * t + 1) * B].T
            odd = res[:, (2 * t + 1) * B:(2 * t + 2) * B].T
            c0 = (2 * g + t) * LANES
            o_ref[:, c0:c0 + LANES] = jnp.where(lo, even, odd).astype(BF16)

    s_next = scores(0)
    pending = None
    for g in range(ATT_KV_HEADS):
        s_cur = s_next
        if g + 1 < ATT_KV_HEADS:
            s_next = scores(g + 1)
        e_den = softmax(g, s_cur)
        if pending is not None:
            values(g - 1, *pending)
        pending = e_den
    values(ATT_KV_HEADS - 1, *pending)


def _attn(sink, proj, kd, vd, ckd, cvd, riders):
    L = proj.shape[0]
    B = ATT_BLOCK
    n = L // B
    lc = ckd.shape[0]
    prev = lambda i: (jnp.maximum(i - 1, 0), 0)
    cur = lambda i: (i, 0)
    nxt = lambda i: (jnp.minimum(i + 1, n - 1), 0)
    kv = lambda f: pl.BlockSpec((B, 512), f)
    full = pl.BlockSpec((lc, 512), lambda i: (0, 0))
    rid_specs, rid_shapes = _rider_specs(riders, n)
    return pl.pallas_call(
        _with_cast_riders(_attn_kernel, 10, 1, len(riders)),
        grid=(n,),
        in_specs=[pl.BlockSpec(memory_space=pltpu.SMEM),
                  pl.BlockSpec((B, 1024), lambda i: (i, 3)),
                  kv(prev), kv(cur), kv(nxt), kv(prev), kv(cur), kv(nxt), full, full] + rid_specs,
        out_specs=[pl.BlockSpec((B, ATT_HEADS * ATT_DH), cur)] + rid_specs,
        out_shape=[jax.ShapeDtypeStruct((L, ATT_HEADS * ATT_DH), BF16)] + rid_shapes,
        compiler_params=_params("parallel"),
        name="attn",
    )(sink, proj, kd, kd, kd, vd, vd, vd, ckd, cvd, *[w for w, _ in riders])


def _out_proj_kernel(yr_ref, ya_ref, w_ref, x_ref, gt_ref, g_ref, sh_ref, sc_ref, o_ref, h_ref):
    kr = yr_ref.shape[1]
    for r in range(yr_ref.shape[0] // OUT_ROW_CHUNK):
        rs = slice(r * OUT_ROW_CHUNK, (r + 1) * OUT_ROW_CHUNK)
        acc = jnp.dot(yr_ref[rs, :], w_ref[:kr, :], preferred_element_type=F32)
        acc = acc + jnp.dot(ya_ref[rs, :], w_ref[kr:, :], preferred_element_type=F32)
        x1 = x_ref[rs, :] + gt_ref[...] * acc
        o_ref[rs, :] = x1
        y = x1 * lax.rsqrt(jnp.mean(x1 * x1, axis=-1, keepdims=True) + NORM_EPS)
        y = y * g_ref[...]
        h_ref[rs, :] = (y * (1.0 + sc_ref[...]) + sh_ref[...]).astype(BF16)


def _out_proj(yr, ya, w, x, gt, g, sh, sc, *, tm):
    m, d = x.shape
    kr, ka = yr.shape[1], ya.shape[1]
    row = lambda i: (i, 0)
    vec = pl.BlockSpec((1, d), lambda i: (0, 0))
    return pl.pallas_call(
        _out_proj_kernel,
        grid=(m // tm,),
        in_specs=[pl.BlockSpec((tm, kr), row), pl.BlockSpec((tm, ka), row),
                  pl.BlockSpec((kr + ka, d), lambda i: (0, 0)),
                  pl.BlockSpec((tm, d), row), vec, vec, vec, vec],
        out_specs=[pl.BlockSpec((tm, d), row), pl.BlockSpec((tm, d), row)],
        out_shape=[jax.ShapeDtypeStruct((m, d), F32), jax.ShapeDtypeStruct((m, d), BF16)],
        compiler_params=_params("parallel"),
        name="out_proj",
    )(yr, ya, w, x, gt, g, sh, sc)


def _ffn_weight_copies(wg_hbm, wu_hbm, wd_hbm, wg_buf, wu_buf, wd_buf, sem, f, slot):
    cols = slice(f * FFN_TILE, (f + 1) * FFN_TILE)
    return (pltpu.make_async_copy(wg_hbm.at[:, cols], wg_buf.at[slot], sem.at[0, slot]),
            pltpu.make_async_copy(wu_hbm.at[:, cols], wu_buf.at[slot], sem.at[1, slot]),
            pltpu.make_async_copy(wd_hbm.at[cols, :], wd_buf.at[slot], sem.at[2, slot]))


def _ffn_kernel(h_ref, x_ref, gt_ref, gfin_ref, wg_hbm, wu_hbm, wd_hbm, o_ref, wg_buf, wu_buf, wd_buf, sem):
    i = pl.program_id(0)
    nf = wg_hbm.shape[1] // FFN_TILE
    assert nf % FFN_SLOTS != 1, "the wrap-around prefetch would overwrite the tile being consumed"

    def copies(f, slot):
        return _ffn_weight_copies(wg_hbm, wu_hbm, wd_hbm, wg_buf, wu_buf, wd_buf, sem, f, slot)

    @pl.when(i == 0)
    def _():
        for cp in copies(0, 0):
            cp.start()

    rows = h_ref.shape[0]
    for f in range(nf):
        slot = f % FFN_SLOTS
        nxt = (f + 1) % nf
        for cp in copies(nxt, nxt % FFN_SLOTS):
            cp.start()
        for cp in copies(f, slot):
            cp.wait()
        for r in range(rows // FFN_ROW_CHUNK):
            rs = slice(r * FFN_ROW_CHUNK, (r + 1) * FFN_ROW_CHUNK)
            h = h_ref[rs, :]
            a = jnp.dot(h, wg_buf[slot], preferred_element_type=F32)
            u = jnp.dot(h, wu_buf[slot], preferred_element_type=F32)
            act = ((a / (1.0 + jnp.exp(-a))) * u).astype(BF16)
            part = jnp.dot(act, wd_buf[slot], preferred_element_type=F32)
            if f == 0:
                o_ref[rs, :] = part
            else:
                o_ref[rs, :] += part

    for r in range(rows // FFN_ROW_CHUNK):
        rs = slice(r * FFN_ROW_CHUNK, (r + 1) * FFN_ROW_CHUNK)
        y = x_ref[rs, :] + gt_ref[...] * o_ref[rs, :]
        y = y * lax.rsqrt(jnp.mean(y * y, axis=-1, keepdims=True) + NORM_EPS)
        o_ref[rs, :] = y * gfin_ref[...]

    @pl.when(i == pl.num_programs(0) - 1)
    def _():
        for cp in copies(0, 0):
            cp.wait()


def _ffn(h, x, gt, gfin, wg, wu, wd, *, tm):
    m, d = x.shape
    ff = wg.shape[1]
    assert ff % FFN_TILE == 0 and m % tm == 0 and tm % FFN_ROW_CHUNK == 0
    row = lambda i: (i, 0)
    vec = pl.BlockSpec((1, d), lambda i: (0, 0))
    hbm = pl.BlockSpec(memory_space=pl.ANY)
    return pl.pallas_call(
        _ffn_kernel,
        grid=(m // tm,),
        in_specs=[pl.BlockSpec((tm, d), row), pl.BlockSpec((tm, d), row), vec, vec, hbm, hbm, hbm],
        out_specs=pl.BlockSpec((tm, d), row),
        out_shape=jax.ShapeDtypeStruct((m, d), F32),
        scratch_shapes=[pltpu.VMEM((FFN_SLOTS, d, FFN_TILE), BF16),
                        pltpu.VMEM((FFN_SLOTS, d, FFN_TILE), BF16),
                        pltpu.VMEM((FFN_SLOTS, FFN_TILE, d), BF16),
                        pltpu.SemaphoreType.DMA((3, FFN_SLOTS))],
        compiler_params=_params("arbitrary"),
        name="ffn",
    )(h, x, gt, gfin, wg, wu, wd)


def _rope_tables(L):
    f32 = np.float32
    lane = np.arange(LANES)
    inv1 = f32(ROPE_BASE) ** (-np.arange(32, dtype=f32) / f32(32))
    ang1 = np.arange(L, dtype=f32)[:, None] * inv1[None, :]
    sgn1 = np.where((lane % 64) < 32, -1.0, 1.0).astype(f32)
    cos1 = np.tile(np.cos(ang1), (1, LANES // 32))
    sin1 = np.tile(np.sin(ang1), (1, LANES // 32)) * sgn1[None, :]
    inv2 = f32(ROPE_BASE) ** (-np.arange(16, dtype=f32) / f32(16))
    nrow = L // GRID_W
    ang_r = np.arange(nrow, dtype=f32)[:, None] * inv2[None, :]
    ang_c = np.arange(GRID_W, dtype=f32)[:, None] * inv2[None, :]
    sgna = np.where((lane % 32) < 16, -1.0, 1.0).astype(f32)

    def expand(fr, fc):
        by_row = np.broadcast_to(np.tile(fr, (1, 2))[:, None, :], (nrow, GRID_W, 32))
        by_col = np.broadcast_to(np.tile(fc, (1, 2))[None, :, :], (nrow, GRID_W, 32))
        head = np.concatenate([by_row, by_col], axis=-1).reshape(L, 64)
        return np.tile(head, (1, LANES // 64))

    cosa = expand(np.cos(ang_r), np.cos(ang_c))
    sina = expand(np.sin(ang_r), np.sin(ang_c)) * sgna[None, :]
    return tuple(np.ascontiguousarray(t, dtype=f32) for t in (cos1, sin1, cosa, sina))


def kernel(x, c, ctx, c_ctx, w_mod, b_mod, norm_mix, norm_ffn, w_in, ret_decay, attn_sink,
           w_out, w_gate, w_up, w_down, norm_final):
    B, L, D = x.shape
    assert B == 1 and w_mod.shape[0] == 1, "single batch element, depth-1 layer"
    x2 = x[0]
    xc2 = ctx[0]

    cv = jnp.zeros((8, D), F32).at[0].set(c[0]).at[1].set(c_ctx)
    mod = _mod(cv, w_mod[0], b_mod[0][None, :])
    sh_m, sc_m, gt_m, sh_f, sc_f, gt_f = [mod[0:1, k * D:(k + 1) * D] for k in range(6)]
    sh_mc, sc_mc = mod[1:2, 0:D], mod[1:2, D:2 * D]

    g_mix = norm_mix[0][None, :]
    cproj, ckd, cvd, w_in_b = _ctx_proj(xc2, g_mix, sh_mc, sc_mc, w_in[0])
    proj, kd, vd = _in_proj(x2, g_mix, sh_m, sc_m, w_in_b, _rope_tables(L), tm=512)

    dec = ret_decay[0].astype(F32)
    sf, sb = _ret_states(dec, proj, cproj)
    y_ret = _ret_out(dec, proj, sf, sb)
    y_att, w_gate_b, w_up_b, w_down_b, w_out_b = _attn(
        attn_sink[0].astype(F32), proj, kd, vd, ckd, cvd,
        [(w_gate[0], 1), (w_up[0], 1), (w_down[0], 2), (w_out[0], 1)])

    x1, hff = _out_proj(y_ret, y_att, w_out_b, x2, gt_m, norm_ffn[0][None, :], sh_f, sc_f, tm=512)
    out = _ffn(hff, x1, gt_f, norm_final[None, :], w_gate_b, w_up_b, w_down_b, tm=512)
    return out[None]
```

```python
import jax
import jax.numpy as jnp
import numpy as np
from jax import lax
from jax.experimental import pallas as pl
from jax.experimental.pallas import tpu as pltpu

GRID_W = 64
RET_HEADS = 8
RET_DK = 64
RET_DV = 128
RET_CHUNK = 128
ATT_HEADS = 16
ATT_KV_HEADS = 4
ATT_DH = 64
ATT_GROUP = ATT_HEADS // ATT_KV_HEADS
WINDOW = 128
ATT_BLOCK = 128
ROPE_BASE = 10000.0
NORM_EPS = 1e-6
K_SCALE = RET_DK ** -0.5
ATT_SCALE = ATT_DH ** -0.5

LANES = 128
RET_PAIRS = RET_HEADS // 2
MASK_NEG = -1e30
VMEM_LIMIT = 56 * 1024 * 1024
RET_STEP_CHUNKS = 4
OUT_ROW_CHUNK = 256
IN_ROW_CHUNK = 256
FFN_TILE = 512
FFN_SLOTS = 4
FFN_AHEAD = 2
FFN_ROW_CHUNK = 256

BF16 = jnp.bfloat16
F32 = jnp.float32


def _params(*sem):
    return pltpu.CompilerParams(dimension_semantics=sem, vmem_limit_bytes=VMEM_LIMIT)


def _with_cast_riders(body, n_in, n_out, n_rid):
    def wrapped(*refs):
        ins = refs[:n_in]
        rid_in = refs[n_in:n_in + n_rid]
        outs = refs[n_in + n_rid:n_in + n_rid + n_out]
        rid_out = refs[n_in + n_rid + n_out:n_in + 2 * n_rid + n_out]
        scratch = refs[n_in + 2 * n_rid + n_out:]
        for src, dst in zip(rid_in, rid_out):
            if len(dst.shape) == 2:
                dst[...] = src[...].astype(BF16)
            else:
                tc = dst.shape[2]
                for t in range(dst.shape[0]):
                    dst[t] = src[:, t * tc:(t + 1) * tc].astype(BF16)
        body(*ins, *outs, *scratch)
    return wrapped


def _rider_specs(riders, steps):
    in_specs, out_specs, shapes = [], [], []
    for w, ncb, tile in riders:
        rows, cols = w.shape
        nrb = steps // ncb
        assert nrb * ncb == steps and rows % nrb == 0 and cols % ncb == 0
        br, bc = rows // nrb, cols // ncb
        assert br % 16 == 0 and bc % LANES == 0, "slab must be bf16-tile aligned"
        in_specs.append(pl.BlockSpec((br, bc), lambda i, ncb=ncb: (i // ncb, i % ncb)))
        if tile is None:
            out_specs.append(in_specs[-1])
            shapes.append(jax.ShapeDtypeStruct(w.shape, BF16))
        else:
            assert ncb == 1 and cols % tile == 0 and tile % LANES == 0
            out_specs.append(pl.BlockSpec((cols // tile, br, tile), lambda i: (0, i, 0)))
            shapes.append(jax.ShapeDtypeStruct((cols // tile, rows, tile), BF16))
    return in_specs, out_specs, shapes


def _mod_kernel(cv_ref, w_ref, b_ref, o_ref):
    cv = cv_ref[...]
    s = cv / (1.0 + jnp.exp(-cv))
    o_ref[...] = jnp.dot(s.astype(BF16), w_ref[...].astype(BF16),
                         preferred_element_type=F32) + b_ref[...]


def _mod(cv, w, b):
    d, n = w.shape
    tn = 1024
    return pl.pallas_call(
        _mod_kernel,
        grid=(n // tn,),
        in_specs=[pl.BlockSpec((8, d), lambda j: (0, 0)),
                  pl.BlockSpec((d, tn), lambda j: (0, j)),
                  pl.BlockSpec((1, tn), lambda j: (0, j))],
        out_specs=pl.BlockSpec((8, tn), lambda j: (0, j)),
        out_shape=jax.ShapeDtypeStruct((8, n), F32),
        compiler_params=_params("parallel"),
        name="mod",
    )(cv, w, b)


def _rot_pairs(a, cos, sin_signed, half):
    lane = lax.broadcasted_iota(jnp.int32, a.shape, 1)
    first = (lane % (2 * half)) < half
    rot = jnp.where(first, pltpu.roll(a, LANES - half, 1), pltpu.roll(a, half, 1))
    return a * cos + rot * sin_signed


def _dup_halves(a):
    lane = lax.broadcasted_iota(jnp.int32, a.shape, 1)
    r = pltpu.roll(a, 64, 1)
    lo = lane < 64
    return jnp.where(lo, a, r), jnp.where(lo, r, a)


_PROJ_TILE = 512
_PROJ_TILE_KINDS = ("ret_q", "ret_k", "plain", "plain", "plain", "plain", "att_q", "att_q", "att_kv")


def _in_proj_kernel(x_ref, g_ref, sh_ref, sc_ref, w_ref, c1_ref, s1_ref, ca_ref, sa_ref,
                    o_ref, kd_ref, vd_ref):
    tn = _PROJ_TILE
    for r in range(x_ref.shape[0] // IN_ROW_CHUNK):
        rs = slice(r * IN_ROW_CHUNK, (r + 1) * IN_ROW_CHUNK)
        xf = x_ref[rs, :]
        y = xf * lax.rsqrt(jnp.mean(xf * xf, axis=-1, keepdims=True) + NORM_EPS)
        y = y * g_ref[...]
        h = (y * (1.0 + sc_ref[...]) + sh_ref[...]).astype(BF16)

        def rope1(a):
            return _rot_pairs(a, c1_ref[rs, :], s1_ref[rs, :], 32)

        def ropea(a):
            return _rot_pairs(a, ca_ref[rs, :], sa_ref[rs, :], 16)

        for j, kind in enumerate(_PROJ_TILE_KINDS):
            acc = jnp.dot(h, w_ref[:, j * tn:(j + 1) * tn], preferred_element_type=F32)
            for c in range(tn // LANES):
                a = acc[:, c * LANES:(c + 1) * LANES]
                if kind == "ret_q":
                    a = rope1(a)
                elif kind == "ret_k":
                    a = rope1(a) * K_SCALE
                elif kind == "att_q" or (kind == "att_kv" and c < 2):
                    a = ropea(a)
                o_ref[rs, j * tn + c * LANES:j * tn + (c + 1) * LANES] = a.astype(BF16)
                if kind == "att_kv":
                    dup_ref = kd_ref if c < 2 else vd_ref
                    d0, d1 = _dup_halves(a)
                    t = 2 * (c % 2)
                    dup_ref[rs, t * LANES:(t + 1) * LANES] = d0.astype(BF16)
                    dup_ref[rs, (t + 1) * LANES:(t + 2) * LANES] = d1.astype(BF16)


def _in_proj(x, g, sh, sc, w, tabs, *, tm):
    m, d = x.shape
    n = w.shape[1]
    assert n == _PROJ_TILE * len(_PROJ_TILE_KINDS) and m % tm == 0 and tm % IN_ROW_CHUNK == 0
    c1, s1, ca, sa = tabs
    row = lambda i: (i, 0)
    vec = pl.BlockSpec((1, d), lambda i: (0, 0))
    tab = pl.BlockSpec((tm, LANES), row)
    return pl.pallas_call(
        _in_proj_kernel,
        grid=(m // tm,),
        in_specs=[pl.BlockSpec((tm, d), row), vec, vec, vec,
                  pl.BlockSpec((d, n), lambda i: (0, 0), pipeline_mode=pl.Buffered(1)),
                  tab, tab, tab, tab],
        out_specs=[pl.BlockSpec((tm, n), row),
                   pl.BlockSpec((tm, 512), row),
                   pl.BlockSpec((tm, 512), row)],
        out_shape=[jax.ShapeDtypeStruct((m, n), BF16),
                   jax.ShapeDtypeStruct((m, 512), BF16),
                   jax.ShapeDtypeStruct((m, 512), BF16)],
        compiler_params=_params("parallel"),
        name="in_proj",
    )(x, g, sh, sc, w, c1, s1, ca, sa)


def _ctx_proj_kernel(x_ref, g_ref, sh_ref, sc_ref, w_ref, o_ref, kd_ref, vd_ref, wb_ref, h_ref):
    j = pl.program_id(0)

    @pl.when(j == 0)
    def _():
        xf = x_ref[...]
        y = xf * lax.rsqrt(jnp.mean(xf * xf, axis=-1, keepdims=True) + NORM_EPS)
        y = y * g_ref[...]
        h_ref[...] = (y * (1.0 + sc_ref[...]) + sh_ref[...]).astype(BF16)

    wb = w_ref[...].astype(BF16)
    wb_ref[...] = wb
    acc = jnp.dot(h_ref[...], wb, preferred_element_type=F32)
    is_ret_k = _PROJ_TILE_KINDS.index("ret_k")
    o_ref[...] = (acc * jnp.where(j == is_ret_k, K_SCALE, 1.0)).astype(BF16)

    @pl.when(j == _PROJ_TILE_KINDS.index("att_kv"))
    def _():
        for c in range(_PROJ_TILE // LANES):
            dup_ref = kd_ref if c < 2 else vd_ref
            d0, d1 = _dup_halves(acc[:, c * LANES:(c + 1) * LANES])
            t = 2 * (c % 2)
            dup_ref[:, t * LANES:(t + 1) * LANES] = d0.astype(BF16)
            dup_ref[:, (t + 1) * LANES:(t + 2) * LANES] = d1.astype(BF16)


def _ctx_proj(x, g, sh, sc, w):
    m, d = x.shape
    n = w.shape[1]
    tn = _PROJ_TILE
    assert n == tn * len(_PROJ_TILE_KINDS)
    fixed = lambda j: (0, 0)
    vec = pl.BlockSpec((1, d), fixed)
    return pl.pallas_call(
        _ctx_proj_kernel,
        grid=(n // tn,),
        in_specs=[pl.BlockSpec((m, d), fixed), vec, vec, vec,
                  pl.BlockSpec((d, tn), lambda j: (0, j))],
        out_specs=[pl.BlockSpec((m, tn), lambda j: (0, j)),
                   pl.BlockSpec((m, 512), fixed),
                   pl.BlockSpec((m, 512), fixed),
                   pl.BlockSpec((d, tn), lambda j: (0, j))],
        out_shape=[jax.ShapeDtypeStruct((m, n), BF16),
                   jax.ShapeDtypeStruct((m, 512), BF16),
                   jax.ShapeDtypeStruct((m, 512), BF16),
                   jax.ShapeDtypeStruct((d, n), BF16)],
        scratch_shapes=[pltpu.VMEM((m, d), BF16)],
        compiler_params=_params("arbitrary"),
        name="ctx_proj",
    )(x, g, sh, sc, w)


def _pair_lg(dec_ref, d, p, shape):
    lane = lax.broadcasted_iota(jnp.int32, shape, 1)
    first = (lane % LANES) < 64
    raw = jnp.where(first, jnp.full(shape, dec_ref[d, 2 * p], F32), jnp.full(shape, dec_ref[d, 2 * p + 1], F32))
    return -jnp.exp(raw)


def _head_block_mask(shape):
    r = lax.broadcasted_iota(jnp.int32, shape, 0)
    c = lax.broadcasted_iota(jnp.int32, shape, 1)
    return (r // 64) == (c // LANES)


def _kv_pair(k_pair, v_pair, w):
    kw = (k_pair.astype(F32) * w).astype(BF16)
    kv = lax.dot_general(kw, v_pair, (((0,), (0,)), ((), ())), preferred_element_type=F32)
    return jnp.where(_head_block_mask(kv.shape), kv, 0.0)


def _ret_state_kernel(dec_ref, kf_ref, vf_ref, kb_ref, vb_ref, ck_ref, cv_ref,
                      sf_ref, sb_ref, sfs, sbs):
    i = pl.program_id(0)
    C = RET_CHUNK
    lc = ck_ref.shape[0]

    @pl.when(i == 0)
    def _():
        pos = lax.broadcasted_iota(jnp.int32, (lc, LANES), 0).astype(F32)
        for p in range(RET_PAIRS):
            ks = slice(p * LANES, (p + 1) * LANES)
            vs = slice(p * 2 * RET_DV, (p + 1) * 2 * RET_DV)
            wf = jnp.exp(_pair_lg(dec_ref, 0, p, (lc, LANES)) * (lc - 1.0 - pos))
            wb = jnp.exp(_pair_lg(dec_ref, 1, p, (lc, LANES)) * pos)
            sfs[p] = _kv_pair(ck_ref[:, ks], cv_ref[:, vs], wf)
            sbs[p] = _kv_pair(ck_ref[:, ks], cv_ref[:, vs], wb)

    pos = lax.broadcasted_iota(jnp.int32, (C, LANES), 0).astype(F32)
    for p in range(RET_PAIRS):
        ks = slice(p * LANES, (p + 1) * LANES)
        vs = slice(p * 2 * RET_DV, (p + 1) * 2 * RET_DV)
        wf = jnp.exp(_pair_lg(dec_ref, 0, p, (C, LANES)) * (C - 1.0 - pos))
        wb = jnp.exp(_pair_lg(dec_ref, 1, p, (C, LANES)) * pos)
        rowh = lax.broadcasted_iota(jnp.int32, (LANES, 2 * RET_DV), 0) < 64
        gf = jnp.exp(-jnp.exp(jnp.where(rowh, jnp.full(rowh.shape, dec_ref[0, 2 * p], F32),
                                         jnp.full(rowh.shape, dec_ref[0, 2 * p + 1], F32))) * float(C))
        gb = jnp.exp(-jnp.exp(jnp.where(rowh, jnp.full(rowh.shape, dec_ref[1, 2 * p], F32),
                                         jnp.full(rowh.shape, dec_ref[1, 2 * p + 1], F32))) * float(C))
        sf = sfs[p]
        for cc in range(RET_STEP_CHUNKS):
            rs = slice(cc * C, (cc + 1) * C)
            sf_ref[cc, p] = sf.astype(BF16)
            sf = gf * sf + _kv_pair(kf_ref[rs, ks], vf_ref[rs, vs], wf)
        sfs[p] = sf
        sb = sbs[p]
        for cc in reversed(range(RET_STEP_CHUNKS)):
            rs = slice(cc * C, (cc + 1) * C)
            sb_ref[cc, p] = sb.astype(BF16)
            sb = gb * sb + _kv_pair(kb_ref[rs, ks], vb_ref[rs, vs], wb)
        sbs[p] = sb


def _ret_states(dec, proj, cproj):
    L = proj.shape[0]
    lc = cproj.shape[0]
    S = RET_STEP_CHUNKS
    R = S * RET_CHUNK
    n = L // R
    st = pl.BlockSpec((S, RET_PAIRS, LANES, 2 * RET_DV), lambda i: (i, 0, 0, 0))
    st_rev = pl.BlockSpec((S, RET_PAIRS, LANES, 2 * RET_DV), lambda i: (n - 1 - i, 0, 0, 0))
    shp = jax.ShapeDtypeStruct((n * S, RET_PAIRS, LANES, 2 * RET_DV), BF16)
    return pl.pallas_call(
        _ret_state_kernel,
        grid=(n,),
        in_specs=[pl.BlockSpec(memory_space=pltpu.SMEM),
                  pl.BlockSpec((R, 512), lambda i: (i, 1)),
                  pl.BlockSpec((R, 1024), lambda i: (i, 1)),
                  pl.BlockSpec((R, 512), lambda i: (n - 1 - i, 1)),
                  pl.BlockSpec((R, 1024), lambda i: (n - 1 - i, 1)),
                  pl.BlockSpec((lc, 512), lambda i: (0, 1)),
                  pl.BlockSpec((lc, 1024), lambda i: (0, 1))],
        out_specs=[st, st_rev],
        out_shape=[shp, shp],
        scratch_shapes=[pltpu.VMEM((RET_PAIRS, LANES, 2 * RET_DV), F32),
                        pltpu.VMEM((RET_PAIRS, LANES, 2 * RET_DV), F32)],
        compiler_params=_params("arbitrary"),
        name="ret_state",
    )(dec, proj, proj, proj, proj, cproj, cproj)


def _ret_out_kernel(dec_ref, q_ref, k_ref, v_ref, g_ref, sf_ref, sb_ref, o_ref):
    C = RET_CHUNK
    pos = lax.broadcasted_iota(jnp.int32, (C, LANES), 0).astype(F32)
    n_i = lax.broadcasted_iota(jnp.int32, (C, 2 * C), 0)
    m_i = lax.broadcasted_iota(jnp.int32, (C, 2 * C), 1) % C
    rel = (n_i - m_i).astype(F32)
    lane = lax.broadcasted_iota(jnp.int32, (C, LANES), 1)
    lo = lane < 64
    for p in range(RET_PAIRS):
        ks = slice(p * LANES, (p + 1) * LANES)
        vs = slice(p * 2 * RET_DV, (p + 1) * 2 * RET_DV)
        col_a = lax.broadcasted_iota(jnp.int32, (C, 2 * C), 1) < C
        raw_f = jnp.where(col_a, jnp.full((C, 2 * C), dec_ref[0, 2 * p], F32), jnp.full((C, 2 * C), dec_ref[0, 2 * p + 1], F32))
        raw_b = jnp.where(col_a, jnp.full((C, 2 * C), dec_ref[1, 2 * p], F32), jnp.full((C, 2 * C), dec_ref[1, 2 * p + 1], F32))
        dmat = jnp.where(rel >= 0, jnp.exp(-jnp.exp(raw_f) * jnp.maximum(rel, 0.0)),
                         jnp.exp(-jnp.exp(raw_b) * jnp.maximum(-rel, 0.0)))
        wqf = jnp.exp(_pair_lg(dec_ref, 0, p, (C, LANES)) * (pos + 1.0))
        wqb = jnp.exp(_pair_lg(dec_ref, 1, p, (C, LANES)) * (float(C) - pos))
        for cc in range(RET_STEP_CHUNKS):
            rs = slice(cc * C, (cc + 1) * C)
            q = q_ref[rs, ks]
            k = k_ref[rs, ks]
            v = v_ref[rs, vs]
            zk = jnp.zeros_like(k)
            kst = jnp.concatenate([jnp.where(lo, k, zk), jnp.where(lo, zk, k)], axis=0)
            s = lax.dot_general(q, kst, (((1,), (1,)), ((), ())), preferred_element_type=F32)
            sd = (s * dmat).astype(BF16)
            qf32 = q.astype(F32)
            qwf = (qf32 * wqf).astype(BF16)
            qwb = (qf32 * wqb).astype(BF16)
            zv = jnp.zeros((C, RET_DV), BF16)
            vbd = jnp.concatenate([jnp.concatenate([v[:, :RET_DV], zv], axis=1),
                                   jnp.concatenate([zv, v[:, RET_DV:]], axis=1)], axis=0)
            lhs = jnp.concatenate([sd, qwf, qwb], axis=1)
            rhs = jnp.concatenate([vbd, sf_ref[cc, p], sb_ref[cc, p]], axis=0)
            o = jnp.dot(lhs, rhs, preferred_element_type=F32)
            for t in range(2):
                oh = o[:, t * RET_DV:(t + 1) * RET_DV]
                oh = oh * lax.rsqrt(jnp.mean(oh * oh, axis=-1, keepdims=True) + NORM_EPS)
                cs = slice(p * 2 * RET_DV + t * RET_DV, p * 2 * RET_DV + (t + 1) * RET_DV)
                gt = g_ref[rs, cs].astype(F32)
                o_ref[rs, cs] = (oh * (gt / (1.0 + jnp.exp(-gt)))).astype(BF16)


def _ret_out(dec, proj, sf, sb):
    L = proj.shape[0]
    S = RET_STEP_CHUNKS
    R = S * RET_CHUNK
    n = L // R
    st = pl.BlockSpec((S, RET_PAIRS, LANES, 2 * RET_DV), lambda i: (i, 0, 0, 0))
    return pl.pallas_call(
        _ret_out_kernel,
        grid=(n,),
        in_specs=[pl.BlockSpec(memory_space=pltpu.SMEM),
                  pl.BlockSpec((R, 512), lambda i: (i, 0)),
                  pl.BlockSpec((R, 512), lambda i: (i, 1)),
                  pl.BlockSpec((R, 1024), lambda i: (i, 1)),
                  pl.BlockSpec((R, 1024), lambda i: (i, 2)),
                  st, st],
        out_specs=pl.BlockSpec((R, RET_HEADS * RET_DV), lambda i: (i, 0)),
        out_shape=jax.ShapeDtypeStruct((L, RET_HEADS * RET_DV), BF16),
        compiler_params=_params("parallel"),
        name="ret_out",
    )(dec, proj, proj, proj, proj, sf, sb)


def _attn_kernel(sink_ref, q_ref, kp_ref, kc_ref, kn_ref, vp_ref, vc_ref, vn_ref, ck_ref, cv_ref, o_ref):
    n = pl.program_id(0)
    nblk = pl.num_programs(0)
    B = ATT_BLOCK
    lc = ck_ref.shape[0]
    kj = lax.broadcasted_iota(jnp.int32, (B, B), 0)
    qi = lax.broadcasted_iota(jnp.int32, (B, B), 1)
    ok_prev = jnp.where(n > 0, 0.0, MASK_NEG).astype(F32)
    ok_next = jnp.where(n < nblk - 1, 0.0, MASK_NEG).astype(F32)
    bias_prev = jnp.concatenate([jnp.where(kj >= qi, ok_prev, MASK_NEG).astype(F32)] * ATT_GROUP, axis=1)
    bias_next = jnp.concatenate([jnp.where(kj <= qi, ok_next, MASK_NEG).astype(F32)] * ATT_GROUP, axis=1)
    lane = lax.broadcasted_iota(jnp.int32, (B, LANES), 1)
    lo = lane < 64
    hi = lane >= 64
    def scores(g):
        gs = slice(g * LANES, (g + 1) * LANES)
        kcat = jnp.concatenate([kp_ref[:, gs], kc_ref[:, gs], kn_ref[:, gs], ck_ref[:, gs]], axis=0)
        qs = []
        for r in range(ATT_GROUP):
            h = ATT_GROUP * g + r
            qt = q_ref[:, (h // 2) * LANES:(h // 2 + 1) * LANES]
            keep = lo if h % 2 == 0 else hi
            qs.append(jnp.where(keep, qt * jnp.asarray(ATT_SCALE, BF16), jnp.zeros_like(qt)))
        q4 = jnp.concatenate(qs, axis=0)
        return lax.dot_general(kcat, q4, (((1,), (1,)), ((), ())), preferred_element_type=F32)

    def softmax(g, s):
        sk = jnp.concatenate([jnp.full((1, B), sink_ref[ATT_GROUP * g + r], F32)
                              for r in range(ATT_GROUP)], axis=1)
        s = jnp.concatenate([s[:B] + bias_prev, s[B:2 * B], s[2 * B:3 * B] + bias_next, s[3 * B:]], axis=0)
        m = jnp.maximum(jnp.max(s, axis=0, keepdims=True), sk)
        e = jnp.exp(s - m)
        den = jnp.sum(e, axis=0, keepdims=True) + jnp.exp(sk - m)
        return e.astype(BF16), den

    def values(g, e, den):
        gs = slice(g * LANES, (g + 1) * LANES)
        vcat = jnp.concatenate([vp_ref[:, gs], vc_ref[:, gs], vn_ref[:, gs], cv_ref[:, gs]], axis=0)
        res = lax.dot_general(vcat, e, (((0,), (0,)), ((), ())), preferred_element_type=F32) / den
        for t in range(2):
            even = res[:, (2 * t) * B:(2 * t + 1) * B].T
            odd = res[:, (2 * t + 1) * B:(2 * t + 2) * B].T
            c0 = (2 * g + t) * LANES
            o_ref[:, c0:c0 + LANES] = jnp.where(lo, even, odd).astype(BF16)

    s_next = scores(0)
    pending = None
    for g in range(ATT_KV_HEADS):
        s_cur = s_next
        if g + 1 < ATT_KV_HEADS:
            s_next = scores(g + 1)
        e_den = softmax(g, s_cur)
        if pending is not None:
            values(g - 1, *pending)
        pending = e_den
    values(ATT_KV_HEADS - 1, *pending)


def _attn(sink, proj, kd, vd, ckd, cvd, riders):
    L = proj.shape[0]
    B = ATT_BLOCK
    n = L // B
    lc = ckd.shape[0]
    prev = lambda i: (jnp.maximum(i - 1, 0), 0)
    cur = lambda i: (i, 0)
    nxt = lambda i: (jnp.minimum(i + 1, n - 1), 0)
    kv = lambda f: pl.BlockSpec((B, 512), f)
    full = pl.BlockSpec((lc, 512), lambda i: (0, 0))
    rid_in_specs, rid_out_specs, rid_shapes = _rider_specs(riders, n)
    return pl.pallas_call(
        _with_cast_riders(_attn_kernel, 10, 1, len(riders)),
        grid=(n,),
        in_specs=[pl.BlockSpec(memory_space=pltpu.SMEM),
                  pl.BlockSpec((B, 1024), lambda i: (i, 3)),
                  kv(prev), kv(cur), kv(nxt), kv(prev), kv(cur), kv(nxt), full, full] + rid_in_specs,
        out_specs=[pl.BlockSpec((B, ATT_HEADS * ATT_DH), cur)] + rid_out_specs,
        out_shape=[jax.ShapeDtypeStruct((L, ATT_HEADS * ATT_DH), BF16)] + rid_shapes,
        compiler_params=_params("parallel"),
        name="attn",
    )(sink, proj, kd, kd, kd, vd, vd, vd, ckd, cvd, *[r[0] for r in riders])


def _out_proj_kernel(yr_ref, ya_ref, w_ref, x_ref, gt_ref, g_ref, sh_ref, sc_ref, o_ref, h_ref):
    kr = yr_ref.shape[1]
    for r in range(yr_ref.shape[0] // OUT_ROW_CHUNK):
        rs = slice(r * OUT_ROW_CHUNK, (r + 1) * OUT_ROW_CHUNK)
        acc = jnp.dot(yr_ref[rs, :], w_ref[:kr, :], preferred_element_type=F32)
        acc = acc + jnp.dot(ya_ref[rs, :], w_ref[kr:, :], preferred_element_type=F32)
        x1 = x_ref[rs, :] + gt_ref[...] * acc
        o_ref[rs, :] = x1
        y = x1 * lax.rsqrt(jnp.mean(x1 * x1, axis=-1, keepdims=True) + NORM_EPS)
        y = y * g_ref[...]
        h_ref[rs, :] = (y * (1.0 + sc_ref[...]) + sh_ref[...]).astype(BF16)


def _out_proj(yr, ya, w, x, gt, g, sh, sc, *, tm):
    m, d = x.shape
    kr, ka = yr.shape[1], ya.shape[1]
    row = lambda i: (i, 0)
    vec = pl.BlockSpec((1, d), lambda i: (0, 0))
    return pl.pallas_call(
        _out_proj_kernel,
        grid=(m // tm,),
        in_specs=[pl.BlockSpec((tm, kr), row), pl.BlockSpec((tm, ka), row),
                  pl.BlockSpec((kr + ka, d), lambda i: (0, 0)),
                  pl.BlockSpec((tm, d), row), vec, vec, vec, vec],
        out_specs=[pl.BlockSpec((tm, d), row), pl.BlockSpec((tm, d), row)],
        out_shape=[jax.ShapeDtypeStruct((m, d), F32), jax.ShapeDtypeStruct((m, d), BF16)],
        compiler_params=_params("parallel"),
        name="out_proj",
    )(yr, ya, w, x, gt, g, sh, sc)


def _ffn_weight_copies(wg_hbm, wu_hbm, wd_hbm, wg_buf, wu_buf, wd_buf, sem, f):
    slot = f % FFN_SLOTS
    return (pltpu.make_async_copy(wg_hbm.at[f], wg_buf.at[slot], sem.at[0, slot]),
            pltpu.make_async_copy(wu_hbm.at[f], wu_buf.at[slot], sem.at[1, slot]),
            pltpu.make_async_copy(wd_hbm.at[f * FFN_TILE:(f + 1) * FFN_TILE, :], wd_buf.at[slot], sem.at[2, slot]))


def _ffn_kernel(h_ref, x_ref, gt_ref, gfin_ref, wg_hbm, wu_hbm, wd_hbm, o_ref, wg_buf, wu_buf, wd_buf, sem):
    i = pl.program_id(0)
    nf = wg_hbm.shape[0]
    assert FFN_SLOTS > FFN_AHEAD
    assert all(t % FFN_SLOTS != k % FFN_SLOTS
               for k in range(FFN_AHEAD) for t in range(nf - FFN_AHEAD + k, nf)), "prefetch would hit a live slot"

    def copies(f):
        return _ffn_weight_copies(wg_hbm, wu_hbm, wd_hbm, wg_buf, wu_buf, wd_buf, sem, f)

    @pl.when(i == 0)
    def _():
        for f in range(FFN_AHEAD):
            for cp in copies(f):
                cp.start()

    rows = h_ref.shape[0]
    for f in range(nf):
        slot = f % FFN_SLOTS
        for cp in copies((f + FFN_AHEAD) % nf):
            cp.start()
        for cp in copies(f):
            cp.wait()
        for r in range(rows // FFN_ROW_CHUNK):
            rs = slice(r * FFN_ROW_CHUNK, (r + 1) * FFN_ROW_CHUNK)
            h = h_ref[rs, :]
            a = jnp.dot(h, wg_buf[slot], preferred_element_type=F32)
            u = jnp.dot(h, wu_buf[slot], preferred_element_type=F32)
            act = ((a / (1.0 + jnp.exp(-a))) * u).astype(BF16)
            part = jnp.dot(act, wd_buf[slot], preferred_element_type=F32)
            if f == 0:
                o_ref[rs, :] = part
            else:
                o_ref[rs, :] += part

    for r in range(rows // FFN_ROW_CHUNK):
        rs = slice(r * FFN_ROW_CHUNK, (r + 1) * FFN_ROW_CHUNK)
        y = x_ref[rs, :] + gt_ref[...] * o_ref[rs, :]
        y = y * lax.rsqrt(jnp.mean(y * y, axis=-1, keepdims=True) + NORM_EPS)
        o_ref[rs, :] = y * gfin_ref[...]

    @pl.when(i == pl.num_programs(0) - 1)
    def _():
        for f in range(FFN_AHEAD):
            for cp in copies(f):
                cp.wait()


def _ffn(h, x, gt, gfin, wg, wu, wd, *, tm):
    m, d = x.shape
    assert wg.shape == wu.shape == (wd.shape[0] // FFN_TILE, d, FFN_TILE)
    assert m % tm == 0 and tm % FFN_ROW_CHUNK == 0
    row = lambda i: (i, 0)
    vec = pl.BlockSpec((1, d), lambda i: (0, 0))
    hbm = pl.BlockSpec(memory_space=pl.ANY)
    return pl.pallas_call(
        _ffn_kernel,
        grid=(m // tm,),
        in_specs=[pl.BlockSpec((tm, d), row), pl.BlockSpec((tm, d), row), vec, vec, hbm, hbm, hbm],
        out_specs=pl.BlockSpec((tm, d), row),
        out_shape=jax.ShapeDtypeStruct((m, d), F32),
        scratch_shapes=[pltpu.VMEM((FFN_SLOTS, d, FFN_TILE), BF16),
                        pltpu.VMEM((FFN_SLOTS, d, FFN_TILE), BF16),
                        pltpu.VMEM((FFN_SLOTS, FFN_TILE, d), BF16),
                        pltpu.SemaphoreType.DMA((3, FFN_SLOTS))],
        compiler_params=_params("arbitrary"),
        name="ffn",
    )(h, x, gt, gfin, wg, wu, wd)


def _rope_tables(L):
    f32 = np.float32
    lane = np.arange(LANES)
    inv1 = f32(ROPE_BASE) ** (-np.arange(32, dtype=f32) / f32(32))
    ang1 = np.arange(L, dtype=f32)[:, None] * inv1[None, :]
    sgn1 = np.where((lane % 64) < 32, -1.0, 1.0).astype(f32)
    cos1 = np.tile(np.cos(ang1), (1, LANES // 32))
    sin1 = np.tile(np.sin(ang1), (1, LANES // 32)) * sgn1[None, :]
    inv2 = f32(ROPE_BASE) ** (-np.arange(16, dtype=f32) / f32(16))
    nrow = L // GRID_W
    ang_r = np.arange(nrow, dtype=f32)[:, None] * inv2[None, :]
    ang_c = np.arange(GRID_W, dtype=f32)[:, None] * inv2[None, :]
    sgna = np.where((lane % 32) < 16, -1.0, 1.0).astype(f32)

    def expand(fr, fc):
        by_row = np.broadcast_to(np.tile(fr, (1, 2))[:, None, :], (nrow, GRID_W, 32))
        by_col = np.broadcast_to(np.tile(fc, (1, 2))[None, :, :], (nrow, GRID_W, 32))
        head = np.concatenate([by_row, by_col], axis=-1).reshape(L, 64)
        return np.tile(head, (1, LANES // 64))

    cosa = expand(np.cos(ang_r), np.cos(ang_c))
    sina = expand(np.sin(ang_r), np.sin(ang_c)) * sgna[None, :]
    return tuple(np.ascontiguousarray(t, dtype=f32) for t in (cos1, sin1, cosa, sina))


def kernel(x, c, ctx, c_ctx, w_mod, b_mod, norm_mix, norm_ffn, w_in, ret_decay, attn_sink,
           w_out, w_gate, w_up, w_down, norm_final):
    B, L, D = x.shape
    assert B == 1 and w_mod.shape[0] == 1, "single batch element, depth-1 layer"
    x2 = x[0]
    xc2 = ctx[0]

    cv = jnp.zeros((8, D), F32).at[0].set(c[0]).at[1].set(c_ctx)
    mod = _mod(cv, w_mod[0], b_mod[0][None, :])
    sh_m, sc_m, gt_m, sh_f, sc_f, gt_f = [mod[0:1, k * D:(k + 1) * D] for k in range(6)]
    sh_mc, sc_mc = mod[1:2, 0:D], mod[1:2, D:2 * D]

    g_mix = norm_mix[0][None, :]
    cproj, ckd, cvd, w_in_b = _ctx_proj(xc2, g_mix, sh_mc, sc_mc, w_in[0])
    proj, kd, vd = _in_proj(x2, g_mix, sh_m, sc_m, w_in_b, _rope_tables(L), tm=512)

    dec = ret_decay[0].astype(F32)
    sf, sb = _ret_states(dec, proj, cproj)
    y_ret = _ret_out(dec, proj, sf, sb)
    y_att, w_gate_b, w_up_b, w_down_b, w_out_b = _attn(
        attn_sink[0].astype(F32), proj, kd, vd, ckd, cvd,
        [(w_gate[0], 1, FFN_TILE), (w_up[0], 1, FFN_TILE), (w_down[0], 2, None), (w_out[0], 1, None)])

    x1, hff = _out_proj(y_ret, y_att, w_out_b, x2, gt_m, norm_ffn[0][None, :], sh_f, sc_f, tm=512)
    out = _ffn(hff, x1, gt_f, norm_final[None, :], w_gate_b, w_up_b, w_down_b, tm=512)
    return out[None]
```

```python
import jax
import jax.numpy as jnp
import numpy as np
from jax import lax
from jax.experimental import pallas as pl
from jax.experimental.pallas import tpu as pltpu

GRID_W = 64
RET_HEADS = 8
RET_DK = 64
RET_DV = 128
RET_CHUNK = 128
ATT_HEADS = 16
ATT_KV_HEADS = 4
ATT_DH = 64
ATT_GROUP = ATT_HEADS // ATT_KV_HEADS
WINDOW = 128
ATT_BLOCK = 128
ROPE_BASE = 10000.0
NORM_EPS = 1e-6
K_SCALE = RET_DK ** -0.5
ATT_SCALE = ATT_DH ** -0.5
LOG2E = 1.4426950408889634

LANES = 128
RET_PAIRS = RET_HEADS // 2
MASK_NEG = -1e30
VMEM_LIMIT = 56 * 1024 * 1024
RET_STEP_CHUNKS = 4
ATT_STEP_BLOCKS = 2
OUT_ROW_CHUNK = 256
IN_ROW_CHUNK = 256
FFN_TILE = 512
FFN_GROUP = 2
FFN_SLOTS = 2 * FFN_GROUP
FFN_ROW_CHUNK = 256

BF16 = jnp.bfloat16
F32 = jnp.float32


def _params(*sem):
    return pltpu.CompilerParams(dimension_semantics=sem, vmem_limit_bytes=VMEM_LIMIT)


def _with_cast_riders(body, n_in, n_out, n_rid):
    def wrapped(*refs):
        ins = refs[:n_in]
        rid_in = refs[n_in:n_in + n_rid]
        outs = refs[n_in + n_rid:n_in + n_rid + n_out]
        rid_out = refs[n_in + n_rid + n_out:n_in + 2 * n_rid + n_out]
        scratch = refs[n_in + 2 * n_rid + n_out:]
        for src, dst in zip(rid_in, rid_out):
            if len(dst.shape) == 2:
                dst[...] = src[...].astype(BF16)
            else:
                tc = dst.shape[2]
                for t in range(dst.shape[0]):
                    dst[t] = src[:, t * tc:(t + 1) * tc].astype(BF16)
        body(*ins, *outs, *scratch)
    return wrapped


def _rider_specs(riders, steps):
    in_specs, out_specs, shapes = [], [], []
    for w, ncb, tile in riders:
        rows, cols = w.shape
        nrb = steps // ncb
        assert nrb * ncb == steps and rows % nrb == 0 and cols % ncb == 0
        br, bc = rows // nrb, cols // ncb
        assert br % 16 == 0 and bc % LANES == 0, "slab must be bf16-tile aligned"
        in_specs.append(pl.BlockSpec((br, bc), lambda i, ncb=ncb: (i // ncb, i % ncb)))
        if tile is None:
            out_specs.append(in_specs[-1])
            shapes.append(jax.ShapeDtypeStruct(w.shape, BF16))
        else:
            assert ncb == 1 and cols % tile == 0 and tile % LANES == 0
            out_specs.append(pl.BlockSpec((cols // tile, br, tile), lambda i: (0, i, 0)))
            shapes.append(jax.ShapeDtypeStruct((cols // tile, rows, tile), BF16))
    return in_specs, out_specs, shapes


def _mod_kernel(cv_ref, w_ref, b_ref, o_ref):
    cv = cv_ref[...]
    s = cv / (1.0 + jnp.exp(-cv))
    o_ref[...] = jnp.dot(s.astype(BF16), w_ref[...].astype(BF16),
                         preferred_element_type=F32) + b_ref[...]


def _mod(cv, w, b):
    d, n = w.shape
    tn = 1024
    return pl.pallas_call(
        _mod_kernel,
        grid=(n // tn,),
        in_specs=[pl.BlockSpec((8, d), lambda j: (0, 0)),
                  pl.BlockSpec((d, tn), lambda j: (0, j)),
                  pl.BlockSpec((1, tn), lambda j: (0, j))],
        out_specs=pl.BlockSpec((8, tn), lambda j: (0, j)),
        out_shape=jax.ShapeDtypeStruct((8, n), F32),
        compiler_params=_params("parallel"),
        name="mod",
    )(cv, w, b)


def _rot_pairs(a, cos, sin_signed, half):
    lane = lax.broadcasted_iota(jnp.int32, a.shape, 1)
    first = (lane % (2 * half)) < half
    rot = jnp.where(first, pltpu.roll(a, LANES - half, 1), pltpu.roll(a, half, 1))
    return a * cos + rot * sin_signed


def _dup_halves(a):
    lane = lax.broadcasted_iota(jnp.int32, a.shape, 1)
    r = pltpu.roll(a, 64, 1)
    lo = lane < 64
    return jnp.where(lo, a, r), jnp.where(lo, r, a)


_PROJ_TILE = 512
_PROJ_TILE_KINDS = ("ret_q", "ret_k", "plain", "plain", "plain", "plain", "att_q", "att_q", "att_kv")


def _in_proj_kernel(x_ref, g_ref, sh_ref, sc_ref, w_ref, c1_ref, s1_ref, ca_ref, sa_ref,
                    o_ref, kd_ref, vd_ref):
    tn = _PROJ_TILE
    for r in range(x_ref.shape[0] // IN_ROW_CHUNK):
        rs = slice(r * IN_ROW_CHUNK, (r + 1) * IN_ROW_CHUNK)
        xf = x_ref[rs, :]
        y = xf * lax.rsqrt(jnp.mean(xf * xf, axis=-1, keepdims=True) + NORM_EPS)
        y = y * g_ref[...]
        h = (y * (1.0 + sc_ref[...]) + sh_ref[...]).astype(BF16)

        def rope1(a):
            return _rot_pairs(a, c1_ref[rs, :], s1_ref[rs, :], 32)

        def ropea(a):
            return _rot_pairs(a, ca_ref[rs, :], sa_ref[rs, :], 16)

        for j, kind in enumerate(_PROJ_TILE_KINDS):
            acc = jnp.dot(h, w_ref[:, j * tn:(j + 1) * tn], preferred_element_type=F32)
            for c in range(tn // LANES):
                a = acc[:, c * LANES:(c + 1) * LANES]
                if kind == "ret_q":
                    a = rope1(a)
                elif kind == "ret_k":
                    a = rope1(a) * K_SCALE
                elif kind == "att_q":
                    a = ropea(a) * (ATT_SCALE * LOG2E)
                elif kind == "att_kv" and c < 2:
                    a = ropea(a)
                o_ref[rs, j * tn + c * LANES:j * tn + (c + 1) * LANES] = a.astype(BF16)
                if kind == "att_kv":
                    dup_ref = kd_ref if c < 2 else vd_ref
                    d0, d1 = _dup_halves(a)
                    t = 2 * (c % 2)
                    dup_ref[rs, t * LANES:(t + 1) * LANES] = d0.astype(BF16)
                    dup_ref[rs, (t + 1) * LANES:(t + 2) * LANES] = d1.astype(BF16)


def _in_proj(x, g, sh, sc, w, tabs, *, tm):
    m, d = x.shape
    n = w.shape[1]
    assert n == _PROJ_TILE * len(_PROJ_TILE_KINDS) and m % tm == 0 and tm % IN_ROW_CHUNK == 0
    c1, s1, ca, sa = tabs
    row = lambda i: (i, 0)
    vec = pl.BlockSpec((1, d), lambda i: (0, 0))
    tab = pl.BlockSpec((tm, LANES), row)
    return pl.pallas_call(
        _in_proj_kernel,
        grid=(m // tm,),
        in_specs=[pl.BlockSpec((tm, d), row), vec, vec, vec,
                  pl.BlockSpec((d, n), lambda i: (0, 0), pipeline_mode=pl.Buffered(1)),
                  tab, tab, tab, tab],
        out_specs=[pl.BlockSpec((tm, n), row),
                   pl.BlockSpec((tm, 512), row),
                   pl.BlockSpec((tm, 512), row)],
        out_shape=[jax.ShapeDtypeStruct((m, n), BF16),
                   jax.ShapeDtypeStruct((m, 512), BF16),
                   jax.ShapeDtypeStruct((m, 512), BF16)],
        compiler_params=_params("parallel"),
        name="in_proj",
    )(x, g, sh, sc, w, c1, s1, ca, sa)


def _ctx_proj_kernel(x_ref, g_ref, sh_ref, sc_ref, w_ref, o_ref, kd_ref, vd_ref, wb_ref, h_ref):
    j = pl.program_id(0)

    @pl.when(j == 0)
    def _():
        xf = x_ref[...]
        y = xf * lax.rsqrt(jnp.mean(xf * xf, axis=-1, keepdims=True) + NORM_EPS)
        y = y * g_ref[...]
        h_ref[...] = (y * (1.0 + sc_ref[...]) + sh_ref[...]).astype(BF16)

    wb = w_ref[...].astype(BF16)
    wb_ref[...] = wb
    acc = jnp.dot(h_ref[...], wb, preferred_element_type=F32)
    is_ret_k = _PROJ_TILE_KINDS.index("ret_k")
    o_ref[...] = (acc * jnp.where(j == is_ret_k, K_SCALE, 1.0)).astype(BF16)

    @pl.when(j == _PROJ_TILE_KINDS.index("att_kv"))
    def _():
        for c in range(_PROJ_TILE // LANES):
            dup_ref = kd_ref if c < 2 else vd_ref
            d0, d1 = _dup_halves(acc[:, c * LANES:(c + 1) * LANES])
            t = 2 * (c % 2)
            dup_ref[:, t * LANES:(t + 1) * LANES] = d0.astype(BF16)
            dup_ref[:, (t + 1) * LANES:(t + 2) * LANES] = d1.astype(BF16)


def _ctx_proj(x, g, sh, sc, w):
    m, d = x.shape
    n = w.shape[1]
    tn = _PROJ_TILE
    assert n == tn * len(_PROJ_TILE_KINDS)
    fixed = lambda j: (0, 0)
    vec = pl.BlockSpec((1, d), fixed)
    return pl.pallas_call(
        _ctx_proj_kernel,
        grid=(n // tn,),
        in_specs=[pl.BlockSpec((m, d), fixed), vec, vec, vec,
                  pl.BlockSpec((d, tn), lambda j: (0, j))],
        out_specs=[pl.BlockSpec((m, tn), lambda j: (0, j)),
                   pl.BlockSpec((m, 512), fixed),
                   pl.BlockSpec((m, 512), fixed),
                   pl.BlockSpec((d, tn), lambda j: (0, j))],
        out_shape=[jax.ShapeDtypeStruct((m, n), BF16),
                   jax.ShapeDtypeStruct((m, 512), BF16),
                   jax.ShapeDtypeStruct((m, 512), BF16),
                   jax.ShapeDtypeStruct((d, n), BF16)],
        scratch_shapes=[pltpu.VMEM((m, d), BF16)],
        compiler_params=_params("arbitrary"),
        name="ctx_proj",
    )(x, g, sh, sc, w)


def _pair_lg(dec_ref, d, p, shape):
    lane = lax.broadcasted_iota(jnp.int32, shape, 1)
    first = (lane % LANES) < 64
    raw = jnp.where(first, jnp.full(shape, dec_ref[d, 2 * p], F32), jnp.full(shape, dec_ref[d, 2 * p + 1], F32))
    return -jnp.exp(raw)


def _head_block_mask(shape):
    r = lax.broadcasted_iota(jnp.int32, shape, 0)
    c = lax.broadcasted_iota(jnp.int32, shape, 1)
    return (r // 64) == (c // LANES)


def _kv_pair(k_pair, v_pair, w):
    kw = (k_pair.astype(F32) * w).astype(BF16)
    kv = lax.dot_general(kw, v_pair, (((0,), (0,)), ((), ())), preferred_element_type=F32)
    return jnp.where(_head_block_mask(kv.shape), kv, 0.0)


def _ret_state_kernel(dec_ref, kf_ref, vf_ref, kb_ref, vb_ref, ck_ref, cv_ref,
                      sf_ref, sb_ref, sfs, sbs):
    i = pl.program_id(0)
    C = RET_CHUNK
    lc = ck_ref.shape[0]

    @pl.when(i == 0)
    def _():
        pos = lax.broadcasted_iota(jnp.int32, (lc, LANES), 0).astype(F32)
        for p in range(RET_PAIRS):
            ks = slice(p * LANES, (p + 1) * LANES)
            vs = slice(p * 2 * RET_DV, (p + 1) * 2 * RET_DV)
            wf = jnp.exp(_pair_lg(dec_ref, 0, p, (lc, LANES)) * (lc - 1.0 - pos))
            wb = jnp.exp(_pair_lg(dec_ref, 1, p, (lc, LANES)) * pos)
            sfs[p] = _kv_pair(ck_ref[:, ks], cv_ref[:, vs], wf)
            sbs[p] = _kv_pair(ck_ref[:, ks], cv_ref[:, vs], wb)

    pos = lax.broadcasted_iota(jnp.int32, (C, LANES), 0).astype(F32)
    for p in range(RET_PAIRS):
        ks = slice(p * LANES, (p + 1) * LANES)
        vs = slice(p * 2 * RET_DV, (p + 1) * 2 * RET_DV)
        wf = jnp.exp(_pair_lg(dec_ref, 0, p, (C, LANES)) * (C - 1.0 - pos))
        wb = jnp.exp(_pair_lg(dec_ref, 1, p, (C, LANES)) * pos)
        rowh = lax.broadcasted_iota(jnp.int32, (LANES, 2 * RET_DV), 0) < 64
        gf = jnp.exp(-jnp.exp(jnp.where(rowh, jnp.full(rowh.shape, dec_ref[0, 2 * p], F32),
                                         jnp.full(rowh.shape, dec_ref[0, 2 * p + 1], F32))) * float(C))
        gb = jnp.exp(-jnp.exp(jnp.where(rowh, jnp.full(rowh.shape, dec_ref[1, 2 * p], F32),
                                         jnp.full(rowh.shape, dec_ref[1, 2 * p + 1], F32))) * float(C))
        sf = sfs[p]
        for cc in range(RET_STEP_CHUNKS):
            rs = slice(cc * C, (cc + 1) * C)
            sf_ref[cc, p] = sf.astype(BF16)
            sf = gf * sf + _kv_pair(kf_ref[rs, ks], vf_ref[rs, vs], wf)
        sfs[p] = sf
        sb = sbs[p]
        for cc in reversed(range(RET_STEP_CHUNKS)):
            rs = slice(cc * C, (cc + 1) * C)
            sb_ref[cc, p] = sb.astype(BF16)
            sb = gb * sb + _kv_pair(kb_ref[rs, ks], vb_ref[rs, vs], wb)
        sbs[p] = sb


def _ret_states(dec, proj, cproj):
    L = proj.shape[0]
    lc = cproj.shape[0]
    S = RET_STEP_CHUNKS
    R = S * RET_CHUNK
    n = L // R
    st = pl.BlockSpec((S, RET_PAIRS, LANES, 2 * RET_DV), lambda i: (i, 0, 0, 0))
    st_rev = pl.BlockSpec((S, RET_PAIRS, LANES, 2 * RET_DV), lambda i: (n - 1 - i, 0, 0, 0))
    shp = jax.ShapeDtypeStruct((n * S, RET_PAIRS, LANES, 2 * RET_DV), BF16)
    return pl.pallas_call(
        _ret_state_kernel,
        grid=(n,),
        in_specs=[pl.BlockSpec(memory_space=pltpu.SMEM),
                  pl.BlockSpec((R, 512), lambda i: (i, 1)),
                  pl.BlockSpec((R, 1024), lambda i: (i, 1)),
                  pl.BlockSpec((R, 512), lambda i: (n - 1 - i, 1)),
                  pl.BlockSpec((R, 1024), lambda i: (n - 1 - i, 1)),
                  pl.BlockSpec((lc, 512), lambda i: (0, 1)),
                  pl.BlockSpec((lc, 1024), lambda i: (0, 1))],
        out_specs=[st, st_rev],
        out_shape=[shp, shp],
        scratch_shapes=[pltpu.VMEM((RET_PAIRS, LANES, 2 * RET_DV), F32),
                        pltpu.VMEM((RET_PAIRS, LANES, 2 * RET_DV), F32)],
        compiler_params=_params("arbitrary"),
        name="ret_state",
    )(dec, proj, proj, proj, proj, cproj, cproj)


def _ret_out_kernel(dec_ref, q_ref, k_ref, v_ref, g_ref, sf_ref, sb_ref, o_ref):
    C = RET_CHUNK
    pos = lax.broadcasted_iota(jnp.int32, (C, LANES), 0).astype(F32)
    n_i = lax.broadcasted_iota(jnp.int32, (C, 2 * C), 0)
    m_i = lax.broadcasted_iota(jnp.int32, (C, 2 * C), 1) % C
    rel = (n_i - m_i).astype(F32)
    lane = lax.broadcasted_iota(jnp.int32, (C, LANES), 1)
    lo = lane < 64
    for p in range(RET_PAIRS):
        ks = slice(p * LANES, (p + 1) * LANES)
        vs = slice(p * 2 * RET_DV, (p + 1) * 2 * RET_DV)
        col_a = lax.broadcasted_iota(jnp.int32, (C, 2 * C), 1) < C
        raw_f = jnp.where(col_a, jnp.full((C, 2 * C), dec_ref[0, 2 * p], F32), jnp.full((C, 2 * C), dec_ref[0, 2 * p + 1], F32))
        raw_b = jnp.where(col_a, jnp.full((C, 2 * C), dec_ref[1, 2 * p], F32), jnp.full((C, 2 * C), dec_ref[1, 2 * p + 1], F32))
        dmat = jnp.where(rel >= 0, jnp.exp(-jnp.exp(raw_f) * jnp.maximum(rel, 0.0)),
                         jnp.exp(-jnp.exp(raw_b) * jnp.maximum(-rel, 0.0)))
        wqf = jnp.exp(_pair_lg(dec_ref, 0, p, (C, LANES)) * (pos + 1.0))
        wqb = jnp.exp(_pair_lg(dec_ref, 1, p, (C, LANES)) * (float(C) - pos))
        for cc in range(RET_STEP_CHUNKS):
            rs = slice(cc * C, (cc + 1) * C)
            q = q_ref[rs, ks]
            k = k_ref[rs, ks]
            v = v_ref[rs, vs]
            zk = jnp.zeros_like(k)
            kst = jnp.concatenate([jnp.where(lo, k, zk), jnp.where(lo, zk, k)], axis=0)
            s = lax.dot_general(q, kst, (((1,), (1,)), ((), ())), preferred_element_type=F32)
            sd = (s * dmat).astype(BF16)
            qf32 = q.astype(F32)
            qwf = (qf32 * wqf).astype(BF16)
            qwb = (qf32 * wqb).astype(BF16)
            zv = jnp.zeros((C, RET_DV), BF16)
            vbd = jnp.concatenate([jnp.concatenate([v[:, :RET_DV], zv], axis=1),
                                   jnp.concatenate([zv, v[:, RET_DV:]], axis=1)], axis=0)
            lhs = jnp.concatenate([sd, qwf, qwb], axis=1)
            rhs = jnp.concatenate([vbd, sf_ref[cc, p], sb_ref[cc, p]], axis=0)
            o = jnp.dot(lhs, rhs, preferred_element_type=F32)
            for t in range(2):
                oh = o[:, t * RET_DV:(t + 1) * RET_DV]
                oh = oh * lax.rsqrt(jnp.mean(oh * oh, axis=-1, keepdims=True) + NORM_EPS)
                cs = slice(p * 2 * RET_DV + t * RET_DV, p * 2 * RET_DV + (t + 1) * RET_DV)
                gt = g_ref[rs, cs].astype(F32)
                o_ref[rs, cs] = (oh * (gt / (1.0 + jnp.exp(-gt)))).astype(BF16)


def _ret_out(dec, proj, sf, sb):
    L = proj.shape[0]
    S = RET_STEP_CHUNKS
    R = S * RET_CHUNK
    n = L // R
    st = pl.BlockSpec((S, RET_PAIRS, LANES, 2 * RET_DV), lambda i: (i, 0, 0, 0))
    return pl.pallas_call(
        _ret_out_kernel,
        grid=(n,),
        in_specs=[pl.BlockSpec(memory_space=pltpu.SMEM),
                  pl.BlockSpec((R, 512), lambda i: (i, 0)),
                  pl.BlockSpec((R, 512), lambda i: (i, 1)),
                  pl.BlockSpec((R, 1024), lambda i: (i, 1)),
                  pl.BlockSpec((R, 1024), lambda i: (i, 2)),
                  st, st],
        out_specs=pl.BlockSpec((R, RET_HEADS * RET_DV), lambda i: (i, 0)),
        out_shape=jax.ShapeDtypeStruct((L, RET_HEADS * RET_DV), BF16),
        compiler_params=_params("parallel"),
        name="ret_out",
    )(dec, proj, proj, proj, proj, sf, sb)


def _attn_kernel(sink_ref, q_ref, kp_ref, kc_ref, kn_ref, vp_ref, vc_ref, vn_ref, ck_ref, cv_ref, o_ref):
    n = pl.program_id(0)
    nstep = pl.num_programs(0)
    B = ATT_BLOCK
    SB = ATT_STEP_BLOCKS
    kj = lax.broadcasted_iota(jnp.int32, (B, B), 0)
    qi = lax.broadcasted_iota(jnp.int32, (B, B), 1)
    ok_prev = jnp.where(n > 0, 0.0, MASK_NEG).astype(F32)
    ok_next = jnp.where(n < nstep - 1, 0.0, MASK_NEG).astype(F32)

    def band(inside, ok):
        return jnp.concatenate([jnp.where(inside, ok, MASK_NEG).astype(F32)] * ATT_GROUP, axis=1)

    bias_prev = [band(kj >= qi, ok_prev if j == 0 else 0.0) for j in range(SB)]
    bias_next = [band(kj <= qi, ok_next if j == SB - 1 else 0.0) for j in range(SB)]
    lane = lax.broadcasted_iota(jnp.int32, (B, LANES), 1)
    lo = lane < 64
    hi = lane >= 64

    def keys_of(j, gs, prev_ref, cur_ref, next_ref, ctx_ref):
        prev = prev_ref[:, gs] if j == 0 else cur_ref[(j - 1) * B:j * B, gs]
        nxt = next_ref[:, gs] if j == SB - 1 else cur_ref[(j + 1) * B:(j + 2) * B, gs]
        return jnp.concatenate([prev, cur_ref[j * B:(j + 1) * B, gs], nxt, ctx_ref[:, gs]], axis=0)

    def scores(j, g):
        gs = slice(g * LANES, (g + 1) * LANES)
        kcat = keys_of(j, gs, kp_ref, kc_ref, kn_ref, ck_ref)
        qs = []
        for r in range(ATT_GROUP):
            h = ATT_GROUP * g + r
            qt = q_ref[j * B:(j + 1) * B, (h // 2) * LANES:(h // 2 + 1) * LANES]
            keep = lo if h % 2 == 0 else hi
            qs.append(jnp.where(keep, qt, jnp.zeros_like(qt)))
        q4 = jnp.concatenate(qs, axis=0)
        return lax.dot_general(kcat, q4, (((1,), (1,)), ((), ())), preferred_element_type=F32)

    def softmax(j, g, s):
        sk = jnp.concatenate([jnp.full((1, B), sink_ref[ATT_GROUP * g + r], F32)
                              for r in range(ATT_GROUP)], axis=1) * LOG2E
        s = jnp.concatenate([s[:B] + bias_prev[j], s[B:2 * B], s[2 * B:3 * B] + bias_next[j], s[3 * B:]], axis=0)
        m = jnp.maximum(jnp.max(s, axis=0, keepdims=True), sk)
        e = jnp.exp2(s - m)
        den = jnp.sum(e, axis=0, keepdims=True) + jnp.exp2(sk - m)
        return e.astype(BF16), den

    def values(j, g, e, den):
        gs = slice(g * LANES, (g + 1) * LANES)
        vcat = keys_of(j, gs, vp_ref, vc_ref, vn_ref, cv_ref)
        res = lax.dot_general(vcat, e, (((0,), (0,)), ((), ())), preferred_element_type=F32) * (1.0 / den)
        for t in range(2):
            even = res[:, (2 * t) * B:(2 * t + 1) * B].T
            odd = res[:, (2 * t + 1) * B:(2 * t + 2) * B].T
            c0 = (2 * g + t) * LANES
            o_ref[j * B:(j + 1) * B, c0:c0 + LANES] = jnp.where(lo, even, odd).astype(BF16)

    units = [(j, g) for j in range(SB) for g in range(ATT_KV_HEADS)]
    s_next = scores(*units[0])
    pending = None
    for u, unit in enumerate(units):
        s_cur = s_next
        if u + 1 < len(units):
            s_next = scores(*units[u + 1])
        e_den = softmax(*unit, s_cur)
        if pending is not None:
            values(*units[u - 1], *pending)
        pending = e_den
    values(*units[-1], *pending)


def _attn(sink, proj, kd, vd, ckd, cvd, riders):
    L = proj.shape[0]
    B = ATT_BLOCK
    SB = ATT_STEP_BLOCKS
    n = L // (SB * B)
    nb = L // B
    lc = ckd.shape[0]
    prev = pl.BlockSpec((B, 512), lambda i: (jnp.maximum(i * SB - 1, 0), 0))
    cur = pl.BlockSpec((SB * B, 512), lambda i: (i, 0))
    nxt = pl.BlockSpec((B, 512), lambda i: (jnp.minimum((i + 1) * SB, nb - 1), 0))
    full = pl.BlockSpec((lc, 512), lambda i: (0, 0))
    rid_in_specs, rid_out_specs, rid_shapes = _rider_specs(riders, n)
    return pl.pallas_call(
        _with_cast_riders(_attn_kernel, 10, 1, len(riders)),
        grid=(n,),
        in_specs=[pl.BlockSpec(memory_space=pltpu.SMEM),
                  pl.BlockSpec((SB * B, 1024), lambda i: (i, 3)),
                  prev, cur, nxt, prev, cur, nxt, full, full] + rid_in_specs,
        out_specs=[pl.BlockSpec((SB * B, ATT_HEADS * ATT_DH), lambda i: (i, 0))] + rid_out_specs,
        out_shape=[jax.ShapeDtypeStruct((L, ATT_HEADS * ATT_DH), BF16)] + rid_shapes,
        compiler_params=_params("parallel"),
        name="attn",
    )(sink, proj, kd, kd, kd, vd, vd, vd, ckd, cvd, *[r[0] for r in riders])


def _out_proj_kernel(yr_ref, ya_ref, w_ref, x_ref, gt_ref, g_ref, sh_ref, sc_ref, o_ref, h_ref):
    kr = yr_ref.shape[1]
    for r in range(yr_ref.shape[0] // OUT_ROW_CHUNK):
        rs = slice(r * OUT_ROW_CHUNK, (r + 1) * OUT_ROW_CHUNK)
        acc = jnp.dot(yr_ref[rs, :], w_ref[:kr, :], preferred_element_type=F32)
        acc = acc + jnp.dot(ya_ref[rs, :], w_ref[kr:, :], preferred_element_type=F32)
        x1 = x_ref[rs, :] + gt_ref[...] * acc
        o_ref[rs, :] = x1
        y = x1 * lax.rsqrt(jnp.mean(x1 * x1, axis=-1, keepdims=True) + NORM_EPS)
        y = y * g_ref[...]
        h_ref[rs, :] = (y * (1.0 + sc_ref[...]) + sh_ref[...]).astype(BF16)


def _out_proj(yr, ya, w, x, gt, g, sh, sc, *, tm):
    m, d = x.shape
    kr, ka = yr.shape[1], ya.shape[1]
    row = lambda i: (i, 0)
    vec = pl.BlockSpec((1, d), lambda i: (0, 0))
    return pl.pallas_call(
        _out_proj_kernel,
        grid=(m // tm,),
        in_specs=[pl.BlockSpec((tm, kr), row), pl.BlockSpec((tm, ka), row),
                  pl.BlockSpec((kr + ka, d), lambda i: (0, 0)),
                  pl.BlockSpec((tm, d), row), vec, vec, vec, vec],
        out_specs=[pl.BlockSpec((tm, d), row), pl.BlockSpec((tm, d), row)],
        out_shape=[jax.ShapeDtypeStruct((m, d), F32), jax.ShapeDtypeStruct((m, d), BF16)],
        compiler_params=_params("parallel"),
        name="out_proj",
    )(yr, ya, w, x, gt, g, sh, sc)


def _ffn_weight_copies(wg_hbm, wu_hbm, wd_hbm, wg_buf, wu_buf, wd_buf, sem, f):
    slot = f % FFN_SLOTS
    return (pltpu.make_async_copy(wg_hbm.at[f], wg_buf.at[slot], sem.at[0, slot]),
            pltpu.make_async_copy(wu_hbm.at[f], wu_buf.at[slot], sem.at[1, slot]),
            pltpu.make_async_copy(wd_hbm.at[f * FFN_TILE:(f + 1) * FFN_TILE, :], wd_buf.at[slot], sem.at[2, slot]))


def _ffn_kernel(h_ref, x_ref, gt_ref, gfin_ref, wg_hbm, wu_hbm, wd_hbm, o_ref, wg_buf, wu_buf, wd_buf, sem):
    i = pl.program_id(0)
    nf = wg_hbm.shape[0]
    groups = [list(range(s, min(s + FFN_GROUP, nf))) for s in range(0, nf, FFN_GROUP)]
    slots = lambda g: {f % FFN_SLOTS for f in groups[g % len(groups)]}
    assert all(not (slots(g) & slots(g + 1)) for g in range(len(groups))), "prefetch would hit a live slot"

    def copies(f):
        return _ffn_weight_copies(wg_hbm, wu_hbm, wd_hbm, wg_buf, wu_buf, wd_buf, sem, f)

    @pl.when(i == 0)
    def _():
        for f in groups[0]:
            for cp in copies(f):
                cp.start()

    rows = h_ref.shape[0]
    for g, tiles in enumerate(groups):
        for f in tiles:
            for cp in copies(f):
                cp.wait()
        for f in groups[(g + 1) % len(groups)]:
            for cp in copies(f):
                cp.start()
        for f in tiles:
            slot = f % FFN_SLOTS
            for r in range(rows // FFN_ROW_CHUNK):
                rs = slice(r * FFN_ROW_CHUNK, (r + 1) * FFN_ROW_CHUNK)
                h = h_ref[rs, :]
                a = jnp.dot(h, wg_buf[slot], preferred_element_type=F32)
                u = jnp.dot(h, wu_buf[slot], preferred_element_type=F32)
                act = ((a / (1.0 + jnp.exp(-a))) * u).astype(BF16)
                part = jnp.dot(act, wd_buf[slot], preferred_element_type=F32)
                if f == 0:
                    o_ref[rs, :] = part
                else:
                    o_ref[rs, :] += part

    for r in range(rows // FFN_ROW_CHUNK):
        rs = slice(r * FFN_ROW_CHUNK, (r + 1) * FFN_ROW_CHUNK)
        y = x_ref[rs, :] + gt_ref[...] * o_ref[rs, :]
        y = y * lax.rsqrt(jnp.mean(y * y, axis=-1, keepdims=True) + NORM_EPS)
        o_ref[rs, :] = y * gfin_ref[...]

    @pl.when(i == pl.num_programs(0) - 1)
    def _():
        for f in groups[0]:
            for cp in copies(f):
                cp.wait()


def _ffn(h, x, gt, gfin, wg, wu, wd, *, tm):
    m, d = x.shape
    assert wg.shape == wu.shape == (wd.shape[0] // FFN_TILE, d, FFN_TILE)
    assert m % tm == 0 and tm % FFN_ROW_CHUNK == 0
    row = lambda i: (i, 0)
    vec = pl.BlockSpec((1, d), lambda i: (0, 0))
    hbm = pl.BlockSpec(memory_space=pl.ANY)
    return pl.pallas_call(
        _ffn_kernel,
        grid=(m // tm,),
        in_specs=[pl.BlockSpec((tm, d), row), pl.BlockSpec((tm, d), row), vec, vec, hbm, hbm, hbm],
        out_specs=pl.BlockSpec((tm, d), row),
        out_shape=jax.ShapeDtypeStruct((m, d), F32),
        scratch_shapes=[pltpu.VMEM((FFN_SLOTS, d, FFN_TILE), BF16),
                        pltpu.VMEM((FFN_SLOTS, d, FFN_TILE), BF16),
                        pltpu.VMEM((FFN_SLOTS, FFN_TILE, d), BF16),
                        pltpu.SemaphoreType.DMA((3, FFN_SLOTS))],
        compiler_params=_params("arbitrary"),
        name="ffn",
    )(h, x, gt, gfin, wg, wu, wd)


def _rope_tables(L):
    f32 = np.float32
    lane = np.arange(LANES)
    inv1 = f32(ROPE_BASE) ** (-np.arange(32, dtype=f32) / f32(32))
    ang1 = np.arange(L, dtype=f32)[:, None] * inv1[None, :]
    sgn1 = np.where((lane % 64) < 32, -1.0, 1.0).astype(f32)
    cos1 = np.tile(np.cos(ang1), (1, LANES // 32))
    sin1 = np.tile(np.sin(ang1), (1, LANES // 32)) * sgn1[None, :]
    inv2 = f32(ROPE_BASE) ** (-np.arange(16, dtype=f32) / f32(16))
    nrow = L // GRID_W
    ang_r = np.arange(nrow, dtype=f32)[:, None] * inv2[None, :]
    ang_c = np.arange(GRID_W, dtype=f32)[:, None] * inv2[None, :]
    sgna = np.where((lane % 32) < 16, -1.0, 1.0).astype(f32)

    def expand(fr, fc):
        by_row = np.broadcast_to(np.tile(fr, (1, 2))[:, None, :], (nrow, GRID_W, 32))
        by_col = np.broadcast_to(np.tile(fc, (1, 2))[None, :, :], (nrow, GRID_W, 32))
        head = np.concatenate([by_row, by_col], axis=-1).reshape(L, 64)
        return np.tile(head, (1, LANES // 64))

    cosa = expand(np.cos(ang_r), np.cos(ang_c))
    sina = expand(np.sin(ang_r), np.sin(ang_c)) * sgna[None, :]
    return tuple(np.ascontiguousarray(t, dtype=f32) for t in (cos1, sin1, cosa, sina))


def kernel(x, c, ctx, c_ctx, w_mod, b_mod, norm_mix, norm_ffn, w_in, ret_decay, attn_sink,
           w_out, w_gate, w_up, w_down, norm_final):
    B, L, D = x.shape
    assert B == 1 and w_mod.shape[0] == 1, "single batch element, depth-1 layer"
    x2 = x[0]
    xc2 = ctx[0]

    cv = jnp.zeros((8, D), F32).at[0].set(c[0]).at[1].set(c_ctx)
    mod = _mod(cv, w_mod[0], b_mod[0][None, :])
    sh_m, sc_m, gt_m, sh_f, sc_f, gt_f = [mod[0:1, k * D:(k + 1) * D] for k in range(6)]
    sh_mc, sc_mc = mod[1:2, 0:D], mod[1:2, D:2 * D]

    g_mix = norm_mix[0][None, :]
    cproj, ckd, cvd, w_in_b = _ctx_proj(xc2, g_mix, sh_mc, sc_mc, w_in[0])
    proj, kd, vd = _in_proj(x2, g_mix, sh_m, sc_m, w_in_b, _rope_tables(L), tm=512)

    dec = ret_decay[0].astype(F32)
    sf, sb = _ret_states(dec, proj, cproj)
    y_ret = _ret_out(dec, proj, sf, sb)
    y_att, w_gate_b, w_up_b, w_down_b, w_out_b = _attn(
        attn_sink[0].astype(F32), proj, kd, vd, ckd, cvd,
        [(w_gate[0], 1, FFN_TILE), (w_up[0], 1, FFN_TILE), (w_down[0], 2, None), (w_out[0], 1, None)])

    x1, hff = _out_proj(y_ret, y_att, w_out_b, x2, gt_m, norm_ffn[0][None, :], sh_f, sc_f, tm=512)
    out = _ffn(hff, x1, gt_f, norm_final[None, :], w_gate_b, w_up_b, w_down_b, tm=512)
    return out[None]
```

```python
import jax
import jax.numpy as jnp
import numpy as np
from jax import lax
from jax.experimental import pallas as pl
from jax.experimental.pallas import tpu as pltpu

GRID_W = 64
RET_HEADS = 8
RET_DK = 64
RET_DV = 128
RET_CHUNK = 128
ATT_HEADS = 16
ATT_KV_HEADS = 4
ATT_DH = 64
ATT_GROUP = ATT_HEADS // ATT_KV_HEADS
WINDOW = 128
ATT_BLOCK = 128
ROPE_BASE = 10000.0
NORM_EPS = 1e-6
K_SCALE = RET_DK ** -0.5
ATT_SCALE = ATT_DH ** -0.5
LOG2E = 1.4426950408889634

LANES = 128
RET_PAIRS = RET_HEADS // 2
MASK_NEG = -1e30
VMEM_LIMIT = 56 * 1024 * 1024
RET_STEP_CHUNKS = 4
ATT_STEP_BLOCKS = 2
OUT_ROW_CHUNK = 256
IN_ROW_CHUNK = 256
FFN_TILE = 512
FFN_ROW_CHUNK = 256

BF16 = jnp.bfloat16
F32 = jnp.float32


def _params(*sem):
    return pltpu.CompilerParams(dimension_semantics=sem, vmem_limit_bytes=VMEM_LIMIT)


def _with_cast_riders(body, n_in, n_out, n_rid):
    def wrapped(*refs):
        ins = refs[:n_in]
        rid_in = refs[n_in:n_in + n_rid]
        outs = refs[n_in + n_rid:n_in + n_rid + n_out]
        rid_out = refs[n_in + n_rid + n_out:n_in + 2 * n_rid + n_out]
        scratch = refs[n_in + 2 * n_rid + n_out:]
        for src, dst in zip(rid_in, rid_out):
            if len(dst.shape) == 2:
                dst[...] = src[...].astype(BF16)
            else:
                tc = dst.shape[2]
                for t in range(dst.shape[0]):
                    dst[t] = src[:, t * tc:(t + 1) * tc].astype(BF16)
        body(*ins, *outs, *scratch)
    return wrapped


def _rider_specs(riders, steps):
    in_specs, out_specs, shapes = [], [], []
    for w, ncb, tile in riders:
        rows, cols = w.shape
        nrb = steps // ncb
        assert nrb * ncb == steps and rows % nrb == 0 and cols % ncb == 0
        br, bc = rows // nrb, cols // ncb
        assert br % 16 == 0 and bc % LANES == 0, "slab must be bf16-tile aligned"
        in_specs.append(pl.BlockSpec((br, bc), lambda i, ncb=ncb: (i // ncb, i % ncb)))
        if tile is None:
            out_specs.append(in_specs[-1])
            shapes.append(jax.ShapeDtypeStruct(w.shape, BF16))
        else:
            assert ncb == 1 and cols % tile == 0 and tile % LANES == 0
            out_specs.append(pl.BlockSpec((cols // tile, br, tile), lambda i: (0, i, 0)))
            shapes.append(jax.ShapeDtypeStruct((cols // tile, rows, tile), BF16))
    return in_specs, out_specs, shapes


def _mod_kernel(cv_ref, w_ref, b_ref, o_ref):
    cv = cv_ref[...]
    s = cv / (1.0 + jnp.exp(-cv))
    o_ref[...] = jnp.dot(s.astype(BF16), w_ref[...].astype(BF16),
                         preferred_element_type=F32) + b_ref[...]


def _mod(cv, w, b):
    d, n = w.shape
    tn = 1024
    return pl.pallas_call(
        _mod_kernel,
        grid=(n // tn,),
        in_specs=[pl.BlockSpec((8, d), lambda j: (0, 0)),
                  pl.BlockSpec((d, tn), lambda j: (0, j)),
                  pl.BlockSpec((1, tn), lambda j: (0, j))],
        out_specs=pl.BlockSpec((8, tn), lambda j: (0, j)),
        out_shape=jax.ShapeDtypeStruct((8, n), F32),
        compiler_params=_params("parallel"),
        name="mod",
    )(cv, w, b)


def _rot_pairs(a, cos, sin_signed, half):
    lane = lax.broadcasted_iota(jnp.int32, a.shape, 1)
    first = (lane % (2 * half)) < half
    rot = jnp.where(first, pltpu.roll(a, LANES - half, 1), pltpu.roll(a, half, 1))
    return a * cos + rot * sin_signed


def _dup_halves(a):
    lane = lax.broadcasted_iota(jnp.int32, a.shape, 1)
    r = pltpu.roll(a, 64, 1)
    lo = lane < 64
    return jnp.where(lo, a, r), jnp.where(lo, r, a)


_PROJ_TILE = 512
_PROJ_TILE_KINDS = ("ret_q", "ret_k", "plain", "plain", "plain", "plain", "att_q", "att_q", "att_kv")


def _in_proj_kernel(x_ref, g_ref, sh_ref, sc_ref, w_ref, c1_ref, s1_ref, ca_ref, sa_ref,
                    o_ref, kd_ref, vd_ref):
    tn = _PROJ_TILE
    for r in range(x_ref.shape[0] // IN_ROW_CHUNK):
        rs = slice(r * IN_ROW_CHUNK, (r + 1) * IN_ROW_CHUNK)
        xf = x_ref[rs, :]
        y = xf * lax.rsqrt(jnp.mean(xf * xf, axis=-1, keepdims=True) + NORM_EPS)
        y = y * g_ref[...]
        h = (y * (1.0 + sc_ref[...]) + sh_ref[...]).astype(BF16)

        def rope1(a):
            return _rot_pairs(a, c1_ref[rs, :], s1_ref[rs, :], 32)

        def ropea(a):
            return _rot_pairs(a, ca_ref[rs, :], sa_ref[rs, :], 16)

        for j, kind in enumerate(_PROJ_TILE_KINDS):
            acc = jnp.dot(h, w_ref[:, j * tn:(j + 1) * tn], preferred_element_type=F32)
            for c in range(tn // LANES):
                a = acc[:, c * LANES:(c + 1) * LANES]
                if kind == "ret_q":
                    a = rope1(a)
                elif kind == "ret_k":
                    a = rope1(a) * K_SCALE
                elif kind == "att_q":
                    a = ropea(a) * (ATT_SCALE * LOG2E)
                elif kind == "att_kv" and c < 2:
                    a = ropea(a)
                o_ref[rs, j * tn + c * LANES:j * tn + (c + 1) * LANES] = a.astype(BF16)
                if kind == "att_kv":
                    dup_ref = kd_ref if c < 2 else vd_ref
                    d0, d1 = _dup_halves(a)
                    t = 2 * (c % 2)
                    dup_ref[rs, t * LANES:(t + 1) * LANES] = d0.astype(BF16)
                    dup_ref[rs, (t + 1) * LANES:(t + 2) * LANES] = d1.astype(BF16)


def _in_proj(x, g, sh, sc, w, tabs, *, tm):
    m, d = x.shape
    n = w.shape[1]
    assert n == _PROJ_TILE * len(_PROJ_TILE_KINDS) and m % tm == 0 and tm % IN_ROW_CHUNK == 0
    c1, s1, ca, sa = tabs
    row = lambda i: (i, 0)
    vec = pl.BlockSpec((1, d), lambda i: (0, 0))
    tab = pl.BlockSpec((tm, LANES), row)
    return pl.pallas_call(
        _in_proj_kernel,
        grid=(m // tm,),
        in_specs=[pl.BlockSpec((tm, d), row), vec, vec, vec,
                  pl.BlockSpec((d, n), lambda i: (0, 0), pipeline_mode=pl.Buffered(1)),
                  tab, tab, tab, tab],
        out_specs=[pl.BlockSpec((tm, n), row),
                   pl.BlockSpec((tm, 512), row),
                   pl.BlockSpec((tm, 512), row)],
        out_shape=[jax.ShapeDtypeStruct((m, n), BF16),
                   jax.ShapeDtypeStruct((m, 512), BF16),
                   jax.ShapeDtypeStruct((m, 512), BF16)],
        compiler_params=_params("parallel"),
        name="in_proj",
    )(x, g, sh, sc, w, c1, s1, ca, sa)


def _ctx_proj_kernel(x_ref, g_ref, sh_ref, sc_ref, w_ref, o_ref, kd_ref, vd_ref, wb_ref, h_ref):
    j = pl.program_id(0)

    @pl.when(j == 0)
    def _():
        xf = x_ref[...]
        y = xf * lax.rsqrt(jnp.mean(xf * xf, axis=-1, keepdims=True) + NORM_EPS)
        y = y * g_ref[...]
        h_ref[...] = (y * (1.0 + sc_ref[...]) + sh_ref[...]).astype(BF16)

    wb = w_ref[...].astype(BF16)
    wb_ref[...] = wb
    acc = jnp.dot(h_ref[...], wb, preferred_element_type=F32)
    is_ret_k = _PROJ_TILE_KINDS.index("ret_k")
    o_ref[...] = (acc * jnp.where(j == is_ret_k, K_SCALE, 1.0)).astype(BF16)

    @pl.when(j == _PROJ_TILE_KINDS.index("att_kv"))
    def _():
        for c in range(_PROJ_TILE // LANES):
            dup_ref = kd_ref if c < 2 else vd_ref
            d0, d1 = _dup_halves(acc[:, c * LANES:(c + 1) * LANES])
            t = 2 * (c % 2)
            dup_ref[:, t * LANES:(t + 1) * LANES] = d0.astype(BF16)
            dup_ref[:, (t + 1) * LANES:(t + 2) * LANES] = d1.astype(BF16)


def _ctx_proj(x, g, sh, sc, w):
    m, d = x.shape
    n = w.shape[1]
    tn = _PROJ_TILE
    assert n == tn * len(_PROJ_TILE_KINDS)
    fixed = lambda j: (0, 0)
    vec = pl.BlockSpec((1, d), fixed)
    return pl.pallas_call(
        _ctx_proj_kernel,
        grid=(n // tn,),
        in_specs=[pl.BlockSpec((m, d), fixed), vec, vec, vec,
                  pl.BlockSpec((d, tn), lambda j: (0, j))],
        out_specs=[pl.BlockSpec((m, tn), lambda j: (0, j)),
                   pl.BlockSpec((m, 512), fixed),
                   pl.BlockSpec((m, 512), fixed),
                   pl.BlockSpec((d, tn), lambda j: (0, j))],
        out_shape=[jax.ShapeDtypeStruct((m, n), BF16),
                   jax.ShapeDtypeStruct((m, 512), BF16),
                   jax.ShapeDtypeStruct((m, 512), BF16),
                   jax.ShapeDtypeStruct((d, n), BF16)],
        scratch_shapes=[pltpu.VMEM((m, d), BF16)],
        compiler_params=_params("arbitrary"),
        name="ctx_proj",
    )(x, g, sh, sc, w)


def _pair_lg(dec_ref, d, p, shape):
    lane = lax.broadcasted_iota(jnp.int32, shape, 1)
    first = (lane % LANES) < 64
    raw = jnp.where(first, jnp.full(shape, dec_ref[d, 2 * p], F32), jnp.full(shape, dec_ref[d, 2 * p + 1], F32))
    return -jnp.exp(raw)


def _head_block_mask(shape):
    r = lax.broadcasted_iota(jnp.int32, shape, 0)
    c = lax.broadcasted_iota(jnp.int32, shape, 1)
    return (r // 64) == (c // LANES)


def _kv_pair(k_pair, v_pair, w):
    kw = (k_pair.astype(F32) * w).astype(BF16)
    kv = lax.dot_general(kw, v_pair, (((0,), (0,)), ((), ())), preferred_element_type=F32)
    return jnp.where(_head_block_mask(kv.shape), kv, 0.0)


def _ret_state_kernel(dec_ref, kf_ref, vf_ref, kb_ref, vb_ref, ck_ref, cv_ref,
                      sf_ref, sb_ref, sfs, sbs):
    i = pl.program_id(0)
    C = RET_CHUNK
    lc = ck_ref.shape[0]

    @pl.when(i == 0)
    def _():
        pos = lax.broadcasted_iota(jnp.int32, (lc, LANES), 0).astype(F32)
        for p in range(RET_PAIRS):
            ks = slice(p * LANES, (p + 1) * LANES)
            vs = slice(p * 2 * RET_DV, (p + 1) * 2 * RET_DV)
            wf = jnp.exp(_pair_lg(dec_ref, 0, p, (lc, LANES)) * (lc - 1.0 - pos))
            wb = jnp.exp(_pair_lg(dec_ref, 1, p, (lc, LANES)) * pos)
            sfs[p] = _kv_pair(ck_ref[:, ks], cv_ref[:, vs], wf)
            sbs[p] = _kv_pair(ck_ref[:, ks], cv_ref[:, vs], wb)

    pos = lax.broadcasted_iota(jnp.int32, (C, LANES), 0).astype(F32)
    for p in range(RET_PAIRS):
        ks = slice(p * LANES, (p + 1) * LANES)
        vs = slice(p * 2 * RET_DV, (p + 1) * 2 * RET_DV)
        wf = jnp.exp(_pair_lg(dec_ref, 0, p, (C, LANES)) * (C - 1.0 - pos))
        wb = jnp.exp(_pair_lg(dec_ref, 1, p, (C, LANES)) * pos)
        rowh = lax.broadcasted_iota(jnp.int32, (LANES, 2 * RET_DV), 0) < 64
        gf = jnp.exp(-jnp.exp(jnp.where(rowh, jnp.full(rowh.shape, dec_ref[0, 2 * p], F32),
                                         jnp.full(rowh.shape, dec_ref[0, 2 * p + 1], F32))) * float(C))
        gb = jnp.exp(-jnp.exp(jnp.where(rowh, jnp.full(rowh.shape, dec_ref[1, 2 * p], F32),
                                         jnp.full(rowh.shape, dec_ref[1, 2 * p + 1], F32))) * float(C))
        sf = sfs[p]
        for cc in range(RET_STEP_CHUNKS):
            rs = slice(cc * C, (cc + 1) * C)
            sf_ref[cc, p] = sf.astype(BF16)
            sf = gf * sf + _kv_pair(kf_ref[rs, ks], vf_ref[rs, vs], wf)
        sfs[p] = sf
        sb = sbs[p]
        for cc in reversed(range(RET_STEP_CHUNKS)):
            rs = slice(cc * C, (cc + 1) * C)
            sb_ref[cc, p] = sb.astype(BF16)
            sb = gb * sb + _kv_pair(kb_ref[rs, ks], vb_ref[rs, vs], wb)
        sbs[p] = sb


def _ret_states(dec, proj, cproj):
    L = proj.shape[0]
    lc = cproj.shape[0]
    S = RET_STEP_CHUNKS
    R = S * RET_CHUNK
    n = L // R
    st = pl.BlockSpec((S, RET_PAIRS, LANES, 2 * RET_DV), lambda i: (i, 0, 0, 0))
    st_rev = pl.BlockSpec((S, RET_PAIRS, LANES, 2 * RET_DV), lambda i: (n - 1 - i, 0, 0, 0))
    shp = jax.ShapeDtypeStruct((n * S, RET_PAIRS, LANES, 2 * RET_DV), BF16)
    return pl.pallas_call(
        _ret_state_kernel,
        grid=(n,),
        in_specs=[pl.BlockSpec(memory_space=pltpu.SMEM),
                  pl.BlockSpec((R, 512), lambda i: (i, 1)),
                  pl.BlockSpec((R, 1024), lambda i: (i, 1)),
                  pl.BlockSpec((R, 512), lambda i: (n - 1 - i, 1)),
                  pl.BlockSpec((R, 1024), lambda i: (n - 1 - i, 1)),
                  pl.BlockSpec((lc, 512), lambda i: (0, 1)),
                  pl.BlockSpec((lc, 1024), lambda i: (0, 1))],
        out_specs=[st, st_rev],
        out_shape=[shp, shp],
        scratch_shapes=[pltpu.VMEM((RET_PAIRS, LANES, 2 * RET_DV), F32),
                        pltpu.VMEM((RET_PAIRS, LANES, 2 * RET_DV), F32)],
        compiler_params=_params("arbitrary"),
        name="ret_state",
    )(dec, proj, proj, proj, proj, cproj, cproj)


def _ret_out_kernel(dec_ref, q_ref, k_ref, v_ref, g_ref, sf_ref, sb_ref, o_ref):
    C = RET_CHUNK
    pos = lax.broadcasted_iota(jnp.int32, (C, LANES), 0).astype(F32)
    n_i = lax.broadcasted_iota(jnp.int32, (C, 2 * C), 0)
    m_i = lax.broadcasted_iota(jnp.int32, (C, 2 * C), 1) % C
    rel = (n_i - m_i).astype(F32)
    lane = lax.broadcasted_iota(jnp.int32, (C, LANES), 1)
    lo = lane < 64
    for p in range(RET_PAIRS):
        ks = slice(p * LANES, (p + 1) * LANES)
        vs = slice(p * 2 * RET_DV, (p + 1) * 2 * RET_DV)
        col_a = lax.broadcasted_iota(jnp.int32, (C, 2 * C), 1) < C
        raw_f = jnp.where(col_a, jnp.full((C, 2 * C), dec_ref[0, 2 * p], F32), jnp.full((C, 2 * C), dec_ref[0, 2 * p + 1], F32))
        raw_b = jnp.where(col_a, jnp.full((C, 2 * C), dec_ref[1, 2 * p], F32), jnp.full((C, 2 * C), dec_ref[1, 2 * p + 1], F32))
        dmat = jnp.where(rel >= 0, jnp.exp(-jnp.exp(raw_f) * jnp.maximum(rel, 0.0)),
                         jnp.exp(-jnp.exp(raw_b) * jnp.maximum(-rel, 0.0)))
        wqf = jnp.exp(_pair_lg(dec_ref, 0, p, (C, LANES)) * (pos + 1.0))
        wqb = jnp.exp(_pair_lg(dec_ref, 1, p, (C, LANES)) * (float(C) - pos))
        for cc in range(RET_STEP_CHUNKS):
            rs = slice(cc * C, (cc + 1) * C)
            q = q_ref[rs, ks]
            k = k_ref[rs, ks]
            v = v_ref[rs, vs]
            zk = jnp.zeros_like(k)
            kst = jnp.concatenate([jnp.where(lo, k, zk), jnp.where(lo, zk, k)], axis=0)
            s = lax.dot_general(q, kst, (((1,), (1,)), ((), ())), preferred_element_type=F32)
            sd = (s * dmat).astype(BF16)
            qf32 = q.astype(F32)
            qwf = (qf32 * wqf).astype(BF16)
            qwb = (qf32 * wqb).astype(BF16)
            zv = jnp.zeros((C, RET_DV), BF16)
            vbd = jnp.concatenate([jnp.concatenate([v[:, :RET_DV], zv], axis=1),
                                   jnp.concatenate([zv, v[:, RET_DV:]], axis=1)], axis=0)
            lhs = jnp.concatenate([sd, qwf, qwb], axis=1)
            rhs = jnp.concatenate([vbd, sf_ref[cc, p], sb_ref[cc, p]], axis=0)
            o = jnp.dot(lhs, rhs, preferred_element_type=F32)
            for t in range(2):
                oh = o[:, t * RET_DV:(t + 1) * RET_DV]
                oh = oh * lax.rsqrt(jnp.mean(oh * oh, axis=-1, keepdims=True) + NORM_EPS)
                cs = slice(p * 2 * RET_DV + t * RET_DV, p * 2 * RET_DV + (t + 1) * RET_DV)
                gt = g_ref[rs, cs].astype(F32)
                o_ref[rs, cs] = (oh * (gt / (1.0 + jnp.exp(-gt)))).astype(BF16)


def _ret_out(dec, proj, sf, sb):
    L = proj.shape[0]
    S = RET_STEP_CHUNKS
    R = S * RET_CHUNK
    n = L // R
    st = pl.BlockSpec((S, RET_PAIRS, LANES, 2 * RET_DV), lambda i: (i, 0, 0, 0))
    return pl.pallas_call(
        _ret_out_kernel,
        grid=(n,),
        in_specs=[pl.BlockSpec(memory_space=pltpu.SMEM),
                  pl.BlockSpec((R, 512), lambda i: (i, 0)),
                  pl.BlockSpec((R, 512), lambda i: (i, 1)),
                  pl.BlockSpec((R, 1024), lambda i: (i, 1)),
                  pl.BlockSpec((R, 1024), lambda i: (i, 2)),
                  st, st],
        out_specs=pl.BlockSpec((R, RET_HEADS * RET_DV), lambda i: (i, 0)),
        out_shape=jax.ShapeDtypeStruct((L, RET_HEADS * RET_DV), BF16),
        compiler_params=_params("parallel"),
        name="ret_out",
    )(dec, proj, proj, proj, proj, sf, sb)


def _attn_kernel(sink_ref, q_ref, kp_ref, kc_ref, kn_ref, vp_ref, vc_ref, vn_ref, ck_ref, cv_ref, o_ref):
    n = pl.program_id(0)
    nstep = pl.num_programs(0)
    B = ATT_BLOCK
    SB = ATT_STEP_BLOCKS
    kj = lax.broadcasted_iota(jnp.int32, (B, B), 0)
    qi = lax.broadcasted_iota(jnp.int32, (B, B), 1)
    ok_prev = jnp.where(n > 0, 0.0, MASK_NEG).astype(F32)
    ok_next = jnp.where(n < nstep - 1, 0.0, MASK_NEG).astype(F32)

    def band(inside, ok):
        return jnp.concatenate([jnp.where(inside, ok, MASK_NEG).astype(F32)] * ATT_GROUP, axis=1)

    bias_prev = [band(kj >= qi, ok_prev if j == 0 else 0.0) for j in range(SB)]
    bias_next = [band(kj <= qi, ok_next if j == SB - 1 else 0.0) for j in range(SB)]
    lane = lax.broadcasted_iota(jnp.int32, (B, LANES), 1)
    lo = lane < 64
    hi = lane >= 64

    def keys_of(j, gs, prev_ref, cur_ref, next_ref, ctx_ref):
        prev = prev_ref[:, gs] if j == 0 else cur_ref[(j - 1) * B:j * B, gs]
        nxt = next_ref[:, gs] if j == SB - 1 else cur_ref[(j + 1) * B:(j + 2) * B, gs]
        return jnp.concatenate([prev, cur_ref[j * B:(j + 1) * B, gs], nxt, ctx_ref[:, gs]], axis=0)

    def scores(j, g):
        gs = slice(g * LANES, (g + 1) * LANES)
        kcat = keys_of(j, gs, kp_ref, kc_ref, kn_ref, ck_ref)
        qs = []
        for r in range(ATT_GROUP):
            h = ATT_GROUP * g + r
            qt = q_ref[j * B:(j + 1) * B, (h // 2) * LANES:(h // 2 + 1) * LANES]
            keep = lo if h % 2 == 0 else hi
            qs.append(jnp.where(keep, qt, jnp.zeros_like(qt)))
        q4 = jnp.concatenate(qs, axis=0)
        return lax.dot_general(kcat, q4, (((1,), (1,)), ((), ())), preferred_element_type=F32)

    def softmax(j, g, s):
        sk = jnp.concatenate([jnp.full((1, B), sink_ref[ATT_GROUP * g + r], F32)
                              for r in range(ATT_GROUP)], axis=1) * LOG2E
        s = jnp.concatenate([s[:B] + bias_prev[j], s[B:2 * B], s[2 * B:3 * B] + bias_next[j], s[3 * B:]], axis=0)
        m = jnp.maximum(jnp.max(s, axis=0, keepdims=True), sk)
        e = jnp.exp2(s - m)
        den = jnp.sum(e, axis=0, keepdims=True) + jnp.exp2(sk - m)
        return e.astype(BF16), den

    def values(j, g, e, den):
        gs = slice(g * LANES, (g + 1) * LANES)
        vcat = keys_of(j, gs, vp_ref, vc_ref, vn_ref, cv_ref)
        res = lax.dot_general(vcat, e, (((0,), (0,)), ((), ())), preferred_element_type=F32) * (1.0 / den)
        for t in range(2):
            even = res[:, (2 * t) * B:(2 * t + 1) * B].T
            odd = res[:, (2 * t + 1) * B:(2 * t + 2) * B].T
            c0 = (2 * g + t) * LANES
            o_ref[j * B:(j + 1) * B, c0:c0 + LANES] = jnp.where(lo, even, odd).astype(BF16)

    units = [(j, g) for j in range(SB) for g in range(ATT_KV_HEADS)]
    s_next = scores(*units[0])
    pending = None
    for u, unit in enumerate(units):
        s_cur = s_next
        if u + 1 < len(units):
            s_next = scores(*units[u + 1])
        e_den = softmax(*unit, s_cur)
        if pending is not None:
            values(*units[u - 1], *pending)
        pending = e_den
    values(*units[-1], *pending)


def _attn(sink, proj, kd, vd, ckd, cvd, riders):
    L = proj.shape[0]
    B = ATT_BLOCK
    SB = ATT_STEP_BLOCKS
    n = L // (SB * B)
    nb = L // B
    lc = ckd.shape[0]
    prev = pl.BlockSpec((B, 512), lambda i: (jnp.maximum(i * SB - 1, 0), 0))
    cur = pl.BlockSpec((SB * B, 512), lambda i: (i, 0))
    nxt = pl.BlockSpec((B, 512), lambda i: (jnp.minimum((i + 1) * SB, nb - 1), 0))
    full = pl.BlockSpec((lc, 512), lambda i: (0, 0))
    rid_in_specs, rid_out_specs, rid_shapes = _rider_specs(riders, n)
    return pl.pallas_call(
        _with_cast_riders(_attn_kernel, 10, 1, len(riders)),
        grid=(n,),
        in_specs=[pl.BlockSpec(memory_space=pltpu.SMEM),
                  pl.BlockSpec((SB * B, 1024), lambda i: (i, 3)),
                  prev, cur, nxt, prev, cur, nxt, full, full] + rid_in_specs,
        out_specs=[pl.BlockSpec((SB * B, ATT_HEADS * ATT_DH), lambda i: (i, 0))] + rid_out_specs,
        out_shape=[jax.ShapeDtypeStruct((L, ATT_HEADS * ATT_DH), BF16)] + rid_shapes,
        compiler_params=_params("parallel"),
        name="attn",
    )(sink, proj, kd, kd, kd, vd, vd, vd, ckd, cvd, *[r[0] for r in riders])


def _out_proj_kernel(yr_ref, ya_ref, w_ref, x_ref, gt_ref, g_ref, sh_ref, sc_ref, o_ref, h_ref):
    kr = yr_ref.shape[1]
    for r in range(yr_ref.shape[0] // OUT_ROW_CHUNK):
        rs = slice(r * OUT_ROW_CHUNK, (r + 1) * OUT_ROW_CHUNK)
        acc = jnp.dot(yr_ref[rs, :], w_ref[:kr, :], preferred_element_type=F32)
        acc = acc + jnp.dot(ya_ref[rs, :], w_ref[kr:, :], preferred_element_type=F32)
        x1 = x_ref[rs, :] + gt_ref[...] * acc
        o_ref[rs, :] = x1
        y = x1 * lax.rsqrt(jnp.mean(x1 * x1, axis=-1, keepdims=True) + NORM_EPS)
        y = y * g_ref[...]
        h_ref[rs, :] = (y * (1.0 + sc_ref[...]) + sh_ref[...]).astype(BF16)


def _out_proj(yr, ya, w, x, gt, g, sh, sc, *, tm):
    m, d = x.shape
    kr, ka = yr.shape[1], ya.shape[1]
    row = lambda i: (i, 0)
    vec = pl.BlockSpec((1, d), lambda i: (0, 0))
    return pl.pallas_call(
        _out_proj_kernel,
        grid=(m // tm,),
        in_specs=[pl.BlockSpec((tm, kr), row), pl.BlockSpec((tm, ka), row),
                  pl.BlockSpec((kr + ka, d), lambda i: (0, 0)),
                  pl.BlockSpec((tm, d), row), vec, vec, vec, vec],
        out_specs=[pl.BlockSpec((tm, d), row), pl.BlockSpec((tm, d), row)],
        out_shape=[jax.ShapeDtypeStruct((m, d), F32), jax.ShapeDtypeStruct((m, d), BF16)],
        compiler_params=_params("parallel"),
        name="out_proj",
    )(yr, ya, w, x, gt, g, sh, sc)


def _ffn_kernel(h_ref, gt_ref, gfin_ref, wg_ref, wu_ref, wd_ref, x_hbm, o_ref, x_buf, sem):
    i = pl.program_id(0)
    f = pl.program_id(1)
    last = pl.num_programs(1) - 1
    rows = o_ref.shape[0]
    x_copy = pltpu.make_async_copy(x_hbm.at[pl.ds(pl.multiple_of(i * rows, rows), rows), :], x_buf, sem.at[0])

    def step(first, final):
        for r in range(rows // FFN_ROW_CHUNK):
            rs = slice(r * FFN_ROW_CHUNK, (r + 1) * FFN_ROW_CHUNK)
            h = h_ref[rs, :]
            a = jnp.dot(h, wg_ref[0], preferred_element_type=F32)
            u = jnp.dot(h, wu_ref[0], preferred_element_type=F32)
            act = ((a / (1.0 + jnp.exp(-a))) * u).astype(BF16)
            part = jnp.dot(act, wd_ref[...], preferred_element_type=F32)
            if first:
                o_ref[rs, :] = part
            elif not final:
                o_ref[rs, :] += part
            else:
                y = x_buf[rs, :] + gt_ref[...] * (o_ref[rs, :] + part)
                y = y * lax.rsqrt(jnp.mean(y * y, axis=-1, keepdims=True) + NORM_EPS)
                o_ref[rs, :] = y * gfin_ref[...]

    @pl.when(f == 0)
    def _():
        x_copy.start()
        step(first=True, final=False)

    @pl.when((f > 0) & (f < last))
    def _():
        step(first=False, final=False)

    @pl.when(f == last)
    def _():
        x_copy.wait()
        step(first=False, final=True)


def _ffn(h, x, gt, gfin, wg, wu, wd, *, tm):
    m, d = x.shape
    nf = wg.shape[0]
    assert wg.shape == wu.shape == (nf, d, FFN_TILE) and wd.shape == (nf * FFN_TILE, d)
    assert m % tm == 0 and tm % FFN_ROW_CHUNK == 0
    row = lambda i, f: (i, 0)
    vec = pl.BlockSpec((1, d), lambda i, f: (0, 0))
    wcol = pl.BlockSpec((1, d, FFN_TILE), lambda i, f: (f, 0, 0))
    return pl.pallas_call(
        _ffn_kernel,
        grid=(m // tm, nf),
        in_specs=[pl.BlockSpec((tm, d), row), vec, vec, wcol, wcol,
                  pl.BlockSpec((FFN_TILE, d), lambda i, f: (f, 0)),
                  pl.BlockSpec(memory_space=pl.ANY)],
        out_specs=pl.BlockSpec((tm, d), row),
        out_shape=jax.ShapeDtypeStruct((m, d), F32),
        scratch_shapes=[pltpu.VMEM((tm, d), F32), pltpu.SemaphoreType.DMA((1,))],
        compiler_params=_params("arbitrary", "arbitrary"),
        name="ffn",
    )(h, gt, gfin, wg, wu, wd, x)


def _rope_tables(L):
    f32 = np.float32
    lane = np.arange(LANES)
    inv1 = f32(ROPE_BASE) ** (-np.arange(32, dtype=f32) / f32(32))
    ang1 = np.arange(L, dtype=f32)[:, None] * inv1[None, :]
    sgn1 = np.where((lane % 64) < 32, -1.0, 1.0).astype(f32)
    cos1 = np.tile(np.cos(ang1), (1, LANES // 32))
    sin1 = np.tile(np.sin(ang1), (1, LANES // 32)) * sgn1[None, :]
    inv2 = f32(ROPE_BASE) ** (-np.arange(16, dtype=f32) / f32(16))
    nrow = L // GRID_W
    ang_r = np.arange(nrow, dtype=f32)[:, None] * inv2[None, :]
    ang_c = np.arange(GRID_W, dtype=f32)[:, None] * inv2[None, :]
    sgna = np.where((lane % 32) < 16, -1.0, 1.0).astype(f32)

    def expand(fr, fc):
        by_row = np.broadcast_to(np.tile(fr, (1, 2))[:, None, :], (nrow, GRID_W, 32))
        by_col = np.broadcast_to(np.tile(fc, (1, 2))[None, :, :], (nrow, GRID_W, 32))
        head = np.concatenate([by_row, by_col], axis=-1).reshape(L, 64)
        return np.tile(head, (1, LANES // 64))

    cosa = expand(np.cos(ang_r), np.cos(ang_c))
    sina = expand(np.sin(ang_r), np.sin(ang_c)) * sgna[None, :]
    return tuple(np.ascontiguousarray(t, dtype=f32) for t in (cos1, sin1, cosa, sina))


def kernel(x, c, ctx, c_ctx, w_mod, b_mod, norm_mix, norm_ffn, w_in, ret_decay, attn_sink,
           w_out, w_gate, w_up, w_down, norm_final):
    B, L, D = x.shape
    assert B == 1 and w_mod.shape[0] == 1, "single batch element, depth-1 layer"
    x2 = x[0]
    xc2 = ctx[0]

    cv = jnp.zeros((8, D), F32).at[0].set(c[0]).at[1].set(c_ctx)
    mod = _mod(cv, w_mod[0], b_mod[0][None, :])
    sh_m, sc_m, gt_m, sh_f, sc_f, gt_f = [mod[0:1, k * D:(k + 1) * D] for k in range(6)]
    sh_mc, sc_mc = mod[1:2, 0:D], mod[1:2, D:2 * D]

    g_mix = norm_mix[0][None, :]
    cproj, ckd, cvd, w_in_b = _ctx_proj(xc2, g_mix, sh_mc, sc_mc, w_in[0])
    proj, kd, vd = _in_proj(x2, g_mix, sh_m, sc_m, w_in_b, _rope_tables(L), tm=512)

    dec = ret_decay[0].astype(F32)
    sf, sb = _ret_states(dec, proj, cproj)
    y_ret = _ret_out(dec, proj, sf, sb)
    y_att, w_gate_b, w_up_b, w_down_b, w_out_b = _attn(
        attn_sink[0].astype(F32), proj, kd, vd, ckd, cvd,
        [(w_gate[0], 1, FFN_TILE), (w_up[0], 1, FFN_TILE), (w_down[0], 2, None), (w_out[0], 1, None)])

    x1, hff = _out_proj(y_ret, y_att, w_out_b, x2, gt_m, norm_ffn[0][None, :], sh_f, sc_f, tm=512)
    out = _ffn(hff, x1, gt_f, norm_final[None, :], w_gate_b, w_up_b, w_down_b, tm=1024)
    return out[None]
```

```python
import jax
import jax.numpy as jnp
import numpy as np
from jax import lax
from jax.experimental import pallas as pl
from jax.experimental.pallas import tpu as pltpu

GRID_W = 64
RET_HEADS = 8
RET_DK = 64
RET_DV = 128
RET_CHUNK = 128
ATT_HEADS = 16
ATT_KV_HEADS = 4
ATT_DH = 64
ATT_GROUP = ATT_HEADS // ATT_KV_HEADS
WINDOW = 128
ATT_BLOCK = 128
ROPE_BASE = 10000.0
NORM_EPS = 1e-6
K_SCALE = RET_DK ** -0.5
ATT_SCALE = ATT_DH ** -0.5
LOG2E = 1.4426950408889634

LANES = 128
RET_PAIRS = RET_HEADS // 2
MASK_NEG = -1e30
VMEM_LIMIT = 56 * 1024 * 1024
RET_STEP_CHUNKS = 4
ATT_STEP_BLOCKS = 2
OUT_ROW_CHUNK = 256
IN_ROW_CHUNK = 256
FFN_TILE = 512
FFN_ROW_CHUNK = 256

BF16 = jnp.bfloat16
F32 = jnp.float32


def _params(*sem):
    return pltpu.CompilerParams(dimension_semantics=sem, vmem_limit_bytes=VMEM_LIMIT)


def _with_cast_riders(body, n_in, n_out, n_rid):
    def wrapped(*refs):
        ins = refs[:n_in]
        rid_in = refs[n_in:n_in + n_rid]
        outs = refs[n_in + n_rid:n_in + n_rid + n_out]
        rid_out = refs[n_in + n_rid + n_out:n_in + 2 * n_rid + n_out]
        scratch = refs[n_in + 2 * n_rid + n_out:]
        for src, dst in zip(rid_in, rid_out):
            if len(dst.shape) == 2:
                dst[...] = src[...].astype(BF16)
            else:
                tc = dst.shape[2]
                for t in range(dst.shape[0]):
                    dst[t] = src[:, t * tc:(t + 1) * tc].astype(BF16)
        body(*ins, *outs, *scratch)
    return wrapped


def _rider_specs(riders, steps):
    in_specs, out_specs, shapes = [], [], []
    for w, ncb, tile in riders:
        rows, cols = w.shape
        nrb = steps // ncb
        assert nrb * ncb == steps and rows % nrb == 0 and cols % ncb == 0
        br, bc = rows // nrb, cols // ncb
        assert br % 16 == 0 and bc % LANES == 0, "slab must be bf16-tile aligned"
        in_specs.append(pl.BlockSpec((br, bc), lambda i, ncb=ncb: (i // ncb, i % ncb)))
        if tile is None:
            out_specs.append(in_specs[-1])
            shapes.append(jax.ShapeDtypeStruct(w.shape, BF16))
        else:
            assert ncb == 1 and cols % tile == 0 and tile % LANES == 0
            out_specs.append(pl.BlockSpec((cols // tile, br, tile), lambda i: (0, i, 0)))
            shapes.append(jax.ShapeDtypeStruct((cols // tile, rows, tile), BF16))
    return in_specs, out_specs, shapes


def _mod_kernel(cv_ref, w_ref, b_ref, o_ref):
    cv = cv_ref[...]
    s = cv / (1.0 + jnp.exp(-cv))
    o_ref[...] = jnp.dot(s.astype(BF16), w_ref[...].astype(BF16),
                         preferred_element_type=F32) + b_ref[...]


def _mod(cv, w, b, n):
    d = w.shape[0]
    tn = 1024
    assert n % tn == 0
    return pl.pallas_call(
        _mod_kernel,
        grid=(n // tn,),
        in_specs=[pl.BlockSpec((8, d), lambda j: (0, 0)),
                  pl.BlockSpec((d, tn), lambda j: (0, j)),
                  pl.BlockSpec((1, tn), lambda j: (0, j))],
        out_specs=pl.BlockSpec((8, tn), lambda j: (0, j)),
        out_shape=jax.ShapeDtypeStruct((8, n), F32),
        compiler_params=_params("parallel"),
        name="mod",
    )(cv, w, b)


def _rot_pairs(a, cos, sin_signed, half):
    lane = lax.broadcasted_iota(jnp.int32, a.shape, 1)
    first = (lane % (2 * half)) < half
    rot = jnp.where(first, pltpu.roll(a, LANES - half, 1), pltpu.roll(a, half, 1))
    return a * cos + rot * sin_signed


def _dup_halves(a):
    lane = lax.broadcasted_iota(jnp.int32, a.shape, 1)
    r = pltpu.roll(a, 64, 1)
    lo = lane < 64
    return jnp.where(lo, a, r), jnp.where(lo, r, a)


_PROJ_TILE = 512
_PROJ_TILE_KINDS = ("ret_q", "ret_k", "plain", "plain", "plain", "plain", "att_q", "att_q", "att_kv")


def _in_proj_kernel(x_ref, g_ref, sh_ref, sc_ref, w_ref, c1_ref, s1_ref, ca_ref, sa_ref,
                    cc_ref, wm_ref, bm_ref, o_ref, kd_ref, vd_ref, mod_ref):
    cc = cc_ref[...]
    s_col = cc / (1.0 + jnp.exp(-cc))
    mod_ref[0] = jnp.sum(wm_ref[...] * s_col, axis=0, keepdims=True) + bm_ref[...]

    tn = _PROJ_TILE
    for r in range(x_ref.shape[0] // IN_ROW_CHUNK):
        rs = slice(r * IN_ROW_CHUNK, (r + 1) * IN_ROW_CHUNK)
        xf = x_ref[rs, :]
        y = xf * lax.rsqrt(jnp.mean(xf * xf, axis=-1, keepdims=True) + NORM_EPS)
        y = y * g_ref[...]
        h = (y * (1.0 + sc_ref[...]) + sh_ref[...]).astype(BF16)

        def rope1(a):
            return _rot_pairs(a, c1_ref[rs, :], s1_ref[rs, :], 32)

        def ropea(a):
            return _rot_pairs(a, ca_ref[rs, :], sa_ref[rs, :], 16)

        for j, kind in enumerate(_PROJ_TILE_KINDS):
            acc = jnp.dot(h, w_ref[:, j * tn:(j + 1) * tn], preferred_element_type=F32)
            for c in range(tn // LANES):
                a = acc[:, c * LANES:(c + 1) * LANES]
                if kind == "ret_q":
                    a = rope1(a)
                elif kind == "ret_k":
                    a = rope1(a) * K_SCALE
                elif kind == "att_q":
                    a = ropea(a) * (ATT_SCALE * LOG2E)
                elif kind == "att_kv" and c < 2:
                    a = ropea(a)
                o_ref[rs, j * tn + c * LANES:j * tn + (c + 1) * LANES] = a.astype(BF16)
                if kind == "att_kv":
                    dup_ref = kd_ref if c < 2 else vd_ref
                    d0, d1 = _dup_halves(a)
                    t = 2 * (c % 2)
                    dup_ref[rs, t * LANES:(t + 1) * LANES] = d0.astype(BF16)
                    dup_ref[rs, (t + 1) * LANES:(t + 2) * LANES] = d1.astype(BF16)


def _in_proj(x, g, sh, sc, w, tabs, c_col, w_mod, b_mod, mod_done, *, tm):
    m, d = x.shape
    n = w.shape[1]
    assert n == _PROJ_TILE * len(_PROJ_TILE_KINDS) and m % tm == 0 and tm % IN_ROW_CHUNK == 0
    steps = m // tm
    slab = (w_mod.shape[1] - mod_done) // steps
    assert slab * steps == w_mod.shape[1] - mod_done and slab % LANES == 0 and mod_done % slab == 0
    slab0 = mod_done // slab
    c1, s1, ca, sa = tabs
    row = lambda i: (i, 0)
    vec = pl.BlockSpec((1, d), lambda i: (0, 0))
    tab = pl.BlockSpec((tm, LANES), row)
    return pl.pallas_call(
        _in_proj_kernel,
        grid=(m // tm,),
        in_specs=[pl.BlockSpec((tm, d), row), vec, vec, vec,
                  pl.BlockSpec((d, n), lambda i: (0, 0), pipeline_mode=pl.Buffered(1)),
                  tab, tab, tab, tab,
                  pl.BlockSpec((d, 1), lambda i: (0, 0)),
                  pl.BlockSpec((d, slab), lambda i: (0, slab0 + i)),
                  pl.BlockSpec((1, slab), lambda i: (0, slab0 + i))],
        out_specs=[pl.BlockSpec((tm, n), row),
                   pl.BlockSpec((tm, 512), row),
                   pl.BlockSpec((tm, 512), row),
                   pl.BlockSpec((1, 1, slab), lambda i: (i, 0, 0))],
        out_shape=[jax.ShapeDtypeStruct((m, n), BF16),
                   jax.ShapeDtypeStruct((m, 512), BF16),
                   jax.ShapeDtypeStruct((m, 512), BF16),
                   jax.ShapeDtypeStruct((steps, 1, slab), F32)],
        compiler_params=_params("parallel"),
        name="in_proj",
    )(x, g, sh, sc, w, c1, s1, ca, sa, c_col, w_mod, b_mod)


def _ctx_proj_kernel(x_ref, g_ref, sh_ref, sc_ref, w_ref, o_ref, kd_ref, vd_ref, wb_ref, h_ref):
    j = pl.program_id(0)

    @pl.when(j == 0)
    def _():
        xf = x_ref[...]
        y = xf * lax.rsqrt(jnp.mean(xf * xf, axis=-1, keepdims=True) + NORM_EPS)
        y = y * g_ref[...]
        h_ref[...] = (y * (1.0 + sc_ref[...]) + sh_ref[...]).astype(BF16)

    wb = w_ref[...].astype(BF16)
    wb_ref[...] = wb
    acc = jnp.dot(h_ref[...], wb, preferred_element_type=F32)
    is_ret_k = _PROJ_TILE_KINDS.index("ret_k")
    o_ref[...] = (acc * jnp.where(j == is_ret_k, K_SCALE, 1.0)).astype(BF16)

    @pl.when(j == _PROJ_TILE_KINDS.index("att_kv"))
    def _():
        for c in range(_PROJ_TILE // LANES):
            dup_ref = kd_ref if c < 2 else vd_ref
            d0, d1 = _dup_halves(acc[:, c * LANES:(c + 1) * LANES])
            t = 2 * (c % 2)
            dup_ref[:, t * LANES:(t + 1) * LANES] = d0.astype(BF16)
            dup_ref[:, (t + 1) * LANES:(t + 2) * LANES] = d1.astype(BF16)


def _ctx_proj(x, g, sh, sc, w):
    m, d = x.shape
    n = w.shape[1]
    tn = _PROJ_TILE
    assert n == tn * len(_PROJ_TILE_KINDS)
    fixed = lambda j: (0, 0)
    vec = pl.BlockSpec((1, d), fixed)
    return pl.pallas_call(
        _ctx_proj_kernel,
        grid=(n // tn,),
        in_specs=[pl.BlockSpec((m, d), fixed), vec, vec, vec,
                  pl.BlockSpec((d, tn), lambda j: (0, j))],
        out_specs=[pl.BlockSpec((m, tn), lambda j: (0, j)),
                   pl.BlockSpec((m, 512), fixed),
                   pl.BlockSpec((m, 512), fixed),
                   pl.BlockSpec((d, tn), lambda j: (0, j))],
        out_shape=[jax.ShapeDtypeStruct((m, n), BF16),
                   jax.ShapeDtypeStruct((m, 512), BF16),
                   jax.ShapeDtypeStruct((m, 512), BF16),
                   jax.ShapeDtypeStruct((d, n), BF16)],
        scratch_shapes=[pltpu.VMEM((m, d), BF16)],
        compiler_params=_params("arbitrary"),
        name="ctx_proj",
    )(x, g, sh, sc, w)


def _pair_lg(dec_ref, d, p, shape):
    lane = lax.broadcasted_iota(jnp.int32, shape, 1)
    first = (lane % LANES) < 64
    raw = jnp.where(first, jnp.full(shape, dec_ref[d, 2 * p], F32), jnp.full(shape, dec_ref[d, 2 * p + 1], F32))
    return -jnp.exp(raw)


def _head_block_mask(shape):
    r = lax.broadcasted_iota(jnp.int32, shape, 0)
    c = lax.broadcasted_iota(jnp.int32, shape, 1)
    return (r // 64) == (c // LANES)


def _kv_pair(k_pair, v_pair, w):
    kw = (k_pair.astype(F32) * w).astype(BF16)
    kv = lax.dot_general(kw, v_pair, (((0,), (0,)), ((), ())), preferred_element_type=F32)
    return jnp.where(_head_block_mask(kv.shape), kv, 0.0)


def _ret_state_kernel(dec_ref, kf_ref, vf_ref, kb_ref, vb_ref, ck_ref, cv_ref,
                      sf_ref, sb_ref, sfs, sbs):
    i = pl.program_id(0)
    C = RET_CHUNK
    lc = ck_ref.shape[0]

    @pl.when(i == 0)
    def _():
        pos = lax.broadcasted_iota(jnp.int32, (lc, LANES), 0).astype(F32)
        for p in range(RET_PAIRS):
            ks = slice(p * LANES, (p + 1) * LANES)
            vs = slice(p * 2 * RET_DV, (p + 1) * 2 * RET_DV)
            wf = jnp.exp(_pair_lg(dec_ref, 0, p, (lc, LANES)) * (lc - 1.0 - pos))
            wb = jnp.exp(_pair_lg(dec_ref, 1, p, (lc, LANES)) * pos)
            sfs[p] = _kv_pair(ck_ref[:, ks], cv_ref[:, vs], wf)
            sbs[p] = _kv_pair(ck_ref[:, ks], cv_ref[:, vs], wb)

    pos = lax.broadcasted_iota(jnp.int32, (C, LANES), 0).astype(F32)
    for p in range(RET_PAIRS):
        ks = slice(p * LANES, (p + 1) * LANES)
        vs = slice(p * 2 * RET_DV, (p + 1) * 2 * RET_DV)
        wf = jnp.exp(_pair_lg(dec_ref, 0, p, (C, LANES)) * (C - 1.0 - pos))
        wb = jnp.exp(_pair_lg(dec_ref, 1, p, (C, LANES)) * pos)
        rowh = lax.broadcasted_iota(jnp.int32, (LANES, 2 * RET_DV), 0) < 64
        gf = jnp.exp(-jnp.exp(jnp.where(rowh, jnp.full(rowh.shape, dec_ref[0, 2 * p], F32),
                                         jnp.full(rowh.shape, dec_ref[0, 2 * p + 1], F32))) * float(C))
        gb = jnp.exp(-jnp.exp(jnp.where(rowh, jnp.full(rowh.shape, dec_ref[1, 2 * p], F32),
                                         jnp.full(rowh.shape, dec_ref[1, 2 * p + 1], F32))) * float(C))
        sf = sfs[p]
        for cc in range(RET_STEP_CHUNKS):
            rs = slice(cc * C, (cc + 1) * C)
            sf_ref[cc, p] = sf.astype(BF16)
            sf = gf * sf + _kv_pair(kf_ref[rs, ks], vf_ref[rs, vs], wf)
        sfs[p] = sf
        sb = sbs[p]
        for cc in reversed(range(RET_STEP_CHUNKS)):
            rs = slice(cc * C, (cc + 1) * C)
            sb_ref[cc, p] = sb.astype(BF16)
            sb = gb * sb + _kv_pair(kb_ref[rs, ks], vb_ref[rs, vs], wb)
        sbs[p] = sb


def _ret_states(dec, proj, cproj):
    L = proj.shape[0]
    lc = cproj.shape[0]
    S = RET_STEP_CHUNKS
    R = S * RET_CHUNK
    n = L // R
    st = pl.BlockSpec((S, RET_PAIRS, LANES, 2 * RET_DV), lambda i: (i, 0, 0, 0))
    st_rev = pl.BlockSpec((S, RET_PAIRS, LANES, 2 * RET_DV), lambda i: (n - 1 - i, 0, 0, 0))
    shp = jax.ShapeDtypeStruct((n * S, RET_PAIRS, LANES, 2 * RET_DV), BF16)
    return pl.pallas_call(
        _ret_state_kernel,
        grid=(n,),
        in_specs=[pl.BlockSpec(memory_space=pltpu.SMEM),
                  pl.BlockSpec((R, 512), lambda i: (i, 1)),
                  pl.BlockSpec((R, 1024), lambda i: (i, 1)),
                  pl.BlockSpec((R, 512), lambda i: (n - 1 - i, 1)),
                  pl.BlockSpec((R, 1024), lambda i: (n - 1 - i, 1)),
                  pl.BlockSpec((lc, 512), lambda i: (0, 1)),
                  pl.BlockSpec((lc, 1024), lambda i: (0, 1))],
        out_specs=[st, st_rev],
        out_shape=[shp, shp],
        scratch_shapes=[pltpu.VMEM((RET_PAIRS, LANES, 2 * RET_DV), F32),
                        pltpu.VMEM((RET_PAIRS, LANES, 2 * RET_DV), F32)],
        compiler_params=_params("arbitrary"),
        name="ret_state",
    )(dec, proj, proj, proj, proj, cproj, cproj)


def _ret_out_kernel(dec_ref, q_ref, k_ref, v_ref, g_ref, sf_ref, sb_ref, o_ref):
    C = RET_CHUNK
    pos = lax.broadcasted_iota(jnp.int32, (C, LANES), 0).astype(F32)
    n_i = lax.broadcasted_iota(jnp.int32, (C, 2 * C), 0)
    m_i = lax.broadcasted_iota(jnp.int32, (C, 2 * C), 1) % C
    rel = (n_i - m_i).astype(F32)
    lane = lax.broadcasted_iota(jnp.int32, (C, LANES), 1)
    lo = lane < 64
    for p in range(RET_PAIRS):
        ks = slice(p * LANES, (p + 1) * LANES)
        vs = slice(p * 2 * RET_DV, (p + 1) * 2 * RET_DV)
        col_a = lax.broadcasted_iota(jnp.int32, (C, 2 * C), 1) < C
        raw_f = jnp.where(col_a, jnp.full((C, 2 * C), dec_ref[0, 2 * p], F32), jnp.full((C, 2 * C), dec_ref[0, 2 * p + 1], F32))
        raw_b = jnp.where(col_a, jnp.full((C, 2 * C), dec_ref[1, 2 * p], F32), jnp.full((C, 2 * C), dec_ref[1, 2 * p + 1], F32))
        dmat = jnp.where(rel >= 0, jnp.exp(-jnp.exp(raw_f) * jnp.maximum(rel, 0.0)),
                         jnp.exp(-jnp.exp(raw_b) * jnp.maximum(-rel, 0.0)))
        wqf = jnp.exp(_pair_lg(dec_ref, 0, p, (C, LANES)) * (pos + 1.0))
        wqb = jnp.exp(_pair_lg(dec_ref, 1, p, (C, LANES)) * (float(C) - pos))
        for cc in range(RET_STEP_CHUNKS):
            rs = slice(cc * C, (cc + 1) * C)
            q = q_ref[rs, ks]
            k = k_ref[rs, ks]
            v = v_ref[rs, vs]
            zk = jnp.zeros_like(k)
            kst = jnp.concatenate([jnp.where(lo, k, zk), jnp.where(lo, zk, k)], axis=0)
            s = lax.dot_general(q, kst, (((1,), (1,)), ((), ())), preferred_element_type=F32)
            sd = (s * dmat).astype(BF16)
            qf32 = q.astype(F32)
            qwf = (qf32 * wqf).astype(BF16)
            qwb = (qf32 * wqb).astype(BF16)
            zv = jnp.zeros((C, RET_DV), BF16)
            vbd = jnp.concatenate([jnp.concatenate([v[:, :RET_DV], zv], axis=1),
                                   jnp.concatenate([zv, v[:, RET_DV:]], axis=1)], axis=0)
            lhs = jnp.concatenate([sd, qwf, qwb], axis=1)
            rhs = jnp.concatenate([vbd, sf_ref[cc, p], sb_ref[cc, p]], axis=0)
            o = jnp.dot(lhs, rhs, preferred_element_type=F32)
            for t in range(2):
                oh = o[:, t * RET_DV:(t + 1) * RET_DV]
                oh = oh * lax.rsqrt(jnp.mean(oh * oh, axis=-1, keepdims=True) + NORM_EPS)
                cs = slice(p * 2 * RET_DV + t * RET_DV, p * 2 * RET_DV + (t + 1) * RET_DV)
                gt = g_ref[rs, cs].astype(F32)
                o_ref[rs, cs] = (oh * (gt / (1.0 + jnp.exp(-gt)))).astype(BF16)


def _ret_out(dec, proj, sf, sb):
    L = proj.shape[0]
    S = RET_STEP_CHUNKS
    R = S * RET_CHUNK
    n = L // R
    st = pl.BlockSpec((S, RET_PAIRS, LANES, 2 * RET_DV), lambda i: (i, 0, 0, 0))
    return pl.pallas_call(
        _ret_out_kernel,
        grid=(n,),
        in_specs=[pl.BlockSpec(memory_space=pltpu.SMEM),
                  pl.BlockSpec((R, 512), lambda i: (i, 0)),
                  pl.BlockSpec((R, 512), lambda i: (i, 1)),
                  pl.BlockSpec((R, 1024), lambda i: (i, 1)),
                  pl.BlockSpec((R, 1024), lambda i: (i, 2)),
                  st, st],
        out_specs=pl.BlockSpec((R, RET_HEADS * RET_DV), lambda i: (i, 0)),
        out_shape=jax.ShapeDtypeStruct((L, RET_HEADS * RET_DV), BF16),
        compiler_params=_params("parallel"),
        name="ret_out",
    )(dec, proj, proj, proj, proj, sf, sb)


def _attn_kernel(sink_ref, q_ref, kp_ref, kc_ref, kn_ref, vp_ref, vc_ref, vn_ref, ck_ref, cv_ref, o_ref):
    n = pl.program_id(0)
    nstep = pl.num_programs(0)
    B = ATT_BLOCK
    SB = ATT_STEP_BLOCKS
    kj = lax.broadcasted_iota(jnp.int32, (B, B), 0)
    qi = lax.broadcasted_iota(jnp.int32, (B, B), 1)
    ok_prev = jnp.where(n > 0, 0.0, MASK_NEG).astype(F32)
    ok_next = jnp.where(n < nstep - 1, 0.0, MASK_NEG).astype(F32)

    def band(inside, ok):
        return jnp.concatenate([jnp.where(inside, ok, MASK_NEG).astype(F32)] * ATT_GROUP, axis=1)

    bias_prev = [band(kj >= qi, ok_prev if j == 0 else 0.0) for j in range(SB)]
    bias_next = [band(kj <= qi, ok_next if j == SB - 1 else 0.0) for j in range(SB)]
    lane = lax.broadcasted_iota(jnp.int32, (B, LANES), 1)
    lo = lane < 64
    hi = lane >= 64

    def keys_of(j, gs, prev_ref, cur_ref, next_ref, ctx_ref):
        prev = prev_ref[:, gs] if j == 0 else cur_ref[(j - 1) * B:j * B, gs]
        nxt = next_ref[:, gs] if j == SB - 1 else cur_ref[(j + 1) * B:(j + 2) * B, gs]
        return jnp.concatenate([prev, cur_ref[j * B:(j + 1) * B, gs], nxt, ctx_ref[:, gs]], axis=0)

    def scores(j, g):
        gs = slice(g * LANES, (g + 1) * LANES)
        kcat = keys_of(j, gs, kp_ref, kc_ref, kn_ref, ck_ref)
        qs = []
        for r in range(ATT_GROUP):
            h = ATT_GROUP * g + r
            qt = q_ref[j * B:(j + 1) * B, (h // 2) * LANES:(h // 2 + 1) * LANES]
            keep = lo if h % 2 == 0 else hi
            qs.append(jnp.where(keep, qt, jnp.zeros_like(qt)))
        q4 = jnp.concatenate(qs, axis=0)
        return lax.dot_general(kcat, q4, (((1,), (1,)), ((), ())), preferred_element_type=F32)

    def softmax(j, g, s):
        sk = jnp.concatenate([jnp.full((1, B), sink_ref[ATT_GROUP * g + r], F32)
                              for r in range(ATT_GROUP)], axis=1) * LOG2E
        s = jnp.concatenate([s[:B] + bias_prev[j], s[B:2 * B], s[2 * B:3 * B] + bias_next[j], s[3 * B:]], axis=0)
        m = jnp.maximum(jnp.max(s, axis=0, keepdims=True), sk)
        e = jnp.exp2(s - m)
        den = jnp.sum(e, axis=0, keepdims=True) + jnp.exp2(sk - m)
        return e.astype(BF16), den

    def values(j, g, e, den):
        gs = slice(g * LANES, (g + 1) * LANES)
        vcat = keys_of(j, gs, vp_ref, vc_ref, vn_ref, cv_ref)
        res = lax.dot_general(vcat, e, (((0,), (0,)), ((), ())), preferred_element_type=F32) * (1.0 / den)
        for t in range(2):
            even = res[:, (2 * t) * B:(2 * t + 1) * B].T
            odd = res[:, (2 * t + 1) * B:(2 * t + 2) * B].T
            c0 = (2 * g + t) * LANES
            o_ref[j * B:(j + 1) * B, c0:c0 + LANES] = jnp.where(lo, even, odd).astype(BF16)

    units = [(j, g) for j in range(SB) for g in range(ATT_KV_HEADS)]
    s_next = scores(*units[0])
    pending = None
    for u, unit in enumerate(units):
        s_cur = s_next
        if u + 1 < len(units):
            s_next = scores(*units[u + 1])
        e_den = softmax(*unit, s_cur)
        if pending is not None:
            values(*units[u - 1], *pending)
        pending = e_den
    values(*units[-1], *pending)


def _attn(sink, proj, kd, vd, ckd, cvd, riders):
    L = proj.shape[0]
    B = ATT_BLOCK
    SB = ATT_STEP_BLOCKS
    n = L // (SB * B)
    nb = L // B
    lc = ckd.shape[0]
    prev = pl.BlockSpec((B, 512), lambda i: (jnp.maximum(i * SB - 1, 0), 0))
    cur = pl.BlockSpec((SB * B, 512), lambda i: (i, 0))
    nxt = pl.BlockSpec((B, 512), lambda i: (jnp.minimum((i + 1) * SB, nb - 1), 0))
    full = pl.BlockSpec((lc, 512), lambda i: (0, 0))
    rid_in_specs, rid_out_specs, rid_shapes = _rider_specs(riders, n)
    return pl.pallas_call(
        _with_cast_riders(_attn_kernel, 10, 1, len(riders)),
        grid=(n,),
        in_specs=[pl.BlockSpec(memory_space=pltpu.SMEM),
                  pl.BlockSpec((SB * B, 1024), lambda i: (i, 3)),
                  prev, cur, nxt, prev, cur, nxt, full, full] + rid_in_specs,
        out_specs=[pl.BlockSpec((SB * B, ATT_HEADS * ATT_DH), lambda i: (i, 0))] + rid_out_specs,
        out_shape=[jax.ShapeDtypeStruct((L, ATT_HEADS * ATT_DH), BF16)] + rid_shapes,
        compiler_params=_params("parallel"),
        name="attn",
    )(sink, proj, kd, kd, kd, vd, vd, vd, ckd, cvd, *[r[0] for r in riders])


def _out_proj_kernel(yr_ref, ya_ref, w_ref, x_ref, gt_ref, g_ref, sh_ref, sc_ref, o_ref, h_ref):
    kr = yr_ref.shape[1]
    for r in range(yr_ref.shape[0] // OUT_ROW_CHUNK):
        rs = slice(r * OUT_ROW_CHUNK, (r + 1) * OUT_ROW_CHUNK)
        acc = jnp.dot(yr_ref[rs, :], w_ref[:kr, :], preferred_element_type=F32)
        acc = acc + jnp.dot(ya_ref[rs, :], w_ref[kr:, :], preferred_element_type=F32)
        x1 = x_ref[rs, :] + gt_ref[...] * acc
        o_ref[rs, :] = x1
        y = x1 * lax.rsqrt(jnp.mean(x1 * x1, axis=-1, keepdims=True) + NORM_EPS)
        y = y * g_ref[...]
        h_ref[rs, :] = (y * (1.0 + sc_ref[...]) + sh_ref[...]).astype(BF16)


def _out_proj(yr, ya, w, x, gt, g, sh, sc, *, tm):
    m, d = x.shape
    kr, ka = yr.shape[1], ya.shape[1]
    row = lambda i: (i, 0)
    vec = pl.BlockSpec((1, d), lambda i: (0, 0))
    return pl.pallas_call(
        _out_proj_kernel,
        grid=(m // tm,),
        in_specs=[pl.BlockSpec((tm, kr), row), pl.BlockSpec((tm, ka), row),
                  pl.BlockSpec((kr + ka, d), lambda i: (0, 0)),
                  pl.BlockSpec((tm, d), row), vec, vec, vec, vec],
        out_specs=[pl.BlockSpec((tm, d), row), pl.BlockSpec((tm, d), row)],
        out_shape=[jax.ShapeDtypeStruct((m, d), F32), jax.ShapeDtypeStruct((m, d), BF16)],
        compiler_params=_params("parallel"),
        name="out_proj",
    )(yr, ya, w, x, gt, g, sh, sc)


def _ffn_kernel(h_ref, gt_ref, gfin_ref, wg_ref, wu_ref, wd_ref, x_hbm, o_ref, x_buf, sem):
    i = pl.program_id(0)
    f = pl.program_id(1)
    last = pl.num_programs(1) - 1
    rows = o_ref.shape[0]
    x_copy = pltpu.make_async_copy(x_hbm.at[pl.ds(pl.multiple_of(i * rows, rows), rows), :], x_buf, sem.at[0])

    def step(first, final):
        wd = wd_ref[...].astype(BF16)
        for r in range(rows // FFN_ROW_CHUNK):
            rs = slice(r * FFN_ROW_CHUNK, (r + 1) * FFN_ROW_CHUNK)
            h = h_ref[rs, :]
            a = jnp.dot(h, wg_ref[0], preferred_element_type=F32)
            u = jnp.dot(h, wu_ref[0], preferred_element_type=F32)
            act = ((a / (1.0 + jnp.exp(-a))) * u).astype(BF16)
            part = jnp.dot(act, wd, preferred_element_type=F32)
            if first:
                o_ref[rs, :] = part
            elif not final:
                o_ref[rs, :] += part
            else:
                y = x_buf[rs, :] + gt_ref[...] * (o_ref[rs, :] + part)
                y = y * lax.rsqrt(jnp.mean(y * y, axis=-1, keepdims=True) + NORM_EPS)
                o_ref[rs, :] = y * gfin_ref[...]

    @pl.when(f == 0)
    def _():
        x_copy.start()
        step(first=True, final=False)

    @pl.when((f > 0) & (f < last))
    def _():
        step(first=False, final=False)

    @pl.when(f == last)
    def _():
        x_copy.wait()
        step(first=False, final=True)


def _ffn(h, x, gt, gfin, wg, wu, wd, *, tm):
    m, d = x.shape
    nf = wg.shape[0]
    assert wg.shape == wu.shape == (nf, d, FFN_TILE) and wd.shape == (nf * FFN_TILE, d)
    assert m % tm == 0 and tm % FFN_ROW_CHUNK == 0
    row = lambda i, f: (i, 0)
    vec = pl.BlockSpec((1, d), lambda i, f: (0, 0))
    wcol = pl.BlockSpec((1, d, FFN_TILE), lambda i, f: (f, 0, 0))
    return pl.pallas_call(
        _ffn_kernel,
        grid=(m // tm, nf),
        in_specs=[pl.BlockSpec((tm, d), row), vec, vec, wcol, wcol,
                  pl.BlockSpec((FFN_TILE, d), lambda i, f: (f, 0)),
                  pl.BlockSpec(memory_space=pl.ANY)],
        out_specs=pl.BlockSpec((tm, d), row),
        out_shape=jax.ShapeDtypeStruct((m, d), F32),
        scratch_shapes=[pltpu.VMEM((tm, d), F32), pltpu.SemaphoreType.DMA((1,))],
        compiler_params=_params("arbitrary", "arbitrary"),
        name="ffn",
    )(h, gt, gfin, wg, wu, wd, x)


def _rope_tables(L):
    f32 = np.float32
    lane = np.arange(LANES)
    inv1 = f32(ROPE_BASE) ** (-np.arange(32, dtype=f32) / f32(32))
    ang1 = np.arange(L, dtype=f32)[:, None] * inv1[None, :]
    sgn1 = np.where((lane % 64) < 32, -1.0, 1.0).astype(f32)
    cos1 = np.tile(np.cos(ang1), (1, LANES // 32))
    sin1 = np.tile(np.sin(ang1), (1, LANES // 32)) * sgn1[None, :]
    inv2 = f32(ROPE_BASE) ** (-np.arange(16, dtype=f32) / f32(16))
    nrow = L // GRID_W
    ang_r = np.arange(nrow, dtype=f32)[:, None] * inv2[None, :]
    ang_c = np.arange(GRID_W, dtype=f32)[:, None] * inv2[None, :]
    sgna = np.where((lane % 32) < 16, -1.0, 1.0).astype(f32)

    def expand(fr, fc):
        by_row = np.broadcast_to(np.tile(fr, (1, 2))[:, None, :], (nrow, GRID_W, 32))
        by_col = np.broadcast_to(np.tile(fc, (1, 2))[None, :, :], (nrow, GRID_W, 32))
        head = np.concatenate([by_row, by_col], axis=-1).reshape(L, 64)
        return np.tile(head, (1, LANES // 64))

    cosa = expand(np.cos(ang_r), np.cos(ang_c))
    sina = expand(np.sin(ang_r), np.sin(ang_c)) * sgna[None, :]
    return tuple(np.ascontiguousarray(t, dtype=f32) for t in (cos1, sin1, cosa, sina))


def kernel(x, c, ctx, c_ctx, w_mod, b_mod, norm_mix, norm_ffn, w_in, ret_decay, attn_sink,
           w_out, w_gate, w_up, w_down, norm_final):
    B, L, D = x.shape
    assert B == 1 and w_mod.shape[0] == 1, "single batch element, depth-1 layer"
    x2 = x[0]
    xc2 = ctx[0]

    cv = jnp.zeros((8, D), F32).at[0].set(c[0]).at[1].set(c_ctx)
    mod = _mod(cv, w_mod[0], b_mod[0][None, :], 2 * D)
    sh_m, sc_m = mod[0:1, 0:D], mod[0:1, D:2 * D]
    sh_mc, sc_mc = mod[1:2, 0:D], mod[1:2, D:2 * D]

    g_mix = norm_mix[0][None, :]
    cproj, ckd, cvd, w_in_b = _ctx_proj(xc2, g_mix, sh_mc, sc_mc, w_in[0])
    proj, kd, vd, mod_rest = _in_proj(x2, g_mix, sh_m, sc_m, w_in_b, _rope_tables(L),
                                      c[0][:, None], w_mod[0], b_mod[0][None, :], 2 * D, tm=512)
    gt_m, sh_f, sc_f, gt_f = [mod_rest.reshape(1, 4 * D)[:, k * D:(k + 1) * D] for k in range(4)]

    dec = ret_decay[0].astype(F32)
    sf, sb = _ret_states(dec, proj, cproj)
    y_ret = _ret_out(dec, proj, sf, sb)
    y_att, w_gate_b, w_up_b, w_out_b = _attn(
        attn_sink[0].astype(F32), proj, kd, vd, ckd, cvd,
        [(w_gate[0], 1, FFN_TILE), (w_up[0], 1, FFN_TILE), (w_out[0], 1, None)])

    x1, hff = _out_proj(y_ret, y_att, w_out_b, x2, gt_m, norm_ffn[0][None, :], sh_f, sc_f, tm=512)
    out = _ffn(hff, x1, gt_f, norm_final[None, :], w_gate_b, w_up_b, w_down[0], tm=1024)
    return out[None]
```

```python
import jax
import jax.numpy as jnp
import numpy as np
from jax import lax
from jax.experimental import pallas as pl
from jax.experimental.pallas import tpu as pltpu

GRID_W = 64
RET_HEADS = 8
RET_DK = 64
RET_DV = 128
RET_CHUNK = 128
ATT_HEADS = 16
ATT_KV_HEADS = 4
ATT_DH = 64
ATT_GROUP = ATT_HEADS // ATT_KV_HEADS
WINDOW = 128
ATT_BLOCK = 128
ROPE_BASE = 10000.0
NORM_EPS = 1e-6
K_SCALE = RET_DK ** -0.5
ATT_SCALE = ATT_DH ** -0.5
LOG2E = 1.4426950408889634

LANES = 128
RET_PAIRS = RET_HEADS // 2
MASK_NEG = -1e30
VMEM_LIMIT = 56 * 1024 * 1024
RET_STEP_CHUNKS = 4
ATT_STEP_BLOCKS = 2
OUT_ROW_CHUNK = 256
IN_ROW_CHUNK = 256
FFN_TILE = 512
FFN_ROW_CHUNK = 256

BF16 = jnp.bfloat16
F32 = jnp.float32


def _params(*sem):
    return pltpu.CompilerParams(dimension_semantics=sem, vmem_limit_bytes=VMEM_LIMIT)


def _with_cast_riders(body, n_in, n_out, n_rid):
    def wrapped(*refs):
        ins = refs[:n_in]
        rid_in = refs[n_in:n_in + n_rid]
        outs = refs[n_in + n_rid:n_in + n_rid + n_out]
        rid_out = refs[n_in + n_rid + n_out:n_in + 2 * n_rid + n_out]
        scratch = refs[n_in + 2 * n_rid + n_out:]
        for src, dst in zip(rid_in, rid_out):
            if len(dst.shape) == 2:
                dst[...] = src[...].astype(BF16)
            else:
                tc = dst.shape[2]
                for t in range(dst.shape[0]):
                    dst[t] = src[:, t * tc:(t + 1) * tc].astype(BF16)
        body(*ins, *outs, *scratch)
    return wrapped


def _rider_specs(riders, steps):
    in_specs, out_specs, shapes = [], [], []
    for w, ncb, tile in riders:
        rows, cols = w.shape
        nrb = steps // ncb
        assert nrb * ncb == steps and rows % nrb == 0 and cols % ncb == 0
        br, bc = rows // nrb, cols // ncb
        assert br % 16 == 0 and bc % LANES == 0, "slab must be bf16-tile aligned"
        in_specs.append(pl.BlockSpec((br, bc), lambda i, ncb=ncb: (i // ncb, i % ncb)))
        if tile is None:
            out_specs.append(in_specs[-1])
            shapes.append(jax.ShapeDtypeStruct(w.shape, BF16))
        else:
            assert ncb == 1 and cols % tile == 0 and tile % LANES == 0
            out_specs.append(pl.BlockSpec((cols // tile, br, tile), lambda i: (0, i, 0)))
            shapes.append(jax.ShapeDtypeStruct((cols // tile, rows, tile), BF16))
    return in_specs, out_specs, shapes


def _mod_kernel(cv_ref, w_ref, b_ref, o_ref):
    cv = cv_ref[...]
    s = cv / (1.0 + jnp.exp(-cv))
    o_ref[...] = jnp.dot(s.astype(BF16), w_ref[...].astype(BF16),
                         preferred_element_type=F32) + b_ref[...]


def _mod(cv, w, b, n):
    d = w.shape[0]
    tn = 1024
    assert n % tn == 0
    return pl.pallas_call(
        _mod_kernel,
        grid=(n // tn,),
        in_specs=[pl.BlockSpec((8, d), lambda j: (0, 0)),
                  pl.BlockSpec((d, tn), lambda j: (0, j)),
                  pl.BlockSpec((1, tn), lambda j: (0, j))],
        out_specs=pl.BlockSpec((8, tn), lambda j: (0, j)),
        out_shape=jax.ShapeDtypeStruct((8, n), F32),
        compiler_params=_params("parallel"),
        name="mod",
    )(cv, w, b)


def _rot_pairs(a, cos, sin_signed, half):
    lane = lax.broadcasted_iota(jnp.int32, a.shape, 1)
    first = (lane % (2 * half)) < half
    rot = jnp.where(first, pltpu.roll(a, LANES - half, 1), pltpu.roll(a, half, 1))
    return a * cos + rot * sin_signed


def _dup_halves(a):
    lane = lax.broadcasted_iota(jnp.int32, a.shape, 1)
    r = pltpu.roll(a, 64, 1)
    lo = lane < 64
    return jnp.where(lo, a, r), jnp.where(lo, r, a)


_PROJ_TILE = 512
_PROJ_TILE_KINDS = ("ret_q", "ret_k", "plain", "plain", "plain", "plain", "att_q", "att_q", "att_kv")


def _in_proj_kernel(x_ref, g_ref, sh_ref, sc_ref, w_ref, c1_ref, s1_ref, ca_ref, sa_ref,
                    cc_ref, wm_ref, bm_ref, o_ref, kd_ref, vd_ref, mod_ref):
    cc = cc_ref[...]
    s_col = cc / (1.0 + jnp.exp(-cc))
    mod_ref[0] = jnp.sum(wm_ref[...] * s_col, axis=0, keepdims=True) + bm_ref[...]

    tn = _PROJ_TILE
    for r in range(x_ref.shape[0] // IN_ROW_CHUNK):
        rs = slice(r * IN_ROW_CHUNK, (r + 1) * IN_ROW_CHUNK)
        xf = x_ref[rs, :]
        y = xf * lax.rsqrt(jnp.mean(xf * xf, axis=-1, keepdims=True) + NORM_EPS)
        y = y * g_ref[...]
        h = (y * (1.0 + sc_ref[...]) + sh_ref[...]).astype(BF16)

        def rope1(a):
            return _rot_pairs(a, c1_ref[rs, :], s1_ref[rs, :], 32)

        def ropea(a):
            return _rot_pairs(a, ca_ref[rs, :], sa_ref[rs, :], 16)

        for j, kind in enumerate(_PROJ_TILE_KINDS):
            acc = jnp.dot(h, w_ref[:, j * tn:(j + 1) * tn], preferred_element_type=F32)
            for c in range(tn // LANES):
                a = acc[:, c * LANES:(c + 1) * LANES]
                if kind == "ret_q":
                    a = rope1(a)
                elif kind == "ret_k":
                    a = rope1(a) * K_SCALE
                elif kind == "att_q":
                    a = ropea(a) * (ATT_SCALE * LOG2E)
                elif kind == "att_kv" and c < 2:
                    a = ropea(a)
                o_ref[rs, j * tn + c * LANES:j * tn + (c + 1) * LANES] = a.astype(BF16)
                if kind == "att_kv":
                    dup_ref = kd_ref if c < 2 else vd_ref
                    d0, d1 = _dup_halves(a)
                    t = 2 * (c % 2)
                    dup_ref[rs, t * LANES:(t + 1) * LANES] = d0.astype(BF16)
                    dup_ref[rs, (t + 1) * LANES:(t + 2) * LANES] = d1.astype(BF16)


def _in_proj(x, g, sh, sc, w, tabs, c_col, w_mod, b_mod, mod_done, *, tm):
    m, d = x.shape
    n = w.shape[1]
    assert n == _PROJ_TILE * len(_PROJ_TILE_KINDS) and m % tm == 0 and tm % IN_ROW_CHUNK == 0
    steps = m // tm
    slab = (w_mod.shape[1] - mod_done) // steps
    assert slab * steps == w_mod.shape[1] - mod_done and slab % LANES == 0 and mod_done % slab == 0
    slab0 = mod_done // slab
    c1, s1, ca, sa = tabs
    row = lambda i: (i, 0)
    vec = pl.BlockSpec((1, d), lambda i: (0, 0))
    tab = pl.BlockSpec((tm, LANES), row)
    return pl.pallas_call(
        _in_proj_kernel,
        grid=(m // tm,),
        in_specs=[pl.BlockSpec((tm, d), row), vec, vec, vec,
                  pl.BlockSpec((d, n), lambda i: (0, 0), pipeline_mode=pl.Buffered(1)),
                  tab, tab, tab, tab,
                  pl.BlockSpec((d, 1), lambda i: (0, 0)),
                  pl.BlockSpec((d, slab), lambda i: (0, slab0 + i)),
                  pl.BlockSpec((1, slab), lambda i: (0, slab0 + i))],
        out_specs=[pl.BlockSpec((tm, n), row),
                   pl.BlockSpec((tm, 512), row),
                   pl.BlockSpec((tm, 512), row),
                   pl.BlockSpec((1, 1, slab), lambda i: (i, 0, 0))],
        out_shape=[jax.ShapeDtypeStruct((m, n), BF16),
                   jax.ShapeDtypeStruct((m, 512), BF16),
                   jax.ShapeDtypeStruct((m, 512), BF16),
                   jax.ShapeDtypeStruct((steps, 1, slab), F32)],
        compiler_params=_params("parallel"),
        name="in_proj",
    )(x, g, sh, sc, w, c1, s1, ca, sa, c_col, w_mod, b_mod)


def _ctx_proj_kernel(x_ref, g_ref, sh_ref, sc_ref, w_ref, o_ref, kd_ref, vd_ref, wb_ref, h_ref):
    j = pl.program_id(0)

    @pl.when(j == 0)
    def _():
        xf = x_ref[...]
        y = xf * lax.rsqrt(jnp.mean(xf * xf, axis=-1, keepdims=True) + NORM_EPS)
        y = y * g_ref[...]
        h_ref[...] = (y * (1.0 + sc_ref[...]) + sh_ref[...]).astype(BF16)

    wb = w_ref[...].astype(BF16)
    wb_ref[...] = wb
    acc = jnp.dot(h_ref[...], wb, preferred_element_type=F32)
    is_ret_k = _PROJ_TILE_KINDS.index("ret_k")
    o_ref[...] = (acc * jnp.where(j == is_ret_k, K_SCALE, 1.0)).astype(BF16)

    @pl.when(j == _PROJ_TILE_KINDS.index("att_kv"))
    def _():
        for c in range(_PROJ_TILE // LANES):
            dup_ref = kd_ref if c < 2 else vd_ref
            d0, d1 = _dup_halves(acc[:, c * LANES:(c + 1) * LANES])
            t = 2 * (c % 2)
            dup_ref[:, t * LANES:(t + 1) * LANES] = d0.astype(BF16)
            dup_ref[:, (t + 1) * LANES:(t + 2) * LANES] = d1.astype(BF16)


def _ctx_proj(x, g, sh, sc, w):
    m, d = x.shape
    n = w.shape[1]
    tn = _PROJ_TILE
    assert n == tn * len(_PROJ_TILE_KINDS)
    fixed = lambda j: (0, 0)
    vec = pl.BlockSpec((1, d), fixed)
    return pl.pallas_call(
        _ctx_proj_kernel,
        grid=(n // tn,),
        in_specs=[pl.BlockSpec((m, d), fixed), vec, vec, vec,
                  pl.BlockSpec((d, tn), lambda j: (0, j))],
        out_specs=[pl.BlockSpec((m, tn), lambda j: (0, j)),
                   pl.BlockSpec((m, 512), fixed),
                   pl.BlockSpec((m, 512), fixed),
                   pl.BlockSpec((d, tn), lambda j: (0, j))],
        out_shape=[jax.ShapeDtypeStruct((m, n), BF16),
                   jax.ShapeDtypeStruct((m, 512), BF16),
                   jax.ShapeDtypeStruct((m, 512), BF16),
                   jax.ShapeDtypeStruct((d, n), BF16)],
        scratch_shapes=[pltpu.VMEM((m, d), BF16)],
        compiler_params=_params("arbitrary"),
        name="ctx_proj",
    )(x, g, sh, sc, w)


def _pair_lg(dec_ref, d, p, shape):
    lane = lax.broadcasted_iota(jnp.int32, shape, 1)
    first = (lane % LANES) < 64
    raw = jnp.where(first, jnp.full(shape, dec_ref[d, 2 * p], F32), jnp.full(shape, dec_ref[d, 2 * p + 1], F32))
    return -jnp.exp(raw)


def _head_block_mask(shape):
    r = lax.broadcasted_iota(jnp.int32, shape, 0)
    c = lax.broadcasted_iota(jnp.int32, shape, 1)
    return (r // 64) == (c // LANES)


def _kv_pair(k_pair, v_pair, w):
    kw = (k_pair.astype(F32) * w).astype(BF16)
    kv = lax.dot_general(kw, v_pair, (((0,), (0,)), ((), ())), preferred_element_type=F32)
    return jnp.where(_head_block_mask(kv.shape), kv, 0.0)


def _row_decay(dec_ref, d, p):
    shape = (LANES, 2 * RET_DV)
    rowh = lax.broadcasted_iota(jnp.int32, shape, 0) < 64
    raw = jnp.where(rowh, jnp.full(shape, dec_ref[d, 2 * p], F32), jnp.full(shape, dec_ref[d, 2 * p + 1], F32))
    return jnp.exp(-jnp.exp(raw) * float(RET_CHUNK))


def _compact_state(s):
    row = lax.broadcasted_iota(jnp.int32, (LANES, RET_DV), 0)
    return jnp.where(row < 64, s[:, :RET_DV], s[:, RET_DV:])


def _expand_state(c):
    row = lax.broadcasted_iota(jnp.int32, c.shape, 0)
    z = jnp.zeros_like(c)
    return jnp.concatenate([jnp.where(row < 64, c, z), jnp.where(row < 64, z, c)], axis=1)


def _ret_bwd_kernel(dec_ref, k_ref, v_ref, ck_ref, cv_ref, sb_ref, sbs):
    i = pl.program_id(0)
    C = RET_CHUNK
    lc = ck_ref.shape[0]

    @pl.when(i == 0)
    def _():
        pos = lax.broadcasted_iota(jnp.int32, (lc, LANES), 0).astype(F32)
        for p in range(RET_PAIRS):
            ks = slice(p * LANES, (p + 1) * LANES)
            vs = slice(p * 2 * RET_DV, (p + 1) * 2 * RET_DV)
            wb = jnp.exp(_pair_lg(dec_ref, 1, p, (lc, LANES)) * pos)
            sbs[p] = _kv_pair(ck_ref[:, ks], cv_ref[:, vs], wb)

    pos = lax.broadcasted_iota(jnp.int32, (C, LANES), 0).astype(F32)
    for p in range(RET_PAIRS):
        ks = slice(p * LANES, (p + 1) * LANES)
        vs = slice(p * 2 * RET_DV, (p + 1) * 2 * RET_DV)
        wb = jnp.exp(_pair_lg(dec_ref, 1, p, (C, LANES)) * pos)
        gb = _row_decay(dec_ref, 1, p)
        sb = sbs[p]
        for cc in reversed(range(RET_STEP_CHUNKS)):
            rs = slice(cc * C, (cc + 1) * C)
            sb_ref[cc, p] = _compact_state(sb).astype(BF16)
            sb = gb * sb + _kv_pair(k_ref[rs, ks], v_ref[rs, vs], wb)
        sbs[p] = sb


def _ret_bwd_states(dec, proj, cproj):
    L = proj.shape[0]
    lc = cproj.shape[0]
    S = RET_STEP_CHUNKS
    R = S * RET_CHUNK
    n = L // R
    return pl.pallas_call(
        _ret_bwd_kernel,
        grid=(n,),
        in_specs=[pl.BlockSpec(memory_space=pltpu.SMEM),
                  pl.BlockSpec((R, 512), lambda i: (n - 1 - i, 1)),
                  pl.BlockSpec((R, 1024), lambda i: (n - 1 - i, 1)),
                  pl.BlockSpec((lc, 512), lambda i: (0, 1)),
                  pl.BlockSpec((lc, 1024), lambda i: (0, 1))],
        out_specs=pl.BlockSpec((S, RET_PAIRS, LANES, RET_DV), lambda i: (n - 1 - i, 0, 0, 0)),
        out_shape=jax.ShapeDtypeStruct((n * S, RET_PAIRS, LANES, RET_DV), BF16),
        scratch_shapes=[pltpu.VMEM((RET_PAIRS, LANES, 2 * RET_DV), F32)],
        compiler_params=_params("arbitrary"),
        name="ret_bwd",
    )(dec, proj, proj, cproj, cproj)


def _ret_out_kernel(dec_ref, q_ref, k_ref, v_ref, g_ref, sb_ref, ck_ref, cv_ref, o_ref, sfs):
    i = pl.program_id(0)
    C = RET_CHUNK
    lc = ck_ref.shape[0]

    @pl.when(i == 0)
    def _():
        cpos = lax.broadcasted_iota(jnp.int32, (lc, LANES), 0).astype(F32)
        for p in range(RET_PAIRS):
            ks = slice(p * LANES, (p + 1) * LANES)
            vs = slice(p * 2 * RET_DV, (p + 1) * 2 * RET_DV)
            wf = jnp.exp(_pair_lg(dec_ref, 0, p, (lc, LANES)) * (lc - 1.0 - cpos))
            sfs[p] = _kv_pair(ck_ref[:, ks], cv_ref[:, vs], wf)

    pos = lax.broadcasted_iota(jnp.int32, (C, LANES), 0).astype(F32)
    n_i = lax.broadcasted_iota(jnp.int32, (C, 2 * C), 0)
    m_i = lax.broadcasted_iota(jnp.int32, (C, 2 * C), 1) % C
    rel = (n_i - m_i).astype(F32)
    lane = lax.broadcasted_iota(jnp.int32, (C, LANES), 1)
    lo = lane < 64
    for p in range(RET_PAIRS):
        ks = slice(p * LANES, (p + 1) * LANES)
        vs = slice(p * 2 * RET_DV, (p + 1) * 2 * RET_DV)
        col_a = lax.broadcasted_iota(jnp.int32, (C, 2 * C), 1) < C
        raw_f = jnp.where(col_a, jnp.full((C, 2 * C), dec_ref[0, 2 * p], F32), jnp.full((C, 2 * C), dec_ref[0, 2 * p + 1], F32))
        raw_b = jnp.where(col_a, jnp.full((C, 2 * C), dec_ref[1, 2 * p], F32), jnp.full((C, 2 * C), dec_ref[1, 2 * p + 1], F32))
        dmat = jnp.where(rel >= 0, jnp.exp(-jnp.exp(raw_f) * jnp.maximum(rel, 0.0)),
                         jnp.exp(-jnp.exp(raw_b) * jnp.maximum(-rel, 0.0)))
        lg_f = _pair_lg(dec_ref, 0, p, (C, LANES))
        wqf = jnp.exp(lg_f * (pos + 1.0))
        wqb = jnp.exp(_pair_lg(dec_ref, 1, p, (C, LANES)) * (float(C) - pos))
        wkf = jnp.exp(lg_f * (C - 1.0 - pos))
        gf = _row_decay(dec_ref, 0, p)
        sf = sfs[p]
        for cc in range(RET_STEP_CHUNKS):
            rs = slice(cc * C, (cc + 1) * C)
            q = q_ref[rs, ks]
            k = k_ref[rs, ks]
            v = v_ref[rs, vs]
            zk = jnp.zeros_like(k)
            kst = jnp.concatenate([jnp.where(lo, k, zk), jnp.where(lo, zk, k)], axis=0)
            s = lax.dot_general(q, kst, (((1,), (1,)), ((), ())), preferred_element_type=F32)
            sd = (s * dmat).astype(BF16)
            qf32 = q.astype(F32)
            qwf = (qf32 * wqf).astype(BF16)
            qwb = (qf32 * wqb).astype(BF16)
            zv = jnp.zeros((C, RET_DV), BF16)
            vbd = jnp.concatenate([jnp.concatenate([v[:, :RET_DV], zv], axis=1),
                                   jnp.concatenate([zv, v[:, RET_DV:]], axis=1)], axis=0)
            lhs = jnp.concatenate([sd, qwf, qwb], axis=1)
            rhs = jnp.concatenate([vbd, sf.astype(BF16), _expand_state(sb_ref[cc, p])], axis=0)
            o = jnp.dot(lhs, rhs, preferred_element_type=F32)
            sf = gf * sf + _kv_pair(k, v, wkf)
            for t in range(2):
                oh = o[:, t * RET_DV:(t + 1) * RET_DV]
                oh = oh * lax.rsqrt(jnp.mean(oh * oh, axis=-1, keepdims=True) + NORM_EPS)
                cs = slice(p * 2 * RET_DV + t * RET_DV, p * 2 * RET_DV + (t + 1) * RET_DV)
                gt = g_ref[rs, cs].astype(F32)
                o_ref[rs, cs] = (oh * (gt / (1.0 + jnp.exp(-gt)))).astype(BF16)
        sfs[p] = sf


def _ret_out(dec, proj, sb, cproj):
    L = proj.shape[0]
    lc = cproj.shape[0]
    S = RET_STEP_CHUNKS
    R = S * RET_CHUNK
    n = L // R
    return pl.pallas_call(
        _ret_out_kernel,
        grid=(n,),
        in_specs=[pl.BlockSpec(memory_space=pltpu.SMEM),
                  pl.BlockSpec((R, 512), lambda i: (i, 0)),
                  pl.BlockSpec((R, 512), lambda i: (i, 1)),
                  pl.BlockSpec((R, 1024), lambda i: (i, 1)),
                  pl.BlockSpec((R, 1024), lambda i: (i, 2)),
                  pl.BlockSpec((S, RET_PAIRS, LANES, RET_DV), lambda i: (i, 0, 0, 0)),
                  pl.BlockSpec((lc, 512), lambda i: (0, 1)),
                  pl.BlockSpec((lc, 1024), lambda i: (0, 1))],
        out_specs=pl.BlockSpec((R, RET_HEADS * RET_DV), lambda i: (i, 0)),
        out_shape=jax.ShapeDtypeStruct((L, RET_HEADS * RET_DV), BF16),
        scratch_shapes=[pltpu.VMEM((RET_PAIRS, LANES, 2 * RET_DV), F32)],
        compiler_params=_params("arbitrary"),
        name="ret_out",
    )(dec, proj, proj, proj, proj, sb, cproj, cproj)


def _attn_kernel(sink_ref, q_ref, kp_ref, kc_ref, kn_ref, vp_ref, vc_ref, vn_ref, ck_ref, cv_ref, o_ref):
    n = pl.program_id(0)
    nstep = pl.num_programs(0)
    B = ATT_BLOCK
    SB = ATT_STEP_BLOCKS
    kj = lax.broadcasted_iota(jnp.int32, (B, B), 0)
    qi = lax.broadcasted_iota(jnp.int32, (B, B), 1)
    ok_prev = jnp.where(n > 0, 0.0, MASK_NEG).astype(F32)
    ok_next = jnp.where(n < nstep - 1, 0.0, MASK_NEG).astype(F32)

    def band(inside, ok):
        return jnp.concatenate([jnp.where(inside, ok, MASK_NEG).astype(F32)] * ATT_GROUP, axis=1)

    bias_prev = [band(kj >= qi, ok_prev if j == 0 else 0.0) for j in range(SB)]
    bias_next = [band(kj <= qi, ok_next if j == SB - 1 else 0.0) for j in range(SB)]
    lane = lax.broadcasted_iota(jnp.int32, (B, LANES), 1)
    lo = lane < 64
    hi = lane >= 64

    def keys_of(j, gs, prev_ref, cur_ref, next_ref, ctx_ref):
        prev = prev_ref[:, gs] if j == 0 else cur_ref[(j - 1) * B:j * B, gs]
        nxt = next_ref[:, gs] if j == SB - 1 else cur_ref[(j + 1) * B:(j + 2) * B, gs]
        return jnp.concatenate([prev, cur_ref[j * B:(j + 1) * B, gs], nxt, ctx_ref[:, gs]], axis=0)

    def scores(j, g):
        gs = slice(g * LANES, (g + 1) * LANES)
        kcat = keys_of(j, gs, kp_ref, kc_ref, kn_ref, ck_ref)
        qs = []
        for r in range(ATT_GROUP):
            h = ATT_GROUP * g + r
            qt = q_ref[j * B:(j + 1) * B, (h // 2) * LANES:(h // 2 + 1) * LANES]
            keep = lo if h % 2 == 0 else hi
            qs.append(jnp.where(keep, qt, jnp.zeros_like(qt)))
        q4 = jnp.concatenate(qs, axis=0)
        return lax.dot_general(kcat, q4, (((1,), (1,)), ((), ())), preferred_element_type=F32)

    def softmax(j, g, s):
        sk = jnp.concatenate([jnp.full((1, B), sink_ref[ATT_GROUP * g + r], F32)
                              for r in range(ATT_GROUP)], axis=1) * LOG2E
        s = jnp.concatenate([s[:B] + bias_prev[j], s[B:2 * B], s[2 * B:3 * B] + bias_next[j], s[3 * B:]], axis=0)
        m = jnp.maximum(jnp.max(s, axis=0, keepdims=True), sk)
        e = jnp.exp2(s - m)
        den = jnp.sum(e, axis=0, keepdims=True) + jnp.exp2(sk - m)
        return e.astype(BF16), den

    def values(j, g, e, den):
        gs = slice(g * LANES, (g + 1) * LANES)
        vcat = keys_of(j, gs, vp_ref, vc_ref, vn_ref, cv_ref)
        res = lax.dot_general(vcat, e, (((0,), (0,)), ((), ())), preferred_element_type=F32) * (1.0 / den)
        for t in range(2):
            even = res[:, (2 * t) * B:(2 * t + 1) * B].T
            odd = res[:, (2 * t + 1) * B:(2 * t + 2) * B].T
            c0 = (2 * g + t) * LANES
            o_ref[j * B:(j + 1) * B, c0:c0 + LANES] = jnp.where(lo, even, odd).astype(BF16)

    units = [(j, g) for j in range(SB) for g in range(ATT_KV_HEADS)]
    s_next = scores(*units[0])
    pending = None
    for u, unit in enumerate(units):
        s_cur = s_next
        if u + 1 < len(units):
            s_next = scores(*units[u + 1])
        e_den = softmax(*unit, s_cur)
        if pending is not None:
            values(*units[u - 1], *pending)
        pending = e_den
    values(*units[-1], *pending)


def _attn(sink, proj, kd, vd, ckd, cvd, riders):
    L = proj.shape[0]
    B = ATT_BLOCK
    SB = ATT_STEP_BLOCKS
    n = L // (SB * B)
    nb = L // B
    lc = ckd.shape[0]
    prev = pl.BlockSpec((B, 512), lambda i: (jnp.maximum(i * SB - 1, 0), 0))
    cur = pl.BlockSpec((SB * B, 512), lambda i: (i, 0))
    nxt = pl.BlockSpec((B, 512), lambda i: (jnp.minimum((i + 1) * SB, nb - 1), 0))
    full = pl.BlockSpec((lc, 512), lambda i: (0, 0))
    rid_in_specs, rid_out_specs, rid_shapes = _rider_specs(riders, n)
    return pl.pallas_call(
        _with_cast_riders(_attn_kernel, 10, 1, len(riders)),
        grid=(n,),
        in_specs=[pl.BlockSpec(memory_space=pltpu.SMEM),
                  pl.BlockSpec((SB * B, 1024), lambda i: (i, 3)),
                  prev, cur, nxt, prev, cur, nxt, full, full] + rid_in_specs,
        out_specs=[pl.BlockSpec((SB * B, ATT_HEADS * ATT_DH), lambda i: (i, 0))] + rid_out_specs,
        out_shape=[jax.ShapeDtypeStruct((L, ATT_HEADS * ATT_DH), BF16)] + rid_shapes,
        compiler_params=_params("parallel"),
        name="attn",
    )(sink, proj, kd, kd, kd, vd, vd, vd, ckd, cvd, *[r[0] for r in riders])


def _out_proj_kernel(yr_ref, ya_ref, w_ref, x_ref, gt_ref, g_ref, sh_ref, sc_ref, o_ref, h_ref):
    kr = yr_ref.shape[1]
    for r in range(yr_ref.shape[0] // OUT_ROW_CHUNK):
        rs = slice(r * OUT_ROW_CHUNK, (r + 1) * OUT_ROW_CHUNK)
        acc = jnp.dot(yr_ref[rs, :], w_ref[:kr, :], preferred_element_type=F32)
        acc = acc + jnp.dot(ya_ref[rs, :], w_ref[kr:, :], preferred_element_type=F32)
        x1 = x_ref[rs, :] + gt_ref[...] * acc
        o_ref[rs, :] = x1
        y = x1 * lax.rsqrt(jnp.mean(x1 * x1, axis=-1, keepdims=True) + NORM_EPS)
        y = y * g_ref[...]
        h_ref[rs, :] = (y * (1.0 + sc_ref[...]) + sh_ref[...]).astype(BF16)


def _out_proj(yr, ya, w, x, gt, g, sh, sc, *, tm):
    m, d = x.shape
    kr, ka = yr.shape[1], ya.shape[1]
    row = lambda i: (i, 0)
    vec = pl.BlockSpec((1, d), lambda i: (0, 0))
    return pl.pallas_call(
        _out_proj_kernel,
        grid=(m // tm,),
        in_specs=[pl.BlockSpec((tm, kr), row), pl.BlockSpec((tm, ka), row),
                  pl.BlockSpec((kr + ka, d), lambda i: (0, 0)),
                  pl.BlockSpec((tm, d), row), vec, vec, vec, vec],
        out_specs=[pl.BlockSpec((tm, d), row), pl.BlockSpec((tm, d), row)],
        out_shape=[jax.ShapeDtypeStruct((m, d), F32), jax.ShapeDtypeStruct((m, d), BF16)],
        compiler_params=_params("parallel"),
        name="out_proj",
    )(yr, ya, w, x, gt, g, sh, sc)


def _ffn_kernel(h_ref, gt_ref, gfin_ref, wg_ref, wu_ref, wd_ref, x_hbm, o_ref, x_buf, sem):
    i = pl.program_id(0)
    f = pl.program_id(1)
    last = pl.num_programs(1) - 1
    rows = o_ref.shape[0]
    x_copy = pltpu.make_async_copy(x_hbm.at[pl.ds(pl.multiple_of(i * rows, rows), rows), :], x_buf, sem.at[0])

    def step(first, final):
        wd = wd_ref[...].astype(BF16)
        for r in range(rows // FFN_ROW_CHUNK):
            rs = slice(r * FFN_ROW_CHUNK, (r + 1) * FFN_ROW_CHUNK)
            h = h_ref[rs, :]
            a = jnp.dot(h, wg_ref[0], preferred_element_type=F32)
            u = jnp.dot(h, wu_ref[0], preferred_element_type=F32)
            act = ((a / (1.0 + jnp.exp(-a))) * u).astype(BF16)
            part = jnp.dot(act, wd, preferred_element_type=F32)
            if first:
                o_ref[rs, :] = part
            elif not final:
                o_ref[rs, :] += part
            else:
                y = x_buf[rs, :] + gt_ref[...] * (o_ref[rs, :] + part)
                y = y * lax.rsqrt(jnp.mean(y * y, axis=-1, keepdims=True) + NORM_EPS)
                o_ref[rs, :] = y * gfin_ref[...]

    @pl.when(f == 0)
    def _():
        x_copy.start()
        step(first=True, final=False)

    @pl.when((f > 0) & (f < last))
    def _():
        step(first=False, final=False)

    @pl.when(f == last)
    def _():
        x_copy.wait()
        step(first=False, final=True)


def _ffn(h, x, gt, gfin, wg, wu, wd, *, tm):
    m, d = x.shape
    nf = wg.shape[0]
    assert wg.shape == wu.shape == (nf, d, FFN_TILE) and wd.shape == (nf * FFN_TILE, d)
    assert m % tm == 0 and tm % FFN_ROW_CHUNK == 0
    row = lambda i, f: (i, 0)
    vec = pl.BlockSpec((1, d), lambda i, f: (0, 0))
    wcol = pl.BlockSpec((1, d, FFN_TILE), lambda i, f: (f, 0, 0))
    return pl.pallas_call(
        _ffn_kernel,
        grid=(m // tm, nf),
        in_specs=[pl.BlockSpec((tm, d), row), vec, vec, wcol, wcol,
                  pl.BlockSpec((FFN_TILE, d), lambda i, f: (f, 0)),
                  pl.BlockSpec(memory_space=pl.ANY)],
        out_specs=pl.BlockSpec((tm, d), row),
        out_shape=jax.ShapeDtypeStruct((m, d), F32),
        scratch_shapes=[pltpu.VMEM((tm, d), F32), pltpu.SemaphoreType.DMA((1,))],
        compiler_params=_params("arbitrary", "arbitrary"),
        name="ffn",
    )(h, gt, gfin, wg, wu, wd, x)


def _rope_tables(L):
    f32 = np.float32
    lane = np.arange(LANES)
    inv1 = f32(ROPE_BASE) ** (-np.arange(32, dtype=f32) / f32(32))
    ang1 = np.arange(L, dtype=f32)[:, None] * inv1[None, :]
    sgn1 = np.where((lane % 64) < 32, -1.0, 1.0).astype(f32)
    cos1 = np.tile(np.cos(ang1), (1, LANES // 32))
    sin1 = np.tile(np.sin(ang1), (1, LANES // 32)) * sgn1[None, :]
    inv2 = f32(ROPE_BASE) ** (-np.arange(16, dtype=f32) / f32(16))
    nrow = L // GRID_W
    ang_r = np.arange(nrow, dtype=f32)[:, None] * inv2[None, :]
    ang_c = np.arange(GRID_W, dtype=f32)[:, None] * inv2[None, :]
    sgna = np.where((lane % 32) < 16, -1.0, 1.0).astype(f32)

    def expand(fr, fc):
        by_row = np.broadcast_to(np.tile(fr, (1, 2))[:, None, :], (nrow, GRID_W, 32))
        by_col = np.broadcast_to(np.tile(fc, (1, 2))[None, :, :], (nrow, GRID_W, 32))
        head = np.concatenate([by_row, by_col], axis=-1).reshape(L, 64)
        return np.tile(head, (1, LANES // 64))

    cosa = expand(np.cos(ang_r), np.cos(ang_c))
    sina = expand(np.sin(ang_r), np.sin(ang_c)) * sgna[None, :]
    return tuple(np.ascontiguousarray(t, dtype=f32) for t in (cos1, sin1, cosa, sina))


def kernel(x, c, ctx, c_ctx, w_mod, b_mod, norm_mix, norm_ffn, w_in, ret_decay, attn_sink,
           w_out, w_gate, w_up, w_down, norm_final):
    B, L, D = x.shape
    assert B == 1 and w_mod.shape[0] == 1, "single batch element, depth-1 layer"
    x2 = x[0]
    xc2 = ctx[0]

    cv = jnp.zeros((8, D), F32).at[0].set(c[0]).at[1].set(c_ctx)
    mod = _mod(cv, w_mod[0], b_mod[0][None, :], 2 * D)
    sh_m, sc_m = mod[0:1, 0:D], mod[0:1, D:2 * D]
    sh_mc, sc_mc = mod[1:2, 0:D], mod[1:2, D:2 * D]

    g_mix = norm_mix[0][None, :]
    cproj, ckd, cvd, w_in_b = _ctx_proj(xc2, g_mix, sh_mc, sc_mc, w_in[0])
    proj, kd, vd, mod_rest = _in_proj(x2, g_mix, sh_m, sc_m, w_in_b, _rope_tables(L),
                                      c[0][:, None], w_mod[0], b_mod[0][None, :], 2 * D, tm=512)
    gt_m, sh_f, sc_f, gt_f = [mod_rest.reshape(1, 4 * D)[:, k * D:(k + 1) * D] for k in range(4)]

    dec = ret_decay[0].astype(F32)
    sb = _ret_bwd_states(dec, proj, cproj)
    y_ret = _ret_out(dec, proj, sb, cproj)
    y_att, w_gate_b, w_up_b, w_out_b = _attn(
        attn_sink[0].astype(F32), proj, kd, vd, ckd, cvd,
        [(w_gate[0], 1, FFN_TILE), (w_up[0], 1, FFN_TILE), (w_out[0], 1, None)])

    x1, hff = _out_proj(y_ret, y_att, w_out_b, x2, gt_m, norm_ffn[0][None, :], sh_f, sc_f, tm=512)
    out = _ffn(hff, x1, gt_f, norm_final[None, :], w_gate_b, w_up_b, w_down[0], tm=1024)
    return out[None]
```

```python
import jax
import jax.numpy as jnp
import numpy as np
from jax import lax
from jax.experimental import pallas as pl
from jax.experimental.pallas import tpu as pltpu

GRID_W = 64
RET_HEADS = 8
RET_DK = 64
RET_DV = 128
RET_CHUNK = 128
ATT_HEADS = 16
ATT_KV_HEADS = 4
ATT_DH = 64
ATT_GROUP = ATT_HEADS // ATT_KV_HEADS
WINDOW = 128
ATT_BLOCK = 128
ROPE_BASE = 10000.0
NORM_EPS = 1e-6
K_SCALE = RET_DK ** -0.5
ATT_SCALE = ATT_DH ** -0.5
LOG2E = 1.4426950408889634

LANES = 128
RET_PAIRS = RET_HEADS // 2
MASK_NEG = -1e30
VMEM_LIMIT = 56 * 1024 * 1024
RET_STEP_CHUNKS = 4
ATT_STEP_BLOCKS = 4
OUT_ROW_CHUNK = 256
IN_ROW_CHUNK = 256
FFN_TILE = 512
FFN_ROW_CHUNK = 512

BF16 = jnp.bfloat16
F32 = jnp.float32


def _params(*sem):
    return pltpu.CompilerParams(dimension_semantics=sem, vmem_limit_bytes=VMEM_LIMIT)


def _with_cast_riders(body, n_in, n_out, n_rid):
    def wrapped(*refs):
        ins = refs[:n_in]
        rid_in = refs[n_in:n_in + n_rid]
        outs = refs[n_in + n_rid:n_in + n_rid + n_out]
        rid_out = refs[n_in + n_rid + n_out:n_in + 2 * n_rid + n_out]
        scratch = refs[n_in + 2 * n_rid + n_out:]
        for src, dst in zip(rid_in, rid_out):
            if len(dst.shape) == 2:
                dst[...] = src[...].astype(BF16)
            else:
                tc = dst.shape[2]
                for t in range(dst.shape[0]):
                    dst[t] = src[:, t * tc:(t + 1) * tc].astype(BF16)
        body(*ins, *outs, *scratch)
    return wrapped


def _rider_specs(riders, steps):
    in_specs, out_specs, shapes = [], [], []
    for w, ncb, tile in riders:
        rows, cols = w.shape
        nrb = steps // ncb
        assert nrb * ncb == steps and rows % nrb == 0 and cols % ncb == 0
        br, bc = rows // nrb, cols // ncb
        assert br % 16 == 0 and bc % LANES == 0, "slab must be bf16-tile aligned"
        in_specs.append(pl.BlockSpec((br, bc), lambda i, ncb=ncb: (i // ncb, i % ncb)))
        if tile is None:
            out_specs.append(in_specs[-1])
            shapes.append(jax.ShapeDtypeStruct(w.shape, BF16))
        else:
            assert ncb == 1 and cols % tile == 0 and tile % LANES == 0
            out_specs.append(pl.BlockSpec((cols // tile, br, tile), lambda i: (0, i, 0)))
            shapes.append(jax.ShapeDtypeStruct((cols // tile, rows, tile), BF16))
    return in_specs, out_specs, shapes


def _mod_kernel(cv_ref, w_ref, b_ref, o_ref):
    cv = cv_ref[...]
    s = cv / (1.0 + jnp.exp(-cv))
    o_ref[...] = jnp.dot(s.astype(BF16), w_ref[...].astype(BF16),
                         preferred_element_type=F32) + b_ref[...]


def _mod(cv, w, b, n):
    d = w.shape[0]
    tn = 1024
    assert n % tn == 0
    return pl.pallas_call(
        _mod_kernel,
        grid=(n // tn,),
        in_specs=[pl.BlockSpec((8, d), lambda j: (0, 0)),
                  pl.BlockSpec((d, tn), lambda j: (0, j)),
                  pl.BlockSpec((1, tn), lambda j: (0, j))],
        out_specs=pl.BlockSpec((8, tn), lambda j: (0, j)),
        out_shape=jax.ShapeDtypeStruct((8, n), F32),
        compiler_params=_params("parallel"),
        name="mod",
    )(cv, w, b)


def _rot_pairs(a, cos, sin_signed, half):
    lane = lax.broadcasted_iota(jnp.int32, a.shape, 1)
    first = (lane % (2 * half)) < half
    rot = jnp.where(first, pltpu.roll(a, LANES - half, 1), pltpu.roll(a, half, 1))
    return a * cos + rot * sin_signed


def _dup_halves(a):
    lane = lax.broadcasted_iota(jnp.int32, a.shape, 1)
    r = pltpu.roll(a, 64, 1)
    lo = lane < 64
    return jnp.where(lo, a, r), jnp.where(lo, r, a)


_PROJ_TILE = 512
_PROJ_TILE_KINDS = ("ret_q", "ret_k", "plain", "plain", "plain", "plain", "att_q", "att_q", "att_kv")


def _in_proj_kernel(x_ref, g_ref, sh_ref, sc_ref, w_ref, c1_ref, s1_ref, ca_ref, sa_ref,
                    cc_ref, wm_ref, bm_ref, o_ref, kd_ref, vd_ref, mod_ref):
    cc = cc_ref[...]
    s_col = cc / (1.0 + jnp.exp(-cc))
    mod_ref[0] = jnp.sum(wm_ref[...] * s_col, axis=0, keepdims=True) + bm_ref[...]

    tn = _PROJ_TILE
    for r in range(x_ref.shape[0] // IN_ROW_CHUNK):
        rs = slice(r * IN_ROW_CHUNK, (r + 1) * IN_ROW_CHUNK)
        xf = x_ref[rs, :]
        y = xf * lax.rsqrt(jnp.mean(xf * xf, axis=-1, keepdims=True) + NORM_EPS)
        y = y * g_ref[...]
        h = (y * (1.0 + sc_ref[...]) + sh_ref[...]).astype(BF16)

        def rope1(a):
            return _rot_pairs(a, c1_ref[rs, :], s1_ref[rs, :], 32)

        def ropea(a):
            return _rot_pairs(a, ca_ref[rs, :], sa_ref[rs, :], 16)

        for j, kind in enumerate(_PROJ_TILE_KINDS):
            acc = jnp.dot(h, w_ref[:, j * tn:(j + 1) * tn], preferred_element_type=F32)
            for c in range(tn // LANES):
                a = acc[:, c * LANES:(c + 1) * LANES]
                if kind == "ret_q":
                    a = rope1(a)
                elif kind == "ret_k":
                    a = rope1(a) * K_SCALE
                elif kind == "att_q":
                    a = ropea(a) * (ATT_SCALE * LOG2E)
                elif kind == "att_kv" and c < 2:
                    a = ropea(a)
                o_ref[rs, j * tn + c * LANES:j * tn + (c + 1) * LANES] = a.astype(BF16)
                if kind == "att_kv":
                    dup_ref = kd_ref if c < 2 else vd_ref
                    d0, d1 = _dup_halves(a)
                    t = 2 * (c % 2)
                    dup_ref[rs, t * LANES:(t + 1) * LANES] = d0.astype(BF16)
                    dup_ref[rs, (t + 1) * LANES:(t + 2) * LANES] = d1.astype(BF16)


def _in_proj(x, g, sh, sc, w, tabs, c_col, w_mod, b_mod, mod_done, *, tm):
    m, d = x.shape
    n = w.shape[1]
    assert n == _PROJ_TILE * len(_PROJ_TILE_KINDS) and m % tm == 0 and tm % IN_ROW_CHUNK == 0
    steps = m // tm
    slab = (w_mod.shape[1] - mod_done) // steps
    assert slab * steps == w_mod.shape[1] - mod_done and slab % LANES == 0 and mod_done % slab == 0
    slab0 = mod_done // slab
    c1, s1, ca, sa = tabs
    row = lambda i: (i, 0)
    vec = pl.BlockSpec((1, d), lambda i: (0, 0))
    tab = pl.BlockSpec((tm, LANES), row)
    return pl.pallas_call(
        _in_proj_kernel,
        grid=(m // tm,),
        in_specs=[pl.BlockSpec((tm, d), row), vec, vec, vec,
                  pl.BlockSpec((d, n), lambda i: (0, 0), pipeline_mode=pl.Buffered(1)),
                  tab, tab, tab, tab,
                  pl.BlockSpec((d, 1), lambda i: (0, 0)),
                  pl.BlockSpec((d, slab), lambda i: (0, slab0 + i)),
                  pl.BlockSpec((1, slab), lambda i: (0, slab0 + i))],
        out_specs=[pl.BlockSpec((tm, n), row),
                   pl.BlockSpec((tm, 512), row),
                   pl.BlockSpec((tm, 512), row),
                   pl.BlockSpec((1, 1, slab), lambda i: (i, 0, 0))],
        out_shape=[jax.ShapeDtypeStruct((m, n), BF16),
                   jax.ShapeDtypeStruct((m, 512), BF16),
                   jax.ShapeDtypeStruct((m, 512), BF16),
                   jax.ShapeDtypeStruct((steps, 1, slab), F32)],
        compiler_params=_params("parallel"),
        name="in_proj",
    )(x, g, sh, sc, w, c1, s1, ca, sa, c_col, w_mod, b_mod)


def _ctx_proj_kernel(x_ref, g_ref, sh_ref, sc_ref, w_ref, o_ref, kd_ref, vd_ref, wb_ref, h_ref):
    j = pl.program_id(0)

    @pl.when(j == 0)
    def _():
        xf = x_ref[...]
        y = xf * lax.rsqrt(jnp.mean(xf * xf, axis=-1, keepdims=True) + NORM_EPS)
        y = y * g_ref[...]
        h_ref[...] = (y * (1.0 + sc_ref[...]) + sh_ref[...]).astype(BF16)

    wb = w_ref[...].astype(BF16)
    wb_ref[...] = wb
    acc = jnp.dot(h_ref[...], wb, preferred_element_type=F32)
    is_ret_k = _PROJ_TILE_KINDS.index("ret_k")
    o_ref[...] = (acc * jnp.where(j == is_ret_k, K_SCALE, 1.0)).astype(BF16)

    @pl.when(j == _PROJ_TILE_KINDS.index("att_kv"))
    def _():
        for c in range(_PROJ_TILE // LANES):
            dup_ref = kd_ref if c < 2 else vd_ref
            d0, d1 = _dup_halves(acc[:, c * LANES:(c + 1) * LANES])
            t = 2 * (c % 2)
            dup_ref[:, t * LANES:(t + 1) * LANES] = d0.astype(BF16)
            dup_ref[:, (t + 1) * LANES:(t + 2) * LANES] = d1.astype(BF16)


def _ctx_proj(x, g, sh, sc, w):
    m, d = x.shape
    n = w.shape[1]
    tn = _PROJ_TILE
    assert n == tn * len(_PROJ_TILE_KINDS)
    fixed = lambda j: (0, 0)
    vec = pl.BlockSpec((1, d), fixed)
    return pl.pallas_call(
        _ctx_proj_kernel,
        grid=(n // tn,),
        in_specs=[pl.BlockSpec((m, d), fixed), vec, vec, vec,
                  pl.BlockSpec((d, tn), lambda j: (0, j))],
        out_specs=[pl.BlockSpec((m, tn), lambda j: (0, j)),
                   pl.BlockSpec((m, 512), fixed),
                   pl.BlockSpec((m, 512), fixed),
                   pl.BlockSpec((d, tn), lambda j: (0, j))],
        out_shape=[jax.ShapeDtypeStruct((m, n), BF16),
                   jax.ShapeDtypeStruct((m, 512), BF16),
                   jax.ShapeDtypeStruct((m, 512), BF16),
                   jax.ShapeDtypeStruct((d, n), BF16)],
        scratch_shapes=[pltpu.VMEM((m, d), BF16)],
        compiler_params=_params("arbitrary"),
        name="ctx_proj",
    )(x, g, sh, sc, w)


def _pair_lg(dec_ref, d, p, shape):
    lane = lax.broadcasted_iota(jnp.int32, shape, 1)
    first = (lane % LANES) < 64
    raw = jnp.where(first, jnp.full(shape, dec_ref[d, 2 * p], F32), jnp.full(shape, dec_ref[d, 2 * p + 1], F32))
    return -jnp.exp(raw)


def _head_block_mask(shape):
    r = lax.broadcasted_iota(jnp.int32, shape, 0)
    c = lax.broadcasted_iota(jnp.int32, shape, 1)
    return (r // 64) == (c // LANES)


def _kv_pair(k_pair, v_pair, w):
    kw = (k_pair.astype(F32) * w).astype(BF16)
    kv = lax.dot_general(kw, v_pair, (((0,), (0,)), ((), ())), preferred_element_type=F32)
    return jnp.where(_head_block_mask(kv.shape), kv, 0.0)


def _row_decay(dec_ref, d, p):
    shape = (LANES, 2 * RET_DV)
    rowh = lax.broadcasted_iota(jnp.int32, shape, 0) < 64
    raw = jnp.where(rowh, jnp.full(shape, dec_ref[d, 2 * p], F32), jnp.full(shape, dec_ref[d, 2 * p + 1], F32))
    return jnp.exp(-jnp.exp(raw) * float(RET_CHUNK))


def _compact_state(s):
    row = lax.broadcasted_iota(jnp.int32, (LANES, RET_DV), 0)
    return jnp.where(row < 64, s[:, :RET_DV], s[:, RET_DV:])


def _expand_state(c):
    row = lax.broadcasted_iota(jnp.int32, c.shape, 0)
    z = jnp.zeros_like(c)
    return jnp.concatenate([jnp.where(row < 64, c, z), jnp.where(row < 64, z, c)], axis=1)


def _ret_bwd_kernel(dec_ref, k_ref, v_ref, ck_ref, cv_ref, sb_ref, sbs):
    i = pl.program_id(0)
    C = RET_CHUNK
    lc = ck_ref.shape[0]

    @pl.when(i == 0)
    def _():
        pos = lax.broadcasted_iota(jnp.int32, (lc, LANES), 0).astype(F32)
        for p in range(RET_PAIRS):
            ks = slice(p * LANES, (p + 1) * LANES)
            vs = slice(p * 2 * RET_DV, (p + 1) * 2 * RET_DV)
            wb = jnp.exp(_pair_lg(dec_ref, 1, p, (lc, LANES)) * pos)
            sbs[p] = _kv_pair(ck_ref[:, ks], cv_ref[:, vs], wb)

    pos = lax.broadcasted_iota(jnp.int32, (C, LANES), 0).astype(F32)
    for p in range(RET_PAIRS):
        ks = slice(p * LANES, (p + 1) * LANES)
        vs = slice(p * 2 * RET_DV, (p + 1) * 2 * RET_DV)
        wb = jnp.exp(_pair_lg(dec_ref, 1, p, (C, LANES)) * pos)
        gb = _row_decay(dec_ref, 1, p)
        sb = sbs[p]
        for cc in reversed(range(RET_STEP_CHUNKS)):
            rs = slice(cc * C, (cc + 1) * C)
            sb_ref[cc, p] = _compact_state(sb).astype(BF16)
            sb = gb * sb + _kv_pair(k_ref[rs, ks], v_ref[rs, vs], wb)
        sbs[p] = sb


def _ret_bwd_states(dec, proj, cproj):
    L = proj.shape[0]
    lc = cproj.shape[0]
    S = RET_STEP_CHUNKS
    R = S * RET_CHUNK
    n = L // R
    return pl.pallas_call(
        _ret_bwd_kernel,
        grid=(n,),
        in_specs=[pl.BlockSpec(memory_space=pltpu.SMEM),
                  pl.BlockSpec((R, 512), lambda i: (n - 1 - i, 1)),
                  pl.BlockSpec((R, 1024), lambda i: (n - 1 - i, 1)),
                  pl.BlockSpec((lc, 512), lambda i: (0, 1)),
                  pl.BlockSpec((lc, 1024), lambda i: (0, 1))],
        out_specs=pl.BlockSpec((S, RET_PAIRS, LANES, RET_DV), lambda i: (n - 1 - i, 0, 0, 0)),
        out_shape=jax.ShapeDtypeStruct((n * S, RET_PAIRS, LANES, RET_DV), BF16),
        scratch_shapes=[pltpu.VMEM((RET_PAIRS, LANES, 2 * RET_DV), F32)],
        compiler_params=_params("arbitrary"),
        name="ret_bwd",
    )(dec, proj, proj, cproj, cproj)


def _ret_out_kernel(dec_ref, q_ref, k_ref, v_ref, g_ref, sb_ref, ck_ref, cv_ref, o_ref, sfs):
    i = pl.program_id(0)
    C = RET_CHUNK
    lc = ck_ref.shape[0]

    @pl.when(i == 0)
    def _():
        cpos = lax.broadcasted_iota(jnp.int32, (lc, LANES), 0).astype(F32)
        for p in range(RET_PAIRS):
            ks = slice(p * LANES, (p + 1) * LANES)
            vs = slice(p * 2 * RET_DV, (p + 1) * 2 * RET_DV)
            wf = jnp.exp(_pair_lg(dec_ref, 0, p, (lc, LANES)) * (lc - 1.0 - cpos))
            sfs[p] = _kv_pair(ck_ref[:, ks], cv_ref[:, vs], wf)

    pos = lax.broadcasted_iota(jnp.int32, (C, LANES), 0).astype(F32)
    n_i = lax.broadcasted_iota(jnp.int32, (C, 2 * C), 0)
    m_i = lax.broadcasted_iota(jnp.int32, (C, 2 * C), 1) % C
    rel = (n_i - m_i).astype(F32)
    lane = lax.broadcasted_iota(jnp.int32, (C, LANES), 1)
    lo = lane < 64
    for p in range(RET_PAIRS):
        ks = slice(p * LANES, (p + 1) * LANES)
        vs = slice(p * 2 * RET_DV, (p + 1) * 2 * RET_DV)
        col_a = lax.broadcasted_iota(jnp.int32, (C, 2 * C), 1) < C
        raw_f = jnp.where(col_a, jnp.full((C, 2 * C), dec_ref[0, 2 * p], F32), jnp.full((C, 2 * C), dec_ref[0, 2 * p + 1], F32))
        raw_b = jnp.where(col_a, jnp.full((C, 2 * C), dec_ref[1, 2 * p], F32), jnp.full((C, 2 * C), dec_ref[1, 2 * p + 1], F32))
        dmat = jnp.where(rel >= 0, jnp.exp(-jnp.exp(raw_f) * jnp.maximum(rel, 0.0)),
                         jnp.exp(-jnp.exp(raw_b) * jnp.maximum(-rel, 0.0)))
        lg_f = _pair_lg(dec_ref, 0, p, (C, LANES))
        wqf = jnp.exp(lg_f * (pos + 1.0))
        wqb = jnp.exp(_pair_lg(dec_ref, 1, p, (C, LANES)) * (float(C) - pos))
        wkf = jnp.exp(lg_f * (C - 1.0 - pos))
        gf = _row_decay(dec_ref, 0, p)
        sf = sfs[p]
        for cc in range(RET_STEP_CHUNKS):
            rs = slice(cc * C, (cc + 1) * C)
            q = q_ref[rs, ks]
            k = k_ref[rs, ks]
            v = v_ref[rs, vs]
            zk = jnp.zeros_like(k)
            kst = jnp.concatenate([jnp.where(lo, k, zk), jnp.where(lo, zk, k)], axis=0)
            s = lax.dot_general(q, kst, (((1,), (1,)), ((), ())), preferred_element_type=F32)
            sd = (s * dmat).astype(BF16)
            qf32 = q.astype(F32)
            qwf = (qf32 * wqf).astype(BF16)
            qwb = (qf32 * wqb).astype(BF16)
            zv = jnp.zeros((C, RET_DV), BF16)
            vbd = jnp.concatenate([jnp.concatenate([v[:, :RET_DV], zv], axis=1),
                                   jnp.concatenate([zv, v[:, RET_DV:]], axis=1)], axis=0)
            lhs = jnp.concatenate([sd, qwf, qwb], axis=1)
            rhs = jnp.concatenate([vbd, sf.astype(BF16), _expand_state(sb_ref[cc, p])], axis=0)
            o = jnp.dot(lhs, rhs, preferred_element_type=F32)
            sf = gf * sf + _kv_pair(k, v, wkf)
            for t in range(2):
                oh = o[:, t * RET_DV:(t + 1) * RET_DV]
                oh = oh * lax.rsqrt(jnp.mean(oh * oh, axis=-1, keepdims=True) + NORM_EPS)
                cs = slice(p * 2 * RET_DV + t * RET_DV, p * 2 * RET_DV + (t + 1) * RET_DV)
                gt = g_ref[rs, cs].astype(F32)
                o_ref[rs, cs] = (oh * (gt / (1.0 + jnp.exp(-gt)))).astype(BF16)
        sfs[p] = sf


def _ret_out(dec, proj, sb, cproj):
    L = proj.shape[0]
    lc = cproj.shape[0]
    S = RET_STEP_CHUNKS
    R = S * RET_CHUNK
    n = L // R
    return pl.pallas_call(
        _ret_out_kernel,
        grid=(n,),
        in_specs=[pl.BlockSpec(memory_space=pltpu.SMEM),
                  pl.BlockSpec((R, 512), lambda i: (i, 0)),
                  pl.BlockSpec((R, 512), lambda i: (i, 1)),
                  pl.BlockSpec((R, 1024), lambda i: (i, 1)),
                  pl.BlockSpec((R, 1024), lambda i: (i, 2)),
                  pl.BlockSpec((S, RET_PAIRS, LANES, RET_DV), lambda i: (i, 0, 0, 0)),
                  pl.BlockSpec((lc, 512), lambda i: (0, 1)),
                  pl.BlockSpec((lc, 1024), lambda i: (0, 1))],
        out_specs=pl.BlockSpec((R, RET_HEADS * RET_DV), lambda i: (i, 0)),
        out_shape=jax.ShapeDtypeStruct((L, RET_HEADS * RET_DV), BF16),
        scratch_shapes=[pltpu.VMEM((RET_PAIRS, LANES, 2 * RET_DV), F32)],
        compiler_params=_params("arbitrary"),
        name="ret_out",
    )(dec, proj, proj, proj, proj, sb, cproj, cproj)


def _attn_kernel(sink_ref, q_ref, kp_ref, kc_ref, kn_ref, vp_ref, vc_ref, vn_ref, ck_ref, cv_ref, o_ref):
    n = pl.program_id(0)
    nstep = pl.num_programs(0)
    B = ATT_BLOCK
    SB = ATT_STEP_BLOCKS
    kj = lax.broadcasted_iota(jnp.int32, (B, B), 0)
    qi = lax.broadcasted_iota(jnp.int32, (B, B), 1)
    ok_prev = jnp.where(n > 0, 0.0, MASK_NEG).astype(F32)
    ok_next = jnp.where(n < nstep - 1, 0.0, MASK_NEG).astype(F32)

    def band(inside, ok):
        return jnp.concatenate([jnp.where(inside, ok, MASK_NEG).astype(F32)] * ATT_GROUP, axis=1)

    bias_prev = [band(kj >= qi, ok_prev if j == 0 else 0.0) for j in range(SB)]
    bias_next = [band(kj <= qi, ok_next if j == SB - 1 else 0.0) for j in range(SB)]
    lane = lax.broadcasted_iota(jnp.int32, (B, LANES), 1)
    lo = lane < 64
    hi = lane >= 64

    def keys_of(j, gs, prev_ref, cur_ref, next_ref, ctx_ref):
        prev = prev_ref[:, gs] if j == 0 else cur_ref[(j - 1) * B:j * B, gs]
        nxt = next_ref[:, gs] if j == SB - 1 else cur_ref[(j + 1) * B:(j + 2) * B, gs]
        return jnp.concatenate([prev, cur_ref[j * B:(j + 1) * B, gs], nxt, ctx_ref[:, gs]], axis=0)

    def scores(j, g):
        gs = slice(g * LANES, (g + 1) * LANES)
        kcat = keys_of(j, gs, kp_ref, kc_ref, kn_ref, ck_ref)
        qs = []
        for r in range(ATT_GROUP):
            h = ATT_GROUP * g + r
            qt = q_ref[j * B:(j + 1) * B, (h // 2) * LANES:(h // 2 + 1) * LANES]
            keep = lo if h % 2 == 0 else hi
            qs.append(jnp.where(keep, qt, jnp.zeros_like(qt)))
        q4 = jnp.concatenate(qs, axis=0)
        return lax.dot_general(kcat, q4, (((1,), (1,)), ((), ())), preferred_element_type=F32)

    def softmax(j, g, s):
        sk = jnp.concatenate([jnp.full((1, B), sink_ref[ATT_GROUP * g + r], F32)
                              for r in range(ATT_GROUP)], axis=1) * LOG2E
        s = jnp.concatenate([s[:B] + bias_prev[j], s[B:2 * B], s[2 * B:3 * B] + bias_next[j], s[3 * B:]], axis=0)
        m = jnp.maximum(jnp.max(s, axis=0, keepdims=True), sk)
        e = jnp.exp2(s - m)
        den = jnp.sum(e, axis=0, keepdims=True) + jnp.exp2(sk - m)
        return e.astype(BF16), den

    def values(j, g, e, den):
        gs = slice(g * LANES, (g + 1) * LANES)
        vcat = keys_of(j, gs, vp_ref, vc_ref, vn_ref, cv_ref)
        res = lax.dot_general(vcat, e, (((0,), (0,)), ((), ())), preferred_element_type=F32) * (1.0 / den)
        for t in range(2):
            even = res[:, (2 * t) * B:(2 * t + 1) * B].T
            odd = res[:, (2 * t + 1) * B:(2 * t + 2) * B].T
            c0 = (2 * g + t) * LANES
            o_ref[j * B:(j + 1) * B, c0:c0 + LANES] = jnp.where(lo, even, odd).astype(BF16)

    units = [(j, g) for j in range(SB) for g in range(ATT_KV_HEADS)]
    s_next = scores(*units[0])
    pending = None
    for u, unit in enumerate(units):
        s_cur = s_next
        if u + 1 < len(units):
            s_next = scores(*units[u + 1])
        e_den = softmax(*unit, s_cur)
        if pending is not None:
            values(*units[u - 1], *pending)
        pending = e_den
    values(*units[-1], *pending)


def _attn(sink, proj, kd, vd, ckd, cvd, riders):
    L = proj.shape[0]
    B = ATT_BLOCK
    SB = ATT_STEP_BLOCKS
    n = L // (SB * B)
    nb = L // B
    lc = ckd.shape[0]
    prev = pl.BlockSpec((B, 512), lambda i: (jnp.maximum(i * SB - 1, 0), 0))
    cur = pl.BlockSpec((SB * B, 512), lambda i: (i, 0))
    nxt = pl.BlockSpec((B, 512), lambda i: (jnp.minimum((i + 1) * SB, nb - 1), 0))
    full = pl.BlockSpec((lc, 512), lambda i: (0, 0))
    rid_in_specs, rid_out_specs, rid_shapes = _rider_specs(riders, n)
    return pl.pallas_call(
        _with_cast_riders(_attn_kernel, 10, 1, len(riders)),
        grid=(n,),
        in_specs=[pl.BlockSpec(memory_space=pltpu.SMEM),
                  pl.BlockSpec((SB * B, 1024), lambda i: (i, 3)),
                  prev, cur, nxt, prev, cur, nxt, full, full] + rid_in_specs,
        out_specs=[pl.BlockSpec((SB * B, ATT_HEADS * ATT_DH), lambda i: (i, 0))] + rid_out_specs,
        out_shape=[jax.ShapeDtypeStruct((L, ATT_HEADS * ATT_DH), BF16)] + rid_shapes,
        compiler_params=_params("parallel"),
        name="attn",
    )(sink, proj, kd, kd, kd, vd, vd, vd, ckd, cvd, *[r[0] for r in riders])


def _out_proj_kernel(yr_ref, ya_ref, w_ref, x_ref, gt_ref, g_ref, sh_ref, sc_ref, o_ref, h_ref):
    kr = yr_ref.shape[1]
    for r in range(yr_ref.shape[0] // OUT_ROW_CHUNK):
        rs = slice(r * OUT_ROW_CHUNK, (r + 1) * OUT_ROW_CHUNK)
        acc = jnp.dot(yr_ref[rs, :], w_ref[:kr, :], preferred_element_type=F32)
        acc = acc + jnp.dot(ya_ref[rs, :], w_ref[kr:, :], preferred_element_type=F32)
        x1 = x_ref[rs, :] + gt_ref[...] * acc
        o_ref[rs, :] = x1
        y = x1 * lax.rsqrt(jnp.mean(x1 * x1, axis=-1, keepdims=True) + NORM_EPS)
        y = y * g_ref[...]
        h_ref[rs, :] = (y * (1.0 + sc_ref[...]) + sh_ref[...]).astype(BF16)


def _out_proj(yr, ya, w, x, gt, g, sh, sc, *, tm):
    m, d = x.shape
    kr, ka = yr.shape[1], ya.shape[1]
    row = lambda i: (i, 0)
    vec = pl.BlockSpec((1, d), lambda i: (0, 0))
    return pl.pallas_call(
        _out_proj_kernel,
        grid=(m // tm,),
        in_specs=[pl.BlockSpec((tm, kr), row), pl.BlockSpec((tm, ka), row),
                  pl.BlockSpec((kr + ka, d), lambda i: (0, 0)),
                  pl.BlockSpec((tm, d), row), vec, vec, vec, vec],
        out_specs=[pl.BlockSpec((tm, d), row), pl.BlockSpec((tm, d), row)],
        out_shape=[jax.ShapeDtypeStruct((m, d), F32), jax.ShapeDtypeStruct((m, d), BF16)],
        compiler_params=_params("parallel"),
        name="out_proj",
    )(yr, ya, w, x, gt, g, sh, sc)


def _ffn_kernel(h_ref, gt_ref, gfin_ref, wg_ref, wu_ref, wd_ref, x_hbm, o_ref, x_buf, sem):
    i = pl.program_id(0)
    f = pl.program_id(1)
    last = pl.num_programs(1) - 1
    rows = o_ref.shape[0]
    x_copy = pltpu.make_async_copy(x_hbm.at[pl.ds(pl.multiple_of(i * rows, rows), rows), :], x_buf, sem.at[0])

    def step(first, final):
        wd = wd_ref[...].astype(BF16)
        for r in range(rows // FFN_ROW_CHUNK):
            rs = slice(r * FFN_ROW_CHUNK, (r + 1) * FFN_ROW_CHUNK)
            h = h_ref[rs, :]
            a = jnp.dot(h, wg_ref[0], preferred_element_type=F32)
            u = jnp.dot(h, wu_ref[0], preferred_element_type=F32)
            act = ((a / (1.0 + jnp.exp(-a))) * u).astype(BF16)
            part = jnp.dot(act, wd, preferred_element_type=F32)
            if first:
                o_ref[rs, :] = part
            elif not final:
                o_ref[rs, :] += part
            else:
                y = x_buf[rs, :] + gt_ref[...] * (o_ref[rs, :] + part)
                y = y * lax.rsqrt(jnp.mean(y * y, axis=-1, keepdims=True) + NORM_EPS)
                o_ref[rs, :] = y * gfin_ref[...]

    @pl.when(f == 0)
    def _():
        x_copy.start()
        step(first=True, final=False)

    @pl.when((f > 0) & (f < last))
    def _():
        step(first=False, final=False)

    @pl.when(f == last)
    def _():
        x_copy.wait()
        step(first=False, final=True)


def _ffn(h, x, gt, gfin, wg, wu, wd, *, tm):
    m, d = x.shape
    nf = wg.shape[0]
    assert wg.shape == wu.shape == (nf, d, FFN_TILE) and wd.shape == (nf * FFN_TILE, d)
    assert m % tm == 0 and tm % FFN_ROW_CHUNK == 0
    row = lambda i, f: (i, 0)
    vec = pl.BlockSpec((1, d), lambda i, f: (0, 0))
    wcol = pl.BlockSpec((1, d, FFN_TILE), lambda i, f: (f, 0, 0))
    return pl.pallas_call(
        _ffn_kernel,
        grid=(m // tm, nf),
        in_specs=[pl.BlockSpec((tm, d), row), vec, vec, wcol, wcol,
                  pl.BlockSpec((FFN_TILE, d), lambda i, f: (f, 0)),
                  pl.BlockSpec(memory_space=pl.ANY)],
        out_specs=pl.BlockSpec((tm, d), row),
        out_shape=jax.ShapeDtypeStruct((m, d), F32),
        scratch_shapes=[pltpu.VMEM((tm, d), F32), pltpu.SemaphoreType.DMA((1,))],
        compiler_params=_params("arbitrary", "arbitrary"),
        name="ffn",
    )(h, gt, gfin, wg, wu, wd, x)


def _rope_tables(L):
    f32 = np.float32
    lane = np.arange(LANES)
    inv1 = f32(ROPE_BASE) ** (-np.arange(32, dtype=f32) / f32(32))
    ang1 = np.arange(L, dtype=f32)[:, None] * inv1[None, :]
    sgn1 = np.where((lane % 64) < 32, -1.0, 1.0).astype(f32)
    cos1 = np.tile(np.cos(ang1), (1, LANES // 32))
    sin1 = np.tile(np.sin(ang1), (1, LANES // 32)) * sgn1[None, :]
    inv2 = f32(ROPE_BASE) ** (-np.arange(16, dtype=f32) / f32(16))
    nrow = L // GRID_W
    ang_r = np.arange(nrow, dtype=f32)[:, None] * inv2[None, :]
    ang_c = np.arange(GRID_W, dtype=f32)[:, None] * inv2[None, :]
    sgna = np.where((lane % 32) < 16, -1.0, 1.0).astype(f32)

    def expand(fr, fc):
        by_row = np.broadcast_to(np.tile(fr, (1, 2))[:, None, :], (nrow, GRID_W, 32))
        by_col = np.broadcast_to(np.tile(fc, (1, 2))[None, :, :], (nrow, GRID_W, 32))
        head = np.concatenate([by_row, by_col], axis=-1).reshape(L, 64)
        return np.tile(head, (1, LANES // 64))

    cosa = expand(np.cos(ang_r), np.cos(ang_c))
    sina = expand(np.sin(ang_r), np.sin(ang_c)) * sgna[None, :]
    return tuple(np.ascontiguousarray(t, dtype=f32) for t in (cos1, sin1, cosa, sina))


def kernel(x, c, ctx, c_ctx, w_mod, b_mod, norm_mix, norm_ffn, w_in, ret_decay, attn_sink,
           w_out, w_gate, w_up, w_down, norm_final):
    B, L, D = x.shape
    assert B == 1 and w_mod.shape[0] == 1, "single batch element, depth-1 layer"
    x2 = x[0]
    xc2 = ctx[0]

    cv = jnp.zeros((8, D), F32).at[0].set(c[0]).at[1].set(c_ctx)
    mod = _mod(cv, w_mod[0], b_mod[0][None, :], 2 * D)
    sh_m, sc_m = mod[0:1, 0:D], mod[0:1, D:2 * D]
    sh_mc, sc_mc = mod[1:2, 0:D], mod[1:2, D:2 * D]

    g_mix = norm_mix[0][None, :]
    cproj, ckd, cvd, w_in_b = _ctx_proj(xc2, g_mix, sh_mc, sc_mc, w_in[0])
    proj, kd, vd, mod_rest = _in_proj(x2, g_mix, sh_m, sc_m, w_in_b, _rope_tables(L),
                                      c[0][:, None], w_mod[0], b_mod[0][None, :], 2 * D, tm=512)
    gt_m, sh_f, sc_f, gt_f = [mod_rest.reshape(1, 4 * D)[:, k * D:(k + 1) * D] for k in range(4)]

    dec = ret_decay[0].astype(F32)
    sb = _ret_bwd_states(dec, proj, cproj)
    y_ret = _ret_out(dec, proj, sb, cproj)
    y_att, w_gate_b, w_up_b, w_out_b = _attn(
        attn_sink[0].astype(F32), proj, kd, vd, ckd, cvd,
        [(w_gate[0], 1, FFN_TILE), (w_up[0], 1, FFN_TILE), (w_out[0], 1, None)])

    x1, hff = _out_proj(y_ret, y_att, w_out_b, x2, gt_m, norm_ffn[0][None, :], sh_f, sc_f, tm=512)
    out = _ffn(hff, x1, gt_f, norm_final[None, :], w_gate_b, w_up_b, w_down[0], tm=1024)
    return out[None]
```

```python
import jax
import jax.numpy as jnp
import numpy as np
from jax import lax
from jax.experimental import pallas as pl
from jax.experimental.pallas import tpu as pltpu

GRID_W = 64
RET_HEADS = 8
RET_DK = 64
RET_DV = 128
RET_CHUNK = 128
ATT_HEADS = 16
ATT_KV_HEADS = 4
ATT_DH = 64
ATT_GROUP = ATT_HEADS // ATT_KV_HEADS
WINDOW = 128
ATT_BLOCK = 128
ROPE_BASE = 10000.0
NORM_EPS = 1e-6
K_SCALE = RET_DK ** -0.5
ATT_SCALE = ATT_DH ** -0.5
LOG2E = 1.4426950408889634

LANES = 128
RET_PAIRS = RET_HEADS // 2
MASK_NEG = -1e30
VMEM_LIMIT = 56 * 1024 * 1024
RET_STEP_CHUNKS = 4
ATT_STEP_BLOCKS = 4
OUT_ROW_CHUNK = 512
IN_ROW_CHUNK = 512
FFN_TILE = 512
FFN_ROW_CHUNK = 1024

BF16 = jnp.bfloat16
F32 = jnp.float32


def _params(*sem):
    return pltpu.CompilerParams(dimension_semantics=sem, vmem_limit_bytes=VMEM_LIMIT)


def _with_cast_riders(body, n_in, n_out, n_rid):
    def wrapped(*refs):
        ins = refs[:n_in]
        rid_in = refs[n_in:n_in + n_rid]
        outs = refs[n_in + n_rid:n_in + n_rid + n_out]
        rid_out = refs[n_in + n_rid + n_out:n_in + 2 * n_rid + n_out]
        scratch = refs[n_in + 2 * n_rid + n_out:]
        for src, dst in zip(rid_in, rid_out):
            if len(dst.shape) == 2:
                dst[...] = src[...].astype(BF16)
            else:
                tc = dst.shape[2]
                for t in range(dst.shape[0]):
                    dst[t] = src[:, t * tc:(t + 1) * tc].astype(BF16)
        body(*ins, *outs, *scratch)
    return wrapped


def _rider_specs(riders, steps):
    in_specs, out_specs, shapes = [], [], []
    for w, ncb, tile in riders:
        rows, cols = w.shape
        nrb = steps // ncb
        assert nrb * ncb == steps and rows % nrb == 0 and cols % ncb == 0
        br, bc = rows // nrb, cols // ncb
        assert br % 16 == 0 and bc % LANES == 0, "slab must be bf16-tile aligned"
        in_specs.append(pl.BlockSpec((br, bc), lambda i, ncb=ncb: (i // ncb, i % ncb)))
        if tile is None:
            out_specs.append(in_specs[-1])
            shapes.append(jax.ShapeDtypeStruct(w.shape, BF16))
        else:
            assert ncb == 1 and cols % tile == 0 and tile % LANES == 0
            out_specs.append(pl.BlockSpec((cols // tile, br, tile), lambda i: (0, i, 0)))
            shapes.append(jax.ShapeDtypeStruct((cols // tile, rows, tile), BF16))
    return in_specs, out_specs, shapes


def _mod_kernel(cv_ref, w_ref, b_ref, o_ref):
    cv = cv_ref[...]
    s = cv / (1.0 + jnp.exp(-cv))
    o_ref[...] = jnp.dot(s.astype(BF16), w_ref[...].astype(BF16),
                         preferred_element_type=F32) + b_ref[...]


def _mod(cv, w, b, n):
    d = w.shape[0]
    tn = 1024
    assert n % tn == 0
    return pl.pallas_call(
        _mod_kernel,
        grid=(n // tn,),
        in_specs=[pl.BlockSpec((8, d), lambda j: (0, 0)),
                  pl.BlockSpec((d, tn), lambda j: (0, j)),
                  pl.BlockSpec((1, tn), lambda j: (0, j))],
        out_specs=pl.BlockSpec((8, tn), lambda j: (0, j)),
        out_shape=jax.ShapeDtypeStruct((8, n), F32),
        compiler_params=_params("parallel"),
        name="mod",
    )(cv, w, b)


def _rot_pairs(a, cos, sin_signed, half):
    lane = lax.broadcasted_iota(jnp.int32, a.shape, 1)
    first = (lane % (2 * half)) < half
    rot = jnp.where(first, pltpu.roll(a, LANES - half, 1), pltpu.roll(a, half, 1))
    return a * cos + rot * sin_signed


def _dup_halves(a):
    lane = lax.broadcasted_iota(jnp.int32, a.shape, 1)
    r = pltpu.roll(a, 64, 1)
    lo = lane < 64
    return jnp.where(lo, a, r), jnp.where(lo, r, a)


_PROJ_TILE = 512
_PROJ_TILE_KINDS = ("ret_q", "ret_k", "plain", "plain", "plain", "plain", "att_q", "att_q", "att_kv")


def _in_proj_kernel(x_ref, g_ref, sh_ref, sc_ref, w_ref, c1_ref, s1_ref, ca_ref, sa_ref,
                    cc_ref, wm_ref, bm_ref, o_ref, kd_ref, vd_ref, mod_ref):
    cc = cc_ref[...]
    s_col = cc / (1.0 + jnp.exp(-cc))
    mod_ref[0] = jnp.sum(wm_ref[...] * s_col, axis=0, keepdims=True) + bm_ref[...]

    tn = _PROJ_TILE
    for r in range(x_ref.shape[0] // IN_ROW_CHUNK):
        rs = slice(r * IN_ROW_CHUNK, (r + 1) * IN_ROW_CHUNK)
        xf = x_ref[rs, :]
        y = xf * lax.rsqrt(jnp.mean(xf * xf, axis=-1, keepdims=True) + NORM_EPS)
        y = y * g_ref[...]
        h = (y * (1.0 + sc_ref[...]) + sh_ref[...]).astype(BF16)

        def rope1(a):
            return _rot_pairs(a, c1_ref[rs, :], s1_ref[rs, :], 32)

        def ropea(a):
            return _rot_pairs(a, ca_ref[rs, :], sa_ref[rs, :], 16)

        for j, kind in enumerate(_PROJ_TILE_KINDS):
            acc = jnp.dot(h, w_ref[:, j * tn:(j + 1) * tn], preferred_element_type=F32)
            for c in range(tn // LANES):
                a = acc[:, c * LANES:(c + 1) * LANES]
                if kind == "ret_q":
                    a = rope1(a)
                elif kind == "ret_k":
                    a = rope1(a) * K_SCALE
                elif kind == "att_q":
                    a = ropea(a) * (ATT_SCALE * LOG2E)
                elif kind == "att_kv" and c < 2:
                    a = ropea(a)
                o_ref[rs, j * tn + c * LANES:j * tn + (c + 1) * LANES] = a.astype(BF16)
                if kind == "att_kv":
                    dup_ref = kd_ref if c < 2 else vd_ref
                    d0, d1 = _dup_halves(a)
                    t = 2 * (c % 2)
                    dup_ref[rs, t * LANES:(t + 1) * LANES] = d0.astype(BF16)
                    dup_ref[rs, (t + 1) * LANES:(t + 2) * LANES] = d1.astype(BF16)


def _in_proj(x, g, sh, sc, w, tabs, c_col, w_mod, b_mod, mod_done, *, tm):
    m, d = x.shape
    n = w.shape[1]
    assert n == _PROJ_TILE * len(_PROJ_TILE_KINDS) and m % tm == 0 and tm % IN_ROW_CHUNK == 0
    steps = m // tm
    slab = (w_mod.shape[1] - mod_done) // steps
    assert slab * steps == w_mod.shape[1] - mod_done and slab % LANES == 0 and mod_done % slab == 0
    slab0 = mod_done // slab
    c1, s1, ca, sa = tabs
    row = lambda i: (i, 0)
    vec = pl.BlockSpec((1, d), lambda i: (0, 0))
    tab = pl.BlockSpec((tm, LANES), row)
    return pl.pallas_call(
        _in_proj_kernel,
        grid=(m // tm,),
        in_specs=[pl.BlockSpec((tm, d), row), vec, vec, vec,
                  pl.BlockSpec((d, n), lambda i: (0, 0), pipeline_mode=pl.Buffered(1)),
                  tab, tab, tab, tab,
                  pl.BlockSpec((d, 1), lambda i: (0, 0)),
                  pl.BlockSpec((d, slab), lambda i: (0, slab0 + i)),
                  pl.BlockSpec((1, slab), lambda i: (0, slab0 + i))],
        out_specs=[pl.BlockSpec((tm, n), row),
                   pl.BlockSpec((tm, 512), row),
                   pl.BlockSpec((tm, 512), row),
                   pl.BlockSpec((1, 1, slab), lambda i: (i, 0, 0))],
        out_shape=[jax.ShapeDtypeStruct((m, n), BF16),
                   jax.ShapeDtypeStruct((m, 512), BF16),
                   jax.ShapeDtypeStruct((m, 512), BF16),
                   jax.ShapeDtypeStruct((steps, 1, slab), F32)],
        compiler_params=_params("parallel"),
        name="in_proj",
    )(x, g, sh, sc, w, c1, s1, ca, sa, c_col, w_mod, b_mod)


def _ctx_proj_kernel(x_ref, g_ref, sh_ref, sc_ref, w_ref, o_ref, kd_ref, vd_ref, wb_ref, h_ref):
    j = pl.program_id(0)

    @pl.when(j == 0)
    def _():
        xf = x_ref[...]
        y = xf * lax.rsqrt(jnp.mean(xf * xf, axis=-1, keepdims=True) + NORM_EPS)
        y = y * g_ref[...]
        h_ref[...] = (y * (1.0 + sc_ref[...]) + sh_ref[...]).astype(BF16)

    wb = w_ref[...].astype(BF16)
    wb_ref[...] = wb
    acc = jnp.dot(h_ref[...], wb, preferred_element_type=F32)
    is_ret_k = _PROJ_TILE_KINDS.index("ret_k")
    o_ref[...] = (acc * jnp.where(j == is_ret_k, K_SCALE, 1.0)).astype(BF16)

    @pl.when(j == _PROJ_TILE_KINDS.index("att_kv"))
    def _():
        for c in range(_PROJ_TILE // LANES):
            dup_ref = kd_ref if c < 2 else vd_ref
            d0, d1 = _dup_halves(acc[:, c * LANES:(c + 1) * LANES])
            t = 2 * (c % 2)
            dup_ref[:, t * LANES:(t + 1) * LANES] = d0.astype(BF16)
            dup_ref[:, (t + 1) * LANES:(t + 2) * LANES] = d1.astype(BF16)


def _ctx_proj(x, g, sh, sc, w):
    m, d = x.shape
    n = w.shape[1]
    tn = _PROJ_TILE
    assert n == tn * len(_PROJ_TILE_KINDS)
    fixed = lambda j: (0, 0)
    vec = pl.BlockSpec((1, d), fixed)
    return pl.pallas_call(
        _ctx_proj_kernel,
        grid=(n // tn,),
        in_specs=[pl.BlockSpec((m, d), fixed), vec, vec, vec,
                  pl.BlockSpec((d, tn), lambda j: (0, j))],
        out_specs=[pl.BlockSpec((m, tn), lambda j: (0, j)),
                   pl.BlockSpec((m, 512), fixed),
                   pl.BlockSpec((m, 512), fixed),
                   pl.BlockSpec((d, tn), lambda j: (0, j))],
        out_shape=[jax.ShapeDtypeStruct((m, n), BF16),
                   jax.ShapeDtypeStruct((m, 512), BF16),
                   jax.ShapeDtypeStruct((m, 512), BF16),
                   jax.ShapeDtypeStruct((d, n), BF16)],
        scratch_shapes=[pltpu.VMEM((m, d), BF16)],
        compiler_params=_params("arbitrary"),
        name="ctx_proj",
    )(x, g, sh, sc, w)


def _pair_lg(dec_ref, d, p, shape):
    lane = lax.broadcasted_iota(jnp.int32, shape, 1)
    first = (lane % LANES) < 64
    raw = jnp.where(first, jnp.full(shape, dec_ref[d, 2 * p], F32), jnp.full(shape, dec_ref[d, 2 * p + 1], F32))
    return -jnp.exp(raw)


def _head_block_mask(shape):
    r = lax.broadcasted_iota(jnp.int32, shape, 0)
    c = lax.broadcasted_iota(jnp.int32, shape, 1)
    return (r // 64) == (c // LANES)


def _kv_pair(k_pair, v_pair, w):
    kw = (k_pair.astype(F32) * w).astype(BF16)
    kv = lax.dot_general(kw, v_pair, (((0,), (0,)), ((), ())), preferred_element_type=F32)
    return jnp.where(_head_block_mask(kv.shape), kv, 0.0)


def _row_decay(dec_ref, d, p):
    shape = (LANES, 2 * RET_DV)
    rowh = lax.broadcasted_iota(jnp.int32, shape, 0) < 64
    raw = jnp.where(rowh, jnp.full(shape, dec_ref[d, 2 * p], F32), jnp.full(shape, dec_ref[d, 2 * p + 1], F32))
    return jnp.exp(-jnp.exp(raw) * float(RET_CHUNK))


def _compact_state(s):
    row = lax.broadcasted_iota(jnp.int32, (LANES, RET_DV), 0)
    return jnp.where(row < 64, s[:, :RET_DV], s[:, RET_DV:])


def _expand_state(c):
    row = lax.broadcasted_iota(jnp.int32, c.shape, 0)
    z = jnp.zeros_like(c)
    return jnp.concatenate([jnp.where(row < 64, c, z), jnp.where(row < 64, z, c)], axis=1)


def _ret_bwd_kernel(dec_ref, k_ref, v_ref, ck_ref, cv_ref, sb_ref, sbs):
    i = pl.program_id(0)
    C = RET_CHUNK
    lc = ck_ref.shape[0]

    @pl.when(i == 0)
    def _():
        pos = lax.broadcasted_iota(jnp.int32, (lc, LANES), 0).astype(F32)
        for p in range(RET_PAIRS):
            ks = slice(p * LANES, (p + 1) * LANES)
            vs = slice(p * 2 * RET_DV, (p + 1) * 2 * RET_DV)
            wb = jnp.exp(_pair_lg(dec_ref, 1, p, (lc, LANES)) * pos)
            sbs[p] = _kv_pair(ck_ref[:, ks], cv_ref[:, vs], wb)

    pos = lax.broadcasted_iota(jnp.int32, (C, LANES), 0).astype(F32)
    for p in range(RET_PAIRS):
        ks = slice(p * LANES, (p + 1) * LANES)
        vs = slice(p * 2 * RET_DV, (p + 1) * 2 * RET_DV)
        wb = jnp.exp(_pair_lg(dec_ref, 1, p, (C, LANES)) * pos)
        gb = _row_decay(dec_ref, 1, p)
        sb = sbs[p]
        for cc in reversed(range(RET_STEP_CHUNKS)):
            rs = slice(cc * C, (cc + 1) * C)
            sb_ref[cc, p] = _compact_state(sb).astype(BF16)
            sb = gb * sb + _kv_pair(k_ref[rs, ks], v_ref[rs, vs], wb)
        sbs[p] = sb


def _ret_bwd_states(dec, proj, cproj):
    L = proj.shape[0]
    lc = cproj.shape[0]
    S = RET_STEP_CHUNKS
    R = S * RET_CHUNK
    n = L // R
    return pl.pallas_call(
        _ret_bwd_kernel,
        grid=(n,),
        in_specs=[pl.BlockSpec(memory_space=pltpu.SMEM),
                  pl.BlockSpec((R, 512), lambda i: (n - 1 - i, 1)),
                  pl.BlockSpec((R, 1024), lambda i: (n - 1 - i, 1)),
                  pl.BlockSpec((lc, 512), lambda i: (0, 1)),
                  pl.BlockSpec((lc, 1024), lambda i: (0, 1))],
        out_specs=pl.BlockSpec((S, RET_PAIRS, LANES, RET_DV), lambda i: (n - 1 - i, 0, 0, 0)),
        out_shape=jax.ShapeDtypeStruct((n * S, RET_PAIRS, LANES, RET_DV), BF16),
        scratch_shapes=[pltpu.VMEM((RET_PAIRS, LANES, 2 * RET_DV), F32)],
        compiler_params=_params("arbitrary"),
        name="ret_bwd",
    )(dec, proj, proj, cproj, cproj)


def _ret_out_kernel(dec_ref, q_ref, k_ref, v_ref, g_ref, sb_ref, ck_ref, cv_ref, o_ref, sfs):
    i = pl.program_id(0)
    C = RET_CHUNK
    lc = ck_ref.shape[0]

    @pl.when(i == 0)
    def _():
        cpos = lax.broadcasted_iota(jnp.int32, (lc, LANES), 0).astype(F32)
        for p in range(RET_PAIRS):
            ks = slice(p * LANES, (p + 1) * LANES)
            vs = slice(p * 2 * RET_DV, (p + 1) * 2 * RET_DV)
            wf = jnp.exp(_pair_lg(dec_ref, 0, p, (lc, LANES)) * (lc - 1.0 - cpos))
            sfs[p] = _kv_pair(ck_ref[:, ks], cv_ref[:, vs], wf)

    pos = lax.broadcasted_iota(jnp.int32, (C, LANES), 0).astype(F32)
    n_i = lax.broadcasted_iota(jnp.int32, (C, 2 * C), 0)
    m_i = lax.broadcasted_iota(jnp.int32, (C, 2 * C), 1) % C
    rel = (n_i - m_i).astype(F32)
    lane = lax.broadcasted_iota(jnp.int32, (C, LANES), 1)
    lo = lane < 64
    for p in range(RET_PAIRS):
        ks = slice(p * LANES, (p + 1) * LANES)
        vs = slice(p * 2 * RET_DV, (p + 1) * 2 * RET_DV)
        col_a = lax.broadcasted_iota(jnp.int32, (C, 2 * C), 1) < C
        raw_f = jnp.where(col_a, jnp.full((C, 2 * C), dec_ref[0, 2 * p], F32), jnp.full((C, 2 * C), dec_ref[0, 2 * p + 1], F32))
        raw_b = jnp.where(col_a, jnp.full((C, 2 * C), dec_ref[1, 2 * p], F32), jnp.full((C, 2 * C), dec_ref[1, 2 * p + 1], F32))
        dmat = jnp.where(rel >= 0, jnp.exp(-jnp.exp(raw_f) * jnp.maximum(rel, 0.0)),
                         jnp.exp(-jnp.exp(raw_b) * jnp.maximum(-rel, 0.0)))
        lg_f = _pair_lg(dec_ref, 0, p, (C, LANES))
        wqf = jnp.exp(lg_f * (pos + 1.0))
        wqb = jnp.exp(_pair_lg(dec_ref, 1, p, (C, LANES)) * (float(C) - pos))
        wkf = jnp.exp(lg_f * (C - 1.0 - pos))
        gf = _row_decay(dec_ref, 0, p)
        sf = sfs[p]
        for cc in range(RET_STEP_CHUNKS):
            rs = slice(cc * C, (cc + 1) * C)
            q = q_ref[rs, ks]
            k = k_ref[rs, ks]
            v = v_ref[rs, vs]
            zk = jnp.zeros_like(k)
            kst = jnp.concatenate([jnp.where(lo, k, zk), jnp.where(lo, zk, k)], axis=0)
            s = lax.dot_general(q, kst, (((1,), (1,)), ((), ())), preferred_element_type=F32)
            sd = (s * dmat).astype(BF16)
            qf32 = q.astype(F32)
            qwf = (qf32 * wqf).astype(BF16)
            qwb = (qf32 * wqb).astype(BF16)
            zv = jnp.zeros((C, RET_DV), BF16)
            vbd = jnp.concatenate([jnp.concatenate([v[:, :RET_DV], zv], axis=1),
                                   jnp.concatenate([zv, v[:, RET_DV:]], axis=1)], axis=0)
            lhs = jnp.concatenate([sd, qwf, qwb], axis=1)
            rhs = jnp.concatenate([vbd, sf.astype(BF16), _expand_state(sb_ref[cc, p])], axis=0)
            o = jnp.dot(lhs, rhs, preferred_element_type=F32)
            sf = gf * sf + _kv_pair(k, v, wkf)
            for t in range(2):
                oh = o[:, t * RET_DV:(t + 1) * RET_DV]
                oh = oh * lax.rsqrt(jnp.mean(oh * oh, axis=-1, keepdims=True) + NORM_EPS)
                cs = slice(p * 2 * RET_DV + t * RET_DV, p * 2 * RET_DV + (t + 1) * RET_DV)
                gt = g_ref[rs, cs].astype(F32)
                o_ref[rs, cs] = (oh * (gt / (1.0 + jnp.exp(-gt)))).astype(BF16)
        sfs[p] = sf


def _ret_out(dec, proj, sb, cproj):
    L = proj.shape[0]
    lc = cproj.shape[0]
    S = RET_STEP_CHUNKS
    R = S * RET_CHUNK
    n = L // R
    return pl.pallas_call(
        _ret_out_kernel,
        grid=(n,),
        in_specs=[pl.BlockSpec(memory_space=pltpu.SMEM),
                  pl.BlockSpec((R, 512), lambda i: (i, 0)),
                  pl.BlockSpec((R, 512), lambda i: (i, 1)),
                  pl.BlockSpec((R, 1024), lambda i: (i, 1)),
                  pl.BlockSpec((R, 1024), lambda i: (i, 2)),
                  pl.BlockSpec((S, RET_PAIRS, LANES, RET_DV), lambda i: (i, 0, 0, 0)),
                  pl.BlockSpec((lc, 512), lambda i: (0, 1)),
                  pl.BlockSpec((lc, 1024), lambda i: (0, 1))],
        out_specs=pl.BlockSpec((R, RET_HEADS * RET_DV), lambda i: (i, 0)),
        out_shape=jax.ShapeDtypeStruct((L, RET_HEADS * RET_DV), BF16),
        scratch_shapes=[pltpu.VMEM((RET_PAIRS, LANES, 2 * RET_DV), F32)],
        compiler_params=_params("arbitrary"),
        name="ret_out",
    )(dec, proj, proj, proj, proj, sb, cproj, cproj)


def _attn_kernel(sink_ref, q_ref, kp_ref, kc_ref, kn_ref, vp_ref, vc_ref, vn_ref, ck_ref, cv_ref, o_ref):
    n = pl.program_id(0)
    nstep = pl.num_programs(0)
    B = ATT_BLOCK
    SB = ATT_STEP_BLOCKS
    kj = lax.broadcasted_iota(jnp.int32, (B, B), 0)
    qi = lax.broadcasted_iota(jnp.int32, (B, B), 1)
    ok_prev = jnp.where(n > 0, 0.0, MASK_NEG).astype(F32)
    ok_next = jnp.where(n < nstep - 1, 0.0, MASK_NEG).astype(F32)

    def band(inside, ok):
        return jnp.concatenate([jnp.where(inside, ok, MASK_NEG).astype(F32)] * ATT_GROUP, axis=1)

    bias_prev = [band(kj >= qi, ok_prev if j == 0 else 0.0) for j in range(SB)]
    bias_next = [band(kj <= qi, ok_next if j == SB - 1 else 0.0) for j in range(SB)]
    lane = lax.broadcasted_iota(jnp.int32, (B, LANES), 1)
    lo = lane < 64
    hi = lane >= 64

    def keys_of(j, gs, prev_ref, cur_ref, next_ref, ctx_ref):
        prev = prev_ref[:, gs] if j == 0 else cur_ref[(j - 1) * B:j * B, gs]
        nxt = next_ref[:, gs] if j == SB - 1 else cur_ref[(j + 1) * B:(j + 2) * B, gs]
        return jnp.concatenate([prev, cur_ref[j * B:(j + 1) * B, gs], nxt, ctx_ref[:, gs]], axis=0)

    def scores(j, g):
        gs = slice(g * LANES, (g + 1) * LANES)
        kcat = keys_of(j, gs, kp_ref, kc_ref, kn_ref, ck_ref)
        qs = []
        for r in range(ATT_GROUP):
            h = ATT_GROUP * g + r
            qt = q_ref[j * B:(j + 1) * B, (h // 2) * LANES:(h // 2 + 1) * LANES]
            keep = lo if h % 2 == 0 else hi
            qs.append(jnp.where(keep, qt, jnp.zeros_like(qt)))
        q4 = jnp.concatenate(qs, axis=0)
        return lax.dot_general(kcat, q4, (((1,), (1,)), ((), ())), preferred_element_type=F32)

    def softmax(j, g, s):
        sk = jnp.concatenate([jnp.full((1, B), sink_ref[ATT_GROUP * g + r], F32)
                              for r in range(ATT_GROUP)], axis=1) * LOG2E
        s = jnp.concatenate([s[:B] + bias_prev[j], s[B:2 * B], s[2 * B:3 * B] + bias_next[j], s[3 * B:]], axis=0)
        m = jnp.maximum(jnp.max(s, axis=0, keepdims=True), sk)
        e = jnp.exp2(s - m)
        den = jnp.sum(e, axis=0, keepdims=True) + jnp.exp2(sk - m)
        return e.astype(BF16), den

    def values(j, g, e, den):
        gs = slice(g * LANES, (g + 1) * LANES)
        vcat = keys_of(j, gs, vp_ref, vc_ref, vn_ref, cv_ref)
        res = lax.dot_general(vcat, e, (((0,), (0,)), ((), ())), preferred_element_type=F32) * (1.0 / den)
        for t in range(2):
            even = res[:, (2 * t) * B:(2 * t + 1) * B].T
            odd = res[:, (2 * t + 1) * B:(2 * t + 2) * B].T
            c0 = (2 * g + t) * LANES
            o_ref[j * B:(j + 1) * B, c0:c0 + LANES] = jnp.where(lo, even, odd).astype(BF16)

    units = [(j, g) for j in range(SB) for g in range(ATT_KV_HEADS)]
    s_next = scores(*units[0])
    pending = None
    for u, unit in enumerate(units):
        s_cur = s_next
        if u + 1 < len(units):
            s_next = scores(*units[u + 1])
        e_den = softmax(*unit, s_cur)
        if pending is not None:
            values(*units[u - 1], *pending)
        pending = e_den
    values(*units[-1], *pending)


def _attn(sink, proj, kd, vd, ckd, cvd, riders):
    L = proj.shape[0]
    B = ATT_BLOCK
    SB = ATT_STEP_BLOCKS
    n = L // (SB * B)
    nb = L // B
    lc = ckd.shape[0]
    prev = pl.BlockSpec((B, 512), lambda i: (jnp.maximum(i * SB - 1, 0), 0))
    cur = pl.BlockSpec((SB * B, 512), lambda i: (i, 0))
    nxt = pl.BlockSpec((B, 512), lambda i: (jnp.minimum((i + 1) * SB, nb - 1), 0))
    full = pl.BlockSpec((lc, 512), lambda i: (0, 0))
    rid_in_specs, rid_out_specs, rid_shapes = _rider_specs(riders, n)
    return pl.pallas_call(
        _with_cast_riders(_attn_kernel, 10, 1, len(riders)),
        grid=(n,),
        in_specs=[pl.BlockSpec(memory_space=pltpu.SMEM),
                  pl.BlockSpec((SB * B, 1024), lambda i: (i, 3)),
                  prev, cur, nxt, prev, cur, nxt, full, full] + rid_in_specs,
        out_specs=[pl.BlockSpec((SB * B, ATT_HEADS * ATT_DH), lambda i: (i, 0))] + rid_out_specs,
        out_shape=[jax.ShapeDtypeStruct((L, ATT_HEADS * ATT_DH), BF16)] + rid_shapes,
        compiler_params=_params("parallel"),
        name="attn",
    )(sink, proj, kd, kd, kd, vd, vd, vd, ckd, cvd, *[r[0] for r in riders])


def _out_proj_kernel(yr_ref, ya_ref, w_ref, x_ref, gt_ref, g_ref, sh_ref, sc_ref, o_ref, h_ref):
    kr = yr_ref.shape[1]
    for r in range(yr_ref.shape[0] // OUT_ROW_CHUNK):
        rs = slice(r * OUT_ROW_CHUNK, (r + 1) * OUT_ROW_CHUNK)
        acc = jnp.dot(yr_ref[rs, :], w_ref[:kr, :], preferred_element_type=F32)
        acc = acc + jnp.dot(ya_ref[rs, :], w_ref[kr:, :], preferred_element_type=F32)
        x1 = x_ref[rs, :] + gt_ref[...] * acc
        o_ref[rs, :] = x1
        y = x1 * lax.rsqrt(jnp.mean(x1 * x1, axis=-1, keepdims=True) + NORM_EPS)
        y = y * g_ref[...]
        h_ref[rs, :] = (y * (1.0 + sc_ref[...]) + sh_ref[...]).astype(BF16)


def _out_proj(yr, ya, w, x, gt, g, sh, sc, *, tm):
    m, d = x.shape
    kr, ka = yr.shape[1], ya.shape[1]
    row = lambda i: (i, 0)
    vec = pl.BlockSpec((1, d), lambda i: (0, 0))
    return pl.pallas_call(
        _out_proj_kernel,
        grid=(m // tm,),
        in_specs=[pl.BlockSpec((tm, kr), row), pl.BlockSpec((tm, ka), row),
                  pl.BlockSpec((kr + ka, d), lambda i: (0, 0)),
                  pl.BlockSpec((tm, d), row), vec, vec, vec, vec],
        out_specs=[pl.BlockSpec((tm, d), row), pl.BlockSpec((tm, d), row)],
        out_shape=[jax.ShapeDtypeStruct((m, d), F32), jax.ShapeDtypeStruct((m, d), BF16)],
        compiler_params=_params("parallel"),
        name="out_proj",
    )(yr, ya, w, x, gt, g, sh, sc)


def _ffn_kernel(h_ref, gt_ref, gfin_ref, wg_ref, wu_ref, wd_ref, x_hbm, o_ref, x_buf, sem):
    i = pl.program_id(0)
    f = pl.program_id(1)
    last = pl.num_programs(1) - 1
    rows = o_ref.shape[0]
    x_copy = pltpu.make_async_copy(x_hbm.at[pl.ds(pl.multiple_of(i * rows, rows), rows), :], x_buf, sem.at[0])

    def step(first, final):
        wd = wd_ref[...].astype(BF16)
        for r in range(rows // FFN_ROW_CHUNK):
            rs = slice(r * FFN_ROW_CHUNK, (r + 1) * FFN_ROW_CHUNK)
            h = h_ref[rs, :]
            a = jnp.dot(h, wg_ref[0], preferred_element_type=F32)
            u = jnp.dot(h, wu_ref[0], preferred_element_type=F32)
            act = ((a / (1.0 + jnp.exp(-a))) * u).astype(BF16)
            part = jnp.dot(act, wd, preferred_element_type=F32)
            if first:
                o_ref[rs, :] = part
            elif not final:
                o_ref[rs, :] += part
            else:
                y = x_buf[rs, :] + gt_ref[...] * (o_ref[rs, :] + part)
                y = y * lax.rsqrt(jnp.mean(y * y, axis=-1, keepdims=True) + NORM_EPS)
                o_ref[rs, :] = y * gfin_ref[...]

    @pl.when(f == 0)
    def _():
        x_copy.start()
        step(first=True, final=False)

    @pl.when((f > 0) & (f < last))
    def _():
        step(first=False, final=False)

    @pl.when(f == last)
    def _():
        x_copy.wait()
        step(first=False, final=True)


def _ffn(h, x, gt, gfin, wg, wu, wd, *, tm):
    m, d = x.shape
    nf = wg.shape[0]
    assert wg.shape == wu.shape == (nf, d, FFN_TILE) and wd.shape == (nf * FFN_TILE, d)
    assert m % tm == 0 and tm % FFN_ROW_CHUNK == 0
    row = lambda i, f: (i, 0)
    vec = pl.BlockSpec((1, d), lambda i, f: (0, 0))
    wcol = pl.BlockSpec((1, d, FFN_TILE), lambda i, f: (f, 0, 0))
    return pl.pallas_call(
        _ffn_kernel,
        grid=(m // tm, nf),
        in_specs=[pl.BlockSpec((tm, d), row), vec, vec, wcol, wcol,
                  pl.BlockSpec((FFN_TILE, d), lambda i, f: (f, 0)),
                  pl.BlockSpec(memory_space=pl.ANY)],
        out_specs=pl.BlockSpec((tm, d), row),
        out_shape=jax.ShapeDtypeStruct((m, d), F32),
        scratch_shapes=[pltpu.VMEM((tm, d), F32), pltpu.SemaphoreType.DMA((1,))],
        compiler_params=_params("arbitrary", "arbitrary"),
        name="ffn",
    )(h, gt, gfin, wg, wu, wd, x)


def _rope_tables(L):
    f32 = np.float32
    lane = np.arange(LANES)
    inv1 = f32(ROPE_BASE) ** (-np.arange(32, dtype=f32) / f32(32))
    ang1 = np.arange(L, dtype=f32)[:, None] * inv1[None, :]
    sgn1 = np.where((lane % 64) < 32, -1.0, 1.0).astype(f32)
    cos1 = np.tile(np.cos(ang1), (1, LANES // 32))
    sin1 = np.tile(np.sin(ang1), (1, LANES // 32)) * sgn1[None, :]
    inv2 = f32(ROPE_BASE) ** (-np.arange(16, dtype=f32) / f32(16))
    nrow = L // GRID_W
    ang_r = np.arange(nrow, dtype=f32)[:, None] * inv2[None, :]
    ang_c = np.arange(GRID_W, dtype=f32)[:, None] * inv2[None, :]
    sgna = np.where((lane % 32) < 16, -1.0, 1.0).astype(f32)

    def expand(fr, fc):
        by_row = np.broadcast_to(np.tile(fr, (1, 2))[:, None, :], (nrow, GRID_W, 32))
        by_col = np.broadcast_to(np.tile(fc, (1, 2))[None, :, :], (nrow, GRID_W, 32))
        head = np.concatenate([by_row, by_col], axis=-1).reshape(L, 64)
        return np.tile(head, (1, LANES // 64))

    cosa = expand(np.cos(ang_r), np.cos(ang_c))
    sina = expand(np.sin(ang_r), np.sin(ang_c)) * sgna[None, :]
    return tuple(np.ascontiguousarray(t, dtype=f32) for t in (cos1, sin1, cosa, sina))


def kernel(x, c, ctx, c_ctx, w_mod, b_mod, norm_mix, norm_ffn, w_in, ret_decay, attn_sink,
           w_out, w_gate, w_up, w_down, norm_final):
    B, L, D = x.shape
    assert B == 1 and w_mod.shape[0] == 1, "single batch element, depth-1 layer"
    x2 = x[0]
    xc2 = ctx[0]

    cv = jnp.zeros((8, D), F32).at[0].set(c[0]).at[1].set(c_ctx)
    mod = _mod(cv, w_mod[0], b_mod[0][None, :], 2 * D)
    sh_m, sc_m = mod[0:1, 0:D], mod[0:1, D:2 * D]
    sh_mc, sc_mc = mod[1:2, 0:D], mod[1:2, D:2 * D]

    g_mix = norm_mix[0][None, :]
    cproj, ckd, cvd, w_in_b = _ctx_proj(xc2, g_mix, sh_mc, sc_mc, w_in[0])
    proj, kd, vd, mod_rest = _in_proj(x2, g_mix, sh_m, sc_m, w_in_b, _rope_tables(L),
                                      c[0][:, None], w_mod[0], b_mod[0][None, :], 2 * D, tm=512)
    gt_m, sh_f, sc_f, gt_f = [mod_rest.reshape(1, 4 * D)[:, k * D:(k + 1) * D] for k in range(4)]

    dec = ret_decay[0].astype(F32)
    sb = _ret_bwd_states(dec, proj, cproj)
    y_ret = _ret_out(dec, proj, sb, cproj)
    y_att, w_gate_b, w_up_b, w_out_b = _attn(
        attn_sink[0].astype(F32), proj, kd, vd, ckd, cvd,
        [(w_gate[0], 1, FFN_TILE), (w_up[0], 1, FFN_TILE), (w_out[0], 1, None)])

    x1, hff = _out_proj(y_ret, y_att, w_out_b, x2, gt_m, norm_ffn[0][None, :], sh_f, sc_f, tm=512)
    out = _ffn(hff, x1, gt_f, norm_final[None, :], w_gate_b, w_up_b, w_down[0], tm=1024)
    return out[None]
```

```python
import jax
import jax.numpy as jnp
import numpy as np
from jax import lax
from jax.experimental import pallas as pl
from jax.experimental.pallas import tpu as pltpu

GRID_W = 64
RET_HEADS = 8
RET_DK = 64
RET_DV = 128
RET_CHUNK = 128
ATT_HEADS = 16
ATT_KV_HEADS = 4
ATT_DH = 64
ATT_GROUP = ATT_HEADS // ATT_KV_HEADS
WINDOW = 128
ATT_BLOCK = 128
ROPE_BASE = 10000.0
NORM_EPS = 1e-6
K_SCALE = RET_DK ** -0.5
ATT_SCALE = ATT_DH ** -0.5
LOG2E = 1.4426950408889634

LANES = 128
RET_PAIRS = RET_HEADS // 2
MASK_NEG = -1e30
VMEM_LIMIT = 56 * 1024 * 1024
RET_STEP_CHUNKS = 8
ATT_STEP_BLOCKS = 4
OUT_ROW_CHUNK = 512
IN_ROW_CHUNK = 512
FFN_TILE = 512
FFN_ROW_CHUNK = 1024

BF16 = jnp.bfloat16
F32 = jnp.float32


def _params(*sem):
    return pltpu.CompilerParams(dimension_semantics=sem, vmem_limit_bytes=VMEM_LIMIT)


def _with_cast_riders(body, n_in, n_out, n_rid):
    def wrapped(*refs):
        ins = refs[:n_in]
        rid_in = refs[n_in:n_in + n_rid]
        outs = refs[n_in + n_rid:n_in + n_rid + n_out]
        rid_out = refs[n_in + n_rid + n_out:n_in + 2 * n_rid + n_out]
        scratch = refs[n_in + 2 * n_rid + n_out:]
        for src, dst in zip(rid_in, rid_out):
            if len(dst.shape) == 2:
                dst[...] = src[...].astype(BF16)
            else:
                tc = dst.shape[2]
                for t in range(dst.shape[0]):
                    dst[t] = src[:, t * tc:(t + 1) * tc].astype(BF16)
        body(*ins, *outs, *scratch)
    return wrapped


def _rider_specs(riders, steps):
    in_specs, out_specs, shapes = [], [], []
    for w, ncb, tile in riders:
        rows, cols = w.shape
        nrb = steps // ncb
        assert nrb * ncb == steps and rows % nrb == 0 and cols % ncb == 0
        br, bc = rows // nrb, cols // ncb
        assert br % 16 == 0 and bc % LANES == 0, "slab must be bf16-tile aligned"
        in_specs.append(pl.BlockSpec((br, bc), lambda i, ncb=ncb: (i // ncb, i % ncb)))
        if tile is None:
            out_specs.append(in_specs[-1])
            shapes.append(jax.ShapeDtypeStruct(w.shape, BF16))
        else:
            assert ncb == 1 and cols % tile == 0 and tile % LANES == 0
            out_specs.append(pl.BlockSpec((cols // tile, br, tile), lambda i: (0, i, 0)))
            shapes.append(jax.ShapeDtypeStruct((cols // tile, rows, tile), BF16))
    return in_specs, out_specs, shapes


def _mod_kernel(cv_ref, w_ref, b_ref, o_ref):
    cv = cv_ref[...]
    s = cv / (1.0 + jnp.exp(-cv))
    o_ref[...] = jnp.dot(s.astype(BF16), w_ref[...].astype(BF16),
                         preferred_element_type=F32) + b_ref[...]


def _mod(cv, w, b, n):
    d = w.shape[0]
    tn = 1024
    assert n % tn == 0
    return pl.pallas_call(
        _mod_kernel,
        grid=(n // tn,),
        in_specs=[pl.BlockSpec((8, d), lambda j: (0, 0)),
                  pl.BlockSpec((d, tn), lambda j: (0, j)),
                  pl.BlockSpec((1, tn), lambda j: (0, j))],
        out_specs=pl.BlockSpec((8, tn), lambda j: (0, j)),
        out_shape=jax.ShapeDtypeStruct((8, n), F32),
        compiler_params=_params("parallel"),
        name="mod",
    )(cv, w, b)


def _rot_pairs(a, cos, sin_signed, half):
    lane = lax.broadcasted_iota(jnp.int32, a.shape, 1)
    first = (lane % (2 * half)) < half
    rot = jnp.where(first, pltpu.roll(a, LANES - half, 1), pltpu.roll(a, half, 1))
    return a * cos + rot * sin_signed


def _dup_halves(a):
    lane = lax.broadcasted_iota(jnp.int32, a.shape, 1)
    r = pltpu.roll(a, 64, 1)
    lo = lane < 64
    return jnp.where(lo, a, r), jnp.where(lo, r, a)


_PROJ_TILE = 512
_PROJ_TILE_KINDS = ("ret_q", "ret_k", "plain", "plain", "plain", "plain", "att_q", "att_q", "att_kv")


def _in_proj_kernel(x_ref, g_ref, sh_ref, sc_ref, w_ref, c1_ref, s1_ref, ca_ref, sa_ref,
                    cc_ref, wm_ref, bm_ref, o_ref, kd_ref, vd_ref, mod_ref):
    cc = cc_ref[...]
    s_col = cc / (1.0 + jnp.exp(-cc))
    mod_ref[0] = jnp.sum(wm_ref[...] * s_col, axis=0, keepdims=True) + bm_ref[...]

    tn = _PROJ_TILE
    for r in range(x_ref.shape[0] // IN_ROW_CHUNK):
        rs = slice(r * IN_ROW_CHUNK, (r + 1) * IN_ROW_CHUNK)
        xf = x_ref[rs, :]
        y = xf * lax.rsqrt(jnp.mean(xf * xf, axis=-1, keepdims=True) + NORM_EPS)
        y = y * g_ref[...]
        h = (y * (1.0 + sc_ref[...]) + sh_ref[...]).astype(BF16)

        def rope1(a):
            return _rot_pairs(a, c1_ref[rs, :], s1_ref[rs, :], 32)

        def ropea(a):
            return _rot_pairs(a, ca_ref[rs, :], sa_ref[rs, :], 16)

        for j, kind in enumerate(_PROJ_TILE_KINDS):
            acc = jnp.dot(h, w_ref[:, j * tn:(j + 1) * tn], preferred_element_type=F32)
            for c in range(tn // LANES):
                a = acc[:, c * LANES:(c + 1) * LANES]
                if kind == "ret_q":
                    a = rope1(a)
                elif kind == "ret_k":
                    a = rope1(a) * K_SCALE
                elif kind == "att_q":
                    a = ropea(a) * (ATT_SCALE * LOG2E)
                elif kind == "att_kv" and c < 2:
                    a = ropea(a)
                o_ref[rs, j * tn + c * LANES:j * tn + (c + 1) * LANES] = a.astype(BF16)
                if kind == "att_kv":
                    dup_ref = kd_ref if c < 2 else vd_ref
                    d0, d1 = _dup_halves(a)
                    t = 2 * (c % 2)
                    dup_ref[rs, t * LANES:(t + 1) * LANES] = d0.astype(BF16)
                    dup_ref[rs, (t + 1) * LANES:(t + 2) * LANES] = d1.astype(BF16)


def _in_proj(x, g, sh, sc, w, tabs, c_col, w_mod, b_mod, mod_done, *, tm):
    m, d = x.shape
    n = w.shape[1]
    assert n == _PROJ_TILE * len(_PROJ_TILE_KINDS) and m % tm == 0 and tm % IN_ROW_CHUNK == 0
    steps = m // tm
    slab = (w_mod.shape[1] - mod_done) // steps
    assert slab * steps == w_mod.shape[1] - mod_done and slab % LANES == 0 and mod_done % slab == 0
    slab0 = mod_done // slab
    c1, s1, ca, sa = tabs
    row = lambda i: (i, 0)
    vec = pl.BlockSpec((1, d), lambda i: (0, 0))
    tab = pl.BlockSpec((tm, LANES), row)
    return pl.pallas_call(
        _in_proj_kernel,
        grid=(m // tm,),
        in_specs=[pl.BlockSpec((tm, d), row), vec, vec, vec,
                  pl.BlockSpec((d, n), lambda i: (0, 0), pipeline_mode=pl.Buffered(1)),
                  tab, tab, tab, tab,
                  pl.BlockSpec((d, 1), lambda i: (0, 0)),
                  pl.BlockSpec((d, slab), lambda i: (0, slab0 + i)),
                  pl.BlockSpec((1, slab), lambda i: (0, slab0 + i))],
        out_specs=[pl.BlockSpec((tm, n), row),
                   pl.BlockSpec((tm, 512), row),
                   pl.BlockSpec((tm, 512), row),
                   pl.BlockSpec((1, 1, slab), lambda i: (i, 0, 0))],
        out_shape=[jax.ShapeDtypeStruct((m, n), BF16),
                   jax.ShapeDtypeStruct((m, 512), BF16),
                   jax.ShapeDtypeStruct((m, 512), BF16),
                   jax.ShapeDtypeStruct((steps, 1, slab), F32)],
        compiler_params=_params("parallel"),
        name="in_proj",
    )(x, g, sh, sc, w, c1, s1, ca, sa, c_col, w_mod, b_mod)


def _ctx_proj_kernel(x_ref, g_ref, sh_ref, sc_ref, w_ref, o_ref, kd_ref, vd_ref, wb_ref, h_ref):
    j = pl.program_id(0)

    @pl.when(j == 0)
    def _():
        xf = x_ref[...]
        y = xf * lax.rsqrt(jnp.mean(xf * xf, axis=-1, keepdims=True) + NORM_EPS)
        y = y * g_ref[...]
        h_ref[...] = (y * (1.0 + sc_ref[...]) + sh_ref[...]).astype(BF16)

    wb = w_ref[...].astype(BF16)
    wb_ref[...] = wb
    acc = jnp.dot(h_ref[...], wb, preferred_element_type=F32)
    is_ret_k = _PROJ_TILE_KINDS.index("ret_k")
    o_ref[...] = (acc * jnp.where(j == is_ret_k, K_SCALE, 1.0)).astype(BF16)

    @pl.when(j == _PROJ_TILE_KINDS.index("att_kv"))
    def _():
        for c in range(_PROJ_TILE // LANES):
            dup_ref = kd_ref if c < 2 else vd_ref
            d0, d1 = _dup_halves(acc[:, c * LANES:(c + 1) * LANES])
            t = 2 * (c % 2)
            dup_ref[:, t * LANES:(t + 1) * LANES] = d0.astype(BF16)
            dup_ref[:, (t + 1) * LANES:(t + 2) * LANES] = d1.astype(BF16)


def _ctx_proj(x, g, sh, sc, w):
    m, d = x.shape
    n = w.shape[1]
    tn = _PROJ_TILE
    assert n == tn * len(_PROJ_TILE_KINDS)
    fixed = lambda j: (0, 0)
    vec = pl.BlockSpec((1, d), fixed)
    return pl.pallas_call(
        _ctx_proj_kernel,
        grid=(n // tn,),
        in_specs=[pl.BlockSpec((m, d), fixed), vec, vec, vec,
                  pl.BlockSpec((d, tn), lambda j: (0, j))],
        out_specs=[pl.BlockSpec((m, tn), lambda j: (0, j)),
                   pl.BlockSpec((m, 512), fixed),
                   pl.BlockSpec((m, 512), fixed),
                   pl.BlockSpec((d, tn), lambda j: (0, j))],
        out_shape=[jax.ShapeDtypeStruct((m, n), BF16),
                   jax.ShapeDtypeStruct((m, 512), BF16),
                   jax.ShapeDtypeStruct((m, 512), BF16),
                   jax.ShapeDtypeStruct((d, n), BF16)],
        scratch_shapes=[pltpu.VMEM((m, d), BF16)],
        compiler_params=_params("arbitrary"),
        name="ctx_proj",
    )(x, g, sh, sc, w)


def _pair_lg(dec_ref, d, p, shape):
    lane = lax.broadcasted_iota(jnp.int32, shape, 1)
    first = (lane % LANES) < 64
    raw = jnp.where(first, jnp.full(shape, dec_ref[d, 2 * p], F32), jnp.full(shape, dec_ref[d, 2 * p + 1], F32))
    return -jnp.exp(raw)


def _head_block_mask(shape):
    r = lax.broadcasted_iota(jnp.int32, shape, 0)
    c = lax.broadcasted_iota(jnp.int32, shape, 1)
    return (r // 64) == (c // LANES)


def _kv_pair(k_pair, v_pair, w):
    kw = (k_pair.astype(F32) * w).astype(BF16)
    kv = lax.dot_general(kw, v_pair, (((0,), (0,)), ((), ())), preferred_element_type=F32)
    return jnp.where(_head_block_mask(kv.shape), kv, 0.0)


def _row_decay(dec_ref, d, p):
    shape = (LANES, 2 * RET_DV)
    rowh = lax.broadcasted_iota(jnp.int32, shape, 0) < 64
    raw = jnp.where(rowh, jnp.full(shape, dec_ref[d, 2 * p], F32), jnp.full(shape, dec_ref[d, 2 * p + 1], F32))
    return jnp.exp(-jnp.exp(raw) * float(RET_CHUNK))


def _compact_state(s):
    row = lax.broadcasted_iota(jnp.int32, (LANES, RET_DV), 0)
    return jnp.where(row < 64, s[:, :RET_DV], s[:, RET_DV:])


def _expand_state(c):
    row = lax.broadcasted_iota(jnp.int32, c.shape, 0)
    z = jnp.zeros_like(c)
    return jnp.concatenate([jnp.where(row < 64, c, z), jnp.where(row < 64, z, c)], axis=1)


def _ret_bwd_kernel(dec_ref, k_ref, v_ref, ck_ref, cv_ref, sb_ref, sbs):
    i = pl.program_id(0)
    C = RET_CHUNK
    lc = ck_ref.shape[0]

    @pl.when(i == 0)
    def _():
        pos = lax.broadcasted_iota(jnp.int32, (lc, LANES), 0).astype(F32)
        for p in range(RET_PAIRS):
            ks = slice(p * LANES, (p + 1) * LANES)
            vs = slice(p * 2 * RET_DV, (p + 1) * 2 * RET_DV)
            wb = jnp.exp(_pair_lg(dec_ref, 1, p, (lc, LANES)) * pos)
            sbs[p] = _kv_pair(ck_ref[:, ks], cv_ref[:, vs], wb)

    pos = lax.broadcasted_iota(jnp.int32, (C, LANES), 0).astype(F32)
    for p in range(RET_PAIRS):
        ks = slice(p * LANES, (p + 1) * LANES)
        vs = slice(p * 2 * RET_DV, (p + 1) * 2 * RET_DV)
        wb = jnp.exp(_pair_lg(dec_ref, 1, p, (C, LANES)) * pos)
        gb = _row_decay(dec_ref, 1, p)
        sb = sbs[p]
        for cc in reversed(range(RET_STEP_CHUNKS)):
            rs = slice(cc * C, (cc + 1) * C)
            sb_ref[cc, p] = _compact_state(sb).astype(BF16)
            sb = gb * sb + _kv_pair(k_ref[rs, ks], v_ref[rs, vs], wb)
        sbs[p] = sb


def _ret_bwd_states(dec, proj, cproj):
    L = proj.shape[0]
    lc = cproj.shape[0]
    S = RET_STEP_CHUNKS
    R = S * RET_CHUNK
    n = L // R
    return pl.pallas_call(
        _ret_bwd_kernel,
        grid=(n,),
        in_specs=[pl.BlockSpec(memory_space=pltpu.SMEM),
                  pl.BlockSpec((R, 512), lambda i: (n - 1 - i, 1)),
                  pl.BlockSpec((R, 1024), lambda i: (n - 1 - i, 1)),
                  pl.BlockSpec((lc, 512), lambda i: (0, 1)),
                  pl.BlockSpec((lc, 1024), lambda i: (0, 1))],
        out_specs=pl.BlockSpec((S, RET_PAIRS, LANES, RET_DV), lambda i: (n - 1 - i, 0, 0, 0)),
        out_shape=jax.ShapeDtypeStruct((n * S, RET_PAIRS, LANES, RET_DV), BF16),
        scratch_shapes=[pltpu.VMEM((RET_PAIRS, LANES, 2 * RET_DV), F32)],
        compiler_params=_params("arbitrary"),
        name="ret_bwd",
    )(dec, proj, proj, cproj, cproj)


def _ret_out_kernel(dec_ref, q_ref, k_ref, v_ref, g_ref, sb_ref, ck_ref, cv_ref, o_ref, sfs):
    i = pl.program_id(0)
    C = RET_CHUNK
    lc = ck_ref.shape[0]

    @pl.when(i == 0)
    def _():
        cpos = lax.broadcasted_iota(jnp.int32, (lc, LANES), 0).astype(F32)
        for p in range(RET_PAIRS):
            ks = slice(p * LANES, (p + 1) * LANES)
            vs = slice(p * 2 * RET_DV, (p + 1) * 2 * RET_DV)
            wf = jnp.exp(_pair_lg(dec_ref, 0, p, (lc, LANES)) * (lc - 1.0 - cpos))
            sfs[p] = _kv_pair(ck_ref[:, ks], cv_ref[:, vs], wf)

    pos = lax.broadcasted_iota(jnp.int32, (C, LANES), 0).astype(F32)
    n_i = lax.broadcasted_iota(jnp.int32, (C, 2 * C), 0)
    m_i = lax.broadcasted_iota(jnp.int32, (C, 2 * C), 1) % C
    rel = (n_i - m_i).astype(F32)
    lane = lax.broadcasted_iota(jnp.int32, (C, LANES), 1)
    lo = lane < 64
    for p in range(RET_PAIRS):
        ks = slice(p * LANES, (p + 1) * LANES)
        vs = slice(p * 2 * RET_DV, (p + 1) * 2 * RET_DV)
        col_a = lax.broadcasted_iota(jnp.int32, (C, 2 * C), 1) < C
        raw_f = jnp.where(col_a, jnp.full((C, 2 * C), dec_ref[0, 2 * p], F32), jnp.full((C, 2 * C), dec_ref[0, 2 * p + 1], F32))
        raw_b = jnp.where(col_a, jnp.full((C, 2 * C), dec_ref[1, 2 * p], F32), jnp.full((C, 2 * C), dec_ref[1, 2 * p + 1], F32))
        dmat = jnp.where(rel >= 0, jnp.exp(-jnp.exp(raw_f) * jnp.maximum(rel, 0.0)),
                         jnp.exp(-jnp.exp(raw_b) * jnp.maximum(-rel, 0.0)))
        lg_f = _pair_lg(dec_ref, 0, p, (C, LANES))
        wqf = jnp.exp(lg_f * (pos + 1.0))
        wqb = jnp.exp(_pair_lg(dec_ref, 1, p, (C, LANES)) * (float(C) - pos))
        wkf = jnp.exp(lg_f * (C - 1.0 - pos))
        gf = _row_decay(dec_ref, 0, p)
        sf = sfs[p]
        for cc in range(RET_STEP_CHUNKS):
            rs = slice(cc * C, (cc + 1) * C)
            q = q_ref[rs, ks]
            k = k_ref[rs, ks]
            v = v_ref[rs, vs]
            zk = jnp.zeros_like(k)
            kst = jnp.concatenate([jnp.where(lo, k, zk), jnp.where(lo, zk, k)], axis=0)
            s = lax.dot_general(q, kst, (((1,), (1,)), ((), ())), preferred_element_type=F32)
            sd = (s * dmat).astype(BF16)
            qf32 = q.astype(F32)
            qwf = (qf32 * wqf).astype(BF16)
            qwb = (qf32 * wqb).astype(BF16)
            zv = jnp.zeros((C, RET_DV), BF16)
            vbd = jnp.concatenate([jnp.concatenate([v[:, :RET_DV], zv], axis=1),
                                   jnp.concatenate([zv, v[:, RET_DV:]], axis=1)], axis=0)
            lhs = jnp.concatenate([sd, qwf, qwb], axis=1)
            rhs = jnp.concatenate([vbd, sf.astype(BF16), _expand_state(sb_ref[cc, p])], axis=0)
            o = jnp.dot(lhs, rhs, preferred_element_type=F32)
            sf = gf * sf + _kv_pair(k, v, wkf)
            for t in range(2):
                oh = o[:, t * RET_DV:(t + 1) * RET_DV]
                oh = oh * lax.rsqrt(jnp.mean(oh * oh, axis=-1, keepdims=True) + NORM_EPS)
                cs = slice(p * 2 * RET_DV + t * RET_DV, p * 2 * RET_DV + (t + 1) * RET_DV)
                gt = g_ref[rs, cs].astype(F32)
                o_ref[rs, cs] = (oh * (gt / (1.0 + jnp.exp(-gt)))).astype(BF16)
        sfs[p] = sf


def _ret_out(dec, proj, sb, cproj):
    L = proj.shape[0]
    lc = cproj.shape[0]
    S = RET_STEP_CHUNKS
    R = S * RET_CHUNK
    n = L // R
    return pl.pallas_call(
        _ret_out_kernel,
        grid=(n,),
        in_specs=[pl.BlockSpec(memory_space=pltpu.SMEM),
                  pl.BlockSpec((R, 512), lambda i: (i, 0)),
                  pl.BlockSpec((R, 512), lambda i: (i, 1)),
                  pl.BlockSpec((R, 1024), lambda i: (i, 1)),
                  pl.BlockSpec((R, 1024), lambda i: (i, 2)),
                  pl.BlockSpec((S, RET_PAIRS, LANES, RET_DV), lambda i: (i, 0, 0, 0)),
                  pl.BlockSpec((lc, 512), lambda i: (0, 1)),
                  pl.BlockSpec((lc, 1024), lambda i: (0, 1))],
        out_specs=pl.BlockSpec((R, RET_HEADS * RET_DV), lambda i: (i, 0)),
        out_shape=jax.ShapeDtypeStruct((L, RET_HEADS * RET_DV), BF16),
        scratch_shapes=[pltpu.VMEM((RET_PAIRS, LANES, 2 * RET_DV), F32)],
        compiler_params=_params("arbitrary"),
        name="ret_out",
    )(dec, proj, proj, proj, proj, sb, cproj, cproj)


def _attn_kernel(sink_ref, q_ref, kp_ref, kc_ref, kn_ref, vp_ref, vc_ref, vn_ref, ck_ref, cv_ref, o_ref):
    n = pl.program_id(0)
    nstep = pl.num_programs(0)
    B = ATT_BLOCK
    SB = ATT_STEP_BLOCKS
    kj = lax.broadcasted_iota(jnp.int32, (B, B), 0)
    qi = lax.broadcasted_iota(jnp.int32, (B, B), 1)
    ok_prev = jnp.where(n > 0, 0.0, MASK_NEG).astype(F32)
    ok_next = jnp.where(n < nstep - 1, 0.0, MASK_NEG).astype(F32)

    def band(inside, ok):
        return jnp.concatenate([jnp.where(inside, ok, MASK_NEG).astype(F32)] * ATT_GROUP, axis=1)

    bias_prev = [band(kj >= qi, ok_prev if j == 0 else 0.0) for j in range(SB)]
    bias_next = [band(kj <= qi, ok_next if j == SB - 1 else 0.0) for j in range(SB)]
    lane = lax.broadcasted_iota(jnp.int32, (B, LANES), 1)
    lo = lane < 64
    hi = lane >= 64

    def keys_of(j, gs, prev_ref, cur_ref, next_ref, ctx_ref):
        prev = prev_ref[:, gs] if j == 0 else cur_ref[(j - 1) * B:j * B, gs]
        nxt = next_ref[:, gs] if j == SB - 1 else cur_ref[(j + 1) * B:(j + 2) * B, gs]
        return jnp.concatenate([prev, cur_ref[j * B:(j + 1) * B, gs], nxt, ctx_ref[:, gs]], axis=0)

    def scores(j, g):
        gs = slice(g * LANES, (g + 1) * LANES)
        kcat = keys_of(j, gs, kp_ref, kc_ref, kn_ref, ck_ref)
        qs = []
        for r in range(ATT_GROUP):
            h = ATT_GROUP * g + r
            qt = q_ref[j * B:(j + 1) * B, (h // 2) * LANES:(h // 2 + 1) * LANES]
            keep = lo if h % 2 == 0 else hi
            qs.append(jnp.where(keep, qt, jnp.zeros_like(qt)))
        q4 = jnp.concatenate(qs, axis=0)
        return lax.dot_general(kcat, q4, (((1,), (1,)), ((), ())), preferred_element_type=F32)

    def softmax(j, g, s):
        sk = jnp.concatenate([jnp.full((1, B), sink_ref[ATT_GROUP * g + r], F32)
                              for r in range(ATT_GROUP)], axis=1) * LOG2E
        s = jnp.concatenate([s[:B] + bias_prev[j], s[B:2 * B], s[2 * B:3 * B] + bias_next[j], s[3 * B:]], axis=0)
        m = jnp.maximum(jnp.max(s, axis=0, keepdims=True), sk)
        e = jnp.exp2(s - m)
        den = jnp.sum(e, axis=0, keepdims=True) + jnp.exp2(sk - m)
        return e.astype(BF16), den

    def values(j, g, e, den):
        gs = slice(g * LANES, (g + 1) * LANES)
        vcat = keys_of(j, gs, vp_ref, vc_ref, vn_ref, cv_ref)
        res = lax.dot_general(vcat, e, (((0,), (0,)), ((), ())), preferred_element_type=F32) * (1.0 / den)
        for t in range(2):
            even = res[:, (2 * t) * B:(2 * t + 1) * B].T
            odd = res[:, (2 * t + 1) * B:(2 * t + 2) * B].T
            c0 = (2 * g + t) * LANES
            o_ref[j * B:(j + 1) * B, c0:c0 + LANES] = jnp.where(lo, even, odd).astype(BF16)

    units = [(j, g) for j in range(SB) for g in range(ATT_KV_HEADS)]
    s_next = scores(*units[0])
    pending = None
    for u, unit in enumerate(units):
        s_cur = s_next
        if u + 1 < len(units):
            s_next = scores(*units[u + 1])
        e_den = softmax(*unit, s_cur)
        if pending is not None:
            values(*units[u - 1], *pending)
        pending = e_den
    values(*units[-1], *pending)


def _attn(sink, proj, kd, vd, ckd, cvd, riders):
    L = proj.shape[0]
    B = ATT_BLOCK
    SB = ATT_STEP_BLOCKS
    n = L // (SB * B)
    nb = L // B
    lc = ckd.shape[0]
    prev = pl.BlockSpec((B, 512), lambda i: (jnp.maximum(i * SB - 1, 0), 0))
    cur = pl.BlockSpec((SB * B, 512), lambda i: (i, 0))
    nxt = pl.BlockSpec((B, 512), lambda i: (jnp.minimum((i + 1) * SB, nb - 1), 0))
    full = pl.BlockSpec((lc, 512), lambda i: (0, 0))
    rid_in_specs, rid_out_specs, rid_shapes = _rider_specs(riders, n)
    return pl.pallas_call(
        _with_cast_riders(_attn_kernel, 10, 1, len(riders)),
        grid=(n,),
        in_specs=[pl.BlockSpec(memory_space=pltpu.SMEM),
                  pl.BlockSpec((SB * B, 1024), lambda i: (i, 3)),
                  prev, cur, nxt, prev, cur, nxt, full, full] + rid_in_specs,
        out_specs=[pl.BlockSpec((SB * B, ATT_HEADS * ATT_DH), lambda i: (i, 0))] + rid_out_specs,
        out_shape=[jax.ShapeDtypeStruct((L, ATT_HEADS * ATT_DH), BF16)] + rid_shapes,
        compiler_params=_params("parallel"),
        name="attn",
    )(sink, proj, kd, kd, kd, vd, vd, vd, ckd, cvd, *[r[0] for r in riders])


def _out_proj_kernel(yr_ref, ya_ref, w_ref, x_ref, gt_ref, g_ref, sh_ref, sc_ref, o_ref, h_ref):
    kr = yr_ref.shape[1]
    for r in range(yr_ref.shape[0] // OUT_ROW_CHUNK):
        rs = slice(r * OUT_ROW_CHUNK, (r + 1) * OUT_ROW_CHUNK)
        acc = jnp.dot(yr_ref[rs, :], w_ref[:kr, :], preferred_element_type=F32)
        acc = acc + jnp.dot(ya_ref[rs, :], w_ref[kr:, :], preferred_element_type=F32)
        x1 = x_ref[rs, :] + gt_ref[...] * acc
        o_ref[rs, :] = x1
        y = x1 * lax.rsqrt(jnp.mean(x1 * x1, axis=-1, keepdims=True) + NORM_EPS)
        y = y * g_ref[...]
        h_ref[rs, :] = (y * (1.0 + sc_ref[...]) + sh_ref[...]).astype(BF16)


def _out_proj(yr, ya, w, x, gt, g, sh, sc, *, tm):
    m, d = x.shape
    kr, ka = yr.shape[1], ya.shape[1]
    row = lambda i: (i, 0)
    vec = pl.BlockSpec((1, d), lambda i: (0, 0))
    return pl.pallas_call(
        _out_proj_kernel,
        grid=(m // tm,),
        in_specs=[pl.BlockSpec((tm, kr), row), pl.BlockSpec((tm, ka), row),
                  pl.BlockSpec((kr + ka, d), lambda i: (0, 0)),
                  pl.BlockSpec((tm, d), row), vec, vec, vec, vec],
        out_specs=[pl.BlockSpec((tm, d), row), pl.BlockSpec((tm, d), row)],
        out_shape=[jax.ShapeDtypeStruct((m, d), F32), jax.ShapeDtypeStruct((m, d), BF16)],
        compiler_params=_params("parallel"),
        name="out_proj",
    )(yr, ya, w, x, gt, g, sh, sc)


def _ffn_kernel(h_ref, gt_ref, gfin_ref, wg_ref, wu_ref, wd_ref, x_hbm, o_ref, x_buf, sem):
    i = pl.program_id(0)
    f = pl.program_id(1)
    last = pl.num_programs(1) - 1
    rows = o_ref.shape[0]
    x_copy = pltpu.make_async_copy(x_hbm.at[pl.ds(pl.multiple_of(i * rows, rows), rows), :], x_buf, sem.at[0])

    def step(first, final):
        wd = wd_ref[...].astype(BF16)
        for r in range(rows // FFN_ROW_CHUNK):
            rs = slice(r * FFN_ROW_CHUNK, (r + 1) * FFN_ROW_CHUNK)
            h = h_ref[rs, :]
            a = jnp.dot(h, wg_ref[0], preferred_element_type=F32)
            u = jnp.dot(h, wu_ref[0], preferred_element_type=F32)
            act = ((a / (1.0 + jnp.exp(-a))) * u).astype(BF16)
            part = jnp.dot(act, wd, preferred_element_type=F32)
            if first:
                o_ref[rs, :] = part
            elif not final:
                o_ref[rs, :] += part
            else:
                y = x_buf[rs, :] + gt_ref[...] * (o_ref[rs, :] + part)
                y = y * lax.rsqrt(jnp.mean(y * y, axis=-1, keepdims=True) + NORM_EPS)
                o_ref[rs, :] = y * gfin_ref[...]

    @pl.when(f == 0)
    def _():
        x_copy.start()
        step(first=True, final=False)

    @pl.when((f > 0) & (f < last))
    def _():
        step(first=False, final=False)

    @pl.when(f == last)
    def _():
        x_copy.wait()
        step(first=False, final=True)


def _ffn(h, x, gt, gfin, wg, wu, wd, *, tm):
    m, d = x.shape
    nf = wg.shape[0]
    assert wg.shape == wu.shape == (nf, d, FFN_TILE) and wd.shape == (nf * FFN_TILE, d)
    assert m % tm == 0 and tm % FFN_ROW_CHUNK == 0
    row = lambda i, f: (i, 0)
    vec = pl.BlockSpec((1, d), lambda i, f: (0, 0))
    wcol = pl.BlockSpec((1, d, FFN_TILE), lambda i, f: (f, 0, 0))
    return pl.pallas_call(
        _ffn_kernel,
        grid=(m // tm, nf),
        in_specs=[pl.BlockSpec((tm, d), row), vec, vec, wcol, wcol,
                  pl.BlockSpec((FFN_TILE, d), lambda i, f: (f, 0)),
                  pl.BlockSpec(memory_space=pl.ANY)],
        out_specs=pl.BlockSpec((tm, d), row),
        out_shape=jax.ShapeDtypeStruct((m, d), F32),
        scratch_shapes=[pltpu.VMEM((tm, d), F32), pltpu.SemaphoreType.DMA((1,))],
        compiler_params=_params("arbitrary", "arbitrary"),
        name="ffn",
    )(h, gt, gfin, wg, wu, wd, x)


def _rope_tables(L):
    f32 = np.float32
    lane = np.arange(LANES)
    inv1 = f32(ROPE_BASE) ** (-np.arange(32, dtype=f32) / f32(32))
    ang1 = np.arange(L, dtype=f32)[:, None] * inv1[None, :]
    sgn1 = np.where((lane % 64) < 32, -1.0, 1.0).astype(f32)
    cos1 = np.tile(np.cos(ang1), (1, LANES // 32))
    sin1 = np.tile(np.sin(ang1), (1, LANES // 32)) * sgn1[None, :]
    inv2 = f32(ROPE_BASE) ** (-np.arange(16, dtype=f32) / f32(16))
    nrow = L // GRID_W
    ang_r = np.arange(nrow, dtype=f32)[:, None] * inv2[None, :]
    ang_c = np.arange(GRID_W, dtype=f32)[:, None] * inv2[None, :]
    sgna = np.where((lane % 32) < 16, -1.0, 1.0).astype(f32)

    def expand(fr, fc):
        by_row = np.broadcast_to(np.tile(fr, (1, 2))[:, None, :], (nrow, GRID_W, 32))
        by_col = np.broadcast_to(np.tile(fc, (1, 2))[None, :, :], (nrow, GRID_W, 32))
        head = np.concatenate([by_row, by_col], axis=-1).reshape(L, 64)
        return np.tile(head, (1, LANES // 64))

    cosa = expand(np.cos(ang_r), np.cos(ang_c))
    sina = expand(np.sin(ang_r), np.sin(ang_c)) * sgna[None, :]
    return tuple(np.ascontiguousarray(t, dtype=f32) for t in (cos1, sin1, cosa, sina))


def kernel(x, c, ctx, c_ctx, w_mod, b_mod, norm_mix, norm_ffn, w_in, ret_decay, attn_sink,
           w_out, w_gate, w_up, w_down, norm_final):
    B, L, D = x.shape
    assert B == 1 and w_mod.shape[0] == 1, "single batch element, depth-1 layer"
    x2 = x[0]
    xc2 = ctx[0]

    cv = jnp.zeros((8, D), F32).at[0].set(c[0]).at[1].set(c_ctx)
    mod = _mod(cv, w_mod[0], b_mod[0][None, :], 2 * D)
    sh_m, sc_m = mod[0:1, 0:D], mod[0:1, D:2 * D]
    sh_mc, sc_mc = mod[1:2, 0:D], mod[1:2, D:2 * D]

    g_mix = norm_mix[0][None, :]
    cproj, ckd, cvd, w_in_b = _ctx_proj(xc2, g_mix, sh_mc, sc_mc, w_in[0])
    proj, kd, vd, mod_rest = _in_proj(x2, g_mix, sh_m, sc_m, w_in_b, _rope_tables(L),
                                      c[0][:, None], w_mod[0], b_mod[0][None, :], 2 * D, tm=512)
    gt_m, sh_f, sc_f, gt_f = [mod_rest.reshape(1, 4 * D)[:, k * D:(k + 1) * D] for k in range(4)]

    dec = ret_decay[0].astype(F32)
    sb = _ret_bwd_states(dec, proj, cproj)
    y_ret = _ret_out(dec, proj, sb, cproj)
    y_att, w_gate_b, w_up_b, w_out_b = _attn(
        attn_sink[0].astype(F32), proj, kd, vd, ckd, cvd,
        [(w_gate[0], 1, FFN_TILE), (w_up[0], 1, FFN_TILE), (w_out[0], 1, None)])

    x1, hff = _out_proj(y_ret, y_att, w_out_b, x2, gt_m, norm_ffn[0][None, :], sh_f, sc_f, tm=512)
    out = _ffn(hff, x1, gt_f, norm_final[None, :], w_gate_b, w_up_b, w_down[0], tm=1024)
    return out[None]
```

```python
import jax
import jax.numpy as jnp
import numpy as np
from jax import lax
from jax.experimental import pallas as pl
from jax.experimental.pallas import tpu as pltpu

GRID_W = 64
RET_HEADS = 8
RET_DK = 64
RET_DV = 128
RET_CHUNK = 128
ATT_HEADS = 16
ATT_KV_HEADS = 4
ATT_DH = 64
ATT_GROUP = ATT_HEADS // ATT_KV_HEADS
WINDOW = 128
ATT_BLOCK = 128
ROPE_BASE = 10000.0
NORM_EPS = 1e-6
K_SCALE = RET_DK ** -0.5
ATT_SCALE = ATT_DH ** -0.5
LOG2E = 1.4426950408889634

LANES = 128
RET_PAIRS = RET_HEADS // 2
MASK_NEG = -1e30
VMEM_LIMIT = 56 * 1024 * 1024
CAST_PIECE_ROWS = 16
RET_STEP_CHUNKS = 8
ATT_STEP_BLOCKS = 4
OUT_ROW_CHUNK = 512
IN_ROW_CHUNK = 512
FFN_TILE = 512
FFN_ROW_CHUNK = 1024

BF16 = jnp.bfloat16
F32 = jnp.float32


def _params(*sem):
    return pltpu.CompilerParams(dimension_semantics=sem, vmem_limit_bytes=VMEM_LIMIT)


def _with_cast_riders(body, n_in, n_out, n_rid):
    def wrapped(*refs):
        ins = refs[:n_in]
        rid_in = refs[n_in:n_in + n_rid]
        outs = refs[n_in + n_rid:n_in + n_rid + n_out]
        rid_out = refs[n_in + n_rid + n_out:n_in + 2 * n_rid + n_out]
        scratch = refs[n_in + 2 * n_rid + n_out:]

        def piece(src, dst, r0):
            rs = slice(r0, r0 + CAST_PIECE_ROWS)
            if len(dst.shape) == 2:
                dst[rs, :] = src[rs, :].astype(BF16)
            else:
                tc = dst.shape[2]
                for t in range(dst.shape[0]):
                    dst[t, rs, :] = src[rs, t * tc:(t + 1) * tc].astype(BF16)

        pieces = [(lambda s=src, d=dst, r=r0: piece(s, d, r))
                  for src, dst in zip(rid_in, rid_out) for r0 in range(0, src.shape[0], CAST_PIECE_ROWS)]
        done = []
        fillers = [(lambda p=p: (done.append(1), p())) for p in pieces]
        body(*ins, *outs, *scratch, fillers=fillers)
        assert len(done) == len(pieces), "every cast piece must be emitted exactly once"
    return wrapped


def _rider_specs(riders, steps):
    in_specs, out_specs, shapes = [], [], []
    for w, ncb, tile in riders:
        rows, cols = w.shape
        nrb = steps // ncb
        assert nrb * ncb == steps and rows % nrb == 0 and cols % ncb == 0
        br, bc = rows // nrb, cols // ncb
        assert br % 16 == 0 and bc % LANES == 0, "slab must be bf16-tile aligned"
        in_specs.append(pl.BlockSpec((br, bc), lambda i, ncb=ncb: (i // ncb, i % ncb)))
        if tile is None:
            out_specs.append(in_specs[-1])
            shapes.append(jax.ShapeDtypeStruct(w.shape, BF16))
        else:
            assert ncb == 1 and cols % tile == 0 and tile % LANES == 0
            out_specs.append(pl.BlockSpec((cols // tile, br, tile), lambda i: (0, i, 0)))
            shapes.append(jax.ShapeDtypeStruct((cols // tile, rows, tile), BF16))
    return in_specs, out_specs, shapes


def _mod_kernel(cv_ref, w_ref, b_ref, o_ref):
    cv = cv_ref[...]
    s = cv / (1.0 + jnp.exp(-cv))
    o_ref[...] = jnp.dot(s.astype(BF16), w_ref[...].astype(BF16),
                         preferred_element_type=F32) + b_ref[...]


def _mod(cv, w, b, n):
    d = w.shape[0]
    tn = 1024
    assert n % tn == 0
    return pl.pallas_call(
        _mod_kernel,
        grid=(n // tn,),
        in_specs=[pl.BlockSpec((8, d), lambda j: (0, 0)),
                  pl.BlockSpec((d, tn), lambda j: (0, j)),
                  pl.BlockSpec((1, tn), lambda j: (0, j))],
        out_specs=pl.BlockSpec((8, tn), lambda j: (0, j)),
        out_shape=jax.ShapeDtypeStruct((8, n), F32),
        compiler_params=_params("parallel"),
        name="mod",
    )(cv, w, b)


def _rot_pairs(a, cos, sin_signed, half):
    lane = lax.broadcasted_iota(jnp.int32, a.shape, 1)
    first = (lane % (2 * half)) < half
    rot = jnp.where(first, pltpu.roll(a, LANES - half, 1), pltpu.roll(a, half, 1))
    return a * cos + rot * sin_signed


def _dup_halves(a):
    lane = lax.broadcasted_iota(jnp.int32, a.shape, 1)
    r = pltpu.roll(a, 64, 1)
    lo = lane < 64
    return jnp.where(lo, a, r), jnp.where(lo, r, a)


_PROJ_TILE = 512
_PROJ_TILE_KINDS = ("ret_q", "ret_k", "plain", "plain", "plain", "plain", "att_q", "att_q", "att_kv")


def _in_proj_kernel(x_ref, g_ref, sh_ref, sc_ref, w_ref, c1_ref, s1_ref, ca_ref, sa_ref,
                    cc_ref, wm_ref, bm_ref, o_ref, kd_ref, vd_ref, mod_ref):
    def mod_rider():
        cc = cc_ref[...]
        s_col = cc / (1.0 + jnp.exp(-cc))
        mod_ref[0] = jnp.sum(wm_ref[...] * s_col, axis=0, keepdims=True) + bm_ref[...]

    tn = _PROJ_TILE
    for r in range(x_ref.shape[0] // IN_ROW_CHUNK):
        rs = slice(r * IN_ROW_CHUNK, (r + 1) * IN_ROW_CHUNK)
        xf = x_ref[rs, :]
        y = xf * lax.rsqrt(jnp.mean(xf * xf, axis=-1, keepdims=True) + NORM_EPS)
        y = y * g_ref[...]
        h = (y * (1.0 + sc_ref[...]) + sh_ref[...]).astype(BF16)

        def rope1(a):
            return _rot_pairs(a, c1_ref[rs, :], s1_ref[rs, :], 32)

        def ropea(a):
            return _rot_pairs(a, ca_ref[rs, :], sa_ref[rs, :], 16)

        order = sorted(range(len(_PROJ_TILE_KINDS)), key=lambda t: _PROJ_TILE_KINDS[t] == "plain")
        for j in order:
            kind = _PROJ_TILE_KINDS[j]
            acc = jnp.dot(h, w_ref[:, j * tn:(j + 1) * tn], preferred_element_type=F32)
            if r == 0 and j == order[-2]:
                mod_rider()
            for c in range(tn // LANES):
                a = acc[:, c * LANES:(c + 1) * LANES]
                if kind == "ret_q":
                    a = rope1(a)
                elif kind == "ret_k":
                    a = rope1(a) * K_SCALE
                elif kind == "att_q":
                    a = ropea(a) * (ATT_SCALE * LOG2E)
                elif kind == "att_kv" and c < 2:
                    a = ropea(a)
                o_ref[rs, j * tn + c * LANES:j * tn + (c + 1) * LANES] = a.astype(BF16)
                if kind == "att_kv":
                    dup_ref = kd_ref if c < 2 else vd_ref
                    d0, d1 = _dup_halves(a)
                    t = 2 * (c % 2)
                    dup_ref[rs, t * LANES:(t + 1) * LANES] = d0.astype(BF16)
                    dup_ref[rs, (t + 1) * LANES:(t + 2) * LANES] = d1.astype(BF16)


def _in_proj(x, g, sh, sc, w, tabs, c_col, w_mod, b_mod, mod_done, *, tm):
    m, d = x.shape
    n = w.shape[1]
    assert n == _PROJ_TILE * len(_PROJ_TILE_KINDS) and m % tm == 0 and tm % IN_ROW_CHUNK == 0
    steps = m // tm
    slab = (w_mod.shape[1] - mod_done) // steps
    assert slab * steps == w_mod.shape[1] - mod_done and slab % LANES == 0 and mod_done % slab == 0
    slab0 = mod_done // slab
    c1, s1, ca, sa = tabs
    row = lambda i: (i, 0)
    vec = pl.BlockSpec((1, d), lambda i: (0, 0))
    tab = pl.BlockSpec((tm, LANES), row)
    return pl.pallas_call(
        _in_proj_kernel,
        grid=(m // tm,),
        in_specs=[pl.BlockSpec((tm, d), row), vec, vec, vec,
                  pl.BlockSpec((d, n), lambda i: (0, 0), pipeline_mode=pl.Buffered(1)),
                  tab, tab, tab, tab,
                  pl.BlockSpec((d, 1), lambda i: (0, 0)),
                  pl.BlockSpec((d, slab), lambda i: (0, slab0 + i)),
                  pl.BlockSpec((1, slab), lambda i: (0, slab0 + i))],
        out_specs=[pl.BlockSpec((tm, n), row),
                   pl.BlockSpec((tm, 512), row),
                   pl.BlockSpec((tm, 512), row),
                   pl.BlockSpec((1, 1, slab), lambda i: (i, 0, 0))],
        out_shape=[jax.ShapeDtypeStruct((m, n), BF16),
                   jax.ShapeDtypeStruct((m, 512), BF16),
                   jax.ShapeDtypeStruct((m, 512), BF16),
                   jax.ShapeDtypeStruct((steps, 1, slab), F32)],
        compiler_params=_params("parallel"),
        name="in_proj",
    )(x, g, sh, sc, w, c1, s1, ca, sa, c_col, w_mod, b_mod)


def _ctx_proj_kernel(x_ref, g_ref, sh_ref, sc_ref, w_ref, o_ref, kd_ref, vd_ref, wb_ref, h_ref):
    j = pl.program_id(0)

    @pl.when(j == 0)
    def _():
        xf = x_ref[...]
        y = xf * lax.rsqrt(jnp.mean(xf * xf, axis=-1, keepdims=True) + NORM_EPS)
        y = y * g_ref[...]
        h_ref[...] = (y * (1.0 + sc_ref[...]) + sh_ref[...]).astype(BF16)

    wb = w_ref[...].astype(BF16)
    wb_ref[...] = wb
    acc = jnp.dot(h_ref[...], wb, preferred_element_type=F32)
    is_ret_k = _PROJ_TILE_KINDS.index("ret_k")
    o_ref[...] = (acc * jnp.where(j == is_ret_k, K_SCALE, 1.0)).astype(BF16)

    @pl.when(j == _PROJ_TILE_KINDS.index("att_kv"))
    def _():
        for c in range(_PROJ_TILE // LANES):
            dup_ref = kd_ref if c < 2 else vd_ref
            d0, d1 = _dup_halves(acc[:, c * LANES:(c + 1) * LANES])
            t = 2 * (c % 2)
            dup_ref[:, t * LANES:(t + 1) * LANES] = d0.astype(BF16)
            dup_ref[:, (t + 1) * LANES:(t + 2) * LANES] = d1.astype(BF16)


def _ctx_proj(x, g, sh, sc, w):
    m, d = x.shape
    n = w.shape[1]
    tn = _PROJ_TILE
    assert n == tn * len(_PROJ_TILE_KINDS)
    fixed = lambda j: (0, 0)
    vec = pl.BlockSpec((1, d), fixed)
    return pl.pallas_call(
        _ctx_proj_kernel,
        grid=(n // tn,),
        in_specs=[pl.BlockSpec((m, d), fixed), vec, vec, vec,
                  pl.BlockSpec((d, tn), lambda j: (0, j))],
        out_specs=[pl.BlockSpec((m, tn), lambda j: (0, j)),
                   pl.BlockSpec((m, 512), fixed),
                   pl.BlockSpec((m, 512), fixed),
                   pl.BlockSpec((d, tn), lambda j: (0, j))],
        out_shape=[jax.ShapeDtypeStruct((m, n), BF16),
                   jax.ShapeDtypeStruct((m, 512), BF16),
                   jax.ShapeDtypeStruct((m, 512), BF16),
                   jax.ShapeDtypeStruct((d, n), BF16)],
        scratch_shapes=[pltpu.VMEM((m, d), BF16)],
        compiler_params=_params("arbitrary"),
        name="ctx_proj",
    )(x, g, sh, sc, w)


def _pair_lg(dec_ref, d, p, shape):
    lane = lax.broadcasted_iota(jnp.int32, shape, 1)
    first = (lane % LANES) < 64
    raw = jnp.where(first, jnp.full(shape, dec_ref[d, 2 * p], F32), jnp.full(shape, dec_ref[d, 2 * p + 1], F32))
    return -jnp.exp(raw)


def _head_block_mask(shape):
    r = lax.broadcasted_iota(jnp.int32, shape, 0)
    c = lax.broadcasted_iota(jnp.int32, shape, 1)
    return (r // 64) == (c // LANES)


def _kv_pair(k_pair, v_pair, w):
    kw = (k_pair.astype(F32) * w).astype(BF16)
    kv = lax.dot_general(kw, v_pair, (((0,), (0,)), ((), ())), preferred_element_type=F32)
    return jnp.where(_head_block_mask(kv.shape), kv, 0.0)


def _row_decay(dec_ref, d, p):
    shape = (LANES, 2 * RET_DV)
    rowh = lax.broadcasted_iota(jnp.int32, shape, 0) < 64
    raw = jnp.where(rowh, jnp.full(shape, dec_ref[d, 2 * p], F32), jnp.full(shape, dec_ref[d, 2 * p + 1], F32))
    return jnp.exp(-jnp.exp(raw) * float(RET_CHUNK))


def _compact_state(s):
    row = lax.broadcasted_iota(jnp.int32, (LANES, RET_DV), 0)
    return jnp.where(row < 64, s[:, :RET_DV], s[:, RET_DV:])


def _expand_state(c):
    row = lax.broadcasted_iota(jnp.int32, c.shape, 0)
    z = jnp.zeros_like(c)
    return jnp.concatenate([jnp.where(row < 64, c, z), jnp.where(row < 64, z, c)], axis=1)


def _ret_bwd_kernel(dec_ref, k_ref, v_ref, ck_ref, cv_ref, sb_ref, sbs):
    i = pl.program_id(0)
    C = RET_CHUNK
    lc = ck_ref.shape[0]

    @pl.when(i == 0)
    def _():
        pos = lax.broadcasted_iota(jnp.int32, (lc, LANES), 0).astype(F32)
        for p in range(RET_PAIRS):
            ks = slice(p * LANES, (p + 1) * LANES)
            vs = slice(p * 2 * RET_DV, (p + 1) * 2 * RET_DV)
            wb = jnp.exp(_pair_lg(dec_ref, 1, p, (lc, LANES)) * pos)
            sbs[p] = _kv_pair(ck_ref[:, ks], cv_ref[:, vs], wb)

    pos = lax.broadcasted_iota(jnp.int32, (C, LANES), 0).astype(F32)
    for p in range(RET_PAIRS):
        ks = slice(p * LANES, (p + 1) * LANES)
        vs = slice(p * 2 * RET_DV, (p + 1) * 2 * RET_DV)
        wb = jnp.exp(_pair_lg(dec_ref, 1, p, (C, LANES)) * pos)
        gb = _row_decay(dec_ref, 1, p)
        sb = sbs[p]
        for cc in reversed(range(RET_STEP_CHUNKS)):
            rs = slice(cc * C, (cc + 1) * C)
            sb_ref[cc, p] = _compact_state(sb).astype(BF16)
            sb = gb * sb + _kv_pair(k_ref[rs, ks], v_ref[rs, vs], wb)
        sbs[p] = sb


def _ret_bwd_states(dec, proj, cproj):
    L = proj.shape[0]
    lc = cproj.shape[0]
    S = RET_STEP_CHUNKS
    R = S * RET_CHUNK
    n = L // R
    return pl.pallas_call(
        _ret_bwd_kernel,
        grid=(n,),
        in_specs=[pl.BlockSpec(memory_space=pltpu.SMEM),
                  pl.BlockSpec((R, 512), lambda i: (n - 1 - i, 1)),
                  pl.BlockSpec((R, 1024), lambda i: (n - 1 - i, 1)),
                  pl.BlockSpec((lc, 512), lambda i: (0, 1)),
                  pl.BlockSpec((lc, 1024), lambda i: (0, 1))],
        out_specs=pl.BlockSpec((S, RET_PAIRS, LANES, RET_DV), lambda i: (n - 1 - i, 0, 0, 0)),
        out_shape=jax.ShapeDtypeStruct((n * S, RET_PAIRS, LANES, RET_DV), BF16),
        scratch_shapes=[pltpu.VMEM((RET_PAIRS, LANES, 2 * RET_DV), F32)],
        compiler_params=_params("arbitrary"),
        name="ret_bwd",
    )(dec, proj, proj, cproj, cproj)


def _ret_out_kernel(dec_ref, q_ref, k_ref, v_ref, g_ref, sb_ref, ck_ref, cv_ref, o_ref, sfs):
    i = pl.program_id(0)
    C = RET_CHUNK
    lc = ck_ref.shape[0]

    @pl.when(i == 0)
    def _():
        cpos = lax.broadcasted_iota(jnp.int32, (lc, LANES), 0).astype(F32)
        for p in range(RET_PAIRS):
            ks = slice(p * LANES, (p + 1) * LANES)
            vs = slice(p * 2 * RET_DV, (p + 1) * 2 * RET_DV)
            wf = jnp.exp(_pair_lg(dec_ref, 0, p, (lc, LANES)) * (lc - 1.0 - cpos))
            sfs[p] = _kv_pair(ck_ref[:, ks], cv_ref[:, vs], wf)

    pos = lax.broadcasted_iota(jnp.int32, (C, LANES), 0).astype(F32)
    n_i = lax.broadcasted_iota(jnp.int32, (C, 2 * C), 0)
    m_i = lax.broadcasted_iota(jnp.int32, (C, 2 * C), 1) % C
    rel = (n_i - m_i).astype(F32)
    lane = lax.broadcasted_iota(jnp.int32, (C, LANES), 1)
    lo = lane < 64
    for p in range(RET_PAIRS):
        ks = slice(p * LANES, (p + 1) * LANES)
        vs = slice(p * 2 * RET_DV, (p + 1) * 2 * RET_DV)
        col_a = lax.broadcasted_iota(jnp.int32, (C, 2 * C), 1) < C
        raw_f = jnp.where(col_a, jnp.full((C, 2 * C), dec_ref[0, 2 * p], F32), jnp.full((C, 2 * C), dec_ref[0, 2 * p + 1], F32))
        raw_b = jnp.where(col_a, jnp.full((C, 2 * C), dec_ref[1, 2 * p], F32), jnp.full((C, 2 * C), dec_ref[1, 2 * p + 1], F32))
        dmat = jnp.where(rel >= 0, jnp.exp(-jnp.exp(raw_f) * jnp.maximum(rel, 0.0)),
                         jnp.exp(-jnp.exp(raw_b) * jnp.maximum(-rel, 0.0)))
        lg_f = _pair_lg(dec_ref, 0, p, (C, LANES))
        wqf = jnp.exp(lg_f * (pos + 1.0))
        wqb = jnp.exp(_pair_lg(dec_ref, 1, p, (C, LANES)) * (float(C) - pos))
        wkf = jnp.exp(lg_f * (C - 1.0 - pos))
        gf = _row_decay(dec_ref, 0, p)
        sf = sfs[p]
        for cc in range(RET_STEP_CHUNKS):
            rs = slice(cc * C, (cc + 1) * C)
            q = q_ref[rs, ks]
            k = k_ref[rs, ks]
            v = v_ref[rs, vs]
            zk = jnp.zeros_like(k)
            kst = jnp.concatenate([jnp.where(lo, k, zk), jnp.where(lo, zk, k)], axis=0)
            s = lax.dot_general(q, kst, (((1,), (1,)), ((), ())), preferred_element_type=F32)
            sd = (s * dmat).astype(BF16)
            qf32 = q.astype(F32)
            qwf = (qf32 * wqf).astype(BF16)
            qwb = (qf32 * wqb).astype(BF16)
            zv = jnp.zeros((C, RET_DV), BF16)
            vbd = jnp.concatenate([jnp.concatenate([v[:, :RET_DV], zv], axis=1),
                                   jnp.concatenate([zv, v[:, RET_DV:]], axis=1)], axis=0)
            lhs = jnp.concatenate([sd, qwf, qwb], axis=1)
            rhs = jnp.concatenate([vbd, sf.astype(BF16), _expand_state(sb_ref[cc, p])], axis=0)
            o = jnp.dot(lhs, rhs, preferred_element_type=F32)
            sf = gf * sf + _kv_pair(k, v, wkf)
            for t in range(2):
                oh = o[:, t * RET_DV:(t + 1) * RET_DV]
                oh = oh * lax.rsqrt(jnp.mean(oh * oh, axis=-1, keepdims=True) + NORM_EPS)
                cs = slice(p * 2 * RET_DV + t * RET_DV, p * 2 * RET_DV + (t + 1) * RET_DV)
                gt = g_ref[rs, cs].astype(F32)
                o_ref[rs, cs] = (oh * (gt / (1.0 + jnp.exp(-gt)))).astype(BF16)
        sfs[p] = sf


def _ret_out(dec, proj, sb, cproj):
    L = proj.shape[0]
    lc = cproj.shape[0]
    S = RET_STEP_CHUNKS
    R = S * RET_CHUNK
    n = L // R
    return pl.pallas_call(
        _ret_out_kernel,
        grid=(n,),
        in_specs=[pl.BlockSpec(memory_space=pltpu.SMEM),
                  pl.BlockSpec((R, 512), lambda i: (i, 0)),
                  pl.BlockSpec((R, 512), lambda i: (i, 1)),
                  pl.BlockSpec((R, 1024), lambda i: (i, 1)),
                  pl.BlockSpec((R, 1024), lambda i: (i, 2)),
                  pl.BlockSpec((S, RET_PAIRS, LANES, RET_DV), lambda i: (i, 0, 0, 0)),
                  pl.BlockSpec((lc, 512), lambda i: (0, 1)),
                  pl.BlockSpec((lc, 1024), lambda i: (0, 1))],
        out_specs=pl.BlockSpec((R, RET_HEADS * RET_DV), lambda i: (i, 0)),
        out_shape=jax.ShapeDtypeStruct((L, RET_HEADS * RET_DV), BF16),
        scratch_shapes=[pltpu.VMEM((RET_PAIRS, LANES, 2 * RET_DV), F32)],
        compiler_params=_params("arbitrary"),
        name="ret_out",
    )(dec, proj, proj, proj, proj, sb, cproj, cproj)


def _attn_kernel(sink_ref, q_ref, kp_ref, kc_ref, kn_ref, vp_ref, vc_ref, vn_ref, ck_ref, cv_ref, o_ref, *, fillers):
    n = pl.program_id(0)
    nstep = pl.num_programs(0)
    B = ATT_BLOCK
    SB = ATT_STEP_BLOCKS
    kj = lax.broadcasted_iota(jnp.int32, (B, B), 0)
    qi = lax.broadcasted_iota(jnp.int32, (B, B), 1)
    ok_prev = jnp.where(n > 0, 0.0, MASK_NEG).astype(F32)
    ok_next = jnp.where(n < nstep - 1, 0.0, MASK_NEG).astype(F32)

    def band(inside, ok):
        return jnp.concatenate([jnp.where(inside, ok, MASK_NEG).astype(F32)] * ATT_GROUP, axis=1)

    bias_prev = [band(kj >= qi, ok_prev if j == 0 else 0.0) for j in range(SB)]
    bias_next = [band(kj <= qi, ok_next if j == SB - 1 else 0.0) for j in range(SB)]
    lane = lax.broadcasted_iota(jnp.int32, (B, LANES), 1)
    lo = lane < 64
    hi = lane >= 64

    def keys_of(j, gs, prev_ref, cur_ref, next_ref, ctx_ref):
        prev = prev_ref[:, gs] if j == 0 else cur_ref[(j - 1) * B:j * B, gs]
        nxt = next_ref[:, gs] if j == SB - 1 else cur_ref[(j + 1) * B:(j + 2) * B, gs]
        return jnp.concatenate([prev, cur_ref[j * B:(j + 1) * B, gs], nxt, ctx_ref[:, gs]], axis=0)

    def scores(j, g):
        gs = slice(g * LANES, (g + 1) * LANES)
        kcat = keys_of(j, gs, kp_ref, kc_ref, kn_ref, ck_ref)
        qs = []
        for r in range(ATT_GROUP):
            h = ATT_GROUP * g + r
            qt = q_ref[j * B:(j + 1) * B, (h // 2) * LANES:(h // 2 + 1) * LANES]
            keep = lo if h % 2 == 0 else hi
            qs.append(jnp.where(keep, qt, jnp.zeros_like(qt)))
        q4 = jnp.concatenate(qs, axis=0)
        return lax.dot_general(kcat, q4, (((1,), (1,)), ((), ())), preferred_element_type=F32)

    def softmax(j, g, s):
        sk = jnp.concatenate([jnp.full((1, B), sink_ref[ATT_GROUP * g + r], F32)
                              for r in range(ATT_GROUP)], axis=1) * LOG2E
        s = jnp.concatenate([s[:B] + bias_prev[j], s[B:2 * B], s[2 * B:3 * B] + bias_next[j], s[3 * B:]], axis=0)
        m = jnp.maximum(jnp.max(s, axis=0, keepdims=True), sk)
        e = jnp.exp2(s - m)
        den = jnp.sum(e, axis=0, keepdims=True) + jnp.exp2(sk - m)
        return e.astype(BF16), den

    def values(j, g, e, den):
        gs = slice(g * LANES, (g + 1) * LANES)
        vcat = keys_of(j, gs, vp_ref, vc_ref, vn_ref, cv_ref)
        res = lax.dot_general(vcat, e, (((0,), (0,)), ((), ())), preferred_element_type=F32) * (1.0 / den)
        for t in range(2):
            even = res[:, (2 * t) * B:(2 * t + 1) * B].T
            odd = res[:, (2 * t + 1) * B:(2 * t + 2) * B].T
            c0 = (2 * g + t) * LANES
            o_ref[j * B:(j + 1) * B, c0:c0 + LANES] = jnp.where(lo, even, odd).astype(BF16)

    units = [(j, g) for j in range(SB) for g in range(ATT_KV_HEADS)]
    per_unit = -(-len(fillers) // len(units))
    s_next = scores(*units[0])
    pending = None
    for u, unit in enumerate(units):
        s_cur = s_next
        if u + 1 < len(units):
            s_next = scores(*units[u + 1])
        e_den = softmax(*unit, s_cur)
        for fill in fillers[u * per_unit:(u + 1) * per_unit]:
            fill()
        if pending is not None:
            values(*units[u - 1], *pending)
        pending = e_den
    values(*units[-1], *pending)


def _attn(sink, proj, kd, vd, ckd, cvd, riders):
    L = proj.shape[0]
    B = ATT_BLOCK
    SB = ATT_STEP_BLOCKS
    n = L // (SB * B)
    nb = L // B
    lc = ckd.shape[0]
    prev = pl.BlockSpec((B, 512), lambda i: (jnp.maximum(i * SB - 1, 0), 0))
    cur = pl.BlockSpec((SB * B, 512), lambda i: (i, 0))
    nxt = pl.BlockSpec((B, 512), lambda i: (jnp.minimum((i + 1) * SB, nb - 1), 0))
    full = pl.BlockSpec((lc, 512), lambda i: (0, 0))
    rid_in_specs, rid_out_specs, rid_shapes = _rider_specs(riders, n)
    return pl.pallas_call(
        _with_cast_riders(_attn_kernel, 10, 1, len(riders)),
        grid=(n,),
        in_specs=[pl.BlockSpec(memory_space=pltpu.SMEM),
                  pl.BlockSpec((SB * B, 1024), lambda i: (i, 3)),
                  prev, cur, nxt, prev, cur, nxt, full, full] + rid_in_specs,
        out_specs=[pl.BlockSpec((SB * B, ATT_HEADS * ATT_DH), lambda i: (i, 0))] + rid_out_specs,
        out_shape=[jax.ShapeDtypeStruct((L, ATT_HEADS * ATT_DH), BF16)] + rid_shapes,
        compiler_params=_params("parallel"),
        name="attn",
    )(sink, proj, kd, kd, kd, vd, vd, vd, ckd, cvd, *[r[0] for r in riders])


def _out_proj_kernel(yr_ref, ya_ref, w_ref, x_ref, gt_ref, g_ref, sh_ref, sc_ref, o_ref, h_ref):
    kr = yr_ref.shape[1]
    for r in range(yr_ref.shape[0] // OUT_ROW_CHUNK):
        rs = slice(r * OUT_ROW_CHUNK, (r + 1) * OUT_ROW_CHUNK)
        acc = jnp.dot(yr_ref[rs, :], w_ref[:kr, :], preferred_element_type=F32)
        acc = acc + jnp.dot(ya_ref[rs, :], w_ref[kr:, :], preferred_element_type=F32)
        x1 = x_ref[rs, :] + gt_ref[...] * acc
        o_ref[rs, :] = x1
        y = x1 * lax.rsqrt(jnp.mean(x1 * x1, axis=-1, keepdims=True) + NORM_EPS)
        y = y * g_ref[...]
        h_ref[rs, :] = (y * (1.0 + sc_ref[...]) + sh_ref[...]).astype(BF16)


def _out_proj(yr, ya, w, x, gt, g, sh, sc, *, tm):
    m, d = x.shape
    kr, ka = yr.shape[1], ya.shape[1]
    row = lambda i: (i, 0)
    vec = pl.BlockSpec((1, d), lambda i: (0, 0))
    return pl.pallas_call(
        _out_proj_kernel,
        grid=(m // tm,),
        in_specs=[pl.BlockSpec((tm, kr), row), pl.BlockSpec((tm, ka), row),
                  pl.BlockSpec((kr + ka, d), lambda i: (0, 0)),
                  pl.BlockSpec((tm, d), row), vec, vec, vec, vec],
        out_specs=[pl.BlockSpec((tm, d), row), pl.BlockSpec((tm, d), row)],
        out_shape=[jax.ShapeDtypeStruct((m, d), F32), jax.ShapeDtypeStruct((m, d), BF16)],
        compiler_params=_params("parallel"),
        name="out_proj",
    )(yr, ya, w, x, gt, g, sh, sc)


def _ffn_kernel(h_ref, gt_ref, gfin_ref, wg_ref, wu_ref, wd_ref, x_hbm, o_ref, x_buf, sem):
    i = pl.program_id(0)
    f = pl.program_id(1)
    last = pl.num_programs(1) - 1
    rows = o_ref.shape[0]
    x_copy = pltpu.make_async_copy(x_hbm.at[pl.ds(pl.multiple_of(i * rows, rows), rows), :], x_buf, sem.at[0])

    def step(first, final):
        wd = wd_ref[...].astype(BF16)
        for r in range(rows // FFN_ROW_CHUNK):
            rs = slice(r * FFN_ROW_CHUNK, (r + 1) * FFN_ROW_CHUNK)
            h = h_ref[rs, :]
            a = jnp.dot(h, wg_ref[0], preferred_element_type=F32)
            u = jnp.dot(h, wu_ref[0], preferred_element_type=F32)
            act = ((a / (1.0 + jnp.exp(-a))) * u).astype(BF16)
            part = jnp.dot(act, wd, preferred_element_type=F32)
            if first:
                o_ref[rs, :] = part
            elif not final:
                o_ref[rs, :] += part
            else:
                y = x_buf[rs, :] + gt_ref[...] * (o_ref[rs, :] + part)
                y = y * lax.rsqrt(jnp.mean(y * y, axis=-1, keepdims=True) + NORM_EPS)
                o_ref[rs, :] = y * gfin_ref[...]

    @pl.when(f == 0)
    def _():
        x_copy.start()
        step(first=True, final=False)

    @pl.when((f > 0) & (f < last))
    def _():
        step(first=False, final=False)

    @pl.when(f == last)
    def _():
        x_copy.wait()
        step(first=False, final=True)


def _ffn(h, x, gt, gfin, wg, wu, wd, *, tm):
    m, d = x.shape
    nf = wg.shape[0]
    assert wg.shape == wu.shape == (nf, d, FFN_TILE) and wd.shape == (nf * FFN_TILE, d)
    assert m % tm == 0 and tm % FFN_ROW_CHUNK == 0
    row = lambda i, f: (i, 0)
    vec = pl.BlockSpec((1, d), lambda i, f: (0, 0))
    wcol = pl.BlockSpec((1, d, FFN_TILE), lambda i, f: (f, 0, 0))
    return pl.pallas_call(
        _ffn_kernel,
        grid=(m // tm, nf),
        in_specs=[pl.BlockSpec((tm, d), row), vec, vec, wcol, wcol,
                  pl.BlockSpec((FFN_TILE, d), lambda i, f: (f, 0)),
                  pl.BlockSpec(memory_space=pl.ANY)],
        out_specs=pl.BlockSpec((tm, d), row),
        out_shape=jax.ShapeDtypeStruct((m, d), F32),
        scratch_shapes=[pltpu.VMEM((tm, d), F32), pltpu.SemaphoreType.DMA((1,))],
        compiler_params=_params("arbitrary", "arbitrary"),
        name="ffn",
    )(h, gt, gfin, wg, wu, wd, x)


def _rope_tables(L):
    f32 = np.float32
    lane = np.arange(LANES)
    inv1 = f32(ROPE_BASE) ** (-np.arange(32, dtype=f32) / f32(32))
    ang1 = np.arange(L, dtype=f32)[:, None] * inv1[None, :]
    sgn1 = np.where((lane % 64) < 32, -1.0, 1.0).astype(f32)
    cos1 = np.tile(np.cos(ang1), (1, LANES // 32))
    sin1 = np.tile(np.sin(ang1), (1, LANES // 32)) * sgn1[None, :]
    inv2 = f32(ROPE_BASE) ** (-np.arange(16, dtype=f32) / f32(16))
    nrow = L // GRID_W
    ang_r = np.arange(nrow, dtype=f32)[:, None] * inv2[None, :]
    ang_c = np.arange(GRID_W, dtype=f32)[:, None] * inv2[None, :]
    sgna = np.where((lane % 32) < 16, -1.0, 1.0).astype(f32)

    def expand(fr, fc):
        by_row = np.broadcast_to(np.tile(fr, (1, 2))[:, None, :], (nrow, GRID_W, 32))
        by_col = np.broadcast_to(np.tile(fc, (1, 2))[None, :, :], (nrow, GRID_W, 32))
        head = np.concatenate([by_row, by_col], axis=-1).reshape(L, 64)
        return np.tile(head, (1, LANES // 64))

    cosa = expand(np.cos(ang_r), np.cos(ang_c))
    sina = expand(np.sin(ang_r), np.sin(ang_c)) * sgna[None, :]
    return tuple(np.ascontiguousarray(t, dtype=f32) for t in (cos1, sin1, cosa, sina))


def kernel(x, c, ctx, c_ctx, w_mod, b_mod, norm_mix, norm_ffn, w_in, ret_decay, attn_sink,
           w_out, w_gate, w_up, w_down, norm_final):
    B, L, D = x.shape
    assert B == 1 and w_mod.shape[0] == 1, "single batch element, depth-1 layer"
    x2 = x[0]
    xc2 = ctx[0]

    cv = jnp.zeros((8, D), F32).at[0].set(c[0]).at[1].set(c_ctx)
    mod = _mod(cv, w_mod[0], b_mod[0][None, :], 2 * D)
    sh_m, sc_m = mod[0:1, 0:D], mod[0:1, D:2 * D]
    sh_mc, sc_mc = mod[1:2, 0:D], mod[1:2, D:2 * D]

    g_mix = norm_mix[0][None, :]
    cproj, ckd, cvd, w_in_b = _ctx_proj(xc2, g_mix, sh_mc, sc_mc, w_in[0])
    proj, kd, vd, mod_rest = _in_proj(x2, g_mix, sh_m, sc_m, w_in_b, _rope_tables(L),
                                      c[0][:, None], w_mod[0], b_mod[0][None, :], 2 * D, tm=512)
    gt_m, sh_f, sc_f, gt_f = [mod_rest.reshape(1, 4 * D)[:, k * D:(k + 1) * D] for k in range(4)]

    dec = ret_decay[0].astype(F32)
    sb = _ret_bwd_states(dec, proj, cproj)
    y_ret = _ret_out(dec, proj, sb, cproj)
    y_att, w_gate_b, w_up_b, w_out_b = _attn(
        attn_sink[0].astype(F32), proj, kd, vd, ckd, cvd,
        [(w_gate[0], 1, FFN_TILE), (w_up[0], 1, FFN_TILE), (w_out[0], 1, None)])

    x1, hff = _out_proj(y_ret, y_att, w_out_b, x2, gt_m, norm_ffn[0][None, :], sh_f, sc_f, tm=512)
    out = _ffn(hff, x1, gt_f, norm_final[None, :], w_gate_b, w_up_b, w_down[0], tm=1024)
    return out[None]
```

```python
import functools

import jax
import jax.numpy as jnp
import numpy as np
from jax import lax
from jax.experimental import pallas as pl
from jax.experimental.pallas import tpu as pltpu

GRID_W = 64
RET_HEADS = 8
RET_DK = 64
RET_DV = 128
RET_CHUNK = 128
ATT_HEADS = 16
ATT_KV_HEADS = 4
ATT_DH = 64
ATT_GROUP = ATT_HEADS // ATT_KV_HEADS
WINDOW = 128
ATT_BLOCK = 128
ROPE_BASE = 10000.0
NORM_EPS = 1e-6
K_SCALE = RET_DK ** -0.5
ATT_SCALE = ATT_DH ** -0.5
LOG2E = 1.4426950408889634

LANES = 128
RET_PAIRS = RET_HEADS // 2
MASK_NEG = -1e30
VMEM_LIMIT = 56 * 1024 * 1024
CAST_PIECE_ROWS = 16
RET_STEP_CHUNKS = 8
ATT_STEP_BLOCKS = 2
OUT_COL_TILE = 512
IN_ROW_CHUNK = 512
FFN_TILE = 512
FFN_ROW_CHUNK = 1024

BF16 = jnp.bfloat16
F32 = jnp.float32


def _params(*sem):
    return pltpu.CompilerParams(dimension_semantics=sem, vmem_limit_bytes=VMEM_LIMIT)


def _with_cast_riders(body, n_in, n_out, n_rid):
    def wrapped(*refs):
        ins = refs[:n_in]
        rid_in = refs[n_in:n_in + n_rid]
        outs = refs[n_in + n_rid:n_in + n_rid + n_out]
        rid_out = refs[n_in + n_rid + n_out:n_in + 2 * n_rid + n_out]
        scratch = refs[n_in + 2 * n_rid + n_out:]

        def piece(src, dst, r0):
            rs = slice(r0, r0 + CAST_PIECE_ROWS)
            if len(dst.shape) == 2:
                dst[rs, :] = src[rs, :].astype(BF16)
            else:
                tc = dst.shape[2]
                for t in range(dst.shape[0]):
                    dst[t, rs, :] = src[rs, t * tc:(t + 1) * tc].astype(BF16)

        pieces = [(lambda s=src, d=dst, r=r0: piece(s, d, r))
                  for src, dst in zip(rid_in, rid_out) for r0 in range(0, src.shape[0], CAST_PIECE_ROWS)]
        body(*ins, *outs, *scratch, rider_pieces=pieces)
    return wrapped


def _rider_specs(riders, steps, extra_steps=0):
    in_specs, out_specs, shapes = [], [], []
    slab = (lambda i: jnp.minimum(i, steps - 1)) if extra_steps else (lambda i: i)
    for w, ncb, tile in riders:
        rows, cols = w.shape
        nrb = steps // ncb
        assert nrb * ncb == steps and rows % nrb == 0 and cols % ncb == 0
        br, bc = rows // nrb, cols // ncb
        assert br % CAST_PIECE_ROWS == 0 and bc % LANES == 0, "slab must be bf16-tile aligned"
        in_specs.append(pl.BlockSpec((br, bc), lambda i, ncb=ncb: (slab(i) // ncb, slab(i) % ncb)))
        if tile is None:
            out_specs.append(in_specs[-1])
            shapes.append(jax.ShapeDtypeStruct(w.shape, BF16))
        else:
            assert ncb == 1 and cols % tile == 0 and tile % LANES == 0
            out_specs.append(pl.BlockSpec((cols // tile, br, tile), lambda i: (0, slab(i), 0)))
            shapes.append(jax.ShapeDtypeStruct((cols // tile, rows, tile), BF16))
    return in_specs, out_specs, shapes


def _mod_kernel(cv_ref, w_ref, b_ref, o_ref):
    cv = cv_ref[...]
    s = cv / (1.0 + jnp.exp(-cv))
    o_ref[...] = jnp.dot(s.astype(BF16), w_ref[...].astype(BF16),
                         preferred_element_type=F32) + b_ref[...]


def _mod(cv, w, b, n):
    d = w.shape[0]
    tn = 1024
    assert n % tn == 0
    return pl.pallas_call(
        _mod_kernel,
        grid=(n // tn,),
        in_specs=[pl.BlockSpec((8, d), lambda j: (0, 0)),
                  pl.BlockSpec((d, tn), lambda j: (0, j)),
                  pl.BlockSpec((1, tn), lambda j: (0, j))],
        out_specs=pl.BlockSpec((8, tn), lambda j: (0, j)),
        out_shape=jax.ShapeDtypeStruct((8, n), F32),
        compiler_params=_params("parallel"),
        name="mod",
    )(cv, w, b)


def _rot_pairs(a, cos, sin_signed, half):
    lane = lax.broadcasted_iota(jnp.int32, a.shape, 1)
    first = (lane % (2 * half)) < half
    rot = jnp.where(first, pltpu.roll(a, LANES - half, 1), pltpu.roll(a, half, 1))
    return a * cos + rot * sin_signed


def _dup_halves(a):
    lane = lax.broadcasted_iota(jnp.int32, a.shape, 1)
    r = pltpu.roll(a, 64, 1)
    lo = lane < 64
    return jnp.where(lo, a, r), jnp.where(lo, r, a)


_PROJ_TILE = 512
_PROJ_TILE_KINDS = ("ret_q", "ret_k", "plain", "plain", "plain", "plain", "att_q", "att_q", "att_kv")


def _in_proj_kernel(x_ref, g_ref, sh_ref, sc_ref, w_ref, c1_ref, s1_ref, ca_ref, sa_ref,
                    cc_ref, wm_ref, bm_ref, o_ref, kd_ref, vd_ref, mod_ref):
    def mod_rider():
        cc = cc_ref[...]
        s_col = cc / (1.0 + jnp.exp(-cc))
        mod_ref[0] = jnp.sum(wm_ref[...] * s_col, axis=0, keepdims=True) + bm_ref[...]

    tn = _PROJ_TILE
    for r in range(x_ref.shape[0] // IN_ROW_CHUNK):
        rs = slice(r * IN_ROW_CHUNK, (r + 1) * IN_ROW_CHUNK)
        xf = x_ref[rs, :]
        y = xf * lax.rsqrt(jnp.mean(xf * xf, axis=-1, keepdims=True) + NORM_EPS)
        y = y * g_ref[...]
        h = (y * (1.0 + sc_ref[...]) + sh_ref[...]).astype(BF16)

        def rope1(a):
            return _rot_pairs(a, c1_ref[rs, :], s1_ref[rs, :], 32)

        def ropea(a):
            return _rot_pairs(a, ca_ref[rs, :], sa_ref[rs, :], 16)

        order = sorted(range(len(_PROJ_TILE_KINDS)), key=lambda t: _PROJ_TILE_KINDS[t] == "plain")
        for j in order:
            kind = _PROJ_TILE_KINDS[j]
            acc = jnp.dot(h, w_ref[:, j * tn:(j + 1) * tn], preferred_element_type=F32)
            if r == 0 and j == order[-2]:
                mod_rider()
            for c in range(tn // LANES):
                a = acc[:, c * LANES:(c + 1) * LANES]
                if kind == "ret_q":
                    a = rope1(a)
                elif kind == "ret_k":
                    a = rope1(a) * K_SCALE
                elif kind == "att_q":
                    a = ropea(a) * (ATT_SCALE * LOG2E)
                elif kind == "att_kv" and c < 2:
                    a = ropea(a)
                o_ref[rs, j * tn + c * LANES:j * tn + (c + 1) * LANES] = a.astype(BF16)
                if kind == "att_kv":
                    dup_ref = kd_ref if c < 2 else vd_ref
                    d0, d1 = _dup_halves(a)
                    t = 2 * (c % 2)
                    dup_ref[rs, t * LANES:(t + 1) * LANES] = d0.astype(BF16)
                    dup_ref[rs, (t + 1) * LANES:(t + 2) * LANES] = d1.astype(BF16)


def _in_proj(x, g, sh, sc, w, tabs, c_col, w_mod, b_mod, mod_done, *, tm):
    m, d = x.shape
    n = w.shape[1]
    assert n == _PROJ_TILE * len(_PROJ_TILE_KINDS) and m % tm == 0 and tm % IN_ROW_CHUNK == 0
    steps = m // tm
    slab = (w_mod.shape[1] - mod_done) // steps
    assert slab * steps == w_mod.shape[1] - mod_done and slab % LANES == 0 and mod_done % slab == 0
    slab0 = mod_done // slab
    c1, s1, ca, sa = tabs
    row = lambda i: (i, 0)
    vec = pl.BlockSpec((1, d), lambda i: (0, 0))
    tab = pl.BlockSpec((tm, LANES), row)
    return pl.pallas_call(
        _in_proj_kernel,
        grid=(m // tm,),
        in_specs=[pl.BlockSpec((tm, d), row), vec, vec, vec,
                  pl.BlockSpec((d, n), lambda i: (0, 0), pipeline_mode=pl.Buffered(1)),
                  tab, tab, tab, tab,
                  pl.BlockSpec((d, 1), lambda i: (0, 0)),
                  pl.BlockSpec((d, slab), lambda i: (0, slab0 + i)),
                  pl.BlockSpec((1, slab), lambda i: (0, slab0 + i))],
        out_specs=[pl.BlockSpec((tm, n), row),
                   pl.BlockSpec((tm, 512), row),
                   pl.BlockSpec((tm, 512), row),
                   pl.BlockSpec((1, 1, slab), lambda i: (i, 0, 0))],
        out_shape=[jax.ShapeDtypeStruct((m, n), BF16),
                   jax.ShapeDtypeStruct((m, 512), BF16),
                   jax.ShapeDtypeStruct((m, 512), BF16),
                   jax.ShapeDtypeStruct((steps, 1, slab), F32)],
        compiler_params=_params("parallel"),
        name="in_proj",
    )(x, g, sh, sc, w, c1, s1, ca, sa, c_col, w_mod, b_mod)


def _ctx_proj_kernel(x_ref, g_ref, sh_ref, sc_ref, w_ref, o_ref, kd_ref, vd_ref, wb_ref, h_ref):
    j = pl.program_id(0)

    @pl.when(j == 0)
    def _():
        xf = x_ref[...]
        y = xf * lax.rsqrt(jnp.mean(xf * xf, axis=-1, keepdims=True) + NORM_EPS)
        y = y * g_ref[...]
        h_ref[...] = (y * (1.0 + sc_ref[...]) + sh_ref[...]).astype(BF16)

    wb = w_ref[...].astype(BF16)
    wb_ref[...] = wb
    acc = jnp.dot(h_ref[...], wb, preferred_element_type=F32)
    is_ret_k = _PROJ_TILE_KINDS.index("ret_k")
    o_ref[...] = (acc * jnp.where(j == is_ret_k, K_SCALE, 1.0)).astype(BF16)

    @pl.when(j == _PROJ_TILE_KINDS.index("att_kv"))
    def _():
        for c in range(_PROJ_TILE // LANES):
            dup_ref = kd_ref if c < 2 else vd_ref
            d0, d1 = _dup_halves(acc[:, c * LANES:(c + 1) * LANES])
            t = 2 * (c % 2)
            dup_ref[:, t * LANES:(t + 1) * LANES] = d0.astype(BF16)
            dup_ref[:, (t + 1) * LANES:(t + 2) * LANES] = d1.astype(BF16)


def _ctx_proj(x, g, sh, sc, w):
    m, d = x.shape
    n = w.shape[1]
    tn = _PROJ_TILE
    assert n == tn * len(_PROJ_TILE_KINDS)
    fixed = lambda j: (0, 0)
    vec = pl.BlockSpec((1, d), fixed)
    return pl.pallas_call(
        _ctx_proj_kernel,
        grid=(n // tn,),
        in_specs=[pl.BlockSpec((m, d), fixed), vec, vec, vec,
                  pl.BlockSpec((d, tn), lambda j: (0, j))],
        out_specs=[pl.BlockSpec((m, tn), lambda j: (0, j)),
                   pl.BlockSpec((m, 512), fixed),
                   pl.BlockSpec((m, 512), fixed),
                   pl.BlockSpec((d, tn), lambda j: (0, j))],
        out_shape=[jax.ShapeDtypeStruct((m, n), BF16),
                   jax.ShapeDtypeStruct((m, 512), BF16),
                   jax.ShapeDtypeStruct((m, 512), BF16),
                   jax.ShapeDtypeStruct((d, n), BF16)],
        scratch_shapes=[pltpu.VMEM((m, d), BF16)],
        compiler_params=_params("arbitrary"),
        name="ctx_proj",
    )(x, g, sh, sc, w)


def _pair_lg(dec_ref, d, p, shape):
    lane = lax.broadcasted_iota(jnp.int32, shape, 1)
    first = (lane % LANES) < 64
    raw = jnp.where(first, jnp.full(shape, dec_ref[d, 2 * p], F32), jnp.full(shape, dec_ref[d, 2 * p + 1], F32))
    return -jnp.exp(raw)


def _head_block_mask(shape):
    r = lax.broadcasted_iota(jnp.int32, shape, 0)
    c = lax.broadcasted_iota(jnp.int32, shape, 1)
    return (r // 64) == (c // LANES)


def _kv_pair(k_pair, v_pair, w):
    kw = (k_pair.astype(F32) * w).astype(BF16)
    kv = lax.dot_general(kw, v_pair, (((0,), (0,)), ((), ())), preferred_element_type=F32)
    return jnp.where(_head_block_mask(kv.shape), kv, 0.0)


def _row_decay(dec_ref, d, p):
    shape = (LANES, 2 * RET_DV)
    rowh = lax.broadcasted_iota(jnp.int32, shape, 0) < 64
    raw = jnp.where(rowh, jnp.full(shape, dec_ref[d, 2 * p], F32), jnp.full(shape, dec_ref[d, 2 * p + 1], F32))
    return jnp.exp(-jnp.exp(raw) * float(RET_CHUNK))


def _compact_state(s):
    row = lax.broadcasted_iota(jnp.int32, (LANES, RET_DV), 0)
    return jnp.where(row < 64, s[:, :RET_DV], s[:, RET_DV:])


def _expand_state(c):
    row = lax.broadcasted_iota(jnp.int32, c.shape, 0)
    z = jnp.zeros_like(c)
    return jnp.concatenate([jnp.where(row < 64, c, z), jnp.where(row < 64, z, c)], axis=1)


def _ret_bwd_kernel(dec_ref, k_ref, v_ref, ck_ref, cv_ref, sb_ref, sbs, *, rider_pieces):
    for piece in rider_pieces:
        piece()
    i = pl.program_id(0)
    C = RET_CHUNK
    lc = ck_ref.shape[0]

    @pl.when(i == 0)
    def _():
        pos = lax.broadcasted_iota(jnp.int32, (lc, LANES), 0).astype(F32)
        for p in range(RET_PAIRS):
            ks = slice(p * LANES, (p + 1) * LANES)
            vs = slice(p * 2 * RET_DV, (p + 1) * 2 * RET_DV)
            wb = jnp.exp(_pair_lg(dec_ref, 1, p, (lc, LANES)) * pos)
            sbs[p] = _kv_pair(ck_ref[:, ks], cv_ref[:, vs], wb)

    pos = lax.broadcasted_iota(jnp.int32, (C, LANES), 0).astype(F32)
    for p in range(RET_PAIRS):
        ks = slice(p * LANES, (p + 1) * LANES)
        vs = slice(p * 2 * RET_DV, (p + 1) * 2 * RET_DV)
        wb = jnp.exp(_pair_lg(dec_ref, 1, p, (C, LANES)) * pos)
        gb = _row_decay(dec_ref, 1, p)
        sb = sbs[p]
        for cc in reversed(range(RET_STEP_CHUNKS)):
            rs = slice(cc * C, (cc + 1) * C)
            sb_ref[cc, p] = _compact_state(sb).astype(BF16)
            sb = gb * sb + _kv_pair(k_ref[rs, ks], v_ref[rs, vs], wb)
        sbs[p] = sb


def _ret_bwd_states(dec, proj, cproj, riders):
    L = proj.shape[0]
    lc = cproj.shape[0]
    S = RET_STEP_CHUNKS
    R = S * RET_CHUNK
    n = L // R
    rid_in_specs, rid_out_specs, rid_shapes = _rider_specs(riders, n)
    return pl.pallas_call(
        _with_cast_riders(_ret_bwd_kernel, 5, 1, len(riders)),
        grid=(n,),
        in_specs=[pl.BlockSpec(memory_space=pltpu.SMEM),
                  pl.BlockSpec((R, 512), lambda i: (n - 1 - i, 1)),
                  pl.BlockSpec((R, 1024), lambda i: (n - 1 - i, 1)),
                  pl.BlockSpec((lc, 512), lambda i: (0, 1)),
                  pl.BlockSpec((lc, 1024), lambda i: (0, 1))] + rid_in_specs,
        out_specs=[pl.BlockSpec((S, RET_PAIRS, LANES, RET_DV), lambda i: (n - 1 - i, 0, 0, 0))] + rid_out_specs,
        out_shape=[jax.ShapeDtypeStruct((n * S, RET_PAIRS, LANES, RET_DV), BF16)] + rid_shapes,
        scratch_shapes=[pltpu.VMEM((RET_PAIRS, LANES, 2 * RET_DV), F32)],
        compiler_params=_params("arbitrary"),
        name="ret_bwd",
    )(dec, proj, proj, cproj, cproj, *[r[0] for r in riders])


def _ret_out_kernel(dec_ref, q_ref, k_ref, v_ref, g_ref, sb_ref, ck_ref, cv_ref, o_ref, sfs):
    i = pl.program_id(0)
    C = RET_CHUNK
    lc = ck_ref.shape[0]

    @pl.when(i == 0)
    def _():
        cpos = lax.broadcasted_iota(jnp.int32, (lc, LANES), 0).astype(F32)
        for p in range(RET_PAIRS):
            ks = slice(p * LANES, (p + 1) * LANES)
            vs = slice(p * 2 * RET_DV, (p + 1) * 2 * RET_DV)
            wf = jnp.exp(_pair_lg(dec_ref, 0, p, (lc, LANES)) * (lc - 1.0 - cpos))
            sfs[p] = _kv_pair(ck_ref[:, ks], cv_ref[:, vs], wf)

    pos = lax.broadcasted_iota(jnp.int32, (C, LANES), 0).astype(F32)
    n_i = lax.broadcasted_iota(jnp.int32, (C, 2 * C), 0)
    m_i = lax.broadcasted_iota(jnp.int32, (C, 2 * C), 1) % C
    rel = (n_i - m_i).astype(F32)
    lane = lax.broadcasted_iota(jnp.int32, (C, LANES), 1)
    lo = lane < 64
    for p in range(RET_PAIRS):
        ks = slice(p * LANES, (p + 1) * LANES)
        vs = slice(p * 2 * RET_DV, (p + 1) * 2 * RET_DV)
        col_a = lax.broadcasted_iota(jnp.int32, (C, 2 * C), 1) < C
        raw_f = jnp.where(col_a, jnp.full((C, 2 * C), dec_ref[0, 2 * p], F32), jnp.full((C, 2 * C), dec_ref[0, 2 * p + 1], F32))
        raw_b = jnp.where(col_a, jnp.full((C, 2 * C), dec_ref[1, 2 * p], F32), jnp.full((C, 2 * C), dec_ref[1, 2 * p + 1], F32))
        dmat = jnp.where(rel >= 0, jnp.exp(-jnp.exp(raw_f) * jnp.maximum(rel, 0.0)),
                         jnp.exp(-jnp.exp(raw_b) * jnp.maximum(-rel, 0.0)))
        lg_f = _pair_lg(dec_ref, 0, p, (C, LANES))
        wqf = jnp.exp(lg_f * (pos + 1.0))
        wqb = jnp.exp(_pair_lg(dec_ref, 1, p, (C, LANES)) * (float(C) - pos))
        wkf = jnp.exp(lg_f * (C - 1.0 - pos))
        gf = _row_decay(dec_ref, 0, p)
        sf = sfs[p]
        for cc in range(RET_STEP_CHUNKS):
            rs = slice(cc * C, (cc + 1) * C)
            q = q_ref[rs, ks]
            k = k_ref[rs, ks]
            v = v_ref[rs, vs]
            zk = jnp.zeros_like(k)
            kst = jnp.concatenate([jnp.where(lo, k, zk), jnp.where(lo, zk, k)], axis=0)
            s = lax.dot_general(q, kst, (((1,), (1,)), ((), ())), preferred_element_type=F32)
            sd = (s * dmat).astype(BF16)
            qf32 = q.astype(F32)
            qwf = (qf32 * wqf).astype(BF16)
            qwb = (qf32 * wqb).astype(BF16)
            zv = jnp.zeros((C, RET_DV), BF16)
            vbd = jnp.concatenate([jnp.concatenate([v[:, :RET_DV], zv], axis=1),
                                   jnp.concatenate([zv, v[:, RET_DV:]], axis=1)], axis=0)
            lhs = jnp.concatenate([sd, qwf, qwb], axis=1)
            rhs = jnp.concatenate([vbd, sf.astype(BF16), _expand_state(sb_ref[cc, p])], axis=0)
            o = jnp.dot(lhs, rhs, preferred_element_type=F32)
            sf = gf * sf + _kv_pair(k, v, wkf)
            for t in range(2):
                oh = o[:, t * RET_DV:(t + 1) * RET_DV]
                oh = oh * lax.rsqrt(jnp.mean(oh * oh, axis=-1, keepdims=True) + NORM_EPS)
                cs = slice(p * 2 * RET_DV + t * RET_DV, p * 2 * RET_DV + (t + 1) * RET_DV)
                gt = g_ref[rs, cs].astype(F32)
                o_ref[rs, cs] = (oh * (gt / (1.0 + jnp.exp(-gt)))).astype(BF16)
        sfs[p] = sf


def _ret_out(dec, proj, sb, cproj):
    L = proj.shape[0]
    lc = cproj.shape[0]
    S = RET_STEP_CHUNKS
    R = S * RET_CHUNK
    n = L // R
    return pl.pallas_call(
        _ret_out_kernel,
        grid=(n,),
        in_specs=[pl.BlockSpec(memory_space=pltpu.SMEM),
                  pl.BlockSpec((R, 512), lambda i: (i, 0)),
                  pl.BlockSpec((R, 512), lambda i: (i, 1)),
                  pl.BlockSpec((R, 1024), lambda i: (i, 1)),
                  pl.BlockSpec((R, 1024), lambda i: (i, 2)),
                  pl.BlockSpec((S, RET_PAIRS, LANES, RET_DV), lambda i: (i, 0, 0, 0)),
                  pl.BlockSpec((lc, 512), lambda i: (0, 1)),
                  pl.BlockSpec((lc, 1024), lambda i: (0, 1))],
        out_specs=pl.BlockSpec((R, RET_HEADS * RET_DV), lambda i: (i, 0)),
        out_shape=jax.ShapeDtypeStruct((L, RET_HEADS * RET_DV), BF16),
        scratch_shapes=[pltpu.VMEM((RET_PAIRS, LANES, 2 * RET_DV), F32)],
        compiler_params=_params("arbitrary"),
        name="ret_out",
    )(dec, proj, proj, proj, proj, sb, cproj, cproj)


def _attn_out_kernel(sink_ref, q_ref, kp_ref, kc_ref, kn_ref, vp_ref, vc_ref, vn_ref, ck_ref, cv_ref,
                     yr_ref, x_ref, w_ref, gt_ref, g_ref, sh_ref, sc_ref,
                     x1_ref, h_ref, y_scr, *, rider_pieces):
    i = pl.program_id(0)
    n = pl.num_programs(0) - 1
    slot = lax.rem(i, 2)
    B = ATT_BLOCK
    SB = ATT_STEP_BLOCKS
    kj = lax.broadcasted_iota(jnp.int32, (B, B), 0)
    qi = lax.broadcasted_iota(jnp.int32, (B, B), 1)
    ok_prev = jnp.where(i > 0, 0.0, MASK_NEG).astype(F32)
    ok_next = jnp.where(i < n - 1, 0.0, MASK_NEG).astype(F32)

    def band(inside, ok):
        return jnp.concatenate([jnp.where(inside, ok, MASK_NEG).astype(F32)] * ATT_GROUP, axis=1)

    lane = lax.broadcasted_iota(jnp.int32, (B, LANES), 1)
    lo = lane < 64
    hi = lane >= 64

    def keys_of(j, gs, prev_ref, cur_ref, next_ref, ctx_ref):
        prev = prev_ref[:, gs] if j == 0 else cur_ref[(j - 1) * B:j * B, gs]
        nxt = next_ref[:, gs] if j == SB - 1 else cur_ref[(j + 1) * B:(j + 2) * B, gs]
        return jnp.concatenate([prev, cur_ref[j * B:(j + 1) * B, gs], nxt, ctx_ref[:, gs]], axis=0)

    def scores(j, g):
        gs = slice(g * LANES, (g + 1) * LANES)
        kcat = keys_of(j, gs, kp_ref, kc_ref, kn_ref, ck_ref)
        qs = []
        for r in range(ATT_GROUP):
            h = ATT_GROUP * g + r
            qt = q_ref[j * B:(j + 1) * B, (h // 2) * LANES:(h // 2 + 1) * LANES]
            keep = lo if h % 2 == 0 else hi
            qs.append(jnp.where(keep, qt, jnp.zeros_like(qt)))
        q4 = jnp.concatenate(qs, axis=0)
        return lax.dot_general(kcat, q4, (((1,), (1,)), ((), ())), preferred_element_type=F32)

    def softmax(j, g, s):
        sk = jnp.concatenate([jnp.full((1, B), sink_ref[ATT_GROUP * g + r], F32)
                              for r in range(ATT_GROUP)], axis=1) * LOG2E
        bias_prev = band(kj >= qi, ok_prev if j == 0 else 0.0)
        bias_next = band(kj <= qi, ok_next if j == SB - 1 else 0.0)
        s = jnp.concatenate([s[:B] + bias_prev, s[B:2 * B], s[2 * B:3 * B] + bias_next, s[3 * B:]], axis=0)
        m = jnp.maximum(jnp.max(s, axis=0, keepdims=True), sk)
        e = jnp.exp2(s - m)
        den = jnp.sum(e, axis=0, keepdims=True) + jnp.exp2(sk - m)
        return e.astype(BF16), den

    def values(j, g, e, den):
        gs = slice(g * LANES, (g + 1) * LANES)
        vcat = keys_of(j, gs, vp_ref, vc_ref, vn_ref, cv_ref)
        res = lax.dot_general(vcat, e, (((0,), (0,)), ((), ())), preferred_element_type=F32) * (1.0 / den)
        for t in range(2):
            even = res[:, (2 * t) * B:(2 * t + 1) * B].T
            odd = res[:, (2 * t + 1) * B:(2 * t + 2) * B].T
            c0 = (2 * g + t) * LANES
            y_scr[slot, j * B:(j + 1) * B, c0:c0 + LANES] = jnp.where(lo, even, odd).astype(BF16)

    def attention(fill_at):
        units = [(j, g) for j in range(SB) for g in range(ATT_KV_HEADS)]
        s_next = scores(*units[0])
        pending = None
        for u, unit in enumerate(units):
            s_cur = s_next
            if u + 1 < len(units):
                s_next = scores(*units[u + 1])
            e_den = softmax(*unit, s_cur)
            for fill in fill_at.get(u, ()):
                fill()
            if pending is not None:
                values(*units[u - 1], *pending)
            pending = e_den
        values(*units[-1], *pending)

    def projection_pieces():
        kr = yr_ref.shape[1]
        tn = OUT_COL_TILE
        ya = y_scr.at[1 - slot]

        def column(c):
            cs = slice(c * tn, (c + 1) * tn)
            acc = jnp.dot(yr_ref[...], w_ref[:kr, cs], preferred_element_type=F32)
            acc = acc + jnp.dot(ya[...], w_ref[kr:, cs], preferred_element_type=F32)
            x1_ref[:, cs] = x_ref[:, cs] + gt_ref[:, cs] * acc

        def finish():
            x1 = x1_ref[...]
            y = x1 * lax.rsqrt(jnp.mean(x1 * x1, axis=-1, keepdims=True) + NORM_EPS)
            y = y * g_ref[...]
            h_ref[...] = (y * (1.0 + sc_ref[...]) + sh_ref[...]).astype(BF16)

        return [functools.partial(column, c) for c in range(x_ref.shape[1] // tn)] + [finish]

    n_units = SB * ATT_KV_HEADS

    def spread(pieces, into):
        for k, p in enumerate(pieces):
            into.setdefault(min(k * n_units // len(pieces), n_units - 1), []).append(p)
        return into

    @pl.when(i == 0)
    def _():
        attention(spread(rider_pieces, {}))

    @pl.when((i > 0) & (i < n))
    def _():
        attention(spread(projection_pieces(), spread(rider_pieces, {})))

    @pl.when(i == n)
    def _():
        for piece in projection_pieces():
            piece()


def _attn_out(sink, proj, kd, vd, ckd, cvd, y_ret, x, w_out, gt, g, sh, sc, riders):
    L, d = x.shape
    B = ATT_BLOCK
    SB = ATT_STEP_BLOCKS
    R = SB * B
    n = L // R
    nb = L // B
    lc = ckd.shape[0]
    att = lambda i: jnp.minimum(i, n - 1)
    lag = lambda i: (jnp.maximum(i - 1, 0), 0)
    prev = pl.BlockSpec((B, 512), lambda i: (jnp.maximum(att(i) * SB - 1, 0), 0))
    cur = pl.BlockSpec((R, 512), lambda i: (att(i), 0))
    nxt = pl.BlockSpec((B, 512), lambda i: (jnp.minimum((att(i) + 1) * SB, nb - 1), 0))
    full = pl.BlockSpec((lc, 512), lambda i: (0, 0))
    vec = pl.BlockSpec((1, d), lambda i: (0, 0))
    rid_in_specs, rid_out_specs, rid_shapes = _rider_specs(riders, n, extra_steps=1)
    return pl.pallas_call(
        _with_cast_riders(_attn_out_kernel, 17, 2, len(riders)),
        grid=(n + 1,),
        in_specs=[pl.BlockSpec(memory_space=pltpu.SMEM),
                  pl.BlockSpec((R, 1024), lambda i: (att(i), 3)),
                  prev, cur, nxt, prev, cur, nxt, full, full,
                  pl.BlockSpec((R, y_ret.shape[1]), lag),
                  pl.BlockSpec((R, d), lag),
                  pl.BlockSpec(w_out.shape, lambda i: (0, 0), pipeline_mode=pl.Buffered(1)),
                  vec, vec, vec, vec] + rid_in_specs,
        out_specs=[pl.BlockSpec((R, d), lag), pl.BlockSpec((R, d), lag)] + rid_out_specs,
        out_shape=[jax.ShapeDtypeStruct((L, d), F32), jax.ShapeDtypeStruct((L, d), BF16)] + rid_shapes,
        scratch_shapes=[pltpu.VMEM((2, R, ATT_HEADS * ATT_DH), BF16)],
        compiler_params=_params("arbitrary"),
        name="attn_out",
    )(sink, proj, kd, kd, kd, vd, vd, vd, ckd, cvd, y_ret, x, w_out, gt, g, sh, sc, *[r[0] for r in riders])


def _ffn_kernel(h_ref, gt_ref, gfin_ref, wg_ref, wu_ref, wd_ref, x_hbm, o_ref, x_buf, sem):
    i = pl.program_id(0)
    f = pl.program_id(1)
    last = pl.num_programs(1) - 1
    rows = o_ref.shape[0]
    x_copy = pltpu.make_async_copy(x_hbm.at[pl.ds(pl.multiple_of(i * rows, rows), rows), :], x_buf, sem.at[0])

    def step(first, final):
        wd = wd_ref[...].astype(BF16)
        for r in range(rows // FFN_ROW_CHUNK):
            rs = slice(r * FFN_ROW_CHUNK, (r + 1) * FFN_ROW_CHUNK)
            h = h_ref[rs, :]
            a = jnp.dot(h, wg_ref[0], preferred_element_type=F32)
            u = jnp.dot(h, wu_ref[0], preferred_element_type=F32)
            act = ((a / (1.0 + jnp.exp(-a))) * u).astype(BF16)
            part = jnp.dot(act, wd, preferred_element_type=F32)
            if first:
                o_ref[rs, :] = part
            elif not final:
                o_ref[rs, :] += part
            else:
                y = x_buf[rs, :] + gt_ref[...] * (o_ref[rs, :] + part)
                y = y * lax.rsqrt(jnp.mean(y * y, axis=-1, keepdims=True) + NORM_EPS)
                o_ref[rs, :] = y * gfin_ref[...]

    @pl.when(f == 0)
    def _():
        x_copy.start()
        step(first=True, final=False)

    @pl.when((f > 0) & (f < last))
    def _():
        step(first=False, final=False)

    @pl.when(f == last)
    def _():
        x_copy.wait()
        step(first=False, final=True)


def _ffn(h, x, gt, gfin, wg, wu, wd, *, tm):
    m, d = x.shape
    nf = wg.shape[0]
    assert wg.shape == wu.shape == (nf, d, FFN_TILE) and wd.shape == (nf * FFN_TILE, d)
    assert m % tm == 0 and tm % FFN_ROW_CHUNK == 0
    row = lambda i, f: (i, 0)
    vec = pl.BlockSpec((1, d), lambda i, f: (0, 0))
    wcol = pl.BlockSpec((1, d, FFN_TILE), lambda i, f: (f, 0, 0))
    return pl.pallas_call(
        _ffn_kernel,
        grid=(m // tm, nf),
        in_specs=[pl.BlockSpec((tm, d), row), vec, vec, wcol, wcol,
                  pl.BlockSpec((FFN_TILE, d), lambda i, f: (f, 0)),
                  pl.BlockSpec(memory_space=pl.ANY)],
        out_specs=pl.BlockSpec((tm, d), row),
        out_shape=jax.ShapeDtypeStruct((m, d), F32),
        scratch_shapes=[pltpu.VMEM((tm, d), F32), pltpu.SemaphoreType.DMA((1,))],
        compiler_params=_params("arbitrary", "arbitrary"),
        name="ffn",
    )(h, gt, gfin, wg, wu, wd, x)


def _rope_tables(L):
    f32 = np.float32
    lane = np.arange(LANES)
    inv1 = f32(ROPE_BASE) ** (-np.arange(32, dtype=f32) / f32(32))
    ang1 = np.arange(L, dtype=f32)[:, None] * inv1[None, :]
    sgn1 = np.where((lane % 64) < 32, -1.0, 1.0).astype(f32)
    cos1 = np.tile(np.cos(ang1), (1, LANES // 32))
    sin1 = np.tile(np.sin(ang1), (1, LANES // 32)) * sgn1[None, :]
    inv2 = f32(ROPE_BASE) ** (-np.arange(16, dtype=f32) / f32(16))
    nrow = L // GRID_W
    ang_r = np.arange(nrow, dtype=f32)[:, None] * inv2[None, :]
    ang_c = np.arange(GRID_W, dtype=f32)[:, None] * inv2[None, :]
    sgna = np.where((lane % 32) < 16, -1.0, 1.0).astype(f32)

    def expand(fr, fc):
        by_row = np.broadcast_to(np.tile(fr, (1, 2))[:, None, :], (nrow, GRID_W, 32))
        by_col = np.broadcast_to(np.tile(fc, (1, 2))[None, :, :], (nrow, GRID_W, 32))
        head = np.concatenate([by_row, by_col], axis=-1).reshape(L, 64)
        return np.tile(head, (1, LANES // 64))

    cosa = expand(np.cos(ang_r), np.cos(ang_c))
    sina = expand(np.sin(ang_r), np.sin(ang_c)) * sgna[None, :]
    return tuple(np.ascontiguousarray(t, dtype=f32) for t in (cos1, sin1, cosa, sina))


def kernel(x, c, ctx, c_ctx, w_mod, b_mod, norm_mix, norm_ffn, w_in, ret_decay, attn_sink,
           w_out, w_gate, w_up, w_down, norm_final):
    B, L, D = x.shape
    assert B == 1 and w_mod.shape[0] == 1, "single batch element, depth-1 layer"
    x2 = x[0]
    xc2 = ctx[0]

    cv = jnp.zeros((8, D), F32).at[0].set(c[0]).at[1].set(c_ctx)
    mod = _mod(cv, w_mod[0], b_mod[0][None, :], 2 * D)
    sh_m, sc_m = mod[0:1, 0:D], mod[0:1, D:2 * D]
    sh_mc, sc_mc = mod[1:2, 0:D], mod[1:2, D:2 * D]

    g_mix = norm_mix[0][None, :]
    cproj, ckd, cvd, w_in_b = _ctx_proj(xc2, g_mix, sh_mc, sc_mc, w_in[0])
    proj, kd, vd, mod_rest = _in_proj(x2, g_mix, sh_m, sc_m, w_in_b, _rope_tables(L),
                                      c[0][:, None], w_mod[0], b_mod[0][None, :], 2 * D, tm=512)
    gt_m, sh_f, sc_f, gt_f = [mod_rest.reshape(1, 4 * D)[:, k * D:(k + 1) * D] for k in range(4)]

    dec = ret_decay[0].astype(F32)
    sb, w_out_b = _ret_bwd_states(dec, proj, cproj, [(w_out[0], 1, None)])
    y_ret = _ret_out(dec, proj, sb, cproj)
    x1, hff, w_gate_b, w_up_b = _attn_out(
        attn_sink[0].astype(F32), proj, kd, vd, ckd, cvd, y_ret, x2, w_out_b,
        gt_m, norm_ffn[0][None, :], sh_f, sc_f,
        [(w_gate[0], 1, FFN_TILE), (w_up[0], 1, FFN_TILE)])
    out = _ffn(hff, x1, gt_f, norm_final[None, :], w_gate_b, w_up_b, w_down[0], tm=1024)
    return out[None]
```

```python
import jax
import jax.numpy as jnp
import numpy as np
from jax import lax
from jax.experimental import pallas as pl
from jax.experimental.pallas import tpu as pltpu

GRID_W = 64
RET_HEADS = 8
RET_DK = 64
RET_DV = 128
RET_CHUNK = 128
ATT_HEADS = 16
ATT_KV_HEADS = 4
ATT_DH = 64
ATT_GROUP = ATT_HEADS // ATT_KV_HEADS
WINDOW = 128
ATT_BLOCK = 128
ROPE_BASE = 10000.0
NORM_EPS = 1e-6
K_SCALE = RET_DK ** -0.5
ATT_SCALE = ATT_DH ** -0.5
LOG2E = 1.4426950408889634

LANES = 128
RET_PAIRS = RET_HEADS // 2
MASK_NEG = -1e30
VMEM_LIMIT = 56 * 1024 * 1024
CAST_PIECE_ROWS = 16
RET_STEP_CHUNKS = 8
ATT_STEP_BLOCKS = 4
OUT_ROW_CHUNK = 512
IN_ROW_CHUNK = 512
FFN_TILE = 512
FFN_ROW_CHUNK = 1024

BF16 = jnp.bfloat16
F32 = jnp.float32


def _params(*sem):
    return pltpu.CompilerParams(dimension_semantics=sem, vmem_limit_bytes=VMEM_LIMIT)


def _with_cast_riders(body, n_in, n_out, n_rid):
    def wrapped(*refs):
        ins = refs[:n_in]
        rid_in = refs[n_in:n_in + n_rid]
        outs = refs[n_in + n_rid:n_in + n_rid + n_out]
        rid_out = refs[n_in + n_rid + n_out:n_in + 2 * n_rid + n_out]
        scratch = refs[n_in + 2 * n_rid + n_out:]

        def piece(src, dst, r0):
            rs = slice(r0, r0 + CAST_PIECE_ROWS)
            if len(dst.shape) == 2:
                dst[rs, :] = src[rs, :].astype(BF16)
            else:
                tc = dst.shape[2]
                for t in range(dst.shape[0]):
                    dst[t, rs, :] = src[rs, t * tc:(t + 1) * tc].astype(BF16)

        pieces = [(lambda s=src, d=dst, r=r0: piece(s, d, r))
                  for src, dst in zip(rid_in, rid_out) for r0 in range(0, src.shape[0], CAST_PIECE_ROWS)]
        done = []
        fillers = [(lambda p=p: (done.append(1), p())) for p in pieces]
        body(*ins, *outs, *scratch, fillers=fillers)
        assert len(done) == len(pieces), "every cast piece must be emitted exactly once"
    return wrapped


def _rider_specs(riders, steps):
    in_specs, out_specs, shapes = [], [], []
    for w, ncb, tile in riders:
        rows, cols = w.shape
        nrb = steps // ncb
        assert nrb * ncb == steps and rows % nrb == 0 and cols % ncb == 0
        br, bc = rows // nrb, cols // ncb
        assert br % CAST_PIECE_ROWS == 0 and bc % LANES == 0, "slab must be bf16-tile aligned"
        in_specs.append(pl.BlockSpec((br, bc), lambda i, ncb=ncb: (i // ncb, i % ncb)))
        if tile is None:
            out_specs.append(in_specs[-1])
            shapes.append(jax.ShapeDtypeStruct(w.shape, BF16))
        else:
            assert ncb == 1 and cols % tile == 0 and tile % LANES == 0
            out_specs.append(pl.BlockSpec((cols // tile, br, tile), lambda i: (0, i, 0)))
            shapes.append(jax.ShapeDtypeStruct((cols // tile, rows, tile), BF16))
    return in_specs, out_specs, shapes


def _mod_kernel(cv_ref, w_ref, b_ref, o_ref):
    cv = cv_ref[...]
    s = cv / (1.0 + jnp.exp(-cv))
    o_ref[...] = jnp.dot(s.astype(BF16), w_ref[...].astype(BF16),
                         preferred_element_type=F32) + b_ref[...]


def _mod(cv, w, b, n):
    d = w.shape[0]
    tn = 512
    assert n % tn == 0
    return pl.pallas_call(
        _mod_kernel,
        grid=(n // tn,),
        in_specs=[pl.BlockSpec((8, d), lambda j: (0, 0)),
                  pl.BlockSpec((d, tn), lambda j: (0, j)),
                  pl.BlockSpec((1, tn), lambda j: (0, j))],
        out_specs=pl.BlockSpec((8, tn), lambda j: (0, j)),
        out_shape=jax.ShapeDtypeStruct((8, n), F32),
        compiler_params=_params("parallel"),
        name="mod",
    )(cv, w, b)


def _rot_pairs(a, cos, sin_signed, half):
    lane = lax.broadcasted_iota(jnp.int32, a.shape, 1)
    first = (lane % (2 * half)) < half
    rot = jnp.where(first, pltpu.roll(a, LANES - half, 1), pltpu.roll(a, half, 1))
    return a * cos + rot * sin_signed


def _dup_halves(a):
    lane = lax.broadcasted_iota(jnp.int32, a.shape, 1)
    r = pltpu.roll(a, 64, 1)
    lo = lane < 64
    return jnp.where(lo, a, r), jnp.where(lo, r, a)


_PROJ_TILE = 512
_PROJ_TILE_KINDS = ("ret_q", "ret_k", "plain", "plain", "plain", "plain", "att_q", "att_q", "att_kv")


def _in_proj_kernel(x_ref, g_ref, sh_ref, sc_ref, w_ref, c1_ref, s1_ref, ca_ref, sa_ref,
                    cc_ref, wm_ref, bm_ref, o_ref, kd_ref, vd_ref, mod_ref):
    def mod_rider():
        cc = cc_ref[...]
        s_col = cc / (1.0 + jnp.exp(-cc))
        mod_ref[0] = jnp.sum(wm_ref[...] * s_col, axis=0, keepdims=True) + bm_ref[...]

    tn = _PROJ_TILE
    for r in range(x_ref.shape[0] // IN_ROW_CHUNK):
        rs = slice(r * IN_ROW_CHUNK, (r + 1) * IN_ROW_CHUNK)
        xf = x_ref[rs, :]
        y = xf * lax.rsqrt(jnp.mean(xf * xf, axis=-1, keepdims=True) + NORM_EPS)
        y = y * g_ref[...]
        h = (y * (1.0 + sc_ref[...]) + sh_ref[...]).astype(BF16)

        def rope1(a):
            return _rot_pairs(a, c1_ref[rs, :], s1_ref[rs, :], 32)

        def ropea(a):
            return _rot_pairs(a, ca_ref[rs, :], sa_ref[rs, :], 16)

        order = sorted(range(len(_PROJ_TILE_KINDS)), key=lambda t: _PROJ_TILE_KINDS[t] == "plain")
        for j in order:
            kind = _PROJ_TILE_KINDS[j]
            acc = jnp.dot(h, w_ref[:, j * tn:(j + 1) * tn], preferred_element_type=F32)
            if r == 0 and j == order[-2]:
                mod_rider()
            for c in range(tn // LANES):
                a = acc[:, c * LANES:(c + 1) * LANES]
                if kind == "ret_q":
                    a = rope1(a)
                elif kind == "ret_k":
                    a = rope1(a) * K_SCALE
                elif kind == "att_q":
                    a = ropea(a) * (ATT_SCALE * LOG2E)
                elif kind == "att_kv" and c < 2:
                    a = ropea(a)
                o_ref[rs, j * tn + c * LANES:j * tn + (c + 1) * LANES] = a.astype(BF16)
                if kind == "att_kv":
                    dup_ref = kd_ref if c < 2 else vd_ref
                    d0, d1 = _dup_halves(a)
                    t = 2 * (c % 2)
                    dup_ref[rs, t * LANES:(t + 1) * LANES] = d0.astype(BF16)
                    dup_ref[rs, (t + 1) * LANES:(t + 2) * LANES] = d1.astype(BF16)


def _in_proj(x, g, sh, sc, w, tabs, c_col, w_mod, b_mod, mod_done, *, tm):
    m, d = x.shape
    n = w.shape[1]
    assert n == _PROJ_TILE * len(_PROJ_TILE_KINDS) and m % tm == 0 and tm % IN_ROW_CHUNK == 0
    steps = m // tm
    slab = (w_mod.shape[1] - mod_done) // steps
    assert slab * steps == w_mod.shape[1] - mod_done and slab % LANES == 0 and mod_done % slab == 0
    slab0 = mod_done // slab
    c1, s1, ca, sa = tabs
    row = lambda i: (i, 0)
    vec = pl.BlockSpec((1, d), lambda i: (0, 0))
    tab = pl.BlockSpec((tm, LANES), row)
    return pl.pallas_call(
        _in_proj_kernel,
        grid=(m // tm,),
        in_specs=[pl.BlockSpec((tm, d), row), vec, vec, vec,
                  pl.BlockSpec((d, n), lambda i: (0, 0), pipeline_mode=pl.Buffered(1)),
                  tab, tab, tab, tab,
                  pl.BlockSpec((d, 1), lambda i: (0, 0)),
                  pl.BlockSpec((d, slab), lambda i: (0, slab0 + i)),
                  pl.BlockSpec((1, slab), lambda i: (0, slab0 + i))],
        out_specs=[pl.BlockSpec((tm, n), row),
                   pl.BlockSpec((tm, 512), row),
                   pl.BlockSpec((tm, 512), row),
                   pl.BlockSpec((1, 1, slab), lambda i: (i, 0, 0))],
        out_shape=[jax.ShapeDtypeStruct((m, n), BF16),
                   jax.ShapeDtypeStruct((m, 512), BF16),
                   jax.ShapeDtypeStruct((m, 512), BF16),
                   jax.ShapeDtypeStruct((steps, 1, slab), F32)],
        compiler_params=_params("parallel"),
        name="in_proj",
    )(x, g, sh, sc, w, c1, s1, ca, sa, c_col, w_mod, b_mod)


def _ctx_proj_kernel(x_ref, g_ref, sh_ref, sc_ref, w_ref, o_ref, kd_ref, vd_ref, wb_ref, h_ref):
    j = pl.program_id(0)

    @pl.when(j == 0)
    def _():
        xf = x_ref[...]
        y = xf * lax.rsqrt(jnp.mean(xf * xf, axis=-1, keepdims=True) + NORM_EPS)
        y = y * g_ref[...]
        h_ref[...] = (y * (1.0 + sc_ref[...]) + sh_ref[...]).astype(BF16)

    wb = w_ref[...].astype(BF16)
    wb_ref[...] = wb
    acc = jnp.dot(h_ref[...], wb, preferred_element_type=F32)
    is_ret_k = _PROJ_TILE_KINDS.index("ret_k")
    o_ref[...] = (acc * jnp.where(j == is_ret_k, K_SCALE, 1.0)).astype(BF16)

    @pl.when(j == _PROJ_TILE_KINDS.index("att_kv"))
    def _():
        for c in range(_PROJ_TILE // LANES):
            dup_ref = kd_ref if c < 2 else vd_ref
            d0, d1 = _dup_halves(acc[:, c * LANES:(c + 1) * LANES])
            t = 2 * (c % 2)
            dup_ref[:, t * LANES:(t + 1) * LANES] = d0.astype(BF16)
            dup_ref[:, (t + 1) * LANES:(t + 2) * LANES] = d1.astype(BF16)


def _ctx_proj(x, g, sh, sc, w):
    m, d = x.shape
    n = w.shape[1]
    tn = _PROJ_TILE
    assert n == tn * len(_PROJ_TILE_KINDS)
    fixed = lambda j: (0, 0)
    vec = pl.BlockSpec((1, d), fixed)
    return pl.pallas_call(
        _ctx_proj_kernel,
        grid=(n // tn,),
        in_specs=[pl.BlockSpec((m, d), fixed), vec, vec, vec,
                  pl.BlockSpec((d, tn), lambda j: (0, j))],
        out_specs=[pl.BlockSpec((m, tn), lambda j: (0, j)),
                   pl.BlockSpec((m, 512), fixed),
                   pl.BlockSpec((m, 512), fixed),
                   pl.BlockSpec((d, tn), lambda j: (0, j))],
        out_shape=[jax.ShapeDtypeStruct((m, n), BF16),
                   jax.ShapeDtypeStruct((m, 512), BF16),
                   jax.ShapeDtypeStruct((m, 512), BF16),
                   jax.ShapeDtypeStruct((d, n), BF16)],
        scratch_shapes=[pltpu.VMEM((m, d), BF16)],
        compiler_params=_params("arbitrary"),
        name="ctx_proj",
    )(x, g, sh, sc, w)


def _pair_lg(dec_ref, d, p, shape):
    lane = lax.broadcasted_iota(jnp.int32, shape, 1)
    first = (lane % LANES) < 64
    raw = jnp.where(first, jnp.full(shape, dec_ref[d, 2 * p], F32), jnp.full(shape, dec_ref[d, 2 * p + 1], F32))
    return -jnp.exp(raw)


def _head_block_mask(shape):
    r = lax.broadcasted_iota(jnp.int32, shape, 0)
    c = lax.broadcasted_iota(jnp.int32, shape, 1)
    return (r // 64) == (c // LANES)


def _kv_pair(k_pair, v_pair, w):
    kw = (k_pair.astype(F32) * w).astype(BF16)
    kv = lax.dot_general(kw, v_pair, (((0,), (0,)), ((), ())), preferred_element_type=F32)
    return jnp.where(_head_block_mask(kv.shape), kv, 0.0)


def _row_decay(dec_ref, d, p):
    shape = (LANES, 2 * RET_DV)
    rowh = lax.broadcasted_iota(jnp.int32, shape, 0) < 64
    raw = jnp.where(rowh, jnp.full(shape, dec_ref[d, 2 * p], F32), jnp.full(shape, dec_ref[d, 2 * p + 1], F32))
    return jnp.exp(-jnp.exp(raw) * float(RET_CHUNK))


def _compact_state(s):
    row = lax.broadcasted_iota(jnp.int32, (LANES, RET_DV), 0)
    return jnp.where(row < 64, s[:, :RET_DV], s[:, RET_DV:])


def _expand_state(c):
    row = lax.broadcasted_iota(jnp.int32, c.shape, 0)
    z = jnp.zeros_like(c)
    return jnp.concatenate([jnp.where(row < 64, c, z), jnp.where(row < 64, z, c)], axis=1)


def _ret_bwd_kernel(dec_ref, k_ref, v_ref, ck_ref, cv_ref, sb_ref, sbs):
    i = pl.program_id(0)
    C = RET_CHUNK
    lc = ck_ref.shape[0]

    @pl.when(i == 0)
    def _():
        pos = lax.broadcasted_iota(jnp.int32, (lc, LANES), 0).astype(F32)
        for p in range(RET_PAIRS):
            ks = slice(p * LANES, (p + 1) * LANES)
            vs = slice(p * 2 * RET_DV, (p + 1) * 2 * RET_DV)
            wb = jnp.exp(_pair_lg(dec_ref, 1, p, (lc, LANES)) * pos)
            sbs[p] = _kv_pair(ck_ref[:, ks], cv_ref[:, vs], wb)

    pos = lax.broadcasted_iota(jnp.int32, (C, LANES), 0).astype(F32)
    for p in range(RET_PAIRS):
        ks = slice(p * LANES, (p + 1) * LANES)
        vs = slice(p * 2 * RET_DV, (p + 1) * 2 * RET_DV)
        wb = jnp.exp(_pair_lg(dec_ref, 1, p, (C, LANES)) * pos)
        gb = _row_decay(dec_ref, 1, p)
        sb = sbs[p]
        for cc in reversed(range(RET_STEP_CHUNKS)):
            rs = slice(cc * C, (cc + 1) * C)
            sb_ref[cc, p] = _compact_state(sb).astype(BF16)
            sb = gb * sb + _kv_pair(k_ref[rs, ks], v_ref[rs, vs], wb)
        sbs[p] = sb


def _ret_bwd_states(dec, proj, cproj):
    L = proj.shape[0]
    lc = cproj.shape[0]
    S = RET_STEP_CHUNKS
    R = S * RET_CHUNK
    n = L // R
    return pl.pallas_call(
        _ret_bwd_kernel,
        grid=(n,),
        in_specs=[pl.BlockSpec(memory_space=pltpu.SMEM),
                  pl.BlockSpec((R, 512), lambda i: (n - 1 - i, 1)),
                  pl.BlockSpec((R, 1024), lambda i: (n - 1 - i, 1)),
                  pl.BlockSpec((lc, 512), lambda i: (0, 1)),
                  pl.BlockSpec((lc, 1024), lambda i: (0, 1))],
        out_specs=pl.BlockSpec((S, RET_PAIRS, LANES, RET_DV), lambda i: (n - 1 - i, 0, 0, 0)),
        out_shape=jax.ShapeDtypeStruct((n * S, RET_PAIRS, LANES, RET_DV), BF16),
        scratch_shapes=[pltpu.VMEM((RET_PAIRS, LANES, 2 * RET_DV), F32)],
        compiler_params=_params("arbitrary"),
        name="ret_bwd",
    )(dec, proj, proj, cproj, cproj)


def _ret_out_kernel(dec_ref, q_ref, k_ref, v_ref, g_ref, sb_ref, ck_ref, cv_ref, o_ref, sfs):
    i = pl.program_id(0)
    C = RET_CHUNK
    lc = ck_ref.shape[0]

    @pl.when(i == 0)
    def _():
        cpos = lax.broadcasted_iota(jnp.int32, (lc, LANES), 0).astype(F32)
        for p in range(RET_PAIRS):
            ks = slice(p * LANES, (p + 1) * LANES)
            vs = slice(p * 2 * RET_DV, (p + 1) * 2 * RET_DV)
            wf = jnp.exp(_pair_lg(dec_ref, 0, p, (lc, LANES)) * (lc - 1.0 - cpos))
            sfs[p] = _kv_pair(ck_ref[:, ks], cv_ref[:, vs], wf)

    pos = lax.broadcasted_iota(jnp.int32, (C, LANES), 0).astype(F32)
    n_i = lax.broadcasted_iota(jnp.int32, (C, 2 * C), 0)
    m_i = lax.broadcasted_iota(jnp.int32, (C, 2 * C), 1) % C
    rel = (n_i - m_i).astype(F32)
    lane = lax.broadcasted_iota(jnp.int32, (C, LANES), 1)
    lo = lane < 64
    for p in range(RET_PAIRS):
        ks = slice(p * LANES, (p + 1) * LANES)
        vs = slice(p * 2 * RET_DV, (p + 1) * 2 * RET_DV)
        col_a = lax.broadcasted_iota(jnp.int32, (C, 2 * C), 1) < C
        raw_f = jnp.where(col_a, jnp.full((C, 2 * C), dec_ref[0, 2 * p], F32), jnp.full((C, 2 * C), dec_ref[0, 2 * p + 1], F32))
        raw_b = jnp.where(col_a, jnp.full((C, 2 * C), dec_ref[1, 2 * p], F32), jnp.full((C, 2 * C), dec_ref[1, 2 * p + 1], F32))
        dmat = jnp.where(rel >= 0, jnp.exp(-jnp.exp(raw_f) * jnp.maximum(rel, 0.0)),
                         jnp.exp(-jnp.exp(raw_b) * jnp.maximum(-rel, 0.0)))
        lg_f = _pair_lg(dec_ref, 0, p, (C, LANES))
        wqf = jnp.exp(lg_f * (pos + 1.0))
        wqb = jnp.exp(_pair_lg(dec_ref, 1, p, (C, LANES)) * (float(C) - pos))
        wkf = jnp.exp(lg_f * (C - 1.0 - pos))
        gf = _row_decay(dec_ref, 0, p)
        sf = sfs[p]
        for cc in range(RET_STEP_CHUNKS):
            rs = slice(cc * C, (cc + 1) * C)
            q = q_ref[rs, ks]
            k = k_ref[rs, ks]
            v = v_ref[rs, vs]
            zk = jnp.zeros_like(k)
            kst = jnp.concatenate([jnp.where(lo, k, zk), jnp.where(lo, zk, k)], axis=0)
            s = lax.dot_general(q, kst, (((1,), (1,)), ((), ())), preferred_element_type=F32)
            sd = (s * dmat).astype(BF16)
            qf32 = q.astype(F32)
            qwf = (qf32 * wqf).astype(BF16)
            qwb = (qf32 * wqb).astype(BF16)
            zv = jnp.zeros((C, RET_DV), BF16)
            vbd = jnp.concatenate([jnp.concatenate([v[:, :RET_DV], zv], axis=1),
                                   jnp.concatenate([zv, v[:, RET_DV:]], axis=1)], axis=0)
            lhs = jnp.concatenate([sd, qwf, qwb], axis=1)
            rhs = jnp.concatenate([vbd, sf.astype(BF16), _expand_state(sb_ref[cc, p])], axis=0)
            o = jnp.dot(lhs, rhs, preferred_element_type=F32)
            sf = gf * sf + _kv_pair(k, v, wkf)
            for t in range(2):
                oh = o[:, t * RET_DV:(t + 1) * RET_DV]
                oh = oh * lax.rsqrt(jnp.mean(oh * oh, axis=-1, keepdims=True) + NORM_EPS)
                cs = slice(p * 2 * RET_DV + t * RET_DV, p * 2 * RET_DV + (t + 1) * RET_DV)
                gt = g_ref[rs, cs].astype(F32)
                o_ref[rs, cs] = (oh * (gt / (1.0 + jnp.exp(-gt)))).astype(BF16)
        sfs[p] = sf


def _ret_out(dec, proj, sb, cproj):
    L = proj.shape[0]
    lc = cproj.shape[0]
    S = RET_STEP_CHUNKS
    R = S * RET_CHUNK
    n = L // R
    return pl.pallas_call(
        _ret_out_kernel,
        grid=(n,),
        in_specs=[pl.BlockSpec(memory_space=pltpu.SMEM),
                  pl.BlockSpec((R, 512), lambda i: (i, 0)),
                  pl.BlockSpec((R, 512), lambda i: (i, 1)),
                  pl.BlockSpec((R, 1024), lambda i: (i, 1)),
                  pl.BlockSpec((R, 1024), lambda i: (i, 2)),
                  pl.BlockSpec((S, RET_PAIRS, LANES, RET_DV), lambda i: (i, 0, 0, 0)),
                  pl.BlockSpec((lc, 512), lambda i: (0, 1)),
                  pl.BlockSpec((lc, 1024), lambda i: (0, 1))],
        out_specs=pl.BlockSpec((R, RET_HEADS * RET_DV), lambda i: (i, 0)),
        out_shape=jax.ShapeDtypeStruct((L, RET_HEADS * RET_DV), BF16),
        scratch_shapes=[pltpu.VMEM((RET_PAIRS, LANES, 2 * RET_DV), F32)],
        compiler_params=_params("arbitrary"),
        name="ret_out",
    )(dec, proj, proj, proj, proj, sb, cproj, cproj)


def _attn_kernel(sink_ref, q_ref, kp_ref, kc_ref, kn_ref, vp_ref, vc_ref, vn_ref, ck_ref, cv_ref, o_ref, *, fillers):
    n = pl.program_id(0)
    nstep = pl.num_programs(0)
    B = ATT_BLOCK
    SB = ATT_STEP_BLOCKS
    kj = lax.broadcasted_iota(jnp.int32, (B, B), 0)
    qi = lax.broadcasted_iota(jnp.int32, (B, B), 1)
    ok_prev = jnp.where(n > 0, 0.0, MASK_NEG).astype(F32)
    ok_next = jnp.where(n < nstep - 1, 0.0, MASK_NEG).astype(F32)

    def band(inside, ok):
        return jnp.concatenate([jnp.where(inside, ok, MASK_NEG).astype(F32)] * ATT_GROUP, axis=1)

    bias_prev = [band(kj >= qi, ok_prev if j == 0 else 0.0) for j in range(SB)]
    bias_next = [band(kj <= qi, ok_next if j == SB - 1 else 0.0) for j in range(SB)]
    lane = lax.broadcasted_iota(jnp.int32, (B, LANES), 1)
    lo = lane < 64
    hi = lane >= 64

    def keys_of(j, gs, prev_ref, cur_ref, next_ref, ctx_ref):
        prev = prev_ref[:, gs] if j == 0 else cur_ref[(j - 1) * B:j * B, gs]
        nxt = next_ref[:, gs] if j == SB - 1 else cur_ref[(j + 1) * B:(j + 2) * B, gs]
        return jnp.concatenate([prev, cur_ref[j * B:(j + 1) * B, gs], nxt, ctx_ref[:, gs]], axis=0)

    def scores(j, g):
        gs = slice(g * LANES, (g + 1) * LANES)
        kcat = keys_of(j, gs, kp_ref, kc_ref, kn_ref, ck_ref)
        qs = []
        for r in range(ATT_GROUP):
            h = ATT_GROUP * g + r
            qt = q_ref[j * B:(j + 1) * B, (h // 2) * LANES:(h // 2 + 1) * LANES]
            keep = lo if h % 2 == 0 else hi
            qs.append(jnp.where(keep, qt, jnp.zeros_like(qt)))
        q4 = jnp.concatenate(qs, axis=0)
        return lax.dot_general(kcat, q4, (((1,), (1,)), ((), ())), preferred_element_type=F32)

    def softmax(j, g, s):
        sk = jnp.concatenate([jnp.full((1, B), sink_ref[ATT_GROUP * g + r], F32)
                              for r in range(ATT_GROUP)], axis=1) * LOG2E
        s = jnp.concatenate([s[:B] + bias_prev[j], s[B:2 * B], s[2 * B:3 * B] + bias_next[j], s[3 * B:]], axis=0)
        m = jnp.maximum(jnp.max(s, axis=0, keepdims=True), sk)
        e = jnp.exp2(s - m)
        den = jnp.sum(e, axis=0, keepdims=True) + jnp.exp2(sk - m)
        return e.astype(BF16), den

    def values(j, g, e, den):
        gs = slice(g * LANES, (g + 1) * LANES)
        vcat = keys_of(j, gs, vp_ref, vc_ref, vn_ref, cv_ref)
        res = lax.dot_general(vcat, e, (((0,), (0,)), ((), ())), preferred_element_type=F32) * (1.0 / den)
        for t in range(2):
            even = res[:, (2 * t) * B:(2 * t + 1) * B].T
            odd = res[:, (2 * t + 1) * B:(2 * t + 2) * B].T
            c0 = (2 * g + t) * LANES
            o_ref[j * B:(j + 1) * B, c0:c0 + LANES] = jnp.where(lo, even, odd).astype(BF16)

    units = [(j, g) for j in range(SB) for g in range(ATT_KV_HEADS)]
    per_unit = -(-len(fillers) // len(units))
    s_next = scores(*units[0])
    pending = None
    for u, unit in enumerate(units):
        s_cur = s_next
        if u + 1 < len(units):
            s_next = scores(*units[u + 1])
        e_den = softmax(*unit, s_cur)
        for fill in fillers[u * per_unit:(u + 1) * per_unit]:
            fill()
        if pending is not None:
            values(*units[u - 1], *pending)
        pending = e_den
    values(*units[-1], *pending)


def _attn(sink, proj, kd, vd, ckd, cvd, riders):
    L = proj.shape[0]
    B = ATT_BLOCK
    SB = ATT_STEP_BLOCKS
    n = L // (SB * B)
    nb = L // B
    lc = ckd.shape[0]
    prev = pl.BlockSpec((B, 512), lambda i: (jnp.maximum(i * SB - 1, 0), 0))
    cur = pl.BlockSpec((SB * B, 512), lambda i: (i, 0))
    nxt = pl.BlockSpec((B, 512), lambda i: (jnp.minimum((i + 1) * SB, nb - 1), 0))
    full = pl.BlockSpec((lc, 512), lambda i: (0, 0))
    rid_in_specs, rid_out_specs, rid_shapes = _rider_specs(riders, n)
    return pl.pallas_call(
        _with_cast_riders(_attn_kernel, 10, 1, len(riders)),
        grid=(n,),
        in_specs=[pl.BlockSpec(memory_space=pltpu.SMEM),
                  pl.BlockSpec((SB * B, 1024), lambda i: (i, 3)),
                  prev, cur, nxt, prev, cur, nxt, full, full] + rid_in_specs,
        out_specs=[pl.BlockSpec((SB * B, ATT_HEADS * ATT_DH), lambda i: (i, 0))] + rid_out_specs,
        out_shape=[jax.ShapeDtypeStruct((L, ATT_HEADS * ATT_DH), BF16)] + rid_shapes,
        compiler_params=_params("parallel"),
        name="attn",
    )(sink, proj, kd, kd, kd, vd, vd, vd, ckd, cvd, *[r[0] for r in riders])


def _out_proj_kernel(yr_ref, ya_ref, w_ref, x_ref, gt_ref, g_ref, sh_ref, sc_ref, o_ref, h_ref):
    kr = yr_ref.shape[1]
    for r in range(yr_ref.shape[0] // OUT_ROW_CHUNK):
        rs = slice(r * OUT_ROW_CHUNK, (r + 1) * OUT_ROW_CHUNK)
        acc = jnp.dot(yr_ref[rs, :], w_ref[:kr, :], preferred_element_type=F32)
        acc = acc + jnp.dot(ya_ref[rs, :], w_ref[kr:, :], preferred_element_type=F32)
        x1 = x_ref[rs, :] + gt_ref[...] * acc
        o_ref[rs, :] = x1
        y = x1 * lax.rsqrt(jnp.mean(x1 * x1, axis=-1, keepdims=True) + NORM_EPS)
        y = y * g_ref[...]
        h_ref[rs, :] = (y * (1.0 + sc_ref[...]) + sh_ref[...]).astype(BF16)


def _out_proj(yr, ya, w, x, gt, g, sh, sc, *, tm):
    m, d = x.shape
    kr, ka = yr.shape[1], ya.shape[1]
    row = lambda i: (i, 0)
    vec = pl.BlockSpec((1, d), lambda i: (0, 0))
    return pl.pallas_call(
        _out_proj_kernel,
        grid=(m // tm,),
        in_specs=[pl.BlockSpec((tm, kr), row), pl.BlockSpec((tm, ka), row),
                  pl.BlockSpec((kr + ka, d), lambda i: (0, 0)),
                  pl.BlockSpec((tm, d), row), vec, vec, vec, vec],
        out_specs=[pl.BlockSpec((tm, d), row), pl.BlockSpec((tm, d), row)],
        out_shape=[jax.ShapeDtypeStruct((m, d), F32), jax.ShapeDtypeStruct((m, d), BF16)],
        compiler_params=_params("parallel"),
        name="out_proj",
    )(yr, ya, w, x, gt, g, sh, sc)


def _ffn_kernel(h_ref, gt_ref, gfin_ref, wg_ref, wu_ref, wd_ref, x_hbm, o_ref, x_buf, sem):
    i = pl.program_id(0)
    f = pl.program_id(1)
    last = pl.num_programs(1) - 1
    rows = o_ref.shape[0]
    x_copy = pltpu.make_async_copy(x_hbm.at[pl.ds(pl.multiple_of(i * rows, rows), rows), :], x_buf, sem.at[0])

    def step(first, final):
        wd = wd_ref[...].astype(BF16)
        for r in range(rows // FFN_ROW_CHUNK):
            rs = slice(r * FFN_ROW_CHUNK, (r + 1) * FFN_ROW_CHUNK)
            h = h_ref[rs, :]
            a = jnp.dot(h, wg_ref[0], preferred_element_type=F32)
            u = jnp.dot(h, wu_ref[0], preferred_element_type=F32)
            act = ((a / (1.0 + jnp.exp(-a))) * u).astype(BF16)
            part = jnp.dot(act, wd, preferred_element_type=F32)
            if first:
                o_ref[rs, :] = part
            elif not final:
                o_ref[rs, :] += part
            else:
                y = x_buf[rs, :] + gt_ref[...] * (o_ref[rs, :] + part)
                y = y * lax.rsqrt(jnp.mean(y * y, axis=-1, keepdims=True) + NORM_EPS)
                o_ref[rs, :] = y * gfin_ref[...]

    @pl.when(f == 0)
    def _():
        x_copy.start()
        step(first=True, final=False)

    @pl.when((f > 0) & (f < last))
    def _():
        step(first=False, final=False)

    @pl.when(f == last)
    def _():
        x_copy.wait()
        step(first=False, final=True)


def _ffn(h, x, gt, gfin, wg, wu, wd, *, tm):
    m, d = x.shape
    nf = wg.shape[0]
    assert wg.shape == wu.shape == (nf, d, FFN_TILE) and wd.shape == (nf * FFN_TILE, d)
    assert m % tm == 0 and tm % FFN_ROW_CHUNK == 0
    row = lambda i, f: (i, 0)
    vec = pl.BlockSpec((1, d), lambda i, f: (0, 0))
    wcol = pl.BlockSpec((1, d, FFN_TILE), lambda i, f: (f, 0, 0))
    return pl.pallas_call(
        _ffn_kernel,
        grid=(m // tm, nf),
        in_specs=[pl.BlockSpec((tm, d), row), vec, vec, wcol, wcol,
                  pl.BlockSpec((FFN_TILE, d), lambda i, f: (f, 0)),
                  pl.BlockSpec(memory_space=pl.ANY)],
        out_specs=pl.BlockSpec((tm, d), row),
        out_shape=jax.ShapeDtypeStruct((m, d), F32),
        scratch_shapes=[pltpu.VMEM((tm, d), F32), pltpu.SemaphoreType.DMA((1,))],
        compiler_params=_params("arbitrary", "arbitrary"),
        name="ffn",
    )(h, gt, gfin, wg, wu, wd, x)


def _rope_tables(L):
    f32 = np.float32
    lane = np.arange(LANES)
    inv1 = f32(ROPE_BASE) ** (-np.arange(32, dtype=f32) / f32(32))
    ang1 = np.arange(L, dtype=f32)[:, None] * inv1[None, :]
    sgn1 = np.where((lane % 64) < 32, -1.0, 1.0).astype(f32)
    cos1 = np.tile(np.cos(ang1), (1, LANES // 32))
    sin1 = np.tile(np.sin(ang1), (1, LANES // 32)) * sgn1[None, :]
    inv2 = f32(ROPE_BASE) ** (-np.arange(16, dtype=f32) / f32(16))
    nrow = L // GRID_W
    ang_r = np.arange(nrow, dtype=f32)[:, None] * inv2[None, :]
    ang_c = np.arange(GRID_W, dtype=f32)[:, None] * inv2[None, :]
    sgna = np.where((lane % 32) < 16, -1.0, 1.0).astype(f32)

    def expand(fr, fc):
        by_row = np.broadcast_to(np.tile(fr, (1, 2))[:, None, :], (nrow, GRID_W, 32))
        by_col = np.broadcast_to(np.tile(fc, (1, 2))[None, :, :], (nrow, GRID_W, 32))
        head = np.concatenate([by_row, by_col], axis=-1).reshape(L, 64)
        return np.tile(head, (1, LANES // 64))

    cosa = expand(np.cos(ang_r), np.cos(ang_c))
    sina = expand(np.sin(ang_r), np.sin(ang_c)) * sgna[None, :]
    return tuple(np.ascontiguousarray(t, dtype=f32) for t in (cos1, sin1, cosa, sina))


def kernel(x, c, ctx, c_ctx, w_mod, b_mod, norm_mix, norm_ffn, w_in, ret_decay, attn_sink,
           w_out, w_gate, w_up, w_down, norm_final):
    B, L, D = x.shape
    assert B == 1 and w_mod.shape[0] == 1, "single batch element, depth-1 layer"
    x2 = x[0]
    xc2 = ctx[0]

    cv = jnp.zeros((8, D), F32).at[0].set(c[0]).at[1].set(c_ctx)
    mod = _mod(cv, w_mod[0], b_mod[0][None, :], 2 * D)
    sh_m, sc_m = mod[0:1, 0:D], mod[0:1, D:2 * D]
    sh_mc, sc_mc = mod[1:2, 0:D], mod[1:2, D:2 * D]

    g_mix = norm_mix[0][None, :]
    cproj, ckd, cvd, w_in_b = _ctx_proj(xc2, g_mix, sh_mc, sc_mc, w_in[0])
    proj, kd, vd, mod_rest = _in_proj(x2, g_mix, sh_m, sc_m, w_in_b, _rope_tables(L),
                                      c[0][:, None], w_mod[0], b_mod[0][None, :], 2 * D, tm=512)
    gt_m, sh_f, sc_f, gt_f = [mod_rest.reshape(1, 4 * D)[:, k * D:(k + 1) * D] for k in range(4)]

    dec = ret_decay[0].astype(F32)
    sb = _ret_bwd_states(dec, proj, cproj)
    y_ret = _ret_out(dec, proj, sb, cproj)
    y_att, w_gate_b, w_up_b, w_out_b = _attn(
        attn_sink[0].astype(F32), proj, kd, vd, ckd, cvd,
        [(w_gate[0], 1, FFN_TILE), (w_up[0], 1, FFN_TILE), (w_out[0], 1, None)])

    x1, hff = _out_proj(y_ret, y_att, w_out_b, x2, gt_m, norm_ffn[0][None, :], sh_f, sc_f, tm=512)
    out = _ffn(hff, x1, gt_f, norm_final[None, :], w_gate_b, w_up_b, w_down[0], tm=1024)
    return out[None]
```

```python
import jax
import jax.numpy as jnp
import numpy as np
from jax import lax
from jax.experimental import pallas as pl
from jax.experimental.pallas import tpu as pltpu

GRID_W = 64
RET_HEADS = 8
RET_DK = 64
RET_DV = 128
RET_CHUNK = 128
ATT_HEADS = 16
ATT_KV_HEADS = 4
ATT_DH = 64
ATT_GROUP = ATT_HEADS // ATT_KV_HEADS
WINDOW = 128
ATT_BLOCK = 128
ROPE_BASE = 10000.0
NORM_EPS = 1e-6
K_SCALE = RET_DK ** -0.5
ATT_SCALE = ATT_DH ** -0.5
LOG2E = 1.4426950408889634

RET_QK_COLS = RET_HEADS * RET_DK
RET_V_COLS = RET_HEADS * RET_DV
ATT_Q_COLS = ATT_HEADS * ATT_DH
KV_DUP_COLS = 2 * ATT_KV_HEADS * ATT_DH

LANES = 128
RET_PAIRS = RET_HEADS // 2
MASK_NEG = -1e30
VMEM_LIMIT = 56 * 1024 * 1024
CAST_PIECE_ROWS = 16
RET_STEP_CHUNKS = 8
ATT_STEP_BLOCKS = 4
ROW_TILE = 512
OUT_ROW_CHUNK = ROW_TILE
IN_ROW_CHUNK = ROW_TILE
FFN_TILE = 512
FFN_ROW_TILE = 1024
FFN_ROW_CHUNK = FFN_ROW_TILE

BF16 = jnp.bfloat16
F32 = jnp.float32


def _params(*sem):
    return pltpu.CompilerParams(dimension_semantics=sem, vmem_limit_bytes=VMEM_LIMIT)


def _with_cast_riders(body, n_in, n_out, n_rid):
    def wrapped(*refs):
        ins = refs[:n_in]
        rid_in = refs[n_in:n_in + n_rid]
        outs = refs[n_in + n_rid:n_in + n_rid + n_out]
        rid_out = refs[n_in + n_rid + n_out:n_in + 2 * n_rid + n_out]
        scratch = refs[n_in + 2 * n_rid + n_out:]

        def piece(src, dst, r0):
            rs = slice(r0, r0 + CAST_PIECE_ROWS)
            if len(dst.shape) == 2:
                dst[rs, :] = src[rs, :].astype(BF16)
            else:
                tc = dst.shape[2]
                for t in range(dst.shape[0]):
                    dst[t, rs, :] = src[rs, t * tc:(t + 1) * tc].astype(BF16)

        pieces = [(lambda s=src, d=dst, r=r0: piece(s, d, r))
                  for src, dst in zip(rid_in, rid_out) for r0 in range(0, src.shape[0], CAST_PIECE_ROWS)]
        done = []
        fillers = [(lambda p=p: (done.append(1), p())) for p in pieces]
        body(*ins, *outs, *scratch, fillers=fillers)
        assert len(done) == len(pieces), "every cast piece must be emitted exactly once"
    return wrapped


def _rider_specs(riders, steps):
    in_specs, out_specs, shapes = [], [], []
    for w, ncb, tile in riders:
        rows, cols = w.shape
        nrb = steps // ncb
        assert nrb * ncb == steps and rows % nrb == 0 and cols % ncb == 0
        br, bc = rows // nrb, cols // ncb
        assert br % CAST_PIECE_ROWS == 0 and bc % LANES == 0, "slab must be bf16-tile aligned"
        in_specs.append(pl.BlockSpec((br, bc), lambda i, ncb=ncb: (i // ncb, i % ncb)))
        if tile is None:
            out_specs.append(in_specs[-1])
            shapes.append(jax.ShapeDtypeStruct(w.shape, BF16))
        else:
            assert ncb == 1 and cols % tile == 0 and tile % LANES == 0
            out_specs.append(pl.BlockSpec((cols // tile, br, tile), lambda i: (0, i, 0)))
            shapes.append(jax.ShapeDtypeStruct((cols // tile, rows, tile), BF16))
    return in_specs, out_specs, shapes


def _mod_kernel(cv_ref, w_ref, b_ref, o_ref):
    cv = cv_ref[...]
    s = cv / (1.0 + jnp.exp(-cv))
    o_ref[...] = jnp.dot(s.astype(BF16), w_ref[...].astype(BF16),
                         preferred_element_type=F32) + b_ref[...]


def _mod(cv, w, b, n):
    d = w.shape[0]
    tn = 512
    assert n % tn == 0
    return pl.pallas_call(
        _mod_kernel,
        grid=(n // tn,),
        in_specs=[pl.BlockSpec((8, d), lambda j: (0, 0)),
                  pl.BlockSpec((d, tn), lambda j: (0, j)),
                  pl.BlockSpec((1, tn), lambda j: (0, j))],
        out_specs=pl.BlockSpec((8, tn), lambda j: (0, j)),
        out_shape=jax.ShapeDtypeStruct((8, n), F32),
        compiler_params=_params("parallel"),
        name="mod",
    )(cv, w, b)


def _rot_pairs(a, cos, sin_signed, half):
    lane = lax.broadcasted_iota(jnp.int32, a.shape, 1)
    first = (lane % (2 * half)) < half
    rot = jnp.where(first, pltpu.roll(a, LANES - half, 1), pltpu.roll(a, half, 1))
    return a * cos + rot * sin_signed


def _dup_halves(a):
    lane = lax.broadcasted_iota(jnp.int32, a.shape, 1)
    r = pltpu.roll(a, 64, 1)
    lo = lane < 64
    return jnp.where(lo, a, r), jnp.where(lo, r, a)


_PROJ_TILE = 512
_PROJ_TILE_KINDS = ("ret_q", "ret_k", "plain", "plain", "plain", "plain", "att_q", "att_q", "att_kv")


def _in_proj_kernel(x_ref, g_ref, sh_ref, sc_ref, w_ref, c1_ref, s1_ref, ca_ref, sa_ref,
                    cc_ref, wm_ref, bm_ref, o_ref, kd_ref, vd_ref, mod_ref):
    def mod_rider():
        cc = cc_ref[...]
        s_col = cc / (1.0 + jnp.exp(-cc))
        mod_ref[0] = jnp.sum(wm_ref[...] * s_col, axis=0, keepdims=True) + bm_ref[...]

    tn = _PROJ_TILE
    for r in range(x_ref.shape[0] // IN_ROW_CHUNK):
        rs = slice(r * IN_ROW_CHUNK, (r + 1) * IN_ROW_CHUNK)
        xf = x_ref[rs, :]
        y = xf * lax.rsqrt(jnp.mean(xf * xf, axis=-1, keepdims=True) + NORM_EPS)
        y = y * g_ref[...]
        h = (y * (1.0 + sc_ref[...]) + sh_ref[...]).astype(BF16)

        def rope1(a):
            return _rot_pairs(a, c1_ref[rs, :], s1_ref[rs, :], 32)

        def ropea(a):
            return _rot_pairs(a, ca_ref[rs, :], sa_ref[rs, :], 16)

        order = sorted(range(len(_PROJ_TILE_KINDS)), key=lambda t: _PROJ_TILE_KINDS[t] == "plain")
        for j in order:
            kind = _PROJ_TILE_KINDS[j]
            acc = jnp.dot(h, w_ref[:, j * tn:(j + 1) * tn], preferred_element_type=F32)
            if r == 0 and j == order[-2]:
                mod_rider()
            for c in range(tn // LANES):
                a = acc[:, c * LANES:(c + 1) * LANES]
                if kind == "ret_q":
                    a = rope1(a)
                elif kind == "ret_k":
                    a = rope1(a) * K_SCALE
                elif kind == "att_q":
                    a = ropea(a) * (ATT_SCALE * LOG2E)
                elif kind == "att_kv" and c < 2:
                    a = ropea(a)
                o_ref[rs, j * tn + c * LANES:j * tn + (c + 1) * LANES] = a.astype(BF16)
                if kind == "att_kv":
                    dup_ref = kd_ref if c < 2 else vd_ref
                    d0, d1 = _dup_halves(a)
                    t = 2 * (c % 2)
                    dup_ref[rs, t * LANES:(t + 1) * LANES] = d0.astype(BF16)
                    dup_ref[rs, (t + 1) * LANES:(t + 2) * LANES] = d1.astype(BF16)


def _in_proj(x, g, sh, sc, w, tabs, c_col, w_mod, b_mod, mod_done, *, tm):
    m, d = x.shape
    n = w.shape[1]
    assert n == _PROJ_TILE * len(_PROJ_TILE_KINDS) and m % tm == 0 and tm % IN_ROW_CHUNK == 0
    steps = m // tm
    slab = (w_mod.shape[1] - mod_done) // steps
    assert slab * steps == w_mod.shape[1] - mod_done and slab % LANES == 0 and mod_done % slab == 0
    slab0 = mod_done // slab
    c1, s1, ca, sa = tabs
    row = lambda i: (i, 0)
    vec = pl.BlockSpec((1, d), lambda i: (0, 0))
    tab = pl.BlockSpec((tm, LANES), row)
    return pl.pallas_call(
        _in_proj_kernel,
        grid=(m // tm,),
        in_specs=[pl.BlockSpec((tm, d), row), vec, vec, vec,
                  pl.BlockSpec((d, n), lambda i: (0, 0), pipeline_mode=pl.Buffered(1)),
                  tab, tab, tab, tab,
                  pl.BlockSpec((d, 1), lambda i: (0, 0)),
                  pl.BlockSpec((d, slab), lambda i: (0, slab0 + i)),
                  pl.BlockSpec((1, slab), lambda i: (0, slab0 + i))],
        out_specs=[pl.BlockSpec((tm, n), row),
                   pl.BlockSpec((tm, KV_DUP_COLS), row),
                   pl.BlockSpec((tm, KV_DUP_COLS), row),
                   pl.BlockSpec((1, 1, slab), lambda i: (i, 0, 0))],
        out_shape=[jax.ShapeDtypeStruct((m, n), BF16),
                   jax.ShapeDtypeStruct((m, KV_DUP_COLS), BF16),
                   jax.ShapeDtypeStruct((m, KV_DUP_COLS), BF16),
                   jax.ShapeDtypeStruct((steps, 1, slab), F32)],
        compiler_params=_params("parallel"),
        name="in_proj",
    )(x, g, sh, sc, w, c1, s1, ca, sa, c_col, w_mod, b_mod)


def _ctx_proj_kernel(x_ref, g_ref, sh_ref, sc_ref, w_ref, o_ref, kd_ref, vd_ref, wb_ref, h_ref):
    j = pl.program_id(0)

    @pl.when(j == 0)
    def _():
        xf = x_ref[...]
        y = xf * lax.rsqrt(jnp.mean(xf * xf, axis=-1, keepdims=True) + NORM_EPS)
        y = y * g_ref[...]
        h_ref[...] = (y * (1.0 + sc_ref[...]) + sh_ref[...]).astype(BF16)

    wb = w_ref[...].astype(BF16)
    wb_ref[...] = wb
    acc = jnp.dot(h_ref[...], wb, preferred_element_type=F32)
    is_ret_k = _PROJ_TILE_KINDS.index("ret_k")
    o_ref[...] = (acc * jnp.where(j == is_ret_k, K_SCALE, 1.0)).astype(BF16)

    @pl.when(j == _PROJ_TILE_KINDS.index("att_kv"))
    def _():
        for c in range(_PROJ_TILE // LANES):
            dup_ref = kd_ref if c < 2 else vd_ref
            d0, d1 = _dup_halves(acc[:, c * LANES:(c + 1) * LANES])
            t = 2 * (c % 2)
            dup_ref[:, t * LANES:(t + 1) * LANES] = d0.astype(BF16)
            dup_ref[:, (t + 1) * LANES:(t + 2) * LANES] = d1.astype(BF16)


def _ctx_proj(x, g, sh, sc, w):
    m, d = x.shape
    n = w.shape[1]
    tn = _PROJ_TILE
    assert n == tn * len(_PROJ_TILE_KINDS)
    fixed = lambda j: (0, 0)
    vec = pl.BlockSpec((1, d), fixed)
    return pl.pallas_call(
        _ctx_proj_kernel,
        grid=(n // tn,),
        in_specs=[pl.BlockSpec((m, d), fixed), vec, vec, vec,
                  pl.BlockSpec((d, tn), lambda j: (0, j))],
        out_specs=[pl.BlockSpec((m, tn), lambda j: (0, j)),
                   pl.BlockSpec((m, KV_DUP_COLS), fixed),
                   pl.BlockSpec((m, KV_DUP_COLS), fixed),
                   pl.BlockSpec((d, tn), lambda j: (0, j))],
        out_shape=[jax.ShapeDtypeStruct((m, n), BF16),
                   jax.ShapeDtypeStruct((m, KV_DUP_COLS), BF16),
                   jax.ShapeDtypeStruct((m, KV_DUP_COLS), BF16),
                   jax.ShapeDtypeStruct((d, n), BF16)],
        scratch_shapes=[pltpu.VMEM((m, d), BF16)],
        compiler_params=_params("arbitrary"),
        name="ctx_proj",
    )(x, g, sh, sc, w)


def _pair_lg(dec_ref, d, p, shape):
    lane = lax.broadcasted_iota(jnp.int32, shape, 1)
    first = (lane % LANES) < 64
    raw = jnp.where(first, jnp.full(shape, dec_ref[d, 2 * p], F32), jnp.full(shape, dec_ref[d, 2 * p + 1], F32))
    return -jnp.exp(raw)


def _head_block_mask(shape):
    r = lax.broadcasted_iota(jnp.int32, shape, 0)
    c = lax.broadcasted_iota(jnp.int32, shape, 1)
    return (r // 64) == (c // LANES)


def _kv_pair(k_pair, v_pair, w):
    kw = (k_pair.astype(F32) * w).astype(BF16)
    kv = lax.dot_general(kw, v_pair, (((0,), (0,)), ((), ())), preferred_element_type=F32)
    return jnp.where(_head_block_mask(kv.shape), kv, 0.0)


def _row_decay(dec_ref, d, p):
    shape = (LANES, 2 * RET_DV)
    rowh = lax.broadcasted_iota(jnp.int32, shape, 0) < 64
    raw = jnp.where(rowh, jnp.full(shape, dec_ref[d, 2 * p], F32), jnp.full(shape, dec_ref[d, 2 * p + 1], F32))
    return jnp.exp(-jnp.exp(raw) * float(RET_CHUNK))


def _compact_state(s):
    row = lax.broadcasted_iota(jnp.int32, (LANES, RET_DV), 0)
    return jnp.where(row < 64, s[:, :RET_DV], s[:, RET_DV:])


def _expand_state(c):
    row = lax.broadcasted_iota(jnp.int32, c.shape, 0)
    z = jnp.zeros_like(c)
    return jnp.concatenate([jnp.where(row < 64, c, z), jnp.where(row < 64, z, c)], axis=1)


def _ret_bwd_kernel(dec_ref, k_ref, v_ref, ck_ref, cv_ref, sb_ref, sbs):
    i = pl.program_id(0)
    C = RET_CHUNK
    lc = ck_ref.shape[0]

    @pl.when(i == 0)
    def _():
        pos = lax.broadcasted_iota(jnp.int32, (lc, LANES), 0).astype(F32)
        for p in range(RET_PAIRS):
            ks = slice(p * LANES, (p + 1) * LANES)
            vs = slice(p * 2 * RET_DV, (p + 1) * 2 * RET_DV)
            wb = jnp.exp(_pair_lg(dec_ref, 1, p, (lc, LANES)) * pos)
            sbs[p] = _kv_pair(ck_ref[:, ks], cv_ref[:, vs], wb)

    pos = lax.broadcasted_iota(jnp.int32, (C, LANES), 0).astype(F32)
    for p in range(RET_PAIRS):
        ks = slice(p * LANES, (p + 1) * LANES)
        vs = slice(p * 2 * RET_DV, (p + 1) * 2 * RET_DV)
        wb = jnp.exp(_pair_lg(dec_ref, 1, p, (C, LANES)) * pos)
        gb = _row_decay(dec_ref, 1, p)
        sb = sbs[p]
        for cc in reversed(range(RET_STEP_CHUNKS)):
            rs = slice(cc * C, (cc + 1) * C)
            sb_ref[cc, p] = _compact_state(sb).astype(BF16)
            sb = gb * sb + _kv_pair(k_ref[rs, ks], v_ref[rs, vs], wb)
        sbs[p] = sb


def _ret_bwd_states(dec, proj, cproj):
    L = proj.shape[0]
    lc = cproj.shape[0]
    S = RET_STEP_CHUNKS
    R = S * RET_CHUNK
    n = L // R
    return pl.pallas_call(
        _ret_bwd_kernel,
        grid=(n,),
        in_specs=[pl.BlockSpec(memory_space=pltpu.SMEM),
                  pl.BlockSpec((R, RET_QK_COLS), lambda i: (n - 1 - i, 1)),
                  pl.BlockSpec((R, RET_V_COLS), lambda i: (n - 1 - i, 1)),
                  pl.BlockSpec((lc, RET_QK_COLS), lambda i: (0, 1)),
                  pl.BlockSpec((lc, RET_V_COLS), lambda i: (0, 1))],
        out_specs=pl.BlockSpec((S, RET_PAIRS, LANES, RET_DV), lambda i: (n - 1 - i, 0, 0, 0)),
        out_shape=jax.ShapeDtypeStruct((n * S, RET_PAIRS, LANES, RET_DV), BF16),
        scratch_shapes=[pltpu.VMEM((RET_PAIRS, LANES, 2 * RET_DV), F32)],
        compiler_params=_params("arbitrary"),
        name="ret_bwd",
    )(dec, proj, proj, cproj, cproj)


def _ret_out_kernel(dec_ref, q_ref, k_ref, v_ref, g_ref, sb_ref, ck_ref, cv_ref, o_ref, sfs):
    i = pl.program_id(0)
    C = RET_CHUNK
    lc = ck_ref.shape[0]

    @pl.when(i == 0)
    def _():
        cpos = lax.broadcasted_iota(jnp.int32, (lc, LANES), 0).astype(F32)
        for p in range(RET_PAIRS):
            ks = slice(p * LANES, (p + 1) * LANES)
            vs = slice(p * 2 * RET_DV, (p + 1) * 2 * RET_DV)
            wf = jnp.exp(_pair_lg(dec_ref, 0, p, (lc, LANES)) * (lc - 1.0 - cpos))
            sfs[p] = _kv_pair(ck_ref[:, ks], cv_ref[:, vs], wf)

    pos = lax.broadcasted_iota(jnp.int32, (C, LANES), 0).astype(F32)
    n_i = lax.broadcasted_iota(jnp.int32, (C, 2 * C), 0)
    m_i = lax.broadcasted_iota(jnp.int32, (C, 2 * C), 1) % C
    rel = (n_i - m_i).astype(F32)
    lane = lax.broadcasted_iota(jnp.int32, (C, LANES), 1)
    lo = lane < 64
    for p in range(RET_PAIRS):
        ks = slice(p * LANES, (p + 1) * LANES)
        vs = slice(p * 2 * RET_DV, (p + 1) * 2 * RET_DV)
        col_a = lax.broadcasted_iota(jnp.int32, (C, 2 * C), 1) < C
        raw_f = jnp.where(col_a, jnp.full((C, 2 * C), dec_ref[0, 2 * p], F32), jnp.full((C, 2 * C), dec_ref[0, 2 * p + 1], F32))
        raw_b = jnp.where(col_a, jnp.full((C, 2 * C), dec_ref[1, 2 * p], F32), jnp.full((C, 2 * C), dec_ref[1, 2 * p + 1], F32))
        dmat = jnp.where(rel >= 0, jnp.exp(-jnp.exp(raw_f) * jnp.maximum(rel, 0.0)),
                         jnp.exp(-jnp.exp(raw_b) * jnp.maximum(-rel, 0.0)))
        lg_f = _pair_lg(dec_ref, 0, p, (C, LANES))
        wqf = jnp.exp(lg_f * (pos + 1.0))
        wqb = jnp.exp(_pair_lg(dec_ref, 1, p, (C, LANES)) * (float(C) - pos))
        wkf = jnp.exp(lg_f * (C - 1.0 - pos))
        gf = _row_decay(dec_ref, 0, p)
        sf = sfs[p]
        for cc in range(RET_STEP_CHUNKS):
            rs = slice(cc * C, (cc + 1) * C)
            q = q_ref[rs, ks]
            k = k_ref[rs, ks]
            v = v_ref[rs, vs]
            zk = jnp.zeros_like(k)
            kst = jnp.concatenate([jnp.where(lo, k, zk), jnp.where(lo, zk, k)], axis=0)
            s = lax.dot_general(q, kst, (((1,), (1,)), ((), ())), preferred_element_type=F32)
            sd = (s * dmat).astype(BF16)
            qf32 = q.astype(F32)
            qwf = (qf32 * wqf).astype(BF16)
            qwb = (qf32 * wqb).astype(BF16)
            zv = jnp.zeros((C, RET_DV), BF16)
            vbd = jnp.concatenate([jnp.concatenate([v[:, :RET_DV], zv], axis=1),
                                   jnp.concatenate([zv, v[:, RET_DV:]], axis=1)], axis=0)
            lhs = jnp.concatenate([sd, qwf, qwb], axis=1)
            rhs = jnp.concatenate([vbd, sf.astype(BF16), _expand_state(sb_ref[cc, p])], axis=0)
            o = jnp.dot(lhs, rhs, preferred_element_type=F32)
            sf = gf * sf + _kv_pair(k, v, wkf)
            for t in range(2):
                oh = o[:, t * RET_DV:(t + 1) * RET_DV]
                oh = oh * lax.rsqrt(jnp.mean(oh * oh, axis=-1, keepdims=True) + NORM_EPS)
                cs = slice(p * 2 * RET_DV + t * RET_DV, p * 2 * RET_DV + (t + 1) * RET_DV)
                gt = g_ref[rs, cs].astype(F32)
                o_ref[rs, cs] = (oh * (gt / (1.0 + jnp.exp(-gt)))).astype(BF16)
        sfs[p] = sf


def _ret_out(dec, proj, sb, cproj):
    L = proj.shape[0]
    lc = cproj.shape[0]
    S = RET_STEP_CHUNKS
    R = S * RET_CHUNK
    n = L // R
    return pl.pallas_call(
        _ret_out_kernel,
        grid=(n,),
        in_specs=[pl.BlockSpec(memory_space=pltpu.SMEM),
                  pl.BlockSpec((R, RET_QK_COLS), lambda i: (i, 0)),
                  pl.BlockSpec((R, RET_QK_COLS), lambda i: (i, 1)),
                  pl.BlockSpec((R, RET_V_COLS), lambda i: (i, 1)),
                  pl.BlockSpec((R, RET_V_COLS), lambda i: (i, 2)),
                  pl.BlockSpec((S, RET_PAIRS, LANES, RET_DV), lambda i: (i, 0, 0, 0)),
                  pl.BlockSpec((lc, RET_QK_COLS), lambda i: (0, 1)),
                  pl.BlockSpec((lc, RET_V_COLS), lambda i: (0, 1))],
        out_specs=pl.BlockSpec((R, RET_HEADS * RET_DV), lambda i: (i, 0)),
        out_shape=jax.ShapeDtypeStruct((L, RET_HEADS * RET_DV), BF16),
        scratch_shapes=[pltpu.VMEM((RET_PAIRS, LANES, 2 * RET_DV), F32)],
        compiler_params=_params("arbitrary"),
        name="ret_out",
    )(dec, proj, proj, proj, proj, sb, cproj, cproj)


def _attn_kernel(sink_ref, q_ref, kp_ref, kc_ref, kn_ref, vp_ref, vc_ref, vn_ref, ck_ref, cv_ref, o_ref, *, fillers):
    n = pl.program_id(0)
    nstep = pl.num_programs(0)
    B = ATT_BLOCK
    SB = ATT_STEP_BLOCKS
    kj = lax.broadcasted_iota(jnp.int32, (B, B), 0)
    qi = lax.broadcasted_iota(jnp.int32, (B, B), 1)
    ok_prev = jnp.where(n > 0, 0.0, MASK_NEG).astype(F32)
    ok_next = jnp.where(n < nstep - 1, 0.0, MASK_NEG).astype(F32)

    def band(inside, ok):
        return jnp.concatenate([jnp.where(inside, ok, MASK_NEG).astype(F32)] * ATT_GROUP, axis=1)

    bias_prev = [band(kj >= qi, ok_prev if j == 0 else 0.0) for j in range(SB)]
    bias_next = [band(kj <= qi, ok_next if j == SB - 1 else 0.0) for j in range(SB)]
    lane = lax.broadcasted_iota(jnp.int32, (B, LANES), 1)
    lo = lane < 64
    hi = lane >= 64

    def keys_of(j, gs, prev_ref, cur_ref, next_ref, ctx_ref):
        prev = prev_ref[:, gs] if j == 0 else cur_ref[(j - 1) * B:j * B, gs]
        nxt = next_ref[:, gs] if j == SB - 1 else cur_ref[(j + 1) * B:(j + 2) * B, gs]
        return jnp.concatenate([prev, cur_ref[j * B:(j + 1) * B, gs], nxt, ctx_ref[:, gs]], axis=0)

    def scores(j, g):
        gs = slice(g * LANES, (g + 1) * LANES)
        kcat = keys_of(j, gs, kp_ref, kc_ref, kn_ref, ck_ref)
        qs = []
        for r in range(ATT_GROUP):
            h = ATT_GROUP * g + r
            qt = q_ref[j * B:(j + 1) * B, (h // 2) * LANES:(h // 2 + 1) * LANES]
            keep = lo if h % 2 == 0 else hi
            qs.append(jnp.where(keep, qt, jnp.zeros_like(qt)))
        q4 = jnp.concatenate(qs, axis=0)
        return lax.dot_general(kcat, q4, (((1,), (1,)), ((), ())), preferred_element_type=F32)

    def softmax(j, g, s):
        sk = jnp.concatenate([jnp.full((1, B), sink_ref[ATT_GROUP * g + r], F32)
                              for r in range(ATT_GROUP)], axis=1) * LOG2E
        s = jnp.concatenate([s[:B] + bias_prev[j], s[B:2 * B], s[2 * B:3 * B] + bias_next[j], s[3 * B:]], axis=0)
        m = jnp.maximum(jnp.max(s, axis=0, keepdims=True), sk)
        e = jnp.exp2(s - m)
        den = jnp.sum(e, axis=0, keepdims=True) + jnp.exp2(sk - m)
        return e.astype(BF16), den

    def values(j, g, e, den):
        gs = slice(g * LANES, (g + 1) * LANES)
        vcat = keys_of(j, gs, vp_ref, vc_ref, vn_ref, cv_ref)
        res = lax.dot_general(vcat, e, (((0,), (0,)), ((), ())), preferred_element_type=F32) * (1.0 / den)
        for t in range(2):
            even = res[:, (2 * t) * B:(2 * t + 1) * B].T
            odd = res[:, (2 * t + 1) * B:(2 * t + 2) * B].T
            c0 = (2 * g + t) * LANES
            o_ref[j * B:(j + 1) * B, c0:c0 + LANES] = jnp.where(lo, even, odd).astype(BF16)

    units = [(j, g) for j in range(SB) for g in range(ATT_KV_HEADS)]
    per_unit = -(-len(fillers) // len(units))
    s_next = scores(*units[0])
    pending = None
    for u, unit in enumerate(units):
        s_cur = s_next
        if u + 1 < len(units):
            s_next = scores(*units[u + 1])
        e_den = softmax(*unit, s_cur)
        for fill in fillers[u * per_unit:(u + 1) * per_unit]:
            fill()
        if pending is not None:
            values(*units[u - 1], *pending)
        pending = e_den
    values(*units[-1], *pending)


def _attn(sink, proj, kd, vd, ckd, cvd, riders):
    L = proj.shape[0]
    B = ATT_BLOCK
    SB = ATT_STEP_BLOCKS
    n = L // (SB * B)
    nb = L // B
    lc = ckd.shape[0]
    prev = pl.BlockSpec((B, KV_DUP_COLS), lambda i: (jnp.maximum(i * SB - 1, 0), 0))
    cur = pl.BlockSpec((SB * B, KV_DUP_COLS), lambda i: (i, 0))
    nxt = pl.BlockSpec((B, KV_DUP_COLS), lambda i: (jnp.minimum((i + 1) * SB, nb - 1), 0))
    full = pl.BlockSpec((lc, KV_DUP_COLS), lambda i: (0, 0))
    rid_in_specs, rid_out_specs, rid_shapes = _rider_specs(riders, n)
    return pl.pallas_call(
        _with_cast_riders(_attn_kernel, 10, 1, len(riders)),
        grid=(n,),
        in_specs=[pl.BlockSpec(memory_space=pltpu.SMEM),
                  pl.BlockSpec((SB * B, ATT_Q_COLS), lambda i: (i, 3)),
                  prev, cur, nxt, prev, cur, nxt, full, full] + rid_in_specs,
        out_specs=[pl.BlockSpec((SB * B, ATT_HEADS * ATT_DH), lambda i: (i, 0))] + rid_out_specs,
        out_shape=[jax.ShapeDtypeStruct((L, ATT_HEADS * ATT_DH), BF16)] + rid_shapes,
        compiler_params=_params("parallel"),
        name="attn",
    )(sink, proj, kd, kd, kd, vd, vd, vd, ckd, cvd, *[r[0] for r in riders])


def _out_proj_kernel(yr_ref, ya_ref, w_ref, x_ref, gt_ref, g_ref, sh_ref, sc_ref, o_ref, h_ref):
    kr = yr_ref.shape[1]
    for r in range(yr_ref.shape[0] // OUT_ROW_CHUNK):
        rs = slice(r * OUT_ROW_CHUNK, (r + 1) * OUT_ROW_CHUNK)
        acc = jnp.dot(yr_ref[rs, :], w_ref[:kr, :], preferred_element_type=F32)
        acc = acc + jnp.dot(ya_ref[rs, :], w_ref[kr:, :], preferred_element_type=F32)
        x1 = x_ref[rs, :] + gt_ref[...] * acc
        o_ref[rs, :] = x1
        y = x1 * lax.rsqrt(jnp.mean(x1 * x1, axis=-1, keepdims=True) + NORM_EPS)
        y = y * g_ref[...]
        h_ref[rs, :] = (y * (1.0 + sc_ref[...]) + sh_ref[...]).astype(BF16)


def _out_proj(yr, ya, w, x, gt, g, sh, sc, *, tm):
    m, d = x.shape
    kr, ka = yr.shape[1], ya.shape[1]
    row = lambda i: (i, 0)
    vec = pl.BlockSpec((1, d), lambda i: (0, 0))
    return pl.pallas_call(
        _out_proj_kernel,
        grid=(m // tm,),
        in_specs=[pl.BlockSpec((tm, kr), row), pl.BlockSpec((tm, ka), row),
                  pl.BlockSpec((kr + ka, d), lambda i: (0, 0)),
                  pl.BlockSpec((tm, d), row), vec, vec, vec, vec],
        out_specs=[pl.BlockSpec((tm, d), row), pl.BlockSpec((tm, d), row)],
        out_shape=[jax.ShapeDtypeStruct((m, d), F32), jax.ShapeDtypeStruct((m, d), BF16)],
        compiler_params=_params("parallel"),
        name="out_proj",
    )(yr, ya, w, x, gt, g, sh, sc)


def _ffn_kernel(h_ref, gt_ref, gfin_ref, wg_ref, wu_ref, wd_ref, x_hbm, o_ref, x_buf, sem):
    i = pl.program_id(0)
    f = pl.program_id(1)
    last = pl.num_programs(1) - 1
    rows = o_ref.shape[0]
    x_copy = pltpu.make_async_copy(x_hbm.at[pl.ds(pl.multiple_of(i * rows, rows), rows), :], x_buf, sem.at[0])

    def step(first, final):
        wd = wd_ref[...].astype(BF16)
        for r in range(rows // FFN_ROW_CHUNK):
            rs = slice(r * FFN_ROW_CHUNK, (r + 1) * FFN_ROW_CHUNK)
            h = h_ref[rs, :]
            a = jnp.dot(h, wg_ref[0], preferred_element_type=F32)
            u = jnp.dot(h, wu_ref[0], preferred_element_type=F32)
            act = ((a / (1.0 + jnp.exp(-a))) * u).astype(BF16)
            part = jnp.dot(act, wd, preferred_element_type=F32)
            if first:
                o_ref[rs, :] = part
            elif not final:
                o_ref[rs, :] += part
            else:
                y = x_buf[rs, :] + gt_ref[...] * (o_ref[rs, :] + part)
                y = y * lax.rsqrt(jnp.mean(y * y, axis=-1, keepdims=True) + NORM_EPS)
                o_ref[rs, :] = y * gfin_ref[...]

    @pl.when(f == 0)
    def _():
        x_copy.start()
        step(first=True, final=False)

    @pl.when((f > 0) & (f < last))
    def _():
        step(first=False, final=False)

    @pl.when(f == last)
    def _():
        x_copy.wait()
        step(first=False, final=True)


def _ffn(h, x, gt, gfin, wg, wu, wd, *, tm):
    m, d = x.shape
    nf = wg.shape[0]
    assert wg.shape == wu.shape == (nf, d, FFN_TILE) and wd.shape == (nf * FFN_TILE, d)
    assert m % tm == 0 and tm % FFN_ROW_CHUNK == 0
    row = lambda i, f: (i, 0)
    vec = pl.BlockSpec((1, d), lambda i, f: (0, 0))
    wcol = pl.BlockSpec((1, d, FFN_TILE), lambda i, f: (f, 0, 0))
    return pl.pallas_call(
        _ffn_kernel,
        grid=(m // tm, nf),
        in_specs=[pl.BlockSpec((tm, d), row), vec, vec, wcol, wcol,
                  pl.BlockSpec((FFN_TILE, d), lambda i, f: (f, 0)),
                  pl.BlockSpec(memory_space=pl.ANY)],
        out_specs=pl.BlockSpec((tm, d), row),
        out_shape=jax.ShapeDtypeStruct((m, d), F32),
        scratch_shapes=[pltpu.VMEM((tm, d), F32), pltpu.SemaphoreType.DMA((1,))],
        compiler_params=_params("arbitrary", "arbitrary"),
        name="ffn",
    )(h, gt, gfin, wg, wu, wd, x)


def _rope_tables(L):
    f32 = np.float32
    lane = np.arange(LANES)
    inv1 = f32(ROPE_BASE) ** (-np.arange(32, dtype=f32) / f32(32))
    ang1 = np.arange(L, dtype=f32)[:, None] * inv1[None, :]
    sgn1 = np.where((lane % 64) < 32, -1.0, 1.0).astype(f32)
    cos1 = np.tile(np.cos(ang1), (1, LANES // 32))
    sin1 = np.tile(np.sin(ang1), (1, LANES // 32)) * sgn1[None, :]
    inv2 = f32(ROPE_BASE) ** (-np.arange(16, dtype=f32) / f32(16))
    nrow = L // GRID_W
    ang_r = np.arange(nrow, dtype=f32)[:, None] * inv2[None, :]
    ang_c = np.arange(GRID_W, dtype=f32)[:, None] * inv2[None, :]
    sgna = np.where((lane % 32) < 16, -1.0, 1.0).astype(f32)

    def expand(fr, fc):
        by_row = np.broadcast_to(np.tile(fr, (1, 2))[:, None, :], (nrow, GRID_W, 32))
        by_col = np.broadcast_to(np.tile(fc, (1, 2))[None, :, :], (nrow, GRID_W, 32))
        head = np.concatenate([by_row, by_col], axis=-1).reshape(L, 64)
        return np.tile(head, (1, LANES // 64))

    cosa = expand(np.cos(ang_r), np.cos(ang_c))
    sina = expand(np.sin(ang_r), np.sin(ang_c)) * sgna[None, :]
    return tuple(np.ascontiguousarray(t, dtype=f32) for t in (cos1, sin1, cosa, sina))


def kernel(x, c, ctx, c_ctx, w_mod, b_mod, norm_mix, norm_ffn, w_in, ret_decay, attn_sink,
           w_out, w_gate, w_up, w_down, norm_final):
    B, L, D = x.shape
    assert B == 1 and w_mod.shape[0] == 1, "single batch element, depth-1 layer"
    x2 = x[0]
    xc2 = ctx[0]

    cv = jnp.zeros((8, D), F32).at[0].set(c[0]).at[1].set(c_ctx)
    mod = _mod(cv, w_mod[0], b_mod[0][None, :], 2 * D)
    sh_m, sc_m = mod[0:1, 0:D], mod[0:1, D:2 * D]
    sh_mc, sc_mc = mod[1:2, 0:D], mod[1:2, D:2 * D]

    g_mix = norm_mix[0][None, :]
    cproj, ckd, cvd, w_in_b = _ctx_proj(xc2, g_mix, sh_mc, sc_mc, w_in[0])
    proj, kd, vd, mod_rest = _in_proj(x2, g_mix, sh_m, sc_m, w_in_b, _rope_tables(L),
                                      c[0][:, None], w_mod[0], b_mod[0][None, :], 2 * D, tm=ROW_TILE)
    gt_m, sh_f, sc_f, gt_f = [mod_rest.reshape(1, 4 * D)[:, k * D:(k + 1) * D] for k in range(4)]

    dec = ret_decay[0].astype(F32)
    sb = _ret_bwd_states(dec, proj, cproj)
    y_ret = _ret_out(dec, proj, sb, cproj)
    y_att, w_gate_b, w_up_b, w_out_b = _attn(
        attn_sink[0].astype(F32), proj, kd, vd, ckd, cvd,
        [(w_gate[0], 1, FFN_TILE), (w_up[0], 1, FFN_TILE), (w_out[0], 1, None)])

    x1, hff = _out_proj(y_ret, y_att, w_out_b, x2, gt_m, norm_ffn[0][None, :], sh_f, sc_f, tm=ROW_TILE)
    out = _ffn(hff, x1, gt_f, norm_final[None, :], w_gate_b, w_up_b, w_down[0], tm=FFN_ROW_TILE)
    return out[None]
```

```python
import jax
import jax.numpy as jnp
import numpy as np
from jax import lax
from jax.experimental import pallas as pl
from jax.experimental.pallas import tpu as pltpu

GRID_W = 64
RET_HEADS = 8
RET_DK = 64
RET_DV = 128
RET_CHUNK = 128
ATT_HEADS = 16
ATT_KV_HEADS = 4
ATT_DH = 64
ATT_GROUP = ATT_HEADS // ATT_KV_HEADS
WINDOW = 128
ATT_BLOCK = 128
ROPE_BASE = 10000.0
NORM_EPS = 1e-6
K_SCALE = RET_DK ** -0.5
ATT_SCALE = ATT_DH ** -0.5
LOG2E = 1.4426950408889634

RET_QK_COLS = RET_HEADS * RET_DK
RET_V_COLS = RET_HEADS * RET_DV
ATT_Q_COLS = ATT_HEADS * ATT_DH
KV_DUP_COLS = 2 * ATT_KV_HEADS * ATT_DH

LANES = 128
RET_PAIRS = RET_HEADS // 2
MASK_NEG = -1e30
VMEM_LIMIT = 56 * 1024 * 1024
CAST_PIECE_ROWS = 16
RET_STEP_CHUNKS = 8
ATT_STEP_BLOCKS = 4
ROW_TILE = 512
OUT_ROW_CHUNK = ROW_TILE
IN_ROW_CHUNK = ROW_TILE
FFN_TILE = 512
FFN_ROW_TILE = 1024
FFN_ROW_CHUNK = FFN_ROW_TILE

BF16 = jnp.bfloat16
F32 = jnp.float32


def _params(*sem):
    return pltpu.CompilerParams(dimension_semantics=sem, vmem_limit_bytes=VMEM_LIMIT)


def _with_cast_riders(body, n_in, n_out, n_rid):
    def wrapped(*refs):
        ins = refs[:n_in]
        rid_in = refs[n_in:n_in + n_rid]
        outs = refs[n_in + n_rid:n_in + n_rid + n_out]
        rid_out = refs[n_in + n_rid + n_out:n_in + 2 * n_rid + n_out]
        scratch = refs[n_in + 2 * n_rid + n_out:]

        def piece(src, dst, r0):
            rs = slice(r0, r0 + CAST_PIECE_ROWS)
            if len(dst.shape) == 2:
                dst[rs, :] = src[rs, :].astype(BF16)
            else:
                tc = dst.shape[2]
                for t in range(dst.shape[0]):
                    dst[t, rs, :] = src[rs, t * tc:(t + 1) * tc].astype(BF16)

        pieces = [(lambda s=src, d=dst, r=r0: piece(s, d, r))
                  for src, dst in zip(rid_in, rid_out) for r0 in range(0, src.shape[0], CAST_PIECE_ROWS)]
        done = []
        fillers = [(lambda p=p: (done.append(1), p())) for p in pieces]
        body(*ins, *outs, *scratch, fillers=fillers)
        assert len(done) == len(pieces), "every cast piece must be emitted exactly once"
    return wrapped


def _rider_specs(riders, steps):
    in_specs, out_specs, shapes = [], [], []
    for w, ncb, tile in riders:
        rows, cols = w.shape
        nrb = steps // ncb
        assert nrb * ncb == steps and rows % nrb == 0 and cols % ncb == 0
        br, bc = rows // nrb, cols // ncb
        assert br % CAST_PIECE_ROWS == 0 and bc % LANES == 0, "slab must be bf16-tile aligned"
        in_specs.append(pl.BlockSpec((br, bc), lambda i, ncb=ncb: (i // ncb, i % ncb)))
        if tile is None:
            out_specs.append(in_specs[-1])
            shapes.append(jax.ShapeDtypeStruct(w.shape, BF16))
        else:
            assert ncb == 1 and cols % tile == 0 and tile % LANES == 0
            out_specs.append(pl.BlockSpec((cols // tile, br, tile), lambda i: (0, i, 0)))
            shapes.append(jax.ShapeDtypeStruct((cols // tile, rows, tile), BF16))
    return in_specs, out_specs, shapes


def _mod_kernel(cv_ref, w_ref, b_ref, o_ref):
    cv = cv_ref[...]
    s = cv / (1.0 + jnp.exp(-cv))
    o_ref[...] = jnp.dot(s.astype(BF16), w_ref[...].astype(BF16),
                         preferred_element_type=F32) + b_ref[...]


def _mod(cv, w, b, n):
    d = w.shape[0]
    tn = 512
    assert n % tn == 0
    return pl.pallas_call(
        _mod_kernel,
        grid=(n // tn,),
        in_specs=[pl.BlockSpec((8, d), lambda j: (0, 0)),
                  pl.BlockSpec((d, tn), lambda j: (0, j)),
                  pl.BlockSpec((1, tn), lambda j: (0, j))],
        out_specs=pl.BlockSpec((8, tn), lambda j: (0, j)),
        out_shape=jax.ShapeDtypeStruct((8, n), F32),
        compiler_params=_params("parallel"),
        name="mod",
    )(cv, w, b)


def _rot_pairs(a, cos, sin_signed, half):
    lane = lax.broadcasted_iota(jnp.int32, a.shape, 1)
    first = (lane % (2 * half)) < half
    rot = jnp.where(first, pltpu.roll(a, LANES - half, 1), pltpu.roll(a, half, 1))
    return a * cos + rot * sin_signed


def _dup_halves(a):
    lane = lax.broadcasted_iota(jnp.int32, a.shape, 1)
    r = pltpu.roll(a, 64, 1)
    lo = lane < 64
    return jnp.where(lo, a, r), jnp.where(lo, r, a)


_PROJ_TILE = 512
_PROJ_TILE_KINDS = ("ret_q", "ret_k", "plain", "plain", "plain", "plain", "att_q", "att_q", "att_kv")


def _in_proj_kernel(x_ref, g_ref, sh_ref, sc_ref, w_ref, c1_ref, s1_ref, ca_ref, sa_ref,
                    cc_ref, wm_ref, bm_ref, o_ref, kd_ref, vd_ref, mod_ref):
    def mod_rider():
        cc = cc_ref[...]
        s_col = cc / (1.0 + jnp.exp(-cc))
        mod_ref[0] = jnp.sum(wm_ref[...] * s_col, axis=0, keepdims=True) + bm_ref[...]

    tn = _PROJ_TILE
    for r in range(x_ref.shape[0] // IN_ROW_CHUNK):
        rs = slice(r * IN_ROW_CHUNK, (r + 1) * IN_ROW_CHUNK)
        xf = x_ref[rs, :]
        y = xf * lax.rsqrt(jnp.mean(xf * xf, axis=-1, keepdims=True) + NORM_EPS)
        y = y * g_ref[...]
        h = (y * (1.0 + sc_ref[...]) + sh_ref[...]).astype(BF16)

        def rope1(a):
            return _rot_pairs(a, c1_ref[rs, :], s1_ref[rs, :], 32)

        def ropea(a):
            return _rot_pairs(a, ca_ref[rs, :], sa_ref[rs, :], 16)

        order = sorted(range(len(_PROJ_TILE_KINDS)), key=lambda t: _PROJ_TILE_KINDS[t] == "plain")
        for j in order:
            kind = _PROJ_TILE_KINDS[j]
            acc = jnp.dot(h, w_ref[:, j * tn:(j + 1) * tn], preferred_element_type=F32)
            if r == 0 and j == order[-2]:
                mod_rider()
            for c in range(tn // LANES):
                a = acc[:, c * LANES:(c + 1) * LANES]
                if kind == "ret_q":
                    a = rope1(a)
                elif kind == "ret_k":
                    a = rope1(a) * K_SCALE
                elif kind == "att_q":
                    a = ropea(a) * (ATT_SCALE * LOG2E)
                elif kind == "att_kv" and c < 2:
                    a = ropea(a)
                o_ref[rs, j * tn + c * LANES:j * tn + (c + 1) * LANES] = a.astype(BF16)
                if kind == "att_kv":
                    dup_ref = kd_ref if c < 2 else vd_ref
                    d0, d1 = _dup_halves(a)
                    t = 2 * (c % 2)
                    dup_ref[rs, t * LANES:(t + 1) * LANES] = d0.astype(BF16)
                    dup_ref[rs, (t + 1) * LANES:(t + 2) * LANES] = d1.astype(BF16)


def _in_proj(x, g, sh, sc, w, tabs, c_col, w_mod, b_mod, mod_done, *, tm):
    m, d = x.shape
    n = w.shape[1]
    assert n == _PROJ_TILE * len(_PROJ_TILE_KINDS) and m % tm == 0 and tm % IN_ROW_CHUNK == 0
    steps = m // tm
    slab = (w_mod.shape[1] - mod_done) // steps
    assert slab * steps == w_mod.shape[1] - mod_done and slab % LANES == 0 and mod_done % slab == 0
    slab0 = mod_done // slab
    c1, s1, ca, sa = tabs
    row = lambda i: (i, 0)
    vec = pl.BlockSpec((1, d), lambda i: (0, 0))
    tab = pl.BlockSpec((tm, LANES), row)
    return pl.pallas_call(
        _in_proj_kernel,
        grid=(m // tm,),
        in_specs=[pl.BlockSpec((tm, d), row), vec, vec, vec,
                  pl.BlockSpec((d, n), lambda i: (0, 0), pipeline_mode=pl.Buffered(1)),
                  tab, tab, tab, tab,
                  pl.BlockSpec((d, 1), lambda i: (0, 0)),
                  pl.BlockSpec((d, slab), lambda i: (0, slab0 + i)),
                  pl.BlockSpec((1, slab), lambda i: (0, slab0 + i))],
        out_specs=[pl.BlockSpec((tm, n), row),
                   pl.BlockSpec((tm, KV_DUP_COLS), row),
                   pl.BlockSpec((tm, KV_DUP_COLS), row),
                   pl.BlockSpec((1, 1, slab), lambda i: (i, 0, 0))],
        out_shape=[jax.ShapeDtypeStruct((m, n), BF16),
                   jax.ShapeDtypeStruct((m, KV_DUP_COLS), BF16),
                   jax.ShapeDtypeStruct((m, KV_DUP_COLS), BF16),
                   jax.ShapeDtypeStruct((steps, 1, slab), F32)],
        compiler_params=_params("parallel"),
        name="in_proj",
    )(x, g, sh, sc, w, c1, s1, ca, sa, c_col, w_mod, b_mod)


def _ctx_proj_kernel(x_ref, g_ref, sh_ref, sc_ref, w_ref, o_ref, kd_ref, vd_ref, wb_ref, h_ref):
    j = pl.program_id(0)

    @pl.when(j == 0)
    def _():
        xf = x_ref[...]
        y = xf * lax.rsqrt(jnp.mean(xf * xf, axis=-1, keepdims=True) + NORM_EPS)
        y = y * g_ref[...]
        h_ref[...] = (y * (1.0 + sc_ref[...]) + sh_ref[...]).astype(BF16)

    wb = w_ref[...].astype(BF16)
    wb_ref[...] = wb
    acc = jnp.dot(h_ref[...], wb, preferred_element_type=F32)
    is_ret_k = _PROJ_TILE_KINDS.index("ret_k")
    o_ref[...] = (acc * jnp.where(j == is_ret_k, K_SCALE, 1.0)).astype(BF16)

    @pl.when(j == _PROJ_TILE_KINDS.index("att_kv"))
    def _():
        for c in range(_PROJ_TILE // LANES):
            dup_ref = kd_ref if c < 2 else vd_ref
            d0, d1 = _dup_halves(acc[:, c * LANES:(c + 1) * LANES])
            t = 2 * (c % 2)
            dup_ref[:, t * LANES:(t + 1) * LANES] = d0.astype(BF16)
            dup_ref[:, (t + 1) * LANES:(t + 2) * LANES] = d1.astype(BF16)


def _ctx_proj(x, g, sh, sc, w):
    m, d = x.shape
    n = w.shape[1]
    tn = _PROJ_TILE
    assert n == tn * len(_PROJ_TILE_KINDS)
    fixed = lambda j: (0, 0)
    vec = pl.BlockSpec((1, d), fixed)
    return pl.pallas_call(
        _ctx_proj_kernel,
        grid=(n // tn,),
        in_specs=[pl.BlockSpec((m, d), fixed), vec, vec, vec,
                  pl.BlockSpec((d, tn), lambda j: (0, j))],
        out_specs=[pl.BlockSpec((m, tn), lambda j: (0, j)),
                   pl.BlockSpec((m, KV_DUP_COLS), fixed),
                   pl.BlockSpec((m, KV_DUP_COLS), fixed),
                   pl.BlockSpec((d, tn), lambda j: (0, j))],
        out_shape=[jax.ShapeDtypeStruct((m, n), BF16),
                   jax.ShapeDtypeStruct((m, KV_DUP_COLS), BF16),
                   jax.ShapeDtypeStruct((m, KV_DUP_COLS), BF16),
                   jax.ShapeDtypeStruct((d, n), BF16)],
        scratch_shapes=[pltpu.VMEM((m, d), BF16)],
        compiler_params=_params("arbitrary"),
        name="ctx_proj",
    )(x, g, sh, sc, w)


def _pair_lg(dec_ref, d, p, shape):
    lane = lax.broadcasted_iota(jnp.int32, shape, 1)
    first = (lane % LANES) < 64
    raw = jnp.where(first, jnp.full(shape, dec_ref[d, 2 * p], F32), jnp.full(shape, dec_ref[d, 2 * p + 1], F32))
    return -jnp.exp(raw)


def _head_block_mask(shape):
    r = lax.broadcasted_iota(jnp.int32, shape, 0)
    c = lax.broadcasted_iota(jnp.int32, shape, 1)
    return (r // 64) == (c // LANES)


def _kv_pair(k_pair, v_pair, w):
    kw = (k_pair.astype(F32) * w).astype(BF16)
    kv = lax.dot_general(kw, v_pair, (((0,), (0,)), ((), ())), preferred_element_type=F32)
    return jnp.where(_head_block_mask(kv.shape), kv, 0.0)


def _row_decay(dec_ref, d, p):
    shape = (LANES, 2 * RET_DV)
    rowh = lax.broadcasted_iota(jnp.int32, shape, 0) < 64
    raw = jnp.where(rowh, jnp.full(shape, dec_ref[d, 2 * p], F32), jnp.full(shape, dec_ref[d, 2 * p + 1], F32))
    return jnp.exp(-jnp.exp(raw) * float(RET_CHUNK))


def _compact_state(s):
    row = lax.broadcasted_iota(jnp.int32, (LANES, RET_DV), 0)
    return jnp.where(row < 64, s[:, :RET_DV], s[:, RET_DV:])


def _expand_state(c):
    row = lax.broadcasted_iota(jnp.int32, c.shape, 0)
    z = jnp.zeros_like(c)
    return jnp.concatenate([jnp.where(row < 64, c, z), jnp.where(row < 64, z, c)], axis=1)


def _ret_bwd_kernel(dec_ref, k_ref, v_ref, ck_ref, cv_ref, sb_ref, sbs):
    i = pl.program_id(0)
    C = RET_CHUNK
    lc = ck_ref.shape[0]

    @pl.when(i == 0)
    def _():
        pos = lax.broadcasted_iota(jnp.int32, (lc, LANES), 0).astype(F32)
        for p in range(RET_PAIRS):
            ks = slice(p * LANES, (p + 1) * LANES)
            vs = slice(p * 2 * RET_DV, (p + 1) * 2 * RET_DV)
            wb = jnp.exp(_pair_lg(dec_ref, 1, p, (lc, LANES)) * pos)
            sbs[p] = _kv_pair(ck_ref[:, ks], cv_ref[:, vs], wb)

    pos = lax.broadcasted_iota(jnp.int32, (C, LANES), 0).astype(F32)
    for p in range(RET_PAIRS):
        ks = slice(p * LANES, (p + 1) * LANES)
        vs = slice(p * 2 * RET_DV, (p + 1) * 2 * RET_DV)
        wb = jnp.exp(_pair_lg(dec_ref, 1, p, (C, LANES)) * pos)
        gb = _row_decay(dec_ref, 1, p)
        sb = sbs[p]
        for cc in reversed(range(RET_STEP_CHUNKS)):
            rs = slice(cc * C, (cc + 1) * C)
            sb_ref[cc, p] = _compact_state(sb).astype(BF16)
            sb = gb * sb + _kv_pair(k_ref[rs, ks], v_ref[rs, vs], wb)
        sbs[p] = sb


def _ret_bwd_states(dec, proj, cproj):
    L = proj.shape[0]
    lc = cproj.shape[0]
    S = RET_STEP_CHUNKS
    R = S * RET_CHUNK
    n = L // R
    return pl.pallas_call(
        _ret_bwd_kernel,
        grid=(n,),
        in_specs=[pl.BlockSpec(memory_space=pltpu.SMEM),
                  pl.BlockSpec((R, RET_QK_COLS), lambda i: (n - 1 - i, 1)),
                  pl.BlockSpec((R, RET_V_COLS), lambda i: (n - 1 - i, 1)),
                  pl.BlockSpec((lc, RET_QK_COLS), lambda i: (0, 1)),
                  pl.BlockSpec((lc, RET_V_COLS), lambda i: (0, 1))],
        out_specs=pl.BlockSpec((S, RET_PAIRS, LANES, RET_DV), lambda i: (n - 1 - i, 0, 0, 0)),
        out_shape=jax.ShapeDtypeStruct((n * S, RET_PAIRS, LANES, RET_DV), BF16),
        scratch_shapes=[pltpu.VMEM((RET_PAIRS, LANES, 2 * RET_DV), F32)],
        compiler_params=_params("arbitrary"),
        name="ret_bwd",
    )(dec, proj, proj, cproj, cproj)


def _ret_out_kernel(dec_ref, q_ref, k_ref, v_ref, g_ref, sb_ref, ck_ref, cv_ref, o_ref, sfs):
    i = pl.program_id(0)
    C = RET_CHUNK
    lc = ck_ref.shape[0]

    @pl.when(i == 0)
    def _():
        cpos = lax.broadcasted_iota(jnp.int32, (lc, LANES), 0).astype(F32)
        for p in range(RET_PAIRS):
            ks = slice(p * LANES, (p + 1) * LANES)
            vs = slice(p * 2 * RET_DV, (p + 1) * 2 * RET_DV)
            wf = jnp.exp(_pair_lg(dec_ref, 0, p, (lc, LANES)) * (lc - 1.0 - cpos))
            sfs[p] = _kv_pair(ck_ref[:, ks], cv_ref[:, vs], wf)

    pos = lax.broadcasted_iota(jnp.int32, (C, LANES), 0).astype(F32)
    n_i = lax.broadcasted_iota(jnp.int32, (C, 2 * C), 0)
    m_i = lax.broadcasted_iota(jnp.int32, (C, 2 * C), 1) % C
    rel = (n_i - m_i).astype(F32)
    lane = lax.broadcasted_iota(jnp.int32, (C, LANES), 1)
    lo = lane < 64
    for p in range(RET_PAIRS):
        ks = slice(p * LANES, (p + 1) * LANES)
        vs = slice(p * 2 * RET_DV, (p + 1) * 2 * RET_DV)
        col_a = lax.broadcasted_iota(jnp.int32, (C, 2 * C), 1) < C
        raw_f = jnp.where(col_a, jnp.full((C, 2 * C), dec_ref[0, 2 * p], F32), jnp.full((C, 2 * C), dec_ref[0, 2 * p + 1], F32))
        raw_b = jnp.where(col_a, jnp.full((C, 2 * C), dec_ref[1, 2 * p], F32), jnp.full((C, 2 * C), dec_ref[1, 2 * p + 1], F32))
        dmat = jnp.where(rel >= 0, jnp.exp(-jnp.exp(raw_f) * jnp.maximum(rel, 0.0)),
                         jnp.exp(-jnp.exp(raw_b) * jnp.maximum(-rel, 0.0)))
        lg_f = _pair_lg(dec_ref, 0, p, (C, LANES))
        wqf = jnp.exp(lg_f * (pos + 1.0))
        wqb = jnp.exp(_pair_lg(dec_ref, 1, p, (C, LANES)) * (float(C) - pos))
        wkf = jnp.exp(lg_f * (C - 1.0 - pos))
        gf = _row_decay(dec_ref, 0, p)
        sf = sfs[p]
        for cc in range(RET_STEP_CHUNKS):
            rs = slice(cc * C, (cc + 1) * C)
            q = q_ref[rs, ks]
            k = k_ref[rs, ks]
            v = v_ref[rs, vs]
            zk = jnp.zeros_like(k)
            kst = jnp.concatenate([jnp.where(lo, k, zk), jnp.where(lo, zk, k)], axis=0)
            s = lax.dot_general(q, kst, (((1,), (1,)), ((), ())), preferred_element_type=F32)
            sd = (s * dmat).astype(BF16)
            qf32 = q.astype(F32)
            qwf = (qf32 * wqf).astype(BF16)
            qwb = (qf32 * wqb).astype(BF16)
            zv = jnp.zeros((C, RET_DV), BF16)
            vbd = jnp.concatenate([jnp.concatenate([v[:, :RET_DV], zv], axis=1),
                                   jnp.concatenate([zv, v[:, RET_DV:]], axis=1)], axis=0)
            lhs = jnp.concatenate([sd, qwf, qwb], axis=1)
            rhs = jnp.concatenate([vbd, sf.astype(BF16), _expand_state(sb_ref[cc, p])], axis=0)
            o = jnp.dot(lhs, rhs, preferred_element_type=F32)
            sf = gf * sf + _kv_pair(k, v, wkf)
            for t in range(2):
                oh = o[:, t * RET_DV:(t + 1) * RET_DV]
                oh = oh * lax.rsqrt(jnp.mean(oh * oh, axis=-1, keepdims=True) + NORM_EPS)
                cs = slice(p * 2 * RET_DV + t * RET_DV, p * 2 * RET_DV + (t + 1) * RET_DV)
                gt = g_ref[rs, cs].astype(F32)
                o_ref[rs, cs] = (oh * (gt / (1.0 + jnp.exp(-gt)))).astype(BF16)
        sfs[p] = sf


def _ret_out(dec, proj, sb, cproj):
    L = proj.shape[0]
    lc = cproj.shape[0]
    S = RET_STEP_CHUNKS
    R = S * RET_CHUNK
    n = L // R
    return pl.pallas_call(
        _ret_out_kernel,
        grid=(n,),
        in_specs=[pl.BlockSpec(memory_space=pltpu.SMEM),
                  pl.BlockSpec((R, RET_QK_COLS), lambda i: (i, 0)),
                  pl.BlockSpec((R, RET_QK_COLS), lambda i: (i, 1)),
                  pl.BlockSpec((R, RET_V_COLS), lambda i: (i, 1)),
                  pl.BlockSpec((R, RET_V_COLS), lambda i: (i, 2)),
                  pl.BlockSpec((S, RET_PAIRS, LANES, RET_DV), lambda i: (i, 0, 0, 0)),
                  pl.BlockSpec((lc, RET_QK_COLS), lambda i: (0, 1)),
                  pl.BlockSpec((lc, RET_V_COLS), lambda i: (0, 1))],
        out_specs=pl.BlockSpec((R, RET_HEADS * RET_DV), lambda i: (i, 0)),
        out_shape=jax.ShapeDtypeStruct((L, RET_HEADS * RET_DV), BF16),
        scratch_shapes=[pltpu.VMEM((RET_PAIRS, LANES, 2 * RET_DV), F32)],
        compiler_params=_params("arbitrary"),
        name="ret_out",
    )(dec, proj, proj, proj, proj, sb, cproj, cproj)


def _attn_kernel(sink_ref, q_ref, kp_ref, kc_ref, kn_ref, vp_ref, vc_ref, vn_ref, ck_ref, cv_ref, o_ref, *, fillers):
    n = pl.program_id(0)
    nstep = pl.num_programs(0)
    B = ATT_BLOCK
    SB = ATT_STEP_BLOCKS
    kj = lax.broadcasted_iota(jnp.int32, (B, B), 0)
    qi = lax.broadcasted_iota(jnp.int32, (B, B), 1)
    ok_prev = jnp.where(n > 0, 0.0, MASK_NEG).astype(F32)
    ok_next = jnp.where(n < nstep - 1, 0.0, MASK_NEG).astype(F32)

    def band(inside, ok):
        return jnp.concatenate([jnp.where(inside, ok, MASK_NEG).astype(F32)] * ATT_GROUP, axis=1)

    bias_prev = [band(kj >= qi, ok_prev if j == 0 else 0.0) for j in range(SB)]
    bias_next = [band(kj <= qi, ok_next if j == SB - 1 else 0.0) for j in range(SB)]
    lane = lax.broadcasted_iota(jnp.int32, (B, LANES), 1)
    lo = lane < 64
    hi = lane >= 64

    def keys_of(j, gs, prev_ref, cur_ref, next_ref, ctx_ref):
        prev = prev_ref[:, gs] if j == 0 else cur_ref[(j - 1) * B:j * B, gs]
        nxt = next_ref[:, gs] if j == SB - 1 else cur_ref[(j + 1) * B:(j + 2) * B, gs]
        return jnp.concatenate([prev, cur_ref[j * B:(j + 1) * B, gs], nxt, ctx_ref[:, gs]], axis=0)

    def scores(j, g):
        gs = slice(g * LANES, (g + 1) * LANES)
        kcat = keys_of(j, gs, kp_ref, kc_ref, kn_ref, ck_ref)
        qs = []
        for r in range(ATT_GROUP):
            h = ATT_GROUP * g + r
            qt = q_ref[j * B:(j + 1) * B, (h // 2) * LANES:(h // 2 + 1) * LANES]
            keep = lo if h % 2 == 0 else hi
            qs.append(jnp.where(keep, qt, jnp.zeros_like(qt)))
        q4 = jnp.concatenate(qs, axis=0)
        return lax.dot_general(kcat, q4, (((1,), (1,)), ((), ())), preferred_element_type=F32)

    def softmax(j, g, s):
        sk = jnp.concatenate([jnp.full((1, B), sink_ref[ATT_GROUP * g + r], F32)
                              for r in range(ATT_GROUP)], axis=1) * LOG2E
        s = jnp.concatenate([s[:B] + bias_prev[j], s[B:2 * B], s[2 * B:3 * B] + bias_next[j], s[3 * B:]], axis=0)
        m = jnp.maximum(jnp.max(s, axis=0, keepdims=True), sk)
        e = jnp.exp2(s - m)
        den = jnp.sum(e, axis=0, keepdims=True) + jnp.exp2(sk - m)
        return e.astype(BF16), den

    def values(j, g, e, den):
        gs = slice(g * LANES, (g + 1) * LANES)
        vcat = keys_of(j, gs, vp_ref, vc_ref, vn_ref, cv_ref)
        res = lax.dot_general(vcat, e, (((0,), (0,)), ((), ())), preferred_element_type=F32) * (1.0 / den)
        for t in range(2):
            even = res[:, (2 * t) * B:(2 * t + 1) * B].T
            odd = res[:, (2 * t + 1) * B:(2 * t + 2) * B].T
            c0 = (2 * g + t) * LANES
            o_ref[j * B:(j + 1) * B, c0:c0 + LANES] = jnp.where(lo, even, odd).astype(BF16)

    units = [(j, g) for j in range(SB) for g in range(ATT_KV_HEADS)]
    per_unit = -(-len(fillers) // len(units))
    s_next = scores(*units[0])
    pending = None
    for u, unit in enumerate(units):
        s_cur = s_next
        if u + 1 < len(units):
            s_next = scores(*units[u + 1])
        e_den = softmax(*unit, s_cur)
        for fill in fillers[u * per_unit:(u + 1) * per_unit]:
            fill()
        if pending is not None:
            values(*units[u - 1], *pending)
        pending = e_den
    values(*units[-1], *pending)


def _attn(sink, proj, kd, vd, ckd, cvd, riders):
    L = proj.shape[0]
    B = ATT_BLOCK
    SB = ATT_STEP_BLOCKS
    n = L // (SB * B)
    nb = L // B
    lc = ckd.shape[0]
    prev = pl.BlockSpec((B, KV_DUP_COLS), lambda i: (jnp.maximum(i * SB - 1, 0), 0))
    cur = pl.BlockSpec((SB * B, KV_DUP_COLS), lambda i: (i, 0))
    nxt = pl.BlockSpec((B, KV_DUP_COLS), lambda i: (jnp.minimum((i + 1) * SB, nb - 1), 0))
    full = pl.BlockSpec((lc, KV_DUP_COLS), lambda i: (0, 0))
    rid_in_specs, rid_out_specs, rid_shapes = _rider_specs(riders, n)
    return pl.pallas_call(
        _with_cast_riders(_attn_kernel, 10, 1, len(riders)),
        grid=(n,),
        in_specs=[pl.BlockSpec(memory_space=pltpu.SMEM),
                  pl.BlockSpec((SB * B, ATT_Q_COLS), lambda i: (i, 3)),
                  prev, cur, nxt, prev, cur, nxt, full, full] + rid_in_specs,
        out_specs=[pl.BlockSpec((SB * B, ATT_HEADS * ATT_DH), lambda i: (i, 0))] + rid_out_specs,
        out_shape=[jax.ShapeDtypeStruct((L, ATT_HEADS * ATT_DH), BF16)] + rid_shapes,
        compiler_params=_params("parallel"),
        name="attn",
    )(sink, proj, kd, kd, kd, vd, vd, vd, ckd, cvd, *[r[0] for r in riders])


def _out_proj_kernel(yr_ref, ya_ref, w_ref, x_hbm, gt_ref, g_ref, sh_ref, sc_ref, o_ref, h_ref, x_buf, sem):
    i = pl.program_id(0)
    rows = o_ref.shape[0]
    slot = lax.rem(i, 2)

    def x_copy(step, buf):
        src = x_hbm.at[pl.ds(pl.multiple_of(step * rows, rows), rows), :]
        return pltpu.make_async_copy(src, x_buf.at[buf], sem.at[buf])

    @pl.when(i == 0)
    def _():
        x_copy(0, 0).start(priority=1)

    @pl.when(i + 1 < pl.num_programs(0))
    def _():
        x_copy(i + 1, 1 - slot).start(priority=1)

    x_copy(i, slot).wait()
    x_ref = x_buf.at[slot]
    kr = yr_ref.shape[1]
    for r in range(yr_ref.shape[0] // OUT_ROW_CHUNK):
        rs = slice(r * OUT_ROW_CHUNK, (r + 1) * OUT_ROW_CHUNK)
        acc = jnp.dot(yr_ref[rs, :], w_ref[:kr, :], preferred_element_type=F32)
        acc = acc + jnp.dot(ya_ref[rs, :], w_ref[kr:, :], preferred_element_type=F32)
        x1 = x_ref[rs, :] + gt_ref[...] * acc
        o_ref[rs, :] = x1
        y = x1 * lax.rsqrt(jnp.mean(x1 * x1, axis=-1, keepdims=True) + NORM_EPS)
        y = y * g_ref[...]
        h_ref[rs, :] = (y * (1.0 + sc_ref[...]) + sh_ref[...]).astype(BF16)


def _out_proj(yr, ya, w, x, gt, g, sh, sc, *, tm):
    m, d = x.shape
    kr, ka = yr.shape[1], ya.shape[1]
    row = lambda i: (i, 0)
    vec = pl.BlockSpec((1, d), lambda i: (0, 0))
    return pl.pallas_call(
        _out_proj_kernel,
        grid=(m // tm,),
        in_specs=[pl.BlockSpec((tm, kr), row), pl.BlockSpec((tm, ka), row),
                  pl.BlockSpec((kr + ka, d), lambda i: (0, 0)),
                  pl.BlockSpec(memory_space=pl.ANY), vec, vec, vec, vec],
        out_specs=[pl.BlockSpec((tm, d), row), pl.BlockSpec((tm, d), row)],
        out_shape=[jax.ShapeDtypeStruct((m, d), F32), jax.ShapeDtypeStruct((m, d), BF16)],
        scratch_shapes=[pltpu.VMEM((2, tm, d), F32), pltpu.SemaphoreType.DMA((2,))],
        compiler_params=_params("arbitrary"),
        name="out_proj",
    )(yr, ya, w, x, gt, g, sh, sc)


def _ffn_kernel(h_ref, gt_ref, gfin_ref, wg_ref, wu_ref, wd_ref, x_hbm, o_ref, x_buf, sem):
    i = pl.program_id(0)
    f = pl.program_id(1)
    last = pl.num_programs(1) - 1
    rows = o_ref.shape[0]
    x_copy = pltpu.make_async_copy(x_hbm.at[pl.ds(pl.multiple_of(i * rows, rows), rows), :], x_buf, sem.at[0])

    def step(first, final):
        wd = wd_ref[...].astype(BF16)
        for r in range(rows // FFN_ROW_CHUNK):
            rs = slice(r * FFN_ROW_CHUNK, (r + 1) * FFN_ROW_CHUNK)
            h = h_ref[rs, :]
            a = jnp.dot(h, wg_ref[0], preferred_element_type=F32)
            u = jnp.dot(h, wu_ref[0], preferred_element_type=F32)
            act = ((a / (1.0 + jnp.exp(-a))) * u).astype(BF16)
            part = jnp.dot(act, wd, preferred_element_type=F32)
            if first:
                o_ref[rs, :] = part
            elif not final:
                o_ref[rs, :] += part
            else:
                y = x_buf[rs, :] + gt_ref[...] * (o_ref[rs, :] + part)
                y = y * lax.rsqrt(jnp.mean(y * y, axis=-1, keepdims=True) + NORM_EPS)
                o_ref[rs, :] = y * gfin_ref[...]

    @pl.when(f == 0)
    def _():
        x_copy.start()
        step(first=True, final=False)

    @pl.when((f > 0) & (f < last))
    def _():
        step(first=False, final=False)

    @pl.when(f == last)
    def _():
        x_copy.wait()
        step(first=False, final=True)


def _ffn(h, x, gt, gfin, wg, wu, wd, *, tm):
    m, d = x.shape
    nf = wg.shape[0]
    assert wg.shape == wu.shape == (nf, d, FFN_TILE) and wd.shape == (nf * FFN_TILE, d)
    assert m % tm == 0 and tm % FFN_ROW_CHUNK == 0
    row = lambda i, f: (i, 0)
    vec = pl.BlockSpec((1, d), lambda i, f: (0, 0))
    wcol = pl.BlockSpec((1, d, FFN_TILE), lambda i, f: (f, 0, 0))
    return pl.pallas_call(
        _ffn_kernel,
        grid=(m // tm, nf),
        in_specs=[pl.BlockSpec((tm, d), row), vec, vec, wcol, wcol,
                  pl.BlockSpec((FFN_TILE, d), lambda i, f: (f, 0)),
                  pl.BlockSpec(memory_space=pl.ANY)],
        out_specs=pl.BlockSpec((tm, d), row),
        out_shape=jax.ShapeDtypeStruct((m, d), F32),
        scratch_shapes=[pltpu.VMEM((tm, d), F32), pltpu.SemaphoreType.DMA((1,))],
        compiler_params=_params("arbitrary", "arbitrary"),
        name="ffn",
    )(h, gt, gfin, wg, wu, wd, x)


def _rope_tables(L):
    f32 = np.float32
    lane = np.arange(LANES)
    inv1 = f32(ROPE_BASE) ** (-np.arange(32, dtype=f32) / f32(32))
    ang1 = np.arange(L, dtype=f32)[:, None] * inv1[None, :]
    sgn1 = np.where((lane % 64) < 32, -1.0, 1.0).astype(f32)
    cos1 = np.tile(np.cos(ang1), (1, LANES // 32))
    sin1 = np.tile(np.sin(ang1), (1, LANES // 32)) * sgn1[None, :]
    inv2 = f32(ROPE_BASE) ** (-np.arange(16, dtype=f32) / f32(16))
    nrow = L // GRID_W
    ang_r = np.arange(nrow, dtype=f32)[:, None] * inv2[None, :]
    ang_c = np.arange(GRID_W, dtype=f32)[:, None] * inv2[None, :]
    sgna = np.where((lane % 32) < 16, -1.0, 1.0).astype(f32)

    def expand(fr, fc):
        by_row = np.broadcast_to(np.tile(fr, (1, 2))[:, None, :], (nrow, GRID_W, 32))
        by_col = np.broadcast_to(np.tile(fc, (1, 2))[None, :, :], (nrow, GRID_W, 32))
        head = np.concatenate([by_row, by_col], axis=-1).reshape(L, 64)
        return np.tile(head, (1, LANES // 64))

    cosa = expand(np.cos(ang_r), np.cos(ang_c))
    sina = expand(np.sin(ang_r), np.sin(ang_c)) * sgna[None, :]
    return tuple(np.ascontiguousarray(t, dtype=f32) for t in (cos1, sin1, cosa, sina))


def kernel(x, c, ctx, c_ctx, w_mod, b_mod, norm_mix, norm_ffn, w_in, ret_decay, attn_sink,
           w_out, w_gate, w_up, w_down, norm_final):
    B, L, D = x.shape
    assert B == 1 and w_mod.shape[0] == 1, "single batch element, depth-1 layer"
    x2 = x[0]
    xc2 = ctx[0]

    cv = jnp.zeros((8, D), F32).at[0].set(c[0]).at[1].set(c_ctx)
    mod = _mod(cv, w_mod[0], b_mod[0][None, :], 2 * D)
    sh_m, sc_m = mod[0:1, 0:D], mod[0:1, D:2 * D]
    sh_mc, sc_mc = mod[1:2, 0:D], mod[1:2, D:2 * D]

    g_mix = norm_mix[0][None, :]
    cproj, ckd, cvd, w_in_b = _ctx_proj(xc2, g_mix, sh_mc, sc_mc, w_in[0])
    proj, kd, vd, mod_rest = _in_proj(x2, g_mix, sh_m, sc_m, w_in_b, _rope_tables(L),
                                      c[0][:, None], w_mod[0], b_mod[0][None, :], 2 * D, tm=ROW_TILE)
    gt_m, sh_f, sc_f, gt_f = [mod_rest.reshape(1, 4 * D)[:, k * D:(k + 1) * D] for k in range(4)]

    dec = ret_decay[0].astype(F32)
    sb = _ret_bwd_states(dec, proj, cproj)
    y_ret = _ret_out(dec, proj, sb, cproj)
    y_att, w_gate_b, w_up_b, w_out_b = _attn(
        attn_sink[0].astype(F32), proj, kd, vd, ckd, cvd,
        [(w_gate[0], 1, FFN_TILE), (w_up[0], 1, FFN_TILE), (w_out[0], 1, None)])

    x1, hff = _out_proj(y_ret, y_att, w_out_b, x2, gt_m, norm_ffn[0][None, :], sh_f, sc_f, tm=ROW_TILE)
    out = _ffn(hff, x1, gt_f, norm_final[None, :], w_gate_b, w_up_b, w_down[0], tm=FFN_ROW_TILE)
    return out[None]
```

```python
import jax
import jax.numpy as jnp
import numpy as np
from jax import lax
from jax.experimental import pallas as pl
from jax.experimental.pallas import tpu as pltpu

GRID_W = 64
RET_HEADS = 8
RET_DK = 64
RET_DV = 128
RET_CHUNK = 128
ATT_HEADS = 16
ATT_KV_HEADS = 4
ATT_DH = 64
ATT_GROUP = ATT_HEADS // ATT_KV_HEADS
WINDOW = 128
ATT_BLOCK = 128
ROPE_BASE = 10000.0
NORM_EPS = 1e-6
K_SCALE = RET_DK ** -0.5
ATT_SCALE = ATT_DH ** -0.5
LOG2E = 1.4426950408889634

RET_QK_COLS = RET_HEADS * RET_DK
RET_V_COLS = RET_HEADS * RET_DV
ATT_Q_COLS = ATT_HEADS * ATT_DH
KV_DUP_COLS = 2 * ATT_KV_HEADS * ATT_DH

LANES = 128
RET_PAIRS = RET_HEADS // 2
MASK_NEG = -1e30
VMEM_LIMIT = 56 * 1024 * 1024
CAST_PIECE_ROWS = 16
RET_STEP_CHUNKS = 16
ATT_STEP_BLOCKS = 4
ROW_TILE = 512
OUT_ROW_CHUNK = ROW_TILE
IN_ROW_CHUNK = ROW_TILE
FFN_TILE = 512
FFN_ROW_TILE = 1024
FFN_ROW_CHUNK = FFN_ROW_TILE

BF16 = jnp.bfloat16
F32 = jnp.float32


def _params(*sem):
    return pltpu.CompilerParams(dimension_semantics=sem, vmem_limit_bytes=VMEM_LIMIT)


def _with_cast_riders(body, n_in, n_out, n_rid):
    def wrapped(*refs):
        ins = refs[:n_in]
        rid_in = refs[n_in:n_in + n_rid]
        outs = refs[n_in + n_rid:n_in + n_rid + n_out]
        rid_out = refs[n_in + n_rid + n_out:n_in + 2 * n_rid + n_out]
        scratch = refs[n_in + 2 * n_rid + n_out:]

        def piece(src, dst, r0):
            rs = slice(r0, r0 + CAST_PIECE_ROWS)
            if len(dst.shape) == 2:
                dst[rs, :] = src[rs, :].astype(BF16)
            else:
                tc = dst.shape[2]
                for t in range(dst.shape[0]):
                    dst[t, rs, :] = src[rs, t * tc:(t + 1) * tc].astype(BF16)

        pieces = [(lambda s=src, d=dst, r=r0: piece(s, d, r))
                  for src, dst in zip(rid_in, rid_out) for r0 in range(0, src.shape[0], CAST_PIECE_ROWS)]
        done = []
        fillers = [(lambda p=p: (done.append(1), p())) for p in pieces]
        body(*ins, *outs, *scratch, fillers=fillers)
        assert len(done) == len(pieces), "every cast piece must be emitted exactly once"
    return wrapped


def _rider_specs(riders, steps):
    in_specs, out_specs, shapes = [], [], []
    for w, ncb, tile in riders:
        rows, cols = w.shape
        nrb = steps // ncb
        assert nrb * ncb == steps and rows % nrb == 0 and cols % ncb == 0
        br, bc = rows // nrb, cols // ncb
        assert br % CAST_PIECE_ROWS == 0 and bc % LANES == 0, "slab must be bf16-tile aligned"
        in_specs.append(pl.BlockSpec((br, bc), lambda i, ncb=ncb: (i // ncb, i % ncb)))
        if tile is None:
            out_specs.append(in_specs[-1])
            shapes.append(jax.ShapeDtypeStruct(w.shape, BF16))
        else:
            assert ncb == 1 and cols % tile == 0 and tile % LANES == 0
            out_specs.append(pl.BlockSpec((cols // tile, br, tile), lambda i: (0, i, 0)))
            shapes.append(jax.ShapeDtypeStruct((cols // tile, rows, tile), BF16))
    return in_specs, out_specs, shapes


def _mod_kernel(cv_ref, w_ref, b_ref, o_ref):
    cv = cv_ref[...]
    s = cv / (1.0 + jnp.exp(-cv))
    o_ref[...] = jnp.dot(s.astype(BF16), w_ref[...].astype(BF16),
                         preferred_element_type=F32) + b_ref[...]


def _mod(cv, w, b, n):
    d = w.shape[0]
    tn = 512
    assert n % tn == 0
    return pl.pallas_call(
        _mod_kernel,
        grid=(n // tn,),
        in_specs=[pl.BlockSpec((8, d), lambda j: (0, 0)),
                  pl.BlockSpec((d, tn), lambda j: (0, j)),
                  pl.BlockSpec((1, tn), lambda j: (0, j))],
        out_specs=pl.BlockSpec((8, tn), lambda j: (0, j)),
        out_shape=jax.ShapeDtypeStruct((8, n), F32),
        compiler_params=_params("parallel"),
        name="mod",
    )(cv, w, b)


def _rot_pairs(a, cos, sin_signed, half):
    lane = lax.broadcasted_iota(jnp.int32, a.shape, 1)
    first = (lane % (2 * half)) < half
    rot = jnp.where(first, pltpu.roll(a, LANES - half, 1), pltpu.roll(a, half, 1))
    return a * cos + rot * sin_signed


def _dup_halves(a):
    lane = lax.broadcasted_iota(jnp.int32, a.shape, 1)
    r = pltpu.roll(a, 64, 1)
    lo = lane < 64
    return jnp.where(lo, a, r), jnp.where(lo, r, a)


_PROJ_TILE = 512
_PROJ_TILE_KINDS = ("ret_q", "ret_k", "plain", "plain", "plain", "plain", "att_q", "att_q", "att_kv")


def _in_proj_kernel(x_ref, g_ref, sh_ref, sc_ref, w_ref, c1_ref, s1_ref, ca_ref, sa_ref,
                    cc_ref, wm_ref, bm_ref, o_ref, kd_ref, vd_ref, mod_ref):
    def mod_rider():
        cc = cc_ref[...]
        s_col = cc / (1.0 + jnp.exp(-cc))
        mod_ref[0] = jnp.sum(wm_ref[...] * s_col, axis=0, keepdims=True) + bm_ref[...]

    tn = _PROJ_TILE
    for r in range(x_ref.shape[0] // IN_ROW_CHUNK):
        rs = slice(r * IN_ROW_CHUNK, (r + 1) * IN_ROW_CHUNK)
        xf = x_ref[rs, :]
        y = xf * lax.rsqrt(jnp.mean(xf * xf, axis=-1, keepdims=True) + NORM_EPS)
        y = y * g_ref[...]
        h = (y * (1.0 + sc_ref[...]) + sh_ref[...]).astype(BF16)

        def rope1(a):
            return _rot_pairs(a, c1_ref[rs, :], s1_ref[rs, :], 32)

        def ropea(a):
            return _rot_pairs(a, ca_ref[rs, :], sa_ref[rs, :], 16)

        order = sorted(range(len(_PROJ_TILE_KINDS)), key=lambda t: _PROJ_TILE_KINDS[t] == "plain")
        for j in order:
            kind = _PROJ_TILE_KINDS[j]
            acc = jnp.dot(h, w_ref[:, j * tn:(j + 1) * tn], preferred_element_type=F32)
            if r == 0 and j == order[-2]:
                mod_rider()
            for c in range(tn // LANES):
                a = acc[:, c * LANES:(c + 1) * LANES]
                if kind == "ret_q":
                    a = rope1(a)
                elif kind == "ret_k":
                    a = rope1(a) * K_SCALE
                elif kind == "att_q":
                    a = ropea(a) * (ATT_SCALE * LOG2E)
                elif kind == "att_kv" and c < 2:
                    a = ropea(a)
                o_ref[rs, j * tn + c * LANES:j * tn + (c + 1) * LANES] = a.astype(BF16)
                if kind == "att_kv":
                    dup_ref = kd_ref if c < 2 else vd_ref
                    d0, d1 = _dup_halves(a)
                    t = 2 * (c % 2)
                    dup_ref[rs, t * LANES:(t + 1) * LANES] = d0.astype(BF16)
                    dup_ref[rs, (t + 1) * LANES:(t + 2) * LANES] = d1.astype(BF16)


def _in_proj(x, g, sh, sc, w, tabs, c_col, w_mod, b_mod, mod_done, *, tm):
    m, d = x.shape
    n = w.shape[1]
    assert n == _PROJ_TILE * len(_PROJ_TILE_KINDS) and m % tm == 0 and tm % IN_ROW_CHUNK == 0
    steps = m // tm
    slab = (w_mod.shape[1] - mod_done) // steps
    assert slab * steps == w_mod.shape[1] - mod_done and slab % LANES == 0 and mod_done % slab == 0
    slab0 = mod_done // slab
    c1, s1, ca, sa = tabs
    row = lambda i: (i, 0)
    vec = pl.BlockSpec((1, d), lambda i: (0, 0))
    tab = pl.BlockSpec((tm, LANES), row)
    return pl.pallas_call(
        _in_proj_kernel,
        grid=(m // tm,),
        in_specs=[pl.BlockSpec((tm, d), row), vec, vec, vec,
                  pl.BlockSpec((d, n), lambda i: (0, 0), pipeline_mode=pl.Buffered(1)),
                  tab, tab, tab, tab,
                  pl.BlockSpec((d, 1), lambda i: (0, 0)),
                  pl.BlockSpec((d, slab), lambda i: (0, slab0 + i)),
                  pl.BlockSpec((1, slab), lambda i: (0, slab0 + i))],
        out_specs=[pl.BlockSpec((tm, n), row),
                   pl.BlockSpec((tm, KV_DUP_COLS), row),
                   pl.BlockSpec((tm, KV_DUP_COLS), row),
                   pl.BlockSpec((1, 1, slab), lambda i: (i, 0, 0))],
        out_shape=[jax.ShapeDtypeStruct((m, n), BF16),
                   jax.ShapeDtypeStruct((m, KV_DUP_COLS), BF16),
                   jax.ShapeDtypeStruct((m, KV_DUP_COLS), BF16),
                   jax.ShapeDtypeStruct((steps, 1, slab), F32)],
        compiler_params=_params("parallel"),
        name="in_proj",
    )(x, g, sh, sc, w, c1, s1, ca, sa, c_col, w_mod, b_mod)


def _ctx_proj_kernel(x_ref, g_ref, sh_ref, sc_ref, w_ref, o_ref, kd_ref, vd_ref, wb_ref, h_ref):
    j = pl.program_id(0)

    @pl.when(j == 0)
    def _():
        xf = x_ref[...]
        y = xf * lax.rsqrt(jnp.mean(xf * xf, axis=-1, keepdims=True) + NORM_EPS)
        y = y * g_ref[...]
        h_ref[...] = (y * (1.0 + sc_ref[...]) + sh_ref[...]).astype(BF16)

    wb = w_ref[...].astype(BF16)
    wb_ref[...] = wb
    acc = jnp.dot(h_ref[...], wb, preferred_element_type=F32)
    is_ret_k = _PROJ_TILE_KINDS.index("ret_k")
    o_ref[...] = (acc * jnp.where(j == is_ret_k, K_SCALE, 1.0)).astype(BF16)

    @pl.when(j == _PROJ_TILE_KINDS.index("att_kv"))
    def _():
        for c in range(_PROJ_TILE // LANES):
            dup_ref = kd_ref if c < 2 else vd_ref
            d0, d1 = _dup_halves(acc[:, c * LANES:(c + 1) * LANES])
            t = 2 * (c % 2)
            dup_ref[:, t * LANES:(t + 1) * LANES] = d0.astype(BF16)
            dup_ref[:, (t + 1) * LANES:(t + 2) * LANES] = d1.astype(BF16)


def _ctx_proj(x, g, sh, sc, w):
    m, d = x.shape
    n = w.shape[1]
    tn = _PROJ_TILE
    assert n == tn * len(_PROJ_TILE_KINDS)
    fixed = lambda j: (0, 0)
    vec = pl.BlockSpec((1, d), fixed)
    return pl.pallas_call(
        _ctx_proj_kernel,
        grid=(n // tn,),
        in_specs=[pl.BlockSpec((m, d), fixed), vec, vec, vec,
                  pl.BlockSpec((d, tn), lambda j: (0, j))],
        out_specs=[pl.BlockSpec((m, tn), lambda j: (0, j)),
                   pl.BlockSpec((m, KV_DUP_COLS), fixed),
                   pl.BlockSpec((m, KV_DUP_COLS), fixed),
                   pl.BlockSpec((d, tn), lambda j: (0, j))],
        out_shape=[jax.ShapeDtypeStruct((m, n), BF16),
                   jax.ShapeDtypeStruct((m, KV_DUP_COLS), BF16),
                   jax.ShapeDtypeStruct((m, KV_DUP_COLS), BF16),
                   jax.ShapeDtypeStruct((d, n), BF16)],
        scratch_shapes=[pltpu.VMEM((m, d), BF16)],
        compiler_params=_params("arbitrary"),
        name="ctx_proj",
    )(x, g, sh, sc, w)


def _pair_lg(dec_ref, d, p, shape):
    lane = lax.broadcasted_iota(jnp.int32, shape, 1)
    first = (lane % LANES) < 64
    raw = jnp.where(first, jnp.full(shape, dec_ref[d, 2 * p], F32), jnp.full(shape, dec_ref[d, 2 * p + 1], F32))
    return -jnp.exp(raw)


def _head_block_mask(shape):
    r = lax.broadcasted_iota(jnp.int32, shape, 0)
    c = lax.broadcasted_iota(jnp.int32, shape, 1)
    return (r // 64) == (c // LANES)


def _kv_pair(k_pair, v_pair, w):
    kw = (k_pair.astype(F32) * w).astype(BF16)
    kv = lax.dot_general(kw, v_pair, (((0,), (0,)), ((), ())), preferred_element_type=F32)
    return jnp.where(_head_block_mask(kv.shape), kv, 0.0)


def _row_decay(dec_ref, d, p):
    shape = (LANES, 2 * RET_DV)
    rowh = lax.broadcasted_iota(jnp.int32, shape, 0) < 64
    raw = jnp.where(rowh, jnp.full(shape, dec_ref[d, 2 * p], F32), jnp.full(shape, dec_ref[d, 2 * p + 1], F32))
    return jnp.exp(-jnp.exp(raw) * float(RET_CHUNK))


def _compact_state(s):
    row = lax.broadcasted_iota(jnp.int32, (LANES, RET_DV), 0)
    return jnp.where(row < 64, s[:, :RET_DV], s[:, RET_DV:])


def _expand_state(c):
    row = lax.broadcasted_iota(jnp.int32, c.shape, 0)
    z = jnp.zeros_like(c)
    return jnp.concatenate([jnp.where(row < 64, c, z), jnp.where(row < 64, z, c)], axis=1)


def _ret_bwd_kernel(dec_ref, k_ref, v_ref, ck_ref, cv_ref, sb_ref, sbs):
    i = pl.program_id(0)
    C = RET_CHUNK
    lc = ck_ref.shape[0]

    @pl.when(i == 0)
    def _():
        pos = lax.broadcasted_iota(jnp.int32, (lc, LANES), 0).astype(F32)
        for p in range(RET_PAIRS):
            ks = slice(p * LANES, (p + 1) * LANES)
            vs = slice(p * 2 * RET_DV, (p + 1) * 2 * RET_DV)
            wb = jnp.exp(_pair_lg(dec_ref, 1, p, (lc, LANES)) * pos)
            sbs[p] = _kv_pair(ck_ref[:, ks], cv_ref[:, vs], wb)

    pos = lax.broadcasted_iota(jnp.int32, (C, LANES), 0).astype(F32)
    for p in range(RET_PAIRS):
        ks = slice(p * LANES, (p + 1) * LANES)
        vs = slice(p * 2 * RET_DV, (p + 1) * 2 * RET_DV)
        wb = jnp.exp(_pair_lg(dec_ref, 1, p, (C, LANES)) * pos)
        gb = _row_decay(dec_ref, 1, p)
        sb = sbs[p]
        for cc in reversed(range(RET_STEP_CHUNKS)):
            rs = slice(cc * C, (cc + 1) * C)
            sb_ref[cc, p] = _compact_state(sb).astype(BF16)
            sb = gb * sb + _kv_pair(k_ref[rs, ks], v_ref[rs, vs], wb)
        sbs[p] = sb


def _ret_bwd_states(dec, proj, cproj):
    L = proj.shape[0]
    lc = cproj.shape[0]
    S = RET_STEP_CHUNKS
    R = S * RET_CHUNK
    n = L // R
    return pl.pallas_call(
        _ret_bwd_kernel,
        grid=(n,),
        in_specs=[pl.BlockSpec(memory_space=pltpu.SMEM),
                  pl.BlockSpec((R, RET_QK_COLS), lambda i: (n - 1 - i, 1)),
                  pl.BlockSpec((R, RET_V_COLS), lambda i: (n - 1 - i, 1)),
                  pl.BlockSpec((lc, RET_QK_COLS), lambda i: (0, 1)),
                  pl.BlockSpec((lc, RET_V_COLS), lambda i: (0, 1))],
        out_specs=pl.BlockSpec((S, RET_PAIRS, LANES, RET_DV), lambda i: (n - 1 - i, 0, 0, 0)),
        out_shape=jax.ShapeDtypeStruct((n * S, RET_PAIRS, LANES, RET_DV), BF16),
        scratch_shapes=[pltpu.VMEM((RET_PAIRS, LANES, 2 * RET_DV), F32)],
        compiler_params=_params("arbitrary"),
        name="ret_bwd",
    )(dec, proj, proj, cproj, cproj)


def _ret_out_kernel(dec_ref, q_ref, k_ref, v_ref, g_ref, sb_ref, ck_ref, cv_ref, o_ref, sfs):
    i = pl.program_id(0)
    C = RET_CHUNK
    lc = ck_ref.shape[0]

    @pl.when(i == 0)
    def _():
        cpos = lax.broadcasted_iota(jnp.int32, (lc, LANES), 0).astype(F32)
        for p in range(RET_PAIRS):
            ks = slice(p * LANES, (p + 1) * LANES)
            vs = slice(p * 2 * RET_DV, (p + 1) * 2 * RET_DV)
            wf = jnp.exp(_pair_lg(dec_ref, 0, p, (lc, LANES)) * (lc - 1.0 - cpos))
            sfs[p] = _kv_pair(ck_ref[:, ks], cv_ref[:, vs], wf)

    pos = lax.broadcasted_iota(jnp.int32, (C, LANES), 0).astype(F32)
    n_i = lax.broadcasted_iota(jnp.int32, (C, 2 * C), 0)
    m_i = lax.broadcasted_iota(jnp.int32, (C, 2 * C), 1) % C
    rel = (n_i - m_i).astype(F32)
    lane = lax.broadcasted_iota(jnp.int32, (C, LANES), 1)
    lo = lane < 64
    for p in range(RET_PAIRS):
        ks = slice(p * LANES, (p + 1) * LANES)
        vs = slice(p * 2 * RET_DV, (p + 1) * 2 * RET_DV)
        col_a = lax.broadcasted_iota(jnp.int32, (C, 2 * C), 1) < C
        raw_f = jnp.where(col_a, jnp.full((C, 2 * C), dec_ref[0, 2 * p], F32), jnp.full((C, 2 * C), dec_ref[0, 2 * p + 1], F32))
        raw_b = jnp.where(col_a, jnp.full((C, 2 * C), dec_ref[1, 2 * p], F32), jnp.full((C, 2 * C), dec_ref[1, 2 * p + 1], F32))
        dmat = jnp.where(rel >= 0, jnp.exp(-jnp.exp(raw_f) * jnp.maximum(rel, 0.0)),
                         jnp.exp(-jnp.exp(raw_b) * jnp.maximum(-rel, 0.0)))
        lg_f = _pair_lg(dec_ref, 0, p, (C, LANES))
        wqf = jnp.exp(lg_f * (pos + 1.0))
        wqb = jnp.exp(_pair_lg(dec_ref, 1, p, (C, LANES)) * (float(C) - pos))
        wkf = jnp.exp(lg_f * (C - 1.0 - pos))
        gf = _row_decay(dec_ref, 0, p)
        sf = sfs[p]
        for cc in range(RET_STEP_CHUNKS):
            rs = slice(cc * C, (cc + 1) * C)
            q = q_ref[rs, ks]
            k = k_ref[rs, ks]
            v = v_ref[rs, vs]
            zk = jnp.zeros_like(k)
            kst = jnp.concatenate([jnp.where(lo, k, zk), jnp.where(lo, zk, k)], axis=0)
            s = lax.dot_general(q, kst, (((1,), (1,)), ((), ())), preferred_element_type=F32)
            sd = (s * dmat).astype(BF16)
            qf32 = q.astype(F32)
            qwf = (qf32 * wqf).astype(BF16)
            qwb = (qf32 * wqb).astype(BF16)
            zv = jnp.zeros((C, RET_DV), BF16)
            vbd = jnp.concatenate([jnp.concatenate([v[:, :RET_DV], zv], axis=1),
                                   jnp.concatenate([zv, v[:, RET_DV:]], axis=1)], axis=0)
            lhs = jnp.concatenate([sd, qwf, qwb], axis=1)
            rhs = jnp.concatenate([vbd, sf.astype(BF16), _expand_state(sb_ref[cc, p])], axis=0)
            o = jnp.dot(lhs, rhs, preferred_element_type=F32)
            sf = gf * sf + _kv_pair(k, v, wkf)
            for t in range(2):
                oh = o[:, t * RET_DV:(t + 1) * RET_DV]
                oh = oh * lax.rsqrt(jnp.mean(oh * oh, axis=-1, keepdims=True) + NORM_EPS)
                cs = slice(p * 2 * RET_DV + t * RET_DV, p * 2 * RET_DV + (t + 1) * RET_DV)
                gt = g_ref[rs, cs].astype(F32)
                o_ref[rs, cs] = (oh * (gt / (1.0 + jnp.exp(-gt)))).astype(BF16)
        sfs[p] = sf


def _ret_out(dec, proj, sb, cproj):
    L = proj.shape[0]
    lc = cproj.shape[0]
    S = RET_STEP_CHUNKS
    R = S * RET_CHUNK
    n = L // R
    return pl.pallas_call(
        _ret_out_kernel,
        grid=(n,),
        in_specs=[pl.BlockSpec(memory_space=pltpu.SMEM),
                  pl.BlockSpec((R, RET_QK_COLS), lambda i: (i, 0)),
                  pl.BlockSpec((R, RET_QK_COLS), lambda i: (i, 1)),
                  pl.BlockSpec((R, RET_V_COLS), lambda i: (i, 1)),
                  pl.BlockSpec((R, RET_V_COLS), lambda i: (i, 2)),
                  pl.BlockSpec((S, RET_PAIRS, LANES, RET_DV), lambda i: (i, 0, 0, 0)),
                  pl.BlockSpec((lc, RET_QK_COLS), lambda i: (0, 1)),
                  pl.BlockSpec((lc, RET_V_COLS), lambda i: (0, 1))],
        out_specs=pl.BlockSpec((R, RET_HEADS * RET_DV), lambda i: (i, 0)),
        out_shape=jax.ShapeDtypeStruct((L, RET_HEADS * RET_DV), BF16),
        scratch_shapes=[pltpu.VMEM((RET_PAIRS, LANES, 2 * RET_DV), F32)],
        compiler_params=_params("arbitrary"),
        name="ret_out",
    )(dec, proj, proj, proj, proj, sb, cproj, cproj)


def _attn_kernel(sink_ref, q_ref, kp_ref, kc_ref, kn_ref, vp_ref, vc_ref, vn_ref, ck_ref, cv_ref, o_ref, *, fillers):
    n = pl.program_id(0)
    nstep = pl.num_programs(0)
    B = ATT_BLOCK
    SB = ATT_STEP_BLOCKS
    kj = lax.broadcasted_iota(jnp.int32, (B, B), 0)
    qi = lax.broadcasted_iota(jnp.int32, (B, B), 1)
    ok_prev = jnp.where(n > 0, 0.0, MASK_NEG).astype(F32)
    ok_next = jnp.where(n < nstep - 1, 0.0, MASK_NEG).astype(F32)

    def band(inside, ok):
        return jnp.concatenate([jnp.where(inside, ok, MASK_NEG).astype(F32)] * ATT_GROUP, axis=1)

    bias_prev = [band(kj >= qi, ok_prev if j == 0 else 0.0) for j in range(SB)]
    bias_next = [band(kj <= qi, ok_next if j == SB - 1 else 0.0) for j in range(SB)]
    lane = lax.broadcasted_iota(jnp.int32, (B, LANES), 1)
    lo = lane < 64
    hi = lane >= 64

    def keys_of(j, gs, prev_ref, cur_ref, next_ref, ctx_ref):
        prev = prev_ref[:, gs] if j == 0 else cur_ref[(j - 1) * B:j * B, gs]
        nxt = next_ref[:, gs] if j == SB - 1 else cur_ref[(j + 1) * B:(j + 2) * B, gs]
        return jnp.concatenate([prev, cur_ref[j * B:(j + 1) * B, gs], nxt, ctx_ref[:, gs]], axis=0)

    def scores(j, g):
        gs = slice(g * LANES, (g + 1) * LANES)
        kcat = keys_of(j, gs, kp_ref, kc_ref, kn_ref, ck_ref)
        qs = []
        for r in range(ATT_GROUP):
            h = ATT_GROUP * g + r
            qt = q_ref[j * B:(j + 1) * B, (h // 2) * LANES:(h // 2 + 1) * LANES]
            keep = lo if h % 2 == 0 else hi
            qs.append(jnp.where(keep, qt, jnp.zeros_like(qt)))
        q4 = jnp.concatenate(qs, axis=0)
        return lax.dot_general(kcat, q4, (((1,), (1,)), ((), ())), preferred_element_type=F32)

    def softmax(j, g, s):
        sk = jnp.concatenate([jnp.full((1, B), sink_ref[ATT_GROUP * g + r], F32)
                              for r in range(ATT_GROUP)], axis=1) * LOG2E
        s = jnp.concatenate([s[:B] + bias_prev[j], s[B:2 * B], s[2 * B:3 * B] + bias_next[j], s[3 * B:]], axis=0)
        m = jnp.maximum(jnp.max(s, axis=0, keepdims=True), sk)
        e = jnp.exp2(s - m)
        den = jnp.sum(e, axis=0, keepdims=True) + jnp.exp2(sk - m)
        return e.astype(BF16), den

    def values(j, g, e, den):
        gs = slice(g * LANES, (g + 1) * LANES)
        vcat = keys_of(j, gs, vp_ref, vc_ref, vn_ref, cv_ref)
        res = lax.dot_general(vcat, e, (((0,), (0,)), ((), ())), preferred_element_type=F32) * (1.0 / den)
        for t in range(2):
            even = res[:, (2 * t) * B:(2 * t + 1) * B].T
            odd = res[:, (2 * t + 1) * B:(2 * t + 2) * B].T
            c0 = (2 * g + t) * LANES
            o_ref[j * B:(j + 1) * B, c0:c0 + LANES] = jnp.where(lo, even, odd).astype(BF16)

    units = [(j, g) for j in range(SB) for g in range(ATT_KV_HEADS)]
    per_unit = -(-len(fillers) // len(units))
    s_next = scores(*units[0])
    pending = None
    for u, unit in enumerate(units):
        s_cur = s_next
        if u + 1 < len(units):
            s_next = scores(*units[u + 1])
        e_den = softmax(*unit, s_cur)
        for fill in fillers[u * per_unit:(u + 1) * per_unit]:
            fill()
        if pending is not None:
            values(*units[u - 1], *pending)
        pending = e_den
    values(*units[-1], *pending)


def _attn(sink, proj, kd, vd, ckd, cvd, riders):
    L = proj.shape[0]
    B = ATT_BLOCK
    SB = ATT_STEP_BLOCKS
    n = L // (SB * B)
    nb = L // B
    lc = ckd.shape[0]
    prev = pl.BlockSpec((B, KV_DUP_COLS), lambda i: (jnp.maximum(i * SB - 1, 0), 0))
    cur = pl.BlockSpec((SB * B, KV_DUP_COLS), lambda i: (i, 0))
    nxt = pl.BlockSpec((B, KV_DUP_COLS), lambda i: (jnp.minimum((i + 1) * SB, nb - 1), 0))
    full = pl.BlockSpec((lc, KV_DUP_COLS), lambda i: (0, 0))
    rid_in_specs, rid_out_specs, rid_shapes = _rider_specs(riders, n)
    return pl.pallas_call(
        _with_cast_riders(_attn_kernel, 10, 1, len(riders)),
        grid=(n,),
        in_specs=[pl.BlockSpec(memory_space=pltpu.SMEM),
                  pl.BlockSpec((SB * B, ATT_Q_COLS), lambda i: (i, 3)),
                  prev, cur, nxt, prev, cur, nxt, full, full] + rid_in_specs,
        out_specs=[pl.BlockSpec((SB * B, ATT_HEADS * ATT_DH), lambda i: (i, 0))] + rid_out_specs,
        out_shape=[jax.ShapeDtypeStruct((L, ATT_HEADS * ATT_DH), BF16)] + rid_shapes,
        compiler_params=_params("parallel"),
        name="attn",
    )(sink, proj, kd, kd, kd, vd, vd, vd, ckd, cvd, *[r[0] for r in riders])


def _out_proj_kernel(yr_ref, ya_ref, w_ref, x_ref, gt_ref, g_ref, sh_ref, sc_ref, o_ref, h_ref):
    kr = yr_ref.shape[1]
    for r in range(yr_ref.shape[0] // OUT_ROW_CHUNK):
        rs = slice(r * OUT_ROW_CHUNK, (r + 1) * OUT_ROW_CHUNK)
        acc = jnp.dot(yr_ref[rs, :], w_ref[:kr, :], preferred_element_type=F32)
        acc = acc + jnp.dot(ya_ref[rs, :], w_ref[kr:, :], preferred_element_type=F32)
        x1 = x_ref[rs, :] + gt_ref[...] * acc
        o_ref[rs, :] = x1
        y = x1 * lax.rsqrt(jnp.mean(x1 * x1, axis=-1, keepdims=True) + NORM_EPS)
        y = y * g_ref[...]
        h_ref[rs, :] = (y * (1.0 + sc_ref[...]) + sh_ref[...]).astype(BF16)


def _out_proj(yr, ya, w, x, gt, g, sh, sc, *, tm):
    m, d = x.shape
    kr, ka = yr.shape[1], ya.shape[1]
    row = lambda i: (i, 0)
    vec = pl.BlockSpec((1, d), lambda i: (0, 0))
    return pl.pallas_call(
        _out_proj_kernel,
        grid=(m // tm,),
        in_specs=[pl.BlockSpec((tm, kr), row), pl.BlockSpec((tm, ka), row),
                  pl.BlockSpec((kr + ka, d), lambda i: (0, 0)),
                  pl.BlockSpec((tm, d), row), vec, vec, vec, vec],
        out_specs=[pl.BlockSpec((tm, d), row), pl.BlockSpec((tm, d), row)],
        out_shape=[jax.ShapeDtypeStruct((m, d), F32), jax.ShapeDtypeStruct((m, d), BF16)],
        compiler_params=_params("parallel"),
        name="out_proj",
    )(yr, ya, w, x, gt, g, sh, sc)


def _ffn_kernel(h_ref, gt_ref, gfin_ref, wg_ref, wu_ref, wd_ref, x_hbm, o_ref, x_buf, sem):
    i = pl.program_id(0)
    f = pl.program_id(1)
    last = pl.num_programs(1) - 1
    rows = o_ref.shape[0]
    x_copy = pltpu.make_async_copy(x_hbm.at[pl.ds(pl.multiple_of(i * rows, rows), rows), :], x_buf, sem.at[0])

    def step(first, final):
        wd = wd_ref[...].astype(BF16)
        for r in range(rows // FFN_ROW_CHUNK):
            rs = slice(r * FFN_ROW_CHUNK, (r + 1) * FFN_ROW_CHUNK)
            h = h_ref[rs, :]
            a = jnp.dot(h, wg_ref[0], preferred_element_type=F32)
            u = jnp.dot(h, wu_ref[0], preferred_element_type=F32)
            act = ((a / (1.0 + jnp.exp(-a))) * u).astype(BF16)
            part = jnp.dot(act, wd, preferred_element_type=F32)
            if first:
                o_ref[rs, :] = part
            elif not final:
                o_ref[rs, :] += part
            else:
                y = x_buf[rs, :] + gt_ref[...] * (o_ref[rs, :] + part)
                y = y * lax.rsqrt(jnp.mean(y * y, axis=-1, keepdims=True) + NORM_EPS)
                o_ref[rs, :] = y * gfin_ref[...]

    @pl.when(f == 0)
    def _():
        x_copy.start()
        step(first=True, final=False)

    @pl.when((f > 0) & (f < last))
    def _():
        step(first=False, final=False)

    @pl.when(f == last)
    def _():
        x_copy.wait()
        step(first=False, final=True)


def _ffn(h, x, gt, gfin, wg, wu, wd, *, tm):
    m, d = x.shape
    nf = wg.shape[0]
    assert wg.shape == wu.shape == (nf, d, FFN_TILE) and wd.shape == (nf * FFN_TILE, d)
    assert m % tm == 0 and tm % FFN_ROW_CHUNK == 0
    row = lambda i, f: (i, 0)
    vec = pl.BlockSpec((1, d), lambda i, f: (0, 0))
    wcol = pl.BlockSpec((1, d, FFN_TILE), lambda i, f: (f, 0, 0))
    return pl.pallas_call(
        _ffn_kernel,
        grid=(m // tm, nf),
        in_specs=[pl.BlockSpec((tm, d), row), vec, vec, wcol, wcol,
                  pl.BlockSpec((FFN_TILE, d), lambda i, f: (f, 0)),
                  pl.BlockSpec(memory_space=pl.ANY)],
        out_specs=pl.BlockSpec((tm, d), row),
        out_shape=jax.ShapeDtypeStruct((m, d), F32),
        scratch_shapes=[pltpu.VMEM((tm, d), F32), pltpu.SemaphoreType.DMA((1,))],
        compiler_params=_params("arbitrary", "arbitrary"),
        name="ffn",
    )(h, gt, gfin, wg, wu, wd, x)


def _rope_tables(L):
    f32 = np.float32
    lane = np.arange(LANES)
    inv1 = f32(ROPE_BASE) ** (-np.arange(32, dtype=f32) / f32(32))
    ang1 = np.arange(L, dtype=f32)[:, None] * inv1[None, :]
    sgn1 = np.where((lane % 64) < 32, -1.0, 1.0).astype(f32)
    cos1 = np.tile(np.cos(ang1), (1, LANES // 32))
    sin1 = np.tile(np.sin(ang1), (1, LANES // 32)) * sgn1[None, :]
    inv2 = f32(ROPE_BASE) ** (-np.arange(16, dtype=f32) / f32(16))
    nrow = L // GRID_W
    ang_r = np.arange(nrow, dtype=f32)[:, None] * inv2[None, :]
    ang_c = np.arange(GRID_W, dtype=f32)[:, None] * inv2[None, :]
    sgna = np.where((lane % 32) < 16, -1.0, 1.0).astype(f32)

    def expand(fr, fc):
        by_row = np.broadcast_to(np.tile(fr, (1, 2))[:, None, :], (nrow, GRID_W, 32))
        by_col = np.broadcast_to(np.tile(fc, (1, 2))[None, :, :], (nrow, GRID_W, 32))
        head = np.concatenate([by_row, by_col], axis=-1).reshape(L, 64)
        return np.tile(head, (1, LANES // 64))

    cosa = expand(np.cos(ang_r), np.cos(ang_c))
    sina = expand(np.sin(ang_r), np.sin(ang_c)) * sgna[None, :]
    return tuple(np.ascontiguousarray(t, dtype=f32) for t in (cos1, sin1, cosa, sina))


def kernel(x, c, ctx, c_ctx, w_mod, b_mod, norm_mix, norm_ffn, w_in, ret_decay, attn_sink,
           w_out, w_gate, w_up, w_down, norm_final):
    B, L, D = x.shape
    assert B == 1 and w_mod.shape[0] == 1, "single batch element, depth-1 layer"
    x2 = x[0]
    xc2 = ctx[0]

    cv = jnp.zeros((8, D), F32).at[0].set(c[0]).at[1].set(c_ctx)
    mod = _mod(cv, w_mod[0], b_mod[0][None, :], 2 * D)
    sh_m, sc_m = mod[0:1, 0:D], mod[0:1, D:2 * D]
    sh_mc, sc_mc = mod[1:2, 0:D], mod[1:2, D:2 * D]

    g_mix = norm_mix[0][None, :]
    cproj, ckd, cvd, w_in_b = _ctx_proj(xc2, g_mix, sh_mc, sc_mc, w_in[0])
    proj, kd, vd, mod_rest = _in_proj(x2, g_mix, sh_m, sc_m, w_in_b, _rope_tables(L),
                                      c[0][:, None], w_mod[0], b_mod[0][None, :], 2 * D, tm=ROW_TILE)
    gt_m, sh_f, sc_f, gt_f = [mod_rest.reshape(1, 4 * D)[:, k * D:(k + 1) * D] for k in range(4)]

    dec = ret_decay[0].astype(F32)
    sb = _ret_bwd_states(dec, proj, cproj)
    y_ret = _ret_out(dec, proj, sb, cproj)
    y_att, w_gate_b, w_up_b, w_out_b = _attn(
        attn_sink[0].astype(F32), proj, kd, vd, ckd, cvd,
        [(w_gate[0], 1, FFN_TILE), (w_up[0], 1, FFN_TILE), (w_out[0], 1, None)])

    x1, hff = _out_proj(y_ret, y_att, w_out_b, x2, gt_m, norm_ffn[0][None, :], sh_f, sc_f, tm=ROW_TILE)
    out = _ffn(hff, x1, gt_f, norm_final[None, :], w_gate_b, w_up_b, w_down[0], tm=FFN_ROW_TILE)
    return out[None]
```

```python
import jax
import jax.numpy as jnp
import numpy as np
from jax import lax
from jax.experimental import pallas as pl
from jax.experimental.pallas import tpu as pltpu

GRID_W = 64
RET_HEADS = 8
RET_DK = 64
RET_DV = 128
RET_CHUNK = 128
ATT_HEADS = 16
ATT_KV_HEADS = 4
ATT_DH = 64
ATT_GROUP = ATT_HEADS // ATT_KV_HEADS
WINDOW = 128
ATT_BLOCK = 128
ROPE_BASE = 10000.0
NORM_EPS = 1e-6
K_SCALE = RET_DK ** -0.5
ATT_SCALE = ATT_DH ** -0.5
LOG2E = 1.4426950408889634

RET_QK_COLS = RET_HEADS * RET_DK
RET_V_COLS = RET_HEADS * RET_DV
ATT_Q_COLS = ATT_HEADS * ATT_DH
KV_DUP_COLS = 2 * ATT_KV_HEADS * ATT_DH

LANES = 128
RET_PAIRS = RET_HEADS // 2
MASK_NEG = -1e30
VMEM_LIMIT = 56 * 1024 * 1024
CAST_PIECE_ROWS = 16
RING_SLOTS = 3
RET_STEP_CHUNKS = 16
ATT_STEP_BLOCKS = 4
ROW_TILE = 512
OUT_ROW_CHUNK = ROW_TILE
IN_ROW_CHUNK = ROW_TILE
FFN_TILE = 512
FFN_ROW_TILE = 1024
FFN_ROW_CHUNK = FFN_ROW_TILE

BF16 = jnp.bfloat16
F32 = jnp.float32


def _params(*sem):
    return pltpu.CompilerParams(dimension_semantics=sem, vmem_limit_bytes=VMEM_LIMIT)


def _ring_fetch(hbm, buf, sem, step, nsteps, rows):
    def copy(s):
        if isinstance(s, int):
            slot, start = s % RING_SLOTS, s * rows
        else:
            slot, start = lax.rem(s, RING_SLOTS), pl.multiple_of(s * rows, rows)
        return pltpu.make_async_copy(hbm.at[pl.ds(start, rows), :], buf.at[slot], sem.at[slot])

    @pl.when(step == 0)
    def _():
        for s in range(min(RING_SLOTS - 1, nsteps)):
            copy(s).start()

    @pl.when(step + (RING_SLOTS - 1) < nsteps)
    def _():
        copy(step + (RING_SLOTS - 1)).start()

    copy(step).wait()
    return buf.at[lax.rem(step, RING_SLOTS)]


def _with_cast_riders(body, n_in, n_out, n_rid, nsteps):
    def wrapped(*refs):
        ins = refs[:n_in]
        rid_hbm = refs[n_in:n_in + n_rid]
        outs = refs[n_in + n_rid:n_in + n_rid + n_out]
        rid_out = refs[n_in + n_rid + n_out:n_in + 2 * n_rid + n_out]
        scratch = refs[n_in + 2 * n_rid + n_out:]
        scratch, rings, sems = scratch[:-2 * n_rid], scratch[-2 * n_rid:-n_rid], scratch[-n_rid:]
        step = pl.program_id(0)
        rid_in = [_ring_fetch(h, b, m, step, nsteps, b.shape[1]) for h, b, m in zip(rid_hbm, rings, sems)]

        def piece(src, dst, r0):
            rs = slice(r0, r0 + CAST_PIECE_ROWS)
            if len(dst.shape) == 2:
                dst[rs, :] = src[rs, :].astype(BF16)
            else:
                tc = dst.shape[2]
                for t in range(dst.shape[0]):
                    dst[t, rs, :] = src[rs, t * tc:(t + 1) * tc].astype(BF16)

        pieces = [(lambda s=src, d=dst, r=r0: piece(s, d, r))
                  for src, dst in zip(rid_in, rid_out) for r0 in range(0, src.shape[0], CAST_PIECE_ROWS)]
        done = []
        fillers = [(lambda p=p: (done.append(1), p())) for p in pieces]
        body(*ins, *outs, *scratch, fillers=fillers)
        assert len(done) == len(pieces), "every cast piece must be emitted exactly once"
    return wrapped


def _rider_specs(riders, steps):
    in_specs, out_specs, shapes, rings, sems = [], [], [], [], []
    for w, tile in riders:
        rows, cols = w.shape
        assert rows % steps == 0 and cols % LANES == 0
        br = rows // steps
        assert br % CAST_PIECE_ROWS == 0, "slab must be bf16-tile aligned"
        in_specs.append(pl.BlockSpec(memory_space=pl.ANY))
        rings.append(pltpu.VMEM((RING_SLOTS, br, cols), F32))
        sems.append(pltpu.SemaphoreType.DMA((RING_SLOTS,)))
        if tile is None:
            out_specs.append(pl.BlockSpec((br, cols), lambda i: (i, 0)))
            shapes.append(jax.ShapeDtypeStruct(w.shape, BF16))
        else:
            assert cols % tile == 0 and tile % LANES == 0
            out_specs.append(pl.BlockSpec((cols // tile, br, tile), lambda i: (0, i, 0)))
            shapes.append(jax.ShapeDtypeStruct((cols // tile, rows, tile), BF16))
    return in_specs, out_specs, shapes, rings + sems


def _mod_kernel(cv_ref, w_ref, b_ref, o_ref):
    cv = cv_ref[...]
    s = cv / (1.0 + jnp.exp(-cv))
    o_ref[...] = jnp.dot(s.astype(BF16), w_ref[...].astype(BF16),
                         preferred_element_type=F32) + b_ref[...]


def _mod(cv, w, b, n):
    d = w.shape[0]
    tn = 512
    assert n % tn == 0
    return pl.pallas_call(
        _mod_kernel,
        grid=(n // tn,),
        in_specs=[pl.BlockSpec((8, d), lambda j: (0, 0)),
                  pl.BlockSpec((d, tn), lambda j: (0, j)),
                  pl.BlockSpec((1, tn), lambda j: (0, j))],
        out_specs=pl.BlockSpec((8, tn), lambda j: (0, j)),
        out_shape=jax.ShapeDtypeStruct((8, n), F32),
        compiler_params=_params("parallel"),
        name="mod",
    )(cv, w, b)


def _rot_pairs(a, cos, sin_signed, half):
    lane = lax.broadcasted_iota(jnp.int32, a.shape, 1)
    first = (lane % (2 * half)) < half
    rot = jnp.where(first, pltpu.roll(a, LANES - half, 1), pltpu.roll(a, half, 1))
    return a * cos + rot * sin_signed


def _dup_halves(a):
    lane = lax.broadcasted_iota(jnp.int32, a.shape, 1)
    r = pltpu.roll(a, 64, 1)
    lo = lane < 64
    return jnp.where(lo, a, r), jnp.where(lo, r, a)


_PROJ_TILE = 512
_PROJ_TILE_KINDS = ("ret_q", "ret_k", "plain", "plain", "plain", "plain", "att_q", "att_q", "att_kv")


def _in_proj_kernel(x_ref, g_ref, sh_ref, sc_ref, w_ref, c1_ref, s1_ref, ca_ref, sa_ref,
                    cc_ref, wm_ref, bm_ref, o_ref, kd_ref, vd_ref, mod_ref):
    def mod_rider():
        cc = cc_ref[...]
        s_col = cc / (1.0 + jnp.exp(-cc))
        mod_ref[0] = jnp.sum(wm_ref[...] * s_col, axis=0, keepdims=True) + bm_ref[...]

    tn = _PROJ_TILE
    for r in range(x_ref.shape[0] // IN_ROW_CHUNK):
        rs = slice(r * IN_ROW_CHUNK, (r + 1) * IN_ROW_CHUNK)
        xf = x_ref[rs, :]
        y = xf * lax.rsqrt(jnp.mean(xf * xf, axis=-1, keepdims=True) + NORM_EPS)
        y = y * g_ref[...]
        h = (y * (1.0 + sc_ref[...]) + sh_ref[...]).astype(BF16)

        def rope1(a):
            return _rot_pairs(a, c1_ref[rs, :], s1_ref[rs, :], 32)

        def ropea(a):
            return _rot_pairs(a, ca_ref[rs, :], sa_ref[rs, :], 16)

        order = sorted(range(len(_PROJ_TILE_KINDS)), key=lambda t: _PROJ_TILE_KINDS[t] == "plain")
        for j in order:
            kind = _PROJ_TILE_KINDS[j]
            acc = jnp.dot(h, w_ref[:, j * tn:(j + 1) * tn], preferred_element_type=F32)
            if r == 0 and j == order[-2]:
                mod_rider()
            for c in range(tn // LANES):
                a = acc[:, c * LANES:(c + 1) * LANES]
                if kind == "ret_q":
                    a = rope1(a)
                elif kind == "ret_k":
                    a = rope1(a) * K_SCALE
                elif kind == "att_q":
                    a = ropea(a) * (ATT_SCALE * LOG2E)
                elif kind == "att_kv" and c < 2:
                    a = ropea(a)
                o_ref[rs, j * tn + c * LANES:j * tn + (c + 1) * LANES] = a.astype(BF16)
                if kind == "att_kv":
                    dup_ref = kd_ref if c < 2 else vd_ref
                    d0, d1 = _dup_halves(a)
                    t = 2 * (c % 2)
                    dup_ref[rs, t * LANES:(t + 1) * LANES] = d0.astype(BF16)
                    dup_ref[rs, (t + 1) * LANES:(t + 2) * LANES] = d1.astype(BF16)


def _in_proj(x, g, sh, sc, w, tabs, c_col, w_mod, b_mod, mod_done, *, tm):
    m, d = x.shape
    n = w.shape[1]
    assert n == _PROJ_TILE * len(_PROJ_TILE_KINDS) and m % tm == 0 and tm % IN_ROW_CHUNK == 0
    steps = m // tm
    slab = (w_mod.shape[1] - mod_done) // steps
    assert slab * steps == w_mod.shape[1] - mod_done and slab % LANES == 0 and mod_done % slab == 0
    slab0 = mod_done // slab
    c1, s1, ca, sa = tabs
    row = lambda i: (i, 0)
    vec = pl.BlockSpec((1, d), lambda i: (0, 0))
    tab = pl.BlockSpec((tm, LANES), row)
    return pl.pallas_call(
        _in_proj_kernel,
        grid=(m // tm,),
        in_specs=[pl.BlockSpec((tm, d), row), vec, vec, vec,
                  pl.BlockSpec((d, n), lambda i: (0, 0), pipeline_mode=pl.Buffered(1)),
                  tab, tab, tab, tab,
                  pl.BlockSpec((d, 1), lambda i: (0, 0)),
                  pl.BlockSpec((d, slab), lambda i: (0, slab0 + i)),
                  pl.BlockSpec((1, slab), lambda i: (0, slab0 + i))],
        out_specs=[pl.BlockSpec((tm, n), row),
                   pl.BlockSpec((tm, KV_DUP_COLS), row),
                   pl.BlockSpec((tm, KV_DUP_COLS), row),
                   pl.BlockSpec((1, 1, slab), lambda i: (i, 0, 0))],
        out_shape=[jax.ShapeDtypeStruct((m, n), BF16),
                   jax.ShapeDtypeStruct((m, KV_DUP_COLS), BF16),
                   jax.ShapeDtypeStruct((m, KV_DUP_COLS), BF16),
                   jax.ShapeDtypeStruct((steps, 1, slab), F32)],
        compiler_params=_params("parallel"),
        name="in_proj",
    )(x, g, sh, sc, w, c1, s1, ca, sa, c_col, w_mod, b_mod)


def _ctx_proj_kernel(x_ref, g_ref, sh_ref, sc_ref, w_ref, o_ref, kd_ref, vd_ref, wb_ref, h_ref):
    j = pl.program_id(0)

    @pl.when(j == 0)
    def _():
        xf = x_ref[...]
        y = xf * lax.rsqrt(jnp.mean(xf * xf, axis=-1, keepdims=True) + NORM_EPS)
        y = y * g_ref[...]
        h_ref[...] = (y * (1.0 + sc_ref[...]) + sh_ref[...]).astype(BF16)

    wb = w_ref[...].astype(BF16)
    wb_ref[...] = wb
    acc = jnp.dot(h_ref[...], wb, preferred_element_type=F32)
    is_ret_k = _PROJ_TILE_KINDS.index("ret_k")
    o_ref[...] = (acc * jnp.where(j == is_ret_k, K_SCALE, 1.0)).astype(BF16)

    @pl.when(j == _PROJ_TILE_KINDS.index("att_kv"))
    def _():
        for c in range(_PROJ_TILE // LANES):
            dup_ref = kd_ref if c < 2 else vd_ref
            d0, d1 = _dup_halves(acc[:, c * LANES:(c + 1) * LANES])
            t = 2 * (c % 2)
            dup_ref[:, t * LANES:(t + 1) * LANES] = d0.astype(BF16)
            dup_ref[:, (t + 1) * LANES:(t + 2) * LANES] = d1.astype(BF16)


def _ctx_proj(x, g, sh, sc, w):
    m, d = x.shape
    n = w.shape[1]
    tn = _PROJ_TILE
    assert n == tn * len(_PROJ_TILE_KINDS)
    fixed = lambda j: (0, 0)
    vec = pl.BlockSpec((1, d), fixed)
    return pl.pallas_call(
        _ctx_proj_kernel,
        grid=(n // tn,),
        in_specs=[pl.BlockSpec((m, d), fixed), vec, vec, vec,
                  pl.BlockSpec((d, tn), lambda j: (0, j))],
        out_specs=[pl.BlockSpec((m, tn), lambda j: (0, j)),
                   pl.BlockSpec((m, KV_DUP_COLS), fixed),
                   pl.BlockSpec((m, KV_DUP_COLS), fixed),
                   pl.BlockSpec((d, tn), lambda j: (0, j))],
        out_shape=[jax.ShapeDtypeStruct((m, n), BF16),
                   jax.ShapeDtypeStruct((m, KV_DUP_COLS), BF16),
                   jax.ShapeDtypeStruct((m, KV_DUP_COLS), BF16),
                   jax.ShapeDtypeStruct((d, n), BF16)],
        scratch_shapes=[pltpu.VMEM((m, d), BF16)],
        compiler_params=_params("arbitrary"),
        name="ctx_proj",
    )(x, g, sh, sc, w)


def _pair_lg(dec_ref, d, p, shape):
    lane = lax.broadcasted_iota(jnp.int32, shape, 1)
    first = (lane % LANES) < 64
    raw = jnp.where(first, jnp.full(shape, dec_ref[d, 2 * p], F32), jnp.full(shape, dec_ref[d, 2 * p + 1], F32))
    return -jnp.exp(raw)


def _head_block_mask(shape):
    r = lax.broadcasted_iota(jnp.int32, shape, 0)
    c = lax.broadcasted_iota(jnp.int32, shape, 1)
    return (r // 64) == (c // LANES)


def _kv_pair(k_pair, v_pair, w):
    kw = (k_pair.astype(F32) * w).astype(BF16)
    kv = lax.dot_general(kw, v_pair, (((0,), (0,)), ((), ())), preferred_element_type=F32)
    return jnp.where(_head_block_mask(kv.shape), kv, 0.0)


def _row_decay(dec_ref, d, p):
    shape = (LANES, 2 * RET_DV)
    rowh = lax.broadcasted_iota(jnp.int32, shape, 0) < 64
    raw = jnp.where(rowh, jnp.full(shape, dec_ref[d, 2 * p], F32), jnp.full(shape, dec_ref[d, 2 * p + 1], F32))
    return jnp.exp(-jnp.exp(raw) * float(RET_CHUNK))


def _compact_state(s):
    row = lax.broadcasted_iota(jnp.int32, (LANES, RET_DV), 0)
    return jnp.where(row < 64, s[:, :RET_DV], s[:, RET_DV:])


def _expand_state(c):
    row = lax.broadcasted_iota(jnp.int32, c.shape, 0)
    z = jnp.zeros_like(c)
    return jnp.concatenate([jnp.where(row < 64, c, z), jnp.where(row < 64, z, c)], axis=1)


def _ret_bwd_kernel(dec_ref, k_ref, v_ref, ck_ref, cv_ref, sb_ref, sbs):
    i = pl.program_id(0)
    C = RET_CHUNK
    lc = ck_ref.shape[0]

    @pl.when(i == 0)
    def _():
        pos = lax.broadcasted_iota(jnp.int32, (lc, LANES), 0).astype(F32)
        for p in range(RET_PAIRS):
            ks = slice(p * LANES, (p + 1) * LANES)
            vs = slice(p * 2 * RET_DV, (p + 1) * 2 * RET_DV)
            wb = jnp.exp(_pair_lg(dec_ref, 1, p, (lc, LANES)) * pos)
            sbs[p] = _kv_pair(ck_ref[:, ks], cv_ref[:, vs], wb)

    pos = lax.broadcasted_iota(jnp.int32, (C, LANES), 0).astype(F32)
    for p in range(RET_PAIRS):
        ks = slice(p * LANES, (p + 1) * LANES)
        vs = slice(p * 2 * RET_DV, (p + 1) * 2 * RET_DV)
        wb = jnp.exp(_pair_lg(dec_ref, 1, p, (C, LANES)) * pos)
        gb = _row_decay(dec_ref, 1, p)
        sb = sbs[p]
        for cc in reversed(range(RET_STEP_CHUNKS)):
            rs = slice(cc * C, (cc + 1) * C)
            sb_ref[cc, p] = _compact_state(sb).astype(BF16)
            sb = gb * sb + _kv_pair(k_ref[rs, ks], v_ref[rs, vs], wb)
        sbs[p] = sb


def _ret_bwd_states(dec, proj, cproj):
    L = proj.shape[0]
    lc = cproj.shape[0]
    S = RET_STEP_CHUNKS
    R = S * RET_CHUNK
    n = L // R
    return pl.pallas_call(
        _ret_bwd_kernel,
        grid=(n,),
        in_specs=[pl.BlockSpec(memory_space=pltpu.SMEM),
                  pl.BlockSpec((R, RET_QK_COLS), lambda i: (n - 1 - i, 1)),
                  pl.BlockSpec((R, RET_V_COLS), lambda i: (n - 1 - i, 1)),
                  pl.BlockSpec((lc, RET_QK_COLS), lambda i: (0, 1)),
                  pl.BlockSpec((lc, RET_V_COLS), lambda i: (0, 1))],
        out_specs=pl.BlockSpec((S, RET_PAIRS, LANES, RET_DV), lambda i: (n - 1 - i, 0, 0, 0)),
        out_shape=jax.ShapeDtypeStruct((n * S, RET_PAIRS, LANES, RET_DV), BF16),
        scratch_shapes=[pltpu.VMEM((RET_PAIRS, LANES, 2 * RET_DV), F32)],
        compiler_params=_params("arbitrary"),
        name="ret_bwd",
    )(dec, proj, proj, cproj, cproj)


def _ret_out_kernel(dec_ref, q_ref, k_ref, v_ref, g_ref, sb_ref, ck_ref, cv_ref, o_ref, sfs):
    i = pl.program_id(0)
    C = RET_CHUNK
    lc = ck_ref.shape[0]

    @pl.when(i == 0)
    def _():
        cpos = lax.broadcasted_iota(jnp.int32, (lc, LANES), 0).astype(F32)
        for p in range(RET_PAIRS):
            ks = slice(p * LANES, (p + 1) * LANES)
            vs = slice(p * 2 * RET_DV, (p + 1) * 2 * RET_DV)
            wf = jnp.exp(_pair_lg(dec_ref, 0, p, (lc, LANES)) * (lc - 1.0 - cpos))
            sfs[p] = _kv_pair(ck_ref[:, ks], cv_ref[:, vs], wf)

    pos = lax.broadcasted_iota(jnp.int32, (C, LANES), 0).astype(F32)
    n_i = lax.broadcasted_iota(jnp.int32, (C, 2 * C), 0)
    m_i = lax.broadcasted_iota(jnp.int32, (C, 2 * C), 1) % C
    rel = (n_i - m_i).astype(F32)
    lane = lax.broadcasted_iota(jnp.int32, (C, LANES), 1)
    lo = lane < 64
    for p in range(RET_PAIRS):
        ks = slice(p * LANES, (p + 1) * LANES)
        vs = slice(p * 2 * RET_DV, (p + 1) * 2 * RET_DV)
        col_a = lax.broadcasted_iota(jnp.int32, (C, 2 * C), 1) < C
        raw_f = jnp.where(col_a, jnp.full((C, 2 * C), dec_ref[0, 2 * p], F32), jnp.full((C, 2 * C), dec_ref[0, 2 * p + 1], F32))
        raw_b = jnp.where(col_a, jnp.full((C, 2 * C), dec_ref[1, 2 * p], F32), jnp.full((C, 2 * C), dec_ref[1, 2 * p + 1], F32))
        dmat = jnp.where(rel >= 0, jnp.exp(-jnp.exp(raw_f) * jnp.maximum(rel, 0.0)),
                         jnp.exp(-jnp.exp(raw_b) * jnp.maximum(-rel, 0.0)))
        lg_f = _pair_lg(dec_ref, 0, p, (C, LANES))
        wqf = jnp.exp(lg_f * (pos + 1.0))
        wqb = jnp.exp(_pair_lg(dec_ref, 1, p, (C, LANES)) * (float(C) - pos))
        wkf = jnp.exp(lg_f * (C - 1.0 - pos))
        gf = _row_decay(dec_ref, 0, p)
        sf = sfs[p]
        for cc in range(RET_STEP_CHUNKS):
            rs = slice(cc * C, (cc + 1) * C)
            q = q_ref[rs, ks]
            k = k_ref[rs, ks]
            v = v_ref[rs, vs]
            zk = jnp.zeros_like(k)
            kst = jnp.concatenate([jnp.where(lo, k, zk), jnp.where(lo, zk, k)], axis=0)
            s = lax.dot_general(q, kst, (((1,), (1,)), ((), ())), preferred_element_type=F32)
            sd = (s * dmat).astype(BF16)
            qf32 = q.astype(F32)
            qwf = (qf32 * wqf).astype(BF16)
            qwb = (qf32 * wqb).astype(BF16)
            zv = jnp.zeros((C, RET_DV), BF16)
            vbd = jnp.concatenate([jnp.concatenate([v[:, :RET_DV], zv], axis=1),
                                   jnp.concatenate([zv, v[:, RET_DV:]], axis=1)], axis=0)
            lhs = jnp.concatenate([sd, qwf, qwb], axis=1)
            rhs = jnp.concatenate([vbd, sf.astype(BF16), _expand_state(sb_ref[cc, p])], axis=0)
            o = jnp.dot(lhs, rhs, preferred_element_type=F32)
            sf = gf * sf + _kv_pair(k, v, wkf)
            for t in range(2):
                oh = o[:, t * RET_DV:(t + 1) * RET_DV]
                oh = oh * lax.rsqrt(jnp.mean(oh * oh, axis=-1, keepdims=True) + NORM_EPS)
                cs = slice(p * 2 * RET_DV + t * RET_DV, p * 2 * RET_DV + (t + 1) * RET_DV)
                gt = g_ref[rs, cs].astype(F32)
                o_ref[rs, cs] = (oh * (gt / (1.0 + jnp.exp(-gt)))).astype(BF16)
        sfs[p] = sf


def _ret_out(dec, proj, sb, cproj):
    L = proj.shape[0]
    lc = cproj.shape[0]
    S = RET_STEP_CHUNKS
    R = S * RET_CHUNK
    n = L // R
    return pl.pallas_call(
        _ret_out_kernel,
        grid=(n,),
        in_specs=[pl.BlockSpec(memory_space=pltpu.SMEM),
                  pl.BlockSpec((R, RET_QK_COLS), lambda i: (i, 0)),
                  pl.BlockSpec((R, RET_QK_COLS), lambda i: (i, 1)),
                  pl.BlockSpec((R, RET_V_COLS), lambda i: (i, 1)),
                  pl.BlockSpec((R, RET_V_COLS), lambda i: (i, 2)),
                  pl.BlockSpec((S, RET_PAIRS, LANES, RET_DV), lambda i: (i, 0, 0, 0)),
                  pl.BlockSpec((lc, RET_QK_COLS), lambda i: (0, 1)),
                  pl.BlockSpec((lc, RET_V_COLS), lambda i: (0, 1))],
        out_specs=pl.BlockSpec((R, RET_HEADS * RET_DV), lambda i: (i, 0)),
        out_shape=jax.ShapeDtypeStruct((L, RET_HEADS * RET_DV), BF16),
        scratch_shapes=[pltpu.VMEM((RET_PAIRS, LANES, 2 * RET_DV), F32)],
        compiler_params=_params("arbitrary"),
        name="ret_out",
    )(dec, proj, proj, proj, proj, sb, cproj, cproj)


def _attn_kernel(sink_ref, q_ref, kp_ref, kc_ref, kn_ref, vp_ref, vc_ref, vn_ref, ck_ref, cv_ref, o_ref, *, fillers):
    n = pl.program_id(0)
    nstep = pl.num_programs(0)
    B = ATT_BLOCK
    SB = ATT_STEP_BLOCKS
    kj = lax.broadcasted_iota(jnp.int32, (B, B), 0)
    qi = lax.broadcasted_iota(jnp.int32, (B, B), 1)
    ok_prev = jnp.where(n > 0, 0.0, MASK_NEG).astype(F32)
    ok_next = jnp.where(n < nstep - 1, 0.0, MASK_NEG).astype(F32)

    def band(inside, ok):
        return jnp.concatenate([jnp.where(inside, ok, MASK_NEG).astype(F32)] * ATT_GROUP, axis=1)

    bias_prev = [band(kj >= qi, ok_prev if j == 0 else 0.0) for j in range(SB)]
    bias_next = [band(kj <= qi, ok_next if j == SB - 1 else 0.0) for j in range(SB)]
    lane = lax.broadcasted_iota(jnp.int32, (B, LANES), 1)
    lo = lane < 64
    hi = lane >= 64

    def keys_of(j, gs, prev_ref, cur_ref, next_ref, ctx_ref):
        prev = prev_ref[:, gs] if j == 0 else cur_ref[(j - 1) * B:j * B, gs]
        nxt = next_ref[:, gs] if j == SB - 1 else cur_ref[(j + 1) * B:(j + 2) * B, gs]
        return jnp.concatenate([prev, cur_ref[j * B:(j + 1) * B, gs], nxt, ctx_ref[:, gs]], axis=0)

    def scores(j, g):
        gs = slice(g * LANES, (g + 1) * LANES)
        kcat = keys_of(j, gs, kp_ref, kc_ref, kn_ref, ck_ref)
        qs = []
        for r in range(ATT_GROUP):
            h = ATT_GROUP * g + r
            qt = q_ref[j * B:(j + 1) * B, (h // 2) * LANES:(h // 2 + 1) * LANES]
            keep = lo if h % 2 == 0 else hi
            qs.append(jnp.where(keep, qt, jnp.zeros_like(qt)))
        q4 = jnp.concatenate(qs, axis=0)
        return lax.dot_general(kcat, q4, (((1,), (1,)), ((), ())), preferred_element_type=F32)

    def softmax(j, g, s):
        sk = jnp.concatenate([jnp.full((1, B), sink_ref[ATT_GROUP * g + r], F32)
                              for r in range(ATT_GROUP)], axis=1) * LOG2E
        s = jnp.concatenate([s[:B] + bias_prev[j], s[B:2 * B], s[2 * B:3 * B] + bias_next[j], s[3 * B:]], axis=0)
        m = jnp.maximum(jnp.max(s, axis=0, keepdims=True), sk)
        e = jnp.exp2(s - m)
        den = jnp.sum(e, axis=0, keepdims=True) + jnp.exp2(sk - m)
        return e.astype(BF16), den

    def values(j, g, e, den):
        gs = slice(g * LANES, (g + 1) * LANES)
        vcat = keys_of(j, gs, vp_ref, vc_ref, vn_ref, cv_ref)
        res = lax.dot_general(vcat, e, (((0,), (0,)), ((), ())), preferred_element_type=F32) * (1.0 / den)
        for t in range(2):
            even = res[:, (2 * t) * B:(2 * t + 1) * B].T
            odd = res[:, (2 * t + 1) * B:(2 * t + 2) * B].T
            c0 = (2 * g + t) * LANES
            o_ref[j * B:(j + 1) * B, c0:c0 + LANES] = jnp.where(lo, even, odd).astype(BF16)

    units = [(j, g) for j in range(SB) for g in range(ATT_KV_HEADS)]
    per_unit = -(-len(fillers) // len(units))
    s_next = scores(*units[0])
    pending = None
    for u, unit in enumerate(units):
        s_cur = s_next
        if u + 1 < len(units):
            s_next = scores(*units[u + 1])
        e_den = softmax(*unit, s_cur)
        for fill in fillers[u * per_unit:(u + 1) * per_unit]:
            fill()
        if pending is not None:
            values(*units[u - 1], *pending)
        pending = e_den
    values(*units[-1], *pending)


def _attn(sink, proj, kd, vd, ckd, cvd, riders):
    L = proj.shape[0]
    B = ATT_BLOCK
    SB = ATT_STEP_BLOCKS
    n = L // (SB * B)
    nb = L // B
    lc = ckd.shape[0]
    prev = pl.BlockSpec((B, KV_DUP_COLS), lambda i: (jnp.maximum(i * SB - 1, 0), 0))
    cur = pl.BlockSpec((SB * B, KV_DUP_COLS), lambda i: (i, 0))
    nxt = pl.BlockSpec((B, KV_DUP_COLS), lambda i: (jnp.minimum((i + 1) * SB, nb - 1), 0))
    full = pl.BlockSpec((lc, KV_DUP_COLS), lambda i: (0, 0))
    rid_in_specs, rid_out_specs, rid_shapes, rid_scratch = _rider_specs(riders, n)
    return pl.pallas_call(
        _with_cast_riders(_attn_kernel, 10, 1, len(riders), n),
        grid=(n,),
        in_specs=[pl.BlockSpec(memory_space=pltpu.SMEM),
                  pl.BlockSpec((SB * B, ATT_Q_COLS), lambda i: (i, 3)),
                  prev, cur, nxt, prev, cur, nxt, full, full] + rid_in_specs,
        out_specs=[pl.BlockSpec((SB * B, ATT_HEADS * ATT_DH), lambda i: (i, 0))] + rid_out_specs,
        out_shape=[jax.ShapeDtypeStruct((L, ATT_HEADS * ATT_DH), BF16)] + rid_shapes,
        scratch_shapes=rid_scratch,
        compiler_params=_params("arbitrary"),
        name="attn",
    )(sink, proj, kd, kd, kd, vd, vd, vd, ckd, cvd, *[r[0] for r in riders])


def _out_proj_kernel(yr_ref, ya_ref, w_ref, x_hbm, gt_ref, g_ref, sh_ref, sc_ref, o_ref, h_ref, x_ring, x_sem):
    rows = o_ref.shape[0]
    x_ref = _ring_fetch(x_hbm, x_ring, x_sem, pl.program_id(0), x_hbm.shape[0] // rows, rows)
    kr = yr_ref.shape[1]
    for r in range(yr_ref.shape[0] // OUT_ROW_CHUNK):
        rs = slice(r * OUT_ROW_CHUNK, (r + 1) * OUT_ROW_CHUNK)
        acc = jnp.dot(yr_ref[rs, :], w_ref[:kr, :], preferred_element_type=F32)
        acc = acc + jnp.dot(ya_ref[rs, :], w_ref[kr:, :], preferred_element_type=F32)
        x1 = x_ref[rs, :] + gt_ref[...] * acc
        o_ref[rs, :] = x1
        y = x1 * lax.rsqrt(jnp.mean(x1 * x1, axis=-1, keepdims=True) + NORM_EPS)
        y = y * g_ref[...]
        h_ref[rs, :] = (y * (1.0 + sc_ref[...]) + sh_ref[...]).astype(BF16)


def _out_proj(yr, ya, w, x, gt, g, sh, sc, *, tm):
    m, d = x.shape
    kr, ka = yr.shape[1], ya.shape[1]
    row = lambda i: (i, 0)
    vec = pl.BlockSpec((1, d), lambda i: (0, 0))
    return pl.pallas_call(
        _out_proj_kernel,
        grid=(m // tm,),
        in_specs=[pl.BlockSpec((tm, kr), row), pl.BlockSpec((tm, ka), row),
                  pl.BlockSpec((kr + ka, d), lambda i: (0, 0), pipeline_mode=pl.Buffered(1)),
                  pl.BlockSpec(memory_space=pl.ANY), vec, vec, vec, vec],
        out_specs=[pl.BlockSpec((tm, d), row), pl.BlockSpec((tm, d), row)],
        out_shape=[jax.ShapeDtypeStruct((m, d), F32), jax.ShapeDtypeStruct((m, d), BF16)],
        scratch_shapes=[pltpu.VMEM((RING_SLOTS, tm, d), F32), pltpu.SemaphoreType.DMA((RING_SLOTS,))],
        compiler_params=_params("arbitrary"),
        name="out_proj",
    )(yr, ya, w, x, gt, g, sh, sc)


def _ffn_kernel(h_ref, gt_ref, gfin_ref, wg_ref, wu_ref, wd_ref, x_hbm, o_ref, x_buf, sem):
    i = pl.program_id(0)
    f = pl.program_id(1)
    last = pl.num_programs(1) - 1
    rows = o_ref.shape[0]
    x_copy = pltpu.make_async_copy(x_hbm.at[pl.ds(pl.multiple_of(i * rows, rows), rows), :], x_buf, sem.at[0])

    def step(first, final):
        wd = wd_ref[...].astype(BF16)
        for r in range(rows // FFN_ROW_CHUNK):
            rs = slice(r * FFN_ROW_CHUNK, (r + 1) * FFN_ROW_CHUNK)
            h = h_ref[rs, :]
            a = jnp.dot(h, wg_ref[0], preferred_element_type=F32)
            u = jnp.dot(h, wu_ref[0], preferred_element_type=F32)
            act = ((a / (1.0 + jnp.exp(-a))) * u).astype(BF16)
            part = jnp.dot(act, wd, preferred_element_type=F32)
            if first:
                o_ref[rs, :] = part
            elif not final:
                o_ref[rs, :] += part
            else:
                y = x_buf[rs, :] + gt_ref[...] * (o_ref[rs, :] + part)
                y = y * lax.rsqrt(jnp.mean(y * y, axis=-1, keepdims=True) + NORM_EPS)
                o_ref[rs, :] = y * gfin_ref[...]

    @pl.when(f == 0)
    def _():
        x_copy.start()
        step(first=True, final=False)

    @pl.when((f > 0) & (f < last))
    def _():
        step(first=False, final=False)

    @pl.when(f == last)
    def _():
        x_copy.wait()
        step(first=False, final=True)


def _ffn(h, x, gt, gfin, wg, wu, wd, *, tm):
    m, d = x.shape
    nf = wg.shape[0]
    assert wg.shape == wu.shape == (nf, d, FFN_TILE) and wd.shape == (nf * FFN_TILE, d)
    assert m % tm == 0 and tm % FFN_ROW_CHUNK == 0
    row = lambda i, f: (i, 0)
    vec = pl.BlockSpec((1, d), lambda i, f: (0, 0))
    wcol = pl.BlockSpec((1, d, FFN_TILE), lambda i, f: (f, 0, 0))
    return pl.pallas_call(
        _ffn_kernel,
        grid=(m // tm, nf),
        in_specs=[pl.BlockSpec((tm, d), row), vec, vec, wcol, wcol,
                  pl.BlockSpec((FFN_TILE, d), lambda i, f: (f, 0)),
                  pl.BlockSpec(memory_space=pl.ANY)],
        out_specs=pl.BlockSpec((tm, d), row),
        out_shape=jax.ShapeDtypeStruct((m, d), F32),
        scratch_shapes=[pltpu.VMEM((tm, d), F32), pltpu.SemaphoreType.DMA((1,))],
        compiler_params=_params("arbitrary", "arbitrary"),
        name="ffn",
    )(h, gt, gfin, wg, wu, wd, x)


def _rope_tables(L):
    f32 = np.float32
    lane = np.arange(LANES)
    inv1 = f32(ROPE_BASE) ** (-np.arange(32, dtype=f32) / f32(32))
    ang1 = np.arange(L, dtype=f32)[:, None] * inv1[None, :]
    sgn1 = np.where((lane % 64) < 32, -1.0, 1.0).astype(f32)
    cos1 = np.tile(np.cos(ang1), (1, LANES // 32))
    sin1 = np.tile(np.sin(ang1), (1, LANES // 32)) * sgn1[None, :]
    inv2 = f32(ROPE_BASE) ** (-np.arange(16, dtype=f32) / f32(16))
    nrow = L // GRID_W
    ang_r = np.arange(nrow, dtype=f32)[:, None] * inv2[None, :]
    ang_c = np.arange(GRID_W, dtype=f32)[:, None] * inv2[None, :]
    sgna = np.where((lane % 32) < 16, -1.0, 1.0).astype(f32)

    def expand(fr, fc):
        by_row = np.broadcast_to(np.tile(fr, (1, 2))[:, None, :], (nrow, GRID_W, 32))
        by_col = np.broadcast_to(np.tile(fc, (1, 2))[None, :, :], (nrow, GRID_W, 32))
        head = np.concatenate([by_row, by_col], axis=-1).reshape(L, 64)
        return np.tile(head, (1, LANES // 64))

    cosa = expand(np.cos(ang_r), np.cos(ang_c))
    sina = expand(np.sin(ang_r), np.sin(ang_c)) * sgna[None, :]
    return tuple(np.ascontiguousarray(t, dtype=f32) for t in (cos1, sin1, cosa, sina))


def kernel(x, c, ctx, c_ctx, w_mod, b_mod, norm_mix, norm_ffn, w_in, ret_decay, attn_sink,
           w_out, w_gate, w_up, w_down, norm_final):
    B, L, D = x.shape
    assert B == 1 and w_mod.shape[0] == 1, "single batch element, depth-1 layer"
    x2 = x[0]
    xc2 = ctx[0]

    cv = jnp.zeros((8, D), F32).at[0].set(c[0]).at[1].set(c_ctx)
    mod = _mod(cv, w_mod[0], b_mod[0][None, :], 2 * D)
    sh_m, sc_m = mod[0:1, 0:D], mod[0:1, D:2 * D]
    sh_mc, sc_mc = mod[1:2, 0:D], mod[1:2, D:2 * D]

    g_mix = norm_mix[0][None, :]
    cproj, ckd, cvd, w_in_b = _ctx_proj(xc2, g_mix, sh_mc, sc_mc, w_in[0])
    proj, kd, vd, mod_rest = _in_proj(x2, g_mix, sh_m, sc_m, w_in_b, _rope_tables(L),
                                      c[0][:, None], w_mod[0], b_mod[0][None, :], 2 * D, tm=ROW_TILE)
    gt_m, sh_f, sc_f, gt_f = [mod_rest.reshape(1, 4 * D)[:, k * D:(k + 1) * D] for k in range(4)]

    dec = ret_decay[0].astype(F32)
    sb = _ret_bwd_states(dec, proj, cproj)
    y_ret = _ret_out(dec, proj, sb, cproj)
    y_att, w_gate_b, w_up_b, w_out_b = _attn(
        attn_sink[0].astype(F32), proj, kd, vd, ckd, cvd,
        [(w_gate[0], FFN_TILE), (w_up[0], FFN_TILE), (w_out[0], None)])

    x1, hff = _out_proj(y_ret, y_att, w_out_b, x2, gt_m, norm_ffn[0][None, :], sh_f, sc_f, tm=ROW_TILE)
    out = _ffn(hff, x1, gt_f, norm_final[None, :], w_gate_b, w_up_b, w_down[0], tm=FFN_ROW_TILE)
    return out[None]
```

```python
import jax
import jax.numpy as jnp
import numpy as np
from jax import lax
from jax.experimental import pallas as pl
from jax.experimental.pallas import tpu as pltpu

GRID_W = 64
RET_HEADS = 8
RET_DK = 64
RET_DV = 128
RET_CHUNK = 128
ATT_HEADS = 16
ATT_KV_HEADS = 4
ATT_DH = 64
ATT_GROUP = ATT_HEADS // ATT_KV_HEADS
WINDOW = 128
ATT_BLOCK = 128
ROPE_BASE = 10000.0
NORM_EPS = 1e-6
K_SCALE = RET_DK ** -0.5
ATT_SCALE = ATT_DH ** -0.5
LOG2E = 1.4426950408889634

RET_QK_COLS = RET_HEADS * RET_DK
RET_V_COLS = RET_HEADS * RET_DV
ATT_Q_COLS = ATT_HEADS * ATT_DH
KV_DUP_COLS = 2 * ATT_KV_HEADS * ATT_DH

LANES = 128
RET_PAIRS = RET_HEADS // 2
MASK_NEG = -1e30
VMEM_LIMIT = 56 * 1024 * 1024
CAST_PIECE_ROWS = 16
RET_STEP_CHUNKS = 16
ATT_STEP_BLOCKS = 4
ROW_TILE = 512
OUT_ROW_CHUNK = ROW_TILE
IN_ROW_CHUNK = ROW_TILE
FFN_TILE = 512
FFN_ROW_TILE = 1024
FFN_ROW_CHUNK = FFN_ROW_TILE

BF16 = jnp.bfloat16
F32 = jnp.float32


def _params(*sem):
    return pltpu.CompilerParams(dimension_semantics=sem, vmem_limit_bytes=VMEM_LIMIT)


def _with_cast_riders(body, n_in, n_out, n_rid):
    def wrapped(*refs):
        ins = refs[:n_in]
        rid_in = refs[n_in:n_in + n_rid]
        outs = refs[n_in + n_rid:n_in + n_rid + n_out]
        rid_out = refs[n_in + n_rid + n_out:n_in + 2 * n_rid + n_out]
        scratch = refs[n_in + 2 * n_rid + n_out:]

        def piece(src, dst, r0):
            rs = slice(r0, r0 + CAST_PIECE_ROWS)
            if len(dst.shape) == 2:
                dst[rs, :] = src[rs, :].astype(BF16)
            else:
                tc = dst.shape[2]
                for t in range(dst.shape[0]):
                    dst[t, rs, :] = src[rs, t * tc:(t + 1) * tc].astype(BF16)

        pieces = [(lambda s=src, d=dst, r=r0: piece(s, d, r))
                  for src, dst in zip(rid_in, rid_out) for r0 in range(0, src.shape[0], CAST_PIECE_ROWS)]
        done = []
        fillers = [(lambda p=p: (done.append(1), p())) for p in pieces]
        body(*ins, *outs, *scratch, fillers=fillers)
        assert len(done) == len(pieces), "every cast piece must be emitted exactly once"
    return wrapped


def _rider_specs(riders, steps):
    in_specs, out_specs, shapes = [], [], []
    for w, ncb, tile in riders:
        rows, cols = w.shape
        nrb = steps // ncb
        assert nrb * ncb == steps and rows % nrb == 0 and cols % ncb == 0
        br, bc = rows // nrb, cols // ncb
        assert br % CAST_PIECE_ROWS == 0 and bc % LANES == 0, "slab must be bf16-tile aligned"
        in_specs.append(pl.BlockSpec((br, bc), lambda i, ncb=ncb: (i // ncb, i % ncb)))
        if tile is None:
            out_specs.append(in_specs[-1])
            shapes.append(jax.ShapeDtypeStruct(w.shape, BF16))
        else:
            assert ncb == 1 and cols % tile == 0 and tile % LANES == 0
            out_specs.append(pl.BlockSpec((cols // tile, br, tile), lambda i: (0, i, 0)))
            shapes.append(jax.ShapeDtypeStruct((cols // tile, rows, tile), BF16))
    return in_specs, out_specs, shapes


def _mod_kernel(cv_ref, w_ref, b_ref, o_ref):
    cv = cv_ref[...]
    s = cv / (1.0 + jnp.exp(-cv))
    o_ref[...] = jnp.dot(s.astype(BF16), w_ref[...].astype(BF16),
                         preferred_element_type=F32) + b_ref[...]


def _mod(cv, w, b, n):
    d = w.shape[0]
    tn = 512
    assert n % tn == 0
    return pl.pallas_call(
        _mod_kernel,
        grid=(n // tn,),
        in_specs=[pl.BlockSpec((8, d), lambda j: (0, 0)),
                  pl.BlockSpec((d, tn), lambda j: (0, j)),
                  pl.BlockSpec((1, tn), lambda j: (0, j))],
        out_specs=pl.BlockSpec((8, tn), lambda j: (0, j)),
        out_shape=jax.ShapeDtypeStruct((8, n), F32),
        compiler_params=_params("parallel"),
        name="mod",
    )(cv, w, b)


def _rot_pairs(a, cos, sin_signed, half):
    lane = lax.broadcasted_iota(jnp.int32, a.shape, 1)
    first = (lane % (2 * half)) < half
    rot = jnp.where(first, pltpu.roll(a, LANES - half, 1), pltpu.roll(a, half, 1))
    return a * cos + rot * sin_signed


def _dup_halves(a):
    lane = lax.broadcasted_iota(jnp.int32, a.shape, 1)
    r = pltpu.roll(a, 64, 1)
    lo = lane < 64
    return jnp.where(lo, a, r), jnp.where(lo, r, a)


_PROJ_TILE = 512
_PROJ_TILE_KINDS = ("ret_q", "ret_k", "plain", "plain", "plain", "plain", "att_q", "att_q", "att_kv")


def _in_proj_kernel(x_ref, g_ref, sh_ref, sc_ref, w_ref, c1_ref, s1_ref, ca_ref, sa_ref,
                    cc_ref, wm_ref, bm_ref, o_ref, kd_ref, vd_ref, mod_ref):
    def mod_rider():
        cc = cc_ref[...]
        s_col = cc / (1.0 + jnp.exp(-cc))
        mod_ref[0] = jnp.sum(wm_ref[...] * s_col, axis=0, keepdims=True) + bm_ref[...]

    tn = _PROJ_TILE
    for r in range(x_ref.shape[0] // IN_ROW_CHUNK):
        rs = slice(r * IN_ROW_CHUNK, (r + 1) * IN_ROW_CHUNK)
        xf = x_ref[rs, :]
        y = xf * lax.rsqrt(jnp.mean(xf * xf, axis=-1, keepdims=True) + NORM_EPS)
        y = y * g_ref[...]
        h = (y * (1.0 + sc_ref[...]) + sh_ref[...]).astype(BF16)

        def rope1(a):
            return _rot_pairs(a, c1_ref[rs, :], s1_ref[rs, :], 32)

        def ropea(a):
            return _rot_pairs(a, ca_ref[rs, :], sa_ref[rs, :], 16)

        order = sorted(range(len(_PROJ_TILE_KINDS)), key=lambda t: _PROJ_TILE_KINDS[t] == "plain")
        for j in order:
            kind = _PROJ_TILE_KINDS[j]
            acc = jnp.dot(h, w_ref[:, j * tn:(j + 1) * tn], preferred_element_type=F32)
            if r == 0 and j == order[-2]:
                mod_rider()
            for c in range(tn // LANES):
                a = acc[:, c * LANES:(c + 1) * LANES]
                if kind == "ret_q":
                    a = rope1(a)
                elif kind == "ret_k":
                    a = rope1(a) * K_SCALE
                elif kind == "att_q":
                    a = ropea(a) * (ATT_SCALE * LOG2E)
                elif kind == "att_kv" and c < 2:
                    a = ropea(a)
                o_ref[rs, j * tn + c * LANES:j * tn + (c + 1) * LANES] = a.astype(BF16)
                if kind == "att_kv":
                    dup_ref = kd_ref if c < 2 else vd_ref
                    d0, d1 = _dup_halves(a)
                    t = 2 * (c % 2)
                    dup_ref[rs, t * LANES:(t + 1) * LANES] = d0.astype(BF16)
                    dup_ref[rs, (t + 1) * LANES:(t + 2) * LANES] = d1.astype(BF16)


def _in_proj(x, g, sh, sc, w, tabs, c_col, w_mod, b_mod, mod_done, *, tm):
    m, d = x.shape
    n = w.shape[1]
    assert n == _PROJ_TILE * len(_PROJ_TILE_KINDS) and m % tm == 0 and tm % IN_ROW_CHUNK == 0
    steps = m // tm
    slab = (w_mod.shape[1] - mod_done) // steps
    assert slab * steps == w_mod.shape[1] - mod_done and slab % LANES == 0 and mod_done % slab == 0
    slab0 = mod_done // slab
    c1, s1, ca, sa = tabs
    row = lambda i: (i, 0)
    vec = pl.BlockSpec((1, d), lambda i: (0, 0))
    tab = pl.BlockSpec((tm, LANES), row)
    return pl.pallas_call(
        _in_proj_kernel,
        grid=(m // tm,),
        in_specs=[pl.BlockSpec((tm, d), row), vec, vec, vec,
                  pl.BlockSpec((d, n), lambda i: (0, 0), pipeline_mode=pl.Buffered(1)),
                  tab, tab, tab, tab,
                  pl.BlockSpec((d, 1), lambda i: (0, 0)),
                  pl.BlockSpec((d, slab), lambda i: (0, slab0 + i)),
                  pl.BlockSpec((1, slab), lambda i: (0, slab0 + i))],
        out_specs=[pl.BlockSpec((tm, n), row),
                   pl.BlockSpec((tm, KV_DUP_COLS), row),
                   pl.BlockSpec((tm, KV_DUP_COLS), row),
                   pl.BlockSpec((1, 1, slab), lambda i: (i, 0, 0))],
        out_shape=[jax.ShapeDtypeStruct((m, n), BF16),
                   jax.ShapeDtypeStruct((m, KV_DUP_COLS), BF16),
                   jax.ShapeDtypeStruct((m, KV_DUP_COLS), BF16),
                   jax.ShapeDtypeStruct((steps, 1, slab), F32)],
        compiler_params=_params("parallel"),
        name="in_proj",
    )(x, g, sh, sc, w, c1, s1, ca, sa, c_col, w_mod, b_mod)


def _ctx_proj_kernel(x_ref, g_ref, sh_ref, sc_ref, w_ref, o_ref, kd_ref, vd_ref, wb_ref, h_ref):
    j = pl.program_id(0)

    @pl.when(j == 0)
    def _():
        xf = x_ref[...]
        y = xf * lax.rsqrt(jnp.mean(xf * xf, axis=-1, keepdims=True) + NORM_EPS)
        y = y * g_ref[...]
        h_ref[...] = (y * (1.0 + sc_ref[...]) + sh_ref[...]).astype(BF16)

    wb = w_ref[...].astype(BF16)
    wb_ref[...] = wb
    acc = jnp.dot(h_ref[...], wb, preferred_element_type=F32)
    is_ret_k = _PROJ_TILE_KINDS.index("ret_k")
    o_ref[...] = (acc * jnp.where(j == is_ret_k, K_SCALE, 1.0)).astype(BF16)

    @pl.when(j == _PROJ_TILE_KINDS.index("att_kv"))
    def _():
        for c in range(_PROJ_TILE // LANES):
            dup_ref = kd_ref if c < 2 else vd_ref
            d0, d1 = _dup_halves(acc[:, c * LANES:(c + 1) * LANES])
            t = 2 * (c % 2)
            dup_ref[:, t * LANES:(t + 1) * LANES] = d0.astype(BF16)
            dup_ref[:, (t + 1) * LANES:(t + 2) * LANES] = d1.astype(BF16)


def _ctx_proj(x, g, sh, sc, w):
    m, d = x.shape
    n = w.shape[1]
    tn = _PROJ_TILE
    assert n == tn * len(_PROJ_TILE_KINDS)
    fixed = lambda j: (0, 0)
    vec = pl.BlockSpec((1, d), fixed)
    return pl.pallas_call(
        _ctx_proj_kernel,
        grid=(n // tn,),
        in_specs=[pl.BlockSpec((m, d), fixed), vec, vec, vec,
                  pl.BlockSpec((d, tn), lambda j: (0, j))],
        out_specs=[pl.BlockSpec((m, tn), lambda j: (0, j)),
                   pl.BlockSpec((m, KV_DUP_COLS), fixed),
                   pl.BlockSpec((m, KV_DUP_COLS), fixed),
                   pl.BlockSpec((d, tn), lambda j: (0, j))],
        out_shape=[jax.ShapeDtypeStruct((m, n), BF16),
                   jax.ShapeDtypeStruct((m, KV_DUP_COLS), BF16),
                   jax.ShapeDtypeStruct((m, KV_DUP_COLS), BF16),
                   jax.ShapeDtypeStruct((d, n), BF16)],
        scratch_shapes=[pltpu.VMEM((m, d), BF16)],
        compiler_params=_params("arbitrary"),
        name="ctx_proj",
    )(x, g, sh, sc, w)


def _pair_lg(dec_ref, d, p, shape):
    lane = lax.broadcasted_iota(jnp.int32, shape, 1)
    first = (lane % LANES) < 64
    raw = jnp.where(first, jnp.full(shape, dec_ref[d, 2 * p], F32), jnp.full(shape, dec_ref[d, 2 * p + 1], F32))
    return -jnp.exp(raw)


def _head_block_mask(shape):
    r = lax.broadcasted_iota(jnp.int32, shape, 0)
    c = lax.broadcasted_iota(jnp.int32, shape, 1)
    return (r // 64) == (c // LANES)


def _kv_pair(k_pair, v_pair, w):
    kw = (k_pair.astype(F32) * w).astype(BF16)
    kv = lax.dot_general(kw, v_pair, (((0,), (0,)), ((), ())), preferred_element_type=F32)
    return jnp.where(_head_block_mask(kv.shape), kv, 0.0)


def _row_decay(dec_ref, d, p):
    shape = (LANES, 2 * RET_DV)
    rowh = lax.broadcasted_iota(jnp.int32, shape, 0) < 64
    raw = jnp.where(rowh, jnp.full(shape, dec_ref[d, 2 * p], F32), jnp.full(shape, dec_ref[d, 2 * p + 1], F32))
    return jnp.exp(-jnp.exp(raw) * float(RET_CHUNK))


def _compact_state(s):
    row = lax.broadcasted_iota(jnp.int32, (LANES, RET_DV), 0)
    return jnp.where(row < 64, s[:, :RET_DV], s[:, RET_DV:])


def _expand_state(c):
    row = lax.broadcasted_iota(jnp.int32, c.shape, 0)
    z = jnp.zeros_like(c)
    return jnp.concatenate([jnp.where(row < 64, c, z), jnp.where(row < 64, z, c)], axis=1)


def _ret_bwd_kernel(dec_ref, k_ref, v_ref, ck_ref, cv_ref, sb_ref, sbs):
    i = pl.program_id(0)
    C = RET_CHUNK
    lc = ck_ref.shape[0]

    @pl.when(i == 0)
    def _():
        pos = lax.broadcasted_iota(jnp.int32, (lc, LANES), 0).astype(F32)
        for p in range(RET_PAIRS):
            ks = slice(p * LANES, (p + 1) * LANES)
            vs = slice(p * 2 * RET_DV, (p + 1) * 2 * RET_DV)
            wb = jnp.exp(_pair_lg(dec_ref, 1, p, (lc, LANES)) * pos)
            sbs[p] = _kv_pair(ck_ref[:, ks], cv_ref[:, vs], wb)

    pos = lax.broadcasted_iota(jnp.int32, (C, LANES), 0).astype(F32)
    for p in range(RET_PAIRS):
        ks = slice(p * LANES, (p + 1) * LANES)
        vs = slice(p * 2 * RET_DV, (p + 1) * 2 * RET_DV)
        wb = jnp.exp(_pair_lg(dec_ref, 1, p, (C, LANES)) * pos)
        gb = _row_decay(dec_ref, 1, p)
        sb = sbs[p]
        for cc in reversed(range(RET_STEP_CHUNKS)):
            rs = slice(cc * C, (cc + 1) * C)
            sb_ref[cc, p] = _compact_state(sb).astype(BF16)
            sb = gb * sb + _kv_pair(k_ref[rs, ks], v_ref[rs, vs], wb)
        sbs[p] = sb


def _ret_bwd_states(dec, proj, cproj):
    L = proj.shape[0]
    lc = cproj.shape[0]
    S = RET_STEP_CHUNKS
    R = S * RET_CHUNK
    n = L // R
    return pl.pallas_call(
        _ret_bwd_kernel,
        grid=(n,),
        in_specs=[pl.BlockSpec(memory_space=pltpu.SMEM),
                  pl.BlockSpec((R, RET_QK_COLS), lambda i: (n - 1 - i, 1)),
                  pl.BlockSpec((R, RET_V_COLS), lambda i: (n - 1 - i, 1)),
                  pl.BlockSpec((lc, RET_QK_COLS), lambda i: (0, 1)),
                  pl.BlockSpec((lc, RET_V_COLS), lambda i: (0, 1))],
        out_specs=pl.BlockSpec((S, RET_PAIRS, LANES, RET_DV), lambda i: (n - 1 - i, 0, 0, 0)),
        out_shape=jax.ShapeDtypeStruct((n * S, RET_PAIRS, LANES, RET_DV), BF16),
        scratch_shapes=[pltpu.VMEM((RET_PAIRS, LANES, 2 * RET_DV), F32)],
        compiler_params=_params("arbitrary"),
        name="ret_bwd",
    )(dec, proj, proj, cproj, cproj)


def _ret_out_pieces(dec_ref, q_ref, k_ref, v_ref, g_ref, sb_ref, ck_ref, cv_ref, o_ref, sfs):
    i = pl.program_id(0)
    C = RET_CHUNK
    lc = ck_ref.shape[0]
    n_chunks = q_ref.shape[0] // C

    @pl.when(i == 0)
    def _():
        cpos = lax.broadcasted_iota(jnp.int32, (lc, LANES), 0).astype(F32)
        for p in range(RET_PAIRS):
            ks = slice(p * LANES, (p + 1) * LANES)
            vs = slice(p * 2 * RET_DV, (p + 1) * 2 * RET_DV)
            wf = jnp.exp(_pair_lg(dec_ref, 0, p, (lc, LANES)) * (lc - 1.0 - cpos))
            sfs[p] = _kv_pair(ck_ref[:, ks], cv_ref[:, vs], wf)

    pos = lax.broadcasted_iota(jnp.int32, (C, LANES), 0).astype(F32)
    n_i = lax.broadcasted_iota(jnp.int32, (C, 2 * C), 0)
    m_i = lax.broadcasted_iota(jnp.int32, (C, 2 * C), 1) % C
    rel = (n_i - m_i).astype(F32)
    lane = lax.broadcasted_iota(jnp.int32, (C, LANES), 1)
    lo = lane < 64

    def pair_tables(p):
        col_a = lax.broadcasted_iota(jnp.int32, (C, 2 * C), 1) < C
        raw_f = jnp.where(col_a, jnp.full((C, 2 * C), dec_ref[0, 2 * p], F32), jnp.full((C, 2 * C), dec_ref[0, 2 * p + 1], F32))
        raw_b = jnp.where(col_a, jnp.full((C, 2 * C), dec_ref[1, 2 * p], F32), jnp.full((C, 2 * C), dec_ref[1, 2 * p + 1], F32))
        dmat = jnp.where(rel >= 0, jnp.exp(-jnp.exp(raw_f) * jnp.maximum(rel, 0.0)),
                         jnp.exp(-jnp.exp(raw_b) * jnp.maximum(-rel, 0.0)))
        lg_f = _pair_lg(dec_ref, 0, p, (C, LANES))
        return dict(dmat=dmat, wqf=jnp.exp(lg_f * (pos + 1.0)),
                    wqb=jnp.exp(_pair_lg(dec_ref, 1, p, (C, LANES)) * (float(C) - pos)),
                    wkf=jnp.exp(lg_f * (C - 1.0 - pos)), gf=_row_decay(dec_ref, 0, p), sf=sfs[p])

    def chunk(p, cc, st):
        if cc == 0:
            st.update(pair_tables(p))
        ks = slice(p * LANES, (p + 1) * LANES)
        vs = slice(p * 2 * RET_DV, (p + 1) * 2 * RET_DV)
        rs = slice(cc * C, (cc + 1) * C)
        q = q_ref[rs, ks]
        k = k_ref[rs, ks]
        v = v_ref[rs, vs]
        zk = jnp.zeros_like(k)
        kst = jnp.concatenate([jnp.where(lo, k, zk), jnp.where(lo, zk, k)], axis=0)
        s = lax.dot_general(q, kst, (((1,), (1,)), ((), ())), preferred_element_type=F32)
        sd = (s * st["dmat"]).astype(BF16)
        qf32 = q.astype(F32)
        qwf = (qf32 * st["wqf"]).astype(BF16)
        qwb = (qf32 * st["wqb"]).astype(BF16)
        zv = jnp.zeros((C, RET_DV), BF16)
        vbd = jnp.concatenate([jnp.concatenate([v[:, :RET_DV], zv], axis=1),
                               jnp.concatenate([zv, v[:, RET_DV:]], axis=1)], axis=0)
        lhs = jnp.concatenate([sd, qwf, qwb], axis=1)
        rhs = jnp.concatenate([vbd, st["sf"].astype(BF16), _expand_state(sb_ref[cc, p])], axis=0)
        o = jnp.dot(lhs, rhs, preferred_element_type=F32)
        st["sf"] = st["gf"] * st["sf"] + _kv_pair(k, v, st["wkf"])
        for t in range(2):
            oh = o[:, t * RET_DV:(t + 1) * RET_DV]
            oh = oh * lax.rsqrt(jnp.mean(oh * oh, axis=-1, keepdims=True) + NORM_EPS)
            cs = slice(p * 2 * RET_DV + t * RET_DV, p * 2 * RET_DV + (t + 1) * RET_DV)
            gt = g_ref[rs, cs].astype(F32)
            o_ref[rs, cs] = (oh * (gt / (1.0 + jnp.exp(-gt)))).astype(BF16)
        if cc == n_chunks - 1:
            sfs[p] = st["sf"]

    pieces = []
    for p in range(RET_PAIRS):
        st = {}
        pieces += [(lambda p=p, cc=cc, st=st: chunk(p, cc, st)) for cc in range(n_chunks)]
    return pieces


def _attn_kernel(sink_ref, q_ref, kp_ref, kc_ref, kn_ref, vp_ref, vc_ref, vn_ref, ck_ref, cv_ref, o_ref, *, fillers):
    n = pl.program_id(0)
    nstep = pl.num_programs(0)
    B = ATT_BLOCK
    SB = ATT_STEP_BLOCKS
    kj = lax.broadcasted_iota(jnp.int32, (B, B), 0)
    qi = lax.broadcasted_iota(jnp.int32, (B, B), 1)
    ok_prev = jnp.where(n > 0, 0.0, MASK_NEG).astype(F32)
    ok_next = jnp.where(n < nstep - 1, 0.0, MASK_NEG).astype(F32)

    def band(inside, ok):
        return jnp.concatenate([jnp.where(inside, ok, MASK_NEG).astype(F32)] * ATT_GROUP, axis=1)

    bias_prev = [band(kj >= qi, ok_prev if j == 0 else 0.0) for j in range(SB)]
    bias_next = [band(kj <= qi, ok_next if j == SB - 1 else 0.0) for j in range(SB)]
    lane = lax.broadcasted_iota(jnp.int32, (B, LANES), 1)
    lo = lane < 64
    hi = lane >= 64

    def keys_of(j, gs, prev_ref, cur_ref, next_ref, ctx_ref):
        prev = prev_ref[:, gs] if j == 0 else cur_ref[(j - 1) * B:j * B, gs]
        nxt = next_ref[:, gs] if j == SB - 1 else cur_ref[(j + 1) * B:(j + 2) * B, gs]
        return jnp.concatenate([prev, cur_ref[j * B:(j + 1) * B, gs], nxt, ctx_ref[:, gs]], axis=0)

    def scores(j, g):
        gs = slice(g * LANES, (g + 1) * LANES)
        kcat = keys_of(j, gs, kp_ref, kc_ref, kn_ref, ck_ref)
        qs = []
        for r in range(ATT_GROUP):
            h = ATT_GROUP * g + r
            qt = q_ref[j * B:(j + 1) * B, (h // 2) * LANES:(h // 2 + 1) * LANES]
            keep = lo if h % 2 == 0 else hi
            qs.append(jnp.where(keep, qt, jnp.zeros_like(qt)))
        q4 = jnp.concatenate(qs, axis=0)
        return lax.dot_general(kcat, q4, (((1,), (1,)), ((), ())), preferred_element_type=F32)

    def softmax(j, g, s):
        sk = jnp.concatenate([jnp.full((1, B), sink_ref[ATT_GROUP * g + r], F32)
                              for r in range(ATT_GROUP)], axis=1) * LOG2E
        s = jnp.concatenate([s[:B] + bias_prev[j], s[B:2 * B], s[2 * B:3 * B] + bias_next[j], s[3 * B:]], axis=0)
        m = jnp.maximum(jnp.max(s, axis=0, keepdims=True), sk)
        e = jnp.exp2(s - m)
        den = jnp.sum(e, axis=0, keepdims=True) + jnp.exp2(sk - m)
        return e.astype(BF16), den

    def values(j, g, e, den):
        gs = slice(g * LANES, (g + 1) * LANES)
        vcat = keys_of(j, gs, vp_ref, vc_ref, vn_ref, cv_ref)
        res = lax.dot_general(vcat, e, (((0,), (0,)), ((), ())), preferred_element_type=F32) * (1.0 / den)
        for t in range(2):
            even = res[:, (2 * t) * B:(2 * t + 1) * B].T
            odd = res[:, (2 * t + 1) * B:(2 * t + 2) * B].T
            c0 = (2 * g + t) * LANES
            o_ref[j * B:(j + 1) * B, c0:c0 + LANES] = jnp.where(lo, even, odd).astype(BF16)

    units = [(j, g) for j in range(SB) for g in range(ATT_KV_HEADS)]
    per_unit = -(-len(fillers) // len(units))
    s_next = scores(*units[0])
    pending = None
    for u, unit in enumerate(units):
        s_cur = s_next
        if u + 1 < len(units):
            s_next = scores(*units[u + 1])
        e_den = softmax(*unit, s_cur)
        for fill in fillers[u * per_unit:(u + 1) * per_unit]:
            fill()
        if pending is not None:
            values(*units[u - 1], *pending)
        pending = e_den
    values(*units[-1], *pending)


def _mid_kernel(*refs, fillers):
    att_in, ret_in = refs[:10], refs[10:18]
    o_att, o_ret, sfs = refs[18:]
    ret = _ret_out_pieces(*ret_in, o_ret, sfs)
    n_units = ATT_STEP_BLOCKS * ATT_KV_HEADS
    assert len(ret) == n_units
    per = -(-len(fillers) // n_units)
    groups = [ret[u:u + 1] + fillers[u * per:(u + 1) * per] for u in range(n_units)]
    _attn_kernel(*att_in, o_att, fillers=[(lambda g=g: [f() for f in g]) for g in groups])


def _mid(sink, dec, proj, kd, vd, ckd, cvd, sb, cproj, riders):
    L = proj.shape[0]
    B = ATT_BLOCK
    SB = ATT_STEP_BLOCKS
    n = L // (SB * B)
    nb = L // B
    lc = ckd.shape[0]
    prev = pl.BlockSpec((B, KV_DUP_COLS), lambda i: (jnp.maximum(i * SB - 1, 0), 0))
    cur = pl.BlockSpec((SB * B, KV_DUP_COLS), lambda i: (i, 0))
    nxt = pl.BlockSpec((B, KV_DUP_COLS), lambda i: (jnp.minimum((i + 1) * SB, nb - 1), 0))
    full = pl.BlockSpec((lc, KV_DUP_COLS), lambda i: (0, 0))
    rid_in_specs, rid_out_specs, rid_shapes = _rider_specs(riders, n)
    R = SB * B
    S = R // RET_CHUNK
    lcr = cproj.shape[0]
    return pl.pallas_call(
        _with_cast_riders(_mid_kernel, 18, 2, len(riders)),
        grid=(n,),
        in_specs=[pl.BlockSpec(memory_space=pltpu.SMEM),
                  pl.BlockSpec((R, ATT_Q_COLS), lambda i: (i, 3)),
                  prev, cur, nxt, prev, cur, nxt, full, full,
                  pl.BlockSpec(memory_space=pltpu.SMEM),
                  pl.BlockSpec((R, RET_QK_COLS), lambda i: (i, 0)),
                  pl.BlockSpec((R, RET_QK_COLS), lambda i: (i, 1)),
                  pl.BlockSpec((R, RET_V_COLS), lambda i: (i, 1)),
                  pl.BlockSpec((R, RET_V_COLS), lambda i: (i, 2)),
                  pl.BlockSpec((S, RET_PAIRS, LANES, RET_DV), lambda i: (i, 0, 0, 0)),
                  pl.BlockSpec((lcr, RET_QK_COLS), lambda i: (0, 1)),
                  pl.BlockSpec((lcr, RET_V_COLS), lambda i: (0, 1))] + rid_in_specs,
        out_specs=[pl.BlockSpec((R, ATT_HEADS * ATT_DH), lambda i: (i, 0)),
                   pl.BlockSpec((R, RET_HEADS * RET_DV), lambda i: (i, 0))] + rid_out_specs,
        out_shape=[jax.ShapeDtypeStruct((L, ATT_HEADS * ATT_DH), BF16),
                   jax.ShapeDtypeStruct((L, RET_HEADS * RET_DV), BF16)] + rid_shapes,
        scratch_shapes=[pltpu.VMEM((RET_PAIRS, LANES, 2 * RET_DV), F32)],
        compiler_params=_params("arbitrary"),
        name="attn",
    )(sink, proj, kd, kd, kd, vd, vd, vd, ckd, cvd, dec, proj, proj, proj, proj, sb, cproj, cproj,
      *[r[0] for r in riders])


def _out_proj_kernel(yr_ref, ya_ref, w_ref, x_ref, gt_ref, g_ref, sh_ref, sc_ref, o_ref, h_ref):
    kr = yr_ref.shape[1]
    for r in range(yr_ref.shape[0] // OUT_ROW_CHUNK):
        rs = slice(r * OUT_ROW_CHUNK, (r + 1) * OUT_ROW_CHUNK)
        acc = jnp.dot(yr_ref[rs, :], w_ref[:kr, :], preferred_element_type=F32)
        acc = acc + jnp.dot(ya_ref[rs, :], w_ref[kr:, :], preferred_element_type=F32)
        x1 = x_ref[rs, :] + gt_ref[...] * acc
        o_ref[rs, :] = x1
        y = x1 * lax.rsqrt(jnp.mean(x1 * x1, axis=-1, keepdims=True) + NORM_EPS)
        y = y * g_ref[...]
        h_ref[rs, :] = (y * (1.0 + sc_ref[...]) + sh_ref[...]).astype(BF16)


def _out_proj(yr, ya, w, x, gt, g, sh, sc, *, tm):
    m, d = x.shape
    kr, ka = yr.shape[1], ya.shape[1]
    row = lambda i: (i, 0)
    vec = pl.BlockSpec((1, d), lambda i: (0, 0))
    return pl.pallas_call(
        _out_proj_kernel,
        grid=(m // tm,),
        in_specs=[pl.BlockSpec((tm, kr), row), pl.BlockSpec((tm, ka), row),
                  pl.BlockSpec((kr + ka, d), lambda i: (0, 0)),
                  pl.BlockSpec((tm, d), row), vec, vec, vec, vec],
        out_specs=[pl.BlockSpec((tm, d), row), pl.BlockSpec((tm, d), row)],
        out_shape=[jax.ShapeDtypeStruct((m, d), F32), jax.ShapeDtypeStruct((m, d), BF16)],
        compiler_params=_params("parallel"),
        name="out_proj",
    )(yr, ya, w, x, gt, g, sh, sc)


def _ffn_kernel(h_ref, gt_ref, gfin_ref, wg_ref, wu_ref, wd_ref, x_hbm, o_ref, x_buf, sem):
    i = pl.program_id(0)
    f = pl.program_id(1)
    last = pl.num_programs(1) - 1
    rows = o_ref.shape[0]
    x_copy = pltpu.make_async_copy(x_hbm.at[pl.ds(pl.multiple_of(i * rows, rows), rows), :], x_buf, sem.at[0])

    def step(first, final):
        wd = wd_ref[...].astype(BF16)
        for r in range(rows // FFN_ROW_CHUNK):
            rs = slice(r * FFN_ROW_CHUNK, (r + 1) * FFN_ROW_CHUNK)
            h = h_ref[rs, :]
            a = jnp.dot(h, wg_ref[0], preferred_element_type=F32)
            u = jnp.dot(h, wu_ref[0], preferred_element_type=F32)
            act = ((a / (1.0 + jnp.exp(-a))) * u).astype(BF16)
            part = jnp.dot(act, wd, preferred_element_type=F32)
            if first:
                o_ref[rs, :] = part
            elif not final:
                o_ref[rs, :] += part
            else:
                y = x_buf[rs, :] + gt_ref[...] * (o_ref[rs, :] + part)
                y = y * lax.rsqrt(jnp.mean(y * y, axis=-1, keepdims=True) + NORM_EPS)
                o_ref[rs, :] = y * gfin_ref[...]

    @pl.when(f == 0)
    def _():
        x_copy.start()
        step(first=True, final=False)

    @pl.when((f > 0) & (f < last))
    def _():
        step(first=False, final=False)

    @pl.when(f == last)
    def _():
        x_copy.wait()
        step(first=False, final=True)


def _ffn(h, x, gt, gfin, wg, wu, wd, *, tm):
    m, d = x.shape
    nf = wg.shape[0]
    assert wg.shape == wu.shape == (nf, d, FFN_TILE) and wd.shape == (nf * FFN_TILE, d)
    assert m % tm == 0 and tm % FFN_ROW_CHUNK == 0
    row = lambda i, f: (i, 0)
    vec = pl.BlockSpec((1, d), lambda i, f: (0, 0))
    wcol = pl.BlockSpec((1, d, FFN_TILE), lambda i, f: (f, 0, 0))
    return pl.pallas_call(
        _ffn_kernel,
        grid=(m // tm, nf),
        in_specs=[pl.BlockSpec((tm, d), row), vec, vec, wcol, wcol,
                  pl.BlockSpec((FFN_TILE, d), lambda i, f: (f, 0)),
                  pl.BlockSpec(memory_space=pl.ANY)],
        out_specs=pl.BlockSpec((tm, d), row),
        out_shape=jax.ShapeDtypeStruct((m, d), F32),
        scratch_shapes=[pltpu.VMEM((tm, d), F32), pltpu.SemaphoreType.DMA((1,))],
        compiler_params=_params("arbitrary", "arbitrary"),
        name="ffn",
    )(h, gt, gfin, wg, wu, wd, x)


def _rope_tables(L):
    f32 = np.float32
    lane = np.arange(LANES)
    inv1 = f32(ROPE_BASE) ** (-np.arange(32, dtype=f32) / f32(32))
    ang1 = np.arange(L, dtype=f32)[:, None] * inv1[None, :]
    sgn1 = np.where((lane % 64) < 32, -1.0, 1.0).astype(f32)
    cos1 = np.tile(np.cos(ang1), (1, LANES // 32))
    sin1 = np.tile(np.sin(ang1), (1, LANES // 32)) * sgn1[None, :]
    inv2 = f32(ROPE_BASE) ** (-np.arange(16, dtype=f32) / f32(16))
    nrow = L // GRID_W
    ang_r = np.arange(nrow, dtype=f32)[:, None] * inv2[None, :]
    ang_c = np.arange(GRID_W, dtype=f32)[:, None] * inv2[None, :]
    sgna = np.where((lane % 32) < 16, -1.0, 1.0).astype(f32)

    def expand(fr, fc):
        by_row = np.broadcast_to(np.tile(fr, (1, 2))[:, None, :], (nrow, GRID_W, 32))
        by_col = np.broadcast_to(np.tile(fc, (1, 2))[None, :, :], (nrow, GRID_W, 32))
        head = np.concatenate([by_row, by_col], axis=-1).reshape(L, 64)
        return np.tile(head, (1, LANES // 64))

    cosa = expand(np.cos(ang_r), np.cos(ang_c))
    sina = expand(np.sin(ang_r), np.sin(ang_c)) * sgna[None, :]
    return tuple(np.ascontiguousarray(t, dtype=f32) for t in (cos1, sin1, cosa, sina))


def kernel(x, c, ctx, c_ctx, w_mod, b_mod, norm_mix, norm_ffn, w_in, ret_decay, attn_sink,
           w_out, w_gate, w_up, w_down, norm_final):
    B, L, D = x.shape
    assert B == 1 and w_mod.shape[0] == 1, "single batch element, depth-1 layer"
    x2 = x[0]
    xc2 = ctx[0]

    cv = jnp.zeros((8, D), F32).at[0].set(c[0]).at[1].set(c_ctx)
    mod = _mod(cv, w_mod[0], b_mod[0][None, :], 2 * D)
    sh_m, sc_m = mod[0:1, 0:D], mod[0:1, D:2 * D]
    sh_mc, sc_mc = mod[1:2, 0:D], mod[1:2, D:2 * D]

    g_mix = norm_mix[0][None, :]
    cproj, ckd, cvd, w_in_b = _ctx_proj(xc2, g_mix, sh_mc, sc_mc, w_in[0])
    proj, kd, vd, mod_rest = _in_proj(x2, g_mix, sh_m, sc_m, w_in_b, _rope_tables(L),
                                      c[0][:, None], w_mod[0], b_mod[0][None, :], 2 * D, tm=ROW_TILE)
    gt_m, sh_f, sc_f, gt_f = [mod_rest.reshape(1, 4 * D)[:, k * D:(k + 1) * D] for k in range(4)]

    dec = ret_decay[0].astype(F32)
    sb = _ret_bwd_states(dec, proj, cproj)
    y_att, y_ret, w_gate_b, w_up_b, w_out_b = _mid(
        attn_sink[0].astype(F32), dec, proj, kd, vd, ckd, cvd, sb, cproj,
        [(w_gate[0], 1, FFN_TILE), (w_up[0], 1, FFN_TILE), (w_out[0], 1, None)])

    x1, hff = _out_proj(y_ret, y_att, w_out_b, x2, gt_m, norm_ffn[0][None, :], sh_f, sc_f, tm=ROW_TILE)
    out = _ffn(hff, x1, gt_f, norm_final[None, :], w_gate_b, w_up_b, w_down[0], tm=FFN_ROW_TILE)
    return out[None]
```

```python
import jax
import jax.numpy as jnp
import numpy as np
from jax import lax
from jax.experimental import pallas as pl
from jax.experimental.pallas import tpu as pltpu

GRID_W = 64
RET_HEADS = 8
RET_DK = 64
RET_DV = 128
RET_CHUNK = 128
ATT_HEADS = 16
ATT_KV_HEADS = 4
ATT_DH = 64
ATT_GROUP = ATT_HEADS // ATT_KV_HEADS
WINDOW = 128
ATT_BLOCK = 128
ROPE_BASE = 10000.0
NORM_EPS = 1e-6
K_SCALE = RET_DK ** -0.5
ATT_SCALE = ATT_DH ** -0.5
LOG2E = 1.4426950408889634

RET_QK_COLS = RET_HEADS * RET_DK
RET_V_COLS = RET_HEADS * RET_DV
ATT_Q_COLS = ATT_HEADS * ATT_DH
KV_DUP_COLS = 2 * ATT_KV_HEADS * ATT_DH

LANES = 128
RET_PAIRS = RET_HEADS // 2
MASK_NEG = -1e30
VMEM_LIMIT = 56 * 1024 * 1024
CAST_PIECE_ROWS = 16
RET_STEP_CHUNKS = 16
ATT_STEP_BLOCKS = 4
ROW_TILE = 512
OUT_ROW_CHUNK = ROW_TILE
IN_ROW_CHUNK = ROW_TILE
FFN_TILE = 512
FFN_ROW_TILE = 1024
FFN_ROW_CHUNK = FFN_ROW_TILE

BF16 = jnp.bfloat16
F32 = jnp.float32


def _params(*sem):
    return pltpu.CompilerParams(dimension_semantics=sem, vmem_limit_bytes=VMEM_LIMIT)


def _with_cast_riders(body, n_in, n_out, n_rid):
    def wrapped(*refs):
        ins = refs[:n_in]
        rid_in = refs[n_in:n_in + n_rid]
        outs = refs[n_in + n_rid:n_in + n_rid + n_out]
        rid_out = refs[n_in + n_rid + n_out:n_in + 2 * n_rid + n_out]
        scratch = refs[n_in + 2 * n_rid + n_out:]

        def piece(src, dst, r0):
            rs = slice(r0, r0 + CAST_PIECE_ROWS)
            if len(dst.shape) == 2:
                dst[rs, :] = src[rs, :].astype(BF16)
            else:
                tc = dst.shape[2]
                for t in range(dst.shape[0]):
                    dst[t, rs, :] = src[rs, t * tc:(t + 1) * tc].astype(BF16)

        pieces = [(lambda s=src, d=dst, r=r0: piece(s, d, r))
                  for src, dst in zip(rid_in, rid_out) for r0 in range(0, src.shape[0], CAST_PIECE_ROWS)]
        done = []
        fillers = [(lambda p=p: (done.append(1), p())) for p in pieces]
        body(*ins, *outs, *scratch, fillers=fillers)
        assert len(done) == len(pieces), "every cast piece must be emitted exactly once"
    return wrapped


def _rider_specs(riders, steps):
    in_specs, out_specs, shapes = [], [], []
    for w, ncb, tile in riders:
        rows, cols = w.shape
        nrb = steps // ncb
        assert nrb * ncb == steps and rows % nrb == 0 and cols % ncb == 0
        br, bc = rows // nrb, cols // ncb
        assert br % CAST_PIECE_ROWS == 0 and bc % LANES == 0, "slab must be bf16-tile aligned"
        in_specs.append(pl.BlockSpec((br, bc), lambda i, ncb=ncb: (i // ncb, i % ncb)))
        if tile is None:
            out_specs.append(in_specs[-1])
            shapes.append(jax.ShapeDtypeStruct(w.shape, BF16))
        else:
            assert ncb == 1 and cols % tile == 0 and tile % LANES == 0
            out_specs.append(pl.BlockSpec((cols // tile, br, tile), lambda i: (0, i, 0)))
            shapes.append(jax.ShapeDtypeStruct((cols // tile, rows, tile), BF16))
    return in_specs, out_specs, shapes


def _mod_kernel(cv_ref, w_ref, b_ref, o_ref):
    cv = cv_ref[...]
    s = cv / (1.0 + jnp.exp(-cv))
    o_ref[...] = jnp.dot(s.astype(BF16), w_ref[...].astype(BF16),
                         preferred_element_type=F32) + b_ref[...]


def _mod(cv, w, b, n):
    d = w.shape[0]
    tn = 512
    assert n % tn == 0
    return pl.pallas_call(
        _mod_kernel,
        grid=(n // tn,),
        in_specs=[pl.BlockSpec((8, d), lambda j: (0, 0)),
                  pl.BlockSpec((d, tn), lambda j: (0, j)),
                  pl.BlockSpec((1, tn), lambda j: (0, j))],
        out_specs=pl.BlockSpec((8, tn), lambda j: (0, j)),
        out_shape=jax.ShapeDtypeStruct((8, n), F32),
        compiler_params=_params("parallel"),
        name="mod",
    )(cv, w, b)


def _rot_pairs(a, cos, sin_signed, half):
    lane = lax.broadcasted_iota(jnp.int32, a.shape, 1)
    first = (lane % (2 * half)) < half
    rot = jnp.where(first, pltpu.roll(a, LANES - half, 1), pltpu.roll(a, half, 1))
    return a * cos + rot * sin_signed


def _dup_halves(a):
    lane = lax.broadcasted_iota(jnp.int32, a.shape, 1)
    r = pltpu.roll(a, 64, 1)
    lo = lane < 64
    return jnp.where(lo, a, r), jnp.where(lo, r, a)


_PROJ_TILE = 512
_PROJ_TILE_KINDS = ("ret_q", "ret_k", "plain", "plain", "plain", "plain", "att_q", "att_q", "att_kv")


def _in_proj_kernel(x_ref, g_ref, sh_ref, sc_ref, w_ref, c1_ref, s1_ref, ca_ref, sa_ref,
                    cc_ref, wm_ref, bm_ref, o_ref, kd_ref, vd_ref, mod_ref):
    def mod_rider():
        cc = cc_ref[...]
        s_col = cc / (1.0 + jnp.exp(-cc))
        mod_ref[0] = jnp.sum(wm_ref[...] * s_col, axis=0, keepdims=True) + bm_ref[...]

    tn = _PROJ_TILE
    for r in range(x_ref.shape[0] // IN_ROW_CHUNK):
        rs = slice(r * IN_ROW_CHUNK, (r + 1) * IN_ROW_CHUNK)
        xf = x_ref[rs, :]
        y = xf * lax.rsqrt(jnp.mean(xf * xf, axis=-1, keepdims=True) + NORM_EPS)
        y = y * g_ref[...]
        h = (y * (1.0 + sc_ref[...]) + sh_ref[...]).astype(BF16)

        def rope1(a):
            return _rot_pairs(a, c1_ref[rs, :], s1_ref[rs, :], 32)

        def ropea(a):
            return _rot_pairs(a, ca_ref[rs, :], sa_ref[rs, :], 16)

        order = sorted(range(len(_PROJ_TILE_KINDS)), key=lambda t: _PROJ_TILE_KINDS[t] == "plain")
        for j in order:
            kind = _PROJ_TILE_KINDS[j]
            acc = jnp.dot(h, w_ref[:, j * tn:(j + 1) * tn], preferred_element_type=F32)
            if r == 0 and j == order[-2]:
                mod_rider()
            for c in range(tn // LANES):
                a = acc[:, c * LANES:(c + 1) * LANES]
                if kind == "ret_q":
                    a = rope1(a)
                elif kind == "ret_k":
                    a = rope1(a) * K_SCALE
                elif kind == "att_q":
                    a = ropea(a) * (ATT_SCALE * LOG2E)
                elif kind == "att_kv" and c < 2:
                    a = ropea(a)
                o_ref[rs, j * tn + c * LANES:j * tn + (c + 1) * LANES] = a.astype(BF16)
                if kind == "att_kv":
                    dup_ref = kd_ref if c < 2 else vd_ref
                    d0, d1 = _dup_halves(a)
                    t = 2 * (c % 2)
                    dup_ref[rs, t * LANES:(t + 1) * LANES] = d0.astype(BF16)
                    dup_ref[rs, (t + 1) * LANES:(t + 2) * LANES] = d1.astype(BF16)


def _in_proj(x, g, sh, sc, w, tabs, c_col, w_mod, b_mod, mod_done, *, tm):
    m, d = x.shape
    n = w.shape[1]
    assert n == _PROJ_TILE * len(_PROJ_TILE_KINDS) and m % tm == 0 and tm % IN_ROW_CHUNK == 0
    steps = m // tm
    slab = (w_mod.shape[1] - mod_done) // steps
    assert slab * steps == w_mod.shape[1] - mod_done and slab % LANES == 0 and mod_done % slab == 0
    slab0 = mod_done // slab
    c1, s1, ca, sa = tabs
    row = lambda i: (i, 0)
    vec = pl.BlockSpec((1, d), lambda i: (0, 0))
    tab = pl.BlockSpec((tm, LANES), row)
    return pl.pallas_call(
        _in_proj_kernel,
        grid=(m // tm,),
        in_specs=[pl.BlockSpec((tm, d), row), vec, vec, vec,
                  pl.BlockSpec((d, n), lambda i: (0, 0), pipeline_mode=pl.Buffered(1)),
                  tab, tab, tab, tab,
                  pl.BlockSpec((d, 1), lambda i: (0, 0)),
                  pl.BlockSpec((d, slab), lambda i: (0, slab0 + i)),
                  pl.BlockSpec((1, slab), lambda i: (0, slab0 + i))],
        out_specs=[pl.BlockSpec((tm, n), row),
                   pl.BlockSpec((tm, KV_DUP_COLS), row),
                   pl.BlockSpec((tm, KV_DUP_COLS), row),
                   pl.BlockSpec((1, 1, slab), lambda i: (i, 0, 0))],
        out_shape=[jax.ShapeDtypeStruct((m, n), BF16),
                   jax.ShapeDtypeStruct((m, KV_DUP_COLS), BF16),
                   jax.ShapeDtypeStruct((m, KV_DUP_COLS), BF16),
                   jax.ShapeDtypeStruct((steps, 1, slab), F32)],
        compiler_params=_params("parallel"),
        name="in_proj",
    )(x, g, sh, sc, w, c1, s1, ca, sa, c_col, w_mod, b_mod)


def _ctx_proj_kernel(x_ref, g_ref, sh_ref, sc_ref, w_ref, o_ref, kd_ref, vd_ref, wb_ref, h_ref):
    j = pl.program_id(0)

    @pl.when(j == 0)
    def _():
        xf = x_ref[...]
        y = xf * lax.rsqrt(jnp.mean(xf * xf, axis=-1, keepdims=True) + NORM_EPS)
        y = y * g_ref[...]
        h_ref[...] = (y * (1.0 + sc_ref[...]) + sh_ref[...]).astype(BF16)

    wb = w_ref[...].astype(BF16)
    wb_ref[...] = wb
    acc = jnp.dot(h_ref[...], wb, preferred_element_type=F32)
    is_ret_k = _PROJ_TILE_KINDS.index("ret_k")
    o_ref[...] = (acc * jnp.where(j == is_ret_k, K_SCALE, 1.0)).astype(BF16)

    @pl.when(j == _PROJ_TILE_KINDS.index("att_kv"))
    def _():
        for c in range(_PROJ_TILE // LANES):
            dup_ref = kd_ref if c < 2 else vd_ref
            d0, d1 = _dup_halves(acc[:, c * LANES:(c + 1) * LANES])
            t = 2 * (c % 2)
            dup_ref[:, t * LANES:(t + 1) * LANES] = d0.astype(BF16)
            dup_ref[:, (t + 1) * LANES:(t + 2) * LANES] = d1.astype(BF16)


def _ctx_proj(x, g, sh, sc, w):
    m, d = x.shape
    n = w.shape[1]
    tn = _PROJ_TILE
    assert n == tn * len(_PROJ_TILE_KINDS)
    fixed = lambda j: (0, 0)
    vec = pl.BlockSpec((1, d), fixed)
    return pl.pallas_call(
        _ctx_proj_kernel,
        grid=(n // tn,),
        in_specs=[pl.BlockSpec((m, d), fixed), vec, vec, vec,
                  pl.BlockSpec((d, tn), lambda j: (0, j))],
        out_specs=[pl.BlockSpec((m, tn), lambda j: (0, j)),
                   pl.BlockSpec((m, KV_DUP_COLS), fixed),
                   pl.BlockSpec((m, KV_DUP_COLS), fixed),
                   pl.BlockSpec((d, tn), lambda j: (0, j))],
        out_shape=[jax.ShapeDtypeStruct((m, n), BF16),
                   jax.ShapeDtypeStruct((m, KV_DUP_COLS), BF16),
                   jax.ShapeDtypeStruct((m, KV_DUP_COLS), BF16),
                   jax.ShapeDtypeStruct((d, n), BF16)],
        scratch_shapes=[pltpu.VMEM((m, d), BF16)],
        compiler_params=_params("arbitrary"),
        name="ctx_proj",
    )(x, g, sh, sc, w)


def _pair_lg(dec_ref, d, p, shape):
    lane = lax.broadcasted_iota(jnp.int32, shape, 1)
    first = (lane % LANES) < 64
    raw = jnp.where(first, jnp.full(shape, dec_ref[d, 2 * p], F32), jnp.full(shape, dec_ref[d, 2 * p + 1], F32))
    return -jnp.exp(raw)


def _head_block_mask(shape):
    r = lax.broadcasted_iota(jnp.int32, shape, 0)
    c = lax.broadcasted_iota(jnp.int32, shape, 1)
    return (r // 64) == (c // LANES)


def _kv_pair(k_pair, v_pair, w):
    kw = (k_pair.astype(F32) * w).astype(BF16)
    kv = lax.dot_general(kw, v_pair, (((0,), (0,)), ((), ())), preferred_element_type=F32)
    return jnp.where(_head_block_mask(kv.shape), kv, 0.0)


def _row_decay(dec_ref, d, p):
    shape = (LANES, 2 * RET_DV)
    rowh = lax.broadcasted_iota(jnp.int32, shape, 0) < 64
    raw = jnp.where(rowh, jnp.full(shape, dec_ref[d, 2 * p], F32), jnp.full(shape, dec_ref[d, 2 * p + 1], F32))
    return jnp.exp(-jnp.exp(raw) * float(RET_CHUNK))


def _compact_state(s):
    row = lax.broadcasted_iota(jnp.int32, (LANES, RET_DV), 0)
    return jnp.where(row < 64, s[:, :RET_DV], s[:, RET_DV:])


def _expand_state(c):
    row = lax.broadcasted_iota(jnp.int32, c.shape, 0)
    z = jnp.zeros_like(c)
    return jnp.concatenate([jnp.where(row < 64, c, z), jnp.where(row < 64, z, c)], axis=1)


def _ret_bwd_kernel(dec_ref, k_ref, v_ref, ck_ref, cv_ref, sb_ref, sbs):
    i = pl.program_id(0)
    C = RET_CHUNK
    lc = ck_ref.shape[0]

    @pl.when(i == 0)
    def _():
        pos = lax.broadcasted_iota(jnp.int32, (lc, LANES), 0).astype(F32)
        for p in range(RET_PAIRS):
            ks = slice(p * LANES, (p + 1) * LANES)
            vs = slice(p * 2 * RET_DV, (p + 1) * 2 * RET_DV)
            wb = jnp.exp(_pair_lg(dec_ref, 1, p, (lc, LANES)) * pos)
            sbs[p] = _kv_pair(ck_ref[:, ks], cv_ref[:, vs], wb)

    pos = lax.broadcasted_iota(jnp.int32, (C, LANES), 0).astype(F32)
    for p in range(RET_PAIRS):
        ks = slice(p * LANES, (p + 1) * LANES)
        vs = slice(p * 2 * RET_DV, (p + 1) * 2 * RET_DV)
        wb = jnp.exp(_pair_lg(dec_ref, 1, p, (C, LANES)) * pos)
        gb = _row_decay(dec_ref, 1, p)
        sb = sbs[p]
        for cc in reversed(range(RET_STEP_CHUNKS)):
            rs = slice(cc * C, (cc + 1) * C)
            sb_ref[cc, p] = _compact_state(sb).astype(BF16)
            sb = gb * sb + _kv_pair(k_ref[rs, ks], v_ref[rs, vs], wb)
        sbs[p] = sb


def _ret_bwd_states(dec, proj, cproj):
    L = proj.shape[0]
    lc = cproj.shape[0]
    S = RET_STEP_CHUNKS
    R = S * RET_CHUNK
    n = L // R
    return pl.pallas_call(
        _ret_bwd_kernel,
        grid=(n,),
        in_specs=[pl.BlockSpec(memory_space=pltpu.SMEM),
                  pl.BlockSpec((R, RET_QK_COLS), lambda i: (n - 1 - i, 1)),
                  pl.BlockSpec((R, RET_V_COLS), lambda i: (n - 1 - i, 1)),
                  pl.BlockSpec((lc, RET_QK_COLS), lambda i: (0, 1)),
                  pl.BlockSpec((lc, RET_V_COLS), lambda i: (0, 1))],
        out_specs=pl.BlockSpec((S, RET_PAIRS, LANES, RET_DV), lambda i: (n - 1 - i, 0, 0, 0)),
        out_shape=jax.ShapeDtypeStruct((n * S, RET_PAIRS, LANES, RET_DV), BF16),
        scratch_shapes=[pltpu.VMEM((RET_PAIRS, LANES, 2 * RET_DV), F32)],
        compiler_params=_params("arbitrary"),
        name="ret_bwd",
    )(dec, proj, proj, cproj, cproj)


def _ret_out_pieces(dec_ref, q_ref, k_ref, v_ref, g_ref, sb_ref, ck_ref, cv_ref, o_ref, sfs):
    i = pl.program_id(0)
    C = RET_CHUNK
    lc = ck_ref.shape[0]
    n_chunks = q_ref.shape[0] // C

    @pl.when(i == 0)
    def _():
        cpos = lax.broadcasted_iota(jnp.int32, (lc, LANES), 0).astype(F32)
        for p in range(RET_PAIRS):
            ks = slice(p * LANES, (p + 1) * LANES)
            vs = slice(p * 2 * RET_DV, (p + 1) * 2 * RET_DV)
            wf = jnp.exp(_pair_lg(dec_ref, 0, p, (lc, LANES)) * (lc - 1.0 - cpos))
            sfs[p] = _kv_pair(ck_ref[:, ks], cv_ref[:, vs], wf)

    pos = lax.broadcasted_iota(jnp.int32, (C, LANES), 0).astype(F32)
    n_i = lax.broadcasted_iota(jnp.int32, (C, 2 * C), 0)
    m_i = lax.broadcasted_iota(jnp.int32, (C, 2 * C), 1) % C
    rel = (n_i - m_i).astype(F32)
    lane = lax.broadcasted_iota(jnp.int32, (C, LANES), 1)
    lo = lane < 64

    def pair_tables(p):
        col_a = lax.broadcasted_iota(jnp.int32, (C, 2 * C), 1) < C
        raw_f = jnp.where(col_a, jnp.full((C, 2 * C), dec_ref[0, 2 * p], F32), jnp.full((C, 2 * C), dec_ref[0, 2 * p + 1], F32))
        raw_b = jnp.where(col_a, jnp.full((C, 2 * C), dec_ref[1, 2 * p], F32), jnp.full((C, 2 * C), dec_ref[1, 2 * p + 1], F32))
        dmat = jnp.where(rel >= 0, jnp.exp(-jnp.exp(raw_f) * jnp.maximum(rel, 0.0)),
                         jnp.exp(-jnp.exp(raw_b) * jnp.maximum(-rel, 0.0)))
        lg_f = _pair_lg(dec_ref, 0, p, (C, LANES))
        return dict(dmat=dmat, wqf=jnp.exp(lg_f * (pos + 1.0)),
                    wqb=jnp.exp(_pair_lg(dec_ref, 1, p, (C, LANES)) * (float(C) - pos)),
                    wkf=jnp.exp(lg_f * (C - 1.0 - pos)), gf=_row_decay(dec_ref, 0, p), sf=sfs[p])

    def chunk(p, cc, st):
        if cc == 0:
            st.update(pair_tables(p))
        ks = slice(p * LANES, (p + 1) * LANES)
        vs = slice(p * 2 * RET_DV, (p + 1) * 2 * RET_DV)
        rs = slice(cc * C, (cc + 1) * C)
        q = q_ref[rs, ks]
        k = k_ref[rs, ks]
        v = v_ref[rs, vs]
        zk = jnp.zeros_like(k)
        kst = jnp.concatenate([jnp.where(lo, k, zk), jnp.where(lo, zk, k)], axis=0)
        s = lax.dot_general(q, kst, (((1,), (1,)), ((), ())), preferred_element_type=F32)
        sd = (s * st["dmat"]).astype(BF16)
        qf32 = q.astype(F32)
        qwf = (qf32 * st["wqf"]).astype(BF16)
        qwb = (qf32 * st["wqb"]).astype(BF16)
        zv = jnp.zeros((C, RET_DV), BF16)
        vbd = jnp.concatenate([jnp.concatenate([v[:, :RET_DV], zv], axis=1),
                               jnp.concatenate([zv, v[:, RET_DV:]], axis=1)], axis=0)
        lhs = jnp.concatenate([sd, qwf, qwb], axis=1)
        st["mid"] = (lhs, vbd, k, v)

    def chunk_out(p, cc, st):
        lhs, vbd, k, v = st.pop("mid")
        rs = slice(cc * C, (cc + 1) * C)
        rhs = jnp.concatenate([vbd, st["sf"].astype(BF16), _expand_state(sb_ref[cc, p])], axis=0)
        o = jnp.dot(lhs, rhs, preferred_element_type=F32)
        st["sf"] = st["gf"] * st["sf"] + _kv_pair(k, v, st["wkf"])
        for t in range(2):
            oh = o[:, t * RET_DV:(t + 1) * RET_DV]
            oh = oh * lax.rsqrt(jnp.mean(oh * oh, axis=-1, keepdims=True) + NORM_EPS)
            cs = slice(p * 2 * RET_DV + t * RET_DV, p * 2 * RET_DV + (t + 1) * RET_DV)
            gt = g_ref[rs, cs].astype(F32)
            o_ref[rs, cs] = (oh * (gt / (1.0 + jnp.exp(-gt)))).astype(BF16)
        if cc == n_chunks - 1:
            sfs[p] = st["sf"]

    pieces = []
    for p in range(RET_PAIRS):
        st = {}
        pieces += [((lambda p=p, cc=cc, st=st: chunk(p, cc, st)), (lambda p=p, cc=cc, st=st: chunk_out(p, cc, st)))
                   for cc in range(n_chunks)]
    return pieces


def _attn_kernel(sink_ref, q_ref, kp_ref, kc_ref, kn_ref, vp_ref, vc_ref, vn_ref, ck_ref, cv_ref, o_ref, *,
                 fillers, pre=None, post=None):
    n = pl.program_id(0)
    nstep = pl.num_programs(0)
    B = ATT_BLOCK
    SB = ATT_STEP_BLOCKS
    kj = lax.broadcasted_iota(jnp.int32, (B, B), 0)
    qi = lax.broadcasted_iota(jnp.int32, (B, B), 1)
    ok_prev = jnp.where(n > 0, 0.0, MASK_NEG).astype(F32)
    ok_next = jnp.where(n < nstep - 1, 0.0, MASK_NEG).astype(F32)

    def band(inside, ok):
        return jnp.concatenate([jnp.where(inside, ok, MASK_NEG).astype(F32)] * ATT_GROUP, axis=1)

    bias_prev = [band(kj >= qi, ok_prev if j == 0 else 0.0) for j in range(SB)]
    bias_next = [band(kj <= qi, ok_next if j == SB - 1 else 0.0) for j in range(SB)]
    lane = lax.broadcasted_iota(jnp.int32, (B, LANES), 1)
    lo = lane < 64
    hi = lane >= 64

    def keys_of(j, gs, prev_ref, cur_ref, next_ref, ctx_ref):
        prev = prev_ref[:, gs] if j == 0 else cur_ref[(j - 1) * B:j * B, gs]
        nxt = next_ref[:, gs] if j == SB - 1 else cur_ref[(j + 1) * B:(j + 2) * B, gs]
        return jnp.concatenate([prev, cur_ref[j * B:(j + 1) * B, gs], nxt, ctx_ref[:, gs]], axis=0)

    def scores(j, g):
        gs = slice(g * LANES, (g + 1) * LANES)
        kcat = keys_of(j, gs, kp_ref, kc_ref, kn_ref, ck_ref)
        qs = []
        for r in range(ATT_GROUP):
            h = ATT_GROUP * g + r
            qt = q_ref[j * B:(j + 1) * B, (h // 2) * LANES:(h // 2 + 1) * LANES]
            keep = lo if h % 2 == 0 else hi
            qs.append(jnp.where(keep, qt, jnp.zeros_like(qt)))
        q4 = jnp.concatenate(qs, axis=0)
        return lax.dot_general(kcat, q4, (((1,), (1,)), ((), ())), preferred_element_type=F32)

    def softmax(j, g, s):
        sk = jnp.concatenate([jnp.full((1, B), sink_ref[ATT_GROUP * g + r], F32)
                              for r in range(ATT_GROUP)], axis=1) * LOG2E
        s = jnp.concatenate([s[:B] + bias_prev[j], s[B:2 * B], s[2 * B:3 * B] + bias_next[j], s[3 * B:]], axis=0)
        m = jnp.maximum(jnp.max(s, axis=0, keepdims=True), sk)
        e = jnp.exp2(s - m)
        den = jnp.sum(e, axis=0, keepdims=True) + jnp.exp2(sk - m)
        return e.astype(BF16), den

    def values(j, g, e, den):
        gs = slice(g * LANES, (g + 1) * LANES)
        vcat = keys_of(j, gs, vp_ref, vc_ref, vn_ref, cv_ref)
        res = lax.dot_general(vcat, e, (((0,), (0,)), ((), ())), preferred_element_type=F32) * (1.0 / den)
        for t in range(2):
            even = res[:, (2 * t) * B:(2 * t + 1) * B].T
            odd = res[:, (2 * t + 1) * B:(2 * t + 2) * B].T
            c0 = (2 * g + t) * LANES
            o_ref[j * B:(j + 1) * B, c0:c0 + LANES] = jnp.where(lo, even, odd).astype(BF16)

    units = [(j, g) for j in range(SB) for g in range(ATT_KV_HEADS)]
    per_unit = -(-len(fillers) // len(units))
    s_next = scores(*units[0])
    pending = None
    for u, unit in enumerate(units):
        s_cur = s_next
        if u + 1 < len(units):
            s_next = scores(*units[u + 1])
        if pre is not None:
            pre[u]()
        e_den = softmax(*unit, s_cur)
        for fill in fillers[u * per_unit:(u + 1) * per_unit]:
            fill()
        if pending is not None:
            values(*units[u - 1], *pending)
        if post is not None:
            post[u]()
        pending = e_den
    values(*units[-1], *pending)


def _mid_kernel(*refs, fillers):
    att_in, ret_in = refs[:10], refs[10:18]
    o_att, o_ret, sfs = refs[18:]
    ret = _ret_out_pieces(*ret_in, o_ret, sfs)
    n_units = ATT_STEP_BLOCKS * ATT_KV_HEADS
    assert len(ret) == n_units
    _attn_kernel(*att_in, o_att, fillers=fillers, post=[(lambda a=a, b=b: (a(), b())) for a, b in ret])


def _mid(sink, dec, proj, kd, vd, ckd, cvd, sb, cproj, riders):
    L = proj.shape[0]
    B = ATT_BLOCK
    SB = ATT_STEP_BLOCKS
    n = L // (SB * B)
    nb = L // B
    lc = ckd.shape[0]
    prev = pl.BlockSpec((B, KV_DUP_COLS), lambda i: (jnp.maximum(i * SB - 1, 0), 0))
    cur = pl.BlockSpec((SB * B, KV_DUP_COLS), lambda i: (i, 0))
    nxt = pl.BlockSpec((B, KV_DUP_COLS), lambda i: (jnp.minimum((i + 1) * SB, nb - 1), 0))
    full = pl.BlockSpec((lc, KV_DUP_COLS), lambda i: (0, 0))
    rid_in_specs, rid_out_specs, rid_shapes = _rider_specs(riders, n)
    R = SB * B
    S = R // RET_CHUNK
    lcr = cproj.shape[0]
    return pl.pallas_call(
        _with_cast_riders(_mid_kernel, 18, 2, len(riders)),
        grid=(n,),
        in_specs=[pl.BlockSpec(memory_space=pltpu.SMEM),
                  pl.BlockSpec((R, ATT_Q_COLS), lambda i: (i, 3)),
                  prev, cur, nxt, prev, cur, nxt, full, full,
                  pl.BlockSpec(memory_space=pltpu.SMEM),
                  pl.BlockSpec((R, RET_QK_COLS), lambda i: (i, 0)),
                  pl.BlockSpec((R, RET_QK_COLS), lambda i: (i, 1)),
                  pl.BlockSpec((R, RET_V_COLS), lambda i: (i, 1)),
                  pl.BlockSpec((R, RET_V_COLS), lambda i: (i, 2)),
                  pl.BlockSpec((S, RET_PAIRS, LANES, RET_DV), lambda i: (i, 0, 0, 0)),
                  pl.BlockSpec((lcr, RET_QK_COLS), lambda i: (0, 1)),
                  pl.BlockSpec((lcr, RET_V_COLS), lambda i: (0, 1))] + rid_in_specs,
        out_specs=[pl.BlockSpec((R, ATT_HEADS * ATT_DH), lambda i: (i, 0)),
                   pl.BlockSpec((R, RET_HEADS * RET_DV), lambda i: (i, 0))] + rid_out_specs,
        out_shape=[jax.ShapeDtypeStruct((L, ATT_HEADS * ATT_DH), BF16),
                   jax.ShapeDtypeStruct((L, RET_HEADS * RET_DV), BF16)] + rid_shapes,
        scratch_shapes=[pltpu.VMEM((RET_PAIRS, LANES, 2 * RET_DV), F32)],
        compiler_params=_params("arbitrary"),
        name="attn",
    )(sink, proj, kd, kd, kd, vd, vd, vd, ckd, cvd, dec, proj, proj, proj, proj, sb, cproj, cproj,
      *[r[0] for r in riders])


def _out_proj_kernel(yr_ref, ya_ref, w_ref, x_ref, gt_ref, g_ref, sh_ref, sc_ref, o_ref, h_ref):
    kr = yr_ref.shape[1]
    for r in range(yr_ref.shape[0] // OUT_ROW_CHUNK):
        rs = slice(r * OUT_ROW_CHUNK, (r + 1) * OUT_ROW_CHUNK)
        acc = jnp.dot(yr_ref[rs, :], w_ref[:kr, :], preferred_element_type=F32)
        acc = acc + jnp.dot(ya_ref[rs, :], w_ref[kr:, :], preferred_element_type=F32)
        x1 = x_ref[rs, :] + gt_ref[...] * acc
        o_ref[rs, :] = x1
        y = x1 * lax.rsqrt(jnp.mean(x1 * x1, axis=-1, keepdims=True) + NORM_EPS)
        y = y * g_ref[...]
        h_ref[rs, :] = (y * (1.0 + sc_ref[...]) + sh_ref[...]).astype(BF16)


def _out_proj(yr, ya, w, x, gt, g, sh, sc, *, tm):
    m, d = x.shape
    kr, ka = yr.shape[1], ya.shape[1]
    row = lambda i: (i, 0)
    vec = pl.BlockSpec((1, d), lambda i: (0, 0))
    return pl.pallas_call(
        _out_proj_kernel,
        grid=(m // tm,),
        in_specs=[pl.BlockSpec((tm, kr), row), pl.BlockSpec((tm, ka), row),
                  pl.BlockSpec((kr + ka, d), lambda i: (0, 0)),
                  pl.BlockSpec((tm, d), row), vec, vec, vec, vec],
        out_specs=[pl.BlockSpec((tm, d), row), pl.BlockSpec((tm, d), row)],
        out_shape=[jax.ShapeDtypeStruct((m, d), F32), jax.ShapeDtypeStruct((m, d), BF16)],
        compiler_params=_params("parallel"),
        name="out_proj",
    )(yr, ya, w, x, gt, g, sh, sc)


def _ffn_kernel(h_ref, gt_ref, gfin_ref, wg_ref, wu_ref, wd_ref, x_hbm, o_ref, x_buf, sem):
    i = pl.program_id(0)
    f = pl.program_id(1)
    last = pl.num_programs(1) - 1
    rows = o_ref.shape[0]
    x_copy = pltpu.make_async_copy(x_hbm.at[pl.ds(pl.multiple_of(i * rows, rows), rows), :], x_buf, sem.at[0])

    def step(first, final):
        wd = wd_ref[...].astype(BF16)
        for r in range(rows // FFN_ROW_CHUNK):
            rs = slice(r * FFN_ROW_CHUNK, (r + 1) * FFN_ROW_CHUNK)
            h = h_ref[rs, :]
            a = jnp.dot(h, wg_ref[0], preferred_element_type=F32)
            u = jnp.dot(h, wu_ref[0], preferred_element_type=F32)
            act = ((a / (1.0 + jnp.exp(-a))) * u).astype(BF16)
            part = jnp.dot(act, wd, preferred_element_type=F32)
            if first:
                o_ref[rs, :] = part
            elif not final:
                o_ref[rs, :] += part
            else:
                y = x_buf[rs, :] + gt_ref[...] * (o_ref[rs, :] + part)
                y = y * lax.rsqrt(jnp.mean(y * y, axis=-1, keepdims=True) + NORM_EPS)
                o_ref[rs, :] = y * gfin_ref[...]

    @pl.when(f == 0)
    def _():
        x_copy.start()
        step(first=True, final=False)

    @pl.when((f > 0) & (f < last))
    def _():
        step(first=False, final=False)

    @pl.when(f == last)
    def _():
        x_copy.wait()
        step(first=False, final=True)


def _ffn(h, x, gt, gfin, wg, wu, wd, *, tm):
    m, d = x.shape
    nf = wg.shape[0]
    assert wg.shape == wu.shape == (nf, d, FFN_TILE) and wd.shape == (nf * FFN_TILE, d)
    assert m % tm == 0 and tm % FFN_ROW_CHUNK == 0
    row = lambda i, f: (i, 0)
    vec = pl.BlockSpec((1, d), lambda i, f: (0, 0))
    wcol = pl.BlockSpec((1, d, FFN_TILE), lambda i, f: (f, 0, 0))
    return pl.pallas_call(
        _ffn_kernel,
        grid=(m // tm, nf),
        in_specs=[pl.BlockSpec((tm, d), row), vec, vec, wcol, wcol,
                  pl.BlockSpec((FFN_TILE, d), lambda i, f: (f, 0)),
                  pl.BlockSpec(memory_space=pl.ANY)],
        out_specs=pl.BlockSpec((tm, d), row),
        out_shape=jax.ShapeDtypeStruct((m, d), F32),
        scratch_shapes=[pltpu.VMEM((tm, d), F32), pltpu.SemaphoreType.DMA((1,))],
        compiler_params=_params("arbitrary", "arbitrary"),
        name="ffn",
    )(h, gt, gfin, wg, wu, wd, x)


def _rope_tables(L):
    f32 = np.float32
    lane = np.arange(LANES)
    inv1 = f32(ROPE_BASE) ** (-np.arange(32, dtype=f32) / f32(32))
    ang1 = np.arange(L, dtype=f32)[:, None] * inv1[None, :]
    sgn1 = np.where((lane % 64) < 32, -1.0, 1.0).astype(f32)
    cos1 = np.tile(np.cos(ang1), (1, LANES // 32))
    sin1 = np.tile(np.sin(ang1), (1, LANES // 32)) * sgn1[None, :]
    inv2 = f32(ROPE_BASE) ** (-np.arange(16, dtype=f32) / f32(16))
    nrow = L // GRID_W
    ang_r = np.arange(nrow, dtype=f32)[:, None] * inv2[None, :]
    ang_c = np.arange(GRID_W, dtype=f32)[:, None] * inv2[None, :]
    sgna = np.where((lane % 32) < 16, -1.0, 1.0).astype(f32)

    def expand(fr, fc):
        by_row = np.broadcast_to(np.tile(fr, (1, 2))[:, None, :], (nrow, GRID_W, 32))
        by_col = np.broadcast_to(np.tile(fc, (1, 2))[None, :, :], (nrow, GRID_W, 32))
        head = np.concatenate([by_row, by_col], axis=-1).reshape(L, 64)
        return np.tile(head, (1, LANES // 64))

    cosa = expand(np.cos(ang_r), np.cos(ang_c))
    sina = expand(np.sin(ang_r), np.sin(ang_c)) * sgna[None, :]
    return tuple(np.ascontiguousarray(t, dtype=f32) for t in (cos1, sin1, cosa, sina))


def kernel(x, c, ctx, c_ctx, w_mod, b_mod, norm_mix, norm_ffn, w_in, ret_decay, attn_sink,
           w_out, w_gate, w_up, w_down, norm_final):
    B, L, D = x.shape
    assert B == 1 and w_mod.shape[0] == 1, "single batch element, depth-1 layer"
    x2 = x[0]
    xc2 = ctx[0]

    cv = jnp.zeros((8, D), F32).at[0].set(c[0]).at[1].set(c_ctx)
    mod = _mod(cv, w_mod[0], b_mod[0][None, :], 2 * D)
    sh_m, sc_m = mod[0:1, 0:D], mod[0:1, D:2 * D]
    sh_mc, sc_mc = mod[1:2, 0:D], mod[1:2, D:2 * D]

    g_mix = norm_mix[0][None, :]
    cproj, ckd, cvd, w_in_b = _ctx_proj(xc2, g_mix, sh_mc, sc_mc, w_in[0])
    proj, kd, vd, mod_rest = _in_proj(x2, g_mix, sh_m, sc_m, w_in_b, _rope_tables(L),
                                      c[0][:, None], w_mod[0], b_mod[0][None, :], 2 * D, tm=ROW_TILE)
    gt_m, sh_f, sc_f, gt_f = [mod_rest.reshape(1, 4 * D)[:, k * D:(k + 1) * D] for k in range(4)]

    dec = ret_decay[0].astype(F32)
    sb = _ret_bwd_states(dec, proj, cproj)
    y_att, y_ret, w_gate_b, w_up_b, w_out_b = _mid(
        attn_sink[0].astype(F32), dec, proj, kd, vd, ckd, cvd, sb, cproj,
        [(w_gate[0], 1, FFN_TILE), (w_up[0], 1, FFN_TILE), (w_out[0], 1, None)])

    x1, hff = _out_proj(y_ret, y_att, w_out_b, x2, gt_m, norm_ffn[0][None, :], sh_f, sc_f, tm=ROW_TILE)
    out = _ffn(hff, x1, gt_f, norm_final[None, :], w_gate_b, w_up_b, w_down[0], tm=FFN_ROW_TILE)
    return out[None]
```

```python
import jax
import jax.numpy as jnp
import numpy as np
from jax import lax
from jax.experimental import pallas as pl
from jax.experimental.pallas import tpu as pltpu

GRID_W = 64
RET_HEADS = 8
RET_DK = 64
RET_DV = 128
RET_CHUNK = 128
ATT_HEADS = 16
ATT_KV_HEADS = 4
ATT_DH = 64
ATT_GROUP = ATT_HEADS // ATT_KV_HEADS
WINDOW = 128
ATT_BLOCK = 128
ROPE_BASE = 10000.0
NORM_EPS = 1e-6
K_SCALE = RET_DK ** -0.5
ATT_SCALE = ATT_DH ** -0.5
LOG2E = 1.4426950408889634

RET_QK_COLS = RET_HEADS * RET_DK
RET_V_COLS = RET_HEADS * RET_DV
ATT_Q_COLS = ATT_HEADS * ATT_DH
KV_DUP_COLS = 2 * ATT_KV_HEADS * ATT_DH

LANES = 128
RET_PAIRS = RET_HEADS // 2
MASK_NEG = -1e30
VMEM_LIMIT = 56 * 1024 * 1024
CAST_PIECE_ROWS = 16
RET_STEP_CHUNKS = 8
ATT_STEP_BLOCKS = 4
ROW_TILE = 512
OUT_ROW_CHUNK = ROW_TILE
IN_ROW_CHUNK = ROW_TILE
FFN_TILE = 512
FFN_ROW_TILE = 1024
FFN_ROW_CHUNK = FFN_ROW_TILE

BF16 = jnp.bfloat16
F32 = jnp.float32


def _params(*sem):
    return pltpu.CompilerParams(dimension_semantics=sem, vmem_limit_bytes=VMEM_LIMIT)


def _with_cast_riders(body, n_in, n_out, n_rid):
    def wrapped(*refs):
        ins = refs[:n_in]
        rid_in = refs[n_in:n_in + n_rid]
        outs = refs[n_in + n_rid:n_in + n_rid + n_out]
        rid_out = refs[n_in + n_rid + n_out:n_in + 2 * n_rid + n_out]
        scratch = refs[n_in + 2 * n_rid + n_out:]

        def piece(src, dst, r0):
            rs = slice(r0, r0 + CAST_PIECE_ROWS)
            if len(dst.shape) == 2:
                dst[rs, :] = src[rs, :].astype(BF16)
            else:
                tc = dst.shape[2]
                for t in range(dst.shape[0]):
                    dst[t, rs, :] = src[rs, t * tc:(t + 1) * tc].astype(BF16)

        pieces = [(lambda s=src, d=dst, r=r0: piece(s, d, r))
                  for src, dst in zip(rid_in, rid_out) for r0 in range(0, src.shape[0], CAST_PIECE_ROWS)]
        done = []
        fillers = [(lambda p=p: (done.append(1), p())) for p in pieces]
        body(*ins, *outs, *scratch, fillers=fillers)
        assert len(done) == len(pieces), "every cast piece must be emitted exactly once"
    return wrapped


def _rider_specs(riders, steps):
    in_specs, out_specs, shapes = [], [], []
    for w, ncb, tile in riders:
        rows, cols = w.shape
        nrb = steps // ncb
        assert nrb * ncb == steps and rows % nrb == 0 and cols % ncb == 0
        br, bc = rows // nrb, cols // ncb
        assert br % CAST_PIECE_ROWS == 0 and bc % LANES == 0, "slab must be bf16-tile aligned"
        in_specs.append(pl.BlockSpec((br, bc), lambda i, ncb=ncb: (i // ncb, i % ncb)))
        if tile is None:
            out_specs.append(in_specs[-1])
            shapes.append(jax.ShapeDtypeStruct(w.shape, BF16))
        else:
            assert ncb == 1 and cols % tile == 0 and tile % LANES == 0
            out_specs.append(pl.BlockSpec((cols // tile, br, tile), lambda i: (0, i, 0)))
            shapes.append(jax.ShapeDtypeStruct((cols // tile, rows, tile), BF16))
    return in_specs, out_specs, shapes


def _mod_kernel(cv_ref, w_ref, b_ref, o_ref):
    cv = cv_ref[...]
    s = cv / (1.0 + jnp.exp(-cv))
    o_ref[...] = jnp.dot(s.astype(BF16), w_ref[...].astype(BF16),
                         preferred_element_type=F32) + b_ref[...]


def _mod(cv, w, b, n):
    d = w.shape[0]
    tn = 512
    assert n % tn == 0
    return pl.pallas_call(
        _mod_kernel,
        grid=(n // tn,),
        in_specs=[pl.BlockSpec((8, d), lambda j: (0, 0)),
                  pl.BlockSpec((d, tn), lambda j: (0, j)),
                  pl.BlockSpec((1, tn), lambda j: (0, j))],
        out_specs=pl.BlockSpec((8, tn), lambda j: (0, j)),
        out_shape=jax.ShapeDtypeStruct((8, n), F32),
        compiler_params=_params("parallel"),
        name="mod",
    )(cv, w, b)


def _rot_pairs(a, cos, sin_signed, half):
    lane = lax.broadcasted_iota(jnp.int32, a.shape, 1)
    first = (lane % (2 * half)) < half
    rot = jnp.where(first, pltpu.roll(a, LANES - half, 1), pltpu.roll(a, half, 1))
    return a * cos + rot * sin_signed


def _dup_halves(a):
    lane = lax.broadcasted_iota(jnp.int32, a.shape, 1)
    r = pltpu.roll(a, 64, 1)
    lo = lane < 64
    return jnp.where(lo, a, r), jnp.where(lo, r, a)


_PROJ_TILE = 512
_PROJ_TILE_KINDS = ("ret_q", "ret_k", "plain", "plain", "plain", "plain", "att_q", "att_q", "att_kv")


def _in_proj_kernel(x_ref, g_ref, sh_ref, sc_ref, w_ref, c1_ref, s1_ref, ca_ref, sa_ref,
                    cc_ref, wm_ref, bm_ref, o_ref, kd_ref, vd_ref, mod_ref):
    def mod_rider():
        cc = cc_ref[...]
        s_col = cc / (1.0 + jnp.exp(-cc))
        mod_ref[0] = jnp.sum(wm_ref[...] * s_col, axis=0, keepdims=True) + bm_ref[...]

    tn = _PROJ_TILE
    for r in range(x_ref.shape[0] // IN_ROW_CHUNK):
        rs = slice(r * IN_ROW_CHUNK, (r + 1) * IN_ROW_CHUNK)
        xf = x_ref[rs, :]
        y = xf * lax.rsqrt(jnp.mean(xf * xf, axis=-1, keepdims=True) + NORM_EPS)
        y = y * g_ref[...]
        h = (y * (1.0 + sc_ref[...]) + sh_ref[...]).astype(BF16)

        def rope1(a):
            return _rot_pairs(a, c1_ref[rs, :], s1_ref[rs, :], 32)

        def ropea(a):
            return _rot_pairs(a, ca_ref[rs, :], sa_ref[rs, :], 16)

        order = sorted(range(len(_PROJ_TILE_KINDS)), key=lambda t: _PROJ_TILE_KINDS[t] == "plain")
        for j in order:
            kind = _PROJ_TILE_KINDS[j]
            acc = jnp.dot(h, w_ref[:, j * tn:(j + 1) * tn], preferred_element_type=F32)
            if r == 0 and j == order[-2]:
                mod_rider()
            for c in range(tn // LANES):
                a = acc[:, c * LANES:(c + 1) * LANES]
                if kind == "ret_q":
                    a = rope1(a)
                elif kind == "ret_k":
                    a = rope1(a) * K_SCALE
                elif kind == "att_q":
                    a = ropea(a) * (ATT_SCALE * LOG2E)
                elif kind == "att_kv" and c < 2:
                    a = ropea(a)
                o_ref[rs, j * tn + c * LANES:j * tn + (c + 1) * LANES] = a.astype(BF16)
                if kind == "att_kv":
                    dup_ref = kd_ref if c < 2 else vd_ref
                    d0, d1 = _dup_halves(a)
                    t = 2 * (c % 2)
                    dup_ref[rs, t * LANES:(t + 1) * LANES] = d0.astype(BF16)
                    dup_ref[rs, (t + 1) * LANES:(t + 2) * LANES] = d1.astype(BF16)


def _in_proj(x, g, sh, sc, w, tabs, c_col, w_mod, b_mod, mod_done, *, tm):
    m, d = x.shape
    n = w.shape[1]
    assert n == _PROJ_TILE * len(_PROJ_TILE_KINDS) and m % tm == 0 and tm % IN_ROW_CHUNK == 0
    steps = m // tm
    slab = (w_mod.shape[1] - mod_done) // steps
    assert slab * steps == w_mod.shape[1] - mod_done and slab % LANES == 0 and mod_done % slab == 0
    slab0 = mod_done // slab
    c1, s1, ca, sa = tabs
    row = lambda i: (i, 0)
    vec = pl.BlockSpec((1, d), lambda i: (0, 0))
    tab = pl.BlockSpec((tm, LANES), row)
    return pl.pallas_call(
        _in_proj_kernel,
        grid=(m // tm,),
        in_specs=[pl.BlockSpec((tm, d), row), vec, vec, vec,
                  pl.BlockSpec((d, n), lambda i: (0, 0), pipeline_mode=pl.Buffered(1)),
                  tab, tab, tab, tab,
                  pl.BlockSpec((d, 1), lambda i: (0, 0)),
                  pl.BlockSpec((d, slab), lambda i: (0, slab0 + i)),
                  pl.BlockSpec((1, slab), lambda i: (0, slab0 + i))],
        out_specs=[pl.BlockSpec((tm, n), row),
                   pl.BlockSpec((tm, KV_DUP_COLS), row),
                   pl.BlockSpec((tm, KV_DUP_COLS), row),
                   pl.BlockSpec((1, 1, slab), lambda i: (i, 0, 0))],
        out_shape=[jax.ShapeDtypeStruct((m, n), BF16),
                   jax.ShapeDtypeStruct((m, KV_DUP_COLS), BF16),
                   jax.ShapeDtypeStruct((m, KV_DUP_COLS), BF16),
                   jax.ShapeDtypeStruct((steps, 1, slab), F32)],
        compiler_params=_params("parallel"),
        name="in_proj",
    )(x, g, sh, sc, w, c1, s1, ca, sa, c_col, w_mod, b_mod)


def _ctx_proj_kernel(x_ref, g_ref, sh_ref, sc_ref, w_ref, o_ref, kd_ref, vd_ref, wb_ref, h_ref):
    j = pl.program_id(0)

    @pl.when(j == 0)
    def _():
        xf = x_ref[...]
        y = xf * lax.rsqrt(jnp.mean(xf * xf, axis=-1, keepdims=True) + NORM_EPS)
        y = y * g_ref[...]
        h_ref[...] = (y * (1.0 + sc_ref[...]) + sh_ref[...]).astype(BF16)

    wb = w_ref[...].astype(BF16)
    wb_ref[...] = wb
    acc = jnp.dot(h_ref[...], wb, preferred_element_type=F32)
    is_ret_k = _PROJ_TILE_KINDS.index("ret_k")
    o_ref[...] = (acc * jnp.where(j == is_ret_k, K_SCALE, 1.0)).astype(BF16)

    @pl.when(j == _PROJ_TILE_KINDS.index("att_kv"))
    def _():
        for c in range(_PROJ_TILE // LANES):
            dup_ref = kd_ref if c < 2 else vd_ref
            d0, d1 = _dup_halves(acc[:, c * LANES:(c + 1) * LANES])
            t = 2 * (c % 2)
            dup_ref[:, t * LANES:(t + 1) * LANES] = d0.astype(BF16)
            dup_ref[:, (t + 1) * LANES:(t + 2) * LANES] = d1.astype(BF16)


def _ctx_proj(x, g, sh, sc, w):
    m, d = x.shape
    n = w.shape[1]
    tn = _PROJ_TILE
    assert n == tn * len(_PROJ_TILE_KINDS)
    fixed = lambda j: (0, 0)
    vec = pl.BlockSpec((1, d), fixed)
    return pl.pallas_call(
        _ctx_proj_kernel,
        grid=(n // tn,),
        in_specs=[pl.BlockSpec((m, d), fixed), vec, vec, vec,
                  pl.BlockSpec((d, tn), lambda j: (0, j))],
        out_specs=[pl.BlockSpec((m, tn), lambda j: (0, j)),
                   pl.BlockSpec((m, KV_DUP_COLS), fixed),
                   pl.BlockSpec((m, KV_DUP_COLS), fixed),
                   pl.BlockSpec((d, tn), lambda j: (0, j))],
        out_shape=[jax.ShapeDtypeStruct((m, n), BF16),
                   jax.ShapeDtypeStruct((m, KV_DUP_COLS), BF16),
                   jax.ShapeDtypeStruct((m, KV_DUP_COLS), BF16),
                   jax.ShapeDtypeStruct((d, n), BF16)],
        scratch_shapes=[pltpu.VMEM((m, d), BF16)],
        compiler_params=_params("arbitrary"),
        name="ctx_proj",
    )(x, g, sh, sc, w)


def _pair_lg(dec_ref, d, p, shape):
    lane = lax.broadcasted_iota(jnp.int32, shape, 1)
    first = (lane % LANES) < 64
    raw = jnp.where(first, jnp.full(shape, dec_ref[d, 2 * p], F32), jnp.full(shape, dec_ref[d, 2 * p + 1], F32))
    return -jnp.exp(raw)


def _head_block_mask(shape):
    r = lax.broadcasted_iota(jnp.int32, shape, 0)
    c = lax.broadcasted_iota(jnp.int32, shape, 1)
    return (r // 64) == (c // LANES)


def _kv_pair(k_pair, v_pair, w):
    kw = (k_pair.astype(F32) * w).astype(BF16)
    kv = lax.dot_general(kw, v_pair, (((0,), (0,)), ((), ())), preferred_element_type=F32)
    return jnp.where(_head_block_mask(kv.shape), kv, 0.0)


def _row_decay(dec_ref, d, p):
    shape = (LANES, 2 * RET_DV)
    rowh = lax.broadcasted_iota(jnp.int32, shape, 0) < 64
    raw = jnp.where(rowh, jnp.full(shape, dec_ref[d, 2 * p], F32), jnp.full(shape, dec_ref[d, 2 * p + 1], F32))
    return jnp.exp(-jnp.exp(raw) * float(RET_CHUNK))


def _compact_state(s):
    row = lax.broadcasted_iota(jnp.int32, (LANES, RET_DV), 0)
    return jnp.where(row < 64, s[:, :RET_DV], s[:, RET_DV:])


def _expand_state(c):
    row = lax.broadcasted_iota(jnp.int32, c.shape, 0)
    z = jnp.zeros_like(c)
    return jnp.concatenate([jnp.where(row < 64, c, z), jnp.where(row < 64, z, c)], axis=1)


def _ret_bwd_kernel(dec_ref, k_ref, v_ref, ck_ref, cv_ref, sb_ref, sbs):
    i = pl.program_id(0)
    C = RET_CHUNK
    lc = ck_ref.shape[0]

    @pl.when(i == 0)
    def _():
        pos = lax.broadcasted_iota(jnp.int32, (lc, LANES), 0).astype(F32)
        for p in range(RET_PAIRS):
            ks = slice(p * LANES, (p + 1) * LANES)
            vs = slice(p * 2 * RET_DV, (p + 1) * 2 * RET_DV)
            wb = jnp.exp(_pair_lg(dec_ref, 1, p, (lc, LANES)) * pos)
            sbs[p] = _kv_pair(ck_ref[:, ks], cv_ref[:, vs], wb)

    pos = lax.broadcasted_iota(jnp.int32, (C, LANES), 0).astype(F32)
    for p in range(RET_PAIRS):
        ks = slice(p * LANES, (p + 1) * LANES)
        vs = slice(p * 2 * RET_DV, (p + 1) * 2 * RET_DV)
        wb = jnp.exp(_pair_lg(dec_ref, 1, p, (C, LANES)) * pos)
        gb = _row_decay(dec_ref, 1, p)
        sb = sbs[p]
        for cc in reversed(range(RET_STEP_CHUNKS)):
            rs = slice(cc * C, (cc + 1) * C)
            sb_ref[cc, p] = _compact_state(sb).astype(BF16)
            sb = gb * sb + _kv_pair(k_ref[rs, ks], v_ref[rs, vs], wb)
        sbs[p] = sb


def _ret_bwd_states(dec, proj, cproj):
    L = proj.shape[0]
    lc = cproj.shape[0]
    S = RET_STEP_CHUNKS
    R = S * RET_CHUNK
    n = L // R
    return pl.pallas_call(
        _ret_bwd_kernel,
        grid=(n,),
        in_specs=[pl.BlockSpec(memory_space=pltpu.SMEM),
                  pl.BlockSpec((R, RET_QK_COLS), lambda i: (n - 1 - i, 1)),
                  pl.BlockSpec((R, RET_V_COLS), lambda i: (n - 1 - i, 1)),
                  pl.BlockSpec((lc, RET_QK_COLS), lambda i: (0, 1)),
                  pl.BlockSpec((lc, RET_V_COLS), lambda i: (0, 1))],
        out_specs=pl.BlockSpec((S, RET_PAIRS, LANES, RET_DV), lambda i: (n - 1 - i, 0, 0, 0)),
        out_shape=jax.ShapeDtypeStruct((n * S, RET_PAIRS, LANES, RET_DV), BF16),
        scratch_shapes=[pltpu.VMEM((RET_PAIRS, LANES, 2 * RET_DV), F32)],
        compiler_params=_params("arbitrary"),
        name="ret_bwd",
    )(dec, proj, proj, cproj, cproj)


def _ret_out_pieces(dec_ref, q_ref, k_ref, v_ref, g_ref, sb_ref, ck_ref, cv_ref, o_ref, sfs):
    i = pl.program_id(0)
    C = RET_CHUNK
    lc = ck_ref.shape[0]
    n_chunks = q_ref.shape[0] // C

    @pl.when(i == 0)
    def _():
        cpos = lax.broadcasted_iota(jnp.int32, (lc, LANES), 0).astype(F32)
        for p in range(RET_PAIRS):
            ks = slice(p * LANES, (p + 1) * LANES)
            vs = slice(p * 2 * RET_DV, (p + 1) * 2 * RET_DV)
            wf = jnp.exp(_pair_lg(dec_ref, 0, p, (lc, LANES)) * (lc - 1.0 - cpos))
            sfs[p] = _kv_pair(ck_ref[:, ks], cv_ref[:, vs], wf)

    pos = lax.broadcasted_iota(jnp.int32, (C, LANES), 0).astype(F32)
    n_i = lax.broadcasted_iota(jnp.int32, (C, 2 * C), 0)
    m_i = lax.broadcasted_iota(jnp.int32, (C, 2 * C), 1) % C
    rel = (n_i - m_i).astype(F32)
    lane = lax.broadcasted_iota(jnp.int32, (C, LANES), 1)
    lo = lane < 64

    def pair_tables(p):
        col_a = lax.broadcasted_iota(jnp.int32, (C, 2 * C), 1) < C
        raw_f = jnp.where(col_a, jnp.full((C, 2 * C), dec_ref[0, 2 * p], F32), jnp.full((C, 2 * C), dec_ref[0, 2 * p + 1], F32))
        raw_b = jnp.where(col_a, jnp.full((C, 2 * C), dec_ref[1, 2 * p], F32), jnp.full((C, 2 * C), dec_ref[1, 2 * p + 1], F32))
        dmat = jnp.where(rel >= 0, jnp.exp(-jnp.exp(raw_f) * jnp.maximum(rel, 0.0)),
                         jnp.exp(-jnp.exp(raw_b) * jnp.maximum(-rel, 0.0)))
        lg_f = _pair_lg(dec_ref, 0, p, (C, LANES))
        return dict(dmat=dmat, wqf=jnp.exp(lg_f * (pos + 1.0)),
                    wqb=jnp.exp(_pair_lg(dec_ref, 1, p, (C, LANES)) * (float(C) - pos)),
                    wkf=jnp.exp(lg_f * (C - 1.0 - pos)), gf=_row_decay(dec_ref, 0, p), sf=sfs[p])

    def chunk(p, cc, st):
        if cc == 0:
            st.update(pair_tables(p))
        ks = slice(p * LANES, (p + 1) * LANES)
        vs = slice(p * 2 * RET_DV, (p + 1) * 2 * RET_DV)
        rs = slice(cc * C, (cc + 1) * C)
        q = q_ref[rs, ks]
        k = k_ref[rs, ks]
        v = v_ref[rs, vs]
        zk = jnp.zeros_like(k)
        kst = jnp.concatenate([jnp.where(lo, k, zk), jnp.where(lo, zk, k)], axis=0)
        s = lax.dot_general(q, kst, (((1,), (1,)), ((), ())), preferred_element_type=F32)
        sd = (s * st["dmat"]).astype(BF16)
        qf32 = q.astype(F32)
        qwf = (qf32 * st["wqf"]).astype(BF16)
        qwb = (qf32 * st["wqb"]).astype(BF16)
        zv = jnp.zeros((C, RET_DV), BF16)
        vbd = jnp.concatenate([jnp.concatenate([v[:, :RET_DV], zv], axis=1),
                               jnp.concatenate([zv, v[:, RET_DV:]], axis=1)], axis=0)
        lhs = jnp.concatenate([sd, qwf, qwb], axis=1)
        st["mid"] = (lhs, vbd, k, v)

    def chunk_out(p, cc, st):
        lhs, vbd, k, v = st.pop("mid")
        rs = slice(cc * C, (cc + 1) * C)
        rhs = jnp.concatenate([vbd, st["sf"].astype(BF16), _expand_state(sb_ref[cc, p])], axis=0)
        o = jnp.dot(lhs, rhs, preferred_element_type=F32)
        st["sf"] = st["gf"] * st["sf"] + _kv_pair(k, v, st["wkf"])
        for t in range(2):
            oh = o[:, t * RET_DV:(t + 1) * RET_DV]
            oh = oh * lax.rsqrt(jnp.mean(oh * oh, axis=-1, keepdims=True) + NORM_EPS)
            cs = slice(p * 2 * RET_DV + t * RET_DV, p * 2 * RET_DV + (t + 1) * RET_DV)
            gt = g_ref[rs, cs].astype(F32)
            o_ref[rs, cs] = (oh * (gt / (1.0 + jnp.exp(-gt)))).astype(BF16)
        if cc == n_chunks - 1:
            sfs[p] = st["sf"]

    pieces = []
    for p in range(RET_PAIRS):
        st = {}
        pieces += [((lambda p=p, cc=cc, st=st: chunk(p, cc, st)), (lambda p=p, cc=cc, st=st: chunk_out(p, cc, st)))
                   for cc in range(n_chunks)]
    return pieces


def _attn_kernel(sink_ref, q_ref, kp_ref, kc_ref, kn_ref, vp_ref, vc_ref, vn_ref, ck_ref, cv_ref, o_ref, *,
                 fillers, pre=None, post=None):
    n = pl.program_id(0)
    nstep = pl.num_programs(0)
    B = ATT_BLOCK
    SB = ATT_STEP_BLOCKS
    kj = lax.broadcasted_iota(jnp.int32, (B, B), 0)
    qi = lax.broadcasted_iota(jnp.int32, (B, B), 1)
    ok_prev = jnp.where(n > 0, 0.0, MASK_NEG).astype(F32)
    ok_next = jnp.where(n < nstep - 1, 0.0, MASK_NEG).astype(F32)

    def band(inside, ok):
        return jnp.concatenate([jnp.where(inside, ok, MASK_NEG).astype(F32)] * ATT_GROUP, axis=1)

    bias_prev = [band(kj >= qi, ok_prev if j == 0 else 0.0) for j in range(SB)]
    bias_next = [band(kj <= qi, ok_next if j == SB - 1 else 0.0) for j in range(SB)]
    lane = lax.broadcasted_iota(jnp.int32, (B, LANES), 1)
    lo = lane < 64
    hi = lane >= 64

    def keys_of(j, gs, prev_ref, cur_ref, next_ref, ctx_ref):
        prev = prev_ref[:, gs] if j == 0 else cur_ref[(j - 1) * B:j * B, gs]
        nxt = next_ref[:, gs] if j == SB - 1 else cur_ref[(j + 1) * B:(j + 2) * B, gs]
        return jnp.concatenate([prev, cur_ref[j * B:(j + 1) * B, gs], nxt, ctx_ref[:, gs]], axis=0)

    def scores(j, g):
        gs = slice(g * LANES, (g + 1) * LANES)
        kcat = keys_of(j, gs, kp_ref, kc_ref, kn_ref, ck_ref)
        qs = []
        for r in range(ATT_GROUP):
            h = ATT_GROUP * g + r
            qt = q_ref[j * B:(j + 1) * B, (h // 2) * LANES:(h // 2 + 1) * LANES]
            keep = lo if h % 2 == 0 else hi
            qs.append(jnp.where(keep, qt, jnp.zeros_like(qt)))
        q4 = jnp.concatenate(qs, axis=0)
        return lax.dot_general(kcat, q4, (((1,), (1,)), ((), ())), preferred_element_type=F32)

    def softmax(j, g, s):
        sk = jnp.concatenate([jnp.full((1, B), sink_ref[ATT_GROUP * g + r], F32)
                              for r in range(ATT_GROUP)], axis=1) * LOG2E
        s = jnp.concatenate([s[:B] + bias_prev[j], s[B:2 * B], s[2 * B:3 * B] + bias_next[j], s[3 * B:]], axis=0)
        m = jnp.maximum(jnp.max(s, axis=0, keepdims=True), sk)
        e = jnp.exp2(s - m)
        den = jnp.sum(e, axis=0, keepdims=True) + jnp.exp2(sk - m)
        return e.astype(BF16), den

    def values(j, g, e, den):
        gs = slice(g * LANES, (g + 1) * LANES)
        vcat = keys_of(j, gs, vp_ref, vc_ref, vn_ref, cv_ref)
        res = lax.dot_general(vcat, e, (((0,), (0,)), ((), ())), preferred_element_type=F32) * (1.0 / den)
        for t in range(2):
            even = res[:, (2 * t) * B:(2 * t + 1) * B].T
            odd = res[:, (2 * t + 1) * B:(2 * t + 2) * B].T
            c0 = (2 * g + t) * LANES
            o_ref[j * B:(j + 1) * B, c0:c0 + LANES] = jnp.where(lo, even, odd).astype(BF16)

    units = [(j, g) for j in range(SB) for g in range(ATT_KV_HEADS)]
    per_unit = -(-len(fillers) // len(units))
    s_next = scores(*units[0])
    pending = None
    for u, unit in enumerate(units):
        s_cur = s_next
        if u + 1 < len(units):
            s_next = scores(*units[u + 1])
        if pre is not None:
            pre[u]()
        e_den = softmax(*unit, s_cur)
        for fill in fillers[u * per_unit:(u + 1) * per_unit]:
            fill()
        if pending is not None:
            values(*units[u - 1], *pending)
        if post is not None:
            post[u]()
        pending = e_den
    values(*units[-1], *pending)


def _mid_kernel(*refs, fillers):
    att_in, ret_in = refs[:10], refs[10:18]
    o_att, o_ret, sfs = refs[18:]
    ret = _ret_out_pieces(*ret_in, o_ret, sfs)
    n_units = ATT_STEP_BLOCKS * ATT_KV_HEADS
    assert len(ret) == n_units
    _attn_kernel(*att_in, o_att, fillers=fillers, post=[(lambda a=a, b=b: (a(), b())) for a, b in ret])


def _mid(sink, dec, proj, kd, vd, ckd, cvd, sb, cproj, riders):
    L = proj.shape[0]
    B = ATT_BLOCK
    SB = ATT_STEP_BLOCKS
    n = L // (SB * B)
    nb = L // B
    lc = ckd.shape[0]
    prev = pl.BlockSpec((B, KV_DUP_COLS), lambda i: (jnp.maximum(i * SB - 1, 0), 0))
    cur = pl.BlockSpec((SB * B, KV_DUP_COLS), lambda i: (i, 0))
    nxt = pl.BlockSpec((B, KV_DUP_COLS), lambda i: (jnp.minimum((i + 1) * SB, nb - 1), 0))
    full = pl.BlockSpec((lc, KV_DUP_COLS), lambda i: (0, 0))
    rid_in_specs, rid_out_specs, rid_shapes = _rider_specs(riders, n)
    R = SB * B
    S = R // RET_CHUNK
    lcr = cproj.shape[0]
    return pl.pallas_call(
        _with_cast_riders(_mid_kernel, 18, 2, len(riders)),
        grid=(n,),
        in_specs=[pl.BlockSpec(memory_space=pltpu.SMEM),
                  pl.BlockSpec((R, ATT_Q_COLS), lambda i: (i, 3)),
                  prev, cur, nxt, prev, cur, nxt, full, full,
                  pl.BlockSpec(memory_space=pltpu.SMEM),
                  pl.BlockSpec((R, RET_QK_COLS), lambda i: (i, 0)),
                  pl.BlockSpec((R, RET_QK_COLS), lambda i: (i, 1)),
                  pl.BlockSpec((R, RET_V_COLS), lambda i: (i, 1)),
                  pl.BlockSpec((R, RET_V_COLS), lambda i: (i, 2)),
                  pl.BlockSpec((S, RET_PAIRS, LANES, RET_DV), lambda i: (i, 0, 0, 0)),
                  pl.BlockSpec((lcr, RET_QK_COLS), lambda i: (0, 1)),
                  pl.BlockSpec((lcr, RET_V_COLS), lambda i: (0, 1))] + rid_in_specs,
        out_specs=[pl.BlockSpec((R, ATT_HEADS * ATT_DH), lambda i: (i, 0)),
                   pl.BlockSpec((R, RET_HEADS * RET_DV), lambda i: (i, 0))] + rid_out_specs,
        out_shape=[jax.ShapeDtypeStruct((L, ATT_HEADS * ATT_DH), BF16),
                   jax.ShapeDtypeStruct((L, RET_HEADS * RET_DV), BF16)] + rid_shapes,
        scratch_shapes=[pltpu.VMEM((RET_PAIRS, LANES, 2 * RET_DV), F32)],
        compiler_params=_params("arbitrary"),
        name="attn",
    )(sink, proj, kd, kd, kd, vd, vd, vd, ckd, cvd, dec, proj, proj, proj, proj, sb, cproj, cproj,
      *[r[0] for r in riders])


def _out_proj_kernel(yr_ref, ya_ref, w_ref, x_ref, gt_ref, g_ref, sh_ref, sc_ref, o_ref, h_ref):
    kr = yr_ref.shape[1]
    for r in range(yr_ref.shape[0] // OUT_ROW_CHUNK):
        rs = slice(r * OUT_ROW_CHUNK, (r + 1) * OUT_ROW_CHUNK)
        acc = jnp.dot(yr_ref[rs, :], w_ref[:kr, :], preferred_element_type=F32)
        acc = acc + jnp.dot(ya_ref[rs, :], w_ref[kr:, :], preferred_element_type=F32)
        x1 = x_ref[rs, :] + gt_ref[...] * acc
        o_ref[rs, :] = x1
        y = x1 * lax.rsqrt(jnp.mean(x1 * x1, axis=-1, keepdims=True) + NORM_EPS)
        y = y * g_ref[...]
        h_ref[rs, :] = (y * (1.0 + sc_ref[...]) + sh_ref[...]).astype(BF16)


def _out_proj(yr, ya, w, x, gt, g, sh, sc, *, tm):
    m, d = x.shape
    kr, ka = yr.shape[1], ya.shape[1]
    row = lambda i: (i, 0)
    vec = pl.BlockSpec((1, d), lambda i: (0, 0))
    return pl.pallas_call(
        _out_proj_kernel,
        grid=(m // tm,),
        in_specs=[pl.BlockSpec((tm, kr), row), pl.BlockSpec((tm, ka), row),
                  pl.BlockSpec((kr + ka, d), lambda i: (0, 0)),
                  pl.BlockSpec((tm, d), row), vec, vec, vec, vec],
        out_specs=[pl.BlockSpec((tm, d), row), pl.BlockSpec((tm, d), row)],
        out_shape=[jax.ShapeDtypeStruct((m, d), F32), jax.ShapeDtypeStruct((m, d), BF16)],
        compiler_params=_params("parallel"),
        name="out_proj",
    )(yr, ya, w, x, gt, g, sh, sc)


def _ffn_kernel(h_ref, gt_ref, gfin_ref, wg_ref, wu_ref, wd_ref, x_hbm, o_ref, x_buf, sem):
    i = pl.program_id(0)
    f = pl.program_id(1)
    last = pl.num_programs(1) - 1
    rows = o_ref.shape[0]
    x_copy = pltpu.make_async_copy(x_hbm.at[pl.ds(pl.multiple_of(i * rows, rows), rows), :], x_buf, sem.at[0])

    def step(first, final):
        wd = wd_ref[...].astype(BF16)
        for r in range(rows // FFN_ROW_CHUNK):
            rs = slice(r * FFN_ROW_CHUNK, (r + 1) * FFN_ROW_CHUNK)
            h = h_ref[rs, :]
            a = jnp.dot(h, wg_ref[0], preferred_element_type=F32)
            u = jnp.dot(h, wu_ref[0], preferred_element_type=F32)
            act = ((a / (1.0 + jnp.exp(-a))) * u).astype(BF16)
            part = jnp.dot(act, wd, preferred_element_type=F32)
            if first:
                o_ref[rs, :] = part
            elif not final:
                o_ref[rs, :] += part
            else:
                y = x_buf[rs, :] + gt_ref[...] * (o_ref[rs, :] + part)
                y = y * lax.rsqrt(jnp.mean(y * y, axis=-1, keepdims=True) + NORM_EPS)
                o_ref[rs, :] = y * gfin_ref[...]

    @pl.when(f == 0)
    def _():
        x_copy.start()
        step(first=True, final=False)

    @pl.when((f > 0) & (f < last))
    def _():
        step(first=False, final=False)

    @pl.when(f == last)
    def _():
        x_copy.wait()
        step(first=False, final=True)


def _ffn(h, x, gt, gfin, wg, wu, wd, *, tm):
    m, d = x.shape
    nf = wg.shape[0]
    assert wg.shape == wu.shape == (nf, d, FFN_TILE) and wd.shape == (nf * FFN_TILE, d)
    assert m % tm == 0 and tm % FFN_ROW_CHUNK == 0
    row = lambda i, f: (i, 0)
    vec = pl.BlockSpec((1, d), lambda i, f: (0, 0))
    wcol = pl.BlockSpec((1, d, FFN_TILE), lambda i, f: (f, 0, 0))
    return pl.pallas_call(
        _ffn_kernel,
        grid=(m // tm, nf),
        in_specs=[pl.BlockSpec((tm, d), row), vec, vec, wcol, wcol,
                  pl.BlockSpec((FFN_TILE, d), lambda i, f: (f, 0)),
                  pl.BlockSpec(memory_space=pl.ANY)],
        out_specs=pl.BlockSpec((tm, d), row),
        out_shape=jax.ShapeDtypeStruct((m, d), F32),
        scratch_shapes=[pltpu.VMEM((tm, d), F32), pltpu.SemaphoreType.DMA((1,))],
        compiler_params=_params("arbitrary", "arbitrary"),
        name="ffn",
    )(h, gt, gfin, wg, wu, wd, x)


def _rope_tables(L):
    f32 = np.float32
    lane = np.arange(LANES)
    inv1 = f32(ROPE_BASE) ** (-np.arange(32, dtype=f32) / f32(32))
    ang1 = np.arange(L, dtype=f32)[:, None] * inv1[None, :]
    sgn1 = np.where((lane % 64) < 32, -1.0, 1.0).astype(f32)
    cos1 = np.tile(np.cos(ang1), (1, LANES // 32))
    sin1 = np.tile(np.sin(ang1), (1, LANES // 32)) * sgn1[None, :]
    inv2 = f32(ROPE_BASE) ** (-np.arange(16, dtype=f32) / f32(16))
    nrow = L // GRID_W
    ang_r = np.arange(nrow, dtype=f32)[:, None] * inv2[None, :]
    ang_c = np.arange(GRID_W, dtype=f32)[:, None] * inv2[None, :]
    sgna = np.where((lane % 32) < 16, -1.0, 1.0).astype(f32)

    def expand(fr, fc):
        by_row = np.broadcast_to(np.tile(fr, (1, 2))[:, None, :], (nrow, GRID_W, 32))
        by_col = np.broadcast_to(np.tile(fc, (1, 2))[None, :, :], (nrow, GRID_W, 32))
        head = np.concatenate([by_row, by_col], axis=-1).reshape(L, 64)
        return np.tile(head, (1, LANES // 64))

    cosa = expand(np.cos(ang_r), np.cos(ang_c))
    sina = expand(np.sin(ang_r), np.sin(ang_c)) * sgna[None, :]
    return tuple(np.ascontiguousarray(t, dtype=f32) for t in (cos1, sin1, cosa, sina))


def kernel(x, c, ctx, c_ctx, w_mod, b_mod, norm_mix, norm_ffn, w_in, ret_decay, attn_sink,
           w_out, w_gate, w_up, w_down, norm_final):
    B, L, D = x.shape
    assert B == 1 and w_mod.shape[0] == 1, "single batch element, depth-1 layer"
    x2 = x[0]
    xc2 = ctx[0]

    cv = jnp.zeros((8, D), F32).at[0].set(c[0]).at[1].set(c_ctx)
    mod = _mod(cv, w_mod[0], b_mod[0][None, :], 2 * D)
    sh_m, sc_m = mod[0:1, 0:D], mod[0:1, D:2 * D]
    sh_mc, sc_mc = mod[1:2, 0:D], mod[1:2, D:2 * D]

    g_mix = norm_mix[0][None, :]
    cproj, ckd, cvd, w_in_b = _ctx_proj(xc2, g_mix, sh_mc, sc_mc, w_in[0])
    proj, kd, vd, mod_rest = _in_proj(x2, g_mix, sh_m, sc_m, w_in_b, _rope_tables(L),
                                      c[0][:, None], w_mod[0], b_mod[0][None, :], 2 * D, tm=ROW_TILE)
    gt_m, sh_f, sc_f, gt_f = [mod_rest.reshape(1, 4 * D)[:, k * D:(k + 1) * D] for k in range(4)]

    dec = ret_decay[0].astype(F32)
    sb = _ret_bwd_states(dec, proj, cproj)
    y_att, y_ret, w_gate_b, w_up_b, w_out_b = _mid(
        attn_sink[0].astype(F32), dec, proj, kd, vd, ckd, cvd, sb, cproj,
        [(w_gate[0], 1, FFN_TILE), (w_up[0], 1, FFN_TILE), (w_out[0], 1, None)])

    x1, hff = _out_proj(y_ret, y_att, w_out_b, x2, gt_m, norm_ffn[0][None, :], sh_f, sc_f, tm=ROW_TILE)
    out = _ffn(hff, x1, gt_f, norm_final[None, :], w_gate_b, w_up_b, w_down[0], tm=FFN_ROW_TILE)
    return out[None]
```

```python
import jax
import jax.numpy as jnp
import numpy as np
from jax import lax
from jax.experimental import pallas as pl
from jax.experimental.pallas import tpu as pltpu

GRID_W = 64
RET_HEADS = 8
RET_DK = 64
RET_DV = 128
RET_CHUNK = 128
ATT_HEADS = 16
ATT_KV_HEADS = 4
ATT_DH = 64
ATT_GROUP = ATT_HEADS // ATT_KV_HEADS
WINDOW = 128
ATT_BLOCK = 128
ROPE_BASE = 10000.0
NORM_EPS = 1e-6
K_SCALE = RET_DK ** -0.5
ATT_SCALE = ATT_DH ** -0.5
LOG2E = 1.4426950408889634

RET_QK_COLS = RET_HEADS * RET_DK
RET_V_COLS = RET_HEADS * RET_DV
ATT_Q_COLS = ATT_HEADS * ATT_DH
KV_DUP_COLS = 2 * ATT_KV_HEADS * ATT_DH

LANES = 128
RET_PAIRS = RET_HEADS // 2
MASK_NEG = -1e30
VMEM_LIMIT = 56 * 1024 * 1024
CAST_PIECE_ROWS = 16
RET_STEP_CHUNKS = 4
ATT_STEP_BLOCKS = 4
ROW_TILE = 512
OUT_ROW_CHUNK = ROW_TILE
IN_ROW_CHUNK = ROW_TILE
FFN_TILE = 512
FFN_ROW_TILE = 1024
FFN_ROW_CHUNK = FFN_ROW_TILE

BF16 = jnp.bfloat16
F32 = jnp.float32


def _params(*sem):
    return pltpu.CompilerParams(dimension_semantics=sem, vmem_limit_bytes=VMEM_LIMIT)


def _with_cast_riders(body, n_in, n_out, n_rid):
    def wrapped(*refs):
        ins = refs[:n_in]
        rid_in = refs[n_in:n_in + n_rid]
        outs = refs[n_in + n_rid:n_in + n_rid + n_out]
        rid_out = refs[n_in + n_rid + n_out:n_in + 2 * n_rid + n_out]
        scratch = refs[n_in + 2 * n_rid + n_out:]

        def piece(src, dst, r0):
            rs = slice(r0, r0 + CAST_PIECE_ROWS)
            if len(dst.shape) == 2:
                dst[rs, :] = src[rs, :].astype(BF16)
            else:
                tc = dst.shape[2]
                for t in range(dst.shape[0]):
                    dst[t, rs, :] = src[rs, t * tc:(t + 1) * tc].astype(BF16)

        pieces = [(lambda s=src, d=dst, r=r0: piece(s, d, r))
                  for src, dst in zip(rid_in, rid_out) for r0 in range(0, src.shape[0], CAST_PIECE_ROWS)]
        done = []
        fillers = [(lambda p=p: (done.append(1), p())) for p in pieces]
        body(*ins, *outs, *scratch, fillers=fillers)
        assert len(done) == len(pieces), "every cast piece must be emitted exactly once"
    return wrapped


def _rider_specs(riders, steps):
    in_specs, out_specs, shapes = [], [], []
    for w, ncb, tile in riders:
        rows, cols = w.shape
        nrb = steps // ncb
        assert nrb * ncb == steps and rows % nrb == 0 and cols % ncb == 0
        br, bc = rows // nrb, cols // ncb
        assert br % CAST_PIECE_ROWS == 0 and bc % LANES == 0, "slab must be bf16-tile aligned"
        in_specs.append(pl.BlockSpec((br, bc), lambda i, ncb=ncb: (i // ncb, i % ncb)))
        if tile is None:
            out_specs.append(in_specs[-1])
            shapes.append(jax.ShapeDtypeStruct(w.shape, BF16))
        else:
            assert ncb == 1 and cols % tile == 0 and tile % LANES == 0
            out_specs.append(pl.BlockSpec((cols // tile, br, tile), lambda i: (0, i, 0)))
            shapes.append(jax.ShapeDtypeStruct((cols // tile, rows, tile), BF16))
    return in_specs, out_specs, shapes


def _mod_kernel(cv_ref, w_ref, b_ref, o_ref):
    cv = cv_ref[...]
    s = cv / (1.0 + jnp.exp(-cv))
    o_ref[...] = jnp.dot(s.astype(BF16), w_ref[...].astype(BF16),
                         preferred_element_type=F32) + b_ref[...]


def _mod(cv, w, b, n):
    d = w.shape[0]
    tn = 512
    assert n % tn == 0
    return pl.pallas_call(
        _mod_kernel,
        grid=(n // tn,),
        in_specs=[pl.BlockSpec((8, d), lambda j: (0, 0)),
                  pl.BlockSpec((d, tn), lambda j: (0, j)),
                  pl.BlockSpec((1, tn), lambda j: (0, j))],
        out_specs=pl.BlockSpec((8, tn), lambda j: (0, j)),
        out_shape=jax.ShapeDtypeStruct((8, n), F32),
        compiler_params=_params("parallel"),
        name="mod",
    )(cv, w, b)


def _rot_pairs(a, cos, sin_signed, half):
    lane = lax.broadcasted_iota(jnp.int32, a.shape, 1)
    first = (lane % (2 * half)) < half
    rot = jnp.where(first, pltpu.roll(a, LANES - half, 1), pltpu.roll(a, half, 1))
    return a * cos + rot * sin_signed


def _dup_halves(a):
    lane = lax.broadcasted_iota(jnp.int32, a.shape, 1)
    r = pltpu.roll(a, 64, 1)
    lo = lane < 64
    return jnp.where(lo, a, r), jnp.where(lo, r, a)


_PROJ_TILE = 512
_PROJ_TILE_KINDS = ("ret_q", "ret_k", "plain", "plain", "plain", "plain", "att_q", "att_q", "att_kv")


def _in_proj_kernel(x_ref, g_ref, sh_ref, sc_ref, w_ref, c1_ref, s1_ref, ca_ref, sa_ref,
                    cc_ref, wm_ref, bm_ref, o_ref, kd_ref, vd_ref, mod_ref):
    def mod_rider():
        cc = cc_ref[...]
        s_col = cc / (1.0 + jnp.exp(-cc))
        mod_ref[0] = jnp.sum(wm_ref[...] * s_col, axis=0, keepdims=True) + bm_ref[...]

    tn = _PROJ_TILE
    for r in range(x_ref.shape[0] // IN_ROW_CHUNK):
        rs = slice(r * IN_ROW_CHUNK, (r + 1) * IN_ROW_CHUNK)
        xf = x_ref[rs, :]
        y = xf * lax.rsqrt(jnp.mean(xf * xf, axis=-1, keepdims=True) + NORM_EPS)
        y = y * g_ref[...]
        h = (y * (1.0 + sc_ref[...]) + sh_ref[...]).astype(BF16)

        def rope1(a):
            return _rot_pairs(a, c1_ref[rs, :], s1_ref[rs, :], 32)

        def ropea(a):
            return _rot_pairs(a, ca_ref[rs, :], sa_ref[rs, :], 16)

        order = sorted(range(len(_PROJ_TILE_KINDS)), key=lambda t: _PROJ_TILE_KINDS[t] == "plain")
        for j in order:
            kind = _PROJ_TILE_KINDS[j]
            acc = jnp.dot(h, w_ref[:, j * tn:(j + 1) * tn], preferred_element_type=F32)
            if r == 0 and j == order[-2]:
                mod_rider()
            for c in range(tn // LANES):
                a = acc[:, c * LANES:(c + 1) * LANES]
                if kind == "ret_q":
                    a = rope1(a)
                elif kind == "ret_k":
                    a = rope1(a) * K_SCALE
                elif kind == "att_q":
                    a = ropea(a) * (ATT_SCALE * LOG2E)
                elif kind == "att_kv" and c < 2:
                    a = ropea(a)
                o_ref[rs, j * tn + c * LANES:j * tn + (c + 1) * LANES] = a.astype(BF16)
                if kind == "att_kv":
                    dup_ref = kd_ref if c < 2 else vd_ref
                    d0, d1 = _dup_halves(a)
                    t = 2 * (c % 2)
                    dup_ref[rs, t * LANES:(t + 1) * LANES] = d0.astype(BF16)
                    dup_ref[rs, (t + 1) * LANES:(t + 2) * LANES] = d1.astype(BF16)


def _in_proj(x, g, sh, sc, w, tabs, c_col, w_mod, b_mod, mod_done, *, tm):
    m, d = x.shape
    n = w.shape[1]
    assert n == _PROJ_TILE * len(_PROJ_TILE_KINDS) and m % tm == 0 and tm % IN_ROW_CHUNK == 0
    steps = m // tm
    slab = (w_mod.shape[1] - mod_done) // steps
    assert slab * steps == w_mod.shape[1] - mod_done and slab % LANES == 0 and mod_done % slab == 0
    slab0 = mod_done // slab
    c1, s1, ca, sa = tabs
    row = lambda i: (i, 0)
    vec = pl.BlockSpec((1, d), lambda i: (0, 0))
    tab = pl.BlockSpec((tm, LANES), row)
    return pl.pallas_call(
        _in_proj_kernel,
        grid=(m // tm,),
        in_specs=[pl.BlockSpec((tm, d), row), vec, vec, vec,
                  pl.BlockSpec((d, n), lambda i: (0, 0), pipeline_mode=pl.Buffered(1)),
                  tab, tab, tab, tab,
                  pl.BlockSpec((d, 1), lambda i: (0, 0)),
                  pl.BlockSpec((d, slab), lambda i: (0, slab0 + i)),
                  pl.BlockSpec((1, slab), lambda i: (0, slab0 + i))],
        out_specs=[pl.BlockSpec((tm, n), row),
                   pl.BlockSpec((tm, KV_DUP_COLS), row),
                   pl.BlockSpec((tm, KV_DUP_COLS), row),
                   pl.BlockSpec((1, 1, slab), lambda i: (i, 0, 0))],
        out_shape=[jax.ShapeDtypeStruct((m, n), BF16),
                   jax.ShapeDtypeStruct((m, KV_DUP_COLS), BF16),
                   jax.ShapeDtypeStruct((m, KV_DUP_COLS), BF16),
                   jax.ShapeDtypeStruct((steps, 1, slab), F32)],
        compiler_params=_params("parallel"),
        name="in_proj",
    )(x, g, sh, sc, w, c1, s1, ca, sa, c_col, w_mod, b_mod)


def _ctx_proj_kernel(x_ref, g_ref, sh_ref, sc_ref, w_ref, o_ref, kd_ref, vd_ref, wb_ref, h_ref):
    j = pl.program_id(0)

    @pl.when(j == 0)
    def _():
        xf = x_ref[...]
        y = xf * lax.rsqrt(jnp.mean(xf * xf, axis=-1, keepdims=True) + NORM_EPS)
        y = y * g_ref[...]
        h_ref[...] = (y * (1.0 + sc_ref[...]) + sh_ref[...]).astype(BF16)

    wb = w_ref[...].astype(BF16)
    wb_ref[...] = wb
    acc = jnp.dot(h_ref[...], wb, preferred_element_type=F32)
    is_ret_k = _PROJ_TILE_KINDS.index("ret_k")
    o_ref[...] = (acc * jnp.where(j == is_ret_k, K_SCALE, 1.0)).astype(BF16)

    @pl.when(j == _PROJ_TILE_KINDS.index("att_kv"))
    def _():
        for c in range(_PROJ_TILE // LANES):
            dup_ref = kd_ref if c < 2 else vd_ref
            d0, d1 = _dup_halves(acc[:, c * LANES:(c + 1) * LANES])
            t = 2 * (c % 2)
            dup_ref[:, t * LANES:(t + 1) * LANES] = d0.astype(BF16)
            dup_ref[:, (t + 1) * LANES:(t + 2) * LANES] = d1.astype(BF16)


def _ctx_proj(x, g, sh, sc, w):
    m, d = x.shape
    n = w.shape[1]
    tn = _PROJ_TILE
    assert n == tn * len(_PROJ_TILE_KINDS)
    fixed = lambda j: (0, 0)
    vec = pl.BlockSpec((1, d), fixed)
    return pl.pallas_call(
        _ctx_proj_kernel,
        grid=(n // tn,),
        in_specs=[pl.BlockSpec((m, d), fixed), vec, vec, vec,
                  pl.BlockSpec((d, tn), lambda j: (0, j))],
        out_specs=[pl.BlockSpec((m, tn), lambda j: (0, j)),
                   pl.BlockSpec((m, KV_DUP_COLS), fixed),
                   pl.BlockSpec((m, KV_DUP_COLS), fixed),
                   pl.BlockSpec((d, tn), lambda j: (0, j))],
        out_shape=[jax.ShapeDtypeStruct((m, n), BF16),
                   jax.ShapeDtypeStruct((m, KV_DUP_COLS), BF16),
                   jax.ShapeDtypeStruct((m, KV_DUP_COLS), BF16),
                   jax.ShapeDtypeStruct((d, n), BF16)],
        scratch_shapes=[pltpu.VMEM((m, d), BF16)],
        compiler_params=_params("arbitrary"),
        name="ctx_proj",
    )(x, g, sh, sc, w)


def _pair_lg(dec_ref, d, p, shape):
    lane = lax.broadcasted_iota(jnp.int32, shape, 1)
    first = (lane % LANES) < 64
    raw = jnp.where(first, jnp.full(shape, dec_ref[d, 2 * p], F32), jnp.full(shape, dec_ref[d, 2 * p + 1], F32))
    return -jnp.exp(raw)


def _head_block_mask(shape):
    r = lax.broadcasted_iota(jnp.int32, shape, 0)
    c = lax.broadcasted_iota(jnp.int32, shape, 1)
    return (r // 64) == (c // LANES)


def _kv_pair(k_pair, v_pair, w):
    kw = (k_pair.astype(F32) * w).astype(BF16)
    kv = lax.dot_general(kw, v_pair, (((0,), (0,)), ((), ())), preferred_element_type=F32)
    return jnp.where(_head_block_mask(kv.shape), kv, 0.0)


def _row_decay(dec_ref, d, p):
    shape = (LANES, 2 * RET_DV)
    rowh = lax.broadcasted_iota(jnp.int32, shape, 0) < 64
    raw = jnp.where(rowh, jnp.full(shape, dec_ref[d, 2 * p], F32), jnp.full(shape, dec_ref[d, 2 * p + 1], F32))
    return jnp.exp(-jnp.exp(raw) * float(RET_CHUNK))


def _compact_state(s):
    row = lax.broadcasted_iota(jnp.int32, (LANES, RET_DV), 0)
    return jnp.where(row < 64, s[:, :RET_DV], s[:, RET_DV:])


def _expand_state(c):
    row = lax.broadcasted_iota(jnp.int32, c.shape, 0)
    z = jnp.zeros_like(c)
    return jnp.concatenate([jnp.where(row < 64, c, z), jnp.where(row < 64, z, c)], axis=1)


def _ret_bwd_kernel(dec_ref, k_ref, v_ref, ck_ref, cv_ref, sb_ref, sbs):
    i = pl.program_id(0)
    C = RET_CHUNK
    lc = ck_ref.shape[0]

    @pl.when(i == 0)
    def _():
        pos = lax.broadcasted_iota(jnp.int32, (lc, LANES), 0).astype(F32)
        for p in range(RET_PAIRS):
            ks = slice(p * LANES, (p + 1) * LANES)
            vs = slice(p * 2 * RET_DV, (p + 1) * 2 * RET_DV)
            wb = jnp.exp(_pair_lg(dec_ref, 1, p, (lc, LANES)) * pos)
            sbs[p] = _kv_pair(ck_ref[:, ks], cv_ref[:, vs], wb)

    pos = lax.broadcasted_iota(jnp.int32, (C, LANES), 0).astype(F32)
    for p in range(RET_PAIRS):
        ks = slice(p * LANES, (p + 1) * LANES)
        vs = slice(p * 2 * RET_DV, (p + 1) * 2 * RET_DV)
        wb = jnp.exp(_pair_lg(dec_ref, 1, p, (C, LANES)) * pos)
        gb = _row_decay(dec_ref, 1, p)
        sb = sbs[p]
        for cc in reversed(range(RET_STEP_CHUNKS)):
            rs = slice(cc * C, (cc + 1) * C)
            sb_ref[cc, p] = _compact_state(sb).astype(BF16)
            sb = gb * sb + _kv_pair(k_ref[rs, ks], v_ref[rs, vs], wb)
        sbs[p] = sb


def _ret_bwd_states(dec, proj, cproj):
    L = proj.shape[0]
    lc = cproj.shape[0]
    S = RET_STEP_CHUNKS
    R = S * RET_CHUNK
    n = L // R
    return pl.pallas_call(
        _ret_bwd_kernel,
        grid=(n,),
        in_specs=[pl.BlockSpec(memory_space=pltpu.SMEM),
                  pl.BlockSpec((R, RET_QK_COLS), lambda i: (n - 1 - i, 1)),
                  pl.BlockSpec((R, RET_V_COLS), lambda i: (n - 1 - i, 1)),
                  pl.BlockSpec((lc, RET_QK_COLS), lambda i: (0, 1)),
                  pl.BlockSpec((lc, RET_V_COLS), lambda i: (0, 1))],
        out_specs=pl.BlockSpec((S, RET_PAIRS, LANES, RET_DV), lambda i: (n - 1 - i, 0, 0, 0)),
        out_shape=jax.ShapeDtypeStruct((n * S, RET_PAIRS, LANES, RET_DV), BF16),
        scratch_shapes=[pltpu.VMEM((RET_PAIRS, LANES, 2 * RET_DV), F32)],
        compiler_params=_params("arbitrary"),
        name="ret_bwd",
    )(dec, proj, proj, cproj, cproj)


def _ret_out_pieces(dec_ref, q_ref, k_ref, v_ref, g_ref, sb_ref, ck_ref, cv_ref, o_ref, sfs):
    i = pl.program_id(0)
    C = RET_CHUNK
    lc = ck_ref.shape[0]
    n_chunks = q_ref.shape[0] // C

    @pl.when(i == 0)
    def _():
        cpos = lax.broadcasted_iota(jnp.int32, (lc, LANES), 0).astype(F32)
        for p in range(RET_PAIRS):
            ks = slice(p * LANES, (p + 1) * LANES)
            vs = slice(p * 2 * RET_DV, (p + 1) * 2 * RET_DV)
            wf = jnp.exp(_pair_lg(dec_ref, 0, p, (lc, LANES)) * (lc - 1.0 - cpos))
            sfs[p] = _kv_pair(ck_ref[:, ks], cv_ref[:, vs], wf)

    pos = lax.broadcasted_iota(jnp.int32, (C, LANES), 0).astype(F32)
    n_i = lax.broadcasted_iota(jnp.int32, (C, 2 * C), 0)
    m_i = lax.broadcasted_iota(jnp.int32, (C, 2 * C), 1) % C
    rel = (n_i - m_i).astype(F32)
    lane = lax.broadcasted_iota(jnp.int32, (C, LANES), 1)
    lo = lane < 64

    def pair_tables(p):
        col_a = lax.broadcasted_iota(jnp.int32, (C, 2 * C), 1) < C
        raw_f = jnp.where(col_a, jnp.full((C, 2 * C), dec_ref[0, 2 * p], F32), jnp.full((C, 2 * C), dec_ref[0, 2 * p + 1], F32))
        raw_b = jnp.where(col_a, jnp.full((C, 2 * C), dec_ref[1, 2 * p], F32), jnp.full((C, 2 * C), dec_ref[1, 2 * p + 1], F32))
        dmat = jnp.where(rel >= 0, jnp.exp(-jnp.exp(raw_f) * jnp.maximum(rel, 0.0)),
                         jnp.exp(-jnp.exp(raw_b) * jnp.maximum(-rel, 0.0)))
        lg_f = _pair_lg(dec_ref, 0, p, (C, LANES))
        return dict(dmat=dmat, wqf=jnp.exp(lg_f * (pos + 1.0)),
                    wqb=jnp.exp(_pair_lg(dec_ref, 1, p, (C, LANES)) * (float(C) - pos)),
                    wkf=jnp.exp(lg_f * (C - 1.0 - pos)), gf=_row_decay(dec_ref, 0, p), sf=sfs[p])

    def chunk(p, cc, st):
        if cc == 0:
            st.update(pair_tables(p))
        ks = slice(p * LANES, (p + 1) * LANES)
        vs = slice(p * 2 * RET_DV, (p + 1) * 2 * RET_DV)
        rs = slice(cc * C, (cc + 1) * C)
        q = q_ref[rs, ks]
        k = k_ref[rs, ks]
        v = v_ref[rs, vs]
        zk = jnp.zeros_like(k)
        kst = jnp.concatenate([jnp.where(lo, k, zk), jnp.where(lo, zk, k)], axis=0)
        s = lax.dot_general(q, kst, (((1,), (1,)), ((), ())), preferred_element_type=F32)
        sd = (s * st["dmat"]).astype(BF16)
        qf32 = q.astype(F32)
        qwf = (qf32 * st["wqf"]).astype(BF16)
        qwb = (qf32 * st["wqb"]).astype(BF16)
        zv = jnp.zeros((C, RET_DV), BF16)
        vbd = jnp.concatenate([jnp.concatenate([v[:, :RET_DV], zv], axis=1),
                               jnp.concatenate([zv, v[:, RET_DV:]], axis=1)], axis=0)
        lhs = jnp.concatenate([sd, qwf, qwb], axis=1)
        st["mid"] = (lhs, vbd, k, v)

    def chunk_out(p, cc, st):
        lhs, vbd, k, v = st.pop("mid")
        rs = slice(cc * C, (cc + 1) * C)
        rhs = jnp.concatenate([vbd, st["sf"].astype(BF16), _expand_state(sb_ref[cc, p])], axis=0)
        o = jnp.dot(lhs, rhs, preferred_element_type=F32)
        st["sf"] = st["gf"] * st["sf"] + _kv_pair(k, v, st["wkf"])
        for t in range(2):
            oh = o[:, t * RET_DV:(t + 1) * RET_DV]
            oh = oh * lax.rsqrt(jnp.mean(oh * oh, axis=-1, keepdims=True) + NORM_EPS)
            cs = slice(p * 2 * RET_DV + t * RET_DV, p * 2 * RET_DV + (t + 1) * RET_DV)
            gt = g_ref[rs, cs].astype(F32)
            o_ref[rs, cs] = (oh * (gt / (1.0 + jnp.exp(-gt)))).astype(BF16)
        if cc == n_chunks - 1:
            sfs[p] = st["sf"]

    pieces = []
    for p in range(RET_PAIRS):
        st = {}
        pieces += [((lambda p=p, cc=cc, st=st: chunk(p, cc, st)), (lambda p=p, cc=cc, st=st: chunk_out(p, cc, st)))
                   for cc in range(n_chunks)]
    return pieces


def _attn_kernel(sink_ref, q_ref, kp_ref, kc_ref, kn_ref, vp_ref, vc_ref, vn_ref, ck_ref, cv_ref, o_ref, *,
                 fillers, pre=None, post=None):
    n = pl.program_id(0)
    nstep = pl.num_programs(0)
    B = ATT_BLOCK
    SB = ATT_STEP_BLOCKS
    kj = lax.broadcasted_iota(jnp.int32, (B, B), 0)
    qi = lax.broadcasted_iota(jnp.int32, (B, B), 1)
    ok_prev = jnp.where(n > 0, 0.0, MASK_NEG).astype(F32)
    ok_next = jnp.where(n < nstep - 1, 0.0, MASK_NEG).astype(F32)

    def band(inside, ok):
        return jnp.concatenate([jnp.where(inside, ok, MASK_NEG).astype(F32)] * ATT_GROUP, axis=1)

    bias_prev = [band(kj >= qi, ok_prev if j == 0 else 0.0) for j in range(SB)]
    bias_next = [band(kj <= qi, ok_next if j == SB - 1 else 0.0) for j in range(SB)]
    lane = lax.broadcasted_iota(jnp.int32, (B, LANES), 1)
    lo = lane < 64
    hi = lane >= 64

    def keys_of(j, gs, prev_ref, cur_ref, next_ref, ctx_ref):
        prev = prev_ref[:, gs] if j == 0 else cur_ref[(j - 1) * B:j * B, gs]
        nxt = next_ref[:, gs] if j == SB - 1 else cur_ref[(j + 1) * B:(j + 2) * B, gs]
        return jnp.concatenate([prev, cur_ref[j * B:(j + 1) * B, gs], nxt, ctx_ref[:, gs]], axis=0)

    def scores(j, g):
        gs = slice(g * LANES, (g + 1) * LANES)
        kcat = keys_of(j, gs, kp_ref, kc_ref, kn_ref, ck_ref)
        qs = []
        for r in range(ATT_GROUP):
            h = ATT_GROUP * g + r
            qt = q_ref[j * B:(j + 1) * B, (h // 2) * LANES:(h // 2 + 1) * LANES]
            keep = lo if h % 2 == 0 else hi
            qs.append(jnp.where(keep, qt, jnp.zeros_like(qt)))
        q4 = jnp.concatenate(qs, axis=0)
        return lax.dot_general(kcat, q4, (((1,), (1,)), ((), ())), preferred_element_type=F32)

    def softmax(j, g, s):
        sk = jnp.concatenate([jnp.full((1, B), sink_ref[ATT_GROUP * g + r], F32)
                              for r in range(ATT_GROUP)], axis=1) * LOG2E
        s = jnp.concatenate([s[:B] + bias_prev[j], s[B:2 * B], s[2 * B:3 * B] + bias_next[j], s[3 * B:]], axis=0)
        m = jnp.maximum(jnp.max(s, axis=0, keepdims=True), sk)
        e = jnp.exp2(s - m)
        den = jnp.sum(e, axis=0, keepdims=True) + jnp.exp2(sk - m)
        return e.astype(BF16), den

    def values(j, g, e, den):
        gs = slice(g * LANES, (g + 1) * LANES)
        vcat = keys_of(j, gs, vp_ref, vc_ref, vn_ref, cv_ref)
        res = lax.dot_general(vcat, e, (((0,), (0,)), ((), ())), preferred_element_type=F32) * (1.0 / den)
        for t in range(2):
            even = res[:, (2 * t) * B:(2 * t + 1) * B].T
            odd = res[:, (2 * t + 1) * B:(2 * t + 2) * B].T
            c0 = (2 * g + t) * LANES
            o_ref[j * B:(j + 1) * B, c0:c0 + LANES] = jnp.where(lo, even, odd).astype(BF16)

    units = [(j, g) for j in range(SB) for g in range(ATT_KV_HEADS)]
    per_unit = -(-len(fillers) // len(units))
    s_next = scores(*units[0])
    pending = None
    for u, unit in enumerate(units):
        s_cur = s_next
        if u + 1 < len(units):
            s_next = scores(*units[u + 1])
        if pre is not None:
            pre[u]()
        e_den = softmax(*unit, s_cur)
        for fill in fillers[u * per_unit:(u + 1) * per_unit]:
            fill()
        if pending is not None:
            values(*units[u - 1], *pending)
        if post is not None:
            post[u]()
        pending = e_den
    values(*units[-1], *pending)


def _mid_kernel(*refs, fillers):
    att_in, ret_in = refs[:10], refs[10:18]
    o_att, o_ret, sfs = refs[18:]
    ret = _ret_out_pieces(*ret_in, o_ret, sfs)
    n_units = ATT_STEP_BLOCKS * ATT_KV_HEADS
    assert len(ret) == n_units
    _attn_kernel(*att_in, o_att, fillers=fillers, post=[(lambda a=a, b=b: (a(), b())) for a, b in ret])


def _mid(sink, dec, proj, kd, vd, ckd, cvd, sb, cproj, riders):
    L = proj.shape[0]
    B = ATT_BLOCK
    SB = ATT_STEP_BLOCKS
    n = L // (SB * B)
    nb = L // B
    lc = ckd.shape[0]
    prev = pl.BlockSpec((B, KV_DUP_COLS), lambda i: (jnp.maximum(i * SB - 1, 0), 0))
    cur = pl.BlockSpec((SB * B, KV_DUP_COLS), lambda i: (i, 0))
    nxt = pl.BlockSpec((B, KV_DUP_COLS), lambda i: (jnp.minimum((i + 1) * SB, nb - 1), 0))
    full = pl.BlockSpec((lc, KV_DUP_COLS), lambda i: (0, 0))
    rid_in_specs, rid_out_specs, rid_shapes = _rider_specs(riders, n)
    R = SB * B
    S = R // RET_CHUNK
    lcr = cproj.shape[0]
    return pl.pallas_call(
        _with_cast_riders(_mid_kernel, 18, 2, len(riders)),
        grid=(n,),
        in_specs=[pl.BlockSpec(memory_space=pltpu.SMEM),
                  pl.BlockSpec((R, ATT_Q_COLS), lambda i: (i, 3)),
                  prev, cur, nxt, prev, cur, nxt, full, full,
                  pl.BlockSpec(memory_space=pltpu.SMEM),
                  pl.BlockSpec((R, RET_QK_COLS), lambda i: (i, 0)),
                  pl.BlockSpec((R, RET_QK_COLS), lambda i: (i, 1)),
                  pl.BlockSpec((R, RET_V_COLS), lambda i: (i, 1)),
                  pl.BlockSpec((R, RET_V_COLS), lambda i: (i, 2)),
                  pl.BlockSpec((S, RET_PAIRS, LANES, RET_DV), lambda i: (i, 0, 0, 0)),
                  pl.BlockSpec((lcr, RET_QK_COLS), lambda i: (0, 1)),
                  pl.BlockSpec((lcr, RET_V_COLS), lambda i: (0, 1))] + rid_in_specs,
        out_specs=[pl.BlockSpec((R, ATT_HEADS * ATT_DH), lambda i: (i, 0)),
                   pl.BlockSpec((R, RET_HEADS * RET_DV), lambda i: (i, 0))] + rid_out_specs,
        out_shape=[jax.ShapeDtypeStruct((L, ATT_HEADS * ATT_DH), BF16),
                   jax.ShapeDtypeStruct((L, RET_HEADS * RET_DV), BF16)] + rid_shapes,
        scratch_shapes=[pltpu.VMEM((RET_PAIRS, LANES, 2 * RET_DV), F32)],
        compiler_params=_params("arbitrary"),
        name="attn",
    )(sink, proj, kd, kd, kd, vd, vd, vd, ckd, cvd, dec, proj, proj, proj, proj, sb, cproj, cproj,
      *[r[0] for r in riders])


def _out_proj_kernel(yr_ref, ya_ref, w_ref, x_ref, gt_ref, g_ref, sh_ref, sc_ref, o_ref, h_ref):
    kr = yr_ref.shape[1]
    for r in range(yr_ref.shape[0] // OUT_ROW_CHUNK):
        rs = slice(r * OUT_ROW_CHUNK, (r + 1) * OUT_ROW_CHUNK)
        acc = jnp.dot(yr_ref[rs, :], w_ref[:kr, :], preferred_element_type=F32)
        acc = acc + jnp.dot(ya_ref[rs, :], w_ref[kr:, :], preferred_element_type=F32)
        x1 = x_ref[rs, :] + gt_ref[...] * acc
        o_ref[rs, :] = x1
        y = x1 * lax.rsqrt(jnp.mean(x1 * x1, axis=-1, keepdims=True) + NORM_EPS)
        y = y * g_ref[...]
        h_ref[rs, :] = (y * (1.0 + sc_ref[...]) + sh_ref[...]).astype(BF16)


def _out_proj(yr, ya, w, x, gt, g, sh, sc, *, tm):
    m, d = x.shape
    kr, ka = yr.shape[1], ya.shape[1]
    row = lambda i: (i, 0)
    vec = pl.BlockSpec((1, d), lambda i: (0, 0))
    return pl.pallas_call(
        _out_proj_kernel,
        grid=(m // tm,),
        in_specs=[pl.BlockSpec((tm, kr), row), pl.BlockSpec((tm, ka), row),
                  pl.BlockSpec((kr + ka, d), lambda i: (0, 0)),
                  pl.BlockSpec((tm, d), row), vec, vec, vec, vec],
        out_specs=[pl.BlockSpec((tm, d), row), pl.BlockSpec((tm, d), row)],
        out_shape=[jax.ShapeDtypeStruct((m, d), F32), jax.ShapeDtypeStruct((m, d), BF16)],
        compiler_params=_params("parallel"),
        name="out_proj",
    )(yr, ya, w, x, gt, g, sh, sc)


def _ffn_kernel(h_ref, gt_ref, gfin_ref, wg_ref, wu_ref, wd_ref, x_hbm, o_ref, x_buf, sem):
    i = pl.program_id(0)
    f = pl.program_id(1)
    last = pl.num_programs(1) - 1
    rows = o_ref.shape[0]
    x_copy = pltpu.make_async_copy(x_hbm.at[pl.ds(pl.multiple_of(i * rows, rows), rows), :], x_buf, sem.at[0])

    def step(first, final):
        wd = wd_ref[...].astype(BF16)
        for r in range(rows // FFN_ROW_CHUNK):
            rs = slice(r * FFN_ROW_CHUNK, (r + 1) * FFN_ROW_CHUNK)
            h = h_ref[rs, :]
            a = jnp.dot(h, wg_ref[0], preferred_element_type=F32)
            u = jnp.dot(h, wu_ref[0], preferred_element_type=F32)
            act = ((a / (1.0 + jnp.exp(-a))) * u).astype(BF16)
            part = jnp.dot(act, wd, preferred_element_type=F32)
            if first:
                o_ref[rs, :] = part
            elif not final:
                o_ref[rs, :] += part
            else:
                y = x_buf[rs, :] + gt_ref[...] * (o_ref[rs, :] + part)
                y = y * lax.rsqrt(jnp.mean(y * y, axis=-1, keepdims=True) + NORM_EPS)
                o_ref[rs, :] = y * gfin_ref[...]

    @pl.when(f == 0)
    def _():
        x_copy.start()
        step(first=True, final=False)

    @pl.when((f > 0) & (f < last))
    def _():
        step(first=False, final=False)

    @pl.when(f == last)
    def _():
        x_copy.wait()
        step(first=False, final=True)


def _ffn(h, x, gt, gfin, wg, wu, wd, *, tm):
    m, d = x.shape
    nf = wg.shape[0]
    assert wg.shape == wu.shape == (nf, d, FFN_TILE) and wd.shape == (nf * FFN_TILE, d)
    assert m % tm == 0 and tm % FFN_ROW_CHUNK == 0
    row = lambda i, f: (i, 0)
    vec = pl.BlockSpec((1, d), lambda i, f: (0, 0))
    wcol = pl.BlockSpec((1, d, FFN_TILE), lambda i, f: (f, 0, 0))
    return pl.pallas_call(
        _ffn_kernel,
        grid=(m // tm, nf),
        in_specs=[pl.BlockSpec((tm, d), row), vec, vec, wcol, wcol,
                  pl.BlockSpec((FFN_TILE, d), lambda i, f: (f, 0)),
                  pl.BlockSpec(memory_space=pl.ANY)],
        out_specs=pl.BlockSpec((tm, d), row),
        out_shape=jax.ShapeDtypeStruct((m, d), F32),
        scratch_shapes=[pltpu.VMEM((tm, d), F32), pltpu.SemaphoreType.DMA((1,))],
        compiler_params=_params("arbitrary", "arbitrary"),
        name="ffn",
    )(h, gt, gfin, wg, wu, wd, x)


def _rope_tables(L):
    f32 = np.float32
    lane = np.arange(LANES)
    inv1 = f32(ROPE_BASE) ** (-np.arange(32, dtype=f32) / f32(32))
    ang1 = np.arange(L, dtype=f32)[:, None] * inv1[None, :]
    sgn1 = np.where((lane % 64) < 32, -1.0, 1.0).astype(f32)
    cos1 = np.tile(np.cos(ang1), (1, LANES // 32))
    sin1 = np.tile(np.sin(ang1), (1, LANES // 32)) * sgn1[None, :]
    inv2 = f32(ROPE_BASE) ** (-np.arange(16, dtype=f32) / f32(16))
    nrow = L // GRID_W
    ang_r = np.arange(nrow, dtype=f32)[:, None] * inv2[None, :]
    ang_c = np.arange(GRID_W, dtype=f32)[:, None] * inv2[None, :]
    sgna = np.where((lane % 32) < 16, -1.0, 1.0).astype(f32)

    def expand(fr, fc):
        by_row = np.broadcast_to(np.tile(fr, (1, 2))[:, None, :], (nrow, GRID_W, 32))
        by_col = np.broadcast_to(np.tile(fc, (1, 2))[None, :, :], (nrow, GRID_W, 32))
        head = np.concatenate([by_row, by_col], axis=-1).reshape(L, 64)
        return np.tile(head, (1, LANES // 64))

    cosa = expand(np.cos(ang_r), np.cos(ang_c))
    sina = expand(np.sin(ang_r), np.sin(ang_c)) * sgna[None, :]
    return tuple(np.ascontiguousarray(t, dtype=f32) for t in (cos1, sin1, cosa, sina))


def kernel(x, c, ctx, c_ctx, w_mod, b_mod, norm_mix, norm_ffn, w_in, ret_decay, attn_sink,
           w_out, w_gate, w_up, w_down, norm_final):
    B, L, D = x.shape
    assert B == 1 and w_mod.shape[0] == 1, "single batch element, depth-1 layer"
    x2 = x[0]
    xc2 = ctx[0]

    cv = jnp.zeros((8, D), F32).at[0].set(c[0]).at[1].set(c_ctx)
    mod = _mod(cv, w_mod[0], b_mod[0][None, :], 2 * D)
    sh_m, sc_m = mod[0:1, 0:D], mod[0:1, D:2 * D]
    sh_mc, sc_mc = mod[1:2, 0:D], mod[1:2, D:2 * D]

    g_mix = norm_mix[0][None, :]
    cproj, ckd, cvd, w_in_b = _ctx_proj(xc2, g_mix, sh_mc, sc_mc, w_in[0])
    proj, kd, vd, mod_rest = _in_proj(x2, g_mix, sh_m, sc_m, w_in_b, _rope_tables(L),
                                      c[0][:, None], w_mod[0], b_mod[0][None, :], 2 * D, tm=ROW_TILE)
    gt_m, sh_f, sc_f, gt_f = [mod_rest.reshape(1, 4 * D)[:, k * D:(k + 1) * D] for k in range(4)]

    dec = ret_decay[0].astype(F32)
    sb = _ret_bwd_states(dec, proj, cproj)
    y_att, y_ret, w_gate_b, w_up_b, w_out_b = _mid(
        attn_sink[0].astype(F32), dec, proj, kd, vd, ckd, cvd, sb, cproj,
        [(w_gate[0], 1, FFN_TILE), (w_up[0], 1, FFN_TILE), (w_out[0], 1, None)])

    x1, hff = _out_proj(y_ret, y_att, w_out_b, x2, gt_m, norm_ffn[0][None, :], sh_f, sc_f, tm=ROW_TILE)
    out = _ffn(hff, x1, gt_f, norm_final[None, :], w_gate_b, w_up_b, w_down[0], tm=FFN_ROW_TILE)
    return out[None]
```
